```python
import math
import jax, jax.numpy as jnp
from jax import lax
import numpy as np

D_MODEL = 2048
BATCH = 8
SEQ = 4096
DEPTH = 4

CTX_LEN = 256
GRID_W = 64
MIX_WIDTH = D_MODEL
POOL_WIDTH = 3 * MIX_WIDTH // 4
SSM_WIDTH = MIX_WIDTH - POOL_WIDTH
POOL_WINDOWS = (2, 4, 8, 16)
N_POOL_GROUPS = len(POOL_WINDOWS)
POOL_GROUP = POOL_WIDTH // N_POOL_GROUPS
SSM_GROUP = 16
N_SSM_GROUPS = SSM_WIDTH // SSM_GROUP
SSM_STATE = 64
D_FF = 5632
CONV_K = 3
DT_MIN, DT_MAX = 1e-3, 1e-1
EPS = 1e-6

kernel_name = "hybrid_pool_s5_prefix_dit_block"


def rms_norm(x, gain):
    xf = x.astype(jnp.float32)
    y = xf * lax.rsqrt(jnp.mean(xf * xf, axis=-1, keepdims=True) + EPS)
    return (y * gain.astype(jnp.float32)).astype(x.dtype)


def modulate(h, shift, scale):
    return h * (1 + scale[:, None, :]) + shift[:, None, :]


def multiscale_pool(u, w_pool, pool_scale):
    bsz, n, _ = u.shape
    uf = u.astype(jnp.float32)
    cs = jnp.concatenate([jnp.zeros((bsz, 1, POOL_WIDTH), jnp.float32), jnp.cumsum(uf, axis=1)], axis=1)
    t = jnp.arange(n)
    parts = []
    for g, w in enumerate(POOL_WINDOWS):
        lo = jnp.maximum(t - w // 2, 0)
        hi = jnp.minimum(t + w // 2, n)
        sl = slice(g * POOL_GROUP, (g + 1) * POOL_GROUP)
        csg = cs[..., sl]
        cnt = (hi - lo).astype(jnp.float32)[None, :, None]
        mean = (jnp.take(csg, hi, axis=1) - jnp.take(csg, lo, axis=1)) / cnt
        parts.append(mean - uf[..., sl])
    p = jnp.stack(parts, axis=2)
    y = jnp.einsum('blgc,gcd->blgd', p, w_pool.astype(jnp.float32))
    return (y.reshape(bsz, n, POOL_WIDTH) * pool_scale.astype(jnp.float32)).astype(u.dtype)


def s5_discretise(a_re, a_im, log_dt, b_re, b_im):
    a_re = a_re.astype(jnp.float32)
    a_im = a_im.astype(jnp.float32)
    dt = jnp.exp(log_dt.astype(jnp.float32))[:, None]
    mag = jnp.exp(a_re * dt)
    lam_re = mag * jnp.cos(a_im * dt)
    lam_im = mag * jnp.sin(a_im * dt)
    denom = a_re * a_re + a_im * a_im
    nr, ni = lam_re - 1.0, lam_im
    f_re = (nr * a_re + ni * a_im) / denom
    f_im = (ni * a_re - nr * a_im) / denom
    b_re = b_re.astype(jnp.float32)
    b_im = b_im.astype(jnp.float32)
    bb_re = f_re[..., None] * b_re - f_im[..., None] * b_im
    bb_im = f_re[..., None] * b_im + f_im[..., None] * b_re
    return lam_re, lam_im, bb_re, bb_im


def _complex_linear_recurrence_op(left, right):
    a1r, a1i, b1r, b1i = left
    a2r, a2i, b2r, b2i = right
    return (a2r * a1r - a2i * a1i,
            a2r * a1i + a2i * a1r,
            a2r * b1r - a2i * b1i + b2r,
            a2r * b1i + a2i * b1r + b2i)


def s5_scan(u_g, lam_re, lam_im, bb_re, bb_im, h0, reverse):
    b_re = jnp.einsum('blgh,gph->blgp', u_g, bb_re)
    b_im = jnp.einsum('blgh,gph->blgp', u_g, bb_im)
    if h0 is not None:
        pos = -1 if reverse else 0
        h0_re, h0_im = h0
        b_re = b_re.at[:, pos].add(lam_re * h0_re - lam_im * h0_im)
        b_im = b_im.at[:, pos].add(lam_re * h0_im + lam_im * h0_re)
    a_re = jnp.broadcast_to(lam_re, b_re.shape)
    a_im = jnp.broadcast_to(lam_im, b_im.shape)
    _, _, h_re, h_im = lax.associative_scan(
        _complex_linear_recurrence_op, (a_re, a_im, b_re, b_im), axis=1, reverse=reverse)
    return h_re, h_im


def s5_readout(h, c_re, c_im):
    h_re, h_im = h
    return (jnp.einsum('blgp,ghp->blgh', h_re, c_re.astype(jnp.float32))
            - jnp.einsum('blgp,ghp->blgh', h_im, c_im.astype(jnp.float32)))


def to_ssm_groups(u_ssm):
    bsz, n, _ = u_ssm.shape
    return u_ssm.astype(jnp.float32).reshape(bsz, n, N_SSM_GROUPS, SSM_GROUP)


def s5_head_output(u_ssm, y, ssm_d, w_glu):
    bsz, n, _ = u_ssm.shape
    yf = y.reshape(bsz, n, SSM_WIDTH) + ssm_d.astype(jnp.float32) * u_ssm.astype(jnp.float32)
    yf = jax.nn.gelu(yf)
    return (yf * jax.nn.sigmoid(yf @ w_glu.astype(jnp.float32))).astype(u_ssm.dtype)


def mix_project(u, ssm_y, w_pool, pool_scale, ssm_d, w_glu, w_out):
    pool_out = multiscale_pool(u[..., :POOL_WIDTH], w_pool, pool_scale)
    ssm_out = s5_head_output(u[..., POOL_WIDTH:], ssm_y, ssm_d, w_glu)
    return jnp.concatenate([pool_out, ssm_out], axis=-1) @ w_out


def conv_glu_ffn(h, w_up, w_conv, w_down, rows):
    bsz, n, _ = h.shape
    z = h @ w_up
    if rows is None:
        grid = z[:, None]
        k = w_conv[1:2]
    else:
        grid = z.reshape(bsz, rows, GRID_W, 2 * D_FF)
        k = w_conv
    grid = lax.conv_general_dilated(grid, k[:, :, None, :], (1, 1), 'SAME',
                                    dimension_numbers=('NHWC', 'HWIO', 'NHWC'),
                                    feature_group_count=2 * D_FF)
    val, gate = jnp.split(grid.reshape(bsz, n, 2 * D_FF), 2, axis=-1)
    return (val * jax.nn.silu(gate)) @ w_down


def _fwd_setup_inputs(seed: int = 0) -> dict:
    key = jax.random.key(seed)
    ks = jax.random.split(key, 26)
    f32 = jnp.float32
    nrm = lambda k, shape, s: jax.random.normal(k, shape, f32) * s
    G, P, H = N_SSM_GROUPS, SSM_STATE, SSM_GROUP
    a_im_base = jnp.pi * jnp.arange(P, dtype=f32)
    return {
        "x": nrm(ks[0], (BATCH, SEQ, D_MODEL), 1.0),
        "c": nrm(ks[1], (BATCH, D_MODEL), 1.0),
        "ctx": nrm(ks[2], (BATCH, CTX_LEN, D_MODEL), 1.0),
        "c_ctx": nrm(ks[3], (D_MODEL,), 1.0),
        "w_ada": nrm(ks[4], (DEPTH, D_MODEL, 6 * D_MODEL), 0.5 * D_MODEL ** -0.5),
        "b_ada": nrm(ks[5], (DEPTH, 6 * D_MODEL), 0.02),
        "w_in": nrm(ks[6], (DEPTH, D_MODEL, MIX_WIDTH), D_MODEL ** -0.5),
        "w_pool": nrm(ks[7], (DEPTH, N_POOL_GROUPS, POOL_GROUP, POOL_GROUP), POOL_GROUP ** -0.5),
        "pool_scale": 1.0 + nrm(ks[8], (DEPTH, POOL_WIDTH), 0.02),
        "ssm_a_re": -0.5 + nrm(ks[9], (DEPTH, 2, G, P), 0.01),
        "ssm_a_im": a_im_base + nrm(ks[10], (DEPTH, 2, G, P), 0.01),
        "ssm_log_dt": jax.random.uniform(ks[11], (DEPTH, 2, G), f32, math.log(DT_MIN), math.log(DT_MAX)),
        "ssm_b_re": nrm(ks[12], (DEPTH, 2, G, P, H), (2 * H) ** -0.5),
        "ssm_b_im": nrm(ks[13], (DEPTH, 2, G, P, H), (2 * H) ** -0.5),
        "ssm_c_re": nrm(ks[14], (DEPTH, 2, G, H, P), P ** -0.5),
        "ssm_c_im": nrm(ks[15], (DEPTH, 2, G, H, P), P ** -0.5),
        "ssm_d": nrm(ks[16], (DEPTH, SSM_WIDTH), 1.0),
        "w_glu": nrm(ks[17], (DEPTH, SSM_WIDTH, SSM_WIDTH), SSM_WIDTH ** -0.5),
        "w_out": nrm(ks[18], (DEPTH, MIX_WIDTH, D_MODEL), MIX_WIDTH ** -0.5),
        "g_pre_mix": 1.0 + nrm(ks[19], (DEPTH, D_MODEL), 0.02),
        "g_post_mix": 1.0 + nrm(ks[20], (DEPTH, D_MODEL), 0.02),
        "g_pre_ffn": 1.0 + nrm(ks[21], (DEPTH, D_MODEL), 0.02),
        "g_post_ffn": 1.0 + nrm(ks[22], (DEPTH, D_MODEL), 0.02),
        "w_up": nrm(ks[23], (DEPTH, D_MODEL, 2 * D_FF), D_MODEL ** -0.5),
        "w_conv": nrm(ks[24], (DEPTH, CONV_K, CONV_K, 2 * D_FF), 1.0 / CONV_K),
        "w_down": nrm(ks[25], (DEPTH, D_FF, D_MODEL), D_FF ** -0.5),
    }


def _fwd_reference(x, c, ctx, c_ctx, w_ada, b_ada, w_in, w_pool, pool_scale, ssm_a_re, ssm_a_im,
              ssm_log_dt, ssm_b_re, ssm_b_im, ssm_c_re, ssm_c_im, ssm_d, w_glu, w_out,
              g_pre_mix, g_post_mix, g_pre_ffn, g_post_ffn, w_up, w_conv, w_down):
    n_lat = x.shape[1]
    rows = n_lat // GRID_W
    s_c = jax.nn.silu(c)
    s_ctx = jax.nn.silu(c_ctx)[None, :]
    for l in range(DEPTH):
        last = l == DEPTH - 1
        mx = jnp.split(s_c @ w_ada[l] + b_ada[l], 6, axis=-1)
        mc = jnp.split(s_ctx @ w_ada[l] + b_ada[l], 6, axis=-1)
        disc = [s5_discretise(ssm_a_re[l, d], ssm_a_im[l, d], ssm_log_dt[l, d],
                              ssm_b_re[l, d], ssm_b_im[l, d]) for d in range(2)]

        h_ctx = modulate(rms_norm(ctx, g_pre_mix[l]), mc[0], mc[1])
        h_lat = modulate(rms_norm(x, g_pre_mix[l]), mx[0], mx[1])
        u_lat = h_lat @ w_in[l]
        if last:
            u_ctx_ssm = h_ctx @ w_in[l][:, POOL_WIDTH:]
        else:
            u_ctx = h_ctx @ w_in[l]
            u_ctx_ssm = u_ctx[..., POOL_WIDTH:]
        ctx_g = to_ssm_groups(u_ctx_ssm)
        lat_g = to_ssm_groups(u_lat[..., POOL_WIDTH:])
        lat_dirs, ctx_dirs = [], []
        for d in range(2):
            rev = d == 1
            lam_re, lam_im, bb_re, bb_im = disc[d]
            hc = s5_scan(ctx_g, lam_re, lam_im, bb_re, bb_im, None, rev)
            fin = 0 if rev else -1
            h0 = (hc[0][:, fin], hc[1][:, fin])
            hl = s5_scan(lat_g, lam_re, lam_im, bb_re, bb_im, h0, rev)
            lat_dirs.append(s5_readout(hl, ssm_c_re[l, d], ssm_c_im[l, d]))
            if not last:
                ctx_dirs.append(s5_readout(hc, ssm_c_re[l, d], ssm_c_im[l, d]))
        mix_lat = mix_project(u_lat, lat_dirs[0] + lat_dirs[1], w_pool[l], pool_scale[l],
                              ssm_d[l], w_glu[l], w_out[l])
        x = x + mx[2][:, None, :] * rms_norm(mix_lat, g_post_mix[l])

        f_lat = conv_glu_ffn(modulate(rms_norm(x, g_pre_ffn[l]), mx[3], mx[4]),
                             w_up[l], w_conv[l], w_down[l], rows)
        x = x + mx[5][:, None, :] * rms_norm(f_lat, g_post_ffn[l])

        if not last:
            mix_ctx = mix_project(u_ctx, ctx_dirs[0] + ctx_dirs[1], w_pool[l], pool_scale[l],
                                  ssm_d[l], w_glu[l], w_out[l])
            ctx = ctx + mc[2][:, None, :] * rms_norm(mix_ctx, g_post_mix[l])
            f_ctx = conv_glu_ffn(modulate(rms_norm(ctx, g_pre_ffn[l]), mc[3], mc[4]),
                                 w_up[l], w_conv[l], w_down[l], None)
            ctx = ctx + mc[5][:, None, :] * rms_norm(f_ctx, g_post_ffn[l])
    return x


import jax as _jax
import jax.numpy as _jnp

TWIN_FORMAT = 'train_step'
FWD_PARAMS = ['x', 'c', 'ctx', 'c_ctx', 'w_ada', 'b_ada', 'w_in', 'w_pool', 'pool_scale', 'ssm_a_re', 'ssm_a_im', 'ssm_log_dt', 'ssm_b_re', 'ssm_b_im', 'ssm_c_re', 'ssm_c_im', 'ssm_d', 'w_glu', 'w_out', 'g_pre_mix', 'g_post_mix', 'g_pre_ffn', 'g_post_ffn', 'w_up', 'w_conv', 'w_down']
TWIN_WEIGHTS = ['c_ctx', 'w_ada', 'b_ada', 'w_in', 'w_pool', 'pool_scale', 'ssm_a_re', 'ssm_a_im', 'ssm_log_dt', 'ssm_b_re', 'ssm_b_im', 'ssm_c_re', 'ssm_c_im', 'ssm_d', 'w_glu', 'w_out', 'g_pre_mix', 'g_post_mix', 'g_pre_ffn', 'g_post_ffn', 'w_up', 'w_conv', 'w_down']
TWIN_DIFF_INPUT = 'x'
TWIN_INPUTS = ['x', 'c', 'ctx', 'c_ctx', 'w_ada', 'b_ada', 'w_in', 'w_pool', 'pool_scale', 'ssm_a_re', 'ssm_a_im', 'ssm_log_dt', 'ssm_b_re', 'ssm_b_im', 'ssm_c_re', 'ssm_c_im', 'ssm_d', 'w_glu', 'w_out', 'g_pre_mix', 'g_post_mix', 'g_pre_ffn', 'g_post_ffn', 'w_up', 'w_conv', 'w_down', 'loss_target', 'm_c_ctx', 'm_w_ada', 'm_b_ada', 'm_w_in', 'm_w_pool', 'm_pool_scale', 'm_ssm_a_re', 'm_ssm_a_im', 'm_ssm_log_dt', 'm_ssm_b_re', 'm_ssm_b_im', 'm_ssm_c_re', 'm_ssm_c_im', 'm_ssm_d', 'm_w_glu', 'm_w_out', 'm_g_pre_mix', 'm_g_post_mix', 'm_g_pre_ffn', 'm_g_post_ffn', 'm_w_up', 'm_w_conv', 'm_w_down', 'v_c_ctx', 'v_w_ada', 'v_b_ada', 'v_w_in', 'v_w_pool', 'v_pool_scale', 'v_ssm_a_re', 'v_ssm_a_im', 'v_ssm_log_dt', 'v_ssm_b_re', 'v_ssm_b_im', 'v_ssm_c_re', 'v_ssm_c_im', 'v_ssm_d', 'v_w_glu', 'v_w_out', 'v_g_pre_mix', 'v_g_post_mix', 'v_g_pre_ffn', 'v_g_post_ffn', 'v_w_up', 'v_w_conv', 'v_w_down']
TWIN_OUTPUTS = ['loss', 'grad_x', 'grad_c_ctx', 'grad_w_ada', 'grad_b_ada', 'grad_w_in', 'grad_w_pool', 'grad_pool_scale', 'grad_ssm_a_re', 'grad_ssm_a_im', 'grad_ssm_log_dt', 'grad_ssm_b_re', 'grad_ssm_b_im', 'grad_ssm_c_re', 'grad_ssm_c_im', 'grad_ssm_d', 'grad_w_glu', 'grad_w_out', 'grad_g_pre_mix', 'grad_g_post_mix', 'grad_g_pre_ffn', 'grad_g_post_ffn', 'grad_w_up', 'grad_w_conv', 'grad_w_down', 'delta_c_ctx', 'delta_w_ada', 'delta_b_ada', 'delta_w_in', 'delta_w_pool', 'delta_pool_scale', 'delta_ssm_a_re', 'delta_ssm_a_im', 'delta_ssm_log_dt', 'delta_ssm_b_re', 'delta_ssm_b_im', 'delta_ssm_c_re', 'delta_ssm_c_im', 'delta_ssm_d', 'delta_w_glu', 'delta_w_out', 'delta_g_pre_mix', 'delta_g_post_mix', 'delta_g_pre_ffn', 'delta_g_post_ffn', 'delta_w_up', 'delta_w_conv', 'delta_w_down', 'new_m_c_ctx', 'new_m_w_ada', 'new_m_b_ada', 'new_m_w_in', 'new_m_w_pool', 'new_m_pool_scale', 'new_m_ssm_a_re', 'new_m_ssm_a_im', 'new_m_ssm_log_dt', 'new_m_ssm_b_re', 'new_m_ssm_b_im', 'new_m_ssm_c_re', 'new_m_ssm_c_im', 'new_m_ssm_d', 'new_m_w_glu', 'new_m_w_out', 'new_m_g_pre_mix', 'new_m_g_post_mix', 'new_m_g_pre_ffn', 'new_m_g_post_ffn', 'new_m_w_up', 'new_m_w_conv', 'new_m_w_down', 'new_v_c_ctx', 'new_v_w_ada', 'new_v_b_ada', 'new_v_w_in', 'new_v_w_pool', 'new_v_pool_scale', 'new_v_ssm_a_re', 'new_v_ssm_a_im', 'new_v_ssm_log_dt', 'new_v_ssm_b_re', 'new_v_ssm_b_im', 'new_v_ssm_c_re', 'new_v_ssm_c_im', 'new_v_ssm_d', 'new_v_w_glu', 'new_v_w_out', 'new_v_g_pre_mix', 'new_v_g_post_mix', 'new_v_g_pre_ffn', 'new_v_g_post_ffn', 'new_v_w_up', 'new_v_w_conv', 'new_v_w_down']
TWIN_LEAF_KINDS = {'loss': 'loss', 'grad_x': 'grad_x', 'grad_c_ctx': 'grad_w', 'grad_w_ada': 'grad_w', 'grad_b_ada': 'grad_w', 'grad_w_in': 'grad_w', 'grad_w_pool': 'grad_w', 'grad_pool_scale': 'grad_w', 'grad_ssm_a_re': 'grad_w', 'grad_ssm_a_im': 'grad_w', 'grad_ssm_log_dt': 'grad_w', 'grad_ssm_b_re': 'grad_w', 'grad_ssm_b_im': 'grad_w', 'grad_ssm_c_re': 'grad_w', 'grad_ssm_c_im': 'grad_w', 'grad_ssm_d': 'grad_w', 'grad_w_glu': 'grad_w', 'grad_w_out': 'grad_w', 'grad_g_pre_mix': 'grad_w', 'grad_g_post_mix': 'grad_w', 'grad_g_pre_ffn': 'grad_w', 'grad_g_post_ffn': 'grad_w', 'grad_w_up': 'grad_w', 'grad_w_conv': 'grad_w', 'grad_w_down': 'grad_w', 'delta_c_ctx': 'delta_w', 'delta_w_ada': 'delta_w', 'delta_b_ada': 'delta_w', 'delta_w_in': 'delta_w', 'delta_w_pool': 'delta_w', 'delta_pool_scale': 'delta_w', 'delta_ssm_a_re': 'delta_w', 'delta_ssm_a_im': 'delta_w', 'delta_ssm_log_dt': 'delta_w', 'delta_ssm_b_re': 'delta_w', 'delta_ssm_b_im': 'delta_w', 'delta_ssm_c_re': 'delta_w', 'delta_ssm_c_im': 'delta_w', 'delta_ssm_d': 'delta_w', 'delta_w_glu': 'delta_w', 'delta_w_out': 'delta_w', 'delta_g_pre_mix': 'delta_w', 'delta_g_post_mix': 'delta_w', 'delta_g_pre_ffn': 'delta_w', 'delta_g_post_ffn': 'delta_w', 'delta_w_up': 'delta_w', 'delta_w_conv': 'delta_w', 'delta_w_down': 'delta_w', 'new_m_c_ctx': 'new_m', 'new_m_w_ada': 'new_m', 'new_m_b_ada': 'new_m', 'new_m_w_in': 'new_m', 'new_m_w_pool': 'new_m', 'new_m_pool_scale': 'new_m', 'new_m_ssm_a_re': 'new_m', 'new_m_ssm_a_im': 'new_m', 'new_m_ssm_log_dt': 'new_m', 'new_m_ssm_b_re': 'new_m', 'new_m_ssm_b_im': 'new_m', 'new_m_ssm_c_re': 'new_m', 'new_m_ssm_c_im': 'new_m', 'new_m_ssm_d': 'new_m', 'new_m_w_glu': 'new_m', 'new_m_w_out': 'new_m', 'new_m_g_pre_mix': 'new_m', 'new_m_g_post_mix': 'new_m', 'new_m_g_pre_ffn': 'new_m', 'new_m_g_post_ffn': 'new_m', 'new_m_w_up': 'new_m', 'new_m_w_conv': 'new_m', 'new_m_w_down': 'new_m', 'new_v_c_ctx': 'new_v', 'new_v_w_ada': 'new_v', 'new_v_b_ada': 'new_v', 'new_v_w_in': 'new_v', 'new_v_w_pool': 'new_v', 'new_v_pool_scale': 'new_v', 'new_v_ssm_a_re': 'new_v', 'new_v_ssm_a_im': 'new_v', 'new_v_ssm_log_dt': 'new_v', 'new_v_ssm_b_re': 'new_v', 'new_v_ssm_b_im': 'new_v', 'new_v_ssm_c_re': 'new_v', 'new_v_ssm_c_im': 'new_v', 'new_v_ssm_d': 'new_v', 'new_v_w_glu': 'new_v', 'new_v_w_out': 'new_v', 'new_v_g_pre_mix': 'new_v', 'new_v_g_post_mix': 'new_v', 'new_v_g_pre_ffn': 'new_v', 'new_v_g_post_ffn': 'new_v', 'new_v_w_up': 'new_v', 'new_v_w_conv': 'new_v', 'new_v_w_down': 'new_v'}


def _forward(args):
    return _fwd_reference(*[args[k] for k in FWD_PARAMS])


def _output_shape():
    out = _jax.eval_shape(lambda: _forward(_fwd_setup_inputs(0)))
    return out.shape, out.dtype

N_MICROBATCH = 1
ADAM_LR = 0.001
ADAM_B1 = 0.9
ADAM_B2 = 0.999
ADAM_EPS = 1e-08
ADAM_WD = 0.01
ADAM_STEP = 10
PER_EXAMPLE_BATCH_AXIS = {'x': 0, 'c': 0, 'ctx': 0, 'loss_target': 0}
SHARED_INPUTS = []
_WEIGHT_DTYPES = {'c_ctx': _jnp.float32, 'w_ada': _jnp.float32, 'b_ada': _jnp.float32, 'w_in': _jnp.float32, 'w_pool': _jnp.float32, 'pool_scale': _jnp.float32, 'ssm_a_re': _jnp.float32, 'ssm_a_im': _jnp.float32, 'ssm_log_dt': _jnp.float32, 'ssm_b_re': _jnp.float32, 'ssm_b_im': _jnp.float32, 'ssm_c_re': _jnp.float32, 'ssm_c_im': _jnp.float32, 'ssm_d': _jnp.float32, 'w_glu': _jnp.float32, 'w_out': _jnp.float32, 'g_pre_mix': _jnp.float32, 'g_post_mix': _jnp.float32, 'g_pre_ffn': _jnp.float32, 'g_post_ffn': _jnp.float32, 'w_up': _jnp.float32, 'w_conv': _jnp.float32, 'w_down': _jnp.float32}
MOMENT_SCALE = {'c_ctx': 1.233530e-03, 'w_ada': 8.081563e-01, 'b_ada': 1.542490e+00, 'w_in': 7.130944e-02, 'w_pool': 8.174905e-02, 'pool_scale': 8.493958e-02, 'ssm_a_re': 3.962053e-03, 'ssm_a_im': 4.497173e-03, 'ssm_log_dt': 1.238400e+00, 'ssm_b_re': 2.553497e-03, 'ssm_b_im': 2.363099e-03, 'ssm_c_re': 3.339852e-03, 'ssm_c_im': 3.423574e-03, 'ssm_d': 5.571015e-02, 'w_glu': 1.107263e-02, 'w_out': 7.686504e-02, 'g_pre_mix': 7.057715e-02, 'g_post_mix': 1.812722e+00, 'g_pre_ffn': 7.171972e-02, 'g_post_ffn': 1.785847e+00, 'w_up': 3.317856e-02, 'w_conv': 3.359702e-02, 'w_down': 5.663369e-02}


def _to_microbatches(a, axis):
    t = _jnp.moveaxis(a, axis, 0)
    t = t.reshape((N_MICROBATCH, t.shape[0] // N_MICROBATCH) + t.shape[1:])
    return _jnp.moveaxis(t, 1, axis + 1)


def setup_inputs(seed: int = 0) -> dict:
    inp = _fwd_setup_inputs(seed)
    key = _jax.random.fold_in(_jax.random.key(seed), 7919)
    shape, _ = _output_shape()
    out = dict(inp)
    out["loss_target"] = _jax.random.normal(_jax.random.fold_in(key, 0), shape, _jnp.float32)
    for i, name in enumerate(TWIN_WEIGHTS):
        w = inp[name].astype(_jnp.float32)
        if MOMENT_SCALE is None:
            s = _jnp.sqrt(_jnp.mean(_jnp.square(w)) + 1e-30)
        else:
            s = MOMENT_SCALE[name]
        km, kv = _jax.random.split(_jax.random.fold_in(key, i + 1))
        out[name] = w
        out["m_" + name] = s * _jax.random.normal(km, w.shape, _jnp.float32)
        out["v_" + name] = (s * s) * _jax.random.uniform(kv, w.shape, _jnp.float32, 0.5, 1.5)
    if N_MICROBATCH > 1:
        for name, axis in PER_EXAMPLE_BATCH_AXIS.items():
            out[name] = _to_microbatches(out[name], axis)
    return {'x': out['x'], 'c': out['c'], 'ctx': out['ctx'], 'c_ctx': out['c_ctx'], 'w_ada': out['w_ada'], 'b_ada': out['b_ada'], 'w_in': out['w_in'], 'w_pool': out['w_pool'], 'pool_scale': out['pool_scale'], 'ssm_a_re': out['ssm_a_re'], 'ssm_a_im': out['ssm_a_im'], 'ssm_log_dt': out['ssm_log_dt'], 'ssm_b_re': out['ssm_b_re'], 'ssm_b_im': out['ssm_b_im'], 'ssm_c_re': out['ssm_c_re'], 'ssm_c_im': out['ssm_c_im'], 'ssm_d': out['ssm_d'], 'w_glu': out['w_glu'], 'w_out': out['w_out'], 'g_pre_mix': out['g_pre_mix'], 'g_post_mix': out['g_post_mix'], 'g_pre_ffn': out['g_pre_ffn'], 'g_post_ffn': out['g_post_ffn'], 'w_up': out['w_up'], 'w_conv': out['w_conv'], 'w_down': out['w_down'], 'loss_target': out['loss_target'], 'm_c_ctx': out['m_c_ctx'], 'm_w_ada': out['m_w_ada'], 'm_b_ada': out['m_b_ada'], 'm_w_in': out['m_w_in'], 'm_w_pool': out['m_w_pool'], 'm_pool_scale': out['m_pool_scale'], 'm_ssm_a_re': out['m_ssm_a_re'], 'm_ssm_a_im': out['m_ssm_a_im'], 'm_ssm_log_dt': out['m_ssm_log_dt'], 'm_ssm_b_re': out['m_ssm_b_re'], 'm_ssm_b_im': out['m_ssm_b_im'], 'm_ssm_c_re': out['m_ssm_c_re'], 'm_ssm_c_im': out['m_ssm_c_im'], 'm_ssm_d': out['m_ssm_d'], 'm_w_glu': out['m_w_glu'], 'm_w_out': out['m_w_out'], 'm_g_pre_mix': out['m_g_pre_mix'], 'm_g_post_mix': out['m_g_post_mix'], 'm_g_pre_ffn': out['m_g_pre_ffn'], 'm_g_post_ffn': out['m_g_post_ffn'], 'm_w_up': out['m_w_up'], 'm_w_conv': out['m_w_conv'], 'm_w_down': out['m_w_down'], 'v_c_ctx': out['v_c_ctx'], 'v_w_ada': out['v_w_ada'], 'v_b_ada': out['v_b_ada'], 'v_w_in': out['v_w_in'], 'v_w_pool': out['v_w_pool'], 'v_pool_scale': out['v_pool_scale'], 'v_ssm_a_re': out['v_ssm_a_re'], 'v_ssm_a_im': out['v_ssm_a_im'], 'v_ssm_log_dt': out['v_ssm_log_dt'], 'v_ssm_b_re': out['v_ssm_b_re'], 'v_ssm_b_im': out['v_ssm_b_im'], 'v_ssm_c_re': out['v_ssm_c_re'], 'v_ssm_c_im': out['v_ssm_c_im'], 'v_ssm_d': out['v_ssm_d'], 'v_w_glu': out['v_w_glu'], 'v_w_out': out['v_w_out'], 'v_g_pre_mix': out['v_g_pre_mix'], 'v_g_post_mix': out['v_g_post_mix'], 'v_g_pre_ffn': out['v_g_pre_ffn'], 'v_g_post_ffn': out['v_g_post_ffn'], 'v_w_up': out['v_w_up'], 'v_w_conv': out['v_w_conv'], 'v_w_down': out['v_w_down']}


def _loss(weights, diff, rest, loss_target):
    with _jax.named_scope("forward"):
        args = {**rest, TWIN_DIFF_INPUT: diff, **{k: w.astype(_WEIGHT_DTYPES[k]) for k, w in weights.items()}}
        y = _forward(args)
    with _jax.named_scope("loss_head"):
        err = _jnp.square(y.astype(_jnp.float32) - loss_target)
        return 0.5 * _jnp.sum(_jnp.mean(err, axis=-1)) if err.ndim else 0.5 * err


def _adamw(w, g, m, v):
    m = ADAM_B1 * m + (1.0 - ADAM_B1) * g
    v = ADAM_B2 * v + (1.0 - ADAM_B2) * _jnp.square(g)
    m_hat = m / (1.0 - ADAM_B1 ** ADAM_STEP)
    v_hat = v / (1.0 - ADAM_B2 ** ADAM_STEP)
    delta = -ADAM_LR * (m_hat / (_jnp.sqrt(v_hat) + ADAM_EPS) + ADAM_WD * w)
    return delta, m, v


def reference(x, c, ctx, c_ctx, w_ada, b_ada, w_in, w_pool, pool_scale, ssm_a_re, ssm_a_im, ssm_log_dt, ssm_b_re, ssm_b_im, ssm_c_re, ssm_c_im, ssm_d, w_glu, w_out, g_pre_mix, g_post_mix, g_pre_ffn, g_post_ffn, w_up, w_conv, w_down, loss_target, m_c_ctx, m_w_ada, m_b_ada, m_w_in, m_w_pool, m_pool_scale, m_ssm_a_re, m_ssm_a_im, m_ssm_log_dt, m_ssm_b_re, m_ssm_b_im, m_ssm_c_re, m_ssm_c_im, m_ssm_d, m_w_glu, m_w_out, m_g_pre_mix, m_g_post_mix, m_g_pre_ffn, m_g_post_ffn, m_w_up, m_w_conv, m_w_down, v_c_ctx, v_w_ada, v_b_ada, v_w_in, v_w_pool, v_pool_scale, v_ssm_a_re, v_ssm_a_im, v_ssm_log_dt, v_ssm_b_re, v_ssm_b_im, v_ssm_c_re, v_ssm_c_im, v_ssm_d, v_w_glu, v_w_out, v_g_pre_mix, v_g_post_mix, v_g_pre_ffn, v_g_post_ffn, v_w_up, v_w_conv, v_w_down):
    given = dict(x=x, c=c, ctx=ctx, c_ctx=c_ctx, w_ada=w_ada, b_ada=b_ada, w_in=w_in, w_pool=w_pool, pool_scale=pool_scale, ssm_a_re=ssm_a_re, ssm_a_im=ssm_a_im, ssm_log_dt=ssm_log_dt, ssm_b_re=ssm_b_re, ssm_b_im=ssm_b_im, ssm_c_re=ssm_c_re, ssm_c_im=ssm_c_im, ssm_d=ssm_d, w_glu=w_glu, w_out=w_out, g_pre_mix=g_pre_mix, g_post_mix=g_post_mix, g_pre_ffn=g_pre_ffn, g_post_ffn=g_post_ffn, w_up=w_up, w_conv=w_conv, w_down=w_down, loss_target=loss_target, m_c_ctx=m_c_ctx, m_w_ada=m_w_ada, m_b_ada=m_b_ada, m_w_in=m_w_in, m_w_pool=m_w_pool, m_pool_scale=m_pool_scale, m_ssm_a_re=m_ssm_a_re, m_ssm_a_im=m_ssm_a_im, m_ssm_log_dt=m_ssm_log_dt, m_ssm_b_re=m_ssm_b_re, m_ssm_b_im=m_ssm_b_im, m_ssm_c_re=m_ssm_c_re, m_ssm_c_im=m_ssm_c_im, m_ssm_d=m_ssm_d, m_w_glu=m_w_glu, m_w_out=m_w_out, m_g_pre_mix=m_g_pre_mix, m_g_post_mix=m_g_post_mix, m_g_pre_ffn=m_g_pre_ffn, m_g_post_ffn=m_g_post_ffn, m_w_up=m_w_up, m_w_conv=m_w_conv, m_w_down=m_w_down, v_c_ctx=v_c_ctx, v_w_ada=v_w_ada, v_b_ada=v_b_ada, v_w_in=v_w_in, v_w_pool=v_w_pool, v_pool_scale=v_pool_scale, v_ssm_a_re=v_ssm_a_re, v_ssm_a_im=v_ssm_a_im, v_ssm_log_dt=v_ssm_log_dt, v_ssm_b_re=v_ssm_b_re, v_ssm_b_im=v_ssm_b_im, v_ssm_c_re=v_ssm_c_re, v_ssm_c_im=v_ssm_c_im, v_ssm_d=v_ssm_d, v_w_glu=v_w_glu, v_w_out=v_w_out, v_g_pre_mix=v_g_pre_mix, v_g_post_mix=v_g_post_mix, v_g_pre_ffn=v_g_pre_ffn, v_g_post_ffn=v_g_post_ffn, v_w_up=v_w_up, v_w_conv=v_w_conv, v_w_down=v_w_down)
    weights = {n: given[n] for n in TWIN_WEIGHTS}
    shared = {n: given[n] for n in SHARED_INPUTS}
    per_example = {n: given[n] for n in ['x', 'c', 'ctx']}
    grad_fn = _jax.value_and_grad(_loss, argnums=(0, 1))

    def one_microbatch(ex, loss_target):
        ex = dict(ex)
        diff = ex.pop(TWIN_DIFF_INPUT)
        return grad_fn(weights, diff, {**shared, **ex}, loss_target)

    if N_MICROBATCH == 1:
        loss, (grad_w, grad_x) = one_microbatch(per_example, given["loss_target"])
    else:
        def body(carry, xs):
            loss_sum, grad_sum = carry
            l_k, (gw_k, gx_k) = one_microbatch(xs[0], xs[1])
            with _jax.named_scope("update"):
                return (loss_sum + l_k, _jax.tree.map(_jnp.add, grad_sum, gw_k)), gx_k

        init = (_jnp.zeros((), _jnp.float32), _jax.tree.map(_jnp.zeros_like, weights))
        (loss, grad_w), grad_x = _jax.lax.scan(body, init, (per_example, given["loss_target"]))
    with _jax.named_scope("update"):
        delta_w, new_m, new_v = {}, {}, {}
        for n in TWIN_WEIGHTS:
            delta_w[n], new_m[n], new_v[n] = _adamw(weights[n], grad_w[n], given["m_" + n], given["v_" + n])
    return (loss, grad_x, *[grad_w[n] for n in TWIN_WEIGHTS], *[delta_w[n] for n in TWIN_WEIGHTS],
            *[new_m[n] for n in TWIN_WEIGHTS], *[new_v[n] for n in TWIN_WEIGHTS])
```

```python
import math

import jax
import jax.numpy as jnp
from jax import lax
from jax.experimental import pallas as pl
from jax.experimental.pallas import tpu as pltpu

F32 = jnp.float32
MXU_DTYPE = jnp.bfloat16
COMM_DTYPE = jnp.bfloat16
HIGHEST = lax.Precision.HIGHEST
VMEM_LIMIT_BYTES = 48 * 1024 * 1024
LANES = 128
SUBLANES = 8
N_CHIPS = 4
N_DEV = 8

EPS = 1e-6
GRID_W = 64
POOL_WINDOWS = (2, 4, 8, 16)
ADAM_LR = 0.001
ADAM_B1 = 0.9
ADAM_B2 = 0.999
ADAM_EPS = 1e-08
ADAM_WD = 0.01
ADAM_STEP = 10
GELU_C0 = math.sqrt(2.0 / math.pi)
GELU_C1 = 0.044715

SDS = jax.ShapeDtypeStruct
ANY = pl.BlockSpec(memory_space=pl.ANY)
MESH = pl.DeviceIdType.MESH


def _cparams(*sem):
    return pltpu.CompilerParams(dimension_semantics=sem if sem else None, vmem_limit_bytes=VMEM_LIMIT_BYTES)


def _pick(n, cands):
    for cand in cands:
        if n % cand == 0:
            return cand
    return n


def _row_tile(n_ctx, n):
    return math.gcd(math.gcd(n_ctx, n - n_ctx), 256)


_DIMS = {"nn": (((1,), (0,)), ((), ())), "nt": (((1,), (1,)), ((), ())), "tn": (((0,), (0,)), ((), ()))}
_TM = (1088, 1024, 512, 384, 256, 128, 64, 32, 16, 8)
_TN = (1024, 1408, 512, 384, 256, 128)
_TK = (1024, 1088, 512, 1408, 384, 256, 128, 64, 32, 16, 8)


def _mm(a, b, mode, out_dtype, name, a_idx=None, b_idx=None, a_cols=None):
    a2, b2 = a.shape[-2:], b.shape[-2:]
    alast = a2[1] if a_cols is None else a_cols[1]
    if mode == "nn":
        m, k, n = a2[0], alast, b2[1]
        assert b2[0] == k
    elif mode == "nt":
        m, k, n = a2[0], alast, b2[0]
        assert b2[1] == k
    else:
        k, m, n = a2[0], alast, b2[1]
        assert b2[0] == k
    tm, tn, tk = _pick(m, _TM), _pick(n, _TN), _pick(k, _TK)
    nk = k // tk
    a_lane_tile = tm if mode == "tn" else tk
    off = 0
    if a_cols is not None:
        assert a_cols[0] % a_lane_tile == 0
        off = a_cols[0] // a_lane_tile

    def body(a_ref, b_ref, o_ref, acc_ref):
        kk = pl.program_id(2)

        @pl.when(kk == 0)
        def _():
            acc_ref[...] = jnp.zeros(acc_ref.shape, F32)

        acc_ref[...] += lax.dot_general(a_ref[...].astype(MXU_DTYPE), b_ref[...].astype(MXU_DTYPE), _DIMS[mode],
                                        preferred_element_type=F32)

        @pl.when(kk == nk - 1)
        def _():
            o_ref[...] = acc_ref[...].astype(o_ref.dtype)

    if mode == "tn":
        a_blk, a_map = (tk, tm), (lambda i, j, kk: (kk, i + off))
    else:
        a_blk, a_map = (tm, tk), (lambda i, j, kk: (i, kk + off))
    if mode == "nt":
        b_blk, b_map = (tn, tk), (lambda i, j, kk: (j, kk))
    else:
        b_blk, b_map = (tk, tn), (lambda i, j, kk: (kk, j))
    if a_idx is not None:
        a_blk, a_map0 = (None,) + a_blk, a_map
        a_map = lambda i, j, kk: (a_idx,) + a_map0(i, j, kk)
    if b_idx is not None:
        b_blk, b_map0 = (None,) + b_blk, b_map
        b_map = lambda i, j, kk: (b_idx,) + b_map0(i, j, kk)
    return pl.pallas_call(
        body, grid=(m // tm, n // tn, nk),
        in_specs=[pl.BlockSpec(a_blk, a_map), pl.BlockSpec(b_blk, b_map)],
        out_specs=pl.BlockSpec((tm, tn), lambda i, j, kk: (i, j)),
        out_shape=SDS((m, n), out_dtype),
        scratch_shapes=[pltpu.VMEM((tm, tn), F32)],
        compiler_params=_cparams("parallel", "parallel", "arbitrary"), name=name)(a, b)


def _seg_map(nbc):
    return lambda i: (jnp.where(i < nbc, 0, 1), 0, 0)


def _rstd(v):
    return lax.rsqrt(jnp.mean(v * v, axis=-1, keepdims=True) + EPS)


def _norm_mod_fwd(x, g, mods, sh, sc, n_ctx, name):
    n, d = x.shape
    tm = _row_tile(n_ctx, n)
    nbc = n_ctx // tm

    def body(x_ref, g_ref, m_ref, h_ref):
        xv = x_ref[...]
        hn = xv * _rstd(xv) * g_ref[...]
        h_ref[...] = (hn * (1.0 + m_ref[0, sc:sc + 1, :]) + m_ref[0, sh:sh + 1, :]).astype(h_ref.dtype)

    row = pl.BlockSpec((tm, d), lambda i: (i, 0))
    return pl.pallas_call(
        body, grid=(n // tm,),
        in_specs=[row, pl.BlockSpec((1, d), lambda i: (0, 0)), pl.BlockSpec((1, 8, d), _seg_map(nbc))],
        out_specs=row, out_shape=SDS((n, d), MXU_DTYPE), compiler_params=_cparams("parallel"), name=name)(x, g, mods)


def _gate_res_fwd(x, f, g, mods, gi, n_ctx, name):
    n, d = x.shape
    tm = _row_tile(n_ctx, n)
    nbc = n_ctx // tm

    def body(x_ref, f_ref, g_ref, m_ref, o_ref):
        fv = f_ref[...]
        o_ref[...] = x_ref[...] + m_ref[0, gi:gi + 1, :] * (fv * _rstd(fv) * g_ref[...])

    row = pl.BlockSpec((tm, d), lambda i: (i, 0))
    return pl.pallas_call(
        body, grid=(n // tm,),
        in_specs=[row, row, pl.BlockSpec((1, d), lambda i: (0, 0)), pl.BlockSpec((1, 8, d), _seg_map(nbc))],
        out_specs=row, out_shape=SDS((n, d), F32), compiler_params=_cparams("parallel"), name=name)(x, f, g, mods)


def _gate_res_bwd(dx, f, g, mods, gi, n_ctx, name):
    n, d = dx.shape
    tm = _row_tile(n_ctx, n)
    nbc = n_ctx // tm

    def body(dx_ref, f_ref, g_ref, m_ref, df_ref, dgate_ref, dg_ref):
        i = pl.program_id(0)

        @pl.when(i == 0)
        def _():
            dg_ref[...] = jnp.zeros(dg_ref.shape, F32)

        @pl.when(jnp.logical_or(i == 0, i == nbc))
        def _():
            dgate_ref[...] = jnp.zeros(dgate_ref.shape, F32)

        dxv, fv, gv = dx_ref[...], f_ref[...], g_ref[...]
        rs = _rstd(fv)
        nv = fv * rs
        dgate_ref[0] += jnp.sum(dxv * (nv * gv), axis=0, keepdims=True)
        dout = dxv * m_ref[0, gi:gi + 1, :]
        dg_ref[...] += jnp.sum(dout * nv, axis=0, keepdims=True)
        dn = dout * gv
        df_ref[...] = (rs * (dn - nv * jnp.mean(dn * nv, axis=-1, keepdims=True))).astype(df_ref.dtype)

    row = pl.BlockSpec((tm, d), lambda i: (i, 0))
    vec = pl.BlockSpec((1, d), lambda i: (0, 0))
    return pl.pallas_call(
        body, grid=(n // tm,),
        in_specs=[row, row, vec, pl.BlockSpec((1, 8, d), _seg_map(nbc))],
        out_specs=[row, pl.BlockSpec((1, 1, d), _seg_map(nbc)), vec],
        out_shape=[SDS((n, d), MXU_DTYPE), SDS((2, 1, d), F32), SDS((1, d), F32)],
        compiler_params=_cparams("arbitrary"), name=name)(dx, f, g, mods)


def _norm_mod_bwd(dh, x, g, mods, sh, sc, dx_res, n_ctx, name):
    n, d = x.shape
    tm = _row_tile(n_ctx, n)
    nbc = n_ctx // tm

    def body(dh_ref, x_ref, g_ref, m_ref, r_ref, dx_ref, dss_ref, dg_ref):
        i = pl.program_id(0)

        @pl.when(i == 0)
        def _():
            dg_ref[...] = jnp.zeros(dg_ref.shape, F32)

        @pl.when(jnp.logical_or(i == 0, i == nbc))
        def _():
            dss_ref[...] = jnp.zeros(dss_ref.shape, F32)

        dhv, xv, gv = dh_ref[...], x_ref[...], g_ref[...]
        rs = _rstd(xv)
        nv = xv * rs
        dss_ref[0, 0:1, :] += jnp.sum(dhv, axis=0, keepdims=True)
        dss_ref[0, 1:2, :] += jnp.sum(dhv * (nv * gv), axis=0, keepdims=True)
        dhn = dhv * (1.0 + m_ref[0, sc:sc + 1, :])
        dg_ref[...] += jnp.sum(dhn * nv, axis=0, keepdims=True)
        dn = dhn * gv
        dx_ref[...] = r_ref[...] + rs * (dn - nv * jnp.mean(dn * nv, axis=-1, keepdims=True))

    row = pl.BlockSpec((tm, d), lambda i: (i, 0))
    vec = pl.BlockSpec((1, d), lambda i: (0, 0))
    return pl.pallas_call(
        body, grid=(n // tm,),
        in_specs=[row, row, vec, pl.BlockSpec((1, 8, d), _seg_map(nbc)), row],
        out_specs=[row, pl.BlockSpec((1, 2, d), _seg_map(nbc)), vec],
        out_shape=[SDS((n, d), F32), SDS((2, 2, d), F32), SDS((1, d), F32)],
        compiler_params=_cparams("arbitrary"), name=name)(dh, x, g, mods, dx_res)


def _loss_grad(xc, target, n_ctx, name):
    n, d = xc.shape
    tm = _row_tile(n_ctx, n)
    nbc = n_ctx // tm
    nb = n // tm

    def body(x_ref, t_ref, dx_ref, l_ref, acc_ref):
        i = pl.program_id(0)

        @pl.when(i == 0)
        def _():
            acc_ref[...] = jnp.zeros(acc_ref.shape, F32)

        @pl.when(i < nbc)
        def _():
            dx_ref[...] = jnp.zeros(dx_ref.shape, F32)

        @pl.when(i >= nbc)
        def _():
            diff = x_ref[...] - t_ref[...]
            dx_ref[...] = diff * (1.0 / d)
            acc_ref[...] += jnp.sum(diff * diff, axis=0, keepdims=True)

        @pl.when(i == nb - 1)
        def _():
            l_ref[...] = jnp.full(l_ref.shape, (0.5 / d) * jnp.sum(acc_ref[...]), F32)

    row = pl.BlockSpec((tm, d), lambda i: (i, 0))
    return pl.pallas_call(
        body, grid=(nb,),
        in_specs=[row, pl.BlockSpec((tm, d), lambda i: (jnp.maximum(i - nbc, 0), 0))],
        out_specs=[row, pl.BlockSpec((SUBLANES, LANES), lambda i: (0, 0))],
        out_shape=[SDS((n, d), F32), SDS((SUBLANES, LANES), F32)],
        scratch_shapes=[pltpu.VMEM((1, d), F32)],
        compiler_params=_cparams("arbitrary"), name=name)(xc, target)


POOL_PAD = 16


def _pool(src, pool_width, pool_group, n_ctx, bwd, out_dtype, name):
    n = src.shape[0]
    n_lat = n - n_ctx
    gb = pool_group // LANES
    segs = ((0, n_ctx, POOL_PAD), (n_ctx, n_lat, 2 * POOL_PAD + n_ctx))
    total = 3 * POOL_PAD + n

    def body(s_ref, o_ref, scr):
        j = pl.program_id(0)
        for base in (0, POOL_PAD + n_ctx, 2 * POOL_PAD + n):
            scr[pl.ds(base, POOL_PAD), :] = jnp.zeros((POOL_PAD, LANES), F32)
        for gi, w in enumerate(POOL_WINDOWS):
            @pl.when(jnp.logical_and(j >= gi * gb, j < (gi + 1) * gb))
            def _(w=w):
                half = w // 2
                offs = range(-half + 1, half + 1) if bwd else range(-half, half)
                for row0, nseg, base in segs:
                    ch = math.gcd(nseg, 256)

                    def count(c0):
                        t = c0 + lax.broadcasted_iota(jnp.int32, (ch, LANES), 0)
                        return (jnp.minimum(t + half, nseg) - jnp.maximum(t - half, 0)).astype(F32)

                    def fill(ci, carry):
                        c0 = pl.multiple_of(ci * ch, ch)
                        v = s_ref[pl.ds(row0 + c0, ch), :]
                        scr[pl.ds(base + c0, ch), :] = v / count(c0) if bwd else v
                        return carry

                    def window(ci, carry):
                        c0 = pl.multiple_of(ci * ch, ch)
                        acc = jnp.zeros((ch, LANES), F32)
                        for off in offs:
                            acc = acc + scr[pl.ds(c0 + (base + off), ch), :]
                        v = s_ref[pl.ds(row0 + c0, ch), :]
                        res = acc - v if bwd else acc / count(c0) - v
                        o_ref[pl.ds(row0 + c0, ch), :] = res.astype(o_ref.dtype)
                        return carry

                    lax.fori_loop(0, nseg // ch, fill, 0)
                    lax.fori_loop(0, nseg // ch, window, 0)

    blk = pl.BlockSpec((n, LANES), lambda j: (0, j))
    return pl.pallas_call(
        body, grid=(pool_width // LANES,), in_specs=[blk], out_specs=blk,
        out_shape=SDS((n, pool_width), out_dtype), scratch_shapes=[pltpu.VMEM((total, LANES), F32)],
        compiler_params=_cparams("parallel"), name=name)(src)


def _pool_proj_fwd(p, wp, l, scale, name):
    n, pw = p.shape
    ng, c = wp.shape[1], wp.shape[2]
    tm = _pick(n, _TM)

    def body(p_ref, w_ref, s_ref, o_ref):
        y = jnp.dot(p_ref[...], w_ref[...].astype(MXU_DTYPE), preferred_element_type=F32)
        o_ref[...] = (y * s_ref[...]).astype(o_ref.dtype)

    return pl.pallas_call(
        body, grid=(ng, n // tm),
        in_specs=[pl.BlockSpec((tm, c), lambda g, i: (i, g)), pl.BlockSpec((None, None, c, c), lambda g, i: (l, g, 0, 0)),
                  pl.BlockSpec((1, c), lambda g, i: (0, g))],
        out_specs=pl.BlockSpec((tm, c), lambda g, i: (i, g)), out_shape=SDS((n, pw), MXU_DTYPE),
        compiler_params=_cparams("parallel", "parallel"), name=name)(p, wp, scale)


def _pool_proj_bwd(p, dcat, wp, l, scale, name):
    n, pw = p.shape
    ng, c = wp.shape[1], wp.shape[2]
    tm = _pick(n, _TM)

    def body(p_ref, dy_ref, w_ref, s_ref, dp_ref, ds_ref, dw_ref):
        i = pl.program_id(1)

        @pl.when(i == 0)
        def _():
            ds_ref[...] = jnp.zeros(ds_ref.shape, F32)
            dw_ref[...] = jnp.zeros(dw_ref.shape, F32)

        pv, wv, dy = p_ref[...], w_ref[...].astype(MXU_DTYPE), dy_ref[...]
        y = jnp.dot(pv, wv, preferred_element_type=F32)
        ds_ref[...] += jnp.sum(dy * y, axis=0, keepdims=True)
        dpw = (dy * s_ref[...]).astype(MXU_DTYPE)
        dp_ref[...] = lax.dot_general(dpw, wv, _DIMS["nt"], preferred_element_type=F32)
        dw_ref[0] += lax.dot_general(pv, dpw, _DIMS["tn"], preferred_element_type=F32)

    return pl.pallas_call(
        body, grid=(ng, n // tm),
        in_specs=[pl.BlockSpec((tm, c), lambda g, i: (i, g)), pl.BlockSpec((tm, c), lambda g, i: (i, g)),
                  pl.BlockSpec((None, None, c, c), lambda g, i: (l, g, 0, 0)), pl.BlockSpec((1, c), lambda g, i: (0, g))],
        out_specs=[pl.BlockSpec((tm, c), lambda g, i: (i, g)), pl.BlockSpec((1, c), lambda g, i: (0, g)),
                   pl.BlockSpec((1, c, c), lambda g, i: (g, 0, 0))],
        out_shape=[SDS((n, pw), F32), SDS((1, pw), F32), SDS((ng, c, c), F32)],
        compiler_params=_cparams("arbitrary", "arbitrary"), name=name)(p, dcat, wp, scale)


def _disc_math(a_re, a_im, logdt, b_re, b_im):
    dt = jnp.exp(logdt)
    mag = jnp.exp(a_re * dt)
    lam_re = mag * jnp.cos(a_im * dt)
    lam_im = mag * jnp.sin(a_im * dt)
    denom = a_re * a_re + a_im * a_im
    nr, ni = lam_re - 1.0, lam_im
    f_re = ((nr * a_re + ni * a_im) / denom)[:, None, :]
    f_im = ((ni * a_re - nr * a_im) / denom)[:, None, :]
    return lam_re, lam_im, f_re * b_re - f_im * b_im, f_re * b_im + f_im * b_re


def _disc_fwd(a_re, a_im, logdt, b_re, b_im, name):
    def body(ar, ai, ld, br, bi, o_lr, o_li, o_br, o_bi):
        lr, li, bbr, bbi = _disc_math(ar[...], ai[...], ld[...], br[...], bi[...])
        o_lr[...] = lr
        o_li[...] = li
        o_br[...] = bbr
        o_bi[...] = bbi

    return pl.pallas_call(
        body, out_shape=[SDS(a_re.shape, F32), SDS(a_re.shape, F32), SDS(b_re.shape, F32), SDS(b_re.shape, F32)],
        compiler_params=_cparams(), name=name)(a_re, a_im, logdt, b_re, b_im)


def _disc_bwd(a_re, a_im, logdt, b_re, b_im, d_lr, d_li, d_bbr, d_bbi, group, name):
    rows, gp = a_re.shape

    def body(ar, ai, ld, br, bi, g_lr, g_li, g_br, g_bi, o_ar, o_ai, o_ld, o_br, o_bi):
        _, vjp = jax.vjp(_disc_math, ar[...], ai[...], ld[...], br[...], bi[...])
        dar, dai, dld, dbr, dbi = vjp((g_lr[...], g_li[...], g_br[...], g_bi[...]))
        o_ar[...] = dar
        o_ai[...] = dai
        state = lax.broadcasted_iota(jnp.int32, (gp, LANES), 0)
        first = lax.broadcasted_iota(jnp.int32, (gp, LANES), 1) * group
        sel = jnp.logical_and(state >= first, state < first + group).astype(F32)
        o_ld[...] = jnp.dot(dld, sel, precision=HIGHEST, preferred_element_type=F32)
        o_br[...] = dbr
        o_bi[...] = dbi

    return pl.pallas_call(
        body, out_shape=[SDS(a_re.shape, F32), SDS(a_re.shape, F32), SDS((rows, LANES), F32),
                         SDS(b_re.shape, F32), SDS(b_re.shape, F32)],
        compiler_params=_cparams(), name=name)(a_re, a_im, logdt, b_re, b_im, d_lr, d_li, d_bbr, d_bbi)


def _scan_maps(nbc, nb):
    nbl = nb - nbc
    fwd0 = lambda i: (i, 0, 0)
    fwd1 = lambda i: (jnp.where(i < nbc, nbc - 1 - i, nb - 1 - (i - nbc)), 0, 0)
    adj0 = lambda i: (nb - 1 - i, 0, 0)
    adj1 = lambda i: (jnp.where(i < nbl, nbc + i, i - nbl), 0, 0)
    return fwd0, fwd1, adj0, adj1


def _scan_fwd(bu0, bu1, lam, n_ctx, name):
    n, s2, _ = bu0.shape
    s = s2 // 2
    tt = math.gcd(math.gcd(n_ctx, n - n_ctx), 128)
    nbc, nb = n_ctx // tt, n // tt
    fwd0, fwd1, _, _ = _scan_maps(nbc, nb)

    def body(b0_ref, b1_ref, lam_ref, h0_ref, h1_ref, st_ref):
        @pl.when(pl.program_id(0) == 0)
        def _():
            st_ref[...] = jnp.zeros(st_ref.shape, F32)

        lr0, li0, lr1, li1 = lam_ref[0], lam_ref[1], lam_ref[2], lam_ref[3]

        def step(j, carry):
            h0r, h0i, h1r, h1i = carry
            t1 = tt - 1 - j
            n0r = lr0 * h0r - li0 * h0i + b0_ref[j, 0:s, :]
            n0i = lr0 * h0i + li0 * h0r + b0_ref[j, s:s2, :]
            n1r = lr1 * h1r - li1 * h1i + b1_ref[t1, 0:s, :]
            n1i = lr1 * h1i + li1 * h1r + b1_ref[t1, s:s2, :]
            h0_ref[j, 0:s, :] = n0r
            h0_ref[j, s:s2, :] = n0i
            h1_ref[t1, 0:s, :] = n1r
            h1_ref[t1, s:s2, :] = n1i
            return n0r, n0i, n1r, n1i

        out = lax.fori_loop(0, tt, step, (st_ref[0], st_ref[1], st_ref[2], st_ref[3]), unroll=2)
        for q in range(4):
            st_ref[q] = out[q]

    blk = (tt, s2, LANES)
    return pl.pallas_call(
        body, grid=(nb,),
        in_specs=[pl.BlockSpec(blk, fwd0), pl.BlockSpec(blk, fwd1), pl.BlockSpec((4, s, LANES), lambda i: (0, 0, 0))],
        out_specs=[pl.BlockSpec(blk, fwd0), pl.BlockSpec(blk, fwd1)],
        out_shape=[SDS(bu0.shape, F32), SDS(bu1.shape, F32)],
        scratch_shapes=[pltpu.VMEM((4, s, LANES), F32)],
        compiler_params=_cparams("arbitrary"), name=name)(bu0, bu1, lam)


def _scan_bwd(dh0, dh1, h0, h1, lam, n_ctx, name):
    n, s2, _ = dh0.shape
    s = s2 // 2
    tt = math.gcd(math.gcd(n_ctx, n - n_ctx), 128)
    nbc, nb = n_ctx // tt, n // tt
    _, _, adj0, adj1 = _scan_maps(nbc, nb)

    def body(d0_ref, d1_ref, h0_ref, h1_ref, lam_ref, a0_ref, a1_ref, dl_ref, st_ref, acc_ref):
        i = pl.program_id(0)

        @pl.when(i == 0)
        def _():
            st_ref[...] = jnp.zeros(st_ref.shape, F32)
            acc_ref[...] = jnp.zeros(acc_ref.shape, F32)

        lr0, li0, lr1, li1 = lam_ref[0], lam_ref[1], lam_ref[2], lam_ref[3]

        def step(j, carry):
            a0r, a0i, a1r, a1i, c0r, c0i, c1r, c1i = carry
            t0 = tt - 1 - j
            g0r, g0i = h0_ref[t0, 0:s, :], h0_ref[t0, s:s2, :]
            g1r, g1i = h1_ref[j, 0:s, :], h1_ref[j, s:s2, :]
            c0r = c0r + a0r * g0r + a0i * g0i
            c0i = c0i + a0i * g0r - a0r * g0i
            c1r = c1r + a1r * g1r + a1i * g1i
            c1i = c1i + a1i * g1r - a1r * g1i
            n0r = lr0 * a0r + li0 * a0i + d0_ref[t0, 0:s, :]
            n0i = lr0 * a0i - li0 * a0r + d0_ref[t0, s:s2, :]
            n1r = lr1 * a1r + li1 * a1i + d1_ref[j, 0:s, :]
            n1i = lr1 * a1i - li1 * a1r + d1_ref[j, s:s2, :]
            a0_ref[t0, 0:s, :] = n0r
            a0_ref[t0, s:s2, :] = n0i
            a1_ref[j, 0:s, :] = n1r
            a1_ref[j, s:s2, :] = n1i
            return n0r, n0i, n1r, n1i, c0r, c0i, c1r, c1i

        init = tuple(st_ref[q] for q in range(4)) + tuple(acc_ref[q] for q in range(4))
        out = lax.fori_loop(0, tt, step, init, unroll=2)
        for q in range(4):
            st_ref[q] = out[q]
            acc_ref[q] = out[4 + q]

        @pl.when(i == nb - 1)
        def _():
            for q in range(4):
                dl_ref[q] = out[4 + q]

    blk = (tt, s2, LANES)
    small = pl.BlockSpec((4, s, LANES), lambda i: (0, 0, 0))
    return pl.pallas_call(
        body, grid=(nb,),
        in_specs=[pl.BlockSpec(blk, adj0), pl.BlockSpec(blk, adj1), pl.BlockSpec(blk, adj0), pl.BlockSpec(blk, adj1), small],
        out_specs=[pl.BlockSpec(blk, adj0), pl.BlockSpec(blk, adj1), small],
        out_shape=[SDS(dh0.shape, F32), SDS(dh1.shape, F32), SDS((4, s, LANES), F32)],
        scratch_shapes=[pltpu.VMEM((4, s, LANES), F32), pltpu.VMEM((4, s, LANES), F32)],
        compiler_params=_cparams("arbitrary"), name=name)(dh0, dh1, h0, h1, lam)


def _gelu(v):
    th = jnp.tanh(GELU_C0 * (v + GELU_C1 * v * v * v))
    return 0.5 * v * (1.0 + th), th


def _ssm_head_fwd(y, u, ssm_d, wg, l, name):
    n, sw = y.shape
    ucol = u.shape[1] // sw - 1
    tm = _pick(n, _TM)

    def body(y_ref, u_ref, d_ref, w_ref, o_ref):
        act, _ = _gelu(y_ref[...] + d_ref[...] * u_ref[...])
        q = jnp.dot(act.astype(MXU_DTYPE), w_ref[...].astype(MXU_DTYPE), preferred_element_type=F32)
        o_ref[...] = (act * jax.nn.sigmoid(q)).astype(o_ref.dtype)

    row = pl.BlockSpec((tm, sw), lambda i: (i, 0))
    return pl.pallas_call(
        body, grid=(n // tm,),
        in_specs=[row, pl.BlockSpec((tm, sw), lambda i: (i, ucol)), pl.BlockSpec((1, sw), lambda i: (0, 0)),
                  pl.BlockSpec((None, sw, sw), lambda i: (l, 0, 0))],
        out_specs=row, out_shape=SDS((n, sw), MXU_DTYPE), compiler_params=_cparams("parallel"), name=name)(y, u, ssm_d, wg)


def _ssm_head_bwd(dcat, y, u, ssm_d, wg, l, name):
    n, sw = y.shape
    ucol = u.shape[1] // sw - 1
    tm = _pick(n, _TM)

    def body(do_ref, y_ref, u_ref, d_ref, w_ref, dy_ref, du_ref, act_ref, dq_ref, dd_ref):
        @pl.when(pl.program_id(0) == 0)
        def _():
            dd_ref[...] = jnp.zeros(dd_ref.shape, F32)

        uv, dv, do = u_ref[...], d_ref[...], do_ref[...]
        yf = y_ref[...] + dv * uv
        act, th = _gelu(yf)
        wv = w_ref[...].astype(MXU_DTYPE)
        sg = jax.nn.sigmoid(jnp.dot(act.astype(MXU_DTYPE), wv, preferred_element_type=F32))
        dq = (do * act * sg * (1.0 - sg)).astype(MXU_DTYPE)
        dact = do * sg + lax.dot_general(dq, wv, _DIMS["nt"], preferred_element_type=F32)
        dgelu = 0.5 * (1.0 + th) + 0.5 * yf * (1.0 - th * th) * GELU_C0 * (1.0 + 3.0 * GELU_C1 * yf * yf)
        dyf = dact * dgelu
        dy_ref[...] = dyf.astype(dy_ref.dtype)
        du_ref[...] = dyf * dv
        act_ref[...] = act.astype(act_ref.dtype)
        dq_ref[...] = dq
        dd_ref[...] += jnp.sum(dyf * uv, axis=0, keepdims=True)

    row = pl.BlockSpec((tm, sw), lambda i: (i, 0))
    last = pl.BlockSpec((tm, sw), lambda i: (i, ucol))
    vec = pl.BlockSpec((1, sw), lambda i: (0, 0))
    return pl.pallas_call(
        body, grid=(n // tm,),
        in_specs=[last, row, last, vec, pl.BlockSpec((None, sw, sw), lambda i: (l, 0, 0))],
        out_specs=[row, row, row, row, vec],
        out_shape=[SDS((n, sw), MXU_DTYPE), SDS((n, sw), F32), SDS((n, sw), MXU_DTYPE), SDS((n, sw), MXU_DTYPE),
                   SDS((1, sw), F32)],
        compiler_params=_cparams("arbitrary"), name=name)(dcat, y, u, ssm_d, wg)


def _assemble_du(du_pool, du_dir, du_proj, name):
    n, pw = du_pool.shape
    sw = du_dir.shape[1]
    tm = _pick(n, _TM)

    def body(p_ref, a_ref, b_ref, o_ref):
        o_ref[:, 0:pw] = p_ref[...].astype(o_ref.dtype)
        o_ref[:, pw:pw + sw] = (a_ref[...] + b_ref[...]).astype(o_ref.dtype)

    return pl.pallas_call(
        body, grid=(n // tm,),
        in_specs=[pl.BlockSpec((tm, pw), lambda i: (i, 0)), pl.BlockSpec((tm, sw), lambda i: (i, 0)),
                  pl.BlockSpec((tm, sw), lambda i: (i, 0))],
        out_specs=pl.BlockSpec((tm, pw + sw), lambda i: (i, 0)), out_shape=SDS((n, pw + sw), MXU_DTYPE),
        compiler_params=_cparams("parallel"), name=name)(du_pool, du_dir, du_proj)


CONV_PAD = GRID_W + SUBLANES


def _conv_layout(n, n_ctx):
    return CONV_PAD, 2 * CONV_PAD + n_ctx, 3 * CONV_PAD + n


def _col_masks(ch):
    col = lax.broadcasted_iota(jnp.int32, (ch, LANES), 0) % GRID_W
    return col != 0, col != GRID_W - 1


def _fill_padded(scr, src_ref, n, n_ctx):
    base_c, base_l, total = _conv_layout(n, n_ctx)
    for base in (0, base_c + n_ctx, base_l + n - n_ctx):
        scr[pl.ds(base, CONV_PAD), :] = jnp.zeros((CONV_PAD, LANES), F32)
    for row0, nseg, base in ((0, n_ctx, base_c), (n_ctx, n - n_ctx, base_l)):
        ch = math.gcd(nseg, 512)

        def copy(ci, carry, row0=row0, base=base, ch=ch):
            c0 = pl.multiple_of(ci * ch, ch)
            scr[pl.ds(base + c0, ch), :] = src_ref[pl.ds(row0 + c0, ch), :]
            return carry

        lax.fori_loop(0, nseg // ch, copy, 0)


def _conv_ctx(scr, k_ref, base, c0, ch, sign):
    acc = scr[pl.ds(c0 + base, ch), :] * k_ref[4:5, :]
    acc = acc + scr[pl.ds(c0 + (base - sign), ch), :] * k_ref[3:4, :]
    return acc + scr[pl.ds(c0 + (base + sign), ch), :] * k_ref[5:6, :]


def _conv_lat(scr, k_ref, base, c0, ch, sign, m_l, m_r):
    cols = []
    for j in range(3):
        acc = None
        for i in range(3):
            off = sign * (GRID_W * (i - 1) + (j - 1))
            term = scr[pl.ds(c0 + (base + off), ch), :] * k_ref[3 * i + j:3 * i + j + 1, :]
            acc = term if acc is None else acc + term
        cols.append(acc)
    first, last = (m_l, m_r) if sign > 0 else (m_r, m_l)
    return cols[1] + jnp.where(first, cols[0], 0.0) + jnp.where(last, cols[2], 0.0)


def _conv_chunk(n_lat):
    return math.gcd(n_lat, 256)


def _conv_glu_fwd(z, wk, n_ctx, name):
    n, f2 = z.shape
    dff = f2 // 2
    nvt = dff // LANES
    n_lat = n - n_ctx
    base_c, base_l, total = _conv_layout(n, n_ctx)
    ch = _conv_chunk(n_lat)
    assert ch % GRID_W == 0

    def body(zv_ref, zg_ref, kv_ref, kg_ref, a_ref, sv, sg):
        _fill_padded(sv, zv_ref, n, n_ctx)
        _fill_padded(sg, zg_ref, n, n_ctx)
        cv = _conv_ctx(sv, kv_ref, base_c, 0, n_ctx, 1)
        cg = _conv_ctx(sg, kg_ref, base_c, 0, n_ctx, 1)
        a_ref[pl.ds(0, n_ctx), :] = (cv * cg * jax.nn.sigmoid(cg)).astype(a_ref.dtype)
        m_l, m_r = _col_masks(ch)

        def lat(ci, carry):
            c0 = pl.multiple_of(ci * ch, ch)
            cv = _conv_lat(sv, kv_ref, base_l, c0, ch, 1, m_l, m_r)
            cg = _conv_lat(sg, kg_ref, base_l, c0, ch, 1, m_l, m_r)
            a_ref[pl.ds(n_ctx + c0, ch), :] = (cv * cg * jax.nn.sigmoid(cg)).astype(a_ref.dtype)
            return carry

        lax.fori_loop(0, n_lat // ch, lat, 0)

    col = lambda shift: pl.BlockSpec((n, LANES), lambda j: (0, j + shift))
    kcol = lambda shift: pl.BlockSpec((9, LANES), lambda j: (0, j + shift))
    return pl.pallas_call(
        body, grid=(nvt,), in_specs=[col(0), col(nvt), kcol(0), kcol(nvt)], out_specs=col(0),
        out_shape=SDS((n, dff), MXU_DTYPE),
        scratch_shapes=[pltpu.VMEM((total, LANES), F32), pltpu.VMEM((total, LANES), F32)],
        compiler_params=_cparams("parallel"), name=name)(z, z, wk, wk)


def _conv_glu_bwd(z, da, wk, n_ctx, name):
    n, f2 = z.shape
    dff = f2 // 2
    nvt = dff // LANES
    n_lat = n - n_ctx
    base_c, base_l, total = _conv_layout(n, n_ctx)
    ch = _conv_chunk(n_lat)
    assert ch % GRID_W == 0
    ctx_taps = [(1, 0), (1, 1), (1, 2)]
    lat_taps = [(i, j) for i in range(3) for j in range(3)]

    def tap_sums(acc, scr, d, base, c0, rows, taps, masks):
        acc = list(acc)
        for i, j in taps:
            src = scr[pl.ds(c0 + (base + GRID_W * (i - 1) + (j - 1)), rows), :]
            if masks is not None and j != 1:
                src = jnp.where(masks[0] if j == 0 else masks[1], src, 0.0)
            acc[3 * i + j] = acc[3 * i + j] + jnp.sum((src * d).reshape(rows // SUBLANES, SUBLANES, LANES), axis=0)
        return acc

    def body(zv_ref, zg_ref, da_ref, kv_ref, kg_ref, dzv_ref, dzg_ref, dkv_ref, dkg_ref, a_ref, sv, sg, dv, dg):
        _fill_padded(sv, zv_ref, n, n_ctx)
        _fill_padded(sg, zg_ref, n, n_ctx)
        for base in (0, base_c + n_ctx, base_l + n_lat):
            dv[pl.ds(base, CONV_PAD), :] = jnp.zeros((CONV_PAD, LANES), F32)
            dg[pl.ds(base, CONV_PAD), :] = jnp.zeros((CONV_PAD, LANES), F32)
        m_l, m_r = _col_masks(ch)

        def first_pass(cv, cg, row, pad_row, rows):
            sig = jax.nn.sigmoid(cg)
            silu = cg * sig
            a_ref[pl.ds(row, rows), :] = (cv * silu).astype(a_ref.dtype)
            dav = da_ref[pl.ds(row, rows), :]
            dcv = dav * silu
            dcg = dav * cv * (sig * (1.0 + cg * (1.0 - sig)))
            dv[pl.ds(pad_row, rows), :] = dcv
            dg[pl.ds(pad_row, rows), :] = dcg
            return dcv, dcg

        zero = [jnp.zeros((SUBLANES, LANES), F32) for _ in range(9)]
        cv = _conv_ctx(sv, kv_ref, base_c, 0, n_ctx, 1)
        cg = _conv_ctx(sg, kg_ref, base_c, 0, n_ctx, 1)
        dcv, dcg = first_pass(cv, cg, 0, base_c, n_ctx)
        accv = tap_sums(zero, sv, dcv, base_c, 0, n_ctx, ctx_taps, None)
        accg = tap_sums(zero, sg, dcg, base_c, 0, n_ctx, ctx_taps, None)

        def lat1(ci, carry):
            accv, accg = carry
            c0 = pl.multiple_of(ci * ch, ch)
            cv = _conv_lat(sv, kv_ref, base_l, c0, ch, 1, m_l, m_r)
            cg = _conv_lat(sg, kg_ref, base_l, c0, ch, 1, m_l, m_r)
            dcv, dcg = first_pass(cv, cg, n_ctx + c0, base_l + c0, ch)
            accv = tap_sums(accv, sv, dcv, base_l, c0, ch, lat_taps, (m_l, m_r))
            accg = tap_sums(accg, sg, dcg, base_l, c0, ch, lat_taps, (m_l, m_r))
            return tuple(accv), tuple(accg)

        accv, accg = lax.fori_loop(0, n_lat // ch, lat1, (tuple(accv), tuple(accg)))
        for t in range(9):
            dkv_ref[t:t + 1, :] = jnp.sum(accv[t], axis=0, keepdims=True)
            dkg_ref[t:t + 1, :] = jnp.sum(accg[t], axis=0, keepdims=True)

        dzv_ref[pl.ds(0, n_ctx), :] = _conv_ctx(dv, kv_ref, base_c, 0, n_ctx, -1).astype(dzv_ref.dtype)
        dzg_ref[pl.ds(0, n_ctx), :] = _conv_ctx(dg, kg_ref, base_c, 0, n_ctx, -1).astype(dzg_ref.dtype)

        def lat2(ci, carry):
            c0 = pl.multiple_of(ci * ch, ch)
            dzv_ref[pl.ds(n_ctx + c0, ch), :] = _conv_lat(dv, kv_ref, base_l, c0, ch, -1, m_l, m_r).astype(dzv_ref.dtype)
            dzg_ref[pl.ds(n_ctx + c0, ch), :] = _conv_lat(dg, kg_ref, base_l, c0, ch, -1, m_l, m_r).astype(dzg_ref.dtype)
            return carry

        lax.fori_loop(0, n_lat // ch, lat2, 0)

    col = lambda shift: pl.BlockSpec((n, LANES), lambda j: (0, j + shift))
    kcol = lambda shift: pl.BlockSpec((9, LANES), lambda j: (0, j + shift))
    pad = pltpu.VMEM((total, LANES), F32)
    return pl.pallas_call(
        body, grid=(nvt,), in_specs=[col(0), col(nvt), col(0), kcol(0), kcol(nvt)],
        out_specs=[col(0), col(0), kcol(0), kcol(0), col(0)],
        out_shape=[SDS((n, dff), MXU_DTYPE), SDS((n, dff), MXU_DTYPE), SDS((9, dff), F32), SDS((9, dff), F32),
                   SDS((n, dff), MXU_DTYPE)],
        scratch_shapes=[pad, pad, pad, pad],
        compiler_params=_cparams("parallel"), name=name)(z, z, da, wk, wk)


def _silu(v):
    return v * jax.nn.sigmoid(v)


def _ada_fwd(cond, w_ada, b_shard, name):
    nl, d, cols = w_ada.shape
    tn = _pick(cols, (512, 256, 128))

    def body(c_ref, w_ref, b_ref, o_ref):
        o_ref[...] = jnp.dot(_silu(c_ref[...]), w_ref[...], precision=HIGHEST, preferred_element_type=F32) + b_ref[...]

    return pl.pallas_call(
        body, grid=(nl, cols // tn),
        in_specs=[pl.BlockSpec(cond.shape, lambda l, j: (0, 0)), pl.BlockSpec((None, d, tn), lambda l, j: (l, 0, j)),
                  pl.BlockSpec((None, 1, tn), lambda l, j: (l, 0, j))],
        out_specs=pl.BlockSpec((None, cond.shape[0], tn), lambda l, j: (l, 0, j)),
        out_shape=SDS((nl, cond.shape[0], cols), F32),
        compiler_params=_cparams("parallel", "parallel"), name=name)(cond, w_ada, b_shard)


def _ada_dw(cond, dmod, name):
    nl, rows, cols = dmod.shape
    d = cond.shape[1]
    tn = _pick(cols, (512, 256, 128))

    def body(c_ref, g_ref, o_ref):
        o_ref[...] = lax.dot_general(_silu(c_ref[...]), g_ref[...], _DIMS["tn"], precision=HIGHEST,
                                     preferred_element_type=F32)

    return pl.pallas_call(
        body, grid=(nl, cols // tn),
        in_specs=[pl.BlockSpec(cond.shape, lambda l, j: (0, 0)), pl.BlockSpec((None, rows, tn), lambda l, j: (l, 0, j))],
        out_specs=pl.BlockSpec((None, d, tn), lambda l, j: (l, 0, j)), out_shape=SDS((nl, d, cols), F32),
        compiler_params=_cparams("parallel", "parallel"), name=name)(cond, dmod)


def _ada_dcond(dmod, w_ada, name):
    nl, rows, cols = dmod.shape
    d = w_ada.shape[1]
    tn = _pick(cols, (512, 256, 128))

    def body(g_ref, w_ref, o_ref):
        @pl.when(jnp.logical_and(pl.program_id(0) == 0, pl.program_id(1) == 0))
        def _():
            o_ref[...] = jnp.zeros(o_ref.shape, F32)

        o_ref[...] += lax.dot_general(g_ref[...], w_ref[...], _DIMS["nt"], precision=HIGHEST, preferred_element_type=F32)

    return pl.pallas_call(
        body, grid=(nl, cols // tn),
        in_specs=[pl.BlockSpec((None, rows, tn), lambda l, j: (l, 0, j)), pl.BlockSpec((None, d, tn), lambda l, j: (l, 0, j))],
        out_specs=pl.BlockSpec((rows, d), lambda l, j: (0, 0)), out_shape=SDS((rows, d), F32),
        compiler_params=_cparams("arbitrary", "arbitrary"), name=name)(dmod, w_ada)


def _ada_rows(dmod_all, name):
    nd, nl, _, w = dmod_all.shape
    tn = _pick(w, (2048, 1024, 512, 256, 128))

    def body(g_ref, rows_ref, db_ref):
        ctx = g_ref[0, 0, 0:1, :]
        for b in range(1, nd):
            ctx = ctx + g_ref[b, 0, 0:1, :]
        total = ctx
        for b in range(nd):
            lat = g_ref[b, 0, 1:2, :]
            rows_ref[b:b + 1, :] = lat
            total = total + lat
        rows_ref[nd:nd + 1, :] = ctx
        rows_ref[nd + 1:16, :] = jnp.zeros((16 - nd - 1, tn), F32)
        db_ref[...] = total

    return pl.pallas_call(
        body, grid=(nl, w // tn),
        in_specs=[pl.BlockSpec((nd, 1, 2, tn), lambda l, j: (0, l, 0, j))],
        out_specs=[pl.BlockSpec((None, 16, tn), lambda l, j: (l, 0, j)), pl.BlockSpec((None, 1, tn), lambda l, j: (l, 0, j))],
        out_shape=[SDS((nl, 16, w), F32), SDS((nl, 1, w), F32)],
        compiler_params=_cparams("parallel", "parallel"), name=name)(dmod_all)


def _ada_dctx(parts, c_ctx, row, name):
    def body(p_ref, c_ref, o_ref):
        ds = p_ref[0, row:row + 1, :]
        for k in range(1, p_ref.shape[0]):
            ds = ds + p_ref[k, row:row + 1, :]
        cv = c_ref[...]
        sg = jax.nn.sigmoid(cv)
        o_ref[...] = ds * (sg * (1.0 + cv * (1.0 - sg)))

    return pl.pallas_call(body, out_shape=SDS(c_ctx.shape, F32), compiler_params=_cparams(), name=name)(parts, c_ctx)


def _as_rows(shape):
    size = math.prod(shape)
    cols = _pick(size, (1024, 512, 256, 128))
    return size // cols, cols


def _sum_slots(buf, out_dtype, name):
    ns = buf.shape[0]
    rows, cols = _as_rows(buf.shape[1:])
    tr = _pick(rows, (512, 256, 128, 64, 32, 16, 8))

    def body(b_ref, o_ref):
        acc = b_ref[0].astype(F32)
        for s in range(1, ns):
            acc = acc + b_ref[s].astype(F32)
        o_ref[...] = acc.astype(o_ref.dtype)

    out = pl.pallas_call(
        body, grid=(rows // tr,), in_specs=[pl.BlockSpec((ns, tr, cols), lambda i: (0, i, 0))],
        out_specs=pl.BlockSpec((tr, cols), lambda i: (i, 0)), out_shape=SDS((rows, cols), out_dtype),
        compiler_params=_cparams("parallel"), name=name)(buf.reshape(ns, rows, cols))
    return out.reshape(buf.shape[1:])


def _adamw(w, g, m, v, name):
    rows, cols = _as_rows(w.shape)
    tr = _pick(rows, (512, 256, 128, 64, 32, 16, 8))
    c1 = 1.0 / (1.0 - ADAM_B1 ** ADAM_STEP)
    c2 = 1.0 / (1.0 - ADAM_B2 ** ADAM_STEP)

    def body(w_ref, g_ref, m_ref, v_ref, d_ref, nm_ref, nv_ref):
        gv = g_ref[...]
        nm = ADAM_B1 * m_ref[...] + (1.0 - ADAM_B1) * gv
        nv = ADAM_B2 * v_ref[...] + (1.0 - ADAM_B2) * (gv * gv)
        nm_ref[...] = nm
        nv_ref[...] = nv
        d_ref[...] = -ADAM_LR * ((nm * c1) / (jnp.sqrt(nv * c2) + ADAM_EPS) + ADAM_WD * w_ref[...])

    blk = pl.BlockSpec((tr, cols), lambda i: (i, 0))
    outs = pl.pallas_call(
        body, grid=(rows // tr,), in_specs=[blk] * 4, out_specs=[blk] * 3, out_shape=[SDS((rows, cols), F32)] * 3,
        compiler_params=_cparams("parallel"), name=name)(*[t.reshape(rows, cols) for t in (w, g, m, v)])
    return tuple(o.reshape(w.shape) for o in outs)


def _coords():
    return lax.axis_index("x"), lax.axis_index("y"), lax.axis_index("c")


def _other_chips(x, y):
    return [(1 - x, y), (x, 1 - y), (1 - x, 1 - y)]


def _gather_chips(shards, name):
    nt = len(shards)
    half = [s.shape[0] // 2 for s in shards]

    def body(*refs):
        s_refs, o_refs = refs[:nt], refs[nt:2 * nt]
        lsem, ssem1, rsem1, ssem2, rsem2 = refs[2 * nt:]
        x, y, c = _coords()
        k = 2 * x + y
        chips = _other_chips(x, y)
        sends, local = [], []
        for t in range(nt):
            s_ref, o_ref = s_refs[t], o_refs[t]
            mine = pl.ds(c * half[t], half[t])
            cp = pltpu.make_async_copy(s_ref, o_ref.at[:, k], lsem.at[t])
            cp.start()
            local.append(cp)
            for r, (px, py) in enumerate(chips):
                cp = pltpu.make_async_remote_copy(
                    src_ref=s_ref.at[mine], dst_ref=o_ref.at[mine, k], send_sem=ssem1.at[3 * t + r],
                    recv_sem=rsem1.at[3 * t + r], device_id=(px, py, c), device_id_type=MESH)
                cp.start()
                sends.append(cp)
        for t in range(nt):
            s_ref, o_ref = s_refs[t], o_refs[t]
            mine = pl.ds(c * half[t], half[t])
            for r, (px, py) in enumerate(chips):
                kk = 2 * px + py
                pltpu.make_async_remote_copy(
                    src_ref=s_ref.at[mine], dst_ref=o_ref.at[mine, kk], send_sem=ssem1.at[3 * t + r],
                    recv_sem=rsem1.at[3 * t + r], device_id=(px, py, c), device_id_type=MESH).wait_recv()
                cp = pltpu.make_async_remote_copy(
                    src_ref=o_ref.at[mine, kk], dst_ref=o_ref.at[mine, kk], send_sem=ssem2.at[3 * t + r],
                    recv_sem=rsem2.at[3 * t + r], device_id=(x, y, 1 - c), device_id_type=MESH)
                cp.start()
                sends.append(cp)
        for t in range(nt):
            o_ref = o_refs[t]
            theirs = pl.ds((1 - c) * half[t], half[t])
            for r, (px, py) in enumerate(chips):
                kk = 2 * px + py
                pltpu.make_async_remote_copy(
                    src_ref=o_ref.at[theirs, kk], dst_ref=o_ref.at[theirs, kk], send_sem=ssem2.at[3 * t + r],
                    recv_sem=rsem2.at[3 * t + r], device_id=(x, y, 1 - c), device_id_type=MESH).wait_recv()
        for cp in sends:
            cp.wait_send()
        for cp in local:
            cp.wait()

    sem = pltpu.SemaphoreType.DMA
    outs = pl.pallas_call(
        body, in_specs=[ANY] * nt, out_specs=[ANY] * nt,
        out_shape=[SDS((s.shape[0], N_CHIPS) + s.shape[1:], s.dtype) for s in shards],
        scratch_shapes=[sem((nt,)), sem((3 * nt,)), sem((3 * nt,)), sem((3 * nt,)), sem((3 * nt,))],
        name=name)(*shards)
    return list(outs)


def _pair_split(grads, name):
    nt = len(grads)
    half = [g.shape[0] // 2 for g in grads]

    def body(*refs):
        g_refs, o_refs = refs[:nt], refs[nt:2 * nt]
        lsem, ssem, rsem = refs[2 * nt:]
        x, y, c = _coords()
        cps = []
        for t in range(nt):
            mine = pl.ds(c * half[t], half[t])
            theirs = pl.ds((1 - c) * half[t], half[t])
            loc = pltpu.make_async_copy(g_refs[t].at[mine], o_refs[t].at[c], lsem.at[t])
            loc.start()
            cp = pltpu.make_async_remote_copy(
                src_ref=g_refs[t].at[theirs], dst_ref=o_refs[t].at[c], send_sem=ssem.at[t], recv_sem=rsem.at[t],
                device_id=(x, y, 1 - c), device_id_type=MESH)
            cp.start()
            cps.append((loc, cp))
        for t in range(nt):
            theirs = pl.ds((1 - c) * half[t], half[t])
            pltpu.make_async_remote_copy(
                src_ref=g_refs[t].at[theirs], dst_ref=o_refs[t].at[1 - c], send_sem=ssem.at[t], recv_sem=rsem.at[t],
                device_id=(x, y, 1 - c), device_id_type=MESH).wait_recv()
        for loc, cp in cps:
            cp.wait_send()
            loc.wait()

    sem = pltpu.SemaphoreType.DMA
    outs = pl.pallas_call(
        body, in_specs=[ANY] * nt, out_specs=[ANY] * nt,
        out_shape=[SDS((2, g.shape[0] // 2) + g.shape[1:], g.dtype) for g in grads],
        scratch_shapes=[sem((nt,)), sem((nt,)), sem((nt,))],
        name=name)(*grads)
    return list(outs)


def _chip_scatter(parts, name):
    nt = len(parts)

    def body(*refs):
        p_refs, o_refs = refs[:nt], refs[nt:2 * nt]
        lsem, ssem, rsem = refs[2 * nt:]
        x, y, c = _coords()
        k = 2 * x + y
        chips = _other_chips(x, y)
        cps = []
        for t in range(nt):
            loc = pltpu.make_async_copy(p_refs[t].at[:, k], o_refs[t].at[k], lsem.at[t])
            loc.start()
            cps.append(loc)
            for r, (px, py) in enumerate(chips):
                cp = pltpu.make_async_remote_copy(
                    src_ref=p_refs[t].at[:, 2 * px + py], dst_ref=o_refs[t].at[k], send_sem=ssem.at[3 * t + r],
                    recv_sem=rsem.at[3 * t + r], device_id=(px, py, c), device_id_type=MESH)
                cp.start()
                cps.append(cp)
        for t in range(nt):
            for r, (px, py) in enumerate(chips):
                pltpu.make_async_remote_copy(
                    src_ref=p_refs[t].at[:, k], dst_ref=o_refs[t].at[2 * px + py], send_sem=ssem.at[3 * t + r],
                    recv_sem=rsem.at[3 * t + r], device_id=(px, py, c), device_id_type=MESH).wait_recv()
        for t in range(nt):
            cps[4 * t].wait()
            for r in range(3):
                cps[4 * t + 1 + r].wait_send()

    sem = pltpu.SemaphoreType.DMA
    outs = pl.pallas_call(
        body, in_specs=[ANY] * nt, out_specs=[ANY] * nt,
        out_shape=[SDS((N_CHIPS, p.shape[0]) + p.shape[2:], p.dtype) for p in parts],
        scratch_shapes=[sem((nt,)), sem((3 * nt,)), sem((3 * nt,))],
        name=name)(*parts)
    return list(outs)


def _pair_join(halves, name):
    nt = len(halves)

    def body(*refs):
        h_refs, o_refs = refs[:nt], refs[nt:2 * nt]
        lsem, ssem, rsem = refs[2 * nt:]
        x, y, c = _coords()
        cps = []
        for t in range(nt):
            hn = h_refs[t].shape[0]
            mine = pl.ds(c * hn, hn)
            loc = pltpu.make_async_copy(h_refs[t], o_refs[t].at[mine], lsem.at[t])
            loc.start()
            cp = pltpu.make_async_remote_copy(
                src_ref=h_refs[t], dst_ref=o_refs[t].at[mine], send_sem=ssem.at[t], recv_sem=rsem.at[t],
                device_id=(x, y, 1 - c), device_id_type=MESH)
            cp.start()
            cps.append((loc, cp))
        for t in range(nt):
            hn = h_refs[t].shape[0]
            pltpu.make_async_remote_copy(
                src_ref=h_refs[t], dst_ref=o_refs[t].at[pl.ds((1 - c) * hn, hn)], send_sem=ssem.at[t],
                recv_sem=rsem.at[t], device_id=(x, y, 1 - c), device_id_type=MESH).wait_recv()
        for loc, cp in cps:
            cp.wait_send()
            loc.wait()

    sem = pltpu.SemaphoreType.DMA
    outs = pl.pallas_call(
        body, in_specs=[ANY] * nt, out_specs=[ANY] * nt,
        out_shape=[SDS((2 * h.shape[0],) + h.shape[1:], h.dtype) for h in halves],
        scratch_shapes=[sem((nt,)), sem((nt,)), sem((nt,))],
        name=name)(*halves)
    return list(outs)


def _gather_devices(vals, name):
    nt = len(vals)
    flips = [(a, b, e) for a in (0, 1) for b in (0, 1) for e in (0, 1)][1:]

    def body(*refs):
        v_refs, o_refs = refs[:nt], refs[nt:2 * nt]
        lsem, ssem, rsem = refs[2 * nt:]
        x, y, c = _coords()
        me = 4 * x + 2 * y + c
        peers = [((1 - x) if a else x, (1 - y) if b else y, (1 - c) if e else c) for a, b, e in flips]
        cps = []
        for t in range(nt):
            loc = pltpu.make_async_copy(v_refs[t], o_refs[t].at[me], lsem.at[t])
            loc.start()
            cps.append(loc)
            for r, peer in enumerate(peers):
                cp = pltpu.make_async_remote_copy(
                    src_ref=v_refs[t], dst_ref=o_refs[t].at[me], send_sem=ssem.at[7 * t + r],
                    recv_sem=rsem.at[7 * t + r], device_id=peer, device_id_type=MESH)
                cp.start()
                cps.append(cp)
        for t in range(nt):
            for r, (px, py, pc) in enumerate(peers):
                pltpu.make_async_remote_copy(
                    src_ref=v_refs[t], dst_ref=o_refs[t].at[4 * px + 2 * py + pc], send_sem=ssem.at[7 * t + r],
                    recv_sem=rsem.at[7 * t + r], device_id=(px, py, pc), device_id_type=MESH).wait_recv()
        for t in range(nt):
            cps[8 * t].wait()
            for r in range(7):
                cps[8 * t + 1 + r].wait_send()

    sem = pltpu.SemaphoreType.DMA
    outs = pl.pallas_call(
        body, in_specs=[ANY] * nt, out_specs=[ANY] * nt,
        out_shape=[SDS((N_DEV,) + v.shape, v.dtype) for v in vals],
        scratch_shapes=[sem((nt,)), sem((7 * nt,)), sem((7 * nt,))],
        name=name)(*vals)
    return list(outs)


def _reduce_to_shards(grads, tag):
    split = _pair_split(grads, f"{tag}_pair_split")
    pair = [_sum_slots(b, b.dtype, f"{tag}_pair_sum{t}") for t, b in enumerate(split)]
    scat = _chip_scatter(pair, f"{tag}_chip_scatter")
    half = [_sum_slots(b, F32, f"{tag}_chip_sum{t}") for t, b in enumerate(scat)]
    return _pair_join(half, f"{tag}_pair_join")


WEIGHT_NAMES = ("c_ctx", "w_ada", "b_ada", "w_in", "w_pool", "pool_scale", "ssm_a_re", "ssm_a_im", "ssm_log_dt",
                "ssm_b_re", "ssm_b_im", "ssm_c_re", "ssm_c_im", "ssm_d", "w_glu", "w_out", "g_pre_mix", "g_post_mix",
                "g_pre_ffn", "g_post_ffn", "w_up", "w_conv", "w_down")


def _block_diag_in(bb, ng):
    nl, nd, npart, h, gp = bb.shape
    p = gp // ng
    w = jnp.einsum("ldqhgp,kg->lkhdqgp", bb.reshape(nl, nd, npart, h, ng, p), jnp.eye(ng, dtype=bb.dtype))
    return w.reshape(nl, ng * h, nd * npart * gp)


def _block_diag_in_grad(dw, nd, npart, h, ng, p):
    return jnp.einsum("khdqkp->dqhkp", dw.reshape(ng, h, nd, npart, ng, p)).reshape(nd, npart, h, ng * p)


def _block_diag_out(cs, ng):
    nl, nd, npart, _, h, p = cs.shape
    w = jnp.einsum("ldqghp,kg->ldqkpgh", cs, jnp.eye(ng, dtype=cs.dtype))
    return w.reshape(nl, nd * npart * ng * p, ng * h)


def _block_diag_out_grad(dw, nd, npart, h, ng, p):
    return jnp.einsum("dqkpkh->dqkhp", dw.reshape(nd, npart, ng, p, ng, h))


def kernel(x, c, ctx, c_ctx, w_ada, b_ada, w_in, w_pool, pool_scale, ssm_a_re, ssm_a_im, ssm_log_dt, ssm_b_re, ssm_b_im, ssm_c_re, ssm_c_im, ssm_d, w_glu, w_out, g_pre_mix, g_post_mix, g_pre_ffn, g_post_ffn, w_up, w_conv, w_down, loss_target, m_c_ctx, m_w_ada, m_b_ada, m_w_in, m_w_pool, m_pool_scale, m_ssm_a_re, m_ssm_a_im, m_ssm_log_dt, m_ssm_b_re, m_ssm_b_im, m_ssm_c_re, m_ssm_c_im, m_ssm_d, m_w_glu, m_w_out, m_g_pre_mix, m_g_post_mix, m_g_pre_ffn, m_g_post_ffn, m_w_up, m_w_conv, m_w_down, v_c_ctx, v_w_ada, v_b_ada, v_w_in, v_w_pool, v_pool_scale, v_ssm_a_re, v_ssm_a_im, v_ssm_log_dt, v_ssm_b_re, v_ssm_b_im, v_ssm_c_re, v_ssm_c_im, v_ssm_d, v_w_glu, v_w_out, v_g_pre_mix, v_g_post_mix, v_g_pre_ffn, v_g_post_ffn, v_w_up, v_w_conv, v_w_down):
    weights = dict(zip(WEIGHT_NAMES, (c_ctx, w_ada, b_ada, w_in, w_pool, pool_scale, ssm_a_re, ssm_a_im, ssm_log_dt,
                                      ssm_b_re, ssm_b_im, ssm_c_re, ssm_c_im, ssm_d, w_glu, w_out, g_pre_mix, g_post_mix,
                                      g_pre_ffn, g_post_ffn, w_up, w_conv, w_down)))
    mom1 = dict(zip(WEIGHT_NAMES, (m_c_ctx, m_w_ada, m_b_ada, m_w_in, m_w_pool, m_pool_scale, m_ssm_a_re, m_ssm_a_im,
                                   m_ssm_log_dt, m_ssm_b_re, m_ssm_b_im, m_ssm_c_re, m_ssm_c_im, m_ssm_d, m_w_glu, m_w_out,
                                   m_g_pre_mix, m_g_post_mix, m_g_pre_ffn, m_g_post_ffn, m_w_up, m_w_conv, m_w_down)))
    mom2 = dict(zip(WEIGHT_NAMES, (v_c_ctx, v_w_ada, v_b_ada, v_w_in, v_w_pool, v_pool_scale, v_ssm_a_re, v_ssm_a_im,
                                   v_ssm_log_dt, v_ssm_b_re, v_ssm_b_im, v_ssm_c_re, v_ssm_c_im, v_ssm_d, v_w_glu, v_w_out,
                                   v_g_pre_mix, v_g_post_mix, v_g_pre_ffn, v_g_post_ffn, v_w_up, v_w_conv, v_w_down)))

    xi, yi, ci = lax.axis_index("x"), lax.axis_index("y"), lax.axis_index("c")
    chip = 2 * xi + yi
    dev = 4 * xi + 2 * yi + ci
    nl = w_in.shape[0]
    n_lat, d = x.shape[1], x.shape[2]
    n_ctx = ctx.shape[1]
    n = n_ctx + n_lat
    _, ndir, ng, nstate, nh = ssm_b_re.shape
    gp = ng * nstate
    sw = ng * nh
    n_pool_groups, pool_group = w_pool.shape[1], w_pool.shape[3]
    pw = n_pool_groups * pool_group
    assert pw + sw == d and pw % sw == 0 and len(POOL_WINDOWS) == n_pool_groups and n_lat % GRID_W == 0
    dff2 = w_up.shape[2] * N_CHIPS
    ada_w = w_ada.shape[2] * N_CHIPS
    ada_cols = w_ada.shape[2]
    s_rows = gp // LANES

    c_pad = jnp.concatenate([c, jnp.zeros((SUBLANES - 1, d), F32)], axis=0)
    c_all = _gather_devices([c_pad], "gather_cond")[0][:, 0, :]
    cond = jnp.concatenate([c_all, c_ctx[None, :], jnp.zeros((16 - N_DEV - 1, d), F32)], axis=0)
    b_shard = lax.dynamic_slice_in_dim(b_ada, chip * ada_cols, ada_cols, axis=1)[:, None, :]
    mod_shard = _ada_fwd(cond, w_ada, b_shard, "ada_fwd")
    mod_all = _gather_chips([mod_shard], "gather_mods")[0]
    mod_all = jnp.transpose(mod_all, (0, 2, 1, 3)).reshape(nl, 16, ada_w)
    mod_lat = lax.dynamic_index_in_dim(mod_all, dev, axis=1, keepdims=False).reshape(nl, 6, d)
    mod_ctx = mod_all[:, N_DEV].reshape(nl, 6, d)
    mods = jnp.concatenate([jnp.stack([mod_ctx, mod_lat], axis=1), jnp.zeros((nl, 2, 2, d), F32)], axis=2)

    shards = [w_in.astype(COMM_DTYPE), w_pool.reshape(nl, pw // N_CHIPS, pool_group).astype(COMM_DTYPE),
              w_glu.astype(COMM_DTYPE), w_out.astype(COMM_DTYPE), w_up.astype(COMM_DTYPE), w_down.astype(COMM_DTYPE),
              w_conv.reshape(nl, 9, dff2 // N_CHIPS)]
    g_in, g_pool, g_glu, g_out, g_up, g_down, g_conv = _gather_chips(shards, "gather_weights")
    wi = g_in.reshape(nl, d, d)
    wp = jnp.transpose(g_pool.reshape(nl, N_CHIPS, n_pool_groups, pool_group // N_CHIPS, pool_group),
                       (0, 2, 1, 3, 4)).reshape(nl, n_pool_groups, pool_group, pool_group)
    wg = g_glu.reshape(nl, sw, sw)
    wo = g_out.reshape(nl, d, d)
    wu = jnp.transpose(g_up, (0, 2, 1, 3)).reshape(nl, d, dff2)
    wd = g_down.reshape(nl, dff2 // 2, d)
    wk = jnp.transpose(g_conv, (0, 2, 1, 3)).reshape(nl, 9, dff2)

    rows = nl * ndir
    a_re2 = ssm_a_re.reshape(rows, gp)
    a_im2 = ssm_a_im.reshape(rows, gp)
    logdt2 = jnp.repeat(ssm_log_dt.reshape(rows, ng), nstate, axis=1)
    b_re2 = jnp.transpose(ssm_b_re.reshape(rows, gp, nh), (0, 2, 1))
    b_im2 = jnp.transpose(ssm_b_im.reshape(rows, gp, nh), (0, 2, 1))
    lam_re, lam_im, bb_re, bb_im = _disc_fwd(a_re2, a_im2, logdt2, b_re2, b_im2, "s5_discretise")
    lam = jnp.stack([lam_re.reshape(nl, ndir, s_rows, LANES), lam_im.reshape(nl, ndir, s_rows, LANES)], axis=2)
    lam = lam.reshape(nl, 2 * ndir, s_rows, LANES)
    bbs = jnp.stack([bb_re.reshape(nl, ndir, nh, gp), bb_im.reshape(nl, ndir, nh, gp)], axis=2)
    w_b = _block_diag_in(bbs, ng).astype(MXU_DTYPE)
    cs = jnp.stack([ssm_c_re, -ssm_c_im], axis=2)
    w_c = _block_diag_out(cs, ng).astype(MXU_DTYPE)

    def row(v, l):
        return v[l:l + 1]

    xc = jnp.concatenate([ctx[0], x[0]], axis=0)
    saved = []
    for l in range(nl):
        t = f"l{l}"
        md = mods[l]
        h = _norm_mod_fwd(xc, row(g_pre_mix, l), md, 0, 1, n_ctx, f"{t}_pre_mix")
        u = _mm(h, wi, "nn", F32, f"{t}_in_proj", b_idx=l)
        p = _pool(u, pw, pool_group, n_ctx, False, MXU_DTYPE, f"{t}_pool")
        ypool = _pool_proj_fwd(p, wp, l, row(pool_scale, l), f"{t}_pool_proj")
        bu = _mm(u, w_b, "nn", F32, f"{t}_s5_in", b_idx=l, a_cols=(pw, sw))
        bu0 = bu[:, :2 * gp].reshape(n, 2 * s_rows, LANES)
        bu1 = bu[:, 2 * gp:].reshape(n, 2 * s_rows, LANES)
        h0, h1 = _scan_fwd(bu0, bu1, lam[l], n_ctx, f"{t}_scan")
        hcat = jnp.concatenate([h0.reshape(n, 2 * gp), h1.reshape(n, 2 * gp)], axis=1).astype(MXU_DTYPE)
        y = _mm(hcat, w_c, "nn", F32, f"{t}_s5_out", b_idx=l)
        s_out = _ssm_head_fwd(y, u, row(ssm_d, l), wg, l, f"{t}_s5_head")
        cat = jnp.concatenate([ypool, s_out], axis=1)
        mix = _mm(cat, wo, "nn", F32, f"{t}_out_proj", b_idx=l)
        x_mid = _gate_res_fwd(xc, mix, row(g_post_mix, l), md, 2, n_ctx, f"{t}_post_mix")
        h2 = _norm_mod_fwd(x_mid, row(g_pre_ffn, l), md, 3, 4, n_ctx, f"{t}_pre_ffn")
        z = _mm(h2, wu, "nn", F32, f"{t}_up", b_idx=l)
        act = _conv_glu_fwd(z, wk[l], n_ctx, f"{t}_conv_glu")
        f = _mm(act, wd, "nn", F32, f"{t}_down", b_idx=l)
        x_out = _gate_res_fwd(x_mid, f, row(g_post_ffn, l), md, 5, n_ctx, f"{t}_post_ffn")
        saved.append(dict(xc=xc, h=h, u=u, p=p, h0=h0, h1=h1, hcat=hcat, y=y, cat=cat, mix=mix, x_mid=x_mid, h2=h2, z=z, f=f))
        xc = x_out

    dx, loss_tile = _loss_grad(xc, loss_target[0], n_ctx, "loss")
    loss = lax.psum(loss_tile[0, 0], ("x", "y", "c"))

    big = {k: [None] * nl for k in ("w_in", "w_pool", "w_glu", "w_out", "w_up", "w_down")}
    small = {k: [None] * nl for k in ("pool_scale", "ssm_d", "g_pre_mix", "g_post_mix", "g_pre_ffn", "g_post_ffn",
                                      "lam", "bb", "cs", "w_conv")}
    dmods = [None] * nl
    for l in reversed(range(nl)):
        t = f"l{l}b"
        md = mods[l]
        sv = saved[l]
        df, dgate_ffn, small["g_post_ffn"][l] = _gate_res_bwd(dx, sv["f"], row(g_post_ffn, l), md, 5, n_ctx, f"{t}_post_ffn")
        dact = _mm(df, wd, "nt", F32, f"{t}_down_dx", b_idx=l)
        dzv, dzg, dkv, dkg, act = _conv_glu_bwd(sv["z"], dact, wk[l], n_ctx, f"{t}_conv_glu")
        dz = jnp.concatenate([dzv, dzg], axis=1)
        small["w_conv"][l] = jnp.concatenate([dkv, dkg], axis=1)
        big["w_down"][l] = _mm(act, df, "tn", COMM_DTYPE, f"{t}_down_dw")
        big["w_up"][l] = _mm(sv["h2"], dz, "tn", COMM_DTYPE, f"{t}_up_dw")
        dh2 = _mm(dz, wu, "nt", F32, f"{t}_up_dx", b_idx=l)
        dx, dss_ffn, small["g_pre_ffn"][l] = _norm_mod_bwd(dh2, sv["x_mid"], row(g_pre_ffn, l), md, 3, 4, dx, n_ctx,
                                                           f"{t}_pre_ffn")
        dmix, dgate_mix, small["g_post_mix"][l] = _gate_res_bwd(dx, sv["mix"], row(g_post_mix, l), md, 2, n_ctx,
                                                                f"{t}_post_mix")
        dcat = _mm(dmix, wo, "nt", F32, f"{t}_out_dx", b_idx=l)
        big["w_out"][l] = _mm(sv["cat"], dmix, "tn", COMM_DTYPE, f"{t}_out_dw")
        dp, small["pool_scale"][l], big["w_pool"][l] = _pool_proj_bwd(sv["p"], dcat, wp, l, row(pool_scale, l),
                                                                      f"{t}_pool_proj")
        du_pool = _pool(dp, pw, pool_group, n_ctx, True, F32, f"{t}_pool")
        dy, du_dir, gact, dq, small["ssm_d"][l] = _ssm_head_bwd(dcat, sv["y"], sv["u"], row(ssm_d, l), wg, l, f"{t}_s5_head")
        big["w_glu"][l] = _mm(gact, dq, "tn", COMM_DTYPE, f"{t}_glu_dw")
        dhcat = _mm(dy, w_c, "nt", F32, f"{t}_s5_out_dx", b_idx=l)
        dwc = _mm(sv["hcat"], dy, "tn", F32, f"{t}_s5_out_dw")
        small["cs"][l] = _block_diag_out_grad(dwc, ndir, 2, nh, ng, nstate)
        dh0 = dhcat[:, :2 * gp].reshape(n, 2 * s_rows, LANES)
        dh1 = dhcat[:, 2 * gp:].reshape(n, 2 * s_rows, LANES)
        a0, a1, small["lam"][l] = _scan_bwd(dh0, dh1, sv["h0"], sv["h1"], lam[l], n_ctx, f"{t}_scan")
        acat = jnp.concatenate([a0.reshape(n, 2 * gp), a1.reshape(n, 2 * gp)], axis=1).astype(MXU_DTYPE)
        du_proj = _mm(acat, w_b, "nt", F32, f"{t}_s5_in_dx", b_idx=l)
        dwb = _mm(sv["u"], acat, "tn", F32, f"{t}_s5_in_dw", a_cols=(pw, sw))
        small["bb"][l] = _block_diag_in_grad(dwb, ndir, 2, nh, ng, nstate)
        du = _assemble_du(du_pool, du_dir, du_proj, f"{t}_du")
        dh = _mm(du, wi, "nt", F32, f"{t}_in_dx", b_idx=l)
        big["w_in"][l] = _mm(sv["h"], du, "tn", COMM_DTYPE, f"{t}_in_dw")
        dx, dss_mix, small["g_pre_mix"][l] = _norm_mod_bwd(dh, sv["xc"], row(g_pre_mix, l), md, 0, 1, dx, n_ctx,
                                                           f"{t}_pre_mix")
        dmods[l] = jnp.concatenate([dss_mix, dgate_mix, dss_ffn, dgate_ffn], axis=1).reshape(2, ada_w)

    grad_x = dx[n_ctx:][None]

    dmod_all = _gather_devices([jnp.stack(dmods, axis=0)], "gather_dmods")[0]
    ada_rows, db_ada = _ada_rows(dmod_all, "ada_rows")
    rows_shard = lax.dynamic_slice_in_dim(ada_rows, chip * ada_cols, ada_cols, axis=2)
    dcond_part = _ada_dcond(rows_shard, w_ada, "ada_dcond")
    dcond_parts = _gather_devices([dcond_part], "gather_dcond")[0][0::2]
    grads = {"w_ada": _ada_dw(cond, rows_shard, "ada_dw"), "b_ada": db_ada[:, 0, :],
             "c_ctx": _ada_dctx(dcond_parts, c_ctx[None, :], N_DEV, "ada_dctx")[0]}

    stacked = {k: jnp.stack(v, axis=0) for k, v in big.items()}
    parts = [stacked["w_in"].reshape(nl, N_CHIPS, d // N_CHIPS, d),
             jnp.transpose(stacked["w_pool"].astype(COMM_DTYPE).reshape(nl, n_pool_groups, N_CHIPS, pool_group // N_CHIPS,
                                                                      pool_group), (0, 2, 1, 3, 4))
             .reshape(nl, N_CHIPS, pw // N_CHIPS, pool_group),
             stacked["w_glu"].reshape(nl, N_CHIPS, sw // N_CHIPS, sw),
             stacked["w_out"].reshape(nl, N_CHIPS, d // N_CHIPS, d),
             jnp.transpose(stacked["w_up"].reshape(nl, d, N_CHIPS, dff2 // N_CHIPS), (0, 2, 1, 3)),
             stacked["w_down"].reshape(nl, N_CHIPS, dff2 // 2 // N_CHIPS, d)]
    r_in, r_pool, r_glu, r_out, r_up, r_down = _reduce_to_shards(parts, "big")
    grads.update(w_in=r_in, w_pool=r_pool.reshape(w_pool.shape), w_glu=r_glu, w_out=r_out, w_up=r_up, w_down=r_down)

    order = ("pool_scale", "ssm_d", "g_pre_mix", "g_post_mix", "g_pre_ffn", "g_post_ffn", "lam", "bb", "cs", "w_conv")
    pieces = [jnp.stack(small[k], axis=0) for k in order]
    flat = jnp.concatenate([q.reshape(-1) for q in pieces])
    unit = nl * N_CHIPS * SUBLANES * 1024
    padded = -(-flat.shape[0] // unit) * unit
    flat = jnp.concatenate([flat, jnp.zeros((padded - flat.shape[0],), F32)])
    vec = flat.reshape(nl, N_CHIPS, padded // (nl * N_CHIPS * 1024), 1024)
    vec = _gather_chips(_reduce_to_shards([vec], "small"), "gather_small")[0].reshape(-1)
    red, pos = {}, 0
    for k, q in zip(order, pieces):
        red[k] = vec[pos:pos + q.size].reshape(q.shape)
        pos += q.size
    for k in ("pool_scale", "ssm_d", "g_pre_mix", "g_post_mix", "g_pre_ffn", "g_post_ffn"):
        grads[k] = red[k][:, 0, :]
    dlam = red["lam"].reshape(nl, ndir, 2, gp)
    dbb = red["bb"].reshape(nl, ndir, 2, nh, gp)
    d_are, d_aim, d_ldt, d_bre, d_bim = _disc_bwd(
        a_re2, a_im2, logdt2, b_re2, b_im2, dlam[:, :, 0].reshape(rows, gp), dlam[:, :, 1].reshape(rows, gp),
        dbb[:, :, 0].reshape(rows, nh, gp), dbb[:, :, 1].reshape(rows, nh, gp), nstate, "s5_discretise_bwd")
    grads["ssm_a_re"] = d_are.reshape(ssm_a_re.shape)
    grads["ssm_a_im"] = d_aim.reshape(ssm_a_im.shape)
    grads["ssm_log_dt"] = d_ldt[:, :ng].reshape(ssm_log_dt.shape)
    grads["ssm_b_re"] = jnp.transpose(d_bre, (0, 2, 1)).reshape(ssm_b_re.shape)
    grads["ssm_b_im"] = jnp.transpose(d_bim, (0, 2, 1)).reshape(ssm_b_im.shape)
    grads["ssm_c_re"] = red["cs"][:, :, 0]
    grads["ssm_c_im"] = -red["cs"][:, :, 1]
    conv_cols = dff2 // N_CHIPS
    grads["w_conv"] = lax.dynamic_slice_in_dim(red["w_conv"], chip * conv_cols, conv_cols, axis=2).reshape(w_conv.shape)

    delta, new_m, new_v = {}, {}, {}
    for k in WEIGHT_NAMES:
        delta[k], new_m[k], new_v[k] = _adamw(weights[k], grads[k], mom1[k], mom2[k], f"adamw_{k}")
    return (loss, grad_x, *[grads[k] for k in WEIGHT_NAMES], *[delta[k] for k in WEIGHT_NAMES],
            *[new_m[k] for k in WEIGHT_NAMES], *[new_v[k] for k in WEIGHT_NAMES])
```

```python
import math

import jax
import jax.numpy as jnp
from jax import lax
from jax.experimental import pallas as pl
from jax.experimental.pallas import tpu as pltpu

F32 = jnp.float32
MXU_DTYPE = jnp.bfloat16
COMM_DTYPE = jnp.bfloat16
HIGHEST = lax.Precision.HIGHEST
VMEM_LIMIT_BYTES = 48 * 1024 * 1024
LANES = 128
SUBLANES = 8
N_CHIPS = 4
N_DEV = 8

EPS = 1e-6
GRID_W = 64
POOL_WINDOWS = (2, 4, 8, 16)
ADAM_LR = 0.001
ADAM_B1 = 0.9
ADAM_B2 = 0.999
ADAM_EPS = 1e-08
ADAM_WD = 0.01
ADAM_STEP = 10
GELU_C0 = math.sqrt(2.0 / math.pi)
GELU_C1 = 0.044715

SDS = jax.ShapeDtypeStruct
ANY = pl.BlockSpec(memory_space=pl.ANY)
MESH = pl.DeviceIdType.MESH


def _cparams(*sem):
    return pltpu.CompilerParams(dimension_semantics=sem if sem else None, vmem_limit_bytes=VMEM_LIMIT_BYTES)


def _pick(n, cands):
    for cand in cands:
        if n % cand == 0:
            return cand
    return n


def _row_tile(n_ctx, n):
    return math.gcd(math.gcd(n_ctx, n - n_ctx), 256)


_DIMS = {"nn": (((1,), (0,)), ((), ())), "nt": (((1,), (1,)), ((), ())), "tn": (((0,), (0,)), ((), ()))}
_TM = (1088, 1024, 512, 384, 256, 128, 64, 32, 16, 8)
_TN = (1024, 1408, 512, 384, 256, 128)
_TK = (1024, 1088, 512, 1408, 384, 256, 128, 64, 32, 16, 8)


def _chunk_of(idx, per, chunks):
    out = 0
    for q in range(1, chunks):
        out = out + (idx >= q * per).astype(jnp.int32)
    return out


def _within(idx, per, chunks):
    return idx - per * _chunk_of(idx, per, chunks)


def _mm(a, b, mode, out_dtype, name, a_idx=None, b_idx=None, a_cols=None, b_chunks=None, out_chunks=None):
    a2, b2 = a.shape[-2:], b.shape[-2:]
    if b_chunks is not None:
        assert mode in ("nn", "nt") and b.shape[-3] == b_chunks
        b2 = (b2[0], b2[1] * b_chunks)
    alast = a2[1] if a_cols is None else a_cols[1]
    if mode == "nn":
        m, k, n = a2[0], alast, b2[1]
        assert b2[0] == k
    elif mode == "nt":
        m, k, n = a2[0], alast, b2[0]
        assert b2[1] == k
    else:
        k, m, n = a2[0], alast, b2[1]
        assert b2[0] == k
    n_unit = n // (out_chunks or 1) // (b_chunks if b_chunks and mode == "nn" else 1)
    k_unit = k // (b_chunks if b_chunks and mode == "nt" else 1)
    tm, tn, tk = _pick(m, _TM), _pick(n_unit, _TN), _pick(k_unit, _TK)
    nk = k // tk
    a_lane_tile = tm if mode == "tn" else tk
    off = 0
    if a_cols is not None:
        assert a_cols[0] % a_lane_tile == 0
        off = a_cols[0] // a_lane_tile

    def body(a_ref, b_ref, o_ref, acc_ref):
        kk = pl.program_id(2)

        @pl.when(kk == 0)
        def _():
            acc_ref[...] = jnp.zeros(acc_ref.shape, F32)

        acc_ref[...] += lax.dot_general(a_ref[...].astype(MXU_DTYPE), b_ref[...].astype(MXU_DTYPE), _DIMS[mode],
                                        preferred_element_type=F32)

        @pl.when(kk == nk - 1)
        def _():
            o_ref[...] = acc_ref[...].astype(o_ref.dtype)

    if mode == "tn":
        a_blk, a_map = (tk, tm), (lambda i, j, kk: (kk, i + off))
    else:
        a_blk, a_map = (tm, tk), (lambda i, j, kk: (i, kk + off))
    if mode == "nt":
        b_blk, b_map = (tn, tk), (lambda i, j, kk: (j, kk))
        if b_chunks is not None:
            per = k // b_chunks // tk
            b_blk, b_map = (None, tn, tk), (lambda i, j, kk: (_chunk_of(kk, per, b_chunks), j, _within(kk, per, b_chunks)))
    else:
        b_blk, b_map = (tk, tn), (lambda i, j, kk: (kk, j))
        if b_chunks is not None:
            per = n // b_chunks // tn
            b_blk, b_map = (None, tk, tn), (lambda i, j, kk: (_chunk_of(j, per, b_chunks), kk, _within(j, per, b_chunks)))
    if a_idx is not None:
        a_blk, a_map0 = (None,) + a_blk, a_map
        a_map = lambda i, j, kk: (a_idx,) + a_map0(i, j, kk)
    if b_idx is not None:
        b_blk, b_map0 = (None,) + b_blk, b_map
        b_map = lambda i, j, kk: (b_idx,) + b_map0(i, j, kk)
    o_blk, o_map, o_shape = (tm, tn), (lambda i, j, kk: (i, j)), (m, n)
    if out_chunks is not None:
        oper = n // out_chunks // tn
        o_map = lambda i, j, kk: (_chunk_of(j, oper, out_chunks), i, _within(j, oper, out_chunks))
        o_blk, o_shape = (None, tm, tn), (out_chunks, m, n // out_chunks)
    return pl.pallas_call(
        body, grid=(m // tm, n // tn, nk),
        in_specs=[pl.BlockSpec(a_blk, a_map), pl.BlockSpec(b_blk, b_map)],
        out_specs=pl.BlockSpec(o_blk, o_map),
        out_shape=SDS(o_shape, out_dtype),
        scratch_shapes=[pltpu.VMEM((tm, tn), F32)],
        compiler_params=_cparams("parallel", "parallel", "arbitrary"), name=name)(a, b)


def _seg_map(nbc):
    return lambda i: (jnp.where(i < nbc, 0, 1), 0, 0)


def _rstd(v):
    return lax.rsqrt(jnp.mean(v * v, axis=-1, keepdims=True) + EPS)


def _norm_mod_fwd(x, g, mods, sh, sc, n_ctx, name):
    n, d = x.shape
    tm = _row_tile(n_ctx, n)
    nbc = n_ctx // tm

    def body(x_ref, g_ref, m_ref, h_ref):
        xv = x_ref[...]
        hn = xv * _rstd(xv) * g_ref[...]
        h_ref[...] = (hn * (1.0 + m_ref[0, sc:sc + 1, :]) + m_ref[0, sh:sh + 1, :]).astype(h_ref.dtype)

    row = pl.BlockSpec((tm, d), lambda i: (i, 0))
    return pl.pallas_call(
        body, grid=(n // tm,),
        in_specs=[row, pl.BlockSpec((1, d), lambda i: (0, 0)), pl.BlockSpec((1, 8, d), _seg_map(nbc))],
        out_specs=row, out_shape=SDS((n, d), MXU_DTYPE), compiler_params=_cparams("parallel"), name=name)(x, g, mods)


def _gate_res_fwd(x, f, g, mods, gi, n_ctx, name):
    n, d = x.shape
    tm = _row_tile(n_ctx, n)
    nbc = n_ctx // tm

    def body(x_ref, f_ref, g_ref, m_ref, o_ref):
        fv = f_ref[...]
        o_ref[...] = x_ref[...] + m_ref[0, gi:gi + 1, :] * (fv * _rstd(fv) * g_ref[...])

    row = pl.BlockSpec((tm, d), lambda i: (i, 0))
    return pl.pallas_call(
        body, grid=(n // tm,),
        in_specs=[row, row, pl.BlockSpec((1, d), lambda i: (0, 0)), pl.BlockSpec((1, 8, d), _seg_map(nbc))],
        out_specs=row, out_shape=SDS((n, d), F32), compiler_params=_cparams("parallel"), name=name)(x, f, g, mods)


def _gate_res_bwd(dx, f, g, mods, gi, n_ctx, name):
    n, d = dx.shape
    tm = _row_tile(n_ctx, n)
    nbc = n_ctx // tm

    def body(dx_ref, f_ref, g_ref, m_ref, df_ref, dgate_ref, dg_ref):
        i = pl.program_id(0)

        @pl.when(i == 0)
        def _():
            dg_ref[...] = jnp.zeros(dg_ref.shape, F32)

        @pl.when(jnp.logical_or(i == 0, i == nbc))
        def _():
            dgate_ref[...] = jnp.zeros(dgate_ref.shape, F32)

        dxv, fv, gv = dx_ref[...], f_ref[...], g_ref[...]
        rs = _rstd(fv)
        nv = fv * rs
        dgate_ref[0] += jnp.sum(dxv * (nv * gv), axis=0, keepdims=True)
        dout = dxv * m_ref[0, gi:gi + 1, :]
        dg_ref[...] += jnp.sum(dout * nv, axis=0, keepdims=True)
        dn = dout * gv
        df_ref[...] = (rs * (dn - nv * jnp.mean(dn * nv, axis=-1, keepdims=True))).astype(df_ref.dtype)

    row = pl.BlockSpec((tm, d), lambda i: (i, 0))
    vec = pl.BlockSpec((1, d), lambda i: (0, 0))
    return pl.pallas_call(
        body, grid=(n // tm,),
        in_specs=[row, row, vec, pl.BlockSpec((1, 8, d), _seg_map(nbc))],
        out_specs=[row, pl.BlockSpec((1, 1, d), _seg_map(nbc)), vec],
        out_shape=[SDS((n, d), MXU_DTYPE), SDS((2, 1, d), F32), SDS((1, d), F32)],
        compiler_params=_cparams("arbitrary"), name=name)(dx, f, g, mods)


def _norm_mod_bwd(dh, x, g, mods, sh, sc, dx_res, n_ctx, name):
    n, d = x.shape
    tm = _row_tile(n_ctx, n)
    nbc = n_ctx // tm

    def body(dh_ref, x_ref, g_ref, m_ref, r_ref, dx_ref, dss_ref, dg_ref):
        i = pl.program_id(0)

        @pl.when(i == 0)
        def _():
            dg_ref[...] = jnp.zeros(dg_ref.shape, F32)

        @pl.when(jnp.logical_or(i == 0, i == nbc))
        def _():
            dss_ref[...] = jnp.zeros(dss_ref.shape, F32)

        dhv, xv, gv = dh_ref[...], x_ref[...], g_ref[...]
        rs = _rstd(xv)
        nv = xv * rs
        dss_ref[0, 0:1, :] += jnp.sum(dhv, axis=0, keepdims=True)
        dss_ref[0, 1:2, :] += jnp.sum(dhv * (nv * gv), axis=0, keepdims=True)
        dhn = dhv * (1.0 + m_ref[0, sc:sc + 1, :])
        dg_ref[...] += jnp.sum(dhn * nv, axis=0, keepdims=True)
        dn = dhn * gv
        dx_ref[...] = r_ref[...] + rs * (dn - nv * jnp.mean(dn * nv, axis=-1, keepdims=True))

    row = pl.BlockSpec((tm, d), lambda i: (i, 0))
    vec = pl.BlockSpec((1, d), lambda i: (0, 0))
    return pl.pallas_call(
        body, grid=(n // tm,),
        in_specs=[row, row, vec, pl.BlockSpec((1, 8, d), _seg_map(nbc)), row],
        out_specs=[row, pl.BlockSpec((1, 2, d), _seg_map(nbc)), vec],
        out_shape=[SDS((n, d), F32), SDS((2, 2, d), F32), SDS((1, d), F32)],
        compiler_params=_cparams("arbitrary"), name=name)(dh, x, g, mods, dx_res)


def _loss_grad(xc, target, n_ctx, name):
    n, d = xc.shape
    tm = _row_tile(n_ctx, n)
    nbc = n_ctx // tm
    nb = n // tm

    def body(x_ref, t_ref, dx_ref, l_ref, acc_ref):
        i = pl.program_id(0)

        @pl.when(i == 0)
        def _():
            acc_ref[...] = jnp.zeros(acc_ref.shape, F32)

        @pl.when(i < nbc)
        def _():
            dx_ref[...] = jnp.zeros(dx_ref.shape, F32)

        @pl.when(i >= nbc)
        def _():
            diff = x_ref[...] - t_ref[...]
            dx_ref[...] = diff * (1.0 / d)
            acc_ref[...] += jnp.sum(diff * diff, axis=0, keepdims=True)

        @pl.when(i == nb - 1)
        def _():
            l_ref[...] = jnp.full(l_ref.shape, (0.5 / d) * jnp.sum(acc_ref[...]), F32)

    row = pl.BlockSpec((tm, d), lambda i: (i, 0))
    return pl.pallas_call(
        body, grid=(nb,),
        in_specs=[row, pl.BlockSpec((tm, d), lambda i: (jnp.maximum(i - nbc, 0), 0))],
        out_specs=[row, pl.BlockSpec((SUBLANES, LANES), lambda i: (0, 0))],
        out_shape=[SDS((n, d), F32), SDS((SUBLANES, LANES), F32)],
        scratch_shapes=[pltpu.VMEM((1, d), F32)],
        compiler_params=_cparams("arbitrary"), name=name)(xc, target)


POOL_PAD = 16


def _pool(src, pool_width, pool_group, n_ctx, bwd, out_dtype, name):
    n = src.shape[0]
    n_lat = n - n_ctx
    gb = pool_group // LANES
    segs = ((0, n_ctx, POOL_PAD), (n_ctx, n_lat, 2 * POOL_PAD + n_ctx))
    total = 3 * POOL_PAD + n

    def body(s_ref, o_ref, scr):
        j = pl.program_id(0)
        for base in (0, POOL_PAD + n_ctx, 2 * POOL_PAD + n):
            scr[pl.ds(base, POOL_PAD), :] = jnp.zeros((POOL_PAD, LANES), F32)
        for gi, w in enumerate(POOL_WINDOWS):
            @pl.when(jnp.logical_and(j >= gi * gb, j < (gi + 1) * gb))
            def _(w=w):
                half = w // 2
                offs = range(-half + 1, half + 1) if bwd else range(-half, half)
                for row0, nseg, base in segs:
                    ch = math.gcd(nseg, 256)

                    def count(c0):
                        t = c0 + lax.broadcasted_iota(jnp.int32, (ch, LANES), 0)
                        return (jnp.minimum(t + half, nseg) - jnp.maximum(t - half, 0)).astype(F32)

                    def fill(ci, carry):
                        c0 = pl.multiple_of(ci * ch, ch)
                        v = s_ref[pl.ds(row0 + c0, ch), :]
                        scr[pl.ds(base + c0, ch), :] = v / count(c0) if bwd else v
                        return carry

                    def window(ci, carry):
                        c0 = pl.multiple_of(ci * ch, ch)
                        acc = jnp.zeros((ch, LANES), F32)
                        for off in offs:
                            acc = acc + scr[pl.ds(c0 + (base + off), ch), :]
                        v = s_ref[pl.ds(row0 + c0, ch), :]
                        res = acc - v if bwd else acc / count(c0) - v
                        o_ref[pl.ds(row0 + c0, ch), :] = res.astype(o_ref.dtype)
                        return carry

                    lax.fori_loop(0, nseg // ch, fill, 0)
                    lax.fori_loop(0, nseg // ch, window, 0)

    blk = pl.BlockSpec((n, LANES), lambda j: (0, j))
    return pl.pallas_call(
        body, grid=(pool_width // LANES,), in_specs=[blk], out_specs=blk,
        out_shape=SDS((n, pool_width), out_dtype), scratch_shapes=[pltpu.VMEM((total, LANES), F32)],
        compiler_params=_cparams("parallel"), name=name)(src)


def _pool_proj_fwd(p, wp, l, scale, name):
    n, pw = p.shape
    ng, c = wp.shape[1], wp.shape[2]
    tm = _pick(n, _TM)

    def body(p_ref, w_ref, s_ref, o_ref):
        y = jnp.dot(p_ref[...], w_ref[...].astype(MXU_DTYPE), preferred_element_type=F32)
        o_ref[...] = (y * s_ref[...]).astype(o_ref.dtype)

    return pl.pallas_call(
        body, grid=(ng, n // tm),
        in_specs=[pl.BlockSpec((tm, c), lambda g, i: (i, g)), pl.BlockSpec((None, None, c, c), lambda g, i: (l, g, 0, 0)),
                  pl.BlockSpec((1, c), lambda g, i: (0, g))],
        out_specs=pl.BlockSpec((tm, c), lambda g, i: (i, g)), out_shape=SDS((n, pw), MXU_DTYPE),
        compiler_params=_cparams("parallel", "parallel"), name=name)(p, wp, scale)


def _pool_proj_bwd(p, dcat, wp, l, scale, name):
    n, pw = p.shape
    ng, c = wp.shape[1], wp.shape[2]
    tm = _pick(n, _TM)

    def body(p_ref, dy_ref, w_ref, s_ref, dp_ref, ds_ref, dw_ref):
        i = pl.program_id(1)

        @pl.when(i == 0)
        def _():
            ds_ref[...] = jnp.zeros(ds_ref.shape, F32)
            dw_ref[...] = jnp.zeros(dw_ref.shape, F32)

        pv, wv, dy = p_ref[...], w_ref[...].astype(MXU_DTYPE), dy_ref[...]
        y = jnp.dot(pv, wv, preferred_element_type=F32)
        ds_ref[...] += jnp.sum(dy * y, axis=0, keepdims=True)
        dpw = (dy * s_ref[...]).astype(MXU_DTYPE)
        dp_ref[...] = lax.dot_general(dpw, wv, _DIMS["nt"], preferred_element_type=F32)
        dw_ref[0] += lax.dot_general(pv, dpw, _DIMS["tn"], preferred_element_type=F32)

    return pl.pallas_call(
        body, grid=(ng, n // tm),
        in_specs=[pl.BlockSpec((tm, c), lambda g, i: (i, g)), pl.BlockSpec((tm, c), lambda g, i: (i, g)),
                  pl.BlockSpec((None, None, c, c), lambda g, i: (l, g, 0, 0)), pl.BlockSpec((1, c), lambda g, i: (0, g))],
        out_specs=[pl.BlockSpec((tm, c), lambda g, i: (i, g)), pl.BlockSpec((1, c), lambda g, i: (0, g)),
                   pl.BlockSpec((1, c, c), lambda g, i: (g, 0, 0))],
        out_shape=[SDS((n, pw), F32), SDS((1, pw), F32), SDS((ng, c, c), F32)],
        compiler_params=_cparams("arbitrary", "arbitrary"), name=name)(p, dcat, wp, scale)


def _disc_math(a_re, a_im, logdt, b_re, b_im):
    dt = jnp.exp(logdt)
    mag = jnp.exp(a_re * dt)
    lam_re = mag * jnp.cos(a_im * dt)
    lam_im = mag * jnp.sin(a_im * dt)
    denom = a_re * a_re + a_im * a_im
    nr, ni = lam_re - 1.0, lam_im
    f_re = ((nr * a_re + ni * a_im) / denom)[:, None, :]
    f_im = ((ni * a_re - nr * a_im) / denom)[:, None, :]
    return lam_re, lam_im, f_re * b_re - f_im * b_im, f_re * b_im + f_im * b_re


def _disc_fwd(a_re, a_im, logdt, b_re, b_im, name):
    def body(ar, ai, ld, br, bi, o_lr, o_li, o_br, o_bi):
        lr, li, bbr, bbi = _disc_math(ar[...], ai[...], ld[...], br[...], bi[...])
        o_lr[...] = lr
        o_li[...] = li
        o_br[...] = bbr
        o_bi[...] = bbi

    return pl.pallas_call(
        body, out_shape=[SDS(a_re.shape, F32), SDS(a_re.shape, F32), SDS(b_re.shape, F32), SDS(b_re.shape, F32)],
        compiler_params=_cparams(), name=name)(a_re, a_im, logdt, b_re, b_im)


def _disc_bwd(a_re, a_im, logdt, b_re, b_im, d_lr, d_li, d_bbr, d_bbi, group, name):
    rows, gp = a_re.shape

    def body(ar, ai, ld, br, bi, g_lr, g_li, g_br, g_bi, o_ar, o_ai, o_ld, o_br, o_bi):
        _, vjp = jax.vjp(_disc_math, ar[...], ai[...], ld[...], br[...], bi[...])
        dar, dai, dld, dbr, dbi = vjp((g_lr[...], g_li[...], g_br[...], g_bi[...]))
        o_ar[...] = dar
        o_ai[...] = dai
        state = lax.broadcasted_iota(jnp.int32, (gp, LANES), 0)
        first = lax.broadcasted_iota(jnp.int32, (gp, LANES), 1) * group
        sel = jnp.logical_and(state >= first, state < first + group).astype(F32)
        o_ld[...] = jnp.dot(dld, sel, precision=HIGHEST, preferred_element_type=F32)
        o_br[...] = dbr
        o_bi[...] = dbi

    return pl.pallas_call(
        body, out_shape=[SDS(a_re.shape, F32), SDS(a_re.shape, F32), SDS((rows, LANES), F32),
                         SDS(b_re.shape, F32), SDS(b_re.shape, F32)],
        compiler_params=_cparams(), name=name)(a_re, a_im, logdt, b_re, b_im, d_lr, d_li, d_bbr, d_bbi)


def _scan_maps(nbc, nb):
    nbl = nb - nbc
    fwd0 = lambda i: (i, 0, 0)
    fwd1 = lambda i: (jnp.where(i < nbc, nbc - 1 - i, nb - 1 - (i - nbc)), 0, 0)
    adj0 = lambda i: (nb - 1 - i, 0, 0)
    adj1 = lambda i: (jnp.where(i < nbl, nbc + i, i - nbl), 0, 0)
    return fwd0, fwd1, adj0, adj1


def _scan_fwd(bu0, bu1, lam, n_ctx, name):
    n, s2, _ = bu0.shape
    s = s2 // 2
    tt = math.gcd(math.gcd(n_ctx, n - n_ctx), 128)
    nbc, nb = n_ctx // tt, n // tt
    fwd0, fwd1, _, _ = _scan_maps(nbc, nb)

    def body(b0_ref, b1_ref, lam_ref, h0_ref, h1_ref, st_ref):
        @pl.when(pl.program_id(0) == 0)
        def _():
            st_ref[...] = jnp.zeros(st_ref.shape, F32)

        lr0, li0, lr1, li1 = lam_ref[0], lam_ref[1], lam_ref[2], lam_ref[3]

        def step(j, carry):
            h0r, h0i, h1r, h1i = carry
            t1 = tt - 1 - j
            n0r = lr0 * h0r - li0 * h0i + b0_ref[j, 0:s, :]
            n0i = lr0 * h0i + li0 * h0r + b0_ref[j, s:s2, :]
            n1r = lr1 * h1r - li1 * h1i + b1_ref[t1, 0:s, :]
            n1i = lr1 * h1i + li1 * h1r + b1_ref[t1, s:s2, :]
            h0_ref[j, 0:s, :] = n0r
            h0_ref[j, s:s2, :] = n0i
            h1_ref[t1, 0:s, :] = n1r
            h1_ref[t1, s:s2, :] = n1i
            return n0r, n0i, n1r, n1i

        out = lax.fori_loop(0, tt, step, (st_ref[0], st_ref[1], st_ref[2], st_ref[3]), unroll=2)
        for q in range(4):
            st_ref[q] = out[q]

    blk = (tt, s2, LANES)
    return pl.pallas_call(
        body, grid=(nb,),
        in_specs=[pl.BlockSpec(blk, fwd0), pl.BlockSpec(blk, fwd1), pl.BlockSpec((4, s, LANES), lambda i: (0, 0, 0))],
        out_specs=[pl.BlockSpec(blk, fwd0), pl.BlockSpec(blk, fwd1)],
        out_shape=[SDS(bu0.shape, F32), SDS(bu1.shape, F32)],
        scratch_shapes=[pltpu.VMEM((4, s, LANES), F32)],
        compiler_params=_cparams("arbitrary"), name=name)(bu0, bu1, lam)


def _scan_bwd(dh0, dh1, h0, h1, lam, n_ctx, name):
    n, s2, _ = dh0.shape
    s = s2 // 2
    tt = math.gcd(math.gcd(n_ctx, n - n_ctx), 128)
    nbc, nb = n_ctx // tt, n // tt
    _, _, adj0, adj1 = _scan_maps(nbc, nb)

    def body(d0_ref, d1_ref, h0_ref, h1_ref, lam_ref, a0_ref, a1_ref, dl_ref, st_ref, acc_ref):
        i = pl.program_id(0)

        @pl.when(i == 0)
        def _():
            st_ref[...] = jnp.zeros(st_ref.shape, F32)
            acc_ref[...] = jnp.zeros(acc_ref.shape, F32)

        lr0, li0, lr1, li1 = lam_ref[0], lam_ref[1], lam_ref[2], lam_ref[3]

        def step(j, carry):
            a0r, a0i, a1r, a1i, c0r, c0i, c1r, c1i = carry
            t0 = tt - 1 - j
            g0r, g0i = h0_ref[t0, 0:s, :], h0_ref[t0, s:s2, :]
            g1r, g1i = h1_ref[j, 0:s, :], h1_ref[j, s:s2, :]
            c0r = c0r + a0r * g0r + a0i * g0i
            c0i = c0i + a0i * g0r - a0r * g0i
            c1r = c1r + a1r * g1r + a1i * g1i
            c1i = c1i + a1i * g1r - a1r * g1i
            n0r = lr0 * a0r + li0 * a0i + d0_ref[t0, 0:s, :]
            n0i = lr0 * a0i - li0 * a0r + d0_ref[t0, s:s2, :]
            n1r = lr1 * a1r + li1 * a1i + d1_ref[j, 0:s, :]
            n1i = lr1 * a1i - li1 * a1r + d1_ref[j, s:s2, :]
            a0_ref[t0, 0:s, :] = n0r
            a0_ref[t0, s:s2, :] = n0i
            a1_ref[j, 0:s, :] = n1r
            a1_ref[j, s:s2, :] = n1i
            return n0r, n0i, n1r, n1i, c0r, c0i, c1r, c1i

        init = tuple(st_ref[q] for q in range(4)) + tuple(acc_ref[q] for q in range(4))
        out = lax.fori_loop(0, tt, step, init, unroll=2)
        for q in range(4):
            st_ref[q] = out[q]
            acc_ref[q] = out[4 + q]

        @pl.when(i == nb - 1)
        def _():
            for q in range(4):
                dl_ref[q] = out[4 + q]

    blk = (tt, s2, LANES)
    small = pl.BlockSpec((4, s, LANES), lambda i: (0, 0, 0))
    return pl.pallas_call(
        body, grid=(nb,),
        in_specs=[pl.BlockSpec(blk, adj0), pl.BlockSpec(blk, adj1), pl.BlockSpec(blk, adj0), pl.BlockSpec(blk, adj1), small],
        out_specs=[pl.BlockSpec(blk, adj0), pl.BlockSpec(blk, adj1), small],
        out_shape=[SDS(dh0.shape, F32), SDS(dh1.shape, F32), SDS((4, s, LANES), F32)],
        scratch_shapes=[pltpu.VMEM((4, s, LANES), F32), pltpu.VMEM((4, s, LANES), F32)],
        compiler_params=_cparams("arbitrary"), name=name)(dh0, dh1, h0, h1, lam)


def _gelu(v):
    th = jnp.tanh(GELU_C0 * (v + GELU_C1 * v * v * v))
    return 0.5 * v * (1.0 + th), th


def _ssm_head_fwd(y, u, ssm_d, wg, l, name):
    n, sw = y.shape
    ucol = u.shape[1] // sw - 1
    tm = _pick(n, _TM)

    def body(y_ref, u_ref, d_ref, w_ref, o_ref):
        act, _ = _gelu(y_ref[...] + d_ref[...] * u_ref[...])
        q = jnp.dot(act.astype(MXU_DTYPE), w_ref[...].astype(MXU_DTYPE), preferred_element_type=F32)
        o_ref[...] = (act * jax.nn.sigmoid(q)).astype(o_ref.dtype)

    row = pl.BlockSpec((tm, sw), lambda i: (i, 0))
    return pl.pallas_call(
        body, grid=(n // tm,),
        in_specs=[row, pl.BlockSpec((tm, sw), lambda i: (i, ucol)), pl.BlockSpec((1, sw), lambda i: (0, 0)),
                  pl.BlockSpec((None, sw, sw), lambda i: (l, 0, 0))],
        out_specs=row, out_shape=SDS((n, sw), MXU_DTYPE), compiler_params=_cparams("parallel"), name=name)(y, u, ssm_d, wg)


def _ssm_head_bwd(dcat, y, u, ssm_d, wg, l, name):
    n, sw = y.shape
    ucol = u.shape[1] // sw - 1
    tm = _pick(n, _TM)

    def body(do_ref, y_ref, u_ref, d_ref, w_ref, dy_ref, du_ref, act_ref, dq_ref, dd_ref):
        @pl.when(pl.program_id(0) == 0)
        def _():
            dd_ref[...] = jnp.zeros(dd_ref.shape, F32)

        uv, dv, do = u_ref[...], d_ref[...], do_ref[...]
        yf = y_ref[...] + dv * uv
        act, th = _gelu(yf)
        wv = w_ref[...].astype(MXU_DTYPE)
        sg = jax.nn.sigmoid(jnp.dot(act.astype(MXU_DTYPE), wv, preferred_element_type=F32))
        dq = (do * act * sg * (1.0 - sg)).astype(MXU_DTYPE)
        dact = do * sg + lax.dot_general(dq, wv, _DIMS["nt"], preferred_element_type=F32)
        dgelu = 0.5 * (1.0 + th) + 0.5 * yf * (1.0 - th * th) * GELU_C0 * (1.0 + 3.0 * GELU_C1 * yf * yf)
        dyf = dact * dgelu
        dy_ref[...] = dyf.astype(dy_ref.dtype)
        du_ref[...] = dyf * dv
        act_ref[...] = act.astype(act_ref.dtype)
        dq_ref[...] = dq
        dd_ref[...] += jnp.sum(dyf * uv, axis=0, keepdims=True)

    row = pl.BlockSpec((tm, sw), lambda i: (i, 0))
    last = pl.BlockSpec((tm, sw), lambda i: (i, ucol))
    vec = pl.BlockSpec((1, sw), lambda i: (0, 0))
    return pl.pallas_call(
        body, grid=(n // tm,),
        in_specs=[last, row, last, vec, pl.BlockSpec((None, sw, sw), lambda i: (l, 0, 0))],
        out_specs=[row, row, row, row, vec],
        out_shape=[SDS((n, sw), MXU_DTYPE), SDS((n, sw), F32), SDS((n, sw), MXU_DTYPE), SDS((n, sw), MXU_DTYPE),
                   SDS((1, sw), F32)],
        compiler_params=_cparams("arbitrary"), name=name)(dcat, y, u, ssm_d, wg)


def _assemble_du(du_pool, du_dir, du_proj, name):
    n, pw = du_pool.shape
    sw = du_dir.shape[1]
    tm = _pick(n, _TM)

    def body(p_ref, a_ref, b_ref, o_ref):
        o_ref[:, 0:pw] = p_ref[...].astype(o_ref.dtype)
        o_ref[:, pw:pw + sw] = (a_ref[...] + b_ref[...]).astype(o_ref.dtype)

    return pl.pallas_call(
        body, grid=(n // tm,),
        in_specs=[pl.BlockSpec((tm, pw), lambda i: (i, 0)), pl.BlockSpec((tm, sw), lambda i: (i, 0)),
                  pl.BlockSpec((tm, sw), lambda i: (i, 0))],
        out_specs=pl.BlockSpec((tm, pw + sw), lambda i: (i, 0)), out_shape=SDS((n, pw + sw), MXU_DTYPE),
        compiler_params=_cparams("parallel"), name=name)(du_pool, du_dir, du_proj)


CONV_PAD = GRID_W + SUBLANES


def _conv_layout(n, n_ctx):
    return CONV_PAD, 2 * CONV_PAD + n_ctx, 3 * CONV_PAD + n


def _col_masks(ch):
    col = lax.broadcasted_iota(jnp.int32, (ch, LANES), 0) % GRID_W
    return col != 0, col != GRID_W - 1


def _fill_padded(scr, src_ref, n, n_ctx):
    base_c, base_l, total = _conv_layout(n, n_ctx)
    for base in (0, base_c + n_ctx, base_l + n - n_ctx):
        scr[pl.ds(base, CONV_PAD), :] = jnp.zeros((CONV_PAD, LANES), F32)
    for row0, nseg, base in ((0, n_ctx, base_c), (n_ctx, n - n_ctx, base_l)):
        ch = math.gcd(nseg, 512)

        def copy(ci, carry, row0=row0, base=base, ch=ch):
            c0 = pl.multiple_of(ci * ch, ch)
            scr[pl.ds(base + c0, ch), :] = src_ref[pl.ds(row0 + c0, ch), :]
            return carry

        lax.fori_loop(0, nseg // ch, copy, 0)


def _conv_ctx(scr, k_ref, base, c0, ch, sign):
    acc = scr[pl.ds(c0 + base, ch), :] * k_ref[4:5, :]
    acc = acc + scr[pl.ds(c0 + (base - sign), ch), :] * k_ref[3:4, :]
    return acc + scr[pl.ds(c0 + (base + sign), ch), :] * k_ref[5:6, :]


def _conv_lat(scr, k_ref, base, c0, ch, sign, m_l, m_r):
    cols = []
    for j in range(3):
        acc = None
        for i in range(3):
            off = sign * (GRID_W * (i - 1) + (j - 1))
            term = scr[pl.ds(c0 + (base + off), ch), :] * k_ref[3 * i + j:3 * i + j + 1, :]
            acc = term if acc is None else acc + term
        cols.append(acc)
    first, last = (m_l, m_r) if sign > 0 else (m_r, m_l)
    return cols[1] + jnp.where(first, cols[0], 0.0) + jnp.where(last, cols[2], 0.0)


def _conv_chunk(n_lat):
    return math.gcd(n_lat, 256)


def _conv_glu_fwd(z, wk, n_ctx, name):
    n, f2 = z.shape
    dff = f2 // 2
    nvt = dff // LANES
    n_lat = n - n_ctx
    base_c, base_l, total = _conv_layout(n, n_ctx)
    ch = _conv_chunk(n_lat)
    assert ch % GRID_W == 0

    def body(zv_ref, zg_ref, kv_ref, kg_ref, a_ref, sv, sg):
        _fill_padded(sv, zv_ref, n, n_ctx)
        _fill_padded(sg, zg_ref, n, n_ctx)
        cv = _conv_ctx(sv, kv_ref, base_c, 0, n_ctx, 1)
        cg = _conv_ctx(sg, kg_ref, base_c, 0, n_ctx, 1)
        a_ref[pl.ds(0, n_ctx), :] = (cv * cg * jax.nn.sigmoid(cg)).astype(a_ref.dtype)
        m_l, m_r = _col_masks(ch)

        def lat(ci, carry):
            c0 = pl.multiple_of(ci * ch, ch)
            cv = _conv_lat(sv, kv_ref, base_l, c0, ch, 1, m_l, m_r)
            cg = _conv_lat(sg, kg_ref, base_l, c0, ch, 1, m_l, m_r)
            a_ref[pl.ds(n_ctx + c0, ch), :] = (cv * cg * jax.nn.sigmoid(cg)).astype(a_ref.dtype)
            return carry

        lax.fori_loop(0, n_lat // ch, lat, 0)

    col = lambda shift: pl.BlockSpec((n, LANES), lambda j: (0, j + shift))
    kcol = lambda shift: pl.BlockSpec((9, LANES), lambda j: (0, j + shift))
    return pl.pallas_call(
        body, grid=(nvt,), in_specs=[col(0), col(nvt), kcol(0), kcol(nvt)], out_specs=col(0),
        out_shape=SDS((n, dff), MXU_DTYPE),
        scratch_shapes=[pltpu.VMEM((total, LANES), F32), pltpu.VMEM((total, LANES), F32)],
        compiler_params=_cparams("parallel"), name=name)(z, z, wk, wk)


def _conv_glu_bwd(z, da, wk, n_ctx, name):
    n, f2 = z.shape
    dff = f2 // 2
    nvt = dff // LANES
    n_lat = n - n_ctx
    base_c, base_l, total = _conv_layout(n, n_ctx)
    ch = _conv_chunk(n_lat)
    assert ch % GRID_W == 0
    ctx_taps = [(1, 0), (1, 1), (1, 2)]
    lat_taps = [(i, j) for i in range(3) for j in range(3)]

    def tap_sums(acc, scr, d, base, c0, rows, taps, masks):
        acc = list(acc)
        for i, j in taps:
            src = scr[pl.ds(c0 + (base + GRID_W * (i - 1) + (j - 1)), rows), :]
            if masks is not None and j != 1:
                src = jnp.where(masks[0] if j == 0 else masks[1], src, 0.0)
            acc[3 * i + j] = acc[3 * i + j] + jnp.sum((src * d).reshape(rows // SUBLANES, SUBLANES, LANES), axis=0)
        return acc

    def body(zv_ref, zg_ref, da_ref, kv_ref, kg_ref, dzv_ref, dzg_ref, dkv_ref, dkg_ref, a_ref, sv, sg, dv, dg):
        _fill_padded(sv, zv_ref, n, n_ctx)
        _fill_padded(sg, zg_ref, n, n_ctx)
        for base in (0, base_c + n_ctx, base_l + n_lat):
            dv[pl.ds(base, CONV_PAD), :] = jnp.zeros((CONV_PAD, LANES), F32)
            dg[pl.ds(base, CONV_PAD), :] = jnp.zeros((CONV_PAD, LANES), F32)
        m_l, m_r = _col_masks(ch)

        def first_pass(cv, cg, row, pad_row, rows):
            sig = jax.nn.sigmoid(cg)
            silu = cg * sig
            a_ref[pl.ds(row, rows), :] = (cv * silu).astype(a_ref.dtype)
            dav = da_ref[pl.ds(row, rows), :]
            dcv = dav * silu
            dcg = dav * cv * (sig * (1.0 + cg * (1.0 - sig)))
            dv[pl.ds(pad_row, rows), :] = dcv
            dg[pl.ds(pad_row, rows), :] = dcg
            return dcv, dcg

        zero = [jnp.zeros((SUBLANES, LANES), F32) for _ in range(9)]
        cv = _conv_ctx(sv, kv_ref, base_c, 0, n_ctx, 1)
        cg = _conv_ctx(sg, kg_ref, base_c, 0, n_ctx, 1)
        dcv, dcg = first_pass(cv, cg, 0, base_c, n_ctx)
        accv = tap_sums(zero, sv, dcv, base_c, 0, n_ctx, ctx_taps, None)
        accg = tap_sums(zero, sg, dcg, base_c, 0, n_ctx, ctx_taps, None)

        def lat1(ci, carry):
            accv, accg = carry
            c0 = pl.multiple_of(ci * ch, ch)
            cv = _conv_lat(sv, kv_ref, base_l, c0, ch, 1, m_l, m_r)
            cg = _conv_lat(sg, kg_ref, base_l, c0, ch, 1, m_l, m_r)
            dcv, dcg = first_pass(cv, cg, n_ctx + c0, base_l + c0, ch)
            accv = tap_sums(accv, sv, dcv, base_l, c0, ch, lat_taps, (m_l, m_r))
            accg = tap_sums(accg, sg, dcg, base_l, c0, ch, lat_taps, (m_l, m_r))
            return tuple(accv), tuple(accg)

        accv, accg = lax.fori_loop(0, n_lat // ch, lat1, (tuple(accv), tuple(accg)))
        for t in range(9):
            dkv_ref[t:t + 1, :] = jnp.sum(accv[t], axis=0, keepdims=True)
            dkg_ref[t:t + 1, :] = jnp.sum(accg[t], axis=0, keepdims=True)

        dzv_ref[pl.ds(0, n_ctx), :] = _conv_ctx(dv, kv_ref, base_c, 0, n_ctx, -1).astype(dzv_ref.dtype)
        dzg_ref[pl.ds(0, n_ctx), :] = _conv_ctx(dg, kg_ref, base_c, 0, n_ctx, -1).astype(dzg_ref.dtype)

        def lat2(ci, carry):
            c0 = pl.multiple_of(ci * ch, ch)
            dzv_ref[pl.ds(n_ctx + c0, ch), :] = _conv_lat(dv, kv_ref, base_l, c0, ch, -1, m_l, m_r).astype(dzv_ref.dtype)
            dzg_ref[pl.ds(n_ctx + c0, ch), :] = _conv_lat(dg, kg_ref, base_l, c0, ch, -1, m_l, m_r).astype(dzg_ref.dtype)
            return carry

        lax.fori_loop(0, n_lat // ch, lat2, 0)

    col = lambda shift: pl.BlockSpec((n, LANES), lambda j: (0, j + shift))
    kcol = lambda shift: pl.BlockSpec((9, LANES), lambda j: (0, j + shift))
    pad = pltpu.VMEM((total, LANES), F32)
    return pl.pallas_call(
        body, grid=(nvt,), in_specs=[col(0), col(nvt), col(0), kcol(0), kcol(nvt)],
        out_specs=[col(0), col(0), kcol(0), kcol(0), col(0)],
        out_shape=[SDS((n, dff), MXU_DTYPE), SDS((n, dff), MXU_DTYPE), SDS((9, dff), F32), SDS((9, dff), F32),
                   SDS((n, dff), MXU_DTYPE)],
        scratch_shapes=[pad, pad, pad, pad],
        compiler_params=_cparams("parallel"), name=name)(z, z, da, wk, wk)


def _silu(v):
    return v * jax.nn.sigmoid(v)


def _ada_fwd(cond, w_ada, b_shard, name):
    nl, d, cols = w_ada.shape
    tn = _pick(cols, (512, 256, 128))

    def body(c_ref, w_ref, b_ref, o_ref):
        o_ref[...] = jnp.dot(_silu(c_ref[...]), w_ref[...], precision=HIGHEST, preferred_element_type=F32) + b_ref[...]

    return pl.pallas_call(
        body, grid=(nl, cols // tn),
        in_specs=[pl.BlockSpec(cond.shape, lambda l, j: (0, 0)), pl.BlockSpec((None, d, tn), lambda l, j: (l, 0, j)),
                  pl.BlockSpec((None, 1, tn), lambda l, j: (l, 0, j))],
        out_specs=pl.BlockSpec((None, cond.shape[0], tn), lambda l, j: (l, 0, j)),
        out_shape=SDS((nl, cond.shape[0], cols), F32),
        compiler_params=_cparams("parallel", "parallel"), name=name)(cond, w_ada, b_shard)


def _ada_dw(cond, dmod, name):
    nl, rows, cols = dmod.shape
    d = cond.shape[1]
    tn = _pick(cols, (512, 256, 128))

    def body(c_ref, g_ref, o_ref):
        o_ref[...] = lax.dot_general(_silu(c_ref[...]), g_ref[...], _DIMS["tn"], precision=HIGHEST,
                                     preferred_element_type=F32)

    return pl.pallas_call(
        body, grid=(nl, cols // tn),
        in_specs=[pl.BlockSpec(cond.shape, lambda l, j: (0, 0)), pl.BlockSpec((None, rows, tn), lambda l, j: (l, 0, j))],
        out_specs=pl.BlockSpec((None, d, tn), lambda l, j: (l, 0, j)), out_shape=SDS((nl, d, cols), F32),
        compiler_params=_cparams("parallel", "parallel"), name=name)(cond, dmod)


def _ada_dcond(dmod, w_ada, name):
    nl, rows, cols = dmod.shape
    d = w_ada.shape[1]
    tn = _pick(cols, (512, 256, 128))

    def body(g_ref, w_ref, o_ref):
        @pl.when(jnp.logical_and(pl.program_id(0) == 0, pl.program_id(1) == 0))
        def _():
            o_ref[...] = jnp.zeros(o_ref.shape, F32)

        o_ref[...] += lax.dot_general(g_ref[...], w_ref[...], _DIMS["nt"], precision=HIGHEST, preferred_element_type=F32)

    return pl.pallas_call(
        body, grid=(nl, cols // tn),
        in_specs=[pl.BlockSpec((None, rows, tn), lambda l, j: (l, 0, j)), pl.BlockSpec((None, d, tn), lambda l, j: (l, 0, j))],
        out_specs=pl.BlockSpec((rows, d), lambda l, j: (0, 0)), out_shape=SDS((rows, d), F32),
        compiler_params=_cparams("arbitrary", "arbitrary"), name=name)(dmod, w_ada)


def _ada_rows(dmod_all, name):
    nd, nl, _, w = dmod_all.shape
    tn = _pick(w, (2048, 1024, 512, 256, 128))

    def body(g_ref, rows_ref, db_ref):
        ctx = g_ref[0, 0, 0:1, :]
        for b in range(1, nd):
            ctx = ctx + g_ref[b, 0, 0:1, :]
        total = ctx
        for b in range(nd):
            lat = g_ref[b, 0, 1:2, :]
            rows_ref[b:b + 1, :] = lat
            total = total + lat
        rows_ref[nd:nd + 1, :] = ctx
        rows_ref[nd + 1:16, :] = jnp.zeros((16 - nd - 1, tn), F32)
        db_ref[...] = total

    return pl.pallas_call(
        body, grid=(nl, w // tn),
        in_specs=[pl.BlockSpec((nd, 1, 2, tn), lambda l, j: (0, l, 0, j))],
        out_specs=[pl.BlockSpec((None, 16, tn), lambda l, j: (l, 0, j)), pl.BlockSpec((None, 1, tn), lambda l, j: (l, 0, j))],
        out_shape=[SDS((nl, 16, w), F32), SDS((nl, 1, w), F32)],
        compiler_params=_cparams("parallel", "parallel"), name=name)(dmod_all)


def _ada_dctx(parts, c_ctx, row, name):
    def body(p_ref, c_ref, o_ref):
        ds = p_ref[0, row:row + 1, :]
        for k in range(1, p_ref.shape[0]):
            ds = ds + p_ref[k, row:row + 1, :]
        cv = c_ref[...]
        sg = jax.nn.sigmoid(cv)
        o_ref[...] = ds * (sg * (1.0 + cv * (1.0 - sg)))

    return pl.pallas_call(body, out_shape=SDS(c_ctx.shape, F32), compiler_params=_cparams(), name=name)(parts, c_ctx)


ROW_BLOCK_BYTES = 1 << 20


def _as_rows(shape):
    size = math.prod(shape)
    cols = shape[-1] if len(shape) >= 2 and shape[-1] % LANES == 0 else _pick(size, (1024, 512, 256, 128))
    rows = size // cols
    fits = [t for t in (512, 256, 128, 64, 32, 16, 8) if t * cols * 4 <= ROW_BLOCK_BYTES]
    return rows, cols, _pick(rows, fits)


def _sum_slots(buf, out_dtype, name):
    ns = buf.shape[0]
    rows, cols, tr = _as_rows(buf.shape[1:])

    def body(b_ref, o_ref):
        acc = b_ref[0].astype(F32)
        for s in range(1, ns):
            acc = acc + b_ref[s].astype(F32)
        o_ref[...] = acc.astype(o_ref.dtype)

    out = pl.pallas_call(
        body, grid=(rows // tr,), in_specs=[pl.BlockSpec((ns, tr, cols), lambda i: (0, i, 0))],
        out_specs=pl.BlockSpec((tr, cols), lambda i: (i, 0)), out_shape=SDS((rows, cols), out_dtype),
        compiler_params=_cparams("parallel"), name=name)(buf.reshape(ns, rows, cols))
    return out.reshape(buf.shape[1:])


def _adamw(w, g, m, v, name):
    rows, cols, tr = _as_rows(w.shape)
    c1 = 1.0 / (1.0 - ADAM_B1 ** ADAM_STEP)
    c2 = 1.0 / (1.0 - ADAM_B2 ** ADAM_STEP)

    def body(w_ref, g_ref, m_ref, v_ref, d_ref, nm_ref, nv_ref):
        gv = g_ref[...]
        nm = ADAM_B1 * m_ref[...] + (1.0 - ADAM_B1) * gv
        nv = ADAM_B2 * v_ref[...] + (1.0 - ADAM_B2) * (gv * gv)
        nm_ref[...] = nm
        nv_ref[...] = nv
        d_ref[...] = -ADAM_LR * ((nm * c1) / (jnp.sqrt(nv * c2) + ADAM_EPS) + ADAM_WD * w_ref[...])

    blk = pl.BlockSpec((tr, cols), lambda i: (i, 0))
    outs = pl.pallas_call(
        body, grid=(rows // tr,), in_specs=[blk] * 4, out_specs=[blk] * 3, out_shape=[SDS((rows, cols), F32)] * 3,
        compiler_params=_cparams("parallel"), name=name)(*[t.reshape(rows, cols) for t in (w, g, m, v)])
    return tuple(o.reshape(w.shape) for o in outs)


PIECE_BYTES = 3 << 20
MAX_PIECES = 16
PIECE_ROW_ALIGN = 16


def _coords():
    return lax.axis_index("x"), lax.axis_index("y"), lax.axis_index("c")


def _other_chips(x, y):
    return [(1 - x, y), (x, 1 - y), (1 - x, 1 - y)]


def _row_pieces(rows, nbytes):
    pieces = 1
    while (pieces < MAX_PIECES and nbytes // pieces > PIECE_BYTES and rows % (2 * pieces * PIECE_ROW_ALIGN) == 0):
        pieces *= 2
    step = rows // pieces
    return [pl.ds(i * step, step) for i in range(pieces)]


def _nbytes(shape, dtype):
    return math.prod(shape) * jnp.dtype(dtype).itemsize


def _offsets(counts):
    out, pos = [], 0
    for cnt in counts:
        out.append(pos)
        pos += cnt
    return out, pos


def _gather_chips(shards, name):
    nt = len(shards)
    half = [s.shape[0] // 2 for s in shards]
    pieces = [_row_pieces(s.shape[1], _nbytes((h,) + s.shape[1:], s.dtype)) for s, h in zip(shards, half)]
    base, total = _offsets([len(p) for p in pieces])

    def body(*refs):
        s_refs, o_refs = refs[:nt], refs[nt:2 * nt]
        lsem, ssem1, rsem1, ssem2, rsem2 = refs[2 * nt:]
        x, y, c = _coords()
        k = 2 * x + y
        chips = _other_chips(x, y)
        sends, local = [], []
        for t in range(nt):
            s_ref, o_ref = s_refs[t], o_refs[t]
            mine = pl.ds(c * half[t], half[t])
            for i, rs in enumerate(pieces[t]):
                cp = pltpu.make_async_copy(s_ref.at[:, rs], o_ref.at[:, k, rs], lsem.at[base[t] + i])
                cp.start()
                local.append(cp)
                for r, (px, py) in enumerate(chips):
                    q = 3 * (base[t] + i) + r
                    cp = pltpu.make_async_remote_copy(
                        src_ref=s_ref.at[mine, rs], dst_ref=o_ref.at[mine, k, rs], send_sem=ssem1.at[q],
                        recv_sem=rsem1.at[q], device_id=(px, py, c), device_id_type=MESH)
                    cp.start()
                    sends.append(cp)
        for t in range(nt):
            s_ref, o_ref = s_refs[t], o_refs[t]
            mine = pl.ds(c * half[t], half[t])
            for i, rs in enumerate(pieces[t]):
                for r, (px, py) in enumerate(chips):
                    q = 3 * (base[t] + i) + r
                    kk = 2 * px + py
                    pltpu.make_async_remote_copy(
                        src_ref=s_ref.at[mine, rs], dst_ref=o_ref.at[mine, kk, rs], send_sem=ssem1.at[q],
                        recv_sem=rsem1.at[q], device_id=(px, py, c), device_id_type=MESH).wait_recv()
                    cp = pltpu.make_async_remote_copy(
                        src_ref=o_ref.at[mine, kk, rs], dst_ref=o_ref.at[mine, kk, rs], send_sem=ssem2.at[q],
                        recv_sem=rsem2.at[q], device_id=(x, y, 1 - c), device_id_type=MESH)
                    cp.start()
                    sends.append(cp)
        for t in range(nt):
            o_ref = o_refs[t]
            theirs = pl.ds((1 - c) * half[t], half[t])
            for i, rs in enumerate(pieces[t]):
                for r, (px, py) in enumerate(chips):
                    q = 3 * (base[t] + i) + r
                    kk = 2 * px + py
                    pltpu.make_async_remote_copy(
                        src_ref=o_ref.at[theirs, kk, rs], dst_ref=o_ref.at[theirs, kk, rs], send_sem=ssem2.at[q],
                        recv_sem=rsem2.at[q], device_id=(x, y, 1 - c), device_id_type=MESH).wait_recv()
        for cp in sends:
            cp.wait_send()
        for cp in local:
            cp.wait()

    sem = pltpu.SemaphoreType.DMA
    outs = pl.pallas_call(
        body, in_specs=[ANY] * nt, out_specs=[ANY] * nt,
        out_shape=[SDS((s.shape[0], N_CHIPS) + s.shape[1:], s.dtype) for s in shards],
        scratch_shapes=[sem((total,)), sem((3 * total,)), sem((3 * total,)), sem((3 * total,)), sem((3 * total,))],
        name=name)(*shards)
    return list(outs)


def _pair_split(grads, name):
    nt = len(grads)
    half = [g.shape[0] // 2 for g in grads]
    pieces = [_row_pieces(g.shape[2], _nbytes((h,) + g.shape[1:], g.dtype)) for g, h in zip(grads, half)]
    base, total = _offsets([len(p) for p in pieces])

    def body(*refs):
        g_refs, o_refs = refs[:nt], refs[nt:2 * nt]
        lsem, ssem, rsem = refs[2 * nt:]
        x, y, c = _coords()
        cps = []
        for t in range(nt):
            mine = pl.ds(c * half[t], half[t])
            theirs = pl.ds((1 - c) * half[t], half[t])
            for i, rs in enumerate(pieces[t]):
                q = base[t] + i
                loc = pltpu.make_async_copy(g_refs[t].at[mine, :, rs], o_refs[t].at[c, :, :, rs], lsem.at[q])
                loc.start()
                cp = pltpu.make_async_remote_copy(
                    src_ref=g_refs[t].at[theirs, :, rs], dst_ref=o_refs[t].at[c, :, :, rs], send_sem=ssem.at[q],
                    recv_sem=rsem.at[q], device_id=(x, y, 1 - c), device_id_type=MESH)
                cp.start()
                cps.append((loc, cp))
        for t in range(nt):
            theirs = pl.ds((1 - c) * half[t], half[t])
            for i, rs in enumerate(pieces[t]):
                q = base[t] + i
                pltpu.make_async_remote_copy(
                    src_ref=g_refs[t].at[theirs, :, rs], dst_ref=o_refs[t].at[1 - c, :, :, rs], send_sem=ssem.at[q],
                    recv_sem=rsem.at[q], device_id=(x, y, 1 - c), device_id_type=MESH).wait_recv()
        for loc, cp in cps:
            cp.wait_send()
            loc.wait()

    sem = pltpu.SemaphoreType.DMA
    outs = pl.pallas_call(
        body, in_specs=[ANY] * nt, out_specs=[ANY] * nt,
        out_shape=[SDS((2, g.shape[0] // 2) + g.shape[1:], g.dtype) for g in grads],
        scratch_shapes=[sem((total,)), sem((total,)), sem((total,))],
        name=name)(*grads)
    return list(outs)


def _chip_scatter(parts, name):
    nt = len(parts)
    pieces = [_row_pieces(p.shape[2], _nbytes((p.shape[0],) + p.shape[2:], p.dtype)) for p in parts]
    base, total = _offsets([len(p) for p in pieces])

    def body(*refs):
        p_refs, o_refs = refs[:nt], refs[nt:2 * nt]
        lsem, ssem, rsem = refs[2 * nt:]
        x, y, c = _coords()
        k = 2 * x + y
        chips = _other_chips(x, y)
        local, sends = [], []
        for t in range(nt):
            for i, rs in enumerate(pieces[t]):
                loc = pltpu.make_async_copy(p_refs[t].at[:, k, rs], o_refs[t].at[k, :, rs], lsem.at[base[t] + i])
                loc.start()
                local.append(loc)
                for r, (px, py) in enumerate(chips):
                    q = 3 * (base[t] + i) + r
                    cp = pltpu.make_async_remote_copy(
                        src_ref=p_refs[t].at[:, 2 * px + py, rs], dst_ref=o_refs[t].at[k, :, rs], send_sem=ssem.at[q],
                        recv_sem=rsem.at[q], device_id=(px, py, c), device_id_type=MESH)
                    cp.start()
                    sends.append(cp)
        for t in range(nt):
            for i, rs in enumerate(pieces[t]):
                for r, (px, py) in enumerate(chips):
                    q = 3 * (base[t] + i) + r
                    pltpu.make_async_remote_copy(
                        src_ref=p_refs[t].at[:, k, rs], dst_ref=o_refs[t].at[2 * px + py, :, rs], send_sem=ssem.at[q],
                        recv_sem=rsem.at[q], device_id=(px, py, c), device_id_type=MESH).wait_recv()
        for cp in sends:
            cp.wait_send()
        for cp in local:
            cp.wait()

    sem = pltpu.SemaphoreType.DMA
    outs = pl.pallas_call(
        body, in_specs=[ANY] * nt, out_specs=[ANY] * nt,
        out_shape=[SDS((N_CHIPS, p.shape[0]) + p.shape[2:], p.dtype) for p in parts],
        scratch_shapes=[sem((total,)), sem((3 * total,)), sem((3 * total,))],
        name=name)(*parts)
    return list(outs)


def _pair_join(halves, name):
    nt = len(halves)
    pieces = [_row_pieces(h.shape[1], _nbytes(h.shape, h.dtype)) for h in halves]
    base, total = _offsets([len(p) for p in pieces])

    def body(*refs):
        h_refs, o_refs = refs[:nt], refs[nt:2 * nt]
        lsem, ssem, rsem = refs[2 * nt:]
        x, y, c = _coords()
        cps = []
        for t in range(nt):
            hn = h_refs[t].shape[0]
            mine = pl.ds(c * hn, hn)
            for i, rs in enumerate(pieces[t]):
                q = base[t] + i
                loc = pltpu.make_async_copy(h_refs[t].at[:, rs], o_refs[t].at[mine, rs], lsem.at[q])
                loc.start()
                cp = pltpu.make_async_remote_copy(
                    src_ref=h_refs[t].at[:, rs], dst_ref=o_refs[t].at[mine, rs], send_sem=ssem.at[q], recv_sem=rsem.at[q],
                    device_id=(x, y, 1 - c), device_id_type=MESH)
                cp.start()
                cps.append((loc, cp))
        for t in range(nt):
            hn = h_refs[t].shape[0]
            theirs = pl.ds((1 - c) * hn, hn)
            for i, rs in enumerate(pieces[t]):
                q = base[t] + i
                pltpu.make_async_remote_copy(
                    src_ref=h_refs[t].at[:, rs], dst_ref=o_refs[t].at[theirs, rs], send_sem=ssem.at[q],
                    recv_sem=rsem.at[q], device_id=(x, y, 1 - c), device_id_type=MESH).wait_recv()
        for loc, cp in cps:
            cp.wait_send()
            loc.wait()

    sem = pltpu.SemaphoreType.DMA
    outs = pl.pallas_call(
        body, in_specs=[ANY] * nt, out_specs=[ANY] * nt,
        out_shape=[SDS((2 * h.shape[0],) + h.shape[1:], h.dtype) for h in halves],
        scratch_shapes=[sem((total,)), sem((total,)), sem((total,))],
        name=name)(*halves)
    return list(outs)


def _gather_devices(vals, name):
    nt = len(vals)
    flips = [(a, b, e) for a in (0, 1) for b in (0, 1) for e in (0, 1)][1:]

    def body(*refs):
        v_refs, o_refs = refs[:nt], refs[nt:2 * nt]
        lsem, ssem, rsem = refs[2 * nt:]
        x, y, c = _coords()
        me = 4 * x + 2 * y + c
        peers = [((1 - x) if a else x, (1 - y) if b else y, (1 - c) if e else c) for a, b, e in flips]
        cps = []
        for t in range(nt):
            loc = pltpu.make_async_copy(v_refs[t], o_refs[t].at[me], lsem.at[t])
            loc.start()
            cps.append(loc)
            for r, peer in enumerate(peers):
                cp = pltpu.make_async_remote_copy(
                    src_ref=v_refs[t], dst_ref=o_refs[t].at[me], send_sem=ssem.at[7 * t + r],
                    recv_sem=rsem.at[7 * t + r], device_id=peer, device_id_type=MESH)
                cp.start()
                cps.append(cp)
        for t in range(nt):
            for r, (px, py, pc) in enumerate(peers):
                pltpu.make_async_remote_copy(
                    src_ref=v_refs[t], dst_ref=o_refs[t].at[4 * px + 2 * py + pc], send_sem=ssem.at[7 * t + r],
                    recv_sem=rsem.at[7 * t + r], device_id=(px, py, pc), device_id_type=MESH).wait_recv()
        for t in range(nt):
            cps[8 * t].wait()
            for r in range(7):
                cps[8 * t + 1 + r].wait_send()

    sem = pltpu.SemaphoreType.DMA
    outs = pl.pallas_call(
        body, in_specs=[ANY] * nt, out_specs=[ANY] * nt,
        out_shape=[SDS((N_DEV,) + v.shape, v.dtype) for v in vals],
        scratch_shapes=[sem((nt,)), sem((7 * nt,)), sem((7 * nt,))],
        name=name)(*vals)
    return list(outs)


def _reduce_to_shards(grads, tag):
    split = _pair_split(grads, f"{tag}_pair_split")
    pair = [_sum_slots(b, b.dtype, f"{tag}_pair_sum{t}") for t, b in enumerate(split)]
    scat = _chip_scatter(pair, f"{tag}_chip_scatter")
    half = [_sum_slots(b, F32, f"{tag}_chip_sum{t}") for t, b in enumerate(scat)]
    return _pair_join(half, f"{tag}_pair_join")


WEIGHT_NAMES = ("c_ctx", "w_ada", "b_ada", "w_in", "w_pool", "pool_scale", "ssm_a_re", "ssm_a_im", "ssm_log_dt",
                "ssm_b_re", "ssm_b_im", "ssm_c_re", "ssm_c_im", "ssm_d", "w_glu", "w_out", "g_pre_mix", "g_post_mix",
                "g_pre_ffn", "g_post_ffn", "w_up", "w_conv", "w_down")


def _block_diag_in(bb, ng):
    nl, nd, npart, h, gp = bb.shape
    p = gp // ng
    w = jnp.einsum("ldqhgp,kg->lkhdqgp", bb.reshape(nl, nd, npart, h, ng, p), jnp.eye(ng, dtype=bb.dtype))
    return w.reshape(nl, ng * h, nd * npart * gp)


def _block_diag_in_grad(dw, nd, npart, h, ng, p):
    return jnp.einsum("khdqkp->dqhkp", dw.reshape(ng, h, nd, npart, ng, p)).reshape(nd, npart, h, ng * p)


def _block_diag_out(cs, ng):
    nl, nd, npart, _, h, p = cs.shape
    w = jnp.einsum("ldqghp,kg->ldqkpgh", cs, jnp.eye(ng, dtype=cs.dtype))
    return w.reshape(nl, nd * npart * ng * p, ng * h)


def _block_diag_out_grad(dw, nd, npart, h, ng, p):
    return jnp.einsum("dqkpkh->dqkhp", dw.reshape(nd, npart, ng, p, ng, h))


def kernel(x, c, ctx, c_ctx, w_ada, b_ada, w_in, w_pool, pool_scale, ssm_a_re, ssm_a_im, ssm_log_dt, ssm_b_re, ssm_b_im, ssm_c_re, ssm_c_im, ssm_d, w_glu, w_out, g_pre_mix, g_post_mix, g_pre_ffn, g_post_ffn, w_up, w_conv, w_down, loss_target, m_c_ctx, m_w_ada, m_b_ada, m_w_in, m_w_pool, m_pool_scale, m_ssm_a_re, m_ssm_a_im, m_ssm_log_dt, m_ssm_b_re, m_ssm_b_im, m_ssm_c_re, m_ssm_c_im, m_ssm_d, m_w_glu, m_w_out, m_g_pre_mix, m_g_post_mix, m_g_pre_ffn, m_g_post_ffn, m_w_up, m_w_conv, m_w_down, v_c_ctx, v_w_ada, v_b_ada, v_w_in, v_w_pool, v_pool_scale, v_ssm_a_re, v_ssm_a_im, v_ssm_log_dt, v_ssm_b_re, v_ssm_b_im, v_ssm_c_re, v_ssm_c_im, v_ssm_d, v_w_glu, v_w_out, v_g_pre_mix, v_g_post_mix, v_g_pre_ffn, v_g_post_ffn, v_w_up, v_w_conv, v_w_down):
    weights = dict(zip(WEIGHT_NAMES, (c_ctx, w_ada, b_ada, w_in, w_pool, pool_scale, ssm_a_re, ssm_a_im, ssm_log_dt,
                                      ssm_b_re, ssm_b_im, ssm_c_re, ssm_c_im, ssm_d, w_glu, w_out, g_pre_mix, g_post_mix,
                                      g_pre_ffn, g_post_ffn, w_up, w_conv, w_down)))
    mom1 = dict(zip(WEIGHT_NAMES, (m_c_ctx, m_w_ada, m_b_ada, m_w_in, m_w_pool, m_pool_scale, m_ssm_a_re, m_ssm_a_im,
                                   m_ssm_log_dt, m_ssm_b_re, m_ssm_b_im, m_ssm_c_re, m_ssm_c_im, m_ssm_d, m_w_glu, m_w_out,
                                   m_g_pre_mix, m_g_post_mix, m_g_pre_ffn, m_g_post_ffn, m_w_up, m_w_conv, m_w_down)))
    mom2 = dict(zip(WEIGHT_NAMES, (v_c_ctx, v_w_ada, v_b_ada, v_w_in, v_w_pool, v_pool_scale, v_ssm_a_re, v_ssm_a_im,
                                   v_ssm_log_dt, v_ssm_b_re, v_ssm_b_im, v_ssm_c_re, v_ssm_c_im, v_ssm_d, v_w_glu, v_w_out,
                                   v_g_pre_mix, v_g_post_mix, v_g_pre_ffn, v_g_post_ffn, v_w_up, v_w_conv, v_w_down)))

    xi, yi, ci = lax.axis_index("x"), lax.axis_index("y"), lax.axis_index("c")
    chip = 2 * xi + yi
    dev = 4 * xi + 2 * yi + ci
    nl = w_in.shape[0]
    n_lat, d = x.shape[1], x.shape[2]
    n_ctx = ctx.shape[1]
    n = n_ctx + n_lat
    _, ndir, ng, nstate, nh = ssm_b_re.shape
    gp = ng * nstate
    sw = ng * nh
    n_pool_groups, pool_group = w_pool.shape[1], w_pool.shape[3]
    pw = n_pool_groups * pool_group
    assert pw + sw == d and pw % sw == 0 and len(POOL_WINDOWS) == n_pool_groups and n_lat % GRID_W == 0
    dff2 = w_up.shape[2] * N_CHIPS
    ada_w = w_ada.shape[2] * N_CHIPS
    ada_cols = w_ada.shape[2]
    s_rows = gp // LANES

    c_pad = jnp.concatenate([c, jnp.zeros((SUBLANES - 1, d), F32)], axis=0)
    c_all = _gather_devices([c_pad], "gather_cond")[0][:, 0, :]
    cond = jnp.concatenate([c_all, c_ctx[None, :], jnp.zeros((16 - N_DEV - 1, d), F32)], axis=0)
    b_shard = lax.dynamic_slice_in_dim(b_ada, chip * ada_cols, ada_cols, axis=1)[:, None, :]
    mod_shard = _ada_fwd(cond, w_ada, b_shard, "ada_fwd")
    mod_all = _gather_chips([mod_shard], "gather_mods")[0]
    mod_all = jnp.transpose(mod_all, (0, 2, 1, 3)).reshape(nl, 16, ada_w)
    mod_lat = lax.dynamic_index_in_dim(mod_all, dev, axis=1, keepdims=False).reshape(nl, 6, d)
    mod_ctx = mod_all[:, N_DEV].reshape(nl, 6, d)
    mods = jnp.concatenate([jnp.stack([mod_ctx, mod_lat], axis=1), jnp.zeros((nl, 2, 2, d), F32)], axis=2)

    shards = [w_in.astype(COMM_DTYPE), w_pool.reshape(nl, pw // N_CHIPS, pool_group).astype(COMM_DTYPE),
              w_glu.astype(COMM_DTYPE), w_out.astype(COMM_DTYPE), w_up.astype(COMM_DTYPE), w_down.astype(COMM_DTYPE),
              w_conv.reshape(nl, 9, dff2 // N_CHIPS)]
    g_in, g_pool, g_glu, g_out, g_up, g_down, g_conv = _gather_chips(shards, "gather_weights")
    wi = g_in.reshape(nl, d, d)
    wp = jnp.transpose(g_pool.reshape(nl, N_CHIPS, n_pool_groups, pool_group // N_CHIPS, pool_group),
                       (0, 2, 1, 3, 4)).reshape(nl, n_pool_groups, pool_group, pool_group)
    wg = g_glu.reshape(nl, sw, sw)
    wo = g_out.reshape(nl, d, d)
    wu = g_up
    wd = g_down.reshape(nl, dff2 // 2, d)
    wk = jnp.transpose(g_conv, (0, 2, 1, 3)).reshape(nl, 9, dff2)

    rows = nl * ndir
    a_re2 = ssm_a_re.reshape(rows, gp)
    a_im2 = ssm_a_im.reshape(rows, gp)
    logdt2 = jnp.repeat(ssm_log_dt.reshape(rows, ng), nstate, axis=1)
    b_re2 = jnp.transpose(ssm_b_re.reshape(rows, gp, nh), (0, 2, 1))
    b_im2 = jnp.transpose(ssm_b_im.reshape(rows, gp, nh), (0, 2, 1))
    lam_re, lam_im, bb_re, bb_im = _disc_fwd(a_re2, a_im2, logdt2, b_re2, b_im2, "s5_discretise")
    lam = jnp.stack([lam_re.reshape(nl, ndir, s_rows, LANES), lam_im.reshape(nl, ndir, s_rows, LANES)], axis=2)
    lam = lam.reshape(nl, 2 * ndir, s_rows, LANES)
    bbs = jnp.stack([bb_re.reshape(nl, ndir, nh, gp), bb_im.reshape(nl, ndir, nh, gp)], axis=2)
    w_b = _block_diag_in(bbs, ng).astype(MXU_DTYPE)
    cs = jnp.stack([ssm_c_re, -ssm_c_im], axis=2)
    w_c = _block_diag_out(cs, ng).astype(MXU_DTYPE)

    def row(v, l):
        return v[l:l + 1]

    xc = jnp.concatenate([ctx[0], x[0]], axis=0)
    saved = []
    for l in range(nl):
        t = f"l{l}"
        md = mods[l]
        h = _norm_mod_fwd(xc, row(g_pre_mix, l), md, 0, 1, n_ctx, f"{t}_pre_mix")
        u = _mm(h, wi, "nn", F32, f"{t}_in_proj", b_idx=l)
        p = _pool(u, pw, pool_group, n_ctx, False, MXU_DTYPE, f"{t}_pool")
        ypool = _pool_proj_fwd(p, wp, l, row(pool_scale, l), f"{t}_pool_proj")
        bu = _mm(u, w_b, "nn", F32, f"{t}_s5_in", b_idx=l, a_cols=(pw, sw))
        bu0 = bu[:, :2 * gp].reshape(n, 2 * s_rows, LANES)
        bu1 = bu[:, 2 * gp:].reshape(n, 2 * s_rows, LANES)
        h0, h1 = _scan_fwd(bu0, bu1, lam[l], n_ctx, f"{t}_scan")
        hcat = jnp.concatenate([h0.reshape(n, 2 * gp), h1.reshape(n, 2 * gp)], axis=1).astype(MXU_DTYPE)
        y = _mm(hcat, w_c, "nn", F32, f"{t}_s5_out", b_idx=l)
        s_out = _ssm_head_fwd(y, u, row(ssm_d, l), wg, l, f"{t}_s5_head")
        cat = jnp.concatenate([ypool, s_out], axis=1)
        mix = _mm(cat, wo, "nn", F32, f"{t}_out_proj", b_idx=l)
        x_mid = _gate_res_fwd(xc, mix, row(g_post_mix, l), md, 2, n_ctx, f"{t}_post_mix")
        h2 = _norm_mod_fwd(x_mid, row(g_pre_ffn, l), md, 3, 4, n_ctx, f"{t}_pre_ffn")
        z = _mm(h2, wu, "nn", F32, f"{t}_up", b_idx=l, b_chunks=N_CHIPS)
        act = _conv_glu_fwd(z, wk[l], n_ctx, f"{t}_conv_glu")
        f = _mm(act, wd, "nn", F32, f"{t}_down", b_idx=l)
        x_out = _gate_res_fwd(x_mid, f, row(g_post_ffn, l), md, 5, n_ctx, f"{t}_post_ffn")
        saved.append(dict(xc=xc, h=h, u=u, p=p, h0=h0, h1=h1, hcat=hcat, y=y, cat=cat, mix=mix, x_mid=x_mid, h2=h2, z=z, f=f))
        xc = x_out

    dx, loss_tile = _loss_grad(xc, loss_target[0], n_ctx, "loss")
    loss = lax.psum(loss_tile[0, 0], ("x", "y", "c"))

    big = {k: [None] * nl for k in ("w_in", "w_pool", "w_glu", "w_out", "w_up", "w_down")}
    small = {k: [None] * nl for k in ("pool_scale", "ssm_d", "g_pre_mix", "g_post_mix", "g_pre_ffn", "g_post_ffn",
                                      "lam", "bb", "cs", "w_conv")}
    dmods = [None] * nl
    for l in reversed(range(nl)):
        t = f"l{l}b"
        md = mods[l]
        sv = saved[l]
        df, dgate_ffn, small["g_post_ffn"][l] = _gate_res_bwd(dx, sv["f"], row(g_post_ffn, l), md, 5, n_ctx, f"{t}_post_ffn")
        dact = _mm(df, wd, "nt", F32, f"{t}_down_dx", b_idx=l)
        dzv, dzg, dkv, dkg, act = _conv_glu_bwd(sv["z"], dact, wk[l], n_ctx, f"{t}_conv_glu")
        dz = jnp.concatenate([dzv, dzg], axis=1)
        small["w_conv"][l] = jnp.concatenate([dkv, dkg], axis=1)
        big["w_down"][l] = _mm(act, df, "tn", COMM_DTYPE, f"{t}_down_dw")
        big["w_up"][l] = _mm(sv["h2"], dz, "tn", COMM_DTYPE, f"{t}_up_dw", out_chunks=N_CHIPS)
        dh2 = _mm(dz, wu, "nt", F32, f"{t}_up_dx", b_idx=l, b_chunks=N_CHIPS)
        dx, dss_ffn, small["g_pre_ffn"][l] = _norm_mod_bwd(dh2, sv["x_mid"], row(g_pre_ffn, l), md, 3, 4, dx, n_ctx,
                                                           f"{t}_pre_ffn")
        dmix, dgate_mix, small["g_post_mix"][l] = _gate_res_bwd(dx, sv["mix"], row(g_post_mix, l), md, 2, n_ctx,
                                                                f"{t}_post_mix")
        dcat = _mm(dmix, wo, "nt", F32, f"{t}_out_dx", b_idx=l)
        big["w_out"][l] = _mm(sv["cat"], dmix, "tn", COMM_DTYPE, f"{t}_out_dw")
        dp, small["pool_scale"][l], big["w_pool"][l] = _pool_proj_bwd(sv["p"], dcat, wp, l, row(pool_scale, l),
                                                                      f"{t}_pool_proj")
        du_pool = _pool(dp, pw, pool_group, n_ctx, True, F32, f"{t}_pool")
        dy, du_dir, gact, dq, small["ssm_d"][l] = _ssm_head_bwd(dcat, sv["y"], sv["u"], row(ssm_d, l), wg, l, f"{t}_s5_head")
        big["w_glu"][l] = _mm(gact, dq, "tn", COMM_DTYPE, f"{t}_glu_dw")
        dhcat = _mm(dy, w_c, "nt", F32, f"{t}_s5_out_dx", b_idx=l)
        dwc = _mm(sv["hcat"], dy, "tn", F32, f"{t}_s5_out_dw")
        small["cs"][l] = _block_diag_out_grad(dwc, ndir, 2, nh, ng, nstate)
        dh0 = dhcat[:, :2 * gp].reshape(n, 2 * s_rows, LANES)
        dh1 = dhcat[:, 2 * gp:].reshape(n, 2 * s_rows, LANES)
        a0, a1, small["lam"][l] = _scan_bwd(dh0, dh1, sv["h0"], sv["h1"], lam[l], n_ctx, f"{t}_scan")
        acat = jnp.concatenate([a0.reshape(n, 2 * gp), a1.reshape(n, 2 * gp)], axis=1).astype(MXU_DTYPE)
        du_proj = _mm(acat, w_b, "nt", F32, f"{t}_s5_in_dx", b_idx=l)
        dwb = _mm(sv["u"], acat, "tn", F32, f"{t}_s5_in_dw", a_cols=(pw, sw))
        small["bb"][l] = _block_diag_in_grad(dwb, ndir, 2, nh, ng, nstate)
        du = _assemble_du(du_pool, du_dir, du_proj, f"{t}_du")
        dh = _mm(du, wi, "nt", F32, f"{t}_in_dx", b_idx=l)
        big["w_in"][l] = _mm(sv["h"], du, "tn", COMM_DTYPE, f"{t}_in_dw")
        dx, dss_mix, small["g_pre_mix"][l] = _norm_mod_bwd(dh, sv["xc"], row(g_pre_mix, l), md, 0, 1, dx, n_ctx,
                                                           f"{t}_pre_mix")
        dmods[l] = jnp.concatenate([dss_mix, dgate_mix, dss_ffn, dgate_ffn], axis=1).reshape(2, ada_w)

    grad_x = dx[n_ctx:][None]

    dmod_all = _gather_devices([jnp.stack(dmods, axis=0)], "gather_dmods")[0]
    ada_rows, db_ada = _ada_rows(dmod_all, "ada_rows")
    rows_shard = lax.dynamic_slice_in_dim(ada_rows, chip * ada_cols, ada_cols, axis=2)
    dcond_part = _ada_dcond(rows_shard, w_ada, "ada_dcond")
    dcond_parts = _gather_devices([dcond_part], "gather_dcond")[0][0::2]
    grads = {"w_ada": _ada_dw(cond, rows_shard, "ada_dw"), "b_ada": db_ada[:, 0, :],
             "c_ctx": _ada_dctx(dcond_parts, c_ctx[None, :], N_DEV, "ada_dctx")[0]}

    stacked = {k: jnp.stack(v, axis=0) for k, v in big.items()}
    parts = [stacked["w_in"].reshape(nl, N_CHIPS, d // N_CHIPS, d),
             jnp.transpose(stacked["w_pool"].astype(COMM_DTYPE).reshape(nl, n_pool_groups, N_CHIPS, pool_group // N_CHIPS,
                                                                      pool_group), (0, 2, 1, 3, 4))
             .reshape(nl, N_CHIPS, pw // N_CHIPS, pool_group),
             stacked["w_glu"].reshape(nl, N_CHIPS, sw // N_CHIPS, sw),
             stacked["w_out"].reshape(nl, N_CHIPS, d // N_CHIPS, d),
             stacked["w_up"],
             stacked["w_down"].reshape(nl, N_CHIPS, dff2 // 2 // N_CHIPS, d)]
    r_in, r_pool, r_glu, r_out, r_up, r_down = _reduce_to_shards(parts, "big")
    grads.update(w_in=r_in, w_pool=r_pool.reshape(w_pool.shape), w_glu=r_glu, w_out=r_out, w_up=r_up, w_down=r_down)

    order = ("pool_scale", "ssm_d", "g_pre_mix", "g_post_mix", "g_pre_ffn", "g_post_ffn", "lam", "bb", "cs", "w_conv")
    pieces = [jnp.stack(small[k], axis=0) for k in order]
    flat = jnp.concatenate([q.reshape(-1) for q in pieces])
    unit = nl * N_CHIPS * SUBLANES * 1024
    padded = -(-flat.shape[0] // unit) * unit
    flat = jnp.concatenate([flat, jnp.zeros((padded - flat.shape[0],), F32)])
    vec = flat.reshape(nl, N_CHIPS, padded // (nl * N_CHIPS * 1024), 1024)
    vec = _gather_chips(_reduce_to_shards([vec], "small"), "gather_small")[0].reshape(-1)
    red, pos = {}, 0
    for k, q in zip(order, pieces):
        red[k] = vec[pos:pos + q.size].reshape(q.shape)
        pos += q.size
    for k in ("pool_scale", "ssm_d", "g_pre_mix", "g_post_mix", "g_pre_ffn", "g_post_ffn"):
        grads[k] = red[k][:, 0, :]
    dlam = red["lam"].reshape(nl, ndir, 2, gp)
    dbb = red["bb"].reshape(nl, ndir, 2, nh, gp)
    d_are, d_aim, d_ldt, d_bre, d_bim = _disc_bwd(
        a_re2, a_im2, logdt2, b_re2, b_im2, dlam[:, :, 0].reshape(rows, gp), dlam[:, :, 1].reshape(rows, gp),
        dbb[:, :, 0].reshape(rows, nh, gp), dbb[:, :, 1].reshape(rows, nh, gp), nstate, "s5_discretise_bwd")
    grads["ssm_a_re"] = d_are.reshape(ssm_a_re.shape)
    grads["ssm_a_im"] = d_aim.reshape(ssm_a_im.shape)
    grads["ssm_log_dt"] = d_ldt[:, :ng].reshape(ssm_log_dt.shape)
    grads["ssm_b_re"] = jnp.transpose(d_bre, (0, 2, 1)).reshape(ssm_b_re.shape)
    grads["ssm_b_im"] = jnp.transpose(d_bim, (0, 2, 1)).reshape(ssm_b_im.shape)
    grads["ssm_c_re"] = red["cs"][:, :, 0]
    grads["ssm_c_im"] = -red["cs"][:, :, 1]
    conv_cols = dff2 // N_CHIPS
    grads["w_conv"] = lax.dynamic_slice_in_dim(red["w_conv"], chip * conv_cols, conv_cols, axis=2).reshape(w_conv.shape)

    delta, new_m, new_v = {}, {}, {}
    for k in WEIGHT_NAMES:
        delta[k], new_m[k], new_v[k] = _adamw(weights[k], grads[k], mom1[k], mom2[k], f"adamw_{k}")
    return (loss, grad_x, *[grads[k] for k in WEIGHT_NAMES], *[delta[k] for k in WEIGHT_NAMES],
            *[new_m[k] for k in WEIGHT_NAMES], *[new_v[k] for k in WEIGHT_NAMES])
```

```python
import math

import jax
import jax.numpy as jnp
from jax import lax
from jax.experimental import pallas as pl
from jax.experimental.pallas import tpu as pltpu

F32 = jnp.float32
MXU_DTYPE = jnp.bfloat16
COMM_DTYPE = jnp.bfloat16
HIGHEST = lax.Precision.HIGHEST
VMEM_LIMIT_BYTES = 48 * 1024 * 1024
LANES = 128
SUBLANES = 8
N_CHIPS = 4
N_DEV = 8

EPS = 1e-6
GRID_W = 64
POOL_WINDOWS = (2, 4, 8, 16)
ADAM_LR = 0.001
ADAM_B1 = 0.9
ADAM_B2 = 0.999
ADAM_EPS = 1e-08
ADAM_WD = 0.01
ADAM_STEP = 10
GELU_C0 = math.sqrt(2.0 / math.pi)
GELU_C1 = 0.044715

SDS = jax.ShapeDtypeStruct
ANY = pl.BlockSpec(memory_space=pl.ANY)
MESH = pl.DeviceIdType.MESH


def _cparams(*sem):
    return pltpu.CompilerParams(dimension_semantics=sem if sem else None, vmem_limit_bytes=VMEM_LIMIT_BYTES)


def _pick(n, cands):
    for cand in cands:
        if n % cand == 0:
            return cand
    return n


def _row_tile(n_ctx, n):
    return math.gcd(math.gcd(n_ctx, n - n_ctx), 256)


_DIMS = {"nn": (((1,), (0,)), ((), ())), "nt": (((1,), (1,)), ((), ())), "tn": (((0,), (0,)), ((), ()))}
_TM = (1088, 1024, 512, 384, 256, 128, 64, 32, 16, 8)
_TN = (1024, 1408, 512, 384, 256, 128)
_TK = (1024, 1088, 512, 1408, 384, 256, 128, 64, 32, 16, 8)


def _chunk_of(idx, per, chunks):
    out = 0
    for q in range(1, chunks):
        out = out + (idx >= q * per).astype(jnp.int32)
    return out


def _within(idx, per, chunks):
    return idx - per * _chunk_of(idx, per, chunks)


def _mm(a, b, mode, out_dtype, name, a_idx=None, b_idx=None, a_cols=None, b_chunks=None, out_chunks=None):
    a2, b2 = a.shape[-2:], b.shape[-2:]
    if b_chunks is not None:
        assert mode in ("nn", "nt") and b.shape[-3] == b_chunks
        b2 = (b2[0], b2[1] * b_chunks)
    alast = a2[1] if a_cols is None else a_cols[1]
    if mode == "nn":
        m, k, n = a2[0], alast, b2[1]
        assert b2[0] == k
    elif mode == "nt":
        m, k, n = a2[0], alast, b2[0]
        assert b2[1] == k
    else:
        k, m, n = a2[0], alast, b2[1]
        assert b2[0] == k
    n_unit = n // (out_chunks or 1) // (b_chunks if b_chunks and mode == "nn" else 1)
    k_unit = k // (b_chunks if b_chunks and mode == "nt" else 1)
    tm, tn, tk = _pick(m, _TM), _pick(n_unit, _TN), _pick(k_unit, _TK)
    nk = k // tk
    a_lane_tile = tm if mode == "tn" else tk
    off = 0
    if a_cols is not None:
        assert a_cols[0] % a_lane_tile == 0
        off = a_cols[0] // a_lane_tile

    def body(a_ref, b_ref, o_ref, acc_ref):
        kk = pl.program_id(2)

        @pl.when(kk == 0)
        def _():
            acc_ref[...] = jnp.zeros(acc_ref.shape, F32)

        acc_ref[...] += lax.dot_general(a_ref[...].astype(MXU_DTYPE), b_ref[...].astype(MXU_DTYPE), _DIMS[mode],
                                        preferred_element_type=F32)

        @pl.when(kk == nk - 1)
        def _():
            o_ref[...] = acc_ref[...].astype(o_ref.dtype)

    if mode == "tn":
        a_blk, a_map = (tk, tm), (lambda i, j, kk: (kk, i + off))
    else:
        a_blk, a_map = (tm, tk), (lambda i, j, kk: (i, kk + off))
    if mode == "nt":
        b_blk, b_map = (tn, tk), (lambda i, j, kk: (j, kk))
        if b_chunks is not None:
            per = k // b_chunks // tk
            b_blk, b_map = (None, tn, tk), (lambda i, j, kk: (_chunk_of(kk, per, b_chunks), j, _within(kk, per, b_chunks)))
    else:
        b_blk, b_map = (tk, tn), (lambda i, j, kk: (kk, j))
        if b_chunks is not None:
            per = n // b_chunks // tn
            b_blk, b_map = (None, tk, tn), (lambda i, j, kk: (_chunk_of(j, per, b_chunks), kk, _within(j, per, b_chunks)))
    if a_idx is not None:
        a_blk, a_map0 = (None,) + a_blk, a_map
        a_map = lambda i, j, kk: (a_idx,) + a_map0(i, j, kk)
    if b_idx is not None:
        b_blk, b_map0 = (None,) + b_blk, b_map
        b_map = lambda i, j, kk: (b_idx,) + b_map0(i, j, kk)
    o_blk, o_map, o_shape = (tm, tn), (lambda i, j, kk: (i, j)), (m, n)
    if out_chunks is not None:
        oper = n // out_chunks // tn
        o_map = lambda i, j, kk: (_chunk_of(j, oper, out_chunks), i, _within(j, oper, out_chunks))
        o_blk, o_shape = (None, tm, tn), (out_chunks, m, n // out_chunks)
    return pl.pallas_call(
        body, grid=(m // tm, n // tn, nk),
        in_specs=[pl.BlockSpec(a_blk, a_map), pl.BlockSpec(b_blk, b_map)],
        out_specs=pl.BlockSpec(o_blk, o_map),
        out_shape=SDS(o_shape, out_dtype),
        scratch_shapes=[pltpu.VMEM((tm, tn), F32)],
        compiler_params=_cparams("parallel", "parallel", "arbitrary"), name=name)(a, b)


def _seg_map(nbc):
    return lambda i: (jnp.where(i < nbc, 0, 1), 0, 0)


def _rstd(v):
    return lax.rsqrt(jnp.mean(v * v, axis=-1, keepdims=True) + EPS)


def _norm_mod_fwd(x, g, mods, sh, sc, n_ctx, name):
    n, d = x.shape
    tm = _row_tile(n_ctx, n)
    nbc = n_ctx // tm

    def body(x_ref, g_ref, m_ref, h_ref):
        xv = x_ref[...]
        hn = xv * _rstd(xv) * g_ref[...]
        h_ref[...] = (hn * (1.0 + m_ref[0, sc:sc + 1, :]) + m_ref[0, sh:sh + 1, :]).astype(h_ref.dtype)

    row = pl.BlockSpec((tm, d), lambda i: (i, 0))
    return pl.pallas_call(
        body, grid=(n // tm,),
        in_specs=[row, pl.BlockSpec((1, d), lambda i: (0, 0)), pl.BlockSpec((1, 8, d), _seg_map(nbc))],
        out_specs=row, out_shape=SDS((n, d), MXU_DTYPE), compiler_params=_cparams("parallel"), name=name)(x, g, mods)


def _gate_res_fwd(x, f, g, mods, gi, n_ctx, name):
    n, d = x.shape
    tm = _row_tile(n_ctx, n)
    nbc = n_ctx // tm

    def body(x_ref, f_ref, g_ref, m_ref, o_ref):
        fv = f_ref[...]
        o_ref[...] = x_ref[...] + m_ref[0, gi:gi + 1, :] * (fv * _rstd(fv) * g_ref[...])

    row = pl.BlockSpec((tm, d), lambda i: (i, 0))
    return pl.pallas_call(
        body, grid=(n // tm,),
        in_specs=[row, row, pl.BlockSpec((1, d), lambda i: (0, 0)), pl.BlockSpec((1, 8, d), _seg_map(nbc))],
        out_specs=row, out_shape=SDS((n, d), F32), compiler_params=_cparams("parallel"), name=name)(x, f, g, mods)


def _gate_res_bwd(dx, f, g, mods, gi, n_ctx, name):
    n, d = dx.shape
    tm = _row_tile(n_ctx, n)
    nbc = n_ctx // tm

    def body(dx_ref, f_ref, g_ref, m_ref, df_ref, dgate_ref, dg_ref):
        i = pl.program_id(0)

        @pl.when(i == 0)
        def _():
            dg_ref[...] = jnp.zeros(dg_ref.shape, F32)

        @pl.when(jnp.logical_or(i == 0, i == nbc))
        def _():
            dgate_ref[...] = jnp.zeros(dgate_ref.shape, F32)

        dxv, fv, gv = dx_ref[...], f_ref[...], g_ref[...]
        rs = _rstd(fv)
        nv = fv * rs
        dgate_ref[0] += jnp.sum(dxv * (nv * gv), axis=0, keepdims=True)
        dout = dxv * m_ref[0, gi:gi + 1, :]
        dg_ref[...] += jnp.sum(dout * nv, axis=0, keepdims=True)
        dn = dout * gv
        df_ref[...] = (rs * (dn - nv * jnp.mean(dn * nv, axis=-1, keepdims=True))).astype(df_ref.dtype)

    row = pl.BlockSpec((tm, d), lambda i: (i, 0))
    vec = pl.BlockSpec((1, d), lambda i: (0, 0))
    return pl.pallas_call(
        body, grid=(n // tm,),
        in_specs=[row, row, vec, pl.BlockSpec((1, 8, d), _seg_map(nbc))],
        out_specs=[row, pl.BlockSpec((1, 1, d), _seg_map(nbc)), vec],
        out_shape=[SDS((n, d), MXU_DTYPE), SDS((2, 1, d), F32), SDS((1, d), F32)],
        compiler_params=_cparams("arbitrary"), name=name)(dx, f, g, mods)


def _norm_mod_bwd(dh, x, g, mods, sh, sc, dx_res, n_ctx, name):
    n, d = x.shape
    tm = _row_tile(n_ctx, n)
    nbc = n_ctx // tm

    def body(dh_ref, x_ref, g_ref, m_ref, r_ref, dx_ref, dss_ref, dg_ref):
        i = pl.program_id(0)

        @pl.when(i == 0)
        def _():
            dg_ref[...] = jnp.zeros(dg_ref.shape, F32)

        @pl.when(jnp.logical_or(i == 0, i == nbc))
        def _():
            dss_ref[...] = jnp.zeros(dss_ref.shape, F32)

        dhv, xv, gv = dh_ref[...], x_ref[...], g_ref[...]
        rs = _rstd(xv)
        nv = xv * rs
        dss_ref[0, 0:1, :] += jnp.sum(dhv, axis=0, keepdims=True)
        dss_ref[0, 1:2, :] += jnp.sum(dhv * (nv * gv), axis=0, keepdims=True)
        dhn = dhv * (1.0 + m_ref[0, sc:sc + 1, :])
        dg_ref[...] += jnp.sum(dhn * nv, axis=0, keepdims=True)
        dn = dhn * gv
        dx_ref[...] = r_ref[...] + rs * (dn - nv * jnp.mean(dn * nv, axis=-1, keepdims=True))

    row = pl.BlockSpec((tm, d), lambda i: (i, 0))
    vec = pl.BlockSpec((1, d), lambda i: (0, 0))
    return pl.pallas_call(
        body, grid=(n // tm,),
        in_specs=[row, row, vec, pl.BlockSpec((1, 8, d), _seg_map(nbc)), row],
        out_specs=[row, pl.BlockSpec((1, 2, d), _seg_map(nbc)), vec],
        out_shape=[SDS((n, d), F32), SDS((2, 2, d), F32), SDS((1, d), F32)],
        compiler_params=_cparams("arbitrary"), name=name)(dh, x, g, mods, dx_res)


def _loss_grad(xc, target, n_ctx, name):
    n, d = xc.shape
    tm = _row_tile(n_ctx, n)
    nbc = n_ctx // tm
    nb = n // tm

    def body(x_ref, t_ref, dx_ref, l_ref, acc_ref):
        i = pl.program_id(0)

        @pl.when(i == 0)
        def _():
            acc_ref[...] = jnp.zeros(acc_ref.shape, F32)

        @pl.when(i < nbc)
        def _():
            dx_ref[...] = jnp.zeros(dx_ref.shape, F32)

        @pl.when(i >= nbc)
        def _():
            diff = x_ref[...] - t_ref[...]
            dx_ref[...] = diff * (1.0 / d)
            acc_ref[...] += jnp.sum(diff * diff, axis=0, keepdims=True)

        @pl.when(i == nb - 1)
        def _():
            l_ref[...] = jnp.full(l_ref.shape, (0.5 / d) * jnp.sum(acc_ref[...]), F32)

    row = pl.BlockSpec((tm, d), lambda i: (i, 0))
    return pl.pallas_call(
        body, grid=(nb,),
        in_specs=[row, pl.BlockSpec((tm, d), lambda i: (jnp.maximum(i - nbc, 0), 0))],
        out_specs=[row, pl.BlockSpec((SUBLANES, LANES), lambda i: (0, 0))],
        out_shape=[SDS((n, d), F32), SDS((SUBLANES, LANES), F32)],
        scratch_shapes=[pltpu.VMEM((1, d), F32)],
        compiler_params=_cparams("arbitrary"), name=name)(xc, target)


POOL_PAD = 16


def _pool(src, pool_width, pool_group, n_ctx, bwd, out_dtype, name):
    n = src.shape[0]
    n_lat = n - n_ctx
    gb = pool_group // LANES
    segs = ((0, n_ctx, POOL_PAD), (n_ctx, n_lat, 2 * POOL_PAD + n_ctx))
    total = 3 * POOL_PAD + n

    def body(s_ref, o_ref, scr):
        j = pl.program_id(0)
        for base in (0, POOL_PAD + n_ctx, 2 * POOL_PAD + n):
            scr[pl.ds(base, POOL_PAD), :] = jnp.zeros((POOL_PAD, LANES), F32)
        for gi, w in enumerate(POOL_WINDOWS):
            @pl.when(jnp.logical_and(j >= gi * gb, j < (gi + 1) * gb))
            def _(w=w):
                half = w // 2
                offs = range(-half + 1, half + 1) if bwd else range(-half, half)
                for row0, nseg, base in segs:
                    ch = math.gcd(nseg, 256)

                    def count(c0):
                        t = c0 + lax.broadcasted_iota(jnp.int32, (ch, LANES), 0)
                        return (jnp.minimum(t + half, nseg) - jnp.maximum(t - half, 0)).astype(F32)

                    def fill(ci, carry):
                        c0 = pl.multiple_of(ci * ch, ch)
                        v = s_ref[pl.ds(row0 + c0, ch), :]
                        scr[pl.ds(base + c0, ch), :] = v / count(c0) if bwd else v
                        return carry

                    def window(ci, carry):
                        c0 = pl.multiple_of(ci * ch, ch)
                        acc = jnp.zeros((ch, LANES), F32)
                        for off in offs:
                            acc = acc + scr[pl.ds(c0 + (base + off), ch), :]
                        v = s_ref[pl.ds(row0 + c0, ch), :]
                        res = acc - v if bwd else acc / count(c0) - v
                        o_ref[pl.ds(row0 + c0, ch), :] = res.astype(o_ref.dtype)
                        return carry

                    lax.fori_loop(0, nseg // ch, fill, 0)
                    lax.fori_loop(0, nseg // ch, window, 0)

    blk = pl.BlockSpec((n, LANES), lambda j: (0, j))
    return pl.pallas_call(
        body, grid=(pool_width // LANES,), in_specs=[blk], out_specs=blk,
        out_shape=SDS((n, pool_width), out_dtype), scratch_shapes=[pltpu.VMEM((total, LANES), F32)],
        compiler_params=_cparams("parallel"), name=name)(src)


def _pool_proj_fwd(p, wp, l, scale, name):
    n, pw = p.shape
    ng, c = wp.shape[1], wp.shape[2]
    tm = _pick(n, _TM)

    def body(p_ref, w_ref, s_ref, o_ref):
        y = jnp.dot(p_ref[...], w_ref[...].astype(MXU_DTYPE), preferred_element_type=F32)
        o_ref[...] = (y * s_ref[...]).astype(o_ref.dtype)

    return pl.pallas_call(
        body, grid=(ng, n // tm),
        in_specs=[pl.BlockSpec((tm, c), lambda g, i: (i, g)), pl.BlockSpec((None, None, c, c), lambda g, i: (l, g, 0, 0)),
                  pl.BlockSpec((1, c), lambda g, i: (0, g))],
        out_specs=pl.BlockSpec((tm, c), lambda g, i: (i, g)), out_shape=SDS((n, pw), MXU_DTYPE),
        compiler_params=_cparams("parallel", "parallel"), name=name)(p, wp, scale)


def _pool_proj_bwd(p, dcat, wp, l, scale, name):
    n, pw = p.shape
    ng, c = wp.shape[1], wp.shape[2]
    tm = _pick(n, _TM)

    def body(p_ref, dy_ref, w_ref, s_ref, dp_ref, ds_ref, dw_ref):
        i = pl.program_id(1)

        @pl.when(i == 0)
        def _():
            ds_ref[...] = jnp.zeros(ds_ref.shape, F32)
            dw_ref[...] = jnp.zeros(dw_ref.shape, F32)

        pv, wv, dy = p_ref[...], w_ref[...].astype(MXU_DTYPE), dy_ref[...]
        y = jnp.dot(pv, wv, preferred_element_type=F32)
        ds_ref[...] += jnp.sum(dy * y, axis=0, keepdims=True)
        dpw = (dy * s_ref[...]).astype(MXU_DTYPE)
        dp_ref[...] = lax.dot_general(dpw, wv, _DIMS["nt"], preferred_element_type=F32)
        dw_ref[0] += lax.dot_general(pv, dpw, _DIMS["tn"], preferred_element_type=F32)

    return pl.pallas_call(
        body, grid=(ng, n // tm),
        in_specs=[pl.BlockSpec((tm, c), lambda g, i: (i, g)), pl.BlockSpec((tm, c), lambda g, i: (i, g)),
                  pl.BlockSpec((None, None, c, c), lambda g, i: (l, g, 0, 0)), pl.BlockSpec((1, c), lambda g, i: (0, g))],
        out_specs=[pl.BlockSpec((tm, c), lambda g, i: (i, g)), pl.BlockSpec((1, c), lambda g, i: (0, g)),
                   pl.BlockSpec((1, c, c), lambda g, i: (g, 0, 0))],
        out_shape=[SDS((n, pw), F32), SDS((1, pw), F32), SDS((ng, c, c), F32)],
        compiler_params=_cparams("arbitrary", "arbitrary"), name=name)(p, dcat, wp, scale)


def _disc_math(a_re, a_im, logdt, b_re, b_im):
    dt = jnp.exp(logdt)
    mag = jnp.exp(a_re * dt)
    lam_re = mag * jnp.cos(a_im * dt)
    lam_im = mag * jnp.sin(a_im * dt)
    denom = a_re * a_re + a_im * a_im
    nr, ni = lam_re - 1.0, lam_im
    f_re = ((nr * a_re + ni * a_im) / denom)[:, None, :]
    f_im = ((ni * a_re - nr * a_im) / denom)[:, None, :]
    return lam_re, lam_im, f_re * b_re - f_im * b_im, f_re * b_im + f_im * b_re


def _disc_fwd(a_re, a_im, logdt, b_re, b_im, name):
    def body(ar, ai, ld, br, bi, o_lr, o_li, o_br, o_bi):
        lr, li, bbr, bbi = _disc_math(ar[...], ai[...], ld[...], br[...], bi[...])
        o_lr[...] = lr
        o_li[...] = li
        o_br[...] = bbr
        o_bi[...] = bbi

    return pl.pallas_call(
        body, out_shape=[SDS(a_re.shape, F32), SDS(a_re.shape, F32), SDS(b_re.shape, F32), SDS(b_re.shape, F32)],
        compiler_params=_cparams(), name=name)(a_re, a_im, logdt, b_re, b_im)


def _disc_bwd(a_re, a_im, logdt, b_re, b_im, d_lr, d_li, d_bbr, d_bbi, group, name):
    rows, gp = a_re.shape

    def body(ar, ai, ld, br, bi, g_lr, g_li, g_br, g_bi, o_ar, o_ai, o_ld, o_br, o_bi):
        _, vjp = jax.vjp(_disc_math, ar[...], ai[...], ld[...], br[...], bi[...])
        dar, dai, dld, dbr, dbi = vjp((g_lr[...], g_li[...], g_br[...], g_bi[...]))
        o_ar[...] = dar
        o_ai[...] = dai
        state = lax.broadcasted_iota(jnp.int32, (gp, LANES), 0)
        first = lax.broadcasted_iota(jnp.int32, (gp, LANES), 1) * group
        sel = jnp.logical_and(state >= first, state < first + group).astype(F32)
        o_ld[...] = jnp.dot(dld, sel, precision=HIGHEST, preferred_element_type=F32)
        o_br[...] = dbr
        o_bi[...] = dbi

    return pl.pallas_call(
        body, out_shape=[SDS(a_re.shape, F32), SDS(a_re.shape, F32), SDS((rows, LANES), F32),
                         SDS(b_re.shape, F32), SDS(b_re.shape, F32)],
        compiler_params=_cparams(), name=name)(a_re, a_im, logdt, b_re, b_im, d_lr, d_li, d_bbr, d_bbi)


def _scan_maps(nbc, nb):
    nbl = nb - nbc
    fwd0 = lambda i: (i, 0, 0)
    fwd1 = lambda i: (jnp.where(i < nbc, nbc - 1 - i, nb - 1 - (i - nbc)), 0, 0)
    adj0 = lambda i: (nb - 1 - i, 0, 0)
    adj1 = lambda i: (jnp.where(i < nbl, nbc + i, i - nbl), 0, 0)
    return fwd0, fwd1, adj0, adj1


def _scan_fwd(bu0, bu1, lam, n_ctx, name):
    n, s2, _ = bu0.shape
    s = s2 // 2
    tt = math.gcd(math.gcd(n_ctx, n - n_ctx), 128)
    nbc, nb = n_ctx // tt, n // tt
    fwd0, fwd1, _, _ = _scan_maps(nbc, nb)

    def body(b0_ref, b1_ref, lam_ref, h0_ref, h1_ref, st_ref):
        @pl.when(pl.program_id(0) == 0)
        def _():
            st_ref[...] = jnp.zeros(st_ref.shape, F32)

        lr0, li0, lr1, li1 = lam_ref[0], lam_ref[1], lam_ref[2], lam_ref[3]

        def step(j, carry):
            h0r, h0i, h1r, h1i = carry
            t1 = tt - 1 - j
            n0r = lr0 * h0r - li0 * h0i + b0_ref[j, 0:s, :]
            n0i = lr0 * h0i + li0 * h0r + b0_ref[j, s:s2, :]
            n1r = lr1 * h1r - li1 * h1i + b1_ref[t1, 0:s, :]
            n1i = lr1 * h1i + li1 * h1r + b1_ref[t1, s:s2, :]
            h0_ref[j, 0:s, :] = n0r
            h0_ref[j, s:s2, :] = n0i
            h1_ref[t1, 0:s, :] = n1r
            h1_ref[t1, s:s2, :] = n1i
            return n0r, n0i, n1r, n1i

        out = lax.fori_loop(0, tt, step, (st_ref[0], st_ref[1], st_ref[2], st_ref[3]), unroll=2)
        for q in range(4):
            st_ref[q] = out[q]

    blk = (tt, s2, LANES)
    return pl.pallas_call(
        body, grid=(nb,),
        in_specs=[pl.BlockSpec(blk, fwd0), pl.BlockSpec(blk, fwd1), pl.BlockSpec((4, s, LANES), lambda i: (0, 0, 0))],
        out_specs=[pl.BlockSpec(blk, fwd0), pl.BlockSpec(blk, fwd1)],
        out_shape=[SDS(bu0.shape, F32), SDS(bu1.shape, F32)],
        scratch_shapes=[pltpu.VMEM((4, s, LANES), F32)],
        compiler_params=_cparams("arbitrary"), name=name)(bu0, bu1, lam)


def _scan_bwd(dh0, dh1, h0, h1, lam, n_ctx, name):
    n, s2, _ = dh0.shape
    s = s2 // 2
    tt = math.gcd(math.gcd(n_ctx, n - n_ctx), 128)
    nbc, nb = n_ctx // tt, n // tt
    _, _, adj0, adj1 = _scan_maps(nbc, nb)

    def body(d0_ref, d1_ref, h0_ref, h1_ref, lam_ref, a0_ref, a1_ref, dl_ref, st_ref, acc_ref):
        i = pl.program_id(0)

        @pl.when(i == 0)
        def _():
            st_ref[...] = jnp.zeros(st_ref.shape, F32)
            acc_ref[...] = jnp.zeros(acc_ref.shape, F32)

        lr0, li0, lr1, li1 = lam_ref[0], lam_ref[1], lam_ref[2], lam_ref[3]

        def step(j, carry):
            a0r, a0i, a1r, a1i, c0r, c0i, c1r, c1i = carry
            t0 = tt - 1 - j
            g0r, g0i = h0_ref[t0, 0:s, :], h0_ref[t0, s:s2, :]
            g1r, g1i = h1_ref[j, 0:s, :], h1_ref[j, s:s2, :]
            c0r = c0r + a0r * g0r + a0i * g0i
            c0i = c0i + a0i * g0r - a0r * g0i
            c1r = c1r + a1r * g1r + a1i * g1i
            c1i = c1i + a1i * g1r - a1r * g1i
            n0r = lr0 * a0r + li0 * a0i + d0_ref[t0, 0:s, :]
            n0i = lr0 * a0i - li0 * a0r + d0_ref[t0, s:s2, :]
            n1r = lr1 * a1r + li1 * a1i + d1_ref[j, 0:s, :]
            n1i = lr1 * a1i - li1 * a1r + d1_ref[j, s:s2, :]
            a0_ref[t0, 0:s, :] = n0r
            a0_ref[t0, s:s2, :] = n0i
            a1_ref[j, 0:s, :] = n1r
            a1_ref[j, s:s2, :] = n1i
            return n0r, n0i, n1r, n1i, c0r, c0i, c1r, c1i

        init = tuple(st_ref[q] for q in range(4)) + tuple(acc_ref[q] for q in range(4))
        out = lax.fori_loop(0, tt, step, init, unroll=2)
        for q in range(4):
            st_ref[q] = out[q]
            acc_ref[q] = out[4 + q]

        @pl.when(i == nb - 1)
        def _():
            for q in range(4):
                dl_ref[q] = out[4 + q]

    blk = (tt, s2, LANES)
    small = pl.BlockSpec((4, s, LANES), lambda i: (0, 0, 0))
    return pl.pallas_call(
        body, grid=(nb,),
        in_specs=[pl.BlockSpec(blk, adj0), pl.BlockSpec(blk, adj1), pl.BlockSpec(blk, adj0), pl.BlockSpec(blk, adj1), small],
        out_specs=[pl.BlockSpec(blk, adj0), pl.BlockSpec(blk, adj1), small],
        out_shape=[SDS(dh0.shape, F32), SDS(dh1.shape, F32), SDS((4, s, LANES), F32)],
        scratch_shapes=[pltpu.VMEM((4, s, LANES), F32), pltpu.VMEM((4, s, LANES), F32)],
        compiler_params=_cparams("arbitrary"), name=name)(dh0, dh1, h0, h1, lam)


def _gelu(v):
    th = jnp.tanh(GELU_C0 * (v + GELU_C1 * v * v * v))
    return 0.5 * v * (1.0 + th), th


def _ssm_head_fwd(y, u, ssm_d, wg, l, name):
    n, sw = y.shape
    ucol = u.shape[1] // sw - 1
    tm = _pick(n, _TM)

    def body(y_ref, u_ref, d_ref, w_ref, o_ref):
        act, _ = _gelu(y_ref[...] + d_ref[...] * u_ref[...])
        q = jnp.dot(act.astype(MXU_DTYPE), w_ref[...].astype(MXU_DTYPE), preferred_element_type=F32)
        o_ref[...] = (act * jax.nn.sigmoid(q)).astype(o_ref.dtype)

    row = pl.BlockSpec((tm, sw), lambda i: (i, 0))
    return pl.pallas_call(
        body, grid=(n // tm,),
        in_specs=[row, pl.BlockSpec((tm, sw), lambda i: (i, ucol)), pl.BlockSpec((1, sw), lambda i: (0, 0)),
                  pl.BlockSpec((None, sw, sw), lambda i: (l, 0, 0))],
        out_specs=row, out_shape=SDS((n, sw), MXU_DTYPE), compiler_params=_cparams("parallel"), name=name)(y, u, ssm_d, wg)


def _ssm_head_bwd(dcat, y, u, ssm_d, wg, l, name):
    n, sw = y.shape
    ucol = u.shape[1] // sw - 1
    tm = _pick(n, _TM)

    def body(do_ref, y_ref, u_ref, d_ref, w_ref, dy_ref, du_ref, act_ref, dq_ref, dd_ref):
        @pl.when(pl.program_id(0) == 0)
        def _():
            dd_ref[...] = jnp.zeros(dd_ref.shape, F32)

        uv, dv, do = u_ref[...], d_ref[...], do_ref[...]
        yf = y_ref[...] + dv * uv
        act, th = _gelu(yf)
        wv = w_ref[...].astype(MXU_DTYPE)
        sg = jax.nn.sigmoid(jnp.dot(act.astype(MXU_DTYPE), wv, preferred_element_type=F32))
        dq = (do * act * sg * (1.0 - sg)).astype(MXU_DTYPE)
        dact = do * sg + lax.dot_general(dq, wv, _DIMS["nt"], preferred_element_type=F32)
        dgelu = 0.5 * (1.0 + th) + 0.5 * yf * (1.0 - th * th) * GELU_C0 * (1.0 + 3.0 * GELU_C1 * yf * yf)
        dyf = dact * dgelu
        dy_ref[...] = dyf.astype(dy_ref.dtype)
        du_ref[...] = dyf * dv
        act_ref[...] = act.astype(act_ref.dtype)
        dq_ref[...] = dq
        dd_ref[...] += jnp.sum(dyf * uv, axis=0, keepdims=True)

    row = pl.BlockSpec((tm, sw), lambda i: (i, 0))
    last = pl.BlockSpec((tm, sw), lambda i: (i, ucol))
    vec = pl.BlockSpec((1, sw), lambda i: (0, 0))
    return pl.pallas_call(
        body, grid=(n // tm,),
        in_specs=[last, row, last, vec, pl.BlockSpec((None, sw, sw), lambda i: (l, 0, 0))],
        out_specs=[row, row, row, row, vec],
        out_shape=[SDS((n, sw), MXU_DTYPE), SDS((n, sw), F32), SDS((n, sw), MXU_DTYPE), SDS((n, sw), MXU_DTYPE),
                   SDS((1, sw), F32)],
        compiler_params=_cparams("arbitrary"), name=name)(dcat, y, u, ssm_d, wg)


def _assemble_du(du_pool, du_dir, du_proj, name):
    n, pw = du_pool.shape
    sw = du_dir.shape[1]
    tm = _pick(n, _TM)

    def body(p_ref, a_ref, b_ref, o_ref):
        o_ref[:, 0:pw] = p_ref[...].astype(o_ref.dtype)
        o_ref[:, pw:pw + sw] = (a_ref[...] + b_ref[...]).astype(o_ref.dtype)

    return pl.pallas_call(
        body, grid=(n // tm,),
        in_specs=[pl.BlockSpec((tm, pw), lambda i: (i, 0)), pl.BlockSpec((tm, sw), lambda i: (i, 0)),
                  pl.BlockSpec((tm, sw), lambda i: (i, 0))],
        out_specs=pl.BlockSpec((tm, pw + sw), lambda i: (i, 0)), out_shape=SDS((n, pw + sw), MXU_DTYPE),
        compiler_params=_cparams("parallel"), name=name)(du_pool, du_dir, du_proj)


CONV_PAD = GRID_W + SUBLANES


def _conv_layout(n, n_ctx):
    return CONV_PAD, 2 * CONV_PAD + n_ctx, 3 * CONV_PAD + n


def _col_masks(ch):
    col = lax.broadcasted_iota(jnp.int32, (ch, LANES), 0) % GRID_W
    return col != 0, col != GRID_W - 1


def _fill_padded(scr, src_ref, n, n_ctx):
    base_c, base_l, total = _conv_layout(n, n_ctx)
    for base in (0, base_c + n_ctx, base_l + n - n_ctx):
        scr[pl.ds(base, CONV_PAD), :] = jnp.zeros((CONV_PAD, LANES), F32)
    for row0, nseg, base in ((0, n_ctx, base_c), (n_ctx, n - n_ctx, base_l)):
        ch = math.gcd(nseg, 512)

        def copy(ci, carry, row0=row0, base=base, ch=ch):
            c0 = pl.multiple_of(ci * ch, ch)
            scr[pl.ds(base + c0, ch), :] = src_ref[pl.ds(row0 + c0, ch), :]
            return carry

        lax.fori_loop(0, nseg // ch, copy, 0)


def _conv_ctx(scr, k_ref, base, c0, ch, sign):
    acc = scr[pl.ds(c0 + base, ch), :] * k_ref[4:5, :]
    acc = acc + scr[pl.ds(c0 + (base - sign), ch), :] * k_ref[3:4, :]
    return acc + scr[pl.ds(c0 + (base + sign), ch), :] * k_ref[5:6, :]


def _conv_lat(scr, k_ref, base, c0, ch, sign, m_l, m_r):
    cols = []
    for j in range(3):
        acc = None
        for i in range(3):
            off = sign * (GRID_W * (i - 1) + (j - 1))
            term = scr[pl.ds(c0 + (base + off), ch), :] * k_ref[3 * i + j:3 * i + j + 1, :]
            acc = term if acc is None else acc + term
        cols.append(acc)
    first, last = (m_l, m_r) if sign > 0 else (m_r, m_l)
    return cols[1] + jnp.where(first, cols[0], 0.0) + jnp.where(last, cols[2], 0.0)


def _conv_chunk(n_lat):
    return math.gcd(n_lat, 256)


def _conv_glu_fwd(z, wk, n_ctx, name):
    n, f2 = z.shape
    dff = f2 // 2
    nvt = dff // LANES
    n_lat = n - n_ctx
    base_c, base_l, total = _conv_layout(n, n_ctx)
    ch = _conv_chunk(n_lat)
    assert ch % GRID_W == 0

    def body(zv_ref, zg_ref, kv_ref, kg_ref, a_ref, sv, sg):
        _fill_padded(sv, zv_ref, n, n_ctx)
        _fill_padded(sg, zg_ref, n, n_ctx)
        cv = _conv_ctx(sv, kv_ref, base_c, 0, n_ctx, 1)
        cg = _conv_ctx(sg, kg_ref, base_c, 0, n_ctx, 1)
        a_ref[pl.ds(0, n_ctx), :] = (cv * cg * jax.nn.sigmoid(cg)).astype(a_ref.dtype)
        m_l, m_r = _col_masks(ch)

        def lat(ci, carry):
            c0 = pl.multiple_of(ci * ch, ch)
            cv = _conv_lat(sv, kv_ref, base_l, c0, ch, 1, m_l, m_r)
            cg = _conv_lat(sg, kg_ref, base_l, c0, ch, 1, m_l, m_r)
            a_ref[pl.ds(n_ctx + c0, ch), :] = (cv * cg * jax.nn.sigmoid(cg)).astype(a_ref.dtype)
            return carry

        lax.fori_loop(0, n_lat // ch, lat, 0)

    col = lambda shift: pl.BlockSpec((n, LANES), lambda j: (0, j + shift))
    kcol = lambda shift: pl.BlockSpec((9, LANES), lambda j: (0, j + shift))
    return pl.pallas_call(
        body, grid=(nvt,), in_specs=[col(0), col(nvt), kcol(0), kcol(nvt)], out_specs=col(0),
        out_shape=SDS((n, dff), MXU_DTYPE),
        scratch_shapes=[pltpu.VMEM((total, LANES), F32), pltpu.VMEM((total, LANES), F32)],
        compiler_params=_cparams("parallel"), name=name)(z, z, wk, wk)


def _conv_glu_bwd(z, da, wk, n_ctx, name):
    n, f2 = z.shape
    dff = f2 // 2
    nvt = dff // LANES
    n_lat = n - n_ctx
    base_c, base_l, total = _conv_layout(n, n_ctx)
    ch = _conv_chunk(n_lat)
    assert ch % GRID_W == 0
    ctx_taps = [(1, 0), (1, 1), (1, 2)]
    lat_taps = [(i, j) for i in range(3) for j in range(3)]

    def tap_sums(acc, scr, d, base, c0, rows, taps, masks):
        acc = list(acc)
        for i, j in taps:
            src = scr[pl.ds(c0 + (base + GRID_W * (i - 1) + (j - 1)), rows), :]
            if masks is not None and j != 1:
                src = jnp.where(masks[0] if j == 0 else masks[1], src, 0.0)
            acc[3 * i + j] = acc[3 * i + j] + jnp.sum((src * d).reshape(rows // SUBLANES, SUBLANES, LANES), axis=0)
        return acc

    def body(zv_ref, zg_ref, da_ref, kv_ref, kg_ref, dzv_ref, dzg_ref, dkv_ref, dkg_ref, a_ref, sv, sg, dv, dg):
        _fill_padded(sv, zv_ref, n, n_ctx)
        _fill_padded(sg, zg_ref, n, n_ctx)
        for base in (0, base_c + n_ctx, base_l + n_lat):
            dv[pl.ds(base, CONV_PAD), :] = jnp.zeros((CONV_PAD, LANES), F32)
            dg[pl.ds(base, CONV_PAD), :] = jnp.zeros((CONV_PAD, LANES), F32)
        m_l, m_r = _col_masks(ch)

        def first_pass(cv, cg, row, pad_row, rows):
            sig = jax.nn.sigmoid(cg)
            silu = cg * sig
            a_ref[pl.ds(row, rows), :] = (cv * silu).astype(a_ref.dtype)
            dav = da_ref[pl.ds(row, rows), :]
            dcv = dav * silu
            dcg = dav * cv * (sig * (1.0 + cg * (1.0 - sig)))
            dv[pl.ds(pad_row, rows), :] = dcv
            dg[pl.ds(pad_row, rows), :] = dcg
            return dcv, dcg

        zero = [jnp.zeros((SUBLANES, LANES), F32) for _ in range(9)]
        cv = _conv_ctx(sv, kv_ref, base_c, 0, n_ctx, 1)
        cg = _conv_ctx(sg, kg_ref, base_c, 0, n_ctx, 1)
        dcv, dcg = first_pass(cv, cg, 0, base_c, n_ctx)
        accv = tap_sums(zero, sv, dcv, base_c, 0, n_ctx, ctx_taps, None)
        accg = tap_sums(zero, sg, dcg, base_c, 0, n_ctx, ctx_taps, None)

        def lat1(ci, carry):
            accv, accg = carry
            c0 = pl.multiple_of(ci * ch, ch)
            cv = _conv_lat(sv, kv_ref, base_l, c0, ch, 1, m_l, m_r)
            cg = _conv_lat(sg, kg_ref, base_l, c0, ch, 1, m_l, m_r)
            dcv, dcg = first_pass(cv, cg, n_ctx + c0, base_l + c0, ch)
            accv = tap_sums(accv, sv, dcv, base_l, c0, ch, lat_taps, (m_l, m_r))
            accg = tap_sums(accg, sg, dcg, base_l, c0, ch, lat_taps, (m_l, m_r))
            return tuple(accv), tuple(accg)

        accv, accg = lax.fori_loop(0, n_lat // ch, lat1, (tuple(accv), tuple(accg)))
        for t in range(9):
            dkv_ref[t:t + 1, :] = jnp.sum(accv[t], axis=0, keepdims=True)
            dkg_ref[t:t + 1, :] = jnp.sum(accg[t], axis=0, keepdims=True)

        dzv_ref[pl.ds(0, n_ctx), :] = _conv_ctx(dv, kv_ref, base_c, 0, n_ctx, -1).astype(dzv_ref.dtype)
        dzg_ref[pl.ds(0, n_ctx), :] = _conv_ctx(dg, kg_ref, base_c, 0, n_ctx, -1).astype(dzg_ref.dtype)

        def lat2(ci, carry):
            c0 = pl.multiple_of(ci * ch, ch)
            dzv_ref[pl.ds(n_ctx + c0, ch), :] = _conv_lat(dv, kv_ref, base_l, c0, ch, -1, m_l, m_r).astype(dzv_ref.dtype)
            dzg_ref[pl.ds(n_ctx + c0, ch), :] = _conv_lat(dg, kg_ref, base_l, c0, ch, -1, m_l, m_r).astype(dzg_ref.dtype)
            return carry

        lax.fori_loop(0, n_lat // ch, lat2, 0)

    col = lambda shift: pl.BlockSpec((n, LANES), lambda j: (0, j + shift))
    kcol = lambda shift: pl.BlockSpec((9, LANES), lambda j: (0, j + shift))
    pad = pltpu.VMEM((total, LANES), F32)
    return pl.pallas_call(
        body, grid=(nvt,), in_specs=[col(0), col(nvt), col(0), kcol(0), kcol(nvt)],
        out_specs=[col(0), col(0), kcol(0), kcol(0), col(0)],
        out_shape=[SDS((n, dff), MXU_DTYPE), SDS((n, dff), MXU_DTYPE), SDS((9, dff), F32), SDS((9, dff), F32),
                   SDS((n, dff), MXU_DTYPE)],
        scratch_shapes=[pad, pad, pad, pad],
        compiler_params=_cparams("parallel"), name=name)(z, z, da, wk, wk)


def _silu(v):
    return v * jax.nn.sigmoid(v)


def _ada_fwd(cond, w_ada, b_shard, name):
    nl, d, cols = w_ada.shape
    tn = _pick(cols, (512, 256, 128))

    def body(c_ref, w_ref, b_ref, o_ref):
        o_ref[...] = jnp.dot(_silu(c_ref[...]), w_ref[...], precision=HIGHEST, preferred_element_type=F32) + b_ref[...]

    return pl.pallas_call(
        body, grid=(nl, cols // tn),
        in_specs=[pl.BlockSpec(cond.shape, lambda l, j: (0, 0)), pl.BlockSpec((None, d, tn), lambda l, j: (l, 0, j)),
                  pl.BlockSpec((None, 1, tn), lambda l, j: (l, 0, j))],
        out_specs=pl.BlockSpec((None, cond.shape[0], tn), lambda l, j: (l, 0, j)),
        out_shape=SDS((nl, cond.shape[0], cols), F32),
        compiler_params=_cparams("parallel", "parallel"), name=name)(cond, w_ada, b_shard)


def _ada_dw(cond, dmod, name):
    nl, rows, cols = dmod.shape
    d = cond.shape[1]
    tn = _pick(cols, (512, 256, 128))

    def body(c_ref, g_ref, o_ref):
        o_ref[...] = lax.dot_general(_silu(c_ref[...]), g_ref[...], _DIMS["tn"], precision=HIGHEST,
                                     preferred_element_type=F32)

    return pl.pallas_call(
        body, grid=(nl, cols // tn),
        in_specs=[pl.BlockSpec(cond.shape, lambda l, j: (0, 0)), pl.BlockSpec((None, rows, tn), lambda l, j: (l, 0, j))],
        out_specs=pl.BlockSpec((None, d, tn), lambda l, j: (l, 0, j)), out_shape=SDS((nl, d, cols), F32),
        compiler_params=_cparams("parallel", "parallel"), name=name)(cond, dmod)


def _ada_dcond(dmod, w_ada, name):
    nl, rows, cols = dmod.shape
    d = w_ada.shape[1]
    tn = _pick(cols, (512, 256, 128))

    def body(g_ref, w_ref, o_ref):
        @pl.when(jnp.logical_and(pl.program_id(0) == 0, pl.program_id(1) == 0))
        def _():
            o_ref[...] = jnp.zeros(o_ref.shape, F32)

        o_ref[...] += lax.dot_general(g_ref[...], w_ref[...], _DIMS["nt"], precision=HIGHEST, preferred_element_type=F32)

    return pl.pallas_call(
        body, grid=(nl, cols // tn),
        in_specs=[pl.BlockSpec((None, rows, tn), lambda l, j: (l, 0, j)), pl.BlockSpec((None, d, tn), lambda l, j: (l, 0, j))],
        out_specs=pl.BlockSpec((rows, d), lambda l, j: (0, 0)), out_shape=SDS((rows, d), F32),
        compiler_params=_cparams("arbitrary", "arbitrary"), name=name)(dmod, w_ada)


def _ada_rows(dmod_all, name):
    nd, nl, _, w = dmod_all.shape
    tn = _pick(w, (2048, 1024, 512, 256, 128))

    def body(g_ref, rows_ref, db_ref):
        ctx = g_ref[0, 0, 0:1, :]
        for b in range(1, nd):
            ctx = ctx + g_ref[b, 0, 0:1, :]
        total = ctx
        for b in range(nd):
            lat = g_ref[b, 0, 1:2, :]
            rows_ref[b:b + 1, :] = lat
            total = total + lat
        rows_ref[nd:nd + 1, :] = ctx
        rows_ref[nd + 1:16, :] = jnp.zeros((16 - nd - 1, tn), F32)
        db_ref[...] = total

    return pl.pallas_call(
        body, grid=(nl, w // tn),
        in_specs=[pl.BlockSpec((nd, 1, 2, tn), lambda l, j: (0, l, 0, j))],
        out_specs=[pl.BlockSpec((None, 16, tn), lambda l, j: (l, 0, j)), pl.BlockSpec((None, 1, tn), lambda l, j: (l, 0, j))],
        out_shape=[SDS((nl, 16, w), F32), SDS((nl, 1, w), F32)],
        compiler_params=_cparams("parallel", "parallel"), name=name)(dmod_all)


def _ada_dctx(parts, c_ctx, row, name):
    def body(p_ref, c_ref, o_ref):
        ds = p_ref[0, row:row + 1, :]
        for k in range(1, p_ref.shape[0]):
            ds = ds + p_ref[k, row:row + 1, :]
        cv = c_ref[...]
        sg = jax.nn.sigmoid(cv)
        o_ref[...] = ds * (sg * (1.0 + cv * (1.0 - sg)))

    return pl.pallas_call(body, out_shape=SDS(c_ctx.shape, F32), compiler_params=_cparams(), name=name)(parts, c_ctx)


ROW_BLOCK_BYTES = 1 << 20


def _as_rows(shape):
    size = math.prod(shape)
    cols = shape[-1] if len(shape) >= 2 and shape[-1] % LANES == 0 else _pick(size, (1024, 512, 256, 128))
    rows = size // cols
    fits = [t for t in (512, 256, 128, 64, 32, 16, 8) if t * cols * 4 <= ROW_BLOCK_BYTES]
    return rows, cols, _pick(rows, fits)


def _adamw(w, g, m, v, name):
    rows, cols, tr = _as_rows(w.shape)
    c1 = 1.0 / (1.0 - ADAM_B1 ** ADAM_STEP)
    c2 = 1.0 / (1.0 - ADAM_B2 ** ADAM_STEP)

    def body(w_ref, g_ref, m_ref, v_ref, d_ref, nm_ref, nv_ref):
        gv = g_ref[...]
        nm = ADAM_B1 * m_ref[...] + (1.0 - ADAM_B1) * gv
        nv = ADAM_B2 * v_ref[...] + (1.0 - ADAM_B2) * (gv * gv)
        nm_ref[...] = nm
        nv_ref[...] = nv
        d_ref[...] = -ADAM_LR * ((nm * c1) / (jnp.sqrt(nv * c2) + ADAM_EPS) + ADAM_WD * w_ref[...])

    blk = pl.BlockSpec((tr, cols), lambda i: (i, 0))
    outs = pl.pallas_call(
        body, grid=(rows // tr,), in_specs=[blk] * 4, out_specs=[blk] * 3, out_shape=[SDS((rows, cols), F32)] * 3,
        compiler_params=_cparams("parallel"), name=name)(*[t.reshape(rows, cols) for t in (w, g, m, v)])
    return tuple(o.reshape(w.shape) for o in outs)


def _tile_rows(rows, cols, dtype):
    size = jnp.dtype(dtype).itemsize
    fits = [t for t in (512, 256, 128, 64, 32, 16, 8) if t * cols * size <= ROW_BLOCK_BYTES and t * size >= 32]
    return _pick(rows, fits)


def _scalar_spec(grid, in_specs, out_specs):
    return pltpu.PrefetchScalarGridSpec(num_scalar_prefetch=1, grid=grid, in_specs=in_specs, out_specs=out_specs)


def _place_chunk(shard, k_idx, name):
    nl, rows, cols = shard.shape
    tr = _tile_rows(rows, cols, shard.dtype)

    def body(k_ref, s_ref, o_ref):
        o_ref[...] = s_ref[...]

    return pl.pallas_call(
        body, out_shape=SDS((nl, N_CHIPS, rows, cols), shard.dtype),
        grid_spec=_scalar_spec((nl, rows // tr), [pl.BlockSpec((None, tr, cols), lambda l, i, k: (l, i, 0))],
                               pl.BlockSpec((None, None, tr, cols), lambda l, i, k: (l, k[0], i, 0))),
        compiler_params=_cparams("parallel", "parallel"), name=name)(k_idx, shard)


def _pair_sum(grads, recv, c_idx, name):
    half, nch, rows, cols = recv.shape
    tr = _tile_rows(rows, cols, recv.dtype)

    def body(c_ref, g_ref, r_ref, o_ref):
        o_ref[...] = (g_ref[...].astype(F32) + r_ref[...].astype(F32)).astype(o_ref.dtype)

    blk = pl.BlockSpec((None, None, tr, cols), lambda h, q, i, c: (h, q, i, 0))
    return pl.pallas_call(
        body, out_shape=SDS(recv.shape, recv.dtype),
        grid_spec=_scalar_spec((half, nch, rows // tr),
                               [pl.BlockSpec((None, None, tr, cols), lambda h, q, i, c: (c[0] * half + h, q, i, 0)), blk], blk),
        compiler_params=_cparams("parallel", "parallel", "parallel"), name=name)(c_idx, grads, recv)


def _chip_sum(parts, recv, kc_idx, name):
    half, nch, rows, cols = parts.shape
    tr = _tile_rows(rows, cols, F32)

    def body(kc_ref, p_ref, r_ref, o_ref):
        acc = p_ref[...].astype(F32)
        for s in range(r_ref.shape[0]):
            acc = acc + r_ref[s].astype(F32)
        o_ref[...] = acc

    return pl.pallas_call(
        body, out_shape=SDS((2 * half, rows, cols), F32),
        grid_spec=_scalar_spec((half, rows // tr),
                               [pl.BlockSpec((None, None, tr, cols), lambda h, i, kc: (h, kc[0], i, 0)),
                                pl.BlockSpec((nch - 1, None, tr, cols), lambda h, i, kc: (0, h, i, 0))],
                               pl.BlockSpec((None, tr, cols), lambda h, i, kc: (kc[1] * half + h, i, 0))),
        compiler_params=_cparams("parallel", "parallel"), name=name)(kc_idx, parts, recv)


PIECE_BYTES = 3 << 20
MAX_PIECES = 16
PIECE_ROW_ALIGN = 16


def _coords():
    return lax.axis_index("x"), lax.axis_index("y"), lax.axis_index("c")


def _other_chips(x, y):
    return [(1 - x, y), (x, 1 - y), (1 - x, 1 - y)]


def _row_pieces(rows, nbytes):
    pieces = 1
    while (pieces < MAX_PIECES and nbytes // pieces > PIECE_BYTES and rows % (2 * pieces * PIECE_ROW_ALIGN) == 0):
        pieces *= 2
    step = rows // pieces
    return [pl.ds(i * step, step) for i in range(pieces)]


def _nbytes(shape, dtype):
    return math.prod(shape) * jnp.dtype(dtype).itemsize


def _offsets(counts):
    out, pos = [], 0
    for cnt in counts:
        out.append(pos)
        pos += cnt
    return out, pos


def _gather_chips(shards, placed, name):
    nt = len(shards)
    half = [s.shape[0] // 2 for s in shards]
    pieces = [_row_pieces(s.shape[1], _nbytes((h,) + s.shape[1:], s.dtype)) for s, h in zip(shards, half)]
    base, total = _offsets([len(p) for p in pieces])

    def body(*refs):
        s_refs, o_refs = refs[:nt], refs[2 * nt:3 * nt]
        ssem1, rsem1, ssem2, rsem2 = refs[3 * nt:]
        x, y, c = _coords()
        k = 2 * x + y
        chips = _other_chips(x, y)
        sends = []
        for t in range(nt):
            s_ref, o_ref = s_refs[t], o_refs[t]
            mine = pl.ds(c * half[t], half[t])
            for i, rs in enumerate(pieces[t]):
                for r, (px, py) in enumerate(chips):
                    q = 3 * (base[t] + i) + r
                    cp = pltpu.make_async_remote_copy(
                        src_ref=s_ref.at[mine, rs], dst_ref=o_ref.at[mine, k, rs], send_sem=ssem1.at[q],
                        recv_sem=rsem1.at[q], device_id=(px, py, c), device_id_type=MESH)
                    cp.start()
                    sends.append(cp)
        for t in range(nt):
            s_ref, o_ref = s_refs[t], o_refs[t]
            mine = pl.ds(c * half[t], half[t])
            for i, rs in enumerate(pieces[t]):
                for r, (px, py) in enumerate(chips):
                    q = 3 * (base[t] + i) + r
                    kk = 2 * px + py
                    pltpu.make_async_remote_copy(
                        src_ref=s_ref.at[mine, rs], dst_ref=o_ref.at[mine, kk, rs], send_sem=ssem1.at[q],
                        recv_sem=rsem1.at[q], device_id=(px, py, c), device_id_type=MESH).wait_recv()
                    cp = pltpu.make_async_remote_copy(
                        src_ref=o_ref.at[mine, kk, rs], dst_ref=o_ref.at[mine, kk, rs], send_sem=ssem2.at[q],
                        recv_sem=rsem2.at[q], device_id=(x, y, 1 - c), device_id_type=MESH)
                    cp.start()
                    sends.append(cp)
        for t in range(nt):
            o_ref = o_refs[t]
            theirs = pl.ds((1 - c) * half[t], half[t])
            for i, rs in enumerate(pieces[t]):
                for r, (px, py) in enumerate(chips):
                    q = 3 * (base[t] + i) + r
                    kk = 2 * px + py
                    pltpu.make_async_remote_copy(
                        src_ref=o_ref.at[theirs, kk, rs], dst_ref=o_ref.at[theirs, kk, rs], send_sem=ssem2.at[q],
                        recv_sem=rsem2.at[q], device_id=(x, y, 1 - c), device_id_type=MESH).wait_recv()
        for cp in sends:
            cp.wait_send()

    sem = pltpu.SemaphoreType.DMA
    outs = pl.pallas_call(
        body, in_specs=[ANY] * (2 * nt), out_specs=[ANY] * nt,
        out_shape=[SDS(p.shape, p.dtype) for p in placed],
        input_output_aliases={nt + t: t for t in range(nt)},
        scratch_shapes=[sem((3 * total,)), sem((3 * total,)), sem((3 * total,)), sem((3 * total,))],
        name=name)(*shards, *placed)
    return list(outs)


def _pair_send(grads, name):
    nt = len(grads)
    half = [g.shape[0] // 2 for g in grads]
    pieces = [_row_pieces(g.shape[2], _nbytes((h,) + g.shape[1:], g.dtype)) for g, h in zip(grads, half)]
    base, total = _offsets([len(p) for p in pieces])

    def body(*refs):
        g_refs, o_refs = refs[:nt], refs[nt:2 * nt]
        ssem, rsem = refs[2 * nt:]
        x, y, c = _coords()
        cps = []
        for t in range(nt):
            theirs = pl.ds((1 - c) * half[t], half[t])
            for i, rs in enumerate(pieces[t]):
                q = base[t] + i
                cp = pltpu.make_async_remote_copy(
                    src_ref=g_refs[t].at[theirs, :, rs], dst_ref=o_refs[t].at[:, :, rs], send_sem=ssem.at[q],
                    recv_sem=rsem.at[q], device_id=(x, y, 1 - c), device_id_type=MESH)
                cp.start()
                cps.append(cp)
        for cp in cps:
            cp.wait_recv()
        for cp in cps:
            cp.wait_send()

    sem = pltpu.SemaphoreType.DMA
    outs = pl.pallas_call(
        body, in_specs=[ANY] * nt, out_specs=[ANY] * nt,
        out_shape=[SDS((g.shape[0] // 2,) + g.shape[1:], g.dtype) for g in grads],
        scratch_shapes=[sem((total,)), sem((total,))],
        name=name)(*grads)
    return list(outs)


def _chip_send(parts, name):
    nt = len(parts)
    pieces = [_row_pieces(p.shape[2], _nbytes((p.shape[0],) + p.shape[2:], p.dtype)) for p in parts]
    base, total = _offsets([len(p) for p in pieces])

    def body(*refs):
        p_refs, o_refs = refs[:nt], refs[nt:2 * nt]
        ssem, rsem = refs[2 * nt:]
        x, y, c = _coords()
        cps = []
        for t in range(nt):
            for i, rs in enumerate(pieces[t]):
                for r, (px, py) in enumerate(_other_chips(x, y)):
                    q = 3 * (base[t] + i) + r
                    cp = pltpu.make_async_remote_copy(
                        src_ref=p_refs[t].at[:, 2 * px + py, rs], dst_ref=o_refs[t].at[r, :, rs], send_sem=ssem.at[q],
                        recv_sem=rsem.at[q], device_id=(px, py, c), device_id_type=MESH)
                    cp.start()
                    cps.append(cp)
        for cp in cps:
            cp.wait_recv()
        for cp in cps:
            cp.wait_send()

    sem = pltpu.SemaphoreType.DMA
    outs = pl.pallas_call(
        body, in_specs=[ANY] * nt, out_specs=[ANY] * nt,
        out_shape=[SDS((N_CHIPS - 1, p.shape[0]) + p.shape[2:], p.dtype) for p in parts],
        scratch_shapes=[sem((3 * total,)), sem((3 * total,))],
        name=name)(*parts)
    return list(outs)


def _pair_join(bufs, name):
    nt = len(bufs)
    half = [b.shape[0] // 2 for b in bufs]
    pieces = [_row_pieces(b.shape[1], _nbytes((h,) + b.shape[1:], b.dtype)) for b, h in zip(bufs, half)]
    base, total = _offsets([len(p) for p in pieces])

    def body(*refs):
        o_refs = refs[nt:2 * nt]
        ssem, rsem = refs[2 * nt:]
        x, y, c = _coords()
        cps = []
        for t in range(nt):
            mine = pl.ds(c * half[t], half[t])
            for i, rs in enumerate(pieces[t]):
                q = base[t] + i
                cp = pltpu.make_async_remote_copy(
                    src_ref=o_refs[t].at[mine, rs], dst_ref=o_refs[t].at[mine, rs], send_sem=ssem.at[q],
                    recv_sem=rsem.at[q], device_id=(x, y, 1 - c), device_id_type=MESH)
                cp.start()
                cps.append(cp)
        for t in range(nt):
            theirs = pl.ds((1 - c) * half[t], half[t])
            for i, rs in enumerate(pieces[t]):
                q = base[t] + i
                pltpu.make_async_remote_copy(
                    src_ref=o_refs[t].at[theirs, rs], dst_ref=o_refs[t].at[theirs, rs], send_sem=ssem.at[q],
                    recv_sem=rsem.at[q], device_id=(x, y, 1 - c), device_id_type=MESH).wait_recv()
        for cp in cps:
            cp.wait_send()

    sem = pltpu.SemaphoreType.DMA
    outs = pl.pallas_call(
        body, in_specs=[ANY] * nt, out_specs=[ANY] * nt,
        out_shape=[SDS(b.shape, b.dtype) for b in bufs],
        input_output_aliases={t: t for t in range(nt)},
        scratch_shapes=[sem((total,)), sem((total,))],
        name=name)(*bufs)
    return list(outs)


def _gather_devices(vals, name):
    nt = len(vals)
    flips = [(a, b, e) for a in (0, 1) for b in (0, 1) for e in (0, 1)][1:]

    def body(*refs):
        v_refs, o_refs = refs[:nt], refs[nt:2 * nt]
        lsem, ssem, rsem = refs[2 * nt:]
        x, y, c = _coords()
        me = 4 * x + 2 * y + c
        peers = [((1 - x) if a else x, (1 - y) if b else y, (1 - c) if e else c) for a, b, e in flips]
        cps = []
        for t in range(nt):
            loc = pltpu.make_async_copy(v_refs[t], o_refs[t].at[me], lsem.at[t])
            loc.start()
            cps.append(loc)
            for r, peer in enumerate(peers):
                cp = pltpu.make_async_remote_copy(
                    src_ref=v_refs[t], dst_ref=o_refs[t].at[me], send_sem=ssem.at[7 * t + r],
                    recv_sem=rsem.at[7 * t + r], device_id=peer, device_id_type=MESH)
                cp.start()
                cps.append(cp)
        for t in range(nt):
            for r, (px, py, pc) in enumerate(peers):
                pltpu.make_async_remote_copy(
                    src_ref=v_refs[t], dst_ref=o_refs[t].at[4 * px + 2 * py + pc], send_sem=ssem.at[7 * t + r],
                    recv_sem=rsem.at[7 * t + r], device_id=(px, py, pc), device_id_type=MESH).wait_recv()
        for t in range(nt):
            cps[8 * t].wait()
            for r in range(7):
                cps[8 * t + 1 + r].wait_send()

    sem = pltpu.SemaphoreType.DMA
    outs = pl.pallas_call(
        body, in_specs=[ANY] * nt, out_specs=[ANY] * nt,
        out_shape=[SDS((N_DEV,) + v.shape, v.dtype) for v in vals],
        scratch_shapes=[sem((nt,)), sem((7 * nt,)), sem((7 * nt,))],
        name=name)(*vals)
    return list(outs)


def _gather_all(shards, k_idx, tag):
    placed = [_place_chunk(s, k_idx, f"{tag}_place{t}") for t, s in enumerate(shards)]
    return _gather_chips(shards, placed, f"{tag}_gather")


def _reduce_to_shards(grads, k_idx, c_idx, kc_idx, tag):
    recv = _pair_send(grads, f"{tag}_pair_send")
    pair = [_pair_sum(g, r, c_idx, f"{tag}_pair_sum{t}") for t, (g, r) in enumerate(zip(grads, recv))]
    recv = _chip_send(pair, f"{tag}_chip_send")
    bufs = [_chip_sum(p, r, kc_idx, f"{tag}_chip_sum{t}") for t, (p, r) in enumerate(zip(pair, recv))]
    return _pair_join(bufs, f"{tag}_pair_join")


WEIGHT_NAMES = ("c_ctx", "w_ada", "b_ada", "w_in", "w_pool", "pool_scale", "ssm_a_re", "ssm_a_im", "ssm_log_dt",
                "ssm_b_re", "ssm_b_im", "ssm_c_re", "ssm_c_im", "ssm_d", "w_glu", "w_out", "g_pre_mix", "g_post_mix",
                "g_pre_ffn", "g_post_ffn", "w_up", "w_conv", "w_down")


def _block_diag_in(bb, ng):
    nl, nd, npart, h, gp = bb.shape
    p = gp // ng
    w = jnp.einsum("ldqhgp,kg->lkhdqgp", bb.reshape(nl, nd, npart, h, ng, p), jnp.eye(ng, dtype=bb.dtype))
    return w.reshape(nl, ng * h, nd * npart * gp)


def _block_diag_in_grad(dw, nd, npart, h, ng, p):
    return jnp.einsum("khdqkp->dqhkp", dw.reshape(ng, h, nd, npart, ng, p)).reshape(nd, npart, h, ng * p)


def _block_diag_out(cs, ng):
    nl, nd, npart, _, h, p = cs.shape
    w = jnp.einsum("ldqghp,kg->ldqkpgh", cs, jnp.eye(ng, dtype=cs.dtype))
    return w.reshape(nl, nd * npart * ng * p, ng * h)


def _block_diag_out_grad(dw, nd, npart, h, ng, p):
    return jnp.einsum("dqkpkh->dqkhp", dw.reshape(nd, npart, ng, p, ng, h))


def kernel(x, c, ctx, c_ctx, w_ada, b_ada, w_in, w_pool, pool_scale, ssm_a_re, ssm_a_im, ssm_log_dt, ssm_b_re, ssm_b_im, ssm_c_re, ssm_c_im, ssm_d, w_glu, w_out, g_pre_mix, g_post_mix, g_pre_ffn, g_post_ffn, w_up, w_conv, w_down, loss_target, m_c_ctx, m_w_ada, m_b_ada, m_w_in, m_w_pool, m_pool_scale, m_ssm_a_re, m_ssm_a_im, m_ssm_log_dt, m_ssm_b_re, m_ssm_b_im, m_ssm_c_re, m_ssm_c_im, m_ssm_d, m_w_glu, m_w_out, m_g_pre_mix, m_g_post_mix, m_g_pre_ffn, m_g_post_ffn, m_w_up, m_w_conv, m_w_down, v_c_ctx, v_w_ada, v_b_ada, v_w_in, v_w_pool, v_pool_scale, v_ssm_a_re, v_ssm_a_im, v_ssm_log_dt, v_ssm_b_re, v_ssm_b_im, v_ssm_c_re, v_ssm_c_im, v_ssm_d, v_w_glu, v_w_out, v_g_pre_mix, v_g_post_mix, v_g_pre_ffn, v_g_post_ffn, v_w_up, v_w_conv, v_w_down):
    weights = dict(zip(WEIGHT_NAMES, (c_ctx, w_ada, b_ada, w_in, w_pool, pool_scale, ssm_a_re, ssm_a_im, ssm_log_dt,
                                      ssm_b_re, ssm_b_im, ssm_c_re, ssm_c_im, ssm_d, w_glu, w_out, g_pre_mix, g_post_mix,
                                      g_pre_ffn, g_post_ffn, w_up, w_conv, w_down)))
    mom1 = dict(zip(WEIGHT_NAMES, (m_c_ctx, m_w_ada, m_b_ada, m_w_in, m_w_pool, m_pool_scale, m_ssm_a_re, m_ssm_a_im,
                                   m_ssm_log_dt, m_ssm_b_re, m_ssm_b_im, m_ssm_c_re, m_ssm_c_im, m_ssm_d, m_w_glu, m_w_out,
                                   m_g_pre_mix, m_g_post_mix, m_g_pre_ffn, m_g_post_ffn, m_w_up, m_w_conv, m_w_down)))
    mom2 = dict(zip(WEIGHT_NAMES, (v_c_ctx, v_w_ada, v_b_ada, v_w_in, v_w_pool, v_pool_scale, v_ssm_a_re, v_ssm_a_im,
                                   v_ssm_log_dt, v_ssm_b_re, v_ssm_b_im, v_ssm_c_re, v_ssm_c_im, v_ssm_d, v_w_glu, v_w_out,
                                   v_g_pre_mix, v_g_post_mix, v_g_pre_ffn, v_g_post_ffn, v_w_up, v_w_conv, v_w_down)))

    xi, yi, ci = lax.axis_index("x"), lax.axis_index("y"), lax.axis_index("c")
    chip = 2 * xi + yi
    dev = 4 * xi + 2 * yi + ci
    nl = w_in.shape[0]
    n_lat, d = x.shape[1], x.shape[2]
    n_ctx = ctx.shape[1]
    n = n_ctx + n_lat
    _, ndir, ng, nstate, nh = ssm_b_re.shape
    gp = ng * nstate
    sw = ng * nh
    n_pool_groups, pool_group = w_pool.shape[1], w_pool.shape[3]
    pw = n_pool_groups * pool_group
    assert pw + sw == d and pw % sw == 0 and len(POOL_WINDOWS) == n_pool_groups and n_lat % GRID_W == 0
    dff2 = w_up.shape[2] * N_CHIPS
    ada_w = w_ada.shape[2] * N_CHIPS
    ada_cols = w_ada.shape[2]
    s_rows = gp // LANES

    c_pad = jnp.concatenate([c, jnp.zeros((SUBLANES - 1, d), F32)], axis=0)
    c_all = _gather_devices([c_pad], "gather_cond")[0][:, 0, :]
    cond = jnp.concatenate([c_all, c_ctx[None, :], jnp.zeros((16 - N_DEV - 1, d), F32)], axis=0)
    b_shard = lax.dynamic_slice_in_dim(b_ada, chip * ada_cols, ada_cols, axis=1)[:, None, :]
    mod_shard = _ada_fwd(cond, w_ada, b_shard, "ada_fwd")
    k_idx, c_idx, kc_idx = jnp.stack([chip]), jnp.stack([ci]), jnp.stack([chip, ci])
    mod_all = _gather_all([mod_shard], k_idx, "mods")[0]
    mod_all = jnp.transpose(mod_all, (0, 2, 1, 3)).reshape(nl, 16, ada_w)
    mod_lat = lax.dynamic_index_in_dim(mod_all, dev, axis=1, keepdims=False).reshape(nl, 6, d)
    mod_ctx = mod_all[:, N_DEV].reshape(nl, 6, d)
    mods = jnp.concatenate([jnp.stack([mod_ctx, mod_lat], axis=1), jnp.zeros((nl, 2, 2, d), F32)], axis=2)

    shards = [w_in.astype(COMM_DTYPE), w_pool.reshape(nl, pw // N_CHIPS, pool_group).astype(COMM_DTYPE),
              w_glu.astype(COMM_DTYPE), w_out.astype(COMM_DTYPE), w_up.astype(COMM_DTYPE), w_down.astype(COMM_DTYPE),
              w_conv.reshape(nl, 9, dff2 // N_CHIPS)]
    g_in, g_pool, g_glu, g_out, g_up, g_down, g_conv = _gather_all(shards, k_idx, "weights")
    wi = g_in.reshape(nl, d, d)
    wp = jnp.transpose(g_pool.reshape(nl, N_CHIPS, n_pool_groups, pool_group // N_CHIPS, pool_group),
                       (0, 2, 1, 3, 4)).reshape(nl, n_pool_groups, pool_group, pool_group)
    wg = g_glu.reshape(nl, sw, sw)
    wo = g_out.reshape(nl, d, d)
    wu = g_up
    wd = g_down.reshape(nl, dff2 // 2, d)
    wk = jnp.transpose(g_conv, (0, 2, 1, 3)).reshape(nl, 9, dff2)

    rows = nl * ndir
    a_re2 = ssm_a_re.reshape(rows, gp)
    a_im2 = ssm_a_im.reshape(rows, gp)
    logdt2 = jnp.repeat(ssm_log_dt.reshape(rows, ng), nstate, axis=1)
    b_re2 = jnp.transpose(ssm_b_re.reshape(rows, gp, nh), (0, 2, 1))
    b_im2 = jnp.transpose(ssm_b_im.reshape(rows, gp, nh), (0, 2, 1))
    lam_re, lam_im, bb_re, bb_im = _disc_fwd(a_re2, a_im2, logdt2, b_re2, b_im2, "s5_discretise")
    lam = jnp.stack([lam_re.reshape(nl, ndir, s_rows, LANES), lam_im.reshape(nl, ndir, s_rows, LANES)], axis=2)
    lam = lam.reshape(nl, 2 * ndir, s_rows, LANES)
    bbs = jnp.stack([bb_re.reshape(nl, ndir, nh, gp), bb_im.reshape(nl, ndir, nh, gp)], axis=2)
    w_b = _block_diag_in(bbs, ng).astype(MXU_DTYPE)
    cs = jnp.stack([ssm_c_re, -ssm_c_im], axis=2)
    w_c = _block_diag_out(cs, ng).astype(MXU_DTYPE)

    def row(v, l):
        return v[l:l + 1]

    xc = jnp.concatenate([ctx[0], x[0]], axis=0)
    saved = []
    for l in range(nl):
        t = f"l{l}"
        md = mods[l]
        h = _norm_mod_fwd(xc, row(g_pre_mix, l), md, 0, 1, n_ctx, f"{t}_pre_mix")
        u = _mm(h, wi, "nn", F32, f"{t}_in_proj", b_idx=l)
        p = _pool(u, pw, pool_group, n_ctx, False, MXU_DTYPE, f"{t}_pool")
        ypool = _pool_proj_fwd(p, wp, l, row(pool_scale, l), f"{t}_pool_proj")
        bu = _mm(u, w_b, "nn", F32, f"{t}_s5_in", b_idx=l, a_cols=(pw, sw))
        bu0 = bu[:, :2 * gp].reshape(n, 2 * s_rows, LANES)
        bu1 = bu[:, 2 * gp:].reshape(n, 2 * s_rows, LANES)
        h0, h1 = _scan_fwd(bu0, bu1, lam[l], n_ctx, f"{t}_scan")
        hcat = jnp.concatenate([h0.reshape(n, 2 * gp), h1.reshape(n, 2 * gp)], axis=1).astype(MXU_DTYPE)
        y = _mm(hcat, w_c, "nn", F32, f"{t}_s5_out", b_idx=l)
        s_out = _ssm_head_fwd(y, u, row(ssm_d, l), wg, l, f"{t}_s5_head")
        cat = jnp.concatenate([ypool, s_out], axis=1)
        mix = _mm(cat, wo, "nn", F32, f"{t}_out_proj", b_idx=l)
        x_mid = _gate_res_fwd(xc, mix, row(g_post_mix, l), md, 2, n_ctx, f"{t}_post_mix")
        h2 = _norm_mod_fwd(x_mid, row(g_pre_ffn, l), md, 3, 4, n_ctx, f"{t}_pre_ffn")
        z = _mm(h2, wu, "nn", F32, f"{t}_up", b_idx=l, b_chunks=N_CHIPS)
        act = _conv_glu_fwd(z, wk[l], n_ctx, f"{t}_conv_glu")
        f = _mm(act, wd, "nn", F32, f"{t}_down", b_idx=l)
        x_out = _gate_res_fwd(x_mid, f, row(g_post_ffn, l), md, 5, n_ctx, f"{t}_post_ffn")
        saved.append(dict(xc=xc, h=h, u=u, p=p, h0=h0, h1=h1, hcat=hcat, y=y, cat=cat, mix=mix, x_mid=x_mid, h2=h2, z=z, f=f))
        xc = x_out

    dx, loss_tile = _loss_grad(xc, loss_target[0], n_ctx, "loss")
    loss = lax.psum(loss_tile[0, 0], ("x", "y", "c"))

    big = {k: [None] * nl for k in ("w_in", "w_pool", "w_glu", "w_out", "w_up", "w_down")}
    small = {k: [None] * nl for k in ("pool_scale", "ssm_d", "g_pre_mix", "g_post_mix", "g_pre_ffn", "g_post_ffn",
                                      "lam", "bb", "cs", "w_conv")}
    dmods = [None] * nl
    for l in reversed(range(nl)):
        t = f"l{l}b"
        md = mods[l]
        sv = saved[l]
        df, dgate_ffn, small["g_post_ffn"][l] = _gate_res_bwd(dx, sv["f"], row(g_post_ffn, l), md, 5, n_ctx, f"{t}_post_ffn")
        dact = _mm(df, wd, "nt", F32, f"{t}_down_dx", b_idx=l)
        dzv, dzg, dkv, dkg, act = _conv_glu_bwd(sv["z"], dact, wk[l], n_ctx, f"{t}_conv_glu")
        dz = jnp.concatenate([dzv, dzg], axis=1)
        small["w_conv"][l] = jnp.concatenate([dkv, dkg], axis=1)
        big["w_down"][l] = _mm(act, df, "tn", COMM_DTYPE, f"{t}_down_dw")
        big["w_up"][l] = _mm(sv["h2"], dz, "tn", COMM_DTYPE, f"{t}_up_dw", out_chunks=N_CHIPS)
        dh2 = _mm(dz, wu, "nt", F32, f"{t}_up_dx", b_idx=l, b_chunks=N_CHIPS)
        dx, dss_ffn, small["g_pre_ffn"][l] = _norm_mod_bwd(dh2, sv["x_mid"], row(g_pre_ffn, l), md, 3, 4, dx, n_ctx,
                                                           f"{t}_pre_ffn")
        dmix, dgate_mix, small["g_post_mix"][l] = _gate_res_bwd(dx, sv["mix"], row(g_post_mix, l), md, 2, n_ctx,
                                                                f"{t}_post_mix")
        dcat = _mm(dmix, wo, "nt", F32, f"{t}_out_dx", b_idx=l)
        big["w_out"][l] = _mm(sv["cat"], dmix, "tn", COMM_DTYPE, f"{t}_out_dw")
        dp, small["pool_scale"][l], big["w_pool"][l] = _pool_proj_bwd(sv["p"], dcat, wp, l, row(pool_scale, l),
                                                                      f"{t}_pool_proj")
        du_pool = _pool(dp, pw, pool_group, n_ctx, True, F32, f"{t}_pool")
        dy, du_dir, gact, dq, small["ssm_d"][l] = _ssm_head_bwd(dcat, sv["y"], sv["u"], row(ssm_d, l), wg, l, f"{t}_s5_head")
        big["w_glu"][l] = _mm(gact, dq, "tn", COMM_DTYPE, f"{t}_glu_dw")
        dhcat = _mm(dy, w_c, "nt", F32, f"{t}_s5_out_dx", b_idx=l)
        dwc = _mm(sv["hcat"], dy, "tn", F32, f"{t}_s5_out_dw")
        small["cs"][l] = _block_diag_out_grad(dwc, ndir, 2, nh, ng, nstate)
        dh0 = dhcat[:, :2 * gp].reshape(n, 2 * s_rows, LANES)
        dh1 = dhcat[:, 2 * gp:].reshape(n, 2 * s_rows, LANES)
        a0, a1, small["lam"][l] = _scan_bwd(dh0, dh1, sv["h0"], sv["h1"], lam[l], n_ctx, f"{t}_scan")
        acat = jnp.concatenate([a0.reshape(n, 2 * gp), a1.reshape(n, 2 * gp)], axis=1).astype(MXU_DTYPE)
        du_proj = _mm(acat, w_b, "nt", F32, f"{t}_s5_in_dx", b_idx=l)
        dwb = _mm(sv["u"], acat, "tn", F32, f"{t}_s5_in_dw", a_cols=(pw, sw))
        small["bb"][l] = _block_diag_in_grad(dwb, ndir, 2, nh, ng, nstate)
        du = _assemble_du(du_pool, du_dir, du_proj, f"{t}_du")
        dh = _mm(du, wi, "nt", F32, f"{t}_in_dx", b_idx=l)
        big["w_in"][l] = _mm(sv["h"], du, "tn", COMM_DTYPE, f"{t}_in_dw")
        dx, dss_mix, small["g_pre_mix"][l] = _norm_mod_bwd(dh, sv["xc"], row(g_pre_mix, l), md, 0, 1, dx, n_ctx,
                                                           f"{t}_pre_mix")
        dmods[l] = jnp.concatenate([dss_mix, dgate_mix, dss_ffn, dgate_ffn], axis=1).reshape(2, ada_w)

    grad_x = dx[n_ctx:][None]

    dmod_all = _gather_devices([jnp.stack(dmods, axis=0)], "gather_dmods")[0]
    ada_rows, db_ada = _ada_rows(dmod_all, "ada_rows")
    rows_shard = lax.dynamic_slice_in_dim(ada_rows, chip * ada_cols, ada_cols, axis=2)
    dcond_part = _ada_dcond(rows_shard, w_ada, "ada_dcond")
    dcond_parts = _gather_devices([dcond_part], "gather_dcond")[0][0::2]
    grads = {"w_ada": _ada_dw(cond, rows_shard, "ada_dw"), "b_ada": db_ada[:, 0, :],
             "c_ctx": _ada_dctx(dcond_parts, c_ctx[None, :], N_DEV, "ada_dctx")[0]}

    stacked = {k: jnp.stack(v, axis=0) for k, v in big.items()}
    parts = [stacked["w_in"].reshape(nl, N_CHIPS, d // N_CHIPS, d),
             jnp.transpose(stacked["w_pool"].astype(COMM_DTYPE).reshape(nl, n_pool_groups, N_CHIPS, pool_group // N_CHIPS,
                                                                      pool_group), (0, 2, 1, 3, 4))
             .reshape(nl, N_CHIPS, pw // N_CHIPS, pool_group),
             stacked["w_glu"].reshape(nl, N_CHIPS, sw // N_CHIPS, sw),
             stacked["w_out"].reshape(nl, N_CHIPS, d // N_CHIPS, d),
             stacked["w_up"],
             stacked["w_down"].reshape(nl, N_CHIPS, dff2 // 2 // N_CHIPS, d)]
    r_in, r_pool, r_glu, r_out, r_up, r_down = _reduce_to_shards(parts, k_idx, c_idx, kc_idx, "big")
    grads.update(w_in=r_in, w_pool=r_pool.reshape(w_pool.shape), w_glu=r_glu, w_out=r_out, w_up=r_up, w_down=r_down)

    order = ("pool_scale", "ssm_d", "g_pre_mix", "g_post_mix", "g_pre_ffn", "g_post_ffn", "lam", "bb", "cs", "w_conv")
    pieces = [jnp.stack(small[k], axis=0) for k in order]
    flat = jnp.concatenate([q.reshape(-1) for q in pieces])
    unit = nl * N_CHIPS * SUBLANES * 1024
    padded = -(-flat.shape[0] // unit) * unit
    flat = jnp.concatenate([flat, jnp.zeros((padded - flat.shape[0],), F32)])
    vec = flat.reshape(nl, N_CHIPS, padded // (nl * N_CHIPS * 1024), 1024)
    vec = _gather_all(_reduce_to_shards([vec], k_idx, c_idx, kc_idx, "small"), k_idx, "small_all")[0].reshape(-1)
    red, pos = {}, 0
    for k, q in zip(order, pieces):
        red[k] = vec[pos:pos + q.size].reshape(q.shape)
        pos += q.size
    for k in ("pool_scale", "ssm_d", "g_pre_mix", "g_post_mix", "g_pre_ffn", "g_post_ffn"):
        grads[k] = red[k][:, 0, :]
    dlam = red["lam"].reshape(nl, ndir, 2, gp)
    dbb = red["bb"].reshape(nl, ndir, 2, nh, gp)
    d_are, d_aim, d_ldt, d_bre, d_bim = _disc_bwd(
        a_re2, a_im2, logdt2, b_re2, b_im2, dlam[:, :, 0].reshape(rows, gp), dlam[:, :, 1].reshape(rows, gp),
        dbb[:, :, 0].reshape(rows, nh, gp), dbb[:, :, 1].reshape(rows, nh, gp), nstate, "s5_discretise_bwd")
    grads["ssm_a_re"] = d_are.reshape(ssm_a_re.shape)
    grads["ssm_a_im"] = d_aim.reshape(ssm_a_im.shape)
    grads["ssm_log_dt"] = d_ldt[:, :ng].reshape(ssm_log_dt.shape)
    grads["ssm_b_re"] = jnp.transpose(d_bre, (0, 2, 1)).reshape(ssm_b_re.shape)
    grads["ssm_b_im"] = jnp.transpose(d_bim, (0, 2, 1)).reshape(ssm_b_im.shape)
    grads["ssm_c_re"] = red["cs"][:, :, 0]
    grads["ssm_c_im"] = -red["cs"][:, :, 1]
    conv_cols = dff2 // N_CHIPS
    grads["w_conv"] = lax.dynamic_slice_in_dim(red["w_conv"], chip * conv_cols, conv_cols, axis=2).reshape(w_conv.shape)

    delta, new_m, new_v = {}, {}, {}
    for k in WEIGHT_NAMES:
        delta[k], new_m[k], new_v[k] = _adamw(weights[k], grads[k], mom1[k], mom2[k], f"adamw_{k}")
    return (loss, grad_x, *[grads[k] for k in WEIGHT_NAMES], *[delta[k] for k in WEIGHT_NAMES],
            *[new_m[k] for k in WEIGHT_NAMES], *[new_v[k] for k in WEIGHT_NAMES])
```

```python
import math

import jax
import jax.numpy as jnp
from jax import lax
from jax.experimental import pallas as pl
from jax.experimental.pallas import tpu as pltpu

F32 = jnp.float32
MXU_DTYPE = jnp.bfloat16
COMM_DTYPE = jnp.bfloat16
HIGHEST = lax.Precision.HIGHEST
VMEM_LIMIT_BYTES = 48 * 1024 * 1024
LANES = 128
SUBLANES = 8
N_CHIPS = 4
N_DEV = 8

EPS = 1e-6
GRID_W = 64
POOL_WINDOWS = (2, 4, 8, 16)
ADAM_LR = 0.001
ADAM_B1 = 0.9
ADAM_B2 = 0.999
ADAM_EPS = 1e-08
ADAM_WD = 0.01
ADAM_STEP = 10
GELU_C0 = math.sqrt(2.0 / math.pi)
GELU_C1 = 0.044715

SDS = jax.ShapeDtypeStruct
ANY = pl.BlockSpec(memory_space=pl.ANY)
MESH = pl.DeviceIdType.MESH


def _cparams(*sem):
    return pltpu.CompilerParams(dimension_semantics=sem if sem else None, vmem_limit_bytes=VMEM_LIMIT_BYTES)


def _pick(n, cands):
    for cand in cands:
        if n % cand == 0:
            return cand
    return n


def _row_tile(n_ctx, n):
    return math.gcd(math.gcd(n_ctx, n - n_ctx), 256)


_DIMS = {"nn": (((1,), (0,)), ((), ())), "nt": (((1,), (1,)), ((), ())), "tn": (((0,), (0,)), ((), ()))}
_TM = (1088, 1024, 512, 384, 256, 128, 64, 32, 16, 8)
_TN = (1024, 1408, 512, 384, 256, 128)
_TK = (1024, 1088, 512, 1408, 384, 256, 128, 64, 32, 16, 8)


def _chunk_of(idx, per, chunks):
    out = 0
    for q in range(1, chunks):
        out = out + (idx >= q * per).astype(jnp.int32)
    return out


def _within(idx, per, chunks):
    return idx - per * _chunk_of(idx, per, chunks)


def _mm(a, b, mode, out_dtype, name, a_idx=None, b_idx=None, a_cols=None, b_chunks=None, out_chunks=None,
        a3d=False, b3d=False, out3d=False, add=None):
    a2, b2 = a.shape[-2:], b.shape[-2:]
    if a3d:
        a2 = (a.shape[0], a.shape[1] * a.shape[2])
    if b3d:
        assert mode == "tn"
        b2 = (b.shape[0], b.shape[1] * b.shape[2])
    if b_chunks is not None:
        assert mode in ("nn", "nt") and b.shape[-3] == b_chunks
        b2 = (b2[0], b2[1] * b_chunks)
    alast = a2[1] if a_cols is None else a_cols[1]
    if mode == "nn":
        m, k, n = a2[0], alast, b2[1]
        assert b2[0] == k
    elif mode == "nt":
        m, k, n = a2[0], alast, b2[0]
        assert b2[1] == k
    else:
        k, m, n = a2[0], alast, b2[1]
        assert b2[0] == k
    n_unit = n // (out_chunks or 1) // (b_chunks if b_chunks and mode == "nn" else 1)
    k_unit = k // (b_chunks if b_chunks and mode == "nt" else 1)
    tm, tn, tk = _pick(m, _TM), _pick(n_unit, _TN), _pick(k_unit, _TK)
    nk = k // tk
    a_lane_tile = tm if mode == "tn" else tk
    off = 0
    if a_cols is not None:
        assert a_cols[0] % a_lane_tile == 0
        off = a_cols[0] // a_lane_tile

    def operand(ref, is3d):
        v = ref[...]
        if is3d:
            v = pltpu.einshape("tjl->t(jl)", v)
        return v.astype(MXU_DTYPE)

    def body(*refs):
        a_ref, b_ref, o_ref, acc_ref = refs[0], refs[1], refs[-2], refs[-1]
        kk = pl.program_id(2)
        prod = lax.dot_general(operand(a_ref, a3d), operand(b_ref, b3d), _DIMS[mode], preferred_element_type=F32)

        def finish(total):
            if add is not None:
                total = total + refs[2][...]
            if out3d:
                total = pltpu.einshape("t(jl)->tjl", total, l=LANES)
            o_ref[...] = total.astype(o_ref.dtype)

        if nk == 1:
            finish(prod)
        else:
            @pl.when(kk == 0)
            def _():
                acc_ref[...] = prod

            @pl.when(jnp.logical_and(kk > 0, kk < nk - 1))
            def _():
                acc_ref[...] += prod

            @pl.when(kk == nk - 1)
            def _():
                finish(acc_ref[...] + prod)

    if mode == "tn":
        a_blk, a_map = (tk, tm), (lambda i, j, kk: (kk, i + off))
    else:
        a_blk, a_map = (tm, tk), (lambda i, j, kk: (i, kk + off))
    if mode == "nt":
        b_blk, b_map = (tn, tk), (lambda i, j, kk: (j, kk))
        if b_chunks is not None:
            per = k // b_chunks // tk
            b_blk, b_map = (None, tn, tk), (lambda i, j, kk: (_chunk_of(kk, per, b_chunks), j, _within(kk, per, b_chunks)))
    else:
        b_blk, b_map = (tk, tn), (lambda i, j, kk: (kk, j))
        if b_chunks is not None:
            per = n // b_chunks // tn
            b_blk, b_map = (None, tk, tn), (lambda i, j, kk: (_chunk_of(j, per, b_chunks), kk, _within(j, per, b_chunks)))
    if a3d:
        a_blk, a_map0 = (a_blk[0], a_blk[1] // LANES, LANES), a_map
        a_map = lambda i, j, kk: a_map0(i, j, kk) + (0,)
    if b3d:
        b_blk, b_map1 = (b_blk[0], b_blk[1] // LANES, LANES), b_map
        b_map = lambda i, j, kk: b_map1(i, j, kk) + (0,)
    if a_idx is not None:
        a_blk, a_map0 = (None,) + a_blk, a_map
        a_map = lambda i, j, kk: (a_idx,) + a_map0(i, j, kk)
    if b_idx is not None:
        b_blk, b_map0 = (None,) + b_blk, b_map
        b_map = lambda i, j, kk: (b_idx,) + b_map0(i, j, kk)
    o_blk, o_map, o_shape = (tm, tn), (lambda i, j, kk: (i, j)), (m, n)
    if out_chunks is not None:
        oper = n // out_chunks // tn
        o_map = lambda i, j, kk: (_chunk_of(j, oper, out_chunks), i, _within(j, oper, out_chunks))
        o_blk, o_shape = (None, tm, tn), (out_chunks, m, n // out_chunks)
    if out3d:
        o_blk, o_map, o_shape = (tm, tn // LANES, LANES), (lambda i, j, kk: (i, j, 0)), (m, n // LANES, LANES)
    in_specs, args = [pl.BlockSpec(a_blk, a_map), pl.BlockSpec(b_blk, b_map)], [a, b]
    if add is not None:
        in_specs.append(pl.BlockSpec((tm, tn), lambda i, j, kk: (i, j)))
        args.append(add)
    return pl.pallas_call(
        body, grid=(m // tm, n // tn, nk),
        in_specs=in_specs,
        out_specs=pl.BlockSpec(o_blk, o_map),
        out_shape=SDS(o_shape, out_dtype),
        scratch_shapes=[pltpu.VMEM((tm, tn), F32)],
        compiler_params=_cparams("parallel", "parallel", "arbitrary"), name=name)(*args)


def _seg_map(nbc):
    return lambda i: (jnp.where(i < nbc, 0, 1), 0, 0)


def _rstd(v):
    return lax.rsqrt(jnp.mean(v * v, axis=-1, keepdims=True) + EPS)


def _norm_mod_fwd(x, g, mods, sh, sc, n_ctx, name):
    n, d = x.shape
    tm = _row_tile(n_ctx, n)
    nbc = n_ctx // tm

    def body(x_ref, g_ref, m_ref, h_ref):
        xv = x_ref[...]
        hn = xv * _rstd(xv) * g_ref[...]
        h_ref[...] = (hn * (1.0 + m_ref[0, sc:sc + 1, :]) + m_ref[0, sh:sh + 1, :]).astype(h_ref.dtype)

    row = pl.BlockSpec((tm, d), lambda i: (i, 0))
    return pl.pallas_call(
        body, grid=(n // tm,),
        in_specs=[row, pl.BlockSpec((1, d), lambda i: (0, 0)), pl.BlockSpec((1, 8, d), _seg_map(nbc))],
        out_specs=row, out_shape=SDS((n, d), MXU_DTYPE), compiler_params=_cparams("parallel"), name=name)(x, g, mods)


def _gate_res_fwd(x, f, g, mods, gi, n_ctx, name):
    n, d = x.shape
    tm = _row_tile(n_ctx, n)
    nbc = n_ctx // tm

    def body(x_ref, f_ref, g_ref, m_ref, o_ref):
        fv = f_ref[...]
        o_ref[...] = x_ref[...] + m_ref[0, gi:gi + 1, :] * (fv * _rstd(fv) * g_ref[...])

    row = pl.BlockSpec((tm, d), lambda i: (i, 0))
    return pl.pallas_call(
        body, grid=(n // tm,),
        in_specs=[row, row, pl.BlockSpec((1, d), lambda i: (0, 0)), pl.BlockSpec((1, 8, d), _seg_map(nbc))],
        out_specs=row, out_shape=SDS((n, d), F32), compiler_params=_cparams("parallel"), name=name)(x, f, g, mods)


def _gate_res_bwd(dx, f, g, mods, gi, n_ctx, name):
    n, d = dx.shape
    tm = _row_tile(n_ctx, n)
    nbc = n_ctx // tm

    def body(dx_ref, f_ref, g_ref, m_ref, df_ref, dgate_ref, dg_ref):
        i = pl.program_id(0)

        @pl.when(i == 0)
        def _():
            dg_ref[...] = jnp.zeros(dg_ref.shape, F32)

        @pl.when(jnp.logical_or(i == 0, i == nbc))
        def _():
            dgate_ref[...] = jnp.zeros(dgate_ref.shape, F32)

        dxv, fv, gv = dx_ref[...], f_ref[...], g_ref[...]
        rs = _rstd(fv)
        nv = fv * rs
        dgate_ref[0] += jnp.sum(dxv * (nv * gv), axis=0, keepdims=True)
        dout = dxv * m_ref[0, gi:gi + 1, :]
        dg_ref[...] += jnp.sum(dout * nv, axis=0, keepdims=True)
        dn = dout * gv
        df_ref[...] = (rs * (dn - nv * jnp.mean(dn * nv, axis=-1, keepdims=True))).astype(df_ref.dtype)

    row = pl.BlockSpec((tm, d), lambda i: (i, 0))
    vec = pl.BlockSpec((1, d), lambda i: (0, 0))
    return pl.pallas_call(
        body, grid=(n // tm,),
        in_specs=[row, row, vec, pl.BlockSpec((1, 8, d), _seg_map(nbc))],
        out_specs=[row, pl.BlockSpec((1, 1, d), _seg_map(nbc)), vec],
        out_shape=[SDS((n, d), MXU_DTYPE), SDS((2, 1, d), F32), SDS((1, d), F32)],
        compiler_params=_cparams("arbitrary"), name=name)(dx, f, g, mods)


def _norm_mod_bwd(dh, x, g, mods, sh, sc, dx_res, n_ctx, name):
    n, d = x.shape
    tm = _row_tile(n_ctx, n)
    nbc = n_ctx // tm

    def body(dh_ref, x_ref, g_ref, m_ref, r_ref, dx_ref, dss_ref, dg_ref):
        i = pl.program_id(0)

        @pl.when(i == 0)
        def _():
            dg_ref[...] = jnp.zeros(dg_ref.shape, F32)

        @pl.when(jnp.logical_or(i == 0, i == nbc))
        def _():
            dss_ref[...] = jnp.zeros(dss_ref.shape, F32)

        dhv, xv, gv = dh_ref[...], x_ref[...], g_ref[...]
        rs = _rstd(xv)
        nv = xv * rs
        dss_ref[0, 0:1, :] += jnp.sum(dhv, axis=0, keepdims=True)
        dss_ref[0, 1:2, :] += jnp.sum(dhv * (nv * gv), axis=0, keepdims=True)
        dhn = dhv * (1.0 + m_ref[0, sc:sc + 1, :])
        dg_ref[...] += jnp.sum(dhn * nv, axis=0, keepdims=True)
        dn = dhn * gv
        dx_ref[...] = r_ref[...] + rs * (dn - nv * jnp.mean(dn * nv, axis=-1, keepdims=True))

    row = pl.BlockSpec((tm, d), lambda i: (i, 0))
    vec = pl.BlockSpec((1, d), lambda i: (0, 0))
    return pl.pallas_call(
        body, grid=(n // tm,),
        in_specs=[row, row, vec, pl.BlockSpec((1, 8, d), _seg_map(nbc)), row],
        out_specs=[row, pl.BlockSpec((1, 2, d), _seg_map(nbc)), vec],
        out_shape=[SDS((n, d), F32), SDS((2, 2, d), F32), SDS((1, d), F32)],
        compiler_params=_cparams("arbitrary"), name=name)(dh, x, g, mods, dx_res)


def _loss_grad(xc, target, n_ctx, name):
    n, d = xc.shape
    tm = _row_tile(n_ctx, n)
    nbc = n_ctx // tm
    nb = n // tm

    def body(x_ref, t_ref, dx_ref, l_ref, acc_ref):
        i = pl.program_id(0)

        @pl.when(i == 0)
        def _():
            acc_ref[...] = jnp.zeros(acc_ref.shape, F32)

        @pl.when(i < nbc)
        def _():
            dx_ref[...] = jnp.zeros(dx_ref.shape, F32)

        @pl.when(i >= nbc)
        def _():
            diff = x_ref[...] - t_ref[...]
            dx_ref[...] = diff * (1.0 / d)
            acc_ref[...] += jnp.sum(diff * diff, axis=0, keepdims=True)

        @pl.when(i == nb - 1)
        def _():
            l_ref[...] = jnp.full(l_ref.shape, (0.5 / d) * jnp.sum(acc_ref[...]), F32)

    row = pl.BlockSpec((tm, d), lambda i: (i, 0))
    return pl.pallas_call(
        body, grid=(nb,),
        in_specs=[row, pl.BlockSpec((tm, d), lambda i: (jnp.maximum(i - nbc, 0), 0))],
        out_specs=[row, pl.BlockSpec((SUBLANES, LANES), lambda i: (0, 0))],
        out_shape=[SDS((n, d), F32), SDS((SUBLANES, LANES), F32)],
        scratch_shapes=[pltpu.VMEM((1, d), F32)],
        compiler_params=_cparams("arbitrary"), name=name)(xc, target)


POOL_PAD = 16


def _pool(src, pool_width, pool_group, n_ctx, bwd, out_dtype, name):
    n = src.shape[0]
    n_lat = n - n_ctx
    gb = pool_group // LANES
    segs = ((0, n_ctx, POOL_PAD), (n_ctx, n_lat, 2 * POOL_PAD + n_ctx))
    total = 3 * POOL_PAD + n

    def body(s_ref, o_ref, scr):
        j = pl.program_id(0)
        for base in (0, POOL_PAD + n_ctx, 2 * POOL_PAD + n):
            scr[pl.ds(base, POOL_PAD), :] = jnp.zeros((POOL_PAD, LANES), F32)
        for gi, w in enumerate(POOL_WINDOWS):
            @pl.when(jnp.logical_and(j >= gi * gb, j < (gi + 1) * gb))
            def _(w=w):
                half = w // 2
                offs = range(-half + 1, half + 1) if bwd else range(-half, half)
                for row0, nseg, base in segs:
                    ch = math.gcd(nseg, 256)

                    def count(c0):
                        t = c0 + lax.broadcasted_iota(jnp.int32, (ch, LANES), 0)
                        return (jnp.minimum(t + half, nseg) - jnp.maximum(t - half, 0)).astype(F32)

                    def fill(ci, carry):
                        c0 = pl.multiple_of(ci * ch, ch)
                        v = s_ref[pl.ds(row0 + c0, ch), :]
                        scr[pl.ds(base + c0, ch), :] = v / count(c0) if bwd else v
                        return carry

                    def window(ci, carry):
                        c0 = pl.multiple_of(ci * ch, ch)
                        acc = jnp.zeros((ch, LANES), F32)
                        for off in offs:
                            acc = acc + scr[pl.ds(c0 + (base + off), ch), :]
                        v = s_ref[pl.ds(row0 + c0, ch), :]
                        res = acc - v if bwd else acc / count(c0) - v
                        o_ref[pl.ds(row0 + c0, ch), :] = res.astype(o_ref.dtype)
                        return carry

                    lax.fori_loop(0, nseg // ch, fill, 0)
                    lax.fori_loop(0, nseg // ch, window, 0)

    blk = pl.BlockSpec((n, LANES), lambda j: (0, j))
    return pl.pallas_call(
        body, grid=(pool_width // LANES,), in_specs=[blk], out_specs=blk,
        out_shape=SDS((n, pool_width), out_dtype), scratch_shapes=[pltpu.VMEM((total, LANES), F32)],
        compiler_params=_cparams("parallel"), name=name)(src)


def _pool_proj_fwd(p, wp, l, scale, name):
    n, pw = p.shape
    ng, c = wp.shape[1], wp.shape[2]
    tm = _pick(n, _TM)

    def body(p_ref, w_ref, s_ref, o_ref):
        y = jnp.dot(p_ref[...], w_ref[...].astype(MXU_DTYPE), preferred_element_type=F32)
        o_ref[...] = (y * s_ref[...]).astype(o_ref.dtype)

    return pl.pallas_call(
        body, grid=(ng, n // tm),
        in_specs=[pl.BlockSpec((tm, c), lambda g, i: (i, g)), pl.BlockSpec((None, None, c, c), lambda g, i: (l, g, 0, 0)),
                  pl.BlockSpec((1, c), lambda g, i: (0, g))],
        out_specs=pl.BlockSpec((tm, c), lambda g, i: (i, g)), out_shape=SDS((n, pw), MXU_DTYPE),
        compiler_params=_cparams("parallel", "parallel"), name=name)(p, wp, scale)


def _pool_proj_bwd(p, dcat, wp, l, scale, name):
    n, pw = p.shape
    ng, c = wp.shape[1], wp.shape[2]
    tm = _pick(n, _TM)

    def body(p_ref, dy_ref, w_ref, s_ref, dp_ref, ds_ref, dw_ref):
        i = pl.program_id(1)

        @pl.when(i == 0)
        def _():
            ds_ref[...] = jnp.zeros(ds_ref.shape, F32)
            dw_ref[...] = jnp.zeros(dw_ref.shape, F32)

        pv, wv, dy = p_ref[...], w_ref[...].astype(MXU_DTYPE), dy_ref[...]
        y = jnp.dot(pv, wv, preferred_element_type=F32)
        ds_ref[...] += jnp.sum(dy * y, axis=0, keepdims=True)
        dpw = (dy * s_ref[...]).astype(MXU_DTYPE)
        dp_ref[...] = lax.dot_general(dpw, wv, _DIMS["nt"], preferred_element_type=F32)
        dw_ref[0] += lax.dot_general(pv, dpw, _DIMS["tn"], preferred_element_type=F32)

    return pl.pallas_call(
        body, grid=(ng, n // tm),
        in_specs=[pl.BlockSpec((tm, c), lambda g, i: (i, g)), pl.BlockSpec((tm, c), lambda g, i: (i, g)),
                  pl.BlockSpec((None, None, c, c), lambda g, i: (l, g, 0, 0)), pl.BlockSpec((1, c), lambda g, i: (0, g))],
        out_specs=[pl.BlockSpec((tm, c), lambda g, i: (i, g)), pl.BlockSpec((1, c), lambda g, i: (0, g)),
                   pl.BlockSpec((1, c, c), lambda g, i: (g, 0, 0))],
        out_shape=[SDS((n, pw), F32), SDS((1, pw), F32), SDS((ng, c, c), F32)],
        compiler_params=_cparams("arbitrary", "arbitrary"), name=name)(p, dcat, wp, scale)


def _disc_math(a_re, a_im, logdt, b_re, b_im):
    dt = jnp.exp(logdt)
    mag = jnp.exp(a_re * dt)
    lam_re = mag * jnp.cos(a_im * dt)
    lam_im = mag * jnp.sin(a_im * dt)
    denom = a_re * a_re + a_im * a_im
    nr, ni = lam_re - 1.0, lam_im
    f_re = ((nr * a_re + ni * a_im) / denom)[:, None, :]
    f_im = ((ni * a_re - nr * a_im) / denom)[:, None, :]
    return lam_re, lam_im, f_re * b_re - f_im * b_im, f_re * b_im + f_im * b_re


def _disc_fwd(a_re, a_im, logdt, b_re, b_im, name):
    def body(ar, ai, ld, br, bi, o_lr, o_li, o_br, o_bi):
        lr, li, bbr, bbi = _disc_math(ar[...], ai[...], ld[...], br[...], bi[...])
        o_lr[...] = lr
        o_li[...] = li
        o_br[...] = bbr
        o_bi[...] = bbi

    return pl.pallas_call(
        body, out_shape=[SDS(a_re.shape, F32), SDS(a_re.shape, F32), SDS(b_re.shape, F32), SDS(b_re.shape, F32)],
        compiler_params=_cparams(), name=name)(a_re, a_im, logdt, b_re, b_im)


def _disc_bwd(a_re, a_im, logdt, b_re, b_im, d_lr, d_li, d_bbr, d_bbi, group, name):
    rows, gp = a_re.shape

    def body(ar, ai, ld, br, bi, g_lr, g_li, g_br, g_bi, o_ar, o_ai, o_ld, o_br, o_bi):
        _, vjp = jax.vjp(_disc_math, ar[...], ai[...], ld[...], br[...], bi[...])
        dar, dai, dld, dbr, dbi = vjp((g_lr[...], g_li[...], g_br[...], g_bi[...]))
        o_ar[...] = dar
        o_ai[...] = dai
        state = lax.broadcasted_iota(jnp.int32, (gp, LANES), 0)
        first = lax.broadcasted_iota(jnp.int32, (gp, LANES), 1) * group
        sel = jnp.logical_and(state >= first, state < first + group).astype(F32)
        o_ld[...] = jnp.dot(dld, sel, precision=HIGHEST, preferred_element_type=F32)
        o_br[...] = dbr
        o_bi[...] = dbi

    return pl.pallas_call(
        body, out_shape=[SDS(a_re.shape, F32), SDS(a_re.shape, F32), SDS((rows, LANES), F32),
                         SDS(b_re.shape, F32), SDS(b_re.shape, F32)],
        compiler_params=_cparams(), name=name)(a_re, a_im, logdt, b_re, b_im, d_lr, d_li, d_bbr, d_bbi)


def _scan_maps(nbc, nb):
    nbl = nb - nbc
    fwd0 = lambda i: (i, 0, 0)
    fwd1 = lambda i: (jnp.where(i < nbc, nbc - 1 - i, nb - 1 - (i - nbc)), 0, 0)
    adj0 = lambda i: (nb - 1 - i, 0, 0)
    adj1 = lambda i: (jnp.where(i < nbl, nbc + i, i - nbl), 0, 0)
    return fwd0, fwd1, adj0, adj1


def _scan_fwd(bu0, bu1, lam, n_ctx, name):
    n, s2, _ = bu0.shape
    s = s2 // 2
    tt = math.gcd(math.gcd(n_ctx, n - n_ctx), 128)
    nbc, nb = n_ctx // tt, n // tt
    fwd0, fwd1, _, _ = _scan_maps(nbc, nb)

    def body(b0_ref, b1_ref, lam_ref, h0_ref, h1_ref, st_ref):
        @pl.when(pl.program_id(0) == 0)
        def _():
            st_ref[...] = jnp.zeros(st_ref.shape, F32)

        lr0, li0, lr1, li1 = lam_ref[0], lam_ref[1], lam_ref[2], lam_ref[3]

        def step(j, carry):
            h0r, h0i, h1r, h1i = carry
            t1 = tt - 1 - j
            n0r = lr0 * h0r - li0 * h0i + b0_ref[j, 0:s, :]
            n0i = lr0 * h0i + li0 * h0r + b0_ref[j, s:s2, :]
            n1r = lr1 * h1r - li1 * h1i + b1_ref[t1, 0:s, :]
            n1i = lr1 * h1i + li1 * h1r + b1_ref[t1, s:s2, :]
            h0_ref[j, 0:s, :] = n0r
            h0_ref[j, s:s2, :] = n0i
            h1_ref[t1, 0:s, :] = n1r
            h1_ref[t1, s:s2, :] = n1i
            return n0r, n0i, n1r, n1i

        out = lax.fori_loop(0, tt, step, (st_ref[0], st_ref[1], st_ref[2], st_ref[3]), unroll=2)
        for q in range(4):
            st_ref[q] = out[q]

    blk = (tt, s2, LANES)
    return pl.pallas_call(
        body, grid=(nb,),
        in_specs=[pl.BlockSpec(blk, fwd0), pl.BlockSpec(blk, fwd1), pl.BlockSpec((4, s, LANES), lambda i: (0, 0, 0))],
        out_specs=[pl.BlockSpec(blk, fwd0), pl.BlockSpec(blk, fwd1)],
        out_shape=[SDS(bu0.shape, F32), SDS(bu1.shape, F32)],
        scratch_shapes=[pltpu.VMEM((4, s, LANES), F32)],
        compiler_params=_cparams("arbitrary"), name=name)(bu0, bu1, lam)


def _scan_bwd(dh0, dh1, h0, h1, lam, n_ctx, name):
    n, s2, _ = dh0.shape
    s = s2 // 2
    tt = math.gcd(math.gcd(n_ctx, n - n_ctx), 128)
    nbc, nb = n_ctx // tt, n // tt
    _, _, adj0, adj1 = _scan_maps(nbc, nb)

    def body(d0_ref, d1_ref, h0_ref, h1_ref, lam_ref, a0_ref, a1_ref, dl_ref, st_ref, acc_ref):
        i = pl.program_id(0)

        @pl.when(i == 0)
        def _():
            st_ref[...] = jnp.zeros(st_ref.shape, F32)
            acc_ref[...] = jnp.zeros(acc_ref.shape, F32)

        lr0, li0, lr1, li1 = lam_ref[0], lam_ref[1], lam_ref[2], lam_ref[3]

        def step(j, carry):
            a0r, a0i, a1r, a1i, c0r, c0i, c1r, c1i = carry
            t0 = tt - 1 - j
            g0r, g0i = h0_ref[t0, 0:s, :], h0_ref[t0, s:s2, :]
            g1r, g1i = h1_ref[j, 0:s, :], h1_ref[j, s:s2, :]
            c0r = c0r + a0r * g0r + a0i * g0i
            c0i = c0i + a0i * g0r - a0r * g0i
            c1r = c1r + a1r * g1r + a1i * g1i
            c1i = c1i + a1i * g1r - a1r * g1i
            n0r = lr0 * a0r + li0 * a0i + d0_ref[t0, 0:s, :]
            n0i = lr0 * a0i - li0 * a0r + d0_ref[t0, s:s2, :]
            n1r = lr1 * a1r + li1 * a1i + d1_ref[j, 0:s, :]
            n1i = lr1 * a1i - li1 * a1r + d1_ref[j, s:s2, :]
            a0_ref[t0, 0:s, :] = n0r
            a0_ref[t0, s:s2, :] = n0i
            a1_ref[j, 0:s, :] = n1r
            a1_ref[j, s:s2, :] = n1i
            return n0r, n0i, n1r, n1i, c0r, c0i, c1r, c1i

        init = tuple(st_ref[q] for q in range(4)) + tuple(acc_ref[q] for q in range(4))
        out = lax.fori_loop(0, tt, step, init, unroll=2)
        for q in range(4):
            st_ref[q] = out[q]
            acc_ref[q] = out[4 + q]

        @pl.when(i == nb - 1)
        def _():
            for q in range(4):
                dl_ref[q] = out[4 + q]

    blk = (tt, s2, LANES)
    small = pl.BlockSpec((4, s, LANES), lambda i: (0, 0, 0))
    return pl.pallas_call(
        body, grid=(nb,),
        in_specs=[pl.BlockSpec(blk, adj0), pl.BlockSpec(blk, adj1), pl.BlockSpec(blk, adj0), pl.BlockSpec(blk, adj1), small],
        out_specs=[pl.BlockSpec(blk, adj0), pl.BlockSpec(blk, adj1), small],
        out_shape=[SDS(dh0.shape, F32), SDS(dh1.shape, F32), SDS((4, s, LANES), F32)],
        scratch_shapes=[pltpu.VMEM((4, s, LANES), F32), pltpu.VMEM((4, s, LANES), F32)],
        compiler_params=_cparams("arbitrary"), name=name)(dh0, dh1, h0, h1, lam)


def _gelu(v):
    th = jnp.tanh(GELU_C0 * (v + GELU_C1 * v * v * v))
    return 0.5 * v * (1.0 + th), th


def _ssm_head_fwd(y, u, ssm_d, wg, l, name):
    n, sw = y.shape
    ucol = u.shape[1] // sw - 1
    tm = _pick(n, _TM)

    def body(y_ref, u_ref, d_ref, w_ref, o_ref):
        act, _ = _gelu(y_ref[...] + d_ref[...] * u_ref[...])
        q = jnp.dot(act.astype(MXU_DTYPE), w_ref[...].astype(MXU_DTYPE), preferred_element_type=F32)
        o_ref[...] = (act * jax.nn.sigmoid(q)).astype(o_ref.dtype)

    row = pl.BlockSpec((tm, sw), lambda i: (i, 0))
    return pl.pallas_call(
        body, grid=(n // tm,),
        in_specs=[row, pl.BlockSpec((tm, sw), lambda i: (i, ucol)), pl.BlockSpec((1, sw), lambda i: (0, 0)),
                  pl.BlockSpec((None, sw, sw), lambda i: (l, 0, 0))],
        out_specs=row, out_shape=SDS((n, sw), MXU_DTYPE), compiler_params=_cparams("parallel"), name=name)(y, u, ssm_d, wg)


def _ssm_head_bwd(dcat, y, u, ssm_d, wg, l, name):
    n, sw = y.shape
    ucol = u.shape[1] // sw - 1
    tm = _pick(n, _TM)

    def body(do_ref, y_ref, u_ref, d_ref, w_ref, dy_ref, du_ref, act_ref, dq_ref, dd_ref):
        @pl.when(pl.program_id(0) == 0)
        def _():
            dd_ref[...] = jnp.zeros(dd_ref.shape, F32)

        uv, dv, do = u_ref[...], d_ref[...], do_ref[...]
        yf = y_ref[...] + dv * uv
        act, th = _gelu(yf)
        wv = w_ref[...].astype(MXU_DTYPE)
        sg = jax.nn.sigmoid(jnp.dot(act.astype(MXU_DTYPE), wv, preferred_element_type=F32))
        dq = (do * act * sg * (1.0 - sg)).astype(MXU_DTYPE)
        dact = do * sg + lax.dot_general(dq, wv, _DIMS["nt"], preferred_element_type=F32)
        dgelu = 0.5 * (1.0 + th) + 0.5 * yf * (1.0 - th * th) * GELU_C0 * (1.0 + 3.0 * GELU_C1 * yf * yf)
        dyf = dact * dgelu
        dy_ref[...] = dyf.astype(dy_ref.dtype)
        du_ref[...] = dyf * dv
        act_ref[...] = act.astype(act_ref.dtype)
        dq_ref[...] = dq
        dd_ref[...] += jnp.sum(dyf * uv, axis=0, keepdims=True)

    row = pl.BlockSpec((tm, sw), lambda i: (i, 0))
    last = pl.BlockSpec((tm, sw), lambda i: (i, ucol))
    vec = pl.BlockSpec((1, sw), lambda i: (0, 0))
    return pl.pallas_call(
        body, grid=(n // tm,),
        in_specs=[last, row, last, vec, pl.BlockSpec((None, sw, sw), lambda i: (l, 0, 0))],
        out_specs=[row, row, row, row, vec],
        out_shape=[SDS((n, sw), MXU_DTYPE), SDS((n, sw), F32), SDS((n, sw), MXU_DTYPE), SDS((n, sw), MXU_DTYPE),
                   SDS((1, sw), F32)],
        compiler_params=_cparams("arbitrary"), name=name)(dcat, y, u, ssm_d, wg)


def _assemble_du(du_pool, du_dir, du_proj, name):
    n, pw = du_pool.shape
    sw = du_dir.shape[1]
    tm = _pick(n, _TM)

    def body(p_ref, a_ref, b_ref, o_ref):
        o_ref[:, 0:pw] = p_ref[...].astype(o_ref.dtype)
        o_ref[:, pw:pw + sw] = (a_ref[...] + b_ref[...]).astype(o_ref.dtype)

    return pl.pallas_call(
        body, grid=(n // tm,),
        in_specs=[pl.BlockSpec((tm, pw), lambda i: (i, 0)), pl.BlockSpec((tm, sw), lambda i: (i, 0)),
                  pl.BlockSpec((tm, sw), lambda i: (i, 0))],
        out_specs=pl.BlockSpec((tm, pw + sw), lambda i: (i, 0)), out_shape=SDS((n, pw + sw), MXU_DTYPE),
        compiler_params=_cparams("parallel"), name=name)(du_pool, du_dir, du_proj)


CONV_PAD = GRID_W + SUBLANES


def _conv_layout(n, n_ctx):
    return CONV_PAD, 2 * CONV_PAD + n_ctx, 3 * CONV_PAD + n


def _col_masks(ch):
    col = lax.broadcasted_iota(jnp.int32, (ch, LANES), 0) % GRID_W
    return col != 0, col != GRID_W - 1


def _fill_padded(scr, src_ref, n, n_ctx):
    base_c, base_l, total = _conv_layout(n, n_ctx)
    for base in (0, base_c + n_ctx, base_l + n - n_ctx):
        scr[pl.ds(base, CONV_PAD), :] = jnp.zeros((CONV_PAD, LANES), F32)
    for row0, nseg, base in ((0, n_ctx, base_c), (n_ctx, n - n_ctx, base_l)):
        ch = math.gcd(nseg, 512)

        def copy(ci, carry, row0=row0, base=base, ch=ch):
            c0 = pl.multiple_of(ci * ch, ch)
            scr[pl.ds(base + c0, ch), :] = src_ref[pl.ds(row0 + c0, ch), :]
            return carry

        lax.fori_loop(0, nseg // ch, copy, 0)


def _conv_ctx(scr, k_ref, base, c0, ch, sign):
    acc = scr[pl.ds(c0 + base, ch), :] * k_ref[4:5, :]
    acc = acc + scr[pl.ds(c0 + (base - sign), ch), :] * k_ref[3:4, :]
    return acc + scr[pl.ds(c0 + (base + sign), ch), :] * k_ref[5:6, :]


def _conv_lat(scr, k_ref, base, c0, ch, sign, m_l, m_r):
    cols = []
    for j in range(3):
        acc = None
        for i in range(3):
            off = sign * (GRID_W * (i - 1) + (j - 1))
            term = scr[pl.ds(c0 + (base + off), ch), :] * k_ref[3 * i + j:3 * i + j + 1, :]
            acc = term if acc is None else acc + term
        cols.append(acc)
    first, last = (m_l, m_r) if sign > 0 else (m_r, m_l)
    return cols[1] + jnp.where(first, cols[0], 0.0) + jnp.where(last, cols[2], 0.0)


def _conv_chunk(n_lat):
    return math.gcd(n_lat, 256)


def _conv_glu_fwd(z, wk, n_ctx, name):
    n, f2 = z.shape
    dff = f2 // 2
    nvt = dff // LANES
    n_lat = n - n_ctx
    base_c, base_l, total = _conv_layout(n, n_ctx)
    ch = _conv_chunk(n_lat)
    assert ch % GRID_W == 0

    def body(zv_ref, zg_ref, kv_ref, kg_ref, a_ref, cv_ref, cg_ref, sv, sg):
        _fill_padded(sv, zv_ref, n, n_ctx)
        _fill_padded(sg, zg_ref, n, n_ctx)

        def emit(cv, cg, row, rows):
            cv_ref[pl.ds(row, rows), :] = cv
            cg_ref[pl.ds(row, rows), :] = cg
            a_ref[pl.ds(row, rows), :] = (cv * cg * jax.nn.sigmoid(cg)).astype(a_ref.dtype)

        emit(_conv_ctx(sv, kv_ref, base_c, 0, n_ctx, 1), _conv_ctx(sg, kg_ref, base_c, 0, n_ctx, 1), 0, n_ctx)
        m_l, m_r = _col_masks(ch)

        def lat(ci, carry):
            c0 = pl.multiple_of(ci * ch, ch)
            emit(_conv_lat(sv, kv_ref, base_l, c0, ch, 1, m_l, m_r), _conv_lat(sg, kg_ref, base_l, c0, ch, 1, m_l, m_r),
                 n_ctx + c0, ch)
            return carry

        lax.fori_loop(0, n_lat // ch, lat, 0)

    col = lambda shift: pl.BlockSpec((n, LANES), lambda j: (0, j + shift))
    kcol = lambda shift: pl.BlockSpec((9, LANES), lambda j: (0, j + shift))
    return pl.pallas_call(
        body, grid=(nvt,), in_specs=[col(0), col(nvt), kcol(0), kcol(nvt)], out_specs=[col(0), col(0), col(0)],
        out_shape=[SDS((n, dff), MXU_DTYPE), SDS((n, dff), F32), SDS((n, dff), F32)],
        scratch_shapes=[pltpu.VMEM((total, LANES), F32), pltpu.VMEM((total, LANES), F32)],
        compiler_params=_cparams("parallel"), name=name)(z, z, wk, wk)


def _conv_glu_bwd(z, cv, cg, da, wk, n_ctx, name):
    n, f2 = z.shape
    dff = f2 // 2
    nvt = dff // LANES
    n_lat = n - n_ctx
    base_c, base_l, total = _conv_layout(n, n_ctx)
    ch = _conv_chunk(n_lat)
    assert ch % GRID_W == 0
    ctx_taps = [(1, 0), (1, 1), (1, 2)]
    lat_taps = [(i, j) for i in range(3) for j in range(3)]

    def tap_sums(acc, scr, d, base, c0, rows, taps, masks):
        acc = list(acc)
        for i, j in taps:
            src = scr[pl.ds(c0 + (base + GRID_W * (i - 1) + (j - 1)), rows), :]
            if masks is not None and j != 1:
                src = jnp.where(masks[0] if j == 0 else masks[1], src, 0.0)
            acc[3 * i + j] = acc[3 * i + j] + jnp.sum((src * d).reshape(rows // SUBLANES, SUBLANES, LANES), axis=0)
        return acc

    def body(zv_ref, zg_ref, cv_ref, cg_ref, da_ref, kv_ref, kg_ref, dzv_ref, dzg_ref, dkv_ref, dkg_ref, a_ref,
             sv, sg, dv, dg):
        _fill_padded(sv, zv_ref, n, n_ctx)
        _fill_padded(sg, zg_ref, n, n_ctx)
        for base in (0, base_c + n_ctx, base_l + n_lat):
            dv[pl.ds(base, CONV_PAD), :] = jnp.zeros((CONV_PAD, LANES), F32)
            dg[pl.ds(base, CONV_PAD), :] = jnp.zeros((CONV_PAD, LANES), F32)
        m_l, m_r = _col_masks(ch)

        def first_pass(row, pad_row, rows):
            cv, cg = cv_ref[pl.ds(row, rows), :], cg_ref[pl.ds(row, rows), :]
            sig = jax.nn.sigmoid(cg)
            silu = cg * sig
            a_ref[pl.ds(row, rows), :] = (cv * silu).astype(a_ref.dtype)
            dav = da_ref[pl.ds(row, rows), :]
            dcv = dav * silu
            dcg = dav * cv * (sig * (1.0 + cg * (1.0 - sig)))
            dv[pl.ds(pad_row, rows), :] = dcv
            dg[pl.ds(pad_row, rows), :] = dcg
            return dcv, dcg

        zero = [jnp.zeros((SUBLANES, LANES), F32) for _ in range(9)]
        dcv, dcg = first_pass(0, base_c, n_ctx)
        accv = tap_sums(zero, sv, dcv, base_c, 0, n_ctx, ctx_taps, None)
        accg = tap_sums(zero, sg, dcg, base_c, 0, n_ctx, ctx_taps, None)

        def lat1(ci, carry):
            accv, accg = carry
            c0 = pl.multiple_of(ci * ch, ch)
            dcv, dcg = first_pass(n_ctx + c0, base_l + c0, ch)
            accv = tap_sums(accv, sv, dcv, base_l, c0, ch, lat_taps, (m_l, m_r))
            accg = tap_sums(accg, sg, dcg, base_l, c0, ch, lat_taps, (m_l, m_r))
            return tuple(accv), tuple(accg)

        accv, accg = lax.fori_loop(0, n_lat // ch, lat1, (tuple(accv), tuple(accg)))
        for t in range(9):
            dkv_ref[t:t + 1, :] = jnp.sum(accv[t], axis=0, keepdims=True)
            dkg_ref[t:t + 1, :] = jnp.sum(accg[t], axis=0, keepdims=True)

        dzv_ref[pl.ds(0, n_ctx), :] = _conv_ctx(dv, kv_ref, base_c, 0, n_ctx, -1).astype(dzv_ref.dtype)
        dzg_ref[pl.ds(0, n_ctx), :] = _conv_ctx(dg, kg_ref, base_c, 0, n_ctx, -1).astype(dzg_ref.dtype)

        def lat2(ci, carry):
            c0 = pl.multiple_of(ci * ch, ch)
            dzv_ref[pl.ds(n_ctx + c0, ch), :] = _conv_lat(dv, kv_ref, base_l, c0, ch, -1, m_l, m_r).astype(dzv_ref.dtype)
            dzg_ref[pl.ds(n_ctx + c0, ch), :] = _conv_lat(dg, kg_ref, base_l, c0, ch, -1, m_l, m_r).astype(dzg_ref.dtype)
            return carry

        lax.fori_loop(0, n_lat // ch, lat2, 0)

    col = lambda shift: pl.BlockSpec((n, LANES), lambda j: (0, j + shift))
    kcol = lambda shift: pl.BlockSpec((9, LANES), lambda j: (0, j + shift))
    pad = pltpu.VMEM((total, LANES), F32)
    return pl.pallas_call(
        body, grid=(nvt,), in_specs=[col(0), col(nvt), col(0), col(0), col(0), kcol(0), kcol(nvt)],
        out_specs=[col(0), col(0), kcol(0), kcol(0), col(0)],
        out_shape=[SDS((n, dff), MXU_DTYPE), SDS((n, dff), MXU_DTYPE), SDS((9, dff), F32), SDS((9, dff), F32),
                   SDS((n, dff), MXU_DTYPE)],
        scratch_shapes=[pad, pad, pad, pad],
        compiler_params=_cparams("parallel"), name=name)(z, z, cv, cg, da, wk, wk)


def _silu(v):
    return v * jax.nn.sigmoid(v)


def _ada_fwd(cond, w_ada, b_shard, name):
    nl, d, cols = w_ada.shape
    tn = _pick(cols, (512, 256, 128))

    def body(c_ref, w_ref, b_ref, o_ref):
        o_ref[...] = jnp.dot(_silu(c_ref[...]), w_ref[...], precision=HIGHEST, preferred_element_type=F32) + b_ref[...]

    return pl.pallas_call(
        body, grid=(nl, cols // tn),
        in_specs=[pl.BlockSpec(cond.shape, lambda l, j: (0, 0)), pl.BlockSpec((None, d, tn), lambda l, j: (l, 0, j)),
                  pl.BlockSpec((None, 1, tn), lambda l, j: (l, 0, j))],
        out_specs=pl.BlockSpec((None, cond.shape[0], tn), lambda l, j: (l, 0, j)),
        out_shape=SDS((nl, cond.shape[0], cols), F32),
        compiler_params=_cparams("parallel", "parallel"), name=name)(cond, w_ada, b_shard)


def _ada_dw(cond, dmod, name):
    nl, rows, cols = dmod.shape
    d = cond.shape[1]
    tn = _pick(cols, (512, 256, 128))

    def body(c_ref, g_ref, o_ref):
        o_ref[...] = lax.dot_general(_silu(c_ref[...]), g_ref[...], _DIMS["tn"], precision=HIGHEST,
                                     preferred_element_type=F32)

    return pl.pallas_call(
        body, grid=(nl, cols // tn),
        in_specs=[pl.BlockSpec(cond.shape, lambda l, j: (0, 0)), pl.BlockSpec((None, rows, tn), lambda l, j: (l, 0, j))],
        out_specs=pl.BlockSpec((None, d, tn), lambda l, j: (l, 0, j)), out_shape=SDS((nl, d, cols), F32),
        compiler_params=_cparams("parallel", "parallel"), name=name)(cond, dmod)


def _ada_dcond(dmod, w_ada, name):
    nl, rows, cols = dmod.shape
    d = w_ada.shape[1]
    tn = _pick(cols, (512, 256, 128))

    def body(g_ref, w_ref, o_ref):
        @pl.when(jnp.logical_and(pl.program_id(0) == 0, pl.program_id(1) == 0))
        def _():
            o_ref[...] = jnp.zeros(o_ref.shape, F32)

        o_ref[...] += lax.dot_general(g_ref[...], w_ref[...], _DIMS["nt"], precision=HIGHEST, preferred_element_type=F32)

    return pl.pallas_call(
        body, grid=(nl, cols // tn),
        in_specs=[pl.BlockSpec((None, rows, tn), lambda l, j: (l, 0, j)), pl.BlockSpec((None, d, tn), lambda l, j: (l, 0, j))],
        out_specs=pl.BlockSpec((rows, d), lambda l, j: (0, 0)), out_shape=SDS((rows, d), F32),
        compiler_params=_cparams("arbitrary", "arbitrary"), name=name)(dmod, w_ada)


def _ada_rows(dmod_all, name):
    nd, nl, _, w = dmod_all.shape
    tn = _pick(w, (2048, 1024, 512, 256, 128))

    def body(g_ref, rows_ref, db_ref):
        ctx = g_ref[0, 0, 0:1, :]
        for b in range(1, nd):
            ctx = ctx + g_ref[b, 0, 0:1, :]
        total = ctx
        for b in range(nd):
            lat = g_ref[b, 0, 1:2, :]
            rows_ref[b:b + 1, :] = lat
            total = total + lat
        rows_ref[nd:nd + 1, :] = ctx
        rows_ref[nd + 1:16, :] = jnp.zeros((16 - nd - 1, tn), F32)
        db_ref[...] = total

    return pl.pallas_call(
        body, grid=(nl, w // tn),
        in_specs=[pl.BlockSpec((nd, 1, 2, tn), lambda l, j: (0, l, 0, j))],
        out_specs=[pl.BlockSpec((None, 16, tn), lambda l, j: (l, 0, j)), pl.BlockSpec((None, 1, tn), lambda l, j: (l, 0, j))],
        out_shape=[SDS((nl, 16, w), F32), SDS((nl, 1, w), F32)],
        compiler_params=_cparams("parallel", "parallel"), name=name)(dmod_all)


def _ada_dctx(parts, c_ctx, row, name):
    def body(p_ref, c_ref, o_ref):
        ds = p_ref[0, row:row + 1, :]
        for k in range(1, p_ref.shape[0]):
            ds = ds + p_ref[k, row:row + 1, :]
        cv = c_ref[...]
        sg = jax.nn.sigmoid(cv)
        o_ref[...] = ds * (sg * (1.0 + cv * (1.0 - sg)))

    return pl.pallas_call(body, out_shape=SDS(c_ctx.shape, F32), compiler_params=_cparams(), name=name)(parts, c_ctx)


ROW_BLOCK_BYTES = 1 << 20


def _as_rows(shape):
    size = math.prod(shape)
    cols = shape[-1] if len(shape) >= 2 and shape[-1] % LANES == 0 else _pick(size, (1024, 512, 256, 128))
    rows = size // cols
    fits = [t for t in (512, 256, 128, 64, 32, 16, 8) if t * cols * 4 <= ROW_BLOCK_BYTES]
    return rows, cols, _pick(rows, fits)


def _adamw(w, g, m, v, name):
    rows, cols, tr = _as_rows(w.shape)
    c1 = 1.0 / (1.0 - ADAM_B1 ** ADAM_STEP)
    c2 = 1.0 / (1.0 - ADAM_B2 ** ADAM_STEP)

    def body(w_ref, g_ref, m_ref, v_ref, d_ref, nm_ref, nv_ref):
        gv = g_ref[...]
        nm = ADAM_B1 * m_ref[...] + (1.0 - ADAM_B1) * gv
        nv = ADAM_B2 * v_ref[...] + (1.0 - ADAM_B2) * (gv * gv)
        nm_ref[...] = nm
        nv_ref[...] = nv
        d_ref[...] = -ADAM_LR * ((nm * c1) / (jnp.sqrt(nv * c2) + ADAM_EPS) + ADAM_WD * w_ref[...])

    blk = pl.BlockSpec((tr, cols), lambda i: (i, 0))
    outs = pl.pallas_call(
        body, grid=(rows // tr,), in_specs=[blk] * 4, out_specs=[blk] * 3, out_shape=[SDS((rows, cols), F32)] * 3,
        compiler_params=_cparams("parallel"), name=name)(*[t.reshape(rows, cols) for t in (w, g, m, v)])
    return tuple(o.reshape(w.shape) for o in outs)


def _tile_rows(rows, cols, dtype):
    size = jnp.dtype(dtype).itemsize
    fits = [t for t in (512, 256, 128, 64, 32, 16, 8) if t * cols * size <= ROW_BLOCK_BYTES and t * size >= 32]
    return _pick(rows, fits)


def _scalar_spec(grid, in_specs, out_specs):
    return pltpu.PrefetchScalarGridSpec(num_scalar_prefetch=1, grid=grid, in_specs=in_specs, out_specs=out_specs)


def _place_chunk(shard, k_idx, name):
    nl, rows, cols = shard.shape
    tr = _tile_rows(rows, cols, shard.dtype)

    def body(k_ref, s_ref, o_ref):
        o_ref[...] = s_ref[...]

    return pl.pallas_call(
        body, out_shape=SDS((nl, N_CHIPS, rows, cols), shard.dtype),
        grid_spec=_scalar_spec((nl, rows // tr), [pl.BlockSpec((None, tr, cols), lambda l, i, k: (l, i, 0))],
                               pl.BlockSpec((None, None, tr, cols), lambda l, i, k: (l, k[0], i, 0))),
        compiler_params=_cparams("parallel", "parallel"), name=name)(k_idx, shard)


def _pair_sum(grads, recv, c_idx, name):
    half, nch, rows, cols = recv.shape
    tr = _tile_rows(rows, cols, recv.dtype)

    def body(c_ref, g_ref, r_ref, o_ref):
        o_ref[...] = (g_ref[...].astype(F32) + r_ref[...].astype(F32)).astype(o_ref.dtype)

    blk = pl.BlockSpec((None, None, tr, cols), lambda h, q, i, c: (h, q, i, 0))
    return pl.pallas_call(
        body, out_shape=SDS(recv.shape, recv.dtype),
        grid_spec=_scalar_spec((half, nch, rows // tr),
                               [pl.BlockSpec((None, None, tr, cols), lambda h, q, i, c: (c[0] * half + h, q, i, 0)), blk], blk),
        compiler_params=_cparams("parallel", "parallel", "parallel"), name=name)(c_idx, grads, recv)


def _chip_sum(parts, recv, kc_idx, name):
    half, nch, rows, cols = parts.shape
    tr = _tile_rows(rows, cols, F32)

    def body(kc_ref, p_ref, r_ref, o_ref):
        acc = p_ref[...].astype(F32)
        for s in range(r_ref.shape[0]):
            acc = acc + r_ref[s].astype(F32)
        o_ref[...] = acc

    return pl.pallas_call(
        body, out_shape=SDS((2 * half, rows, cols), F32),
        grid_spec=_scalar_spec((half, rows // tr),
                               [pl.BlockSpec((None, None, tr, cols), lambda h, i, kc: (h, kc[0], i, 0)),
                                pl.BlockSpec((nch - 1, None, tr, cols), lambda h, i, kc: (0, h, i, 0))],
                               pl.BlockSpec((None, tr, cols), lambda h, i, kc: (kc[1] * half + h, i, 0))),
        compiler_params=_cparams("parallel", "parallel"), name=name)(kc_idx, parts, recv)


PIECE_BYTES = 3 << 20
MAX_PIECES = 16
PIECE_ROW_ALIGN = 16


def _coords():
    return lax.axis_index("x"), lax.axis_index("y"), lax.axis_index("c")


def _other_chips(x, y):
    return [(1 - x, y), (x, 1 - y), (1 - x, 1 - y)]


def _row_pieces(rows, nbytes):
    pieces = 1
    while (pieces < MAX_PIECES and nbytes // pieces > PIECE_BYTES and rows % (2 * pieces * PIECE_ROW_ALIGN) == 0):
        pieces *= 2
    step = rows // pieces
    return [pl.ds(i * step, step) for i in range(pieces)]


def _nbytes(shape, dtype):
    return math.prod(shape) * jnp.dtype(dtype).itemsize


def _offsets(counts):
    out, pos = [], 0
    for cnt in counts:
        out.append(pos)
        pos += cnt
    return out, pos


def _gather_chips(shards, placed, name):
    nt = len(shards)
    half = [s.shape[0] // 2 for s in shards]
    pieces = [_row_pieces(s.shape[1], _nbytes((h,) + s.shape[1:], s.dtype)) for s, h in zip(shards, half)]
    base, total = _offsets([len(p) for p in pieces])

    def body(*refs):
        s_refs, o_refs = refs[:nt], refs[2 * nt:3 * nt]
        ssem1, rsem1, ssem2, rsem2 = refs[3 * nt:]
        x, y, c = _coords()
        k = 2 * x + y
        chips = _other_chips(x, y)
        sends = []
        for t in range(nt):
            s_ref, o_ref = s_refs[t], o_refs[t]
            mine = pl.ds(c * half[t], half[t])
            for i, rs in enumerate(pieces[t]):
                for r, (px, py) in enumerate(chips):
                    q = 3 * (base[t] + i) + r
                    cp = pltpu.make_async_remote_copy(
                        src_ref=s_ref.at[mine, rs], dst_ref=o_ref.at[mine, k, rs], send_sem=ssem1.at[q],
                        recv_sem=rsem1.at[q], device_id=(px, py, c), device_id_type=MESH)
                    cp.start()
                    sends.append(cp)
        for t in range(nt):
            s_ref, o_ref = s_refs[t], o_refs[t]
            mine = pl.ds(c * half[t], half[t])
            for i, rs in enumerate(pieces[t]):
                for r, (px, py) in enumerate(chips):
                    q = 3 * (base[t] + i) + r
                    kk = 2 * px + py
                    pltpu.make_async_remote_copy(
                        src_ref=s_ref.at[mine, rs], dst_ref=o_ref.at[mine, kk, rs], send_sem=ssem1.at[q],
                        recv_sem=rsem1.at[q], device_id=(px, py, c), device_id_type=MESH).wait_recv()
                    cp = pltpu.make_async_remote_copy(
                        src_ref=o_ref.at[mine, kk, rs], dst_ref=o_ref.at[mine, kk, rs], send_sem=ssem2.at[q],
                        recv_sem=rsem2.at[q], device_id=(x, y, 1 - c), device_id_type=MESH)
                    cp.start()
                    sends.append(cp)
        for t in range(nt):
            o_ref = o_refs[t]
            theirs = pl.ds((1 - c) * half[t], half[t])
            for i, rs in enumerate(pieces[t]):
                for r, (px, py) in enumerate(chips):
                    q = 3 * (base[t] + i) + r
                    kk = 2 * px + py
                    pltpu.make_async_remote_copy(
                        src_ref=o_ref.at[theirs, kk, rs], dst_ref=o_ref.at[theirs, kk, rs], send_sem=ssem2.at[q],
                        recv_sem=rsem2.at[q], device_id=(x, y, 1 - c), device_id_type=MESH).wait_recv()
        for cp in sends:
            cp.wait_send()

    sem = pltpu.SemaphoreType.DMA
    outs = pl.pallas_call(
        body, in_specs=[ANY] * (2 * nt), out_specs=[ANY] * nt,
        out_shape=[SDS(p.shape, p.dtype) for p in placed],
        input_output_aliases={nt + t: t for t in range(nt)},
        scratch_shapes=[sem((3 * total,)), sem((3 * total,)), sem((3 * total,)), sem((3 * total,))],
        name=name)(*shards, *placed)
    return list(outs)


def _pair_send(grads, name):
    nt = len(grads)
    half = [g.shape[0] // 2 for g in grads]
    pieces = [_row_pieces(g.shape[2], _nbytes((h,) + g.shape[1:], g.dtype)) for g, h in zip(grads, half)]
    base, total = _offsets([len(p) for p in pieces])

    def body(*refs):
        g_refs, o_refs = refs[:nt], refs[nt:2 * nt]
        ssem, rsem = refs[2 * nt:]
        x, y, c = _coords()
        cps = []
        for t in range(nt):
            theirs = pl.ds((1 - c) * half[t], half[t])
            for i, rs in enumerate(pieces[t]):
                q = base[t] + i
                cp = pltpu.make_async_remote_copy(
                    src_ref=g_refs[t].at[theirs, :, rs], dst_ref=o_refs[t].at[:, :, rs], send_sem=ssem.at[q],
                    recv_sem=rsem.at[q], device_id=(x, y, 1 - c), device_id_type=MESH)
                cp.start()
                cps.append(cp)
        for cp in cps:
            cp.wait_recv()
        for cp in cps:
            cp.wait_send()

    sem = pltpu.SemaphoreType.DMA
    outs = pl.pallas_call(
        body, in_specs=[ANY] * nt, out_specs=[ANY] * nt,
        out_shape=[SDS((g.shape[0] // 2,) + g.shape[1:], g.dtype) for g in grads],
        scratch_shapes=[sem((total,)), sem((total,))],
        name=name)(*grads)
    return list(outs)


def _chip_send(parts, name):
    nt = len(parts)
    pieces = [_row_pieces(p.shape[2], _nbytes((p.shape[0],) + p.shape[2:], p.dtype)) for p in parts]
    base, total = _offsets([len(p) for p in pieces])

    def body(*refs):
        p_refs, o_refs = refs[:nt], refs[nt:2 * nt]
        ssem, rsem = refs[2 * nt:]
        x, y, c = _coords()
        cps = []
        for t in range(nt):
            for i, rs in enumerate(pieces[t]):
                for r, (px, py) in enumerate(_other_chips(x, y)):
                    q = 3 * (base[t] + i) + r
                    cp = pltpu.make_async_remote_copy(
                        src_ref=p_refs[t].at[:, 2 * px + py, rs], dst_ref=o_refs[t].at[r, :, rs], send_sem=ssem.at[q],
                        recv_sem=rsem.at[q], device_id=(px, py, c), device_id_type=MESH)
                    cp.start()
                    cps.append(cp)
        for cp in cps:
            cp.wait_recv()
        for cp in cps:
            cp.wait_send()

    sem = pltpu.SemaphoreType.DMA
    outs = pl.pallas_call(
        body, in_specs=[ANY] * nt, out_specs=[ANY] * nt,
        out_shape=[SDS((N_CHIPS - 1, p.shape[0]) + p.shape[2:], p.dtype) for p in parts],
        scratch_shapes=[sem((3 * total,)), sem((3 * total,))],
        name=name)(*parts)
    return list(outs)


def _pair_join(bufs, name):
    nt = len(bufs)
    half = [b.shape[0] // 2 for b in bufs]
    pieces = [_row_pieces(b.shape[1], _nbytes((h,) + b.shape[1:], b.dtype)) for b, h in zip(bufs, half)]
    base, total = _offsets([len(p) for p in pieces])

    def body(*refs):
        o_refs = refs[nt:2 * nt]
        ssem, rsem = refs[2 * nt:]
        x, y, c = _coords()
        cps = []
        for t in range(nt):
            mine = pl.ds(c * half[t], half[t])
            for i, rs in enumerate(pieces[t]):
                q = base[t] + i
                cp = pltpu.make_async_remote_copy(
                    src_ref=o_refs[t].at[mine, rs], dst_ref=o_refs[t].at[mine, rs], send_sem=ssem.at[q],
                    recv_sem=rsem.at[q], device_id=(x, y, 1 - c), device_id_type=MESH)
                cp.start()
                cps.append(cp)
        for t in range(nt):
            theirs = pl.ds((1 - c) * half[t], half[t])
            for i, rs in enumerate(pieces[t]):
                q = base[t] + i
                pltpu.make_async_remote_copy(
                    src_ref=o_refs[t].at[theirs, rs], dst_ref=o_refs[t].at[theirs, rs], send_sem=ssem.at[q],
                    recv_sem=rsem.at[q], device_id=(x, y, 1 - c), device_id_type=MESH).wait_recv()
        for cp in cps:
            cp.wait_send()

    sem = pltpu.SemaphoreType.DMA
    outs = pl.pallas_call(
        body, in_specs=[ANY] * nt, out_specs=[ANY] * nt,
        out_shape=[SDS(b.shape, b.dtype) for b in bufs],
        input_output_aliases={t: t for t in range(nt)},
        scratch_shapes=[sem((total,)), sem((total,))],
        name=name)(*bufs)
    return list(outs)


def _gather_devices(vals, name):
    nt = len(vals)
    flips = [(a, b, e) for a in (0, 1) for b in (0, 1) for e in (0, 1)][1:]

    def body(*refs):
        v_refs, o_refs = refs[:nt], refs[nt:2 * nt]
        lsem, ssem, rsem = refs[2 * nt:]
        x, y, c = _coords()
        me = 4 * x + 2 * y + c
        peers = [((1 - x) if a else x, (1 - y) if b else y, (1 - c) if e else c) for a, b, e in flips]
        cps = []
        for t in range(nt):
            loc = pltpu.make_async_copy(v_refs[t], o_refs[t].at[me], lsem.at[t])
            loc.start()
            cps.append(loc)
            for r, peer in enumerate(peers):
                cp = pltpu.make_async_remote_copy(
                    src_ref=v_refs[t], dst_ref=o_refs[t].at[me], send_sem=ssem.at[7 * t + r],
                    recv_sem=rsem.at[7 * t + r], device_id=peer, device_id_type=MESH)
                cp.start()
                cps.append(cp)
        for t in range(nt):
            for r, (px, py, pc) in enumerate(peers):
                pltpu.make_async_remote_copy(
                    src_ref=v_refs[t], dst_ref=o_refs[t].at[4 * px + 2 * py + pc], send_sem=ssem.at[7 * t + r],
                    recv_sem=rsem.at[7 * t + r], device_id=(px, py, pc), device_id_type=MESH).wait_recv()
        for t in range(nt):
            cps[8 * t].wait()
            for r in range(7):
                cps[8 * t + 1 + r].wait_send()

    sem = pltpu.SemaphoreType.DMA
    outs = pl.pallas_call(
        body, in_specs=[ANY] * nt, out_specs=[ANY] * nt,
        out_shape=[SDS((N_DEV,) + v.shape, v.dtype) for v in vals],
        scratch_shapes=[sem((nt,)), sem((7 * nt,)), sem((7 * nt,))],
        name=name)(*vals)
    return list(outs)


def _gather_all(shards, k_idx, tag):
    placed = [_place_chunk(s, k_idx, f"{tag}_place{t}") for t, s in enumerate(shards)]
    return _gather_chips(shards, placed, f"{tag}_gather")


def _reduce_to_shards(grads, k_idx, c_idx, kc_idx, tag):
    recv = _pair_send(grads, f"{tag}_pair_send")
    pair = [_pair_sum(g, r, c_idx, f"{tag}_pair_sum{t}") for t, (g, r) in enumerate(zip(grads, recv))]
    recv = _chip_send(pair, f"{tag}_chip_send")
    bufs = [_chip_sum(p, r, kc_idx, f"{tag}_chip_sum{t}") for t, (p, r) in enumerate(zip(pair, recv))]
    return _pair_join(bufs, f"{tag}_pair_join")


WEIGHT_NAMES = ("c_ctx", "w_ada", "b_ada", "w_in", "w_pool", "pool_scale", "ssm_a_re", "ssm_a_im", "ssm_log_dt",
                "ssm_b_re", "ssm_b_im", "ssm_c_re", "ssm_c_im", "ssm_d", "w_glu", "w_out", "g_pre_mix", "g_post_mix",
                "g_pre_ffn", "g_post_ffn", "w_up", "w_conv", "w_down")


def _block_diag_in(bb, ng):
    nl, nd, npart, h, gp = bb.shape
    p = gp // ng
    w = jnp.einsum("ldqhgp,kg->lkhdqgp", bb.reshape(nl, nd, npart, h, ng, p), jnp.eye(ng, dtype=bb.dtype))
    return w.reshape(nl, ng * h, nd * npart * gp)


def _block_diag_in_grad(dw, nd, npart, h, ng, p):
    return jnp.einsum("khdqkp->dqhkp", dw.reshape(ng, h, nd, npart, ng, p)).reshape(nd, npart, h, ng * p)


def _block_diag_out(cs, ng):
    nl, nd, npart, _, h, p = cs.shape
    w = jnp.einsum("ldqghp,kg->ldqkpgh", cs, jnp.eye(ng, dtype=cs.dtype))
    return w.reshape(nl, nd * npart * ng * p, ng * h)


def _block_diag_out_grad(dw, nd, npart, h, ng, p):
    return jnp.einsum("dqkpkh->dqkhp", dw.reshape(nd, npart, ng, p, ng, h))


def kernel(x, c, ctx, c_ctx, w_ada, b_ada, w_in, w_pool, pool_scale, ssm_a_re, ssm_a_im, ssm_log_dt, ssm_b_re, ssm_b_im, ssm_c_re, ssm_c_im, ssm_d, w_glu, w_out, g_pre_mix, g_post_mix, g_pre_ffn, g_post_ffn, w_up, w_conv, w_down, loss_target, m_c_ctx, m_w_ada, m_b_ada, m_w_in, m_w_pool, m_pool_scale, m_ssm_a_re, m_ssm_a_im, m_ssm_log_dt, m_ssm_b_re, m_ssm_b_im, m_ssm_c_re, m_ssm_c_im, m_ssm_d, m_w_glu, m_w_out, m_g_pre_mix, m_g_post_mix, m_g_pre_ffn, m_g_post_ffn, m_w_up, m_w_conv, m_w_down, v_c_ctx, v_w_ada, v_b_ada, v_w_in, v_w_pool, v_pool_scale, v_ssm_a_re, v_ssm_a_im, v_ssm_log_dt, v_ssm_b_re, v_ssm_b_im, v_ssm_c_re, v_ssm_c_im, v_ssm_d, v_w_glu, v_w_out, v_g_pre_mix, v_g_post_mix, v_g_pre_ffn, v_g_post_ffn, v_w_up, v_w_conv, v_w_down):
    weights = dict(zip(WEIGHT_NAMES, (c_ctx, w_ada, b_ada, w_in, w_pool, pool_scale, ssm_a_re, ssm_a_im, ssm_log_dt,
                                      ssm_b_re, ssm_b_im, ssm_c_re, ssm_c_im, ssm_d, w_glu, w_out, g_pre_mix, g_post_mix,
                                      g_pre_ffn, g_post_ffn, w_up, w_conv, w_down)))
    mom1 = dict(zip(WEIGHT_NAMES, (m_c_ctx, m_w_ada, m_b_ada, m_w_in, m_w_pool, m_pool_scale, m_ssm_a_re, m_ssm_a_im,
                                   m_ssm_log_dt, m_ssm_b_re, m_ssm_b_im, m_ssm_c_re, m_ssm_c_im, m_ssm_d, m_w_glu, m_w_out,
                                   m_g_pre_mix, m_g_post_mix, m_g_pre_ffn, m_g_post_ffn, m_w_up, m_w_conv, m_w_down)))
    mom2 = dict(zip(WEIGHT_NAMES, (v_c_ctx, v_w_ada, v_b_ada, v_w_in, v_w_pool, v_pool_scale, v_ssm_a_re, v_ssm_a_im,
                                   v_ssm_log_dt, v_ssm_b_re, v_ssm_b_im, v_ssm_c_re, v_ssm_c_im, v_ssm_d, v_w_glu, v_w_out,
                                   v_g_pre_mix, v_g_post_mix, v_g_pre_ffn, v_g_post_ffn, v_w_up, v_w_conv, v_w_down)))

    xi, yi, ci = lax.axis_index("x"), lax.axis_index("y"), lax.axis_index("c")
    chip = 2 * xi + yi
    dev = 4 * xi + 2 * yi + ci
    nl = w_in.shape[0]
    n_lat, d = x.shape[1], x.shape[2]
    n_ctx = ctx.shape[1]
    n = n_ctx + n_lat
    _, ndir, ng, nstate, nh = ssm_b_re.shape
    gp = ng * nstate
    sw = ng * nh
    n_pool_groups, pool_group = w_pool.shape[1], w_pool.shape[3]
    pw = n_pool_groups * pool_group
    assert pw + sw == d and pw % sw == 0 and len(POOL_WINDOWS) == n_pool_groups and n_lat % GRID_W == 0
    dff2 = w_up.shape[2] * N_CHIPS
    ada_w = w_ada.shape[2] * N_CHIPS
    ada_cols = w_ada.shape[2]
    s_rows = gp // LANES

    c_pad = jnp.concatenate([c, jnp.zeros((SUBLANES - 1, d), F32)], axis=0)
    c_all = _gather_devices([c_pad], "gather_cond")[0][:, 0, :]
    cond = jnp.concatenate([c_all, c_ctx[None, :], jnp.zeros((16 - N_DEV - 1, d), F32)], axis=0)
    b_shard = lax.dynamic_slice_in_dim(b_ada, chip * ada_cols, ada_cols, axis=1)[:, None, :]
    mod_shard = _ada_fwd(cond, w_ada, b_shard, "ada_fwd")
    k_idx, c_idx, kc_idx = jnp.stack([chip]), jnp.stack([ci]), jnp.stack([chip, ci])
    mod_all = _gather_all([mod_shard], k_idx, "mods")[0]
    mod_all = jnp.transpose(mod_all, (0, 2, 1, 3)).reshape(nl, 16, ada_w)
    mod_lat = lax.dynamic_index_in_dim(mod_all, dev, axis=1, keepdims=False).reshape(nl, 6, d)
    mod_ctx = mod_all[:, N_DEV].reshape(nl, 6, d)
    mods = jnp.concatenate([jnp.stack([mod_ctx, mod_lat], axis=1), jnp.zeros((nl, 2, 2, d), F32)], axis=2)

    shards = [w_in.astype(COMM_DTYPE), w_pool.reshape(nl, pw // N_CHIPS, pool_group).astype(COMM_DTYPE),
              w_glu.astype(COMM_DTYPE), w_out.astype(COMM_DTYPE), w_up.astype(COMM_DTYPE), w_down.astype(COMM_DTYPE),
              w_conv.reshape(nl, 9, dff2 // N_CHIPS)]
    g_in, g_pool, g_glu, g_out, g_up, g_down, g_conv = _gather_all(shards, k_idx, "weights")
    wi = g_in.reshape(nl, d, d)
    wp = jnp.transpose(g_pool.reshape(nl, N_CHIPS, n_pool_groups, pool_group // N_CHIPS, pool_group),
                       (0, 2, 1, 3, 4)).reshape(nl, n_pool_groups, pool_group, pool_group)
    wg = g_glu.reshape(nl, sw, sw)
    wo = g_out.reshape(nl, d, d)
    wu = g_up
    wd = g_down.reshape(nl, dff2 // 2, d)
    wk = jnp.transpose(g_conv, (0, 2, 1, 3)).reshape(nl, 9, dff2)

    rows = nl * ndir
    a_re2 = ssm_a_re.reshape(rows, gp)
    a_im2 = ssm_a_im.reshape(rows, gp)
    logdt2 = jnp.repeat(ssm_log_dt.reshape(rows, ng), nstate, axis=1)
    b_re2 = jnp.transpose(ssm_b_re.reshape(rows, gp, nh), (0, 2, 1))
    b_im2 = jnp.transpose(ssm_b_im.reshape(rows, gp, nh), (0, 2, 1))
    lam_re, lam_im, bb_re, bb_im = _disc_fwd(a_re2, a_im2, logdt2, b_re2, b_im2, "s5_discretise")
    lam = jnp.stack([lam_re.reshape(nl, ndir, s_rows, LANES), lam_im.reshape(nl, ndir, s_rows, LANES)], axis=2)
    lam = lam.reshape(nl, 2 * ndir, s_rows, LANES)
    bbs = jnp.stack([bb_re.reshape(nl, ndir, nh, gp), bb_im.reshape(nl, ndir, nh, gp)], axis=2)
    w_b = _block_diag_in(bbs, ng).astype(MXU_DTYPE)
    w_b = [w_b[:, :, dr * 2 * gp:(dr + 1) * 2 * gp] for dr in range(ndir)]
    cs = jnp.stack([ssm_c_re, -ssm_c_im], axis=2)
    w_c = _block_diag_out(cs, ng).astype(MXU_DTYPE)
    w_c = [w_c[:, dr * 2 * gp:(dr + 1) * 2 * gp] for dr in range(ndir)]

    def row(v, l):
        return v[l:l + 1]

    xc = jnp.concatenate([ctx[0], x[0]], axis=0)
    saved = []
    for l in range(nl):
        t = f"l{l}"
        md = mods[l]
        h = _norm_mod_fwd(xc, row(g_pre_mix, l), md, 0, 1, n_ctx, f"{t}_pre_mix")
        u = _mm(h, wi, "nn", F32, f"{t}_in_proj", b_idx=l)
        p = _pool(u, pw, pool_group, n_ctx, False, MXU_DTYPE, f"{t}_pool")
        ypool = _pool_proj_fwd(p, wp, l, row(pool_scale, l), f"{t}_pool_proj")
        bu0 = _mm(u, w_b[0], "nn", F32, f"{t}_s5_in0", b_idx=l, a_cols=(pw, sw), out3d=True)
        bu1 = _mm(u, w_b[1], "nn", F32, f"{t}_s5_in1", b_idx=l, a_cols=(pw, sw), out3d=True)
        h0, h1 = _scan_fwd(bu0, bu1, lam[l], n_ctx, f"{t}_scan")
        y = _mm(h0, w_c[0], "nn", F32, f"{t}_s5_out0", b_idx=l, a3d=True)
        y = _mm(h1, w_c[1], "nn", F32, f"{t}_s5_out1", b_idx=l, a3d=True, add=y)
        s_out = _ssm_head_fwd(y, u, row(ssm_d, l), wg, l, f"{t}_s5_head")
        cat = jnp.concatenate([ypool, s_out], axis=1)
        mix = _mm(cat, wo, "nn", F32, f"{t}_out_proj", b_idx=l)
        x_mid = _gate_res_fwd(xc, mix, row(g_post_mix, l), md, 2, n_ctx, f"{t}_post_mix")
        h2 = _norm_mod_fwd(x_mid, row(g_pre_ffn, l), md, 3, 4, n_ctx, f"{t}_pre_ffn")
        z = _mm(h2, wu, "nn", F32, f"{t}_up", b_idx=l, b_chunks=N_CHIPS)
        act, cv, cg = _conv_glu_fwd(z, wk[l], n_ctx, f"{t}_conv_glu")
        f = _mm(act, wd, "nn", F32, f"{t}_down", b_idx=l)
        x_out = _gate_res_fwd(x_mid, f, row(g_post_ffn, l), md, 5, n_ctx, f"{t}_post_ffn")
        saved.append(dict(xc=xc, h=h, u=u, p=p, h0=h0, h1=h1, y=y, cat=cat, mix=mix, x_mid=x_mid, h2=h2, z=z, cv=cv, cg=cg, f=f))
        xc = x_out

    dx, loss_tile = _loss_grad(xc, loss_target[0], n_ctx, "loss")
    loss = lax.psum(loss_tile[0, 0], ("x", "y", "c"))

    big = {k: [None] * nl for k in ("w_in", "w_pool", "w_glu", "w_out", "w_up", "w_down")}
    small = {k: [None] * nl for k in ("pool_scale", "ssm_d", "g_pre_mix", "g_post_mix", "g_pre_ffn", "g_post_ffn",
                                      "lam", "bb", "cs", "w_conv")}
    dmods = [None] * nl
    for l in reversed(range(nl)):
        t = f"l{l}b"
        md = mods[l]
        sv = saved[l]
        df, dgate_ffn, small["g_post_ffn"][l] = _gate_res_bwd(dx, sv["f"], row(g_post_ffn, l), md, 5, n_ctx, f"{t}_post_ffn")
        dact = _mm(df, wd, "nt", F32, f"{t}_down_dx", b_idx=l)
        dzv, dzg, dkv, dkg, act = _conv_glu_bwd(sv["z"], sv["cv"], sv["cg"], dact, wk[l], n_ctx, f"{t}_conv_glu")
        dz = jnp.concatenate([dzv, dzg], axis=1)
        small["w_conv"][l] = jnp.concatenate([dkv, dkg], axis=1)
        big["w_down"][l] = _mm(act, df, "tn", COMM_DTYPE, f"{t}_down_dw")
        big["w_up"][l] = _mm(sv["h2"], dz, "tn", COMM_DTYPE, f"{t}_up_dw", out_chunks=N_CHIPS)
        dh2 = _mm(dz, wu, "nt", F32, f"{t}_up_dx", b_idx=l, b_chunks=N_CHIPS)
        dx, dss_ffn, small["g_pre_ffn"][l] = _norm_mod_bwd(dh2, sv["x_mid"], row(g_pre_ffn, l), md, 3, 4, dx, n_ctx,
                                                           f"{t}_pre_ffn")
        dmix, dgate_mix, small["g_post_mix"][l] = _gate_res_bwd(dx, sv["mix"], row(g_post_mix, l), md, 2, n_ctx,
                                                                f"{t}_post_mix")
        dcat = _mm(dmix, wo, "nt", F32, f"{t}_out_dx", b_idx=l)
        big["w_out"][l] = _mm(sv["cat"], dmix, "tn", COMM_DTYPE, f"{t}_out_dw")
        dp, small["pool_scale"][l], big["w_pool"][l] = _pool_proj_bwd(sv["p"], dcat, wp, l, row(pool_scale, l),
                                                                      f"{t}_pool_proj")
        du_pool = _pool(dp, pw, pool_group, n_ctx, True, F32, f"{t}_pool")
        dy, du_dir, gact, dq, small["ssm_d"][l] = _ssm_head_bwd(dcat, sv["y"], sv["u"], row(ssm_d, l), wg, l, f"{t}_s5_head")
        big["w_glu"][l] = _mm(gact, dq, "tn", COMM_DTYPE, f"{t}_glu_dw")
        dh0 = _mm(dy, w_c[0], "nt", F32, f"{t}_s5_out_dx0", b_idx=l, out3d=True)
        dh1 = _mm(dy, w_c[1], "nt", F32, f"{t}_s5_out_dx1", b_idx=l, out3d=True)
        dwc = jnp.concatenate([_mm(sv["h0"], dy, "tn", F32, f"{t}_s5_out_dw0", a3d=True),
                               _mm(sv["h1"], dy, "tn", F32, f"{t}_s5_out_dw1", a3d=True)], axis=0)
        small["cs"][l] = _block_diag_out_grad(dwc, ndir, 2, nh, ng, nstate)
        a0, a1, small["lam"][l] = _scan_bwd(dh0, dh1, sv["h0"], sv["h1"], lam[l], n_ctx, f"{t}_scan")
        du_proj = _mm(a0, w_b[0], "nt", F32, f"{t}_s5_in_dx0", b_idx=l, a3d=True)
        du_proj = _mm(a1, w_b[1], "nt", F32, f"{t}_s5_in_dx1", b_idx=l, a3d=True, add=du_proj)
        dwb = jnp.concatenate([_mm(sv["u"], a0, "tn", F32, f"{t}_s5_in_dw0", a_cols=(pw, sw), b3d=True),
                               _mm(sv["u"], a1, "tn", F32, f"{t}_s5_in_dw1", a_cols=(pw, sw), b3d=True)], axis=1)
        small["bb"][l] = _block_diag_in_grad(dwb, ndir, 2, nh, ng, nstate)
        du = _assemble_du(du_pool, du_dir, du_proj, f"{t}_du")
        dh = _mm(du, wi, "nt", F32, f"{t}_in_dx", b_idx=l)
        big["w_in"][l] = _mm(sv["h"], du, "tn", COMM_DTYPE, f"{t}_in_dw")
        dx, dss_mix, small["g_pre_mix"][l] = _norm_mod_bwd(dh, sv["xc"], row(g_pre_mix, l), md, 0, 1, dx, n_ctx,
                                                           f"{t}_pre_mix")
        dmods[l] = jnp.concatenate([dss_mix, dgate_mix, dss_ffn, dgate_ffn], axis=1).reshape(2, ada_w)

    grad_x = dx[n_ctx:][None]

    dmod_all = _gather_devices([jnp.stack(dmods, axis=0)], "gather_dmods")[0]
    ada_rows, db_ada = _ada_rows(dmod_all, "ada_rows")
    rows_shard = lax.dynamic_slice_in_dim(ada_rows, chip * ada_cols, ada_cols, axis=2)
    dcond_part = _ada_dcond(rows_shard, w_ada, "ada_dcond")
    dcond_parts = _gather_devices([dcond_part], "gather_dcond")[0][0::2]
    grads = {"w_ada": _ada_dw(cond, rows_shard, "ada_dw"), "b_ada": db_ada[:, 0, :],
             "c_ctx": _ada_dctx(dcond_parts, c_ctx[None, :], N_DEV, "ada_dctx")[0]}

    stacked = {k: jnp.stack(v, axis=0) for k, v in big.items()}
    parts = [stacked["w_in"].reshape(nl, N_CHIPS, d // N_CHIPS, d),
             jnp.transpose(stacked["w_pool"].astype(COMM_DTYPE).reshape(nl, n_pool_groups, N_CHIPS, pool_group // N_CHIPS,
                                                                      pool_group), (0, 2, 1, 3, 4))
             .reshape(nl, N_CHIPS, pw // N_CHIPS, pool_group),
             stacked["w_glu"].reshape(nl, N_CHIPS, sw // N_CHIPS, sw),
             stacked["w_out"].reshape(nl, N_CHIPS, d // N_CHIPS, d),
             stacked["w_up"],
             stacked["w_down"].reshape(nl, N_CHIPS, dff2 // 2 // N_CHIPS, d)]
    r_in, r_pool, r_glu, r_out, r_up, r_down = _reduce_to_shards(parts, k_idx, c_idx, kc_idx, "big")
    grads.update(w_in=r_in, w_pool=r_pool.reshape(w_pool.shape), w_glu=r_glu, w_out=r_out, w_up=r_up, w_down=r_down)

    order = ("pool_scale", "ssm_d", "g_pre_mix", "g_post_mix", "g_pre_ffn", "g_post_ffn", "lam", "bb", "cs", "w_conv")
    pieces = [jnp.stack(small[k], axis=0) for k in order]
    flat = jnp.concatenate([q.reshape(-1) for q in pieces])
    unit = nl * N_CHIPS * SUBLANES * 1024
    padded = -(-flat.shape[0] // unit) * unit
    flat = jnp.concatenate([flat, jnp.zeros((padded - flat.shape[0],), F32)])
    vec = flat.reshape(nl, N_CHIPS, padded // (nl * N_CHIPS * 1024), 1024)
    vec = _gather_all(_reduce_to_shards([vec], k_idx, c_idx, kc_idx, "small"), k_idx, "small_all")[0].reshape(-1)
    red, pos = {}, 0
    for k, q in zip(order, pieces):
        red[k] = vec[pos:pos + q.size].reshape(q.shape)
        pos += q.size
    for k in ("pool_scale", "ssm_d", "g_pre_mix", "g_post_mix", "g_pre_ffn", "g_post_ffn"):
        grads[k] = red[k][:, 0, :]
    dlam = red["lam"].reshape(nl, ndir, 2, gp)
    dbb = red["bb"].reshape(nl, ndir, 2, nh, gp)
    d_are, d_aim, d_ldt, d_bre, d_bim = _disc_bwd(
        a_re2, a_im2, logdt2, b_re2, b_im2, dlam[:, :, 0].reshape(rows, gp), dlam[:, :, 1].reshape(rows, gp),
        dbb[:, :, 0].reshape(rows, nh, gp), dbb[:, :, 1].reshape(rows, nh, gp), nstate, "s5_discretise_bwd")
    grads["ssm_a_re"] = d_are.reshape(ssm_a_re.shape)
    grads["ssm_a_im"] = d_aim.reshape(ssm_a_im.shape)
    grads["ssm_log_dt"] = d_ldt[:, :ng].reshape(ssm_log_dt.shape)
    grads["ssm_b_re"] = jnp.transpose(d_bre, (0, 2, 1)).reshape(ssm_b_re.shape)
    grads["ssm_b_im"] = jnp.transpose(d_bim, (0, 2, 1)).reshape(ssm_b_im.shape)
    grads["ssm_c_re"] = red["cs"][:, :, 0]
    grads["ssm_c_im"] = -red["cs"][:, :, 1]
    conv_cols = dff2 // N_CHIPS
    grads["w_conv"] = lax.dynamic_slice_in_dim(red["w_conv"], chip * conv_cols, conv_cols, axis=2).reshape(w_conv.shape)

    delta, new_m, new_v = {}, {}, {}
    for k in WEIGHT_NAMES:
        delta[k], new_m[k], new_v[k] = _adamw(weights[k], grads[k], mom1[k], mom2[k], f"adamw_{k}")
    return (loss, grad_x, *[grads[k] for k in WEIGHT_NAMES], *[delta[k] for k in WEIGHT_NAMES],
            *[new_m[k] for k in WEIGHT_NAMES], *[new_v[k] for k in WEIGHT_NAMES])
```

```python
import math

import jax
import jax.numpy as jnp
from jax import lax
from jax.experimental import pallas as pl
from jax.experimental.pallas import tpu as pltpu

F32 = jnp.float32
MXU_DTYPE = jnp.bfloat16
COMM_DTYPE = jnp.bfloat16
HIGHEST = lax.Precision.HIGHEST
VMEM_LIMIT_BYTES = 48 * 1024 * 1024
LANES = 128
SUBLANES = 8
N_CHIPS = 4
N_DEV = 8

EPS = 1e-6
GRID_W = 64
POOL_WINDOWS = (2, 4, 8, 16)
ADAM_LR = 0.001
ADAM_B1 = 0.9
ADAM_B2 = 0.999
ADAM_EPS = 1e-08
ADAM_WD = 0.01
ADAM_STEP = 10
GELU_C0 = math.sqrt(2.0 / math.pi)
GELU_C1 = 0.044715

SDS = jax.ShapeDtypeStruct
ANY = pl.BlockSpec(memory_space=pl.ANY)
MESH = pl.DeviceIdType.MESH


def _cparams(*sem):
    return pltpu.CompilerParams(dimension_semantics=sem if sem else None, vmem_limit_bytes=VMEM_LIMIT_BYTES)


def _pick(n, cands):
    for cand in cands:
        if n % cand == 0:
            return cand
    return n


def _row_tile(n_ctx, n):
    return math.gcd(math.gcd(n_ctx, n - n_ctx), 256)


_DIMS = {"nn": (((1,), (0,)), ((), ())), "nt": (((1,), (1,)), ((), ())), "tn": (((0,), (0,)), ((), ()))}
_TM = (1088, 1024, 512, 384, 256, 128, 64, 32, 16, 8)
_TN = (1024, 1408, 512, 384, 256, 128)
_TK = (1024, 1088, 512, 1408, 384, 256, 128, 64, 32, 16, 8)


def _chunk_of(idx, per, chunks):
    out = 0
    for q in range(1, chunks):
        out = out + (idx >= q * per).astype(jnp.int32)
    return out


def _within(idx, per, chunks):
    return idx - per * _chunk_of(idx, per, chunks)


def _mm_call(args, mode, out_dtype, name, grid, tiles, specs, o_spec, o_shape, a3d=False, b3d=False, out3d=False,
             add=False):
    tm, tn, _ = tiles
    nk = grid[2]

    def operand(ref, is3d):
        v = ref[...]
        if is3d:
            v = pltpu.einshape("tjl->t(jl)", v)
        return v.astype(MXU_DTYPE)

    def body(*refs):
        a_ref, b_ref, o_ref, acc_ref = refs[0], refs[1], refs[-2], refs[-1]
        kk = pl.program_id(2)

        @pl.when(kk == 0)
        def _():
            acc_ref[...] = jnp.zeros(acc_ref.shape, F32)

        acc_ref[...] += lax.dot_general(operand(a_ref, a3d), operand(b_ref, b3d), _DIMS[mode], preferred_element_type=F32)

        @pl.when(kk == nk - 1)
        def _():
            total = acc_ref[...]
            if add:
                total = total + refs[2][...]
            if out3d:
                total = pltpu.einshape("t(jl)->tjl", total, l=LANES)
            o_ref[...] = total.astype(o_ref.dtype)

    return pl.pallas_call(
        body, grid=grid, in_specs=specs, out_specs=o_spec, out_shape=SDS(o_shape, out_dtype),
        scratch_shapes=[pltpu.VMEM((tm, tn), F32)],
        compiler_params=_cparams("parallel", "parallel", "arbitrary"), name=name)(*args)


def _lanes3(blk, imap):
    return (blk[0], blk[1] // LANES, LANES), (lambda i, j, kk: imap(i, j, kk) + (0,))


def _mm(a, b, mode, out_dtype, name, a_idx=None, b_idx=None, a_cols=None, a_chunks=None, b_chunks=None, out_chunks=None,
        a3d=False, b3d=False, out3d=False, add=None):
    a2, b2 = a.shape[-2:], b.shape[-2:]
    if a3d:
        a2 = (a.shape[0], a.shape[1] * a.shape[2])
    if b3d:
        assert mode == "tn"
        b2 = (b.shape[0], b.shape[1] * b.shape[2])
    if a_chunks is not None:
        assert mode == "nt" and a.shape[-3] == a_chunks
        a2 = (a2[0], a2[1] * a_chunks)
    if b_chunks is not None:
        assert b.shape[-3] == b_chunks
        b2 = (b2[0], b2[1] * b_chunks)
    alast = a2[1] if a_cols is None else a_cols[1]
    if mode == "nn":
        m, k, n = a2[0], alast, b2[1]
        assert b2[0] == k
    elif mode == "nt":
        m, k, n = a2[0], alast, b2[0]
        assert b2[1] == k
    else:
        k, m, n = a2[0], alast, b2[1]
        assert b2[0] == k
    n_unit = n // (out_chunks or 1) // (b_chunks if b_chunks and mode != "nt" else 1)
    k_unit = k // (b_chunks if b_chunks and mode == "nt" else 1) // (a_chunks or 1)
    tm, tn, tk = _pick(m, _TM), _pick(n_unit, _TN), _pick(k_unit, _TK)
    nk = k // tk
    a_lane_tile = tm if mode == "tn" else tk
    off = 0
    if a_cols is not None:
        assert a_cols[0] % a_lane_tile == 0
        off = a_cols[0] // a_lane_tile

    if mode == "tn":
        a_blk, a_map = (tk, tm), (lambda i, j, kk: (kk, i + off))
    else:
        a_blk, a_map = (tm, tk), (lambda i, j, kk: (i, kk + off))
        if a_chunks is not None:
            aper = k // a_chunks // tk
            a_blk, a_map = (None, tm, tk), (lambda i, j, kk: (_chunk_of(kk, aper, a_chunks), i, _within(kk, aper, a_chunks)))
    if mode == "nt":
        b_blk, b_map = (tn, tk), (lambda i, j, kk: (j, kk))
        if b_chunks is not None:
            per = k // b_chunks // tk
            b_blk, b_map = (None, tn, tk), (lambda i, j, kk: (_chunk_of(kk, per, b_chunks), j, _within(kk, per, b_chunks)))
    else:
        b_blk, b_map = (tk, tn), (lambda i, j, kk: (kk, j))
        if b_chunks is not None:
            per = n // b_chunks // tn
            b_blk, b_map = (None, tk, tn), (lambda i, j, kk: (_chunk_of(j, per, b_chunks), kk, _within(j, per, b_chunks)))
    if a3d:
        a_blk, a_map = _lanes3(a_blk, a_map)
    if b3d:
        b_blk, b_map = _lanes3(b_blk, b_map)
    if a_idx is not None:
        a_blk, a_map0 = (None,) + a_blk, a_map
        a_map = lambda i, j, kk: (a_idx,) + a_map0(i, j, kk)
    if b_idx is not None:
        b_blk, b_map0 = (None,) + b_blk, b_map
        b_map = lambda i, j, kk: (b_idx,) + b_map0(i, j, kk)
    o_blk, o_map, o_shape = (tm, tn), (lambda i, j, kk: (i, j)), (m, n)
    if out_chunks is not None:
        oper = n // out_chunks // tn
        o_map = lambda i, j, kk: (_chunk_of(j, oper, out_chunks), i, _within(j, oper, out_chunks))
        o_blk, o_shape = (None, tm, tn), (out_chunks, m, n // out_chunks)
    if out3d:
        o_blk, o_map = _lanes3(o_blk, o_map)
        o_shape = (m, n // LANES, LANES)
    specs, args = [pl.BlockSpec(a_blk, a_map), pl.BlockSpec(b_blk, b_map)], [a, b]
    if add is not None:
        specs.append(pl.BlockSpec((tm, tn), lambda i, j, kk: (i, j)))
        args.append(add)
    return _mm_call(args, mode, out_dtype, name, (m // tm, n // tn, nk), (tm, tn, tk), specs, pl.BlockSpec(o_blk, o_map),
                    o_shape, a3d, b3d, out3d, add is not None)


S5_BAND = 2


def _mm_s5(a, b, kind, name, b_idx=None, a_cols=None, add=None):
    nb = S5_BAND
    wide3 = a if kind in ("in_dx", "out", "out_dw") else (b if kind == "in_dw" else None)
    if kind in ("in", "out_dx"):
        rows, wide = a.shape[0], b.shape[-1] if kind == "in" else b.shape[-2]
    else:
        rows, wide = wide3.shape[0], wide3.shape[1] * LANES
    sw = a_cols[1] if a_cols is not None else (b.shape[-1] if kind in ("out", "out_dw") else
                                                 (a.shape[1] if kind == "out_dx" else b.shape[-2]))
    tw, ts = wide // (2 * nb), sw // nb
    nwt = wide // tw
    off = 0 if a_cols is None else a_cols[0] // ts
    half = lambda t: _within(t, nb, nwt // nb)
    lead = (lambda blk, imap: (blk, imap)) if b_idx is None else (
        lambda blk, imap: ((None,) + blk, lambda i, j, kk: (b_idx,) + imap(i, j, kk)))
    rt = _pick(rows, _TM)
    if kind in ("in", "out_dx"):
        mode = "nn" if kind == "in" else "nt"
        a_spec = pl.BlockSpec((rt, ts), lambda i, j, kk: (i, off + half(j)))
        b_blk, b_map = ((ts, tw), lambda i, j, kk: (half(j), j)) if kind == "in" else ((tw, ts), lambda i, j, kk: (j, half(j)))
        o_blk, o_map = _lanes3((rt, tw), lambda i, j, kk: (i, j))
        return _mm_call([a, b], mode, F32, name, (rows // rt, nwt, 1), (rt, tw, ts), [a_spec, pl.BlockSpec(*lead(b_blk, b_map))],
                        pl.BlockSpec(o_blk, o_map), (rows, wide // LANES, LANES), out3d=True)
    if kind in ("out", "in_dx"):
        mode = "nn" if kind == "out" else "nt"
        a_blk, a_map = _lanes3((rt, tw), lambda i, j, kk: (i, kk * nb + j))
        b_blk, b_map = ((tw, ts), lambda i, j, kk: (kk * nb + j, j)) if kind == "out" else (
            (ts, tw), lambda i, j, kk: (j, kk * nb + j))
        specs, args = [pl.BlockSpec(a_blk, a_map), pl.BlockSpec(*lead(b_blk, b_map))], [a, b]
        if add is not None:
            specs.append(pl.BlockSpec((rt, ts), lambda i, j, kk: (i, j)))
            args.append(add)
        return _mm_call(args, mode, F32, name, (rows // rt, nb, nwt // nb), (rt, ts, tw), specs,
                        pl.BlockSpec((rt, ts), lambda i, j, kk: (i, j)), (rows, sw), a3d=True, add=add is not None)
    kt = _pick(rows, _TK)
    if kind == "out_dw":
        a_blk, a_map = _lanes3((kt, tw), lambda i, j, kk: (kk, i))
        return _mm_call([a, b], "tn", F32, name, (nwt, 1, rows // kt), (tw, ts, kt),
                        [pl.BlockSpec(a_blk, a_map), pl.BlockSpec((kt, ts), lambda i, j, kk: (kk, half(i)))],
                        pl.BlockSpec((tw, ts), lambda i, j, kk: (i, 0)), (wide, ts), a3d=True)
    assert kind == "in_dw"
    b_blk, b_map = _lanes3((kt, tw), lambda i, j, kk: (kk, j))
    return _mm_call([a, b], "tn", F32, name, (1, nwt, rows // kt), (ts, tw, kt),
                    [pl.BlockSpec((kt, ts), lambda i, j, kk: (kk, off + half(j))), pl.BlockSpec(b_blk, b_map)],
                    pl.BlockSpec((ts, tw), lambda i, j, kk: (0, j)), (ts, wide), b3d=True)


def _seg_map(nbc):
    return lambda i: (jnp.where(i < nbc, 0, 1), 0, 0)


def _rstd(v):
    return lax.rsqrt(jnp.mean(v * v, axis=-1, keepdims=True) + EPS)


def _norm_mod_fwd(x, g, mods, sh, sc, n_ctx, name):
    n, d = x.shape
    tm = _row_tile(n_ctx, n)
    nbc = n_ctx // tm

    def body(x_ref, g_ref, m_ref, h_ref):
        xv = x_ref[...]
        hn = xv * _rstd(xv) * g_ref[...]
        h_ref[...] = (hn * (1.0 + m_ref[0, sc:sc + 1, :]) + m_ref[0, sh:sh + 1, :]).astype(h_ref.dtype)

    row = pl.BlockSpec((tm, d), lambda i: (i, 0))
    return pl.pallas_call(
        body, grid=(n // tm,),
        in_specs=[row, pl.BlockSpec((1, d), lambda i: (0, 0)), pl.BlockSpec((1, 8, d), _seg_map(nbc))],
        out_specs=row, out_shape=SDS((n, d), MXU_DTYPE), compiler_params=_cparams("parallel"), name=name)(x, g, mods)


def _gate_res_fwd(x, f, g, mods, gi, n_ctx, name):
    n, d = x.shape
    tm = _row_tile(n_ctx, n)
    nbc = n_ctx // tm

    def body(x_ref, f_ref, g_ref, m_ref, o_ref):
        fv = f_ref[...]
        o_ref[...] = x_ref[...] + m_ref[0, gi:gi + 1, :] * (fv * _rstd(fv) * g_ref[...])

    row = pl.BlockSpec((tm, d), lambda i: (i, 0))
    return pl.pallas_call(
        body, grid=(n // tm,),
        in_specs=[row, row, pl.BlockSpec((1, d), lambda i: (0, 0)), pl.BlockSpec((1, 8, d), _seg_map(nbc))],
        out_specs=row, out_shape=SDS((n, d), F32), compiler_params=_cparams("parallel"), name=name)(x, f, g, mods)


def _gate_res_bwd(dx, f, g, mods, gi, n_ctx, name):
    n, d = dx.shape
    tm = _row_tile(n_ctx, n)
    nbc = n_ctx // tm

    def body(dx_ref, f_ref, g_ref, m_ref, df_ref, dgate_ref, dg_ref):
        i = pl.program_id(0)

        @pl.when(i == 0)
        def _():
            dg_ref[...] = jnp.zeros(dg_ref.shape, F32)

        @pl.when(jnp.logical_or(i == 0, i == nbc))
        def _():
            dgate_ref[...] = jnp.zeros(dgate_ref.shape, F32)

        dxv, fv, gv = dx_ref[...], f_ref[...], g_ref[...]
        rs = _rstd(fv)
        nv = fv * rs
        dgate_ref[0] += jnp.sum(dxv * (nv * gv), axis=0, keepdims=True)
        dout = dxv * m_ref[0, gi:gi + 1, :]
        dg_ref[...] += jnp.sum(dout * nv, axis=0, keepdims=True)
        dn = dout * gv
        df_ref[...] = (rs * (dn - nv * jnp.mean(dn * nv, axis=-1, keepdims=True))).astype(df_ref.dtype)

    row = pl.BlockSpec((tm, d), lambda i: (i, 0))
    vec = pl.BlockSpec((1, d), lambda i: (0, 0))
    return pl.pallas_call(
        body, grid=(n // tm,),
        in_specs=[row, row, vec, pl.BlockSpec((1, 8, d), _seg_map(nbc))],
        out_specs=[row, pl.BlockSpec((1, 1, d), _seg_map(nbc)), vec],
        out_shape=[SDS((n, d), MXU_DTYPE), SDS((2, 1, d), F32), SDS((1, d), F32)],
        compiler_params=_cparams("arbitrary"), name=name)(dx, f, g, mods)


def _norm_mod_bwd(dh, x, g, mods, sh, sc, dx_res, n_ctx, name):
    n, d = x.shape
    tm = _row_tile(n_ctx, n)
    nbc = n_ctx // tm

    def body(dh_ref, x_ref, g_ref, m_ref, r_ref, dx_ref, dss_ref, dg_ref):
        i = pl.program_id(0)

        @pl.when(i == 0)
        def _():
            dg_ref[...] = jnp.zeros(dg_ref.shape, F32)

        @pl.when(jnp.logical_or(i == 0, i == nbc))
        def _():
            dss_ref[...] = jnp.zeros(dss_ref.shape, F32)

        dhv, xv, gv = dh_ref[...], x_ref[...], g_ref[...]
        rs = _rstd(xv)
        nv = xv * rs
        dss_ref[0, 0:1, :] += jnp.sum(dhv, axis=0, keepdims=True)
        dss_ref[0, 1:2, :] += jnp.sum(dhv * (nv * gv), axis=0, keepdims=True)
        dhn = dhv * (1.0 + m_ref[0, sc:sc + 1, :])
        dg_ref[...] += jnp.sum(dhn * nv, axis=0, keepdims=True)
        dn = dhn * gv
        dx_ref[...] = r_ref[...] + rs * (dn - nv * jnp.mean(dn * nv, axis=-1, keepdims=True))

    row = pl.BlockSpec((tm, d), lambda i: (i, 0))
    vec = pl.BlockSpec((1, d), lambda i: (0, 0))
    return pl.pallas_call(
        body, grid=(n // tm,),
        in_specs=[row, row, vec, pl.BlockSpec((1, 8, d), _seg_map(nbc)), row],
        out_specs=[row, pl.BlockSpec((1, 2, d), _seg_map(nbc)), vec],
        out_shape=[SDS((n, d), F32), SDS((2, 2, d), F32), SDS((1, d), F32)],
        compiler_params=_cparams("arbitrary"), name=name)(dh, x, g, mods, dx_res)


def _loss_grad(xc, target, n_ctx, name):
    n, d = xc.shape
    tm = _row_tile(n_ctx, n)
    nbc = n_ctx // tm
    nb = n // tm

    def body(x_ref, t_ref, dx_ref, l_ref, acc_ref):
        i = pl.program_id(0)

        @pl.when(i == 0)
        def _():
            acc_ref[...] = jnp.zeros(acc_ref.shape, F32)

        @pl.when(i < nbc)
        def _():
            dx_ref[...] = jnp.zeros(dx_ref.shape, F32)

        @pl.when(i >= nbc)
        def _():
            diff = x_ref[...] - t_ref[...]
            dx_ref[...] = diff * (1.0 / d)
            acc_ref[...] += jnp.sum(diff * diff, axis=0, keepdims=True)

        @pl.when(i == nb - 1)
        def _():
            l_ref[...] = jnp.full(l_ref.shape, (0.5 / d) * jnp.sum(acc_ref[...]), F32)

    row = pl.BlockSpec((tm, d), lambda i: (i, 0))
    return pl.pallas_call(
        body, grid=(nb,),
        in_specs=[row, pl.BlockSpec((tm, d), lambda i: (jnp.maximum(i - nbc, 0), 0))],
        out_specs=[row, pl.BlockSpec((SUBLANES, LANES), lambda i: (0, 0))],
        out_shape=[SDS((n, d), F32), SDS((SUBLANES, LANES), F32)],
        scratch_shapes=[pltpu.VMEM((1, d), F32)],
        compiler_params=_cparams("arbitrary"), name=name)(xc, target)


POOL_PAD = 16


def _pool(src, pool_width, pool_group, n_ctx, bwd, out_dtype, name):
    n = src.shape[0]
    n_lat = n - n_ctx
    gb = pool_group // LANES
    segs = ((0, n_ctx, POOL_PAD), (n_ctx, n_lat, 2 * POOL_PAD + n_ctx))
    total = 3 * POOL_PAD + n

    def body(s_ref, o_ref, scr):
        j = pl.program_id(0)
        for base in (0, POOL_PAD + n_ctx, 2 * POOL_PAD + n):
            scr[pl.ds(base, POOL_PAD), :] = jnp.zeros((POOL_PAD, LANES), F32)
        for gi, w in enumerate(POOL_WINDOWS):
            @pl.when(jnp.logical_and(j >= gi * gb, j < (gi + 1) * gb))
            def _(w=w):
                half = w // 2
                offs = range(-half + 1, half + 1) if bwd else range(-half, half)
                for row0, nseg, base in segs:
                    ch = math.gcd(nseg, 256)

                    def count(c0):
                        t = c0 + lax.broadcasted_iota(jnp.int32, (ch, LANES), 0)
                        return (jnp.minimum(t + half, nseg) - jnp.maximum(t - half, 0)).astype(F32)

                    def fill(ci, carry):
                        c0 = pl.multiple_of(ci * ch, ch)
                        v = s_ref[pl.ds(row0 + c0, ch), :]
                        scr[pl.ds(base + c0, ch), :] = v / count(c0) if bwd else v
                        return carry

                    def window(ci, carry):
                        c0 = pl.multiple_of(ci * ch, ch)
                        acc = jnp.zeros((ch, LANES), F32)
                        for off in offs:
                            acc = acc + scr[pl.ds(c0 + (base + off), ch), :]
                        v = s_ref[pl.ds(row0 + c0, ch), :]
                        res = acc - v if bwd else acc / count(c0) - v
                        o_ref[pl.ds(row0 + c0, ch), :] = res.astype(o_ref.dtype)
                        return carry

                    lax.fori_loop(0, nseg // ch, fill, 0)
                    lax.fori_loop(0, nseg // ch, window, 0)

    blk = pl.BlockSpec((n, LANES), lambda j: (0, j))
    return pl.pallas_call(
        body, grid=(pool_width // LANES,), in_specs=[blk], out_specs=blk,
        out_shape=SDS((n, pool_width), out_dtype), scratch_shapes=[pltpu.VMEM((total, LANES), F32)],
        compiler_params=_cparams("parallel"), name=name)(src)


def _pool_proj_fwd(p, wp, l, scale, name):
    n, pw = p.shape
    ng, c = wp.shape[1], wp.shape[2]
    tm = _pick(n, _TM)

    def body(p_ref, w_ref, s_ref, o_ref):
        y = jnp.dot(p_ref[...], w_ref[...].astype(MXU_DTYPE), preferred_element_type=F32)
        o_ref[...] = (y * s_ref[...]).astype(o_ref.dtype)

    return pl.pallas_call(
        body, grid=(ng, n // tm),
        in_specs=[pl.BlockSpec((tm, c), lambda g, i: (i, g)), pl.BlockSpec((None, None, c, c), lambda g, i: (l, g, 0, 0)),
                  pl.BlockSpec((1, c), lambda g, i: (0, g))],
        out_specs=pl.BlockSpec((tm, c), lambda g, i: (i, g)), out_shape=SDS((n, pw), MXU_DTYPE),
        compiler_params=_cparams("parallel", "parallel"), name=name)(p, wp, scale)


def _pool_proj_bwd(p, dcat, wp, l, scale, name):
    n, pw = p.shape
    ng, c = wp.shape[1], wp.shape[2]
    tm = _pick(n, _TM)

    def body(p_ref, dy_ref, w_ref, s_ref, dp_ref, ds_ref, dw_ref):
        i = pl.program_id(1)

        @pl.when(i == 0)
        def _():
            ds_ref[...] = jnp.zeros(ds_ref.shape, F32)
            dw_ref[...] = jnp.zeros(dw_ref.shape, F32)

        pv, wv, dy = p_ref[...], w_ref[...].astype(MXU_DTYPE), dy_ref[...]
        y = jnp.dot(pv, wv, preferred_element_type=F32)
        ds_ref[...] += jnp.sum(dy * y, axis=0, keepdims=True)
        dpw = (dy * s_ref[...]).astype(MXU_DTYPE)
        dp_ref[...] = lax.dot_general(dpw, wv, _DIMS["nt"], preferred_element_type=F32)
        dw_ref[0] += lax.dot_general(pv, dpw, _DIMS["tn"], preferred_element_type=F32)

    return pl.pallas_call(
        body, grid=(ng, n // tm),
        in_specs=[pl.BlockSpec((tm, c), lambda g, i: (i, g)), pl.BlockSpec((tm, c), lambda g, i: (i, g)),
                  pl.BlockSpec((None, None, c, c), lambda g, i: (l, g, 0, 0)), pl.BlockSpec((1, c), lambda g, i: (0, g))],
        out_specs=[pl.BlockSpec((tm, c), lambda g, i: (i, g)), pl.BlockSpec((1, c), lambda g, i: (0, g)),
                   pl.BlockSpec((1, c, c), lambda g, i: (g, 0, 0))],
        out_shape=[SDS((n, pw), F32), SDS((1, pw), F32), SDS((ng, c, c), F32)],
        compiler_params=_cparams("arbitrary", "arbitrary"), name=name)(p, dcat, wp, scale)


def _disc_math(a_re, a_im, logdt, b_re, b_im):
    dt = jnp.exp(logdt)
    mag = jnp.exp(a_re * dt)
    lam_re = mag * jnp.cos(a_im * dt)
    lam_im = mag * jnp.sin(a_im * dt)
    denom = a_re * a_re + a_im * a_im
    nr, ni = lam_re - 1.0, lam_im
    f_re = ((nr * a_re + ni * a_im) / denom)[:, None, :]
    f_im = ((ni * a_re - nr * a_im) / denom)[:, None, :]
    return lam_re, lam_im, f_re * b_re - f_im * b_im, f_re * b_im + f_im * b_re


def _disc_fwd(a_re, a_im, logdt, b_re, b_im, name):
    def body(ar, ai, ld, br, bi, o_lr, o_li, o_br, o_bi):
        lr, li, bbr, bbi = _disc_math(ar[...], ai[...], ld[...], br[...], bi[...])
        o_lr[...] = lr
        o_li[...] = li
        o_br[...] = bbr
        o_bi[...] = bbi

    return pl.pallas_call(
        body, out_shape=[SDS(a_re.shape, F32), SDS(a_re.shape, F32), SDS(b_re.shape, F32), SDS(b_re.shape, F32)],
        compiler_params=_cparams(), name=name)(a_re, a_im, logdt, b_re, b_im)


def _disc_bwd(a_re, a_im, logdt, b_re, b_im, d_lr, d_li, d_bbr, d_bbi, group, name):
    rows, gp = a_re.shape

    def body(ar, ai, ld, br, bi, g_lr, g_li, g_br, g_bi, o_ar, o_ai, o_ld, o_br, o_bi):
        _, vjp = jax.vjp(_disc_math, ar[...], ai[...], ld[...], br[...], bi[...])
        dar, dai, dld, dbr, dbi = vjp((g_lr[...], g_li[...], g_br[...], g_bi[...]))
        o_ar[...] = dar
        o_ai[...] = dai
        state = lax.broadcasted_iota(jnp.int32, (gp, LANES), 0)
        first = lax.broadcasted_iota(jnp.int32, (gp, LANES), 1) * group
        sel = jnp.logical_and(state >= first, state < first + group).astype(F32)
        o_ld[...] = jnp.dot(dld, sel, precision=HIGHEST, preferred_element_type=F32)
        o_br[...] = dbr
        o_bi[...] = dbi

    return pl.pallas_call(
        body, out_shape=[SDS(a_re.shape, F32), SDS(a_re.shape, F32), SDS((rows, LANES), F32),
                         SDS(b_re.shape, F32), SDS(b_re.shape, F32)],
        compiler_params=_cparams(), name=name)(a_re, a_im, logdt, b_re, b_im, d_lr, d_li, d_bbr, d_bbi)


def _scan_maps(nbc, nb):
    nbl = nb - nbc
    fwd0 = lambda i: (i, 0, 0)
    fwd1 = lambda i: (jnp.where(i < nbc, nbc - 1 - i, nb - 1 - (i - nbc)), 0, 0)
    adj0 = lambda i: (nb - 1 - i, 0, 0)
    adj1 = lambda i: (jnp.where(i < nbl, nbc + i, i - nbl), 0, 0)
    return fwd0, fwd1, adj0, adj1


def _scan_fwd(bu0, bu1, lam, n_ctx, name):
    n, s2, _ = bu0.shape
    s = s2 // 2
    tt = math.gcd(math.gcd(n_ctx, n - n_ctx), 128)
    nbc, nb = n_ctx // tt, n // tt
    fwd0, fwd1, _, _ = _scan_maps(nbc, nb)

    def body(b0_ref, b1_ref, lam_ref, h0_ref, h1_ref, st_ref):
        @pl.when(pl.program_id(0) == 0)
        def _():
            st_ref[...] = jnp.zeros(st_ref.shape, F32)

        lr0, li0, lr1, li1 = lam_ref[0], lam_ref[1], lam_ref[2], lam_ref[3]

        def step(j, carry):
            h0r, h0i, h1r, h1i = carry
            t1 = tt - 1 - j
            n0r = lr0 * h0r - li0 * h0i + b0_ref[j, 0:s, :]
            n0i = lr0 * h0i + li0 * h0r + b0_ref[j, s:s2, :]
            n1r = lr1 * h1r - li1 * h1i + b1_ref[t1, 0:s, :]
            n1i = lr1 * h1i + li1 * h1r + b1_ref[t1, s:s2, :]
            h0_ref[j, 0:s, :] = n0r
            h0_ref[j, s:s2, :] = n0i
            h1_ref[t1, 0:s, :] = n1r
            h1_ref[t1, s:s2, :] = n1i
            return n0r, n0i, n1r, n1i

        out = lax.fori_loop(0, tt, step, (st_ref[0], st_ref[1], st_ref[2], st_ref[3]), unroll=2)
        for q in range(4):
            st_ref[q] = out[q]

    blk = (tt, s2, LANES)
    return pl.pallas_call(
        body, grid=(nb,),
        in_specs=[pl.BlockSpec(blk, fwd0), pl.BlockSpec(blk, fwd1), pl.BlockSpec((4, s, LANES), lambda i: (0, 0, 0))],
        out_specs=[pl.BlockSpec(blk, fwd0), pl.BlockSpec(blk, fwd1)],
        out_shape=[SDS(bu0.shape, F32), SDS(bu1.shape, F32)],
        scratch_shapes=[pltpu.VMEM((4, s, LANES), F32)],
        compiler_params=_cparams("arbitrary"), name=name)(bu0, bu1, lam)


def _scan_bwd(dh0, dh1, h0, h1, lam, n_ctx, name):
    n, s2, _ = dh0.shape
    s = s2 // 2
    tt = math.gcd(math.gcd(n_ctx, n - n_ctx), 128)
    nbc, nb = n_ctx // tt, n // tt
    _, _, adj0, adj1 = _scan_maps(nbc, nb)

    def body(d0_ref, d1_ref, h0_ref, h1_ref, lam_ref, a0_ref, a1_ref, dl_ref, st_ref, acc_ref):
        i = pl.program_id(0)

        @pl.when(i == 0)
        def _():
            st_ref[...] = jnp.zeros(st_ref.shape, F32)
            acc_ref[...] = jnp.zeros(acc_ref.shape, F32)

        lr0, li0, lr1, li1 = lam_ref[0], lam_ref[1], lam_ref[2], lam_ref[3]

        def step(j, carry):
            a0r, a0i, a1r, a1i, c0r, c0i, c1r, c1i = carry
            t0 = tt - 1 - j
            g0r, g0i = h0_ref[t0, 0:s, :], h0_ref[t0, s:s2, :]
            g1r, g1i = h1_ref[j, 0:s, :], h1_ref[j, s:s2, :]
            c0r = c0r + a0r * g0r + a0i * g0i
            c0i = c0i + a0i * g0r - a0r * g0i
            c1r = c1r + a1r * g1r + a1i * g1i
            c1i = c1i + a1i * g1r - a1r * g1i
            n0r = lr0 * a0r + li0 * a0i + d0_ref[t0, 0:s, :]
            n0i = lr0 * a0i - li0 * a0r + d0_ref[t0, s:s2, :]
            n1r = lr1 * a1r + li1 * a1i + d1_ref[j, 0:s, :]
            n1i = lr1 * a1i - li1 * a1r + d1_ref[j, s:s2, :]
            a0_ref[t0, 0:s, :] = n0r
            a0_ref[t0, s:s2, :] = n0i
            a1_ref[j, 0:s, :] = n1r
            a1_ref[j, s:s2, :] = n1i
            return n0r, n0i, n1r, n1i, c0r, c0i, c1r, c1i

        init = tuple(st_ref[q] for q in range(4)) + tuple(acc_ref[q] for q in range(4))
        out = lax.fori_loop(0, tt, step, init, unroll=2)
        for q in range(4):
            st_ref[q] = out[q]
            acc_ref[q] = out[4 + q]

        @pl.when(i == nb - 1)
        def _():
            for q in range(4):
                dl_ref[q] = out[4 + q]

    blk = (tt, s2, LANES)
    small = pl.BlockSpec((4, s, LANES), lambda i: (0, 0, 0))
    return pl.pallas_call(
        body, grid=(nb,),
        in_specs=[pl.BlockSpec(blk, adj0), pl.BlockSpec(blk, adj1), pl.BlockSpec(blk, adj0), pl.BlockSpec(blk, adj1), small],
        out_specs=[pl.BlockSpec(blk, adj0), pl.BlockSpec(blk, adj1), small],
        out_shape=[SDS(dh0.shape, F32), SDS(dh1.shape, F32), SDS((4, s, LANES), F32)],
        scratch_shapes=[pltpu.VMEM((4, s, LANES), F32), pltpu.VMEM((4, s, LANES), F32)],
        compiler_params=_cparams("arbitrary"), name=name)(dh0, dh1, h0, h1, lam)


def _gelu(v):
    th = jnp.tanh(GELU_C0 * (v + GELU_C1 * v * v * v))
    return 0.5 * v * (1.0 + th), th


def _ssm_head_fwd(y, u, ssm_d, wg, l, name):
    n, sw = y.shape
    ucol = u.shape[1] // sw - 1
    tm = _pick(n, _TM)

    def body(y_ref, u_ref, d_ref, w_ref, o_ref):
        act, _ = _gelu(y_ref[...] + d_ref[...] * u_ref[...])
        q = jnp.dot(act.astype(MXU_DTYPE), w_ref[...].astype(MXU_DTYPE), preferred_element_type=F32)
        o_ref[...] = (act * jax.nn.sigmoid(q)).astype(o_ref.dtype)

    row = pl.BlockSpec((tm, sw), lambda i: (i, 0))
    return pl.pallas_call(
        body, grid=(n // tm,),
        in_specs=[row, pl.BlockSpec((tm, sw), lambda i: (i, ucol)), pl.BlockSpec((1, sw), lambda i: (0, 0)),
                  pl.BlockSpec((None, sw, sw), lambda i: (l, 0, 0))],
        out_specs=row, out_shape=SDS((n, sw), MXU_DTYPE), compiler_params=_cparams("parallel"), name=name)(y, u, ssm_d, wg)


def _ssm_head_bwd(dcat, y, u, ssm_d, wg, l, name):
    n, sw = y.shape
    ucol = u.shape[1] // sw - 1
    tm = _pick(n, _TM)

    def body(do_ref, y_ref, u_ref, d_ref, w_ref, dy_ref, du_ref, act_ref, dq_ref, dd_ref):
        @pl.when(pl.program_id(0) == 0)
        def _():
            dd_ref[...] = jnp.zeros(dd_ref.shape, F32)

        uv, dv, do = u_ref[...], d_ref[...], do_ref[...]
        yf = y_ref[...] + dv * uv
        act, th = _gelu(yf)
        wv = w_ref[...].astype(MXU_DTYPE)
        sg = jax.nn.sigmoid(jnp.dot(act.astype(MXU_DTYPE), wv, preferred_element_type=F32))
        dq = (do * act * sg * (1.0 - sg)).astype(MXU_DTYPE)
        dact = do * sg + lax.dot_general(dq, wv, _DIMS["nt"], preferred_element_type=F32)
        dgelu = 0.5 * (1.0 + th) + 0.5 * yf * (1.0 - th * th) * GELU_C0 * (1.0 + 3.0 * GELU_C1 * yf * yf)
        dyf = dact * dgelu
        dy_ref[...] = dyf.astype(dy_ref.dtype)
        du_ref[...] = dyf * dv
        act_ref[...] = act.astype(act_ref.dtype)
        dq_ref[...] = dq
        dd_ref[...] += jnp.sum(dyf * uv, axis=0, keepdims=True)

    row = pl.BlockSpec((tm, sw), lambda i: (i, 0))
    last = pl.BlockSpec((tm, sw), lambda i: (i, ucol))
    vec = pl.BlockSpec((1, sw), lambda i: (0, 0))
    return pl.pallas_call(
        body, grid=(n // tm,),
        in_specs=[last, row, last, vec, pl.BlockSpec((None, sw, sw), lambda i: (l, 0, 0))],
        out_specs=[row, row, row, row, vec],
        out_shape=[SDS((n, sw), MXU_DTYPE), SDS((n, sw), F32), SDS((n, sw), MXU_DTYPE), SDS((n, sw), MXU_DTYPE),
                   SDS((1, sw), F32)],
        compiler_params=_cparams("arbitrary"), name=name)(dcat, y, u, ssm_d, wg)


def _assemble_du(du_pool, du_dir, du_proj, name):
    n, pw = du_pool.shape
    sw = du_dir.shape[1]
    tm = _pick(n, _TM)

    def body(p_ref, a_ref, b_ref, o_ref):
        o_ref[:, 0:pw] = p_ref[...].astype(o_ref.dtype)
        o_ref[:, pw:pw + sw] = (a_ref[...] + b_ref[...]).astype(o_ref.dtype)

    return pl.pallas_call(
        body, grid=(n // tm,),
        in_specs=[pl.BlockSpec((tm, pw), lambda i: (i, 0)), pl.BlockSpec((tm, sw), lambda i: (i, 0)),
                  pl.BlockSpec((tm, sw), lambda i: (i, 0))],
        out_specs=pl.BlockSpec((tm, pw + sw), lambda i: (i, 0)), out_shape=SDS((n, pw + sw), MXU_DTYPE),
        compiler_params=_cparams("parallel"), name=name)(du_pool, du_dir, du_proj)


CONV_PAD = GRID_W + SUBLANES


def _conv_layout(n, n_ctx):
    return CONV_PAD, 2 * CONV_PAD + n_ctx, 3 * CONV_PAD + n


def _col_masks(ch):
    col = lax.broadcasted_iota(jnp.int32, (ch, LANES), 0) % GRID_W
    return col != 0, col != GRID_W - 1


def _fill_padded(scr, src_ref, n, n_ctx):
    base_c, base_l, total = _conv_layout(n, n_ctx)
    for base in (0, base_c + n_ctx, base_l + n - n_ctx):
        scr[pl.ds(base, CONV_PAD), :] = jnp.zeros((CONV_PAD, LANES), F32)
    for row0, nseg, base in ((0, n_ctx, base_c), (n_ctx, n - n_ctx, base_l)):
        ch = math.gcd(nseg, 512)

        def copy(ci, carry, row0=row0, base=base, ch=ch):
            c0 = pl.multiple_of(ci * ch, ch)
            scr[pl.ds(base + c0, ch), :] = src_ref[pl.ds(row0 + c0, ch), :]
            return carry

        lax.fori_loop(0, nseg // ch, copy, 0)


def _conv_ctx(scr, k_ref, base, c0, ch, sign):
    acc = scr[pl.ds(c0 + base, ch), :] * k_ref[4:5, :]
    acc = acc + scr[pl.ds(c0 + (base - sign), ch), :] * k_ref[3:4, :]
    return acc + scr[pl.ds(c0 + (base + sign), ch), :] * k_ref[5:6, :]


def _conv_lat(scr, k_ref, base, c0, ch, sign, m_l, m_r):
    cols = []
    for j in range(3):
        acc = None
        for i in range(3):
            off = sign * (GRID_W * (i - 1) + (j - 1))
            term = scr[pl.ds(c0 + (base + off), ch), :] * k_ref[3 * i + j:3 * i + j + 1, :]
            acc = term if acc is None else acc + term
        cols.append(acc)
    first, last = (m_l, m_r) if sign > 0 else (m_r, m_l)
    return cols[1] + jnp.where(first, cols[0], 0.0) + jnp.where(last, cols[2], 0.0)


def _conv_chunk(n_lat):
    return math.gcd(n_lat, 256)


def _conv_glu_fwd(z, wk, n_ctx, name):
    n, f2 = z.shape
    dff = f2 // 2
    nvt = dff // LANES
    n_lat = n - n_ctx
    base_c, base_l, total = _conv_layout(n, n_ctx)
    ch = _conv_chunk(n_lat)
    assert ch % GRID_W == 0

    def body(zv_ref, zg_ref, kv_ref, kg_ref, a_ref, cv_ref, cg_ref, sv, sg):
        _fill_padded(sv, zv_ref, n, n_ctx)
        _fill_padded(sg, zg_ref, n, n_ctx)

        def emit(cv, cg, row, rows):
            cv_ref[pl.ds(row, rows), :] = cv
            cg_ref[pl.ds(row, rows), :] = cg
            a_ref[pl.ds(row, rows), :] = (cv * cg * jax.nn.sigmoid(cg)).astype(a_ref.dtype)

        emit(_conv_ctx(sv, kv_ref, base_c, 0, n_ctx, 1), _conv_ctx(sg, kg_ref, base_c, 0, n_ctx, 1), 0, n_ctx)
        m_l, m_r = _col_masks(ch)

        def lat(ci, carry):
            c0 = pl.multiple_of(ci * ch, ch)
            emit(_conv_lat(sv, kv_ref, base_l, c0, ch, 1, m_l, m_r), _conv_lat(sg, kg_ref, base_l, c0, ch, 1, m_l, m_r),
                 n_ctx + c0, ch)
            return carry

        lax.fori_loop(0, n_lat // ch, lat, 0)

    col = lambda shift: pl.BlockSpec((n, LANES), lambda j: (0, j + shift))
    kcol = lambda shift: pl.BlockSpec((9, LANES), lambda j: (0, j + shift))
    return pl.pallas_call(
        body, grid=(nvt,), in_specs=[col(0), col(nvt), kcol(0), kcol(nvt)], out_specs=[col(0), col(0), col(0)],
        out_shape=[SDS((n, dff), MXU_DTYPE), SDS((n, dff), F32), SDS((n, dff), F32)],
        scratch_shapes=[pltpu.VMEM((total, LANES), F32), pltpu.VMEM((total, LANES), F32)],
        compiler_params=_cparams("parallel"), name=name)(z, z, wk, wk)


def _conv_glu_bwd(z, cv, cg, da, wk, n_ctx, name):
    n, f2 = z.shape
    dff = f2 // 2
    nvt = dff // LANES
    n_lat = n - n_ctx
    base_c, base_l, total = _conv_layout(n, n_ctx)
    ch = _conv_chunk(n_lat)
    assert ch % GRID_W == 0
    ctx_taps = [(1, 0), (1, 1), (1, 2)]
    lat_taps = [(i, j) for i in range(3) for j in range(3)]

    def tap_sums(acc, scr, d, base, c0, rows, taps, masks):
        acc = list(acc)
        for i, j in taps:
            src = scr[pl.ds(c0 + (base + GRID_W * (i - 1) + (j - 1)), rows), :]
            if masks is not None and j != 1:
                src = jnp.where(masks[0] if j == 0 else masks[1], src, 0.0)
            acc[3 * i + j] = acc[3 * i + j] + jnp.sum((src * d).reshape(rows // SUBLANES, SUBLANES, LANES), axis=0)
        return acc

    def body(zv_ref, zg_ref, cv_ref, cg_ref, da_ref, kv_ref, kg_ref, dz_ref, dkv_ref, dkg_ref, a_ref, sv, sg, dv, dg):
        _fill_padded(sv, zv_ref, n, n_ctx)
        _fill_padded(sg, zg_ref, n, n_ctx)
        for base in (0, base_c + n_ctx, base_l + n_lat):
            dv[pl.ds(base, CONV_PAD), :] = jnp.zeros((CONV_PAD, LANES), F32)
            dg[pl.ds(base, CONV_PAD), :] = jnp.zeros((CONV_PAD, LANES), F32)
        m_l, m_r = _col_masks(ch)

        def first_pass(row, pad_row, rows):
            cv, cg = cv_ref[pl.ds(row, rows), :], cg_ref[pl.ds(row, rows), :]
            sig = jax.nn.sigmoid(cg)
            silu = cg * sig
            a_ref[pl.ds(row, rows), :] = (cv * silu).astype(a_ref.dtype)
            dav = da_ref[pl.ds(row, rows), :]
            dcv = dav * silu
            dcg = dav * cv * (sig * (1.0 + cg * (1.0 - sig)))
            dv[pl.ds(pad_row, rows), :] = dcv
            dg[pl.ds(pad_row, rows), :] = dcg
            return dcv, dcg

        zero = [jnp.zeros((SUBLANES, LANES), F32) for _ in range(9)]
        dcv, dcg = first_pass(0, base_c, n_ctx)
        accv = tap_sums(zero, sv, dcv, base_c, 0, n_ctx, ctx_taps, None)
        accg = tap_sums(zero, sg, dcg, base_c, 0, n_ctx, ctx_taps, None)

        def lat1(ci, carry):
            accv, accg = carry
            c0 = pl.multiple_of(ci * ch, ch)
            dcv, dcg = first_pass(n_ctx + c0, base_l + c0, ch)
            accv = tap_sums(accv, sv, dcv, base_l, c0, ch, lat_taps, (m_l, m_r))
            accg = tap_sums(accg, sg, dcg, base_l, c0, ch, lat_taps, (m_l, m_r))
            return tuple(accv), tuple(accg)

        accv, accg = lax.fori_loop(0, n_lat // ch, lat1, (tuple(accv), tuple(accg)))
        for t in range(9):
            dkv_ref[t:t + 1, :] = jnp.sum(accv[t], axis=0, keepdims=True)
            dkg_ref[t:t + 1, :] = jnp.sum(accg[t], axis=0, keepdims=True)

        dz_ref[0, pl.ds(0, n_ctx), :] = _conv_ctx(dv, kv_ref, base_c, 0, n_ctx, -1).astype(dz_ref.dtype)
        dz_ref[1, pl.ds(0, n_ctx), :] = _conv_ctx(dg, kg_ref, base_c, 0, n_ctx, -1).astype(dz_ref.dtype)

        def lat2(ci, carry):
            c0 = pl.multiple_of(ci * ch, ch)
            dz_ref[0, pl.ds(n_ctx + c0, ch), :] = _conv_lat(dv, kv_ref, base_l, c0, ch, -1, m_l, m_r).astype(dz_ref.dtype)
            dz_ref[1, pl.ds(n_ctx + c0, ch), :] = _conv_lat(dg, kg_ref, base_l, c0, ch, -1, m_l, m_r).astype(dz_ref.dtype)
            return carry

        lax.fori_loop(0, n_lat // ch, lat2, 0)

    col = lambda shift: pl.BlockSpec((n, LANES), lambda j: (0, j + shift))
    kcol = lambda shift: pl.BlockSpec((9, LANES), lambda j: (0, j + shift))
    pad = pltpu.VMEM((total, LANES), F32)
    return pl.pallas_call(
        body, grid=(nvt,), in_specs=[col(0), col(nvt), col(0), col(0), col(0), kcol(0), kcol(nvt)],
        out_specs=[pl.BlockSpec((2, n, LANES), lambda j: (0, 0, j)), kcol(0), kcol(0), col(0)],
        out_shape=[SDS((2, n, dff), MXU_DTYPE), SDS((9, dff), F32), SDS((9, dff), F32), SDS((n, dff), MXU_DTYPE)],
        scratch_shapes=[pad, pad, pad, pad],
        compiler_params=_cparams("parallel"), name=name)(z, z, cv, cg, da, wk, wk)


def _silu(v):
    return v * jax.nn.sigmoid(v)


def _ada_fwd(cond, w_ada, b_shard, name):
    nl, d, cols = w_ada.shape
    tn = _pick(cols, (512, 256, 128))

    def body(c_ref, w_ref, b_ref, o_ref):
        o_ref[...] = jnp.dot(_silu(c_ref[...]), w_ref[...], precision=HIGHEST, preferred_element_type=F32) + b_ref[...]

    return pl.pallas_call(
        body, grid=(nl, cols // tn),
        in_specs=[pl.BlockSpec(cond.shape, lambda l, j: (0, 0)), pl.BlockSpec((None, d, tn), lambda l, j: (l, 0, j)),
                  pl.BlockSpec((None, 1, tn), lambda l, j: (l, 0, j))],
        out_specs=pl.BlockSpec((None, cond.shape[0], tn), lambda l, j: (l, 0, j)),
        out_shape=SDS((nl, cond.shape[0], cols), F32),
        compiler_params=_cparams("parallel", "parallel"), name=name)(cond, w_ada, b_shard)


def _ada_dw(cond, dmod, name):
    nl, rows, cols = dmod.shape
    d = cond.shape[1]
    tn = _pick(cols, (512, 256, 128))

    def body(c_ref, g_ref, o_ref):
        o_ref[...] = lax.dot_general(_silu(c_ref[...]), g_ref[...], _DIMS["tn"], precision=HIGHEST,
                                     preferred_element_type=F32)

    return pl.pallas_call(
        body, grid=(nl, cols // tn),
        in_specs=[pl.BlockSpec(cond.shape, lambda l, j: (0, 0)), pl.BlockSpec((None, rows, tn), lambda l, j: (l, 0, j))],
        out_specs=pl.BlockSpec((None, d, tn), lambda l, j: (l, 0, j)), out_shape=SDS((nl, d, cols), F32),
        compiler_params=_cparams("parallel", "parallel"), name=name)(cond, dmod)


def _ada_dcond(dmod, w_ada, name):
    nl, rows, cols = dmod.shape
    d = w_ada.shape[1]
    tn = _pick(cols, (512, 256, 128))

    def body(g_ref, w_ref, o_ref):
        @pl.when(jnp.logical_and(pl.program_id(0) == 0, pl.program_id(1) == 0))
        def _():
            o_ref[...] = jnp.zeros(o_ref.shape, F32)

        o_ref[...] += lax.dot_general(g_ref[...], w_ref[...], _DIMS["nt"], precision=HIGHEST, preferred_element_type=F32)

    return pl.pallas_call(
        body, grid=(nl, cols // tn),
        in_specs=[pl.BlockSpec((None, rows, tn), lambda l, j: (l, 0, j)), pl.BlockSpec((None, d, tn), lambda l, j: (l, 0, j))],
        out_specs=pl.BlockSpec((rows, d), lambda l, j: (0, 0)), out_shape=SDS((rows, d), F32),
        compiler_params=_cparams("arbitrary", "arbitrary"), name=name)(dmod, w_ada)


def _ada_rows(dmod_all, name):
    nd, nl, _, w = dmod_all.shape
    tn = _pick(w, (2048, 1024, 512, 256, 128))

    def body(g_ref, rows_ref, db_ref):
        ctx = g_ref[0, 0, 0:1, :]
        for b in range(1, nd):
            ctx = ctx + g_ref[b, 0, 0:1, :]
        total = ctx
        for b in range(nd):
            lat = g_ref[b, 0, 1:2, :]
            rows_ref[b:b + 1, :] = lat
            total = total + lat
        rows_ref[nd:nd + 1, :] = ctx
        rows_ref[nd + 1:16, :] = jnp.zeros((16 - nd - 1, tn), F32)
        db_ref[...] = total

    return pl.pallas_call(
        body, grid=(nl, w // tn),
        in_specs=[pl.BlockSpec((nd, 1, 2, tn), lambda l, j: (0, l, 0, j))],
        out_specs=[pl.BlockSpec((None, 16, tn), lambda l, j: (l, 0, j)), pl.BlockSpec((None, 1, tn), lambda l, j: (l, 0, j))],
        out_shape=[SDS((nl, 16, w), F32), SDS((nl, 1, w), F32)],
        compiler_params=_cparams("parallel", "parallel"), name=name)(dmod_all)


def _ada_dctx(parts, c_ctx, row, name):
    def body(p_ref, c_ref, o_ref):
        ds = p_ref[0, row:row + 1, :]
        for k in range(1, p_ref.shape[0]):
            ds = ds + p_ref[k, row:row + 1, :]
        cv = c_ref[...]
        sg = jax.nn.sigmoid(cv)
        o_ref[...] = ds * (sg * (1.0 + cv * (1.0 - sg)))

    return pl.pallas_call(body, out_shape=SDS(c_ctx.shape, F32), compiler_params=_cparams(), name=name)(parts, c_ctx)


ROW_BLOCK_BYTES = 1 << 20


def _as_rows(shape):
    size = math.prod(shape)
    cols = shape[-1] if len(shape) >= 2 and shape[-1] % LANES == 0 else _pick(size, (1024, 512, 256, 128))
    rows = size // cols
    fits = [t for t in (512, 256, 128, 64, 32, 16, 8) if t * cols * 4 <= ROW_BLOCK_BYTES]
    return rows, cols, _pick(rows, fits)


def _adamw(w, g, m, v, name):
    rows, cols, tr = _as_rows(w.shape)
    c1 = 1.0 / (1.0 - ADAM_B1 ** ADAM_STEP)
    c2 = 1.0 / (1.0 - ADAM_B2 ** ADAM_STEP)

    def body(w_ref, g_ref, m_ref, v_ref, d_ref, nm_ref, nv_ref):
        gv = g_ref[...]
        nm = ADAM_B1 * m_ref[...] + (1.0 - ADAM_B1) * gv
        nv = ADAM_B2 * v_ref[...] + (1.0 - ADAM_B2) * (gv * gv)
        nm_ref[...] = nm
        nv_ref[...] = nv
        d_ref[...] = -ADAM_LR * ((nm * c1) / (jnp.sqrt(nv * c2) + ADAM_EPS) + ADAM_WD * w_ref[...])

    blk = pl.BlockSpec((tr, cols), lambda i: (i, 0))
    outs = pl.pallas_call(
        body, grid=(rows // tr,), in_specs=[blk] * 4, out_specs=[blk] * 3, out_shape=[SDS((rows, cols), F32)] * 3,
        compiler_params=_cparams("parallel"), name=name)(*[t.reshape(rows, cols) for t in (w, g, m, v)])
    return tuple(o.reshape(w.shape) for o in outs)


def _tile_rows(rows, cols, dtype):
    size = jnp.dtype(dtype).itemsize
    fits = [t for t in (512, 256, 128, 64, 32, 16, 8) if t * cols * size <= ROW_BLOCK_BYTES and t * size >= 32]
    return _pick(rows, fits)


def _scalar_spec(grid, in_specs, out_specs):
    return pltpu.PrefetchScalarGridSpec(num_scalar_prefetch=1, grid=grid, in_specs=in_specs, out_specs=out_specs)


def _place_chunk(shard, k_idx, dtype, name):
    nl, rows, cols = shard.shape
    tr = _tile_rows(rows, cols, dtype)

    def body(k_ref, s_ref, o_ref):
        o_ref[...] = s_ref[...].astype(o_ref.dtype)

    return pl.pallas_call(
        body, out_shape=SDS((nl, N_CHIPS, rows, cols), dtype),
        grid_spec=_scalar_spec((nl, rows // tr), [pl.BlockSpec((None, tr, cols), lambda l, i, k: (l, i, 0))],
                               pl.BlockSpec((None, None, tr, cols), lambda l, i, k: (l, k[0], i, 0))),
        compiler_params=_cparams("parallel", "parallel"), name=name)(k_idx, shard)


def _pair_sum(grads, recv, c_idx, name):
    half, nch, rows, cols = recv.shape
    tr = _tile_rows(rows, cols, recv.dtype)

    def body(c_ref, g_ref, r_ref, o_ref):
        o_ref[...] = (g_ref[...].astype(F32) + r_ref[...].astype(F32)).astype(o_ref.dtype)

    blk = pl.BlockSpec((None, None, tr, cols), lambda h, q, i, c: (h, q, i, 0))
    return pl.pallas_call(
        body, out_shape=SDS(recv.shape, recv.dtype),
        grid_spec=_scalar_spec((half, nch, rows // tr),
                               [pl.BlockSpec((None, None, tr, cols), lambda h, q, i, c: (c[0] * half + h, q, i, 0)), blk], blk),
        compiler_params=_cparams("parallel", "parallel", "parallel"), name=name)(c_idx, grads, recv)


def _chip_sum(parts, recv, kc_idx, name):
    half, nch, rows, cols = parts.shape
    tr = _tile_rows(rows, cols, F32)

    def body(kc_ref, p_ref, r_ref, o_ref):
        acc = p_ref[...].astype(F32)
        for s in range(r_ref.shape[0]):
            acc = acc + r_ref[s].astype(F32)
        o_ref[...] = acc

    return pl.pallas_call(
        body, out_shape=SDS((2 * half, rows, cols), F32),
        grid_spec=_scalar_spec((half, rows // tr),
                               [pl.BlockSpec((None, None, tr, cols), lambda h, i, kc: (h, kc[0], i, 0)),
                                pl.BlockSpec((nch - 1, None, tr, cols), lambda h, i, kc: (0, h, i, 0))],
                               pl.BlockSpec((None, tr, cols), lambda h, i, kc: (kc[1] * half + h, i, 0))),
        compiler_params=_cparams("parallel", "parallel"), name=name)(kc_idx, parts, recv)


PIECE_BYTES = 3 << 20
MAX_PIECES = 16
PIECE_ROW_ALIGN = 16


def _coords():
    return lax.axis_index("x"), lax.axis_index("y"), lax.axis_index("c")


def _other_chips(x, y):
    return [(1 - x, y), (x, 1 - y), (1 - x, 1 - y)]


def _row_pieces(rows, nbytes):
    pieces = 1
    while (pieces < MAX_PIECES and nbytes // pieces > PIECE_BYTES and rows % (2 * pieces * PIECE_ROW_ALIGN) == 0):
        pieces *= 2
    step = rows // pieces
    return [pl.ds(i * step, step) for i in range(pieces)]


def _nbytes(shape, dtype):
    return math.prod(shape) * jnp.dtype(dtype).itemsize


def _offsets(counts):
    out, pos = [], 0
    for cnt in counts:
        out.append(pos)
        pos += cnt
    return out, pos


def _gather_chips(placed, name):
    nt = len(placed)
    half = [p.shape[0] // 2 for p in placed]
    pieces = [_row_pieces(p.shape[2], _nbytes((h,) + p.shape[2:], p.dtype)) for p, h in zip(placed, half)]
    base, total = _offsets([len(p) for p in pieces])

    def body(*refs):
        o_refs = refs[nt:2 * nt]
        ssem1, rsem1, ssem2, rsem2 = refs[2 * nt:]
        x, y, c = _coords()
        k = 2 * x + y
        chips = _other_chips(x, y)
        sends = []
        for t in range(nt):
            o_ref = o_refs[t]
            mine = pl.ds(c * half[t], half[t])
            for i, rs in enumerate(pieces[t]):
                for r, (px, py) in enumerate(chips):
                    q = 3 * (base[t] + i) + r
                    cp = pltpu.make_async_remote_copy(
                        src_ref=o_ref.at[mine, k, rs], dst_ref=o_ref.at[mine, k, rs], send_sem=ssem1.at[q],
                        recv_sem=rsem1.at[q], device_id=(px, py, c), device_id_type=MESH)
                    cp.start()
                    sends.append(cp)
        for t in range(nt):
            o_ref = o_refs[t]
            mine = pl.ds(c * half[t], half[t])
            for i, rs in enumerate(pieces[t]):
                for r, (px, py) in enumerate(chips):
                    q = 3 * (base[t] + i) + r
                    kk = 2 * px + py
                    pltpu.make_async_remote_copy(
                        src_ref=o_ref.at[mine, kk, rs], dst_ref=o_ref.at[mine, kk, rs], send_sem=ssem1.at[q],
                        recv_sem=rsem1.at[q], device_id=(px, py, c), device_id_type=MESH).wait_recv()
                    cp = pltpu.make_async_remote_copy(
                        src_ref=o_ref.at[mine, kk, rs], dst_ref=o_ref.at[mine, kk, rs], send_sem=ssem2.at[q],
                        recv_sem=rsem2.at[q], device_id=(x, y, 1 - c), device_id_type=MESH)
                    cp.start()
                    sends.append(cp)
        for t in range(nt):
            o_ref = o_refs[t]
            theirs = pl.ds((1 - c) * half[t], half[t])
            for i, rs in enumerate(pieces[t]):
                for r, (px, py) in enumerate(chips):
                    q = 3 * (base[t] + i) + r
                    kk = 2 * px + py
                    pltpu.make_async_remote_copy(
                        src_ref=o_ref.at[theirs, kk, rs], dst_ref=o_ref.at[theirs, kk, rs], send_sem=ssem2.at[q],
                        recv_sem=rsem2.at[q], device_id=(x, y, 1 - c), device_id_type=MESH).wait_recv()
        for cp in sends:
            cp.wait_send()

    sem = pltpu.SemaphoreType.DMA
    outs = pl.pallas_call(
        body, in_specs=[ANY] * nt, out_specs=[ANY] * nt,
        out_shape=[SDS(p.shape, p.dtype) for p in placed],
        input_output_aliases={t: t for t in range(nt)},
        scratch_shapes=[sem((3 * total,)), sem((3 * total,)), sem((3 * total,)), sem((3 * total,))],
        name=name)(*placed)
    return list(outs)


def _pair_send(grads, name):
    nt = len(grads)
    half = [g.shape[0] // 2 for g in grads]
    pieces = [_row_pieces(g.shape[2], _nbytes((h,) + g.shape[1:], g.dtype)) for g, h in zip(grads, half)]
    base, total = _offsets([len(p) for p in pieces])

    def body(*refs):
        g_refs, o_refs = refs[:nt], refs[nt:2 * nt]
        ssem, rsem = refs[2 * nt:]
        x, y, c = _coords()
        cps = []
        for t in range(nt):
            theirs = pl.ds((1 - c) * half[t], half[t])
            for i, rs in enumerate(pieces[t]):
                q = base[t] + i
                cp = pltpu.make_async_remote_copy(
                    src_ref=g_refs[t].at[theirs, :, rs], dst_ref=o_refs[t].at[:, :, rs], send_sem=ssem.at[q],
                    recv_sem=rsem.at[q], device_id=(x, y, 1 - c), device_id_type=MESH)
                cp.start()
                cps.append(cp)
        for cp in cps:
            cp.wait_recv()
        for cp in cps:
            cp.wait_send()

    sem = pltpu.SemaphoreType.DMA
    outs = pl.pallas_call(
        body, in_specs=[ANY] * nt, out_specs=[ANY] * nt,
        out_shape=[SDS((g.shape[0] // 2,) + g.shape[1:], g.dtype) for g in grads],
        scratch_shapes=[sem((total,)), sem((total,))],
        name=name)(*grads)
    return list(outs)


def _chip_send(parts, name):
    nt = len(parts)
    pieces = [_row_pieces(p.shape[2], _nbytes((p.shape[0],) + p.shape[2:], p.dtype)) for p in parts]
    base, total = _offsets([len(p) for p in pieces])

    def body(*refs):
        p_refs, o_refs = refs[:nt], refs[nt:2 * nt]
        ssem, rsem = refs[2 * nt:]
        x, y, c = _coords()
        cps = []
        for t in range(nt):
            for i, rs in enumerate(pieces[t]):
                for r, (px, py) in enumerate(_other_chips(x, y)):
                    q = 3 * (base[t] + i) + r
                    cp = pltpu.make_async_remote_copy(
                        src_ref=p_refs[t].at[:, 2 * px + py, rs], dst_ref=o_refs[t].at[r, :, rs], send_sem=ssem.at[q],
                        recv_sem=rsem.at[q], device_id=(px, py, c), device_id_type=MESH)
                    cp.start()
                    cps.append(cp)
        for cp in cps:
            cp.wait_recv()
        for cp in cps:
            cp.wait_send()

    sem = pltpu.SemaphoreType.DMA
    outs = pl.pallas_call(
        body, in_specs=[ANY] * nt, out_specs=[ANY] * nt,
        out_shape=[SDS((N_CHIPS - 1, p.shape[0]) + p.shape[2:], p.dtype) for p in parts],
        scratch_shapes=[sem((3 * total,)), sem((3 * total,))],
        name=name)(*parts)
    return list(outs)


def _pair_join(bufs, name):
    nt = len(bufs)
    half = [b.shape[0] // 2 for b in bufs]
    pieces = [_row_pieces(b.shape[1], _nbytes((h,) + b.shape[1:], b.dtype)) for b, h in zip(bufs, half)]
    base, total = _offsets([len(p) for p in pieces])

    def body(*refs):
        o_refs = refs[nt:2 * nt]
        ssem, rsem = refs[2 * nt:]
        x, y, c = _coords()
        cps = []
        for t in range(nt):
            mine = pl.ds(c * half[t], half[t])
            for i, rs in enumerate(pieces[t]):
                q = base[t] + i
                cp = pltpu.make_async_remote_copy(
                    src_ref=o_refs[t].at[mine, rs], dst_ref=o_refs[t].at[mine, rs], send_sem=ssem.at[q],
                    recv_sem=rsem.at[q], device_id=(x, y, 1 - c), device_id_type=MESH)
                cp.start()
                cps.append(cp)
        for t in range(nt):
            theirs = pl.ds((1 - c) * half[t], half[t])
            for i, rs in enumerate(pieces[t]):
                q = base[t] + i
                pltpu.make_async_remote_copy(
                    src_ref=o_refs[t].at[theirs, rs], dst_ref=o_refs[t].at[theirs, rs], send_sem=ssem.at[q],
                    recv_sem=rsem.at[q], device_id=(x, y, 1 - c), device_id_type=MESH).wait_recv()
        for cp in cps:
            cp.wait_send()

    sem = pltpu.SemaphoreType.DMA
    outs = pl.pallas_call(
        body, in_specs=[ANY] * nt, out_specs=[ANY] * nt,
        out_shape=[SDS(b.shape, b.dtype) for b in bufs],
        input_output_aliases={t: t for t in range(nt)},
        scratch_shapes=[sem((total,)), sem((total,))],
        name=name)(*bufs)
    return list(outs)


def _gather_devices(vals, name):
    nt = len(vals)
    flips = [(a, b, e) for a in (0, 1) for b in (0, 1) for e in (0, 1)][1:]

    def body(*refs):
        v_refs, o_refs = refs[:nt], refs[nt:2 * nt]
        lsem, ssem, rsem = refs[2 * nt:]
        x, y, c = _coords()
        me = 4 * x + 2 * y + c
        peers = [((1 - x) if a else x, (1 - y) if b else y, (1 - c) if e else c) for a, b, e in flips]
        cps = []
        for t in range(nt):
            loc = pltpu.make_async_copy(v_refs[t], o_refs[t].at[me], lsem.at[t])
            loc.start()
            cps.append(loc)
            for r, peer in enumerate(peers):
                cp = pltpu.make_async_remote_copy(
                    src_ref=v_refs[t], dst_ref=o_refs[t].at[me], send_sem=ssem.at[7 * t + r],
                    recv_sem=rsem.at[7 * t + r], device_id=peer, device_id_type=MESH)
                cp.start()
                cps.append(cp)
        for t in range(nt):
            for r, (px, py, pc) in enumerate(peers):
                pltpu.make_async_remote_copy(
                    src_ref=v_refs[t], dst_ref=o_refs[t].at[4 * px + 2 * py + pc], send_sem=ssem.at[7 * t + r],
                    recv_sem=rsem.at[7 * t + r], device_id=(px, py, pc), device_id_type=MESH).wait_recv()
        for t in range(nt):
            cps[8 * t].wait()
            for r in range(7):
                cps[8 * t + 1 + r].wait_send()

    sem = pltpu.SemaphoreType.DMA
    outs = pl.pallas_call(
        body, in_specs=[ANY] * nt, out_specs=[ANY] * nt,
        out_shape=[SDS((N_DEV,) + v.shape, v.dtype) for v in vals],
        scratch_shapes=[sem((nt,)), sem((7 * nt,)), sem((7 * nt,))],
        name=name)(*vals)
    return list(outs)


def _gather_all(shards, dtypes, k_idx, tag):
    placed = [_place_chunk(s, k_idx, dt, f"{tag}_place{t}") for t, (s, dt) in enumerate(zip(shards, dtypes))]
    return _gather_chips(placed, f"{tag}_gather")


def _reduce_to_shards(grads, k_idx, c_idx, kc_idx, tag):
    recv = _pair_send(grads, f"{tag}_pair_send")
    pair = [_pair_sum(g, r, c_idx, f"{tag}_pair_sum{t}") for t, (g, r) in enumerate(zip(grads, recv))]
    recv = _chip_send(pair, f"{tag}_chip_send")
    bufs = [_chip_sum(p, r, kc_idx, f"{tag}_chip_sum{t}") for t, (p, r) in enumerate(zip(pair, recv))]
    return _pair_join(bufs, f"{tag}_pair_join")


WEIGHT_NAMES = ("c_ctx", "w_ada", "b_ada", "w_in", "w_pool", "pool_scale", "ssm_a_re", "ssm_a_im", "ssm_log_dt",
                "ssm_b_re", "ssm_b_im", "ssm_c_re", "ssm_c_im", "ssm_d", "w_glu", "w_out", "g_pre_mix", "g_post_mix",
                "g_pre_ffn", "g_post_ffn", "w_up", "w_conv", "w_down")


def _block_diag_in(bb, ng):
    nl, nd, npart, h, gp = bb.shape
    p = gp // ng
    w = jnp.einsum("ldqhgp,kg->lkhdqgp", bb.reshape(nl, nd, npart, h, ng, p), jnp.eye(ng, dtype=bb.dtype))
    return w.reshape(nl, ng * h, nd * npart * gp)


def _diag_in_grad(dw, ng, nh, p):
    gl = ng // S5_BAND
    out = jnp.einsum("ghqagp->qhagp", dw.reshape(gl, nh, 2, S5_BAND, gl, p))
    return out.reshape(2, nh, ng * p)


def _block_diag_out(cs, ng):
    nl, nd, npart, _, h, p = cs.shape
    w = jnp.einsum("ldqghp,kg->ldqkpgh", cs, jnp.eye(ng, dtype=cs.dtype))
    return w.reshape(nl, nd * npart * ng * p, ng * h)


def _diag_out_grad(dw, ng, nh, p):
    gl = ng // S5_BAND
    out = jnp.einsum("qagpgh->qaghp", dw.reshape(2, S5_BAND, gl, p, gl, nh))
    return out.reshape(2, ng, nh, p)


def kernel(x, c, ctx, c_ctx, w_ada, b_ada, w_in, w_pool, pool_scale, ssm_a_re, ssm_a_im, ssm_log_dt, ssm_b_re, ssm_b_im, ssm_c_re, ssm_c_im, ssm_d, w_glu, w_out, g_pre_mix, g_post_mix, g_pre_ffn, g_post_ffn, w_up, w_conv, w_down, loss_target, m_c_ctx, m_w_ada, m_b_ada, m_w_in, m_w_pool, m_pool_scale, m_ssm_a_re, m_ssm_a_im, m_ssm_log_dt, m_ssm_b_re, m_ssm_b_im, m_ssm_c_re, m_ssm_c_im, m_ssm_d, m_w_glu, m_w_out, m_g_pre_mix, m_g_post_mix, m_g_pre_ffn, m_g_post_ffn, m_w_up, m_w_conv, m_w_down, v_c_ctx, v_w_ada, v_b_ada, v_w_in, v_w_pool, v_pool_scale, v_ssm_a_re, v_ssm_a_im, v_ssm_log_dt, v_ssm_b_re, v_ssm_b_im, v_ssm_c_re, v_ssm_c_im, v_ssm_d, v_w_glu, v_w_out, v_g_pre_mix, v_g_post_mix, v_g_pre_ffn, v_g_post_ffn, v_w_up, v_w_conv, v_w_down):
    weights = dict(zip(WEIGHT_NAMES, (c_ctx, w_ada, b_ada, w_in, w_pool, pool_scale, ssm_a_re, ssm_a_im, ssm_log_dt,
                                      ssm_b_re, ssm_b_im, ssm_c_re, ssm_c_im, ssm_d, w_glu, w_out, g_pre_mix, g_post_mix,
                                      g_pre_ffn, g_post_ffn, w_up, w_conv, w_down)))
    mom1 = dict(zip(WEIGHT_NAMES, (m_c_ctx, m_w_ada, m_b_ada, m_w_in, m_w_pool, m_pool_scale, m_ssm_a_re, m_ssm_a_im,
                                   m_ssm_log_dt, m_ssm_b_re, m_ssm_b_im, m_ssm_c_re, m_ssm_c_im, m_ssm_d, m_w_glu, m_w_out,
                                   m_g_pre_mix, m_g_post_mix, m_g_pre_ffn, m_g_post_ffn, m_w_up, m_w_conv, m_w_down)))
    mom2 = dict(zip(WEIGHT_NAMES, (v_c_ctx, v_w_ada, v_b_ada, v_w_in, v_w_pool, v_pool_scale, v_ssm_a_re, v_ssm_a_im,
                                   v_ssm_log_dt, v_ssm_b_re, v_ssm_b_im, v_ssm_c_re, v_ssm_c_im, v_ssm_d, v_w_glu, v_w_out,
                                   v_g_pre_mix, v_g_post_mix, v_g_pre_ffn, v_g_post_ffn, v_w_up, v_w_conv, v_w_down)))

    xi, yi, ci = lax.axis_index("x"), lax.axis_index("y"), lax.axis_index("c")
    chip = 2 * xi + yi
    dev = 4 * xi + 2 * yi + ci
    nl = w_in.shape[0]
    n_lat, d = x.shape[1], x.shape[2]
    n_ctx = ctx.shape[1]
    n = n_ctx + n_lat
    _, ndir, ng, nstate, nh = ssm_b_re.shape
    gp = ng * nstate
    sw = ng * nh
    n_pool_groups, pool_group = w_pool.shape[1], w_pool.shape[3]
    pw = n_pool_groups * pool_group
    assert pw + sw == d and pw % sw == 0 and len(POOL_WINDOWS) == n_pool_groups and n_lat % GRID_W == 0
    dff2 = w_up.shape[2] * N_CHIPS
    ada_w = w_ada.shape[2] * N_CHIPS
    ada_cols = w_ada.shape[2]
    s_rows = gp // LANES

    c_pad = jnp.concatenate([c, jnp.zeros((SUBLANES - 1, d), F32)], axis=0)
    c_all = _gather_devices([c_pad], "gather_cond")[0][:, 0, :]
    cond = jnp.concatenate([c_all, c_ctx[None, :], jnp.zeros((16 - N_DEV - 1, d), F32)], axis=0)
    b_shard = lax.dynamic_slice_in_dim(b_ada, chip * ada_cols, ada_cols, axis=1)[:, None, :]
    mod_shard = _ada_fwd(cond, w_ada, b_shard, "ada_fwd")
    k_idx, c_idx, kc_idx = jnp.stack([chip]), jnp.stack([ci]), jnp.stack([chip, ci])
    mod_all = _gather_all([mod_shard], [F32], k_idx, "mods")[0]
    mod_all = jnp.transpose(mod_all, (0, 2, 1, 3)).reshape(nl, 16, ada_w)
    mod_lat = lax.dynamic_index_in_dim(mod_all, dev, axis=1, keepdims=False).reshape(nl, 6, d)
    mod_ctx = mod_all[:, N_DEV].reshape(nl, 6, d)
    mods = jnp.concatenate([jnp.stack([mod_ctx, mod_lat], axis=1), jnp.zeros((nl, 2, 2, d), F32)], axis=2)

    shards = [w_in, w_pool.reshape(nl, pw // N_CHIPS, pool_group), w_glu, w_out, w_up, w_down,
              w_conv.reshape(nl, 9, dff2 // N_CHIPS)]
    g_in, g_pool, g_glu, g_out, g_up, g_down, g_conv = _gather_all(shards, [COMM_DTYPE] * 6 + [F32], k_idx, "weights")
    wi = g_in.reshape(nl, d, d)
    wp = jnp.transpose(g_pool.reshape(nl, N_CHIPS, n_pool_groups, pool_group // N_CHIPS, pool_group),
                       (0, 2, 1, 3, 4)).reshape(nl, n_pool_groups, pool_group, pool_group)
    wg = g_glu.reshape(nl, sw, sw)
    wo = g_out.reshape(nl, d, d)
    wu = g_up
    wd = g_down.reshape(nl, dff2 // 2, d)
    wk = jnp.transpose(g_conv, (0, 2, 1, 3)).reshape(nl, 9, dff2)

    rows = nl * ndir
    a_re2 = ssm_a_re.reshape(rows, gp)
    a_im2 = ssm_a_im.reshape(rows, gp)
    logdt2 = jnp.repeat(ssm_log_dt.reshape(rows, ng), nstate, axis=1)
    b_re2 = jnp.transpose(ssm_b_re.reshape(rows, gp, nh), (0, 2, 1))
    b_im2 = jnp.transpose(ssm_b_im.reshape(rows, gp, nh), (0, 2, 1))
    lam_re, lam_im, bb_re, bb_im = _disc_fwd(a_re2, a_im2, logdt2, b_re2, b_im2, "s5_discretise")
    lam = jnp.stack([lam_re.reshape(nl, ndir, s_rows, LANES), lam_im.reshape(nl, ndir, s_rows, LANES)], axis=2)
    lam = lam.reshape(nl, 2 * ndir, s_rows, LANES)
    bbs = jnp.stack([bb_re.reshape(nl, ndir, nh, gp), bb_im.reshape(nl, ndir, nh, gp)], axis=2)
    w_b = _block_diag_in(bbs, ng).astype(MXU_DTYPE)
    w_b = [w_b[:, :, dr * 2 * gp:(dr + 1) * 2 * gp] for dr in range(ndir)]
    cs = jnp.stack([ssm_c_re, -ssm_c_im], axis=2)
    w_c = _block_diag_out(cs, ng).astype(MXU_DTYPE)
    w_c = [w_c[:, dr * 2 * gp:(dr + 1) * 2 * gp] for dr in range(ndir)]

    def row(v, l):
        return v[l:l + 1]

    xc = jnp.concatenate([ctx[0], x[0]], axis=0)
    saved = []
    for l in range(nl):
        t = f"l{l}"
        md = mods[l]
        h = _norm_mod_fwd(xc, row(g_pre_mix, l), md, 0, 1, n_ctx, f"{t}_pre_mix")
        u = _mm(h, wi, "nn", F32, f"{t}_in_proj", b_idx=l)
        p = _pool(u, pw, pool_group, n_ctx, False, MXU_DTYPE, f"{t}_pool")
        ypool = _pool_proj_fwd(p, wp, l, row(pool_scale, l), f"{t}_pool_proj")
        bu0 = _mm_s5(u, w_b[0], "in", f"{t}_s5_in0", b_idx=l, a_cols=(pw, sw))
        bu1 = _mm_s5(u, w_b[1], "in", f"{t}_s5_in1", b_idx=l, a_cols=(pw, sw))
        h0, h1 = _scan_fwd(bu0, bu1, lam[l], n_ctx, f"{t}_scan")
        y = _mm_s5(h0, w_c[0], "out", f"{t}_s5_out0", b_idx=l)
        y = _mm_s5(h1, w_c[1], "out", f"{t}_s5_out1", b_idx=l, add=y)
        s_out = _ssm_head_fwd(y, u, row(ssm_d, l), wg, l, f"{t}_s5_head")
        cat = jnp.concatenate([ypool, s_out], axis=1)
        mix = _mm(cat, wo, "nn", F32, f"{t}_out_proj", b_idx=l)
        x_mid = _gate_res_fwd(xc, mix, row(g_post_mix, l), md, 2, n_ctx, f"{t}_post_mix")
        h2 = _norm_mod_fwd(x_mid, row(g_pre_ffn, l), md, 3, 4, n_ctx, f"{t}_pre_ffn")
        z = _mm(h2, wu, "nn", F32, f"{t}_up", b_idx=l, b_chunks=N_CHIPS)
        act, cv, cg = _conv_glu_fwd(z, wk[l], n_ctx, f"{t}_conv_glu")
        f = _mm(act, wd, "nn", F32, f"{t}_down", b_idx=l)
        x_out = _gate_res_fwd(x_mid, f, row(g_post_ffn, l), md, 5, n_ctx, f"{t}_post_ffn")
        saved.append(dict(xc=xc, h=h, u=u, p=p, h0=h0, h1=h1, y=y, cat=cat, mix=mix, x_mid=x_mid, h2=h2, z=z, cv=cv, cg=cg, f=f))
        xc = x_out

    dx, loss_tile = _loss_grad(xc, loss_target[0], n_ctx, "loss")
    loss = lax.psum(loss_tile[0, 0], ("x", "y", "c"))

    big = {k: [None] * nl for k in ("w_in", "w_pool", "w_glu", "w_out", "w_up", "w_down")}
    small = {k: [None] * nl for k in ("pool_scale", "ssm_d", "g_pre_mix", "g_post_mix", "g_pre_ffn", "g_post_ffn",
                                      "lam", "bb", "cs", "w_conv")}
    dmods = [None] * nl
    for l in reversed(range(nl)):
        t = f"l{l}b"
        md = mods[l]
        sv = saved[l]
        df, dgate_ffn, small["g_post_ffn"][l] = _gate_res_bwd(dx, sv["f"], row(g_post_ffn, l), md, 5, n_ctx, f"{t}_post_ffn")
        dact = _mm(df, wd, "nt", F32, f"{t}_down_dx", b_idx=l)
        dz, dkv, dkg, act = _conv_glu_bwd(sv["z"], sv["cv"], sv["cg"], dact, wk[l], n_ctx, f"{t}_conv_glu")
        small["w_conv"][l] = jnp.concatenate([dkv, dkg], axis=1)
        big["w_down"][l] = _mm(act, df, "tn", COMM_DTYPE, f"{t}_down_dw")
        big["w_up"][l] = _mm(sv["h2"], dz, "tn", COMM_DTYPE, f"{t}_up_dw", b_chunks=2, out_chunks=N_CHIPS)
        dh2 = _mm(dz, wu, "nt", F32, f"{t}_up_dx", b_idx=l, a_chunks=2, b_chunks=N_CHIPS)
        dx, dss_ffn, small["g_pre_ffn"][l] = _norm_mod_bwd(dh2, sv["x_mid"], row(g_pre_ffn, l), md, 3, 4, dx, n_ctx,
                                                           f"{t}_pre_ffn")
        dmix, dgate_mix, small["g_post_mix"][l] = _gate_res_bwd(dx, sv["mix"], row(g_post_mix, l), md, 2, n_ctx,
                                                                f"{t}_post_mix")
        dcat = _mm(dmix, wo, "nt", F32, f"{t}_out_dx", b_idx=l)
        big["w_out"][l] = _mm(sv["cat"], dmix, "tn", COMM_DTYPE, f"{t}_out_dw")
        dp, small["pool_scale"][l], big["w_pool"][l] = _pool_proj_bwd(sv["p"], dcat, wp, l, row(pool_scale, l),
                                                                      f"{t}_pool_proj")
        du_pool = _pool(dp, pw, pool_group, n_ctx, True, F32, f"{t}_pool")
        dy, du_dir, gact, dq, small["ssm_d"][l] = _ssm_head_bwd(dcat, sv["y"], sv["u"], row(ssm_d, l), wg, l, f"{t}_s5_head")
        big["w_glu"][l] = _mm(gact, dq, "tn", COMM_DTYPE, f"{t}_glu_dw")
        dh0 = _mm_s5(dy, w_c[0], "out_dx", f"{t}_s5_out_dx0", b_idx=l)
        dh1 = _mm_s5(dy, w_c[1], "out_dx", f"{t}_s5_out_dx1", b_idx=l)
        small["cs"][l] = jnp.stack([_diag_out_grad(_mm_s5(sv[hk], dy, "out_dw", f"{t}_s5_out_dw{dr}"), ng, nh, nstate)
                                    for dr, hk in enumerate(("h0", "h1"))], axis=0)
        a0, a1, small["lam"][l] = _scan_bwd(dh0, dh1, sv["h0"], sv["h1"], lam[l], n_ctx, f"{t}_scan")
        du_proj = _mm_s5(a0, w_b[0], "in_dx", f"{t}_s5_in_dx0", b_idx=l)
        du_proj = _mm_s5(a1, w_b[1], "in_dx", f"{t}_s5_in_dx1", b_idx=l, add=du_proj)
        small["bb"][l] = jnp.stack([_diag_in_grad(_mm_s5(sv["u"], adj, "in_dw", f"{t}_s5_in_dw{dr}", a_cols=(pw, sw)),
                                                  ng, nh, nstate) for dr, adj in enumerate((a0, a1))], axis=0)
        du = _assemble_du(du_pool, du_dir, du_proj, f"{t}_du")
        dh = _mm(du, wi, "nt", F32, f"{t}_in_dx", b_idx=l)
        big["w_in"][l] = _mm(sv["h"], du, "tn", COMM_DTYPE, f"{t}_in_dw")
        dx, dss_mix, small["g_pre_mix"][l] = _norm_mod_bwd(dh, sv["xc"], row(g_pre_mix, l), md, 0, 1, dx, n_ctx,
                                                           f"{t}_pre_mix")
        dmods[l] = jnp.concatenate([dss_mix, dgate_mix, dss_ffn, dgate_ffn], axis=1).reshape(2, ada_w)

    grad_x = dx[n_ctx:][None]

    dmod_all = _gather_devices([jnp.stack(dmods, axis=0)], "gather_dmods")[0]
    ada_rows, db_ada = _ada_rows(dmod_all, "ada_rows")
    rows_shard = lax.dynamic_slice_in_dim(ada_rows, chip * ada_cols, ada_cols, axis=2)
    dcond_part = _ada_dcond(rows_shard, w_ada, "ada_dcond")
    dcond_parts = _gather_devices([dcond_part], "gather_dcond")[0][0::2]
    grads = {"w_ada": _ada_dw(cond, rows_shard, "ada_dw"), "b_ada": db_ada[:, 0, :],
             "c_ctx": _ada_dctx(dcond_parts, c_ctx[None, :], N_DEV, "ada_dctx")[0]}

    stacked = {k: jnp.stack(v, axis=0) for k, v in big.items()}
    parts = [stacked["w_in"].reshape(nl, N_CHIPS, d // N_CHIPS, d),
             jnp.transpose(stacked["w_pool"].astype(COMM_DTYPE).reshape(nl, n_pool_groups, N_CHIPS, pool_group // N_CHIPS,
                                                                      pool_group), (0, 2, 1, 3, 4))
             .reshape(nl, N_CHIPS, pw // N_CHIPS, pool_group),
             stacked["w_glu"].reshape(nl, N_CHIPS, sw // N_CHIPS, sw),
             stacked["w_out"].reshape(nl, N_CHIPS, d // N_CHIPS, d),
             stacked["w_up"],
             stacked["w_down"].reshape(nl, N_CHIPS, dff2 // 2 // N_CHIPS, d)]
    r_in, r_pool, r_glu, r_out, r_up, r_down = _reduce_to_shards(parts, k_idx, c_idx, kc_idx, "big")
    grads.update(w_in=r_in, w_pool=r_pool.reshape(w_pool.shape), w_glu=r_glu, w_out=r_out, w_up=r_up, w_down=r_down)

    order = ("pool_scale", "ssm_d", "g_pre_mix", "g_post_mix", "g_pre_ffn", "g_post_ffn", "lam", "bb", "cs", "w_conv")
    pieces = [jnp.stack(small[k], axis=0) for k in order]
    flat = jnp.concatenate([q.reshape(-1) for q in pieces])
    unit = nl * N_CHIPS * SUBLANES * 1024
    padded = -(-flat.shape[0] // unit) * unit
    flat = jnp.concatenate([flat, jnp.zeros((padded - flat.shape[0],), F32)])
    vec = flat.reshape(nl, N_CHIPS, padded // (nl * N_CHIPS * 1024), 1024)
    vec = _gather_all(_reduce_to_shards([vec], k_idx, c_idx, kc_idx, "small"), [F32], k_idx, "small_all")[0].reshape(-1)
    red, pos = {}, 0
    for k, q in zip(order, pieces):
        red[k] = vec[pos:pos + q.size].reshape(q.shape)
        pos += q.size
    for k in ("pool_scale", "ssm_d", "g_pre_mix", "g_post_mix", "g_pre_ffn", "g_post_ffn"):
        grads[k] = red[k][:, 0, :]
    dlam = red["lam"].reshape(nl, ndir, 2, gp)
    dbb = red["bb"].reshape(nl, ndir, 2, nh, gp)
    d_are, d_aim, d_ldt, d_bre, d_bim = _disc_bwd(
        a_re2, a_im2, logdt2, b_re2, b_im2, dlam[:, :, 0].reshape(rows, gp), dlam[:, :, 1].reshape(rows, gp),
        dbb[:, :, 0].reshape(rows, nh, gp), dbb[:, :, 1].reshape(rows, nh, gp), nstate, "s5_discretise_bwd")
    grads["ssm_a_re"] = d_are.reshape(ssm_a_re.shape)
    grads["ssm_a_im"] = d_aim.reshape(ssm_a_im.shape)
    grads["ssm_log_dt"] = d_ldt[:, :ng].reshape(ssm_log_dt.shape)
    grads["ssm_b_re"] = jnp.transpose(d_bre, (0, 2, 1)).reshape(ssm_b_re.shape)
    grads["ssm_b_im"] = jnp.transpose(d_bim, (0, 2, 1)).reshape(ssm_b_im.shape)
    grads["ssm_c_re"] = red["cs"][:, :, 0]
    grads["ssm_c_im"] = -red["cs"][:, :, 1]
    conv_cols = dff2 // N_CHIPS
    grads["w_conv"] = lax.dynamic_slice_in_dim(red["w_conv"], chip * conv_cols, conv_cols, axis=2).reshape(w_conv.shape)

    delta, new_m, new_v = {}, {}, {}
    for k in WEIGHT_NAMES:
        delta[k], new_m[k], new_v[k] = _adamw(weights[k], grads[k], mom1[k], mom2[k], f"adamw_{k}")
    return (loss, grad_x, *[grads[k] for k in WEIGHT_NAMES], *[delta[k] for k in WEIGHT_NAMES],
            *[new_m[k] for k in WEIGHT_NAMES], *[new_v[k] for k in WEIGHT_NAMES])
```

```python
import math

import jax
import jax.numpy as jnp
from jax import lax
from jax.experimental import pallas as pl
from jax.experimental.pallas import tpu as pltpu

F32 = jnp.float32
MXU_DTYPE = jnp.bfloat16
COMM_DTYPE = jnp.bfloat16
HIGHEST = lax.Precision.HIGHEST
VMEM_LIMIT_BYTES = 48 * 1024 * 1024
LANES = 128
SUBLANES = 8
N_CHIPS = 4
N_DEV = 8

EPS = 1e-6
GRID_W = 64
POOL_WINDOWS = (2, 4, 8, 16)
ADAM_LR = 0.001
ADAM_B1 = 0.9
ADAM_B2 = 0.999
ADAM_EPS = 1e-08
ADAM_WD = 0.01
ADAM_STEP = 10
GELU_C0 = math.sqrt(2.0 / math.pi)
GELU_C1 = 0.044715

SDS = jax.ShapeDtypeStruct
ANY = pl.BlockSpec(memory_space=pl.ANY)
MESH = pl.DeviceIdType.MESH


def _cparams(*sem):
    return pltpu.CompilerParams(dimension_semantics=sem if sem else None, vmem_limit_bytes=VMEM_LIMIT_BYTES)


def _pick(n, cands):
    for cand in cands:
        if n % cand == 0:
            return cand
    return n


def _row_tile(n_ctx, n):
    return math.gcd(math.gcd(n_ctx, n - n_ctx), 256)


_DIMS = {"nn": (((1,), (0,)), ((), ())), "nt": (((1,), (1,)), ((), ())), "tn": (((0,), (0,)), ((), ()))}
_TM = (1088, 1024, 512, 384, 256, 128, 64, 32, 16, 8)
_TN = (1024, 1408, 512, 384, 256, 128)
_TK = (1024, 1088, 512, 1408, 384, 256, 128, 64, 32, 16, 8)


def _chunk_of(idx, per, chunks):
    out = 0
    for q in range(1, chunks):
        out = out + (idx >= q * per).astype(jnp.int32)
    return out


def _within(idx, per, chunks):
    return idx - per * _chunk_of(idx, per, chunks)


def _mm_call(args, mode, out_dtype, name, grid, tiles, specs, o_spec, o_shape, a3d=False, b3d=False, out3d=False,
             add=False):
    tm, tn, _ = tiles
    nk = grid[2]

    def operand(ref, is3d):
        v = ref[...]
        if is3d:
            v = pltpu.einshape("tjl->t(jl)", v)
        return v.astype(MXU_DTYPE)

    def body(*refs):
        a_ref, b_ref, o_ref, acc_ref = refs[0], refs[1], refs[-2], refs[-1]
        kk = pl.program_id(2)

        @pl.when(kk == 0)
        def _():
            acc_ref[...] = jnp.zeros(acc_ref.shape, F32)

        acc_ref[...] += lax.dot_general(operand(a_ref, a3d), operand(b_ref, b3d), _DIMS[mode], preferred_element_type=F32)

        @pl.when(kk == nk - 1)
        def _():
            total = acc_ref[...]
            if add:
                total = total + refs[2][...]
            if out3d:
                total = pltpu.einshape("t(jl)->tjl", total, l=LANES)
            o_ref[...] = total.astype(o_ref.dtype)

    return pl.pallas_call(
        body, grid=grid, in_specs=specs, out_specs=o_spec, out_shape=SDS(o_shape, out_dtype),
        scratch_shapes=[pltpu.VMEM((tm, tn), F32)],
        compiler_params=_cparams("parallel", "parallel", "arbitrary"), name=name)(*args)


def _lanes3(blk, imap):
    return (blk[0], blk[1] // LANES, LANES), (lambda i, j, kk: imap(i, j, kk) + (0,))


def _mm(a, b, mode, out_dtype, name, a_idx=None, b_idx=None, a_cols=None, a_chunks=None, b_chunks=None, out_chunks=None,
        a3d=False, b3d=False, out3d=False, add=None):
    a2, b2 = a.shape[-2:], b.shape[-2:]
    if a3d:
        a2 = (a.shape[0], a.shape[1] * a.shape[2])
    if b3d:
        assert mode == "tn"
        b2 = (b.shape[0], b.shape[1] * b.shape[2])
    if a_chunks is not None:
        assert mode == "nt" and a.shape[-3] == a_chunks
        a2 = (a2[0], a2[1] * a_chunks)
    if b_chunks is not None:
        assert b.shape[-3] == b_chunks
        b2 = (b2[0], b2[1] * b_chunks)
    alast = a2[1] if a_cols is None else a_cols[1]
    if mode == "nn":
        m, k, n = a2[0], alast, b2[1]
        assert b2[0] == k
    elif mode == "nt":
        m, k, n = a2[0], alast, b2[0]
        assert b2[1] == k
    else:
        k, m, n = a2[0], alast, b2[1]
        assert b2[0] == k
    n_unit = n // (out_chunks or 1) // (b_chunks if b_chunks and mode != "nt" else 1)
    k_unit = k // (b_chunks if b_chunks and mode == "nt" else 1) // (a_chunks or 1)
    tm, tn, tk = _pick(m, _TM), _pick(n_unit, _TN), _pick(k_unit, _TK)
    nk = k // tk
    a_lane_tile = tm if mode == "tn" else tk
    off = 0
    if a_cols is not None:
        assert a_cols[0] % a_lane_tile == 0
        off = a_cols[0] // a_lane_tile

    if mode == "tn":
        a_blk, a_map = (tk, tm), (lambda i, j, kk: (kk, i + off))
    else:
        a_blk, a_map = (tm, tk), (lambda i, j, kk: (i, kk + off))
        if a_chunks is not None:
            aper = k // a_chunks // tk
            a_blk, a_map = (None, tm, tk), (lambda i, j, kk: (_chunk_of(kk, aper, a_chunks), i, _within(kk, aper, a_chunks)))
    if mode == "nt":
        b_blk, b_map = (tn, tk), (lambda i, j, kk: (j, kk))
        if b_chunks is not None:
            per = k // b_chunks // tk
            b_blk, b_map = (None, tn, tk), (lambda i, j, kk: (_chunk_of(kk, per, b_chunks), j, _within(kk, per, b_chunks)))
    else:
        b_blk, b_map = (tk, tn), (lambda i, j, kk: (kk, j))
        if b_chunks is not None:
            per = n // b_chunks // tn
            b_blk, b_map = (None, tk, tn), (lambda i, j, kk: (_chunk_of(j, per, b_chunks), kk, _within(j, per, b_chunks)))
    if a3d:
        a_blk, a_map = _lanes3(a_blk, a_map)
    if b3d:
        b_blk, b_map = _lanes3(b_blk, b_map)
    if a_idx is not None:
        a_blk, a_map0 = (None,) + a_blk, a_map
        a_map = lambda i, j, kk: (a_idx,) + a_map0(i, j, kk)
    if b_idx is not None:
        b_blk, b_map0 = (None,) + b_blk, b_map
        b_map = lambda i, j, kk: (b_idx,) + b_map0(i, j, kk)
    o_blk, o_map, o_shape = (tm, tn), (lambda i, j, kk: (i, j)), (m, n)
    if out_chunks is not None:
        oper = n // out_chunks // tn
        o_map = lambda i, j, kk: (_chunk_of(j, oper, out_chunks), i, _within(j, oper, out_chunks))
        o_blk, o_shape = (None, tm, tn), (out_chunks, m, n // out_chunks)
    if out3d:
        o_blk, o_map = _lanes3(o_blk, o_map)
        o_shape = (m, n // LANES, LANES)
    specs, args = [pl.BlockSpec(a_blk, a_map), pl.BlockSpec(b_blk, b_map)], [a, b]
    if add is not None:
        specs.append(pl.BlockSpec((tm, tn), lambda i, j, kk: (i, j)))
        args.append(add)
    return _mm_call(args, mode, out_dtype, name, (m // tm, n // tn, nk), (tm, tn, tk), specs, pl.BlockSpec(o_blk, o_map),
                    o_shape, a3d, b3d, out3d, add is not None)


S5_BAND = 2


def _mm_s5(a, b, kind, name, b_idx=None, a_cols=None, add=None):
    nb = S5_BAND
    wide3 = a if kind in ("in_dx", "out", "out_dw") else (b if kind == "in_dw" else None)
    if kind in ("in", "out_dx"):
        rows, wide = a.shape[0], b.shape[-1] if kind == "in" else b.shape[-2]
    else:
        rows, wide = wide3.shape[0], wide3.shape[1] * LANES
    sw = a_cols[1] if a_cols is not None else (b.shape[-1] if kind in ("out", "out_dw") else
                                                 (a.shape[1] if kind == "out_dx" else b.shape[-2]))
    tw, ts = wide // (2 * nb), sw // nb
    nwt = wide // tw
    off = 0 if a_cols is None else a_cols[0] // ts
    half = lambda t: _within(t, nb, nwt // nb)
    lead = (lambda blk, imap: (blk, imap)) if b_idx is None else (
        lambda blk, imap: ((None,) + blk, lambda i, j, kk: (b_idx,) + imap(i, j, kk)))
    rt = _pick(rows, _TM)
    if kind in ("in", "out_dx"):
        mode = "nn" if kind == "in" else "nt"
        a_spec = pl.BlockSpec((rt, ts), lambda i, j, kk: (i, off + half(j)))
        b_blk, b_map = ((ts, tw), lambda i, j, kk: (half(j), j)) if kind == "in" else ((tw, ts), lambda i, j, kk: (j, half(j)))
        o_blk, o_map = _lanes3((rt, tw), lambda i, j, kk: (i, j))
        return _mm_call([a, b], mode, F32, name, (rows // rt, nwt, 1), (rt, tw, ts), [a_spec, pl.BlockSpec(*lead(b_blk, b_map))],
                        pl.BlockSpec(o_blk, o_map), (rows, wide // LANES, LANES), out3d=True)
    if kind in ("out", "in_dx"):
        mode = "nn" if kind == "out" else "nt"
        a_blk, a_map = _lanes3((rt, tw), lambda i, j, kk: (i, kk * nb + j))
        b_blk, b_map = ((tw, ts), lambda i, j, kk: (kk * nb + j, j)) if kind == "out" else (
            (ts, tw), lambda i, j, kk: (j, kk * nb + j))
        specs, args = [pl.BlockSpec(a_blk, a_map), pl.BlockSpec(*lead(b_blk, b_map))], [a, b]
        if add is not None:
            specs.append(pl.BlockSpec((rt, ts), lambda i, j, kk: (i, j)))
            args.append(add)
        return _mm_call(args, mode, F32, name, (rows // rt, nb, nwt // nb), (rt, ts, tw), specs,
                        pl.BlockSpec((rt, ts), lambda i, j, kk: (i, j)), (rows, sw), a3d=True, add=add is not None)
    kt = _pick(rows, _TK)
    if kind == "out_dw":
        a_blk, a_map = _lanes3((kt, tw), lambda i, j, kk: (kk, i))
        return _mm_call([a, b], "tn", F32, name, (nwt, 1, rows // kt), (tw, ts, kt),
                        [pl.BlockSpec(a_blk, a_map), pl.BlockSpec((kt, ts), lambda i, j, kk: (kk, half(i)))],
                        pl.BlockSpec((tw, ts), lambda i, j, kk: (i, 0)), (wide, ts), a3d=True)
    assert kind == "in_dw"
    b_blk, b_map = _lanes3((kt, tw), lambda i, j, kk: (kk, j))
    return _mm_call([a, b], "tn", F32, name, (1, nwt, rows // kt), (ts, tw, kt),
                    [pl.BlockSpec((kt, ts), lambda i, j, kk: (kk, off + half(j))), pl.BlockSpec(b_blk, b_map)],
                    pl.BlockSpec((ts, tw), lambda i, j, kk: (0, j)), (ts, wide), b3d=True)


def _seg_map(nbc):
    return lambda i: (jnp.where(i < nbc, 0, 1), 0, 0)


def _rstd(v):
    return lax.rsqrt(jnp.mean(v * v, axis=-1, keepdims=True) + EPS)


def _norm_mod_fwd(x, g, mods, sh, sc, n_ctx, name):
    n, d = x.shape
    tm = _row_tile(n_ctx, n)
    nbc = n_ctx // tm

    def body(x_ref, g_ref, m_ref, h_ref):
        xv = x_ref[...]
        hn = xv * _rstd(xv) * g_ref[...]
        h_ref[...] = (hn * (1.0 + m_ref[0, sc:sc + 1, :]) + m_ref[0, sh:sh + 1, :]).astype(h_ref.dtype)

    row = pl.BlockSpec((tm, d), lambda i: (i, 0))
    return pl.pallas_call(
        body, grid=(n // tm,),
        in_specs=[row, pl.BlockSpec((1, d), lambda i: (0, 0)), pl.BlockSpec((1, 8, d), _seg_map(nbc))],
        out_specs=row, out_shape=SDS((n, d), MXU_DTYPE), compiler_params=_cparams("parallel"), name=name)(x, g, mods)


def _gate_res_fwd(x, f, g, mods, gi, n_ctx, name):
    n, d = x.shape
    tm = _row_tile(n_ctx, n)
    nbc = n_ctx // tm

    def body(x_ref, f_ref, g_ref, m_ref, o_ref):
        fv = f_ref[...]
        o_ref[...] = x_ref[...] + m_ref[0, gi:gi + 1, :] * (fv * _rstd(fv) * g_ref[...])

    row = pl.BlockSpec((tm, d), lambda i: (i, 0))
    return pl.pallas_call(
        body, grid=(n // tm,),
        in_specs=[row, row, pl.BlockSpec((1, d), lambda i: (0, 0)), pl.BlockSpec((1, 8, d), _seg_map(nbc))],
        out_specs=row, out_shape=SDS((n, d), F32), compiler_params=_cparams("parallel"), name=name)(x, f, g, mods)


def _gate_res_bwd(dx, f, g, mods, gi, n_ctx, name):
    n, d = dx.shape
    tm = _row_tile(n_ctx, n)
    nbc = n_ctx // tm

    def body(dx_ref, f_ref, g_ref, m_ref, df_ref, dgate_ref, dg_ref):
        i = pl.program_id(0)

        @pl.when(i == 0)
        def _():
            dg_ref[...] = jnp.zeros(dg_ref.shape, F32)

        @pl.when(jnp.logical_or(i == 0, i == nbc))
        def _():
            dgate_ref[...] = jnp.zeros(dgate_ref.shape, F32)

        dxv, fv, gv = dx_ref[...], f_ref[...], g_ref[...]
        rs = _rstd(fv)
        nv = fv * rs
        dgate_ref[0] += jnp.sum(dxv * (nv * gv), axis=0, keepdims=True)
        dout = dxv * m_ref[0, gi:gi + 1, :]
        dg_ref[...] += jnp.sum(dout * nv, axis=0, keepdims=True)
        dn = dout * gv
        df_ref[...] = (rs * (dn - nv * jnp.mean(dn * nv, axis=-1, keepdims=True))).astype(df_ref.dtype)

    row = pl.BlockSpec((tm, d), lambda i: (i, 0))
    vec = pl.BlockSpec((1, d), lambda i: (0, 0))
    return pl.pallas_call(
        body, grid=(n // tm,),
        in_specs=[row, row, vec, pl.BlockSpec((1, 8, d), _seg_map(nbc))],
        out_specs=[row, pl.BlockSpec((1, 1, d), _seg_map(nbc)), vec],
        out_shape=[SDS((n, d), MXU_DTYPE), SDS((2, 1, d), F32), SDS((1, d), F32)],
        compiler_params=_cparams("arbitrary"), name=name)(dx, f, g, mods)


def _norm_mod_bwd(dh, x, g, mods, sh, sc, dx_res, n_ctx, name):
    n, d = x.shape
    tm = _row_tile(n_ctx, n)
    nbc = n_ctx // tm

    def body(dh_ref, x_ref, g_ref, m_ref, r_ref, dx_ref, dss_ref, dg_ref):
        i = pl.program_id(0)

        @pl.when(i == 0)
        def _():
            dg_ref[...] = jnp.zeros(dg_ref.shape, F32)

        @pl.when(jnp.logical_or(i == 0, i == nbc))
        def _():
            dss_ref[...] = jnp.zeros(dss_ref.shape, F32)

        dhv, xv, gv = dh_ref[...], x_ref[...], g_ref[...]
        rs = _rstd(xv)
        nv = xv * rs
        dss_ref[0, 0:1, :] += jnp.sum(dhv, axis=0, keepdims=True)
        dss_ref[0, 1:2, :] += jnp.sum(dhv * (nv * gv), axis=0, keepdims=True)
        dhn = dhv * (1.0 + m_ref[0, sc:sc + 1, :])
        dg_ref[...] += jnp.sum(dhn * nv, axis=0, keepdims=True)
        dn = dhn * gv
        dx_ref[...] = r_ref[...] + rs * (dn - nv * jnp.mean(dn * nv, axis=-1, keepdims=True))

    row = pl.BlockSpec((tm, d), lambda i: (i, 0))
    vec = pl.BlockSpec((1, d), lambda i: (0, 0))
    return pl.pallas_call(
        body, grid=(n // tm,),
        in_specs=[row, row, vec, pl.BlockSpec((1, 8, d), _seg_map(nbc)), row],
        out_specs=[row, pl.BlockSpec((1, 2, d), _seg_map(nbc)), vec],
        out_shape=[SDS((n, d), F32), SDS((2, 2, d), F32), SDS((1, d), F32)],
        compiler_params=_cparams("arbitrary"), name=name)(dh, x, g, mods, dx_res)


def _loss_grad(xc, target, n_ctx, name):
    n, d = xc.shape
    tm = _row_tile(n_ctx, n)
    nbc = n_ctx // tm
    nb = n // tm

    def body(x_ref, t_ref, dx_ref, l_ref, acc_ref):
        i = pl.program_id(0)

        @pl.when(i == 0)
        def _():
            acc_ref[...] = jnp.zeros(acc_ref.shape, F32)

        @pl.when(i < nbc)
        def _():
            dx_ref[...] = jnp.zeros(dx_ref.shape, F32)

        @pl.when(i >= nbc)
        def _():
            diff = x_ref[...] - t_ref[...]
            dx_ref[...] = diff * (1.0 / d)
            acc_ref[...] += jnp.sum(diff * diff, axis=0, keepdims=True)

        @pl.when(i == nb - 1)
        def _():
            l_ref[...] = jnp.full(l_ref.shape, (0.5 / d) * jnp.sum(acc_ref[...]), F32)

    row = pl.BlockSpec((tm, d), lambda i: (i, 0))
    return pl.pallas_call(
        body, grid=(nb,),
        in_specs=[row, pl.BlockSpec((tm, d), lambda i: (jnp.maximum(i - nbc, 0), 0))],
        out_specs=[row, pl.BlockSpec((SUBLANES, LANES), lambda i: (0, 0))],
        out_shape=[SDS((n, d), F32), SDS((SUBLANES, LANES), F32)],
        scratch_shapes=[pltpu.VMEM((1, d), F32)],
        compiler_params=_cparams("arbitrary"), name=name)(xc, target)


POOL_PAD = 16


def _pool(src, pool_width, pool_group, n_ctx, bwd, out_dtype, name):
    n = src.shape[0]
    n_lat = n - n_ctx
    gb = pool_group // LANES
    segs = ((0, n_ctx, POOL_PAD), (n_ctx, n_lat, 2 * POOL_PAD + n_ctx))
    total = 3 * POOL_PAD + n

    def body(s_ref, o_ref, scr):
        j = pl.program_id(0)
        for base in (0, POOL_PAD + n_ctx, 2 * POOL_PAD + n):
            scr[pl.ds(base, POOL_PAD), :] = jnp.zeros((POOL_PAD, LANES), F32)
        for gi, w in enumerate(POOL_WINDOWS):
            @pl.when(jnp.logical_and(j >= gi * gb, j < (gi + 1) * gb))
            def _(w=w):
                half = w // 2
                offs = range(-half + 1, half + 1) if bwd else range(-half, half)
                for row0, nseg, base in segs:
                    ch = math.gcd(nseg, 256)

                    def count(c0):
                        t = c0 + lax.broadcasted_iota(jnp.int32, (ch, LANES), 0)
                        return (jnp.minimum(t + half, nseg) - jnp.maximum(t - half, 0)).astype(F32)

                    def fill(ci, carry):
                        c0 = pl.multiple_of(ci * ch, ch)
                        v = s_ref[pl.ds(row0 + c0, ch), :]
                        scr[pl.ds(base + c0, ch), :] = v / count(c0) if bwd else v
                        return carry

                    def window(ci, carry):
                        c0 = pl.multiple_of(ci * ch, ch)
                        acc = jnp.zeros((ch, LANES), F32)
                        for off in offs:
                            acc = acc + scr[pl.ds(c0 + (base + off), ch), :]
                        v = s_ref[pl.ds(row0 + c0, ch), :]
                        res = acc - v if bwd else acc / count(c0) - v
                        o_ref[pl.ds(row0 + c0, ch), :] = res.astype(o_ref.dtype)
                        return carry

                    lax.fori_loop(0, nseg // ch, fill, 0)
                    lax.fori_loop(0, nseg // ch, window, 0)

    blk = pl.BlockSpec((n, LANES), lambda j: (0, j))
    return pl.pallas_call(
        body, grid=(pool_width // LANES,), in_specs=[blk], out_specs=blk,
        out_shape=SDS((n, pool_width), out_dtype), scratch_shapes=[pltpu.VMEM((total, LANES), F32)],
        compiler_params=_cparams("parallel"), name=name)(src)


def _pool_proj_fwd(p, wp, l, scale, name):
    n, pw = p.shape
    ng, c = wp.shape[1], wp.shape[2]
    tm = _pick(n, _TM)

    def body(p_ref, w_ref, s_ref, o_ref):
        y = jnp.dot(p_ref[...], w_ref[...].astype(MXU_DTYPE), preferred_element_type=F32)
        o_ref[...] = (y * s_ref[...]).astype(o_ref.dtype)

    return pl.pallas_call(
        body, grid=(ng, n // tm),
        in_specs=[pl.BlockSpec((tm, c), lambda g, i: (i, g)), pl.BlockSpec((None, None, c, c), lambda g, i: (l, g, 0, 0)),
                  pl.BlockSpec((1, c), lambda g, i: (0, g))],
        out_specs=pl.BlockSpec((tm, c), lambda g, i: (i, g)), out_shape=SDS((n, pw), MXU_DTYPE),
        compiler_params=_cparams("parallel", "parallel"), name=name)(p, wp, scale)


def _pool_proj_bwd(p, dcat, wp, l, scale, name):
    n, pw = p.shape
    ng, c = wp.shape[1], wp.shape[2]
    tm = _pick(n, _TM)

    def body(p_ref, dy_ref, w_ref, s_ref, dp_ref, ds_ref, dw_ref):
        i = pl.program_id(1)

        @pl.when(i == 0)
        def _():
            ds_ref[...] = jnp.zeros(ds_ref.shape, F32)
            dw_ref[...] = jnp.zeros(dw_ref.shape, F32)

        pv, wv, dy = p_ref[...], w_ref[...].astype(MXU_DTYPE), dy_ref[...]
        y = jnp.dot(pv, wv, preferred_element_type=F32)
        ds_ref[...] += jnp.sum(dy * y, axis=0, keepdims=True)
        dpw = (dy * s_ref[...]).astype(MXU_DTYPE)
        dp_ref[...] = lax.dot_general(dpw, wv, _DIMS["nt"], preferred_element_type=F32)
        dw_ref[0] += lax.dot_general(pv, dpw, _DIMS["tn"], preferred_element_type=F32)

    return pl.pallas_call(
        body, grid=(ng, n // tm),
        in_specs=[pl.BlockSpec((tm, c), lambda g, i: (i, g)), pl.BlockSpec((tm, c), lambda g, i: (i, g)),
                  pl.BlockSpec((None, None, c, c), lambda g, i: (l, g, 0, 0)), pl.BlockSpec((1, c), lambda g, i: (0, g))],
        out_specs=[pl.BlockSpec((tm, c), lambda g, i: (i, g)), pl.BlockSpec((1, c), lambda g, i: (0, g)),
                   pl.BlockSpec((1, c, c), lambda g, i: (g, 0, 0))],
        out_shape=[SDS((n, pw), F32), SDS((1, pw), F32), SDS((ng, c, c), F32)],
        compiler_params=_cparams("arbitrary", "arbitrary"), name=name)(p, dcat, wp, scale)


def _disc_math(a_re, a_im, logdt, b_re, b_im):
    dt = jnp.exp(logdt)
    mag = jnp.exp(a_re * dt)
    lam_re = mag * jnp.cos(a_im * dt)
    lam_im = mag * jnp.sin(a_im * dt)
    denom = a_re * a_re + a_im * a_im
    nr, ni = lam_re - 1.0, lam_im
    f_re = ((nr * a_re + ni * a_im) / denom)[:, None, :]
    f_im = ((ni * a_re - nr * a_im) / denom)[:, None, :]
    return lam_re, lam_im, f_re * b_re - f_im * b_im, f_re * b_im + f_im * b_re


def _disc_fwd(a_re, a_im, logdt, b_re, b_im, name):
    def body(ar, ai, ld, br, bi, o_lr, o_li, o_br, o_bi):
        lr, li, bbr, bbi = _disc_math(ar[...], ai[...], ld[...], br[...], bi[...])
        o_lr[...] = lr
        o_li[...] = li
        o_br[...] = bbr
        o_bi[...] = bbi

    return pl.pallas_call(
        body, out_shape=[SDS(a_re.shape, F32), SDS(a_re.shape, F32), SDS(b_re.shape, F32), SDS(b_re.shape, F32)],
        compiler_params=_cparams(), name=name)(a_re, a_im, logdt, b_re, b_im)


def _disc_bwd(a_re, a_im, logdt, b_re, b_im, d_lr, d_li, d_bbr, d_bbi, group, name):
    rows, gp = a_re.shape

    def body(ar, ai, ld, br, bi, g_lr, g_li, g_br, g_bi, o_ar, o_ai, o_ld, o_br, o_bi):
        _, vjp = jax.vjp(_disc_math, ar[...], ai[...], ld[...], br[...], bi[...])
        dar, dai, dld, dbr, dbi = vjp((g_lr[...], g_li[...], g_br[...], g_bi[...]))
        o_ar[...] = dar
        o_ai[...] = dai
        state = lax.broadcasted_iota(jnp.int32, (gp, LANES), 0)
        first = lax.broadcasted_iota(jnp.int32, (gp, LANES), 1) * group
        sel = jnp.logical_and(state >= first, state < first + group).astype(F32)
        o_ld[...] = jnp.dot(dld, sel, precision=HIGHEST, preferred_element_type=F32)
        o_br[...] = dbr
        o_bi[...] = dbi

    return pl.pallas_call(
        body, out_shape=[SDS(a_re.shape, F32), SDS(a_re.shape, F32), SDS((rows, LANES), F32),
                         SDS(b_re.shape, F32), SDS(b_re.shape, F32)],
        compiler_params=_cparams(), name=name)(a_re, a_im, logdt, b_re, b_im, d_lr, d_li, d_bbr, d_bbi)


def _scan_maps(nbc, nb):
    nbl = nb - nbc
    fwd0 = lambda i: (i, 0, 0)
    fwd1 = lambda i: (jnp.where(i < nbc, nbc - 1 - i, nb - 1 - (i - nbc)), 0, 0)
    adj0 = lambda i: (nb - 1 - i, 0, 0)
    adj1 = lambda i: (jnp.where(i < nbl, nbc + i, i - nbl), 0, 0)
    return fwd0, fwd1, adj0, adj1


def _scan_fwd(bu0, bu1, lam, n_ctx, name):
    n, s2, _ = bu0.shape
    s = s2 // 2
    tt = math.gcd(math.gcd(n_ctx, n - n_ctx), 128)
    nbc, nb = n_ctx // tt, n // tt
    fwd0, fwd1, _, _ = _scan_maps(nbc, nb)

    def body(b0_ref, b1_ref, lam_ref, h0_ref, h1_ref, st_ref):
        @pl.when(pl.program_id(0) == 0)
        def _():
            st_ref[...] = jnp.zeros(st_ref.shape, F32)

        lam = [(lam_ref[0], lam_ref[1]), (lam_ref[2], lam_ref[3])]
        lam2 = [(lr * lr - li * li, 2.0 * lr * li) for lr, li in lam]

        def pair(b_ref, h_ref, ra, rb, lm, lm2, hr, hi):
            (lr, li), (l2r, l2i) = lm, lm2
            bar, bai = b_ref[ra, 0:s, :], b_ref[ra, s:s2, :]
            cr = lr * bar - li * bai + b_ref[rb, 0:s, :]
            ci = lr * bai + li * bar + b_ref[rb, s:s2, :]
            h_ref[ra, 0:s, :] = lr * hr - li * hi + bar
            h_ref[ra, s:s2, :] = lr * hi + li * hr + bai
            nr = l2r * hr - l2i * hi + cr
            ni = l2r * hi + l2i * hr + ci
            h_ref[rb, 0:s, :] = nr
            h_ref[rb, s:s2, :] = ni
            return nr, ni

        def step(jj, carry):
            h0r, h0i, h1r, h1i = carry
            ja = 2 * jj
            ta = tt - 1 - ja
            h0r, h0i = pair(b0_ref, h0_ref, ja, ja + 1, lam[0], lam2[0], h0r, h0i)
            h1r, h1i = pair(b1_ref, h1_ref, ta, ta - 1, lam[1], lam2[1], h1r, h1i)
            return h0r, h0i, h1r, h1i

        out = lax.fori_loop(0, tt // 2, step, (st_ref[0], st_ref[1], st_ref[2], st_ref[3]), unroll=2)
        for q in range(4):
            st_ref[q] = out[q]

    blk = (tt, s2, LANES)
    return pl.pallas_call(
        body, grid=(nb,),
        in_specs=[pl.BlockSpec(blk, fwd0), pl.BlockSpec(blk, fwd1), pl.BlockSpec((4, s, LANES), lambda i: (0, 0, 0))],
        out_specs=[pl.BlockSpec(blk, fwd0), pl.BlockSpec(blk, fwd1)],
        out_shape=[SDS(bu0.shape, F32), SDS(bu1.shape, F32)],
        scratch_shapes=[pltpu.VMEM((4, s, LANES), F32)],
        compiler_params=_cparams("arbitrary"), name=name)(bu0, bu1, lam)


def _scan_bwd(dh0, dh1, h0, h1, lam, n_ctx, name):
    n, s2, _ = dh0.shape
    s = s2 // 2
    tt = math.gcd(math.gcd(n_ctx, n - n_ctx), 128)
    nbc, nb = n_ctx // tt, n // tt
    _, _, adj0, adj1 = _scan_maps(nbc, nb)

    def body(d0_ref, d1_ref, h0_ref, h1_ref, lam_ref, a0_ref, a1_ref, dl_ref, st_ref, acc_ref):
        i = pl.program_id(0)

        @pl.when(i == 0)
        def _():
            st_ref[...] = jnp.zeros(st_ref.shape, F32)
            acc_ref[...] = jnp.zeros(acc_ref.shape, F32)

        lam = [(lam_ref[0], lam_ref[1]), (lam_ref[2], lam_ref[3])]
        lam2 = [(lr * lr - li * li, 2.0 * lr * li) for lr, li in lam]

        def pair(d_ref, h_ref, a_ref, ra, rb, lm, lm2, ar, ai, cr, ci):
            (lr, li), (l2r, l2i) = lm, lm2
            dar, dai = d_ref[ra, 0:s, :], d_ref[ra, s:s2, :]
            er = lr * dar + li * dai + d_ref[rb, 0:s, :]
            ei = lr * dai - li * dar + d_ref[rb, s:s2, :]
            mr = lr * ar + li * ai + dar
            mi = lr * ai - li * ar + dai
            a_ref[ra, 0:s, :] = mr
            a_ref[ra, s:s2, :] = mi
            gar, gai = h_ref[ra, 0:s, :], h_ref[ra, s:s2, :]
            gbr, gbi = h_ref[rb, 0:s, :], h_ref[rb, s:s2, :]
            cr = cr + ((ar * gar + ai * gai) + (mr * gbr + mi * gbi))
            ci = ci + ((ai * gar - ar * gai) + (mi * gbr - mr * gbi))
            nr = l2r * ar + l2i * ai + er
            ni = l2r * ai - l2i * ar + ei
            a_ref[rb, 0:s, :] = nr
            a_ref[rb, s:s2, :] = ni
            return nr, ni, cr, ci

        def step(jj, carry):
            a0r, a0i, a1r, a1i, c0r, c0i, c1r, c1i = carry
            ja = 2 * jj
            ta = tt - 1 - ja
            a0r, a0i, c0r, c0i = pair(d0_ref, h0_ref, a0_ref, ta, ta - 1, lam[0], lam2[0], a0r, a0i, c0r, c0i)
            a1r, a1i, c1r, c1i = pair(d1_ref, h1_ref, a1_ref, ja, ja + 1, lam[1], lam2[1], a1r, a1i, c1r, c1i)
            return a0r, a0i, a1r, a1i, c0r, c0i, c1r, c1i

        init = tuple(st_ref[q] for q in range(4)) + tuple(acc_ref[q] for q in range(4))
        out = lax.fori_loop(0, tt // 2, step, init, unroll=2)
        for q in range(4):
            st_ref[q] = out[q]
            acc_ref[q] = out[4 + q]

        @pl.when(i == nb - 1)
        def _():
            for q in range(4):
                dl_ref[q] = out[4 + q]

    blk = (tt, s2, LANES)
    small = pl.BlockSpec((4, s, LANES), lambda i: (0, 0, 0))
    return pl.pallas_call(
        body, grid=(nb,),
        in_specs=[pl.BlockSpec(blk, adj0), pl.BlockSpec(blk, adj1), pl.BlockSpec(blk, adj0), pl.BlockSpec(blk, adj1), small],
        out_specs=[pl.BlockSpec(blk, adj0), pl.BlockSpec(blk, adj1), small],
        out_shape=[SDS(dh0.shape, F32), SDS(dh1.shape, F32), SDS((4, s, LANES), F32)],
        scratch_shapes=[pltpu.VMEM((4, s, LANES), F32), pltpu.VMEM((4, s, LANES), F32)],
        compiler_params=_cparams("arbitrary"), name=name)(dh0, dh1, h0, h1, lam)


def _gelu(v):
    th = jnp.tanh(GELU_C0 * (v + GELU_C1 * v * v * v))
    return 0.5 * v * (1.0 + th), th


def _ssm_head_fwd(y, u, ssm_d, wg, l, name):
    n, sw = y.shape
    ucol = u.shape[1] // sw - 1
    tm = _pick(n, _TM)

    def body(y_ref, u_ref, d_ref, w_ref, o_ref):
        act, _ = _gelu(y_ref[...] + d_ref[...] * u_ref[...])
        q = jnp.dot(act.astype(MXU_DTYPE), w_ref[...].astype(MXU_DTYPE), preferred_element_type=F32)
        o_ref[...] = (act * jax.nn.sigmoid(q)).astype(o_ref.dtype)

    row = pl.BlockSpec((tm, sw), lambda i: (i, 0))
    return pl.pallas_call(
        body, grid=(n // tm,),
        in_specs=[row, pl.BlockSpec((tm, sw), lambda i: (i, ucol)), pl.BlockSpec((1, sw), lambda i: (0, 0)),
                  pl.BlockSpec((None, sw, sw), lambda i: (l, 0, 0))],
        out_specs=row, out_shape=SDS((n, sw), MXU_DTYPE), compiler_params=_cparams("parallel"), name=name)(y, u, ssm_d, wg)


def _ssm_head_bwd(dcat, y, u, ssm_d, wg, l, name):
    n, sw = y.shape
    ucol = u.shape[1] // sw - 1
    tm = _pick(n, _TM)

    def body(do_ref, y_ref, u_ref, d_ref, w_ref, dy_ref, du_ref, act_ref, dq_ref, dd_ref):
        @pl.when(pl.program_id(0) == 0)
        def _():
            dd_ref[...] = jnp.zeros(dd_ref.shape, F32)

        uv, dv, do = u_ref[...], d_ref[...], do_ref[...]
        yf = y_ref[...] + dv * uv
        act, th = _gelu(yf)
        wv = w_ref[...].astype(MXU_DTYPE)
        sg = jax.nn.sigmoid(jnp.dot(act.astype(MXU_DTYPE), wv, preferred_element_type=F32))
        dq = (do * act * sg * (1.0 - sg)).astype(MXU_DTYPE)
        dact = do * sg + lax.dot_general(dq, wv, _DIMS["nt"], preferred_element_type=F32)
        dgelu = 0.5 * (1.0 + th) + 0.5 * yf * (1.0 - th * th) * GELU_C0 * (1.0 + 3.0 * GELU_C1 * yf * yf)
        dyf = dact * dgelu
        dy_ref[...] = dyf.astype(dy_ref.dtype)
        du_ref[...] = dyf * dv
        act_ref[...] = act.astype(act_ref.dtype)
        dq_ref[...] = dq
        dd_ref[...] += jnp.sum(dyf * uv, axis=0, keepdims=True)

    row = pl.BlockSpec((tm, sw), lambda i: (i, 0))
    last = pl.BlockSpec((tm, sw), lambda i: (i, ucol))
    vec = pl.BlockSpec((1, sw), lambda i: (0, 0))
    return pl.pallas_call(
        body, grid=(n // tm,),
        in_specs=[last, row, last, vec, pl.BlockSpec((None, sw, sw), lambda i: (l, 0, 0))],
        out_specs=[row, row, row, row, vec],
        out_shape=[SDS((n, sw), MXU_DTYPE), SDS((n, sw), F32), SDS((n, sw), MXU_DTYPE), SDS((n, sw), MXU_DTYPE),
                   SDS((1, sw), F32)],
        compiler_params=_cparams("arbitrary"), name=name)(dcat, y, u, ssm_d, wg)


def _assemble_du(du_pool, du_dir, du_proj, name):
    n, pw = du_pool.shape
    sw = du_dir.shape[1]
    tm = _pick(n, _TM)

    def body(p_ref, a_ref, b_ref, o_ref):
        o_ref[:, 0:pw] = p_ref[...].astype(o_ref.dtype)
        o_ref[:, pw:pw + sw] = (a_ref[...] + b_ref[...]).astype(o_ref.dtype)

    return pl.pallas_call(
        body, grid=(n // tm,),
        in_specs=[pl.BlockSpec((tm, pw), lambda i: (i, 0)), pl.BlockSpec((tm, sw), lambda i: (i, 0)),
                  pl.BlockSpec((tm, sw), lambda i: (i, 0))],
        out_specs=pl.BlockSpec((tm, pw + sw), lambda i: (i, 0)), out_shape=SDS((n, pw + sw), MXU_DTYPE),
        compiler_params=_cparams("parallel"), name=name)(du_pool, du_dir, du_proj)


CONV_PAD = GRID_W + SUBLANES


def _conv_layout(n, n_ctx):
    return CONV_PAD, 2 * CONV_PAD + n_ctx, 3 * CONV_PAD + n


def _col_masks(ch):
    col = lax.broadcasted_iota(jnp.int32, (ch, LANES), 0) % GRID_W
    return col != 0, col != GRID_W - 1


def _fill_padded(scr, src_ref, n, n_ctx):
    base_c, base_l, total = _conv_layout(n, n_ctx)
    for base in (0, base_c + n_ctx, base_l + n - n_ctx):
        scr[pl.ds(base, CONV_PAD), :] = jnp.zeros((CONV_PAD, LANES), F32)
    for row0, nseg, base in ((0, n_ctx, base_c), (n_ctx, n - n_ctx, base_l)):
        ch = math.gcd(nseg, 512)

        def copy(ci, carry, row0=row0, base=base, ch=ch):
            c0 = pl.multiple_of(ci * ch, ch)
            scr[pl.ds(base + c0, ch), :] = src_ref[pl.ds(row0 + c0, ch), :]
            return carry

        lax.fori_loop(0, nseg // ch, copy, 0)


def _conv_ctx(scr, k_ref, base, c0, ch, sign):
    acc = scr[pl.ds(c0 + base, ch), :] * k_ref[4:5, :]
    acc = acc + scr[pl.ds(c0 + (base - sign), ch), :] * k_ref[3:4, :]
    return acc + scr[pl.ds(c0 + (base + sign), ch), :] * k_ref[5:6, :]


def _conv_lat(scr, k_ref, base, c0, ch, sign, m_l, m_r):
    cols = []
    for j in range(3):
        acc = None
        for i in range(3):
            off = sign * (GRID_W * (i - 1) + (j - 1))
            term = scr[pl.ds(c0 + (base + off), ch), :] * k_ref[3 * i + j:3 * i + j + 1, :]
            acc = term if acc is None else acc + term
        cols.append(acc)
    first, last = (m_l, m_r) if sign > 0 else (m_r, m_l)
    return cols[1] + jnp.where(first, cols[0], 0.0) + jnp.where(last, cols[2], 0.0)


def _conv_chunk(n_lat):
    return math.gcd(n_lat, 256)


def _conv_glu_fwd(z, wk, n_ctx, name):
    n, f2 = z.shape
    dff = f2 // 2
    nvt = dff // LANES
    n_lat = n - n_ctx
    base_c, base_l, total = _conv_layout(n, n_ctx)
    ch = _conv_chunk(n_lat)
    assert ch % GRID_W == 0

    def body(zv_ref, zg_ref, kv_ref, kg_ref, a_ref, cv_ref, cg_ref, sv, sg):
        _fill_padded(sv, zv_ref, n, n_ctx)
        _fill_padded(sg, zg_ref, n, n_ctx)

        def emit(cv, cg, row, rows):
            cv_ref[pl.ds(row, rows), :] = cv
            cg_ref[pl.ds(row, rows), :] = cg
            a_ref[pl.ds(row, rows), :] = (cv * cg * jax.nn.sigmoid(cg)).astype(a_ref.dtype)

        emit(_conv_ctx(sv, kv_ref, base_c, 0, n_ctx, 1), _conv_ctx(sg, kg_ref, base_c, 0, n_ctx, 1), 0, n_ctx)
        m_l, m_r = _col_masks(ch)

        def lat(ci, carry):
            c0 = pl.multiple_of(ci * ch, ch)
            emit(_conv_lat(sv, kv_ref, base_l, c0, ch, 1, m_l, m_r), _conv_lat(sg, kg_ref, base_l, c0, ch, 1, m_l, m_r),
                 n_ctx + c0, ch)
            return carry

        lax.fori_loop(0, n_lat // ch, lat, 0)

    col = lambda shift: pl.BlockSpec((n, LANES), lambda j: (0, j + shift))
    kcol = lambda shift: pl.BlockSpec((9, LANES), lambda j: (0, j + shift))
    return pl.pallas_call(
        body, grid=(nvt,), in_specs=[col(0), col(nvt), kcol(0), kcol(nvt)], out_specs=[col(0), col(0), col(0)],
        out_shape=[SDS((n, dff), MXU_DTYPE), SDS((n, dff), F32), SDS((n, dff), F32)],
        scratch_shapes=[pltpu.VMEM((total, LANES), F32), pltpu.VMEM((total, LANES), F32)],
        compiler_params=_cparams("parallel"), name=name)(z, z, wk, wk)


def _conv_glu_bwd(z, cv, cg, da, wk, n_ctx, name):
    n, f2 = z.shape
    dff = f2 // 2
    nvt = dff // LANES
    n_lat = n - n_ctx
    base_c, base_l, total = _conv_layout(n, n_ctx)
    ch = _conv_chunk(n_lat)
    assert ch % GRID_W == 0
    ctx_taps = [(1, 0), (1, 1), (1, 2)]
    lat_taps = [(i, j) for i in range(3) for j in range(3)]

    def tap_sums(acc, scr, d, base, c0, rows, taps, masks):
        acc = list(acc)
        by_col = (d, d, d) if masks is None else (jnp.where(masks[0], d, 0.0), d, jnp.where(masks[1], d, 0.0))
        for i, j in taps:
            src = scr[pl.ds(c0 + (base + GRID_W * (i - 1) + (j - 1)), rows), :]
            acc[3 * i + j] = acc[3 * i + j] + jnp.sum((src * by_col[j]).reshape(rows // SUBLANES, SUBLANES, LANES), axis=0)
        return acc

    def body(zv_ref, zg_ref, cv_ref, cg_ref, da_ref, kv_ref, kg_ref, dz_ref, dkv_ref, dkg_ref, a_ref, sv, sg, dv, dg):
        _fill_padded(sv, zv_ref, n, n_ctx)
        _fill_padded(sg, zg_ref, n, n_ctx)
        for base in (0, base_c + n_ctx, base_l + n_lat):
            dv[pl.ds(base, CONV_PAD), :] = jnp.zeros((CONV_PAD, LANES), F32)
            dg[pl.ds(base, CONV_PAD), :] = jnp.zeros((CONV_PAD, LANES), F32)
        m_l, m_r = _col_masks(ch)

        def first_pass(row, pad_row, rows):
            cv, cg = cv_ref[pl.ds(row, rows), :], cg_ref[pl.ds(row, rows), :]
            sig = jax.nn.sigmoid(cg)
            silu = cg * sig
            a_ref[pl.ds(row, rows), :] = (cv * silu).astype(a_ref.dtype)
            dav = da_ref[pl.ds(row, rows), :]
            dcv = dav * silu
            dcg = dav * cv * (sig * (1.0 + cg * (1.0 - sig)))
            dv[pl.ds(pad_row, rows), :] = dcv
            dg[pl.ds(pad_row, rows), :] = dcg
            return dcv, dcg

        zero = [jnp.zeros((SUBLANES, LANES), F32) for _ in range(9)]
        dcv, dcg = first_pass(0, base_c, n_ctx)
        accv = tap_sums(zero, sv, dcv, base_c, 0, n_ctx, ctx_taps, None)
        accg = tap_sums(zero, sg, dcg, base_c, 0, n_ctx, ctx_taps, None)

        def lat1(ci, carry):
            accv, accg = carry
            c0 = pl.multiple_of(ci * ch, ch)
            dcv, dcg = first_pass(n_ctx + c0, base_l + c0, ch)
            accv = tap_sums(accv, sv, dcv, base_l, c0, ch, lat_taps, (m_l, m_r))
            accg = tap_sums(accg, sg, dcg, base_l, c0, ch, lat_taps, (m_l, m_r))
            return tuple(accv), tuple(accg)

        accv, accg = lax.fori_loop(0, n_lat // ch, lat1, (tuple(accv), tuple(accg)))
        for t in range(9):
            dkv_ref[t:t + 1, :] = jnp.sum(accv[t], axis=0, keepdims=True)
            dkg_ref[t:t + 1, :] = jnp.sum(accg[t], axis=0, keepdims=True)

        dz_ref[0, pl.ds(0, n_ctx), :] = _conv_ctx(dv, kv_ref, base_c, 0, n_ctx, -1).astype(dz_ref.dtype)
        dz_ref[1, pl.ds(0, n_ctx), :] = _conv_ctx(dg, kg_ref, base_c, 0, n_ctx, -1).astype(dz_ref.dtype)

        def lat2(ci, carry):
            c0 = pl.multiple_of(ci * ch, ch)
            dz_ref[0, pl.ds(n_ctx + c0, ch), :] = _conv_lat(dv, kv_ref, base_l, c0, ch, -1, m_l, m_r).astype(dz_ref.dtype)
            dz_ref[1, pl.ds(n_ctx + c0, ch), :] = _conv_lat(dg, kg_ref, base_l, c0, ch, -1, m_l, m_r).astype(dz_ref.dtype)
            return carry

        lax.fori_loop(0, n_lat // ch, lat2, 0)

    col = lambda shift: pl.BlockSpec((n, LANES), lambda j: (0, j + shift))
    kcol = lambda shift: pl.BlockSpec((9, LANES), lambda j: (0, j + shift))
    pad = pltpu.VMEM((total, LANES), F32)
    return pl.pallas_call(
        body, grid=(nvt,), in_specs=[col(0), col(nvt), col(0), col(0), col(0), kcol(0), kcol(nvt)],
        out_specs=[pl.BlockSpec((2, n, LANES), lambda j: (0, 0, j)), kcol(0), kcol(0), col(0)],
        out_shape=[SDS((2, n, dff), MXU_DTYPE), SDS((9, dff), F32), SDS((9, dff), F32), SDS((n, dff), MXU_DTYPE)],
        scratch_shapes=[pad, pad, pad, pad],
        compiler_params=_cparams("parallel"), name=name)(z, z, cv, cg, da, wk, wk)


def _silu(v):
    return v * jax.nn.sigmoid(v)


def _ada_fwd(cond, w_ada, b_shard, name):
    nl, d, cols = w_ada.shape
    tn = _pick(cols, (512, 256, 128))

    def body(c_ref, w_ref, b_ref, o_ref):
        o_ref[...] = jnp.dot(_silu(c_ref[...]), w_ref[...], precision=HIGHEST, preferred_element_type=F32) + b_ref[...]

    return pl.pallas_call(
        body, grid=(nl, cols // tn),
        in_specs=[pl.BlockSpec(cond.shape, lambda l, j: (0, 0)), pl.BlockSpec((None, d, tn), lambda l, j: (l, 0, j)),
                  pl.BlockSpec((None, 1, tn), lambda l, j: (l, 0, j))],
        out_specs=pl.BlockSpec((None, cond.shape[0], tn), lambda l, j: (l, 0, j)),
        out_shape=SDS((nl, cond.shape[0], cols), F32),
        compiler_params=_cparams("parallel", "parallel"), name=name)(cond, w_ada, b_shard)


def _ada_dw(cond, dmod, name):
    nl, rows, cols = dmod.shape
    d = cond.shape[1]
    tn = _pick(cols, (512, 256, 128))

    def body(c_ref, g_ref, o_ref):
        o_ref[...] = lax.dot_general(_silu(c_ref[...]), g_ref[...], _DIMS["tn"], precision=HIGHEST,
                                     preferred_element_type=F32)

    return pl.pallas_call(
        body, grid=(nl, cols // tn),
        in_specs=[pl.BlockSpec(cond.shape, lambda l, j: (0, 0)), pl.BlockSpec((None, rows, tn), lambda l, j: (l, 0, j))],
        out_specs=pl.BlockSpec((None, d, tn), lambda l, j: (l, 0, j)), out_shape=SDS((nl, d, cols), F32),
        compiler_params=_cparams("parallel", "parallel"), name=name)(cond, dmod)


def _ada_dcond(dmod, w_ada, name):
    nl, rows, cols = dmod.shape
    d = w_ada.shape[1]
    tn = _pick(cols, (512, 256, 128))

    def body(g_ref, w_ref, o_ref):
        @pl.when(jnp.logical_and(pl.program_id(0) == 0, pl.program_id(1) == 0))
        def _():
            o_ref[...] = jnp.zeros(o_ref.shape, F32)

        o_ref[...] += lax.dot_general(g_ref[...], w_ref[...], _DIMS["nt"], precision=HIGHEST, preferred_element_type=F32)

    return pl.pallas_call(
        body, grid=(nl, cols // tn),
        in_specs=[pl.BlockSpec((None, rows, tn), lambda l, j: (l, 0, j)), pl.BlockSpec((None, d, tn), lambda l, j: (l, 0, j))],
        out_specs=pl.BlockSpec((rows, d), lambda l, j: (0, 0)), out_shape=SDS((rows, d), F32),
        compiler_params=_cparams("arbitrary", "arbitrary"), name=name)(dmod, w_ada)


def _ada_rows(dmod_all, name):
    nd, nl, _, w = dmod_all.shape
    tn = _pick(w, (2048, 1024, 512, 256, 128))

    def body(g_ref, rows_ref, db_ref):
        ctx = g_ref[0, 0, 0:1, :]
        for b in range(1, nd):
            ctx = ctx + g_ref[b, 0, 0:1, :]
        total = ctx
        for b in range(nd):
            lat = g_ref[b, 0, 1:2, :]
            rows_ref[b:b + 1, :] = lat
            total = total + lat
        rows_ref[nd:nd + 1, :] = ctx
        rows_ref[nd + 1:16, :] = jnp.zeros((16 - nd - 1, tn), F32)
        db_ref[...] = total

    return pl.pallas_call(
        body, grid=(nl, w // tn),
        in_specs=[pl.BlockSpec((nd, 1, 2, tn), lambda l, j: (0, l, 0, j))],
        out_specs=[pl.BlockSpec((None, 16, tn), lambda l, j: (l, 0, j)), pl.BlockSpec((None, 1, tn), lambda l, j: (l, 0, j))],
        out_shape=[SDS((nl, 16, w), F32), SDS((nl, 1, w), F32)],
        compiler_params=_cparams("parallel", "parallel"), name=name)(dmod_all)


def _ada_dctx(parts, c_ctx, row, name):
    def body(p_ref, c_ref, o_ref):
        ds = p_ref[0, row:row + 1, :]
        for k in range(1, p_ref.shape[0]):
            ds = ds + p_ref[k, row:row + 1, :]
        cv = c_ref[...]
        sg = jax.nn.sigmoid(cv)
        o_ref[...] = ds * (sg * (1.0 + cv * (1.0 - sg)))

    return pl.pallas_call(body, out_shape=SDS(c_ctx.shape, F32), compiler_params=_cparams(), name=name)(parts, c_ctx)


ROW_BLOCK_BYTES = 1 << 20


def _as_rows(shape):
    size = math.prod(shape)
    cols = shape[-1] if len(shape) >= 2 and shape[-1] % LANES == 0 else _pick(size, (1024, 512, 256, 128))
    rows = size // cols
    fits = [t for t in (512, 256, 128, 64, 32, 16, 8) if t * cols * 4 <= ROW_BLOCK_BYTES]
    return rows, cols, _pick(rows, fits)


def _adamw(w, g, m, v, name):
    rows, cols, tr = _as_rows(w.shape)
    c1 = 1.0 / (1.0 - ADAM_B1 ** ADAM_STEP)
    c2 = 1.0 / (1.0 - ADAM_B2 ** ADAM_STEP)

    def body(w_ref, g_ref, m_ref, v_ref, d_ref, nm_ref, nv_ref):
        gv = g_ref[...]
        nm = ADAM_B1 * m_ref[...] + (1.0 - ADAM_B1) * gv
        nv = ADAM_B2 * v_ref[...] + (1.0 - ADAM_B2) * (gv * gv)
        nm_ref[...] = nm
        nv_ref[...] = nv
        d_ref[...] = -ADAM_LR * ((nm * c1) / (jnp.sqrt(nv * c2) + ADAM_EPS) + ADAM_WD * w_ref[...])

    blk = pl.BlockSpec((tr, cols), lambda i: (i, 0))
    outs = pl.pallas_call(
        body, grid=(rows // tr,), in_specs=[blk] * 4, out_specs=[blk] * 3, out_shape=[SDS((rows, cols), F32)] * 3,
        compiler_params=_cparams("parallel"), name=name)(*[t.reshape(rows, cols) for t in (w, g, m, v)])
    return tuple(o.reshape(w.shape) for o in outs)


def _tile_rows(rows, cols, dtype):
    size = jnp.dtype(dtype).itemsize
    fits = [t for t in (512, 256, 128, 64, 32, 16, 8) if t * cols * size <= ROW_BLOCK_BYTES and t * size >= 32]
    return _pick(rows, fits)


def _scalar_spec(grid, in_specs, out_specs):
    return pltpu.PrefetchScalarGridSpec(num_scalar_prefetch=1, grid=grid, in_specs=in_specs, out_specs=out_specs)


def _place_chunk(shard, k_idx, dtype, name):
    nl, rows, cols = shard.shape
    tr = _tile_rows(rows, cols, dtype)

    def body(k_ref, s_ref, o_ref):
        o_ref[...] = s_ref[...].astype(o_ref.dtype)

    return pl.pallas_call(
        body, out_shape=SDS((nl, N_CHIPS, rows, cols), dtype),
        grid_spec=_scalar_spec((nl, rows // tr), [pl.BlockSpec((None, tr, cols), lambda l, i, k: (l, i, 0))],
                               pl.BlockSpec((None, None, tr, cols), lambda l, i, k: (l, k[0], i, 0))),
        compiler_params=_cparams("parallel", "parallel"), name=name)(k_idx, shard)


def _pair_sum(grads, recv, c_idx, name):
    half, nch, rows, cols = recv.shape
    tr = _tile_rows(rows, cols, recv.dtype)

    def body(c_ref, g_ref, r_ref, o_ref):
        o_ref[...] = (g_ref[...].astype(F32) + r_ref[...].astype(F32)).astype(o_ref.dtype)

    blk = pl.BlockSpec((None, None, tr, cols), lambda h, q, i, c: (h, q, i, 0))
    return pl.pallas_call(
        body, out_shape=SDS(recv.shape, recv.dtype),
        grid_spec=_scalar_spec((half, nch, rows // tr),
                               [pl.BlockSpec((None, None, tr, cols), lambda h, q, i, c: (c[0] * half + h, q, i, 0)), blk], blk),
        compiler_params=_cparams("parallel", "parallel", "parallel"), name=name)(c_idx, grads, recv)


def _chip_sum(parts, recv, kc_idx, name):
    half, nch, rows, cols = parts.shape
    tr = _tile_rows(rows, cols, F32)

    def body(kc_ref, p_ref, r_ref, o_ref):
        acc = p_ref[...].astype(F32)
        for s in range(r_ref.shape[0]):
            acc = acc + r_ref[s].astype(F32)
        o_ref[...] = acc

    return pl.pallas_call(
        body, out_shape=SDS((2 * half, rows, cols), F32),
        grid_spec=_scalar_spec((half, rows // tr),
                               [pl.BlockSpec((None, None, tr, cols), lambda h, i, kc: (h, kc[0], i, 0)),
                                pl.BlockSpec((nch - 1, None, tr, cols), lambda h, i, kc: (0, h, i, 0))],
                               pl.BlockSpec((None, tr, cols), lambda h, i, kc: (kc[1] * half + h, i, 0))),
        compiler_params=_cparams("parallel", "parallel"), name=name)(kc_idx, parts, recv)


PIECE_BYTES = 3 << 20
MAX_PIECES = 16
PIECE_ROW_ALIGN = 16


def _coords():
    return lax.axis_index("x"), lax.axis_index("y"), lax.axis_index("c")


def _other_chips(x, y):
    return [(1 - x, y), (x, 1 - y), (1 - x, 1 - y)]


def _row_pieces(rows, nbytes):
    pieces = 1
    while (pieces < MAX_PIECES and nbytes // pieces > PIECE_BYTES and rows % (2 * pieces * PIECE_ROW_ALIGN) == 0):
        pieces *= 2
    step = rows // pieces
    return [pl.ds(i * step, step) for i in range(pieces)]


def _nbytes(shape, dtype):
    return math.prod(shape) * jnp.dtype(dtype).itemsize


def _offsets(counts):
    out, pos = [], 0
    for cnt in counts:
        out.append(pos)
        pos += cnt
    return out, pos


def _gather_chips(placed, name):
    nt = len(placed)
    half = [p.shape[0] // 2 for p in placed]
    pieces = [_row_pieces(p.shape[2], _nbytes((h,) + p.shape[2:], p.dtype)) for p, h in zip(placed, half)]
    base, total = _offsets([len(p) for p in pieces])

    def body(*refs):
        o_refs = refs[nt:2 * nt]
        s_nbr, r_nbr, s_fwd, r_fwd, s_sib, r_sib = refs[2 * nt:]
        x, y, c = _coords()
        k, kx, ky, kd = 2 * x + y, 2 * (1 - x) + y, 2 * x + (1 - y), 2 * (1 - x) + (1 - y)
        across_x, across_y, sibling = (1 - x, y, c), (x, 1 - y, c), (x, y, 1 - c)
        sends = []

        def copy(o_ref, rows, slot, rs, ssem, rsem, q, to):
            return pltpu.make_async_remote_copy(
                src_ref=o_ref.at[rows, slot, rs], dst_ref=o_ref.at[rows, slot, rs], send_sem=ssem.at[q], recv_sem=rsem.at[q],
                device_id=to, device_id_type=MESH)

        def start(cp):
            cp.start()
            sends.append(cp)

        work = [(t, i, rs, base[t] + i, 2 * i < len(pieces[t]) or len(pieces[t]) == 1)
                for t in range(nt) for i, rs in enumerate(pieces[t])]
        for t, i, rs, q, _ in work:
            mine = pl.ds(c * half[t], half[t])
            start(copy(o_refs[t], mine, k, rs, s_nbr, r_nbr, 2 * q, across_x))
            start(copy(o_refs[t], mine, k, rs, s_nbr, r_nbr, 2 * q + 1, across_y))
        for t, i, rs, q, via_x in work:
            mine = pl.ds(c * half[t], half[t])
            copy(o_refs[t], mine, kx, rs, s_nbr, r_nbr, 2 * q, across_x).wait_recv()
            start(copy(o_refs[t], mine, kx, rs, s_sib, r_sib, 3 * q, sibling))
            if not via_x:
                start(copy(o_refs[t], mine, kx, rs, s_fwd, r_fwd, q, across_y))
            copy(o_refs[t], mine, ky, rs, s_nbr, r_nbr, 2 * q + 1, across_y).wait_recv()
            start(copy(o_refs[t], mine, ky, rs, s_sib, r_sib, 3 * q + 1, sibling))
            if via_x:
                start(copy(o_refs[t], mine, ky, rs, s_fwd, r_fwd, q, across_x))
        for t, i, rs, q, via_x in work:
            mine = pl.ds(c * half[t], half[t])
            copy(o_refs[t], mine, kd, rs, s_fwd, r_fwd, q, across_x if via_x else across_y).wait_recv()
            start(copy(o_refs[t], mine, kd, rs, s_sib, r_sib, 3 * q + 2, sibling))
        for t, i, rs, q, _ in work:
            theirs = pl.ds((1 - c) * half[t], half[t])
            for r, slot in enumerate((kx, ky, kd)):
                copy(o_refs[t], theirs, slot, rs, s_sib, r_sib, 3 * q + r, sibling).wait_recv()
        for cp in sends:
            cp.wait_send()

    sem = pltpu.SemaphoreType.DMA
    outs = pl.pallas_call(
        body, in_specs=[ANY] * nt, out_specs=[ANY] * nt,
        out_shape=[SDS(p.shape, p.dtype) for p in placed],
        input_output_aliases={t: t for t in range(nt)},
        scratch_shapes=[sem((2 * total,)), sem((2 * total,)), sem((total,)), sem((total,)), sem((3 * total,)),
                        sem((3 * total,))],
        name=name)(*placed)
    return list(outs)


def _pair_send(grads, name):
    nt = len(grads)
    half = [g.shape[0] // 2 for g in grads]
    pieces = [_row_pieces(g.shape[2], _nbytes((h,) + g.shape[1:], g.dtype)) for g, h in zip(grads, half)]
    base, total = _offsets([len(p) for p in pieces])

    def body(*refs):
        g_refs, o_refs = refs[:nt], refs[nt:2 * nt]
        ssem, rsem = refs[2 * nt:]
        x, y, c = _coords()
        cps = []
        for t in range(nt):
            theirs = pl.ds((1 - c) * half[t], half[t])
            for i, rs in enumerate(pieces[t]):
                q = base[t] + i
                cp = pltpu.make_async_remote_copy(
                    src_ref=g_refs[t].at[theirs, :, rs], dst_ref=o_refs[t].at[:, :, rs], send_sem=ssem.at[q],
                    recv_sem=rsem.at[q], device_id=(x, y, 1 - c), device_id_type=MESH)
                cp.start()
                cps.append(cp)
        for cp in cps:
            cp.wait_recv()
        for cp in cps:
            cp.wait_send()

    sem = pltpu.SemaphoreType.DMA
    outs = pl.pallas_call(
        body, in_specs=[ANY] * nt, out_specs=[ANY] * nt,
        out_shape=[SDS((g.shape[0] // 2,) + g.shape[1:], g.dtype) for g in grads],
        scratch_shapes=[sem((total,)), sem((total,))],
        name=name)(*grads)
    return list(outs)


def _chip_send(parts, name):
    nt = len(parts)
    pieces = [_row_pieces(p.shape[2], _nbytes((p.shape[0],) + p.shape[2:], p.dtype)) for p in parts]
    base, total = _offsets([len(p) for p in pieces])

    def body(*refs):
        p_refs, o_refs = refs[:nt], refs[nt:2 * nt]
        ssem, rsem = refs[2 * nt:]
        x, y, c = _coords()
        cps = []
        for t in range(nt):
            for i, rs in enumerate(pieces[t]):
                for r, (px, py) in enumerate(_other_chips(x, y)):
                    q = 3 * (base[t] + i) + r
                    cp = pltpu.make_async_remote_copy(
                        src_ref=p_refs[t].at[:, 2 * px + py, rs], dst_ref=o_refs[t].at[r, :, rs], send_sem=ssem.at[q],
                        recv_sem=rsem.at[q], device_id=(px, py, c), device_id_type=MESH)
                    cp.start()
                    cps.append(cp)
        for cp in cps:
            cp.wait_recv()
        for cp in cps:
            cp.wait_send()

    sem = pltpu.SemaphoreType.DMA
    outs = pl.pallas_call(
        body, in_specs=[ANY] * nt, out_specs=[ANY] * nt,
        out_shape=[SDS((N_CHIPS - 1, p.shape[0]) + p.shape[2:], p.dtype) for p in parts],
        scratch_shapes=[sem((3 * total,)), sem((3 * total,))],
        name=name)(*parts)
    return list(outs)


def _pair_join(bufs, name):
    nt = len(bufs)
    half = [b.shape[0] // 2 for b in bufs]
    pieces = [_row_pieces(b.shape[1], _nbytes((h,) + b.shape[1:], b.dtype)) for b, h in zip(bufs, half)]
    base, total = _offsets([len(p) for p in pieces])

    def body(*refs):
        o_refs = refs[nt:2 * nt]
        ssem, rsem = refs[2 * nt:]
        x, y, c = _coords()
        cps = []
        for t in range(nt):
            mine = pl.ds(c * half[t], half[t])
            for i, rs in enumerate(pieces[t]):
                q = base[t] + i
                cp = pltpu.make_async_remote_copy(
                    src_ref=o_refs[t].at[mine, rs], dst_ref=o_refs[t].at[mine, rs], send_sem=ssem.at[q],
                    recv_sem=rsem.at[q], device_id=(x, y, 1 - c), device_id_type=MESH)
                cp.start()
                cps.append(cp)
        for t in range(nt):
            theirs = pl.ds((1 - c) * half[t], half[t])
            for i, rs in enumerate(pieces[t]):
                q = base[t] + i
                pltpu.make_async_remote_copy(
                    src_ref=o_refs[t].at[theirs, rs], dst_ref=o_refs[t].at[theirs, rs], send_sem=ssem.at[q],
                    recv_sem=rsem.at[q], device_id=(x, y, 1 - c), device_id_type=MESH).wait_recv()
        for cp in cps:
            cp.wait_send()

    sem = pltpu.SemaphoreType.DMA
    outs = pl.pallas_call(
        body, in_specs=[ANY] * nt, out_specs=[ANY] * nt,
        out_shape=[SDS(b.shape, b.dtype) for b in bufs],
        input_output_aliases={t: t for t in range(nt)},
        scratch_shapes=[sem((total,)), sem((total,))],
        name=name)(*bufs)
    return list(outs)


def _gather_devices(vals, name):
    nt = len(vals)
    flips = [(a, b, e) for a in (0, 1) for b in (0, 1) for e in (0, 1)][1:]

    def body(*refs):
        v_refs, o_refs = refs[:nt], refs[nt:2 * nt]
        lsem, ssem, rsem = refs[2 * nt:]
        x, y, c = _coords()
        me = 4 * x + 2 * y + c
        peers = [((1 - x) if a else x, (1 - y) if b else y, (1 - c) if e else c) for a, b, e in flips]
        cps = []
        for t in range(nt):
            loc = pltpu.make_async_copy(v_refs[t], o_refs[t].at[me], lsem.at[t])
            loc.start()
            cps.append(loc)
            for r, peer in enumerate(peers):
                cp = pltpu.make_async_remote_copy(
                    src_ref=v_refs[t], dst_ref=o_refs[t].at[me], send_sem=ssem.at[7 * t + r],
                    recv_sem=rsem.at[7 * t + r], device_id=peer, device_id_type=MESH)
                cp.start()
                cps.append(cp)
        for t in range(nt):
            for r, (px, py, pc) in enumerate(peers):
                pltpu.make_async_remote_copy(
                    src_ref=v_refs[t], dst_ref=o_refs[t].at[4 * px + 2 * py + pc], send_sem=ssem.at[7 * t + r],
                    recv_sem=rsem.at[7 * t + r], device_id=(px, py, pc), device_id_type=MESH).wait_recv()
        for t in range(nt):
            cps[8 * t].wait()
            for r in range(7):
                cps[8 * t + 1 + r].wait_send()

    sem = pltpu.SemaphoreType.DMA
    outs = pl.pallas_call(
        body, in_specs=[ANY] * nt, out_specs=[ANY] * nt,
        out_shape=[SDS((N_DEV,) + v.shape, v.dtype) for v in vals],
        scratch_shapes=[sem((nt,)), sem((7 * nt,)), sem((7 * nt,))],
        name=name)(*vals)
    return list(outs)


def _gather_all(shards, dtypes, k_idx, tag):
    placed = [_place_chunk(s, k_idx, dt, f"{tag}_place{t}") for t, (s, dt) in enumerate(zip(shards, dtypes))]
    return _gather_chips(placed, f"{tag}_gather")


def _reduce_to_shards(grads, k_idx, c_idx, kc_idx, tag):
    recv = _pair_send(grads, f"{tag}_pair_send")
    pair = [_pair_sum(g, r, c_idx, f"{tag}_pair_sum{t}") for t, (g, r) in enumerate(zip(grads, recv))]
    recv = _chip_send(pair, f"{tag}_chip_send")
    bufs = [_chip_sum(p, r, kc_idx, f"{tag}_chip_sum{t}") for t, (p, r) in enumerate(zip(pair, recv))]
    return _pair_join(bufs, f"{tag}_pair_join")


WEIGHT_NAMES = ("c_ctx", "w_ada", "b_ada", "w_in", "w_pool", "pool_scale", "ssm_a_re", "ssm_a_im", "ssm_log_dt",
                "ssm_b_re", "ssm_b_im", "ssm_c_re", "ssm_c_im", "ssm_d", "w_glu", "w_out", "g_pre_mix", "g_post_mix",
                "g_pre_ffn", "g_post_ffn", "w_up", "w_conv", "w_down")


def _block_diag_in(bb, ng):
    nl, nd, npart, h, gp = bb.shape
    p = gp // ng
    w = jnp.einsum("ldqhgp,kg->lkhdqgp", bb.reshape(nl, nd, npart, h, ng, p), jnp.eye(ng, dtype=bb.dtype))
    return w.reshape(nl, ng * h, nd * npart * gp)


def _diag_in_grad(dw, ng, nh, p):
    gl = ng // S5_BAND
    out = jnp.einsum("ghqagp->qhagp", dw.reshape(gl, nh, 2, S5_BAND, gl, p))
    return out.reshape(2, nh, ng * p)


def _block_diag_out(cs, ng):
    nl, nd, npart, _, h, p = cs.shape
    w = jnp.einsum("ldqghp,kg->ldqkpgh", cs, jnp.eye(ng, dtype=cs.dtype))
    return w.reshape(nl, nd * npart * ng * p, ng * h)


def _diag_out_grad(dw, ng, nh, p):
    gl = ng // S5_BAND
    out = jnp.einsum("qagpgh->qaghp", dw.reshape(2, S5_BAND, gl, p, gl, nh))
    return out.reshape(2, ng, nh, p)


def kernel(x, c, ctx, c_ctx, w_ada, b_ada, w_in, w_pool, pool_scale, ssm_a_re, ssm_a_im, ssm_log_dt, ssm_b_re, ssm_b_im, ssm_c_re, ssm_c_im, ssm_d, w_glu, w_out, g_pre_mix, g_post_mix, g_pre_ffn, g_post_ffn, w_up, w_conv, w_down, loss_target, m_c_ctx, m_w_ada, m_b_ada, m_w_in, m_w_pool, m_pool_scale, m_ssm_a_re, m_ssm_a_im, m_ssm_log_dt, m_ssm_b_re, m_ssm_b_im, m_ssm_c_re, m_ssm_c_im, m_ssm_d, m_w_glu, m_w_out, m_g_pre_mix, m_g_post_mix, m_g_pre_ffn, m_g_post_ffn, m_w_up, m_w_conv, m_w_down, v_c_ctx, v_w_ada, v_b_ada, v_w_in, v_w_pool, v_pool_scale, v_ssm_a_re, v_ssm_a_im, v_ssm_log_dt, v_ssm_b_re, v_ssm_b_im, v_ssm_c_re, v_ssm_c_im, v_ssm_d, v_w_glu, v_w_out, v_g_pre_mix, v_g_post_mix, v_g_pre_ffn, v_g_post_ffn, v_w_up, v_w_conv, v_w_down):
    weights = dict(zip(WEIGHT_NAMES, (c_ctx, w_ada, b_ada, w_in, w_pool, pool_scale, ssm_a_re, ssm_a_im, ssm_log_dt,
                                      ssm_b_re, ssm_b_im, ssm_c_re, ssm_c_im, ssm_d, w_glu, w_out, g_pre_mix, g_post_mix,
                                      g_pre_ffn, g_post_ffn, w_up, w_conv, w_down)))
    mom1 = dict(zip(WEIGHT_NAMES, (m_c_ctx, m_w_ada, m_b_ada, m_w_in, m_w_pool, m_pool_scale, m_ssm_a_re, m_ssm_a_im,
                                   m_ssm_log_dt, m_ssm_b_re, m_ssm_b_im, m_ssm_c_re, m_ssm_c_im, m_ssm_d, m_w_glu, m_w_out,
                                   m_g_pre_mix, m_g_post_mix, m_g_pre_ffn, m_g_post_ffn, m_w_up, m_w_conv, m_w_down)))
    mom2 = dict(zip(WEIGHT_NAMES, (v_c_ctx, v_w_ada, v_b_ada, v_w_in, v_w_pool, v_pool_scale, v_ssm_a_re, v_ssm_a_im,
                                   v_ssm_log_dt, v_ssm_b_re, v_ssm_b_im, v_ssm_c_re, v_ssm_c_im, v_ssm_d, v_w_glu, v_w_out,
                                   v_g_pre_mix, v_g_post_mix, v_g_pre_ffn, v_g_post_ffn, v_w_up, v_w_conv, v_w_down)))

    xi, yi, ci = lax.axis_index("x"), lax.axis_index("y"), lax.axis_index("c")
    chip = 2 * xi + yi
    dev = 4 * xi + 2 * yi + ci
    nl = w_in.shape[0]
    n_lat, d = x.shape[1], x.shape[2]
    n_ctx = ctx.shape[1]
    n = n_ctx + n_lat
    _, ndir, ng, nstate, nh = ssm_b_re.shape
    gp = ng * nstate
    sw = ng * nh
    n_pool_groups, pool_group = w_pool.shape[1], w_pool.shape[3]
    pw = n_pool_groups * pool_group
    assert pw + sw == d and pw % sw == 0 and len(POOL_WINDOWS) == n_pool_groups and n_lat % GRID_W == 0
    dff2 = w_up.shape[2] * N_CHIPS
    ada_w = w_ada.shape[2] * N_CHIPS
    ada_cols = w_ada.shape[2]
    s_rows = gp // LANES

    c_pad = jnp.concatenate([c, jnp.zeros((SUBLANES - 1, d), F32)], axis=0)
    c_all = _gather_devices([c_pad], "gather_cond")[0][:, 0, :]
    cond = jnp.concatenate([c_all, c_ctx[None, :], jnp.zeros((16 - N_DEV - 1, d), F32)], axis=0)
    b_shard = lax.dynamic_slice_in_dim(b_ada, chip * ada_cols, ada_cols, axis=1)[:, None, :]
    mod_shard = _ada_fwd(cond, w_ada, b_shard, "ada_fwd")
    k_idx, c_idx, kc_idx = jnp.stack([chip]), jnp.stack([ci]), jnp.stack([chip, ci])
    mod_all = _gather_all([mod_shard], [F32], k_idx, "mods")[0]
    mod_all = jnp.transpose(mod_all, (0, 2, 1, 3)).reshape(nl, 16, ada_w)
    mod_lat = lax.dynamic_index_in_dim(mod_all, dev, axis=1, keepdims=False).reshape(nl, 6, d)
    mod_ctx = mod_all[:, N_DEV].reshape(nl, 6, d)
    mods = jnp.concatenate([jnp.stack([mod_ctx, mod_lat], axis=1), jnp.zeros((nl, 2, 2, d), F32)], axis=2)

    shards = [w_in, w_pool.reshape(nl, pw // N_CHIPS, pool_group), w_glu, w_out, w_up, w_down,
              w_conv.reshape(nl, 9, dff2 // N_CHIPS)]
    g_in, g_pool, g_glu, g_out, g_up, g_down, g_conv = _gather_all(shards, [COMM_DTYPE] * 6 + [F32], k_idx, "weights")
    wi = g_in.reshape(nl, d, d)
    wp = jnp.transpose(g_pool.reshape(nl, N_CHIPS, n_pool_groups, pool_group // N_CHIPS, pool_group),
                       (0, 2, 1, 3, 4)).reshape(nl, n_pool_groups, pool_group, pool_group)
    wg = g_glu.reshape(nl, sw, sw)
    wo = g_out.reshape(nl, d, d)
    wu = g_up
    wd = g_down.reshape(nl, dff2 // 2, d)
    wk = jnp.transpose(g_conv, (0, 2, 1, 3)).reshape(nl, 9, dff2)

    rows = nl * ndir
    a_re2 = ssm_a_re.reshape(rows, gp)
    a_im2 = ssm_a_im.reshape(rows, gp)
    logdt2 = jnp.repeat(ssm_log_dt.reshape(rows, ng), nstate, axis=1)
    b_re2 = jnp.transpose(ssm_b_re.reshape(rows, gp, nh), (0, 2, 1))
    b_im2 = jnp.transpose(ssm_b_im.reshape(rows, gp, nh), (0, 2, 1))
    lam_re, lam_im, bb_re, bb_im = _disc_fwd(a_re2, a_im2, logdt2, b_re2, b_im2, "s5_discretise")
    lam = jnp.stack([lam_re.reshape(nl, ndir, s_rows, LANES), lam_im.reshape(nl, ndir, s_rows, LANES)], axis=2)
    lam = lam.reshape(nl, 2 * ndir, s_rows, LANES)
    bbs = jnp.stack([bb_re.reshape(nl, ndir, nh, gp), bb_im.reshape(nl, ndir, nh, gp)], axis=2)
    w_b = _block_diag_in(bbs.astype(MXU_DTYPE), ng)
    w_b = [w_b[:, :, dr * 2 * gp:(dr + 1) * 2 * gp] for dr in range(ndir)]
    cs = jnp.stack([ssm_c_re, -ssm_c_im], axis=2)
    w_c = _block_diag_out(cs.astype(MXU_DTYPE), ng)
    w_c = [w_c[:, dr * 2 * gp:(dr + 1) * 2 * gp] for dr in range(ndir)]

    def row(v, l):
        return v[l:l + 1]

    xc = jnp.concatenate([ctx[0], x[0]], axis=0)
    saved = []
    for l in range(nl):
        t = f"l{l}"
        md = mods[l]
        h = _norm_mod_fwd(xc, row(g_pre_mix, l), md, 0, 1, n_ctx, f"{t}_pre_mix")
        u = _mm(h, wi, "nn", F32, f"{t}_in_proj", b_idx=l)
        p = _pool(u, pw, pool_group, n_ctx, False, MXU_DTYPE, f"{t}_pool")
        ypool = _pool_proj_fwd(p, wp, l, row(pool_scale, l), f"{t}_pool_proj")
        bu0 = _mm_s5(u, w_b[0], "in", f"{t}_s5_in0", b_idx=l, a_cols=(pw, sw))
        bu1 = _mm_s5(u, w_b[1], "in", f"{t}_s5_in1", b_idx=l, a_cols=(pw, sw))
        h0, h1 = _scan_fwd(bu0, bu1, lam[l], n_ctx, f"{t}_scan")
        y = _mm_s5(h0, w_c[0], "out", f"{t}_s5_out0", b_idx=l)
        y = _mm_s5(h1, w_c[1], "out", f"{t}_s5_out1", b_idx=l, add=y)
        s_out = _ssm_head_fwd(y, u, row(ssm_d, l), wg, l, f"{t}_s5_head")
        cat = jnp.concatenate([ypool, s_out], axis=1)
        mix = _mm(cat, wo, "nn", F32, f"{t}_out_proj", b_idx=l)
        x_mid = _gate_res_fwd(xc, mix, row(g_post_mix, l), md, 2, n_ctx, f"{t}_post_mix")
        h2 = _norm_mod_fwd(x_mid, row(g_pre_ffn, l), md, 3, 4, n_ctx, f"{t}_pre_ffn")
        z = _mm(h2, wu, "nn", F32, f"{t}_up", b_idx=l, b_chunks=N_CHIPS)
        act, cv, cg = _conv_glu_fwd(z, wk[l], n_ctx, f"{t}_conv_glu")
        f = _mm(act, wd, "nn", F32, f"{t}_down", b_idx=l)
        x_out = _gate_res_fwd(x_mid, f, row(g_post_ffn, l), md, 5, n_ctx, f"{t}_post_ffn")
        saved.append(dict(xc=xc, h=h, u=u, p=p, h0=h0, h1=h1, y=y, cat=cat, mix=mix, x_mid=x_mid, h2=h2, z=z, cv=cv, cg=cg, f=f))
        xc = x_out

    dx, loss_tile = _loss_grad(xc, loss_target[0], n_ctx, "loss")
    loss = lax.psum(loss_tile[0, 0], ("x", "y", "c"))

    big = {k: [None] * nl for k in ("w_in", "w_pool", "w_glu", "w_out", "w_up", "w_down")}
    small = {k: [None] * nl for k in ("pool_scale", "ssm_d", "g_pre_mix", "g_post_mix", "g_pre_ffn", "g_post_ffn",
                                      "lam", "bb", "cs", "w_conv")}
    dmods = [None] * nl
    for l in reversed(range(nl)):
        t = f"l{l}b"
        md = mods[l]
        sv = saved[l]
        df, dgate_ffn, small["g_post_ffn"][l] = _gate_res_bwd(dx, sv["f"], row(g_post_ffn, l), md, 5, n_ctx, f"{t}_post_ffn")
        dact = _mm(df, wd, "nt", F32, f"{t}_down_dx", b_idx=l)
        dz, dkv, dkg, act = _conv_glu_bwd(sv["z"], sv["cv"], sv["cg"], dact, wk[l], n_ctx, f"{t}_conv_glu")
        small["w_conv"][l] = jnp.concatenate([dkv, dkg], axis=1)
        big["w_down"][l] = _mm(act, df, "tn", COMM_DTYPE, f"{t}_down_dw")
        big["w_up"][l] = _mm(sv["h2"], dz, "tn", COMM_DTYPE, f"{t}_up_dw", b_chunks=2, out_chunks=N_CHIPS)
        dh2 = _mm(dz, wu, "nt", F32, f"{t}_up_dx", b_idx=l, a_chunks=2, b_chunks=N_CHIPS)
        dx, dss_ffn, small["g_pre_ffn"][l] = _norm_mod_bwd(dh2, sv["x_mid"], row(g_pre_ffn, l), md, 3, 4, dx, n_ctx,
                                                           f"{t}_pre_ffn")
        dmix, dgate_mix, small["g_post_mix"][l] = _gate_res_bwd(dx, sv["mix"], row(g_post_mix, l), md, 2, n_ctx,
                                                                f"{t}_post_mix")
        dcat = _mm(dmix, wo, "nt", F32, f"{t}_out_dx", b_idx=l)
        big["w_out"][l] = _mm(sv["cat"], dmix, "tn", COMM_DTYPE, f"{t}_out_dw")
        dp, small["pool_scale"][l], big["w_pool"][l] = _pool_proj_bwd(sv["p"], dcat, wp, l, row(pool_scale, l),
                                                                      f"{t}_pool_proj")
        du_pool = _pool(dp, pw, pool_group, n_ctx, True, F32, f"{t}_pool")
        dy, du_dir, gact, dq, small["ssm_d"][l] = _ssm_head_bwd(dcat, sv["y"], sv["u"], row(ssm_d, l), wg, l, f"{t}_s5_head")
        big["w_glu"][l] = _mm(gact, dq, "tn", COMM_DTYPE, f"{t}_glu_dw")
        dh0 = _mm_s5(dy, w_c[0], "out_dx", f"{t}_s5_out_dx0", b_idx=l)
        dh1 = _mm_s5(dy, w_c[1], "out_dx", f"{t}_s5_out_dx1", b_idx=l)
        small["cs"][l] = jnp.stack([_diag_out_grad(_mm_s5(sv[hk], dy, "out_dw", f"{t}_s5_out_dw{dr}"), ng, nh, nstate)
                                    for dr, hk in enumerate(("h0", "h1"))], axis=0)
        a0, a1, small["lam"][l] = _scan_bwd(dh0, dh1, sv["h0"], sv["h1"], lam[l], n_ctx, f"{t}_scan")
        du_proj = _mm_s5(a0, w_b[0], "in_dx", f"{t}_s5_in_dx0", b_idx=l)
        du_proj = _mm_s5(a1, w_b[1], "in_dx", f"{t}_s5_in_dx1", b_idx=l, add=du_proj)
        small["bb"][l] = jnp.stack([_diag_in_grad(_mm_s5(sv["u"], adj, "in_dw", f"{t}_s5_in_dw{dr}", a_cols=(pw, sw)),
                                                  ng, nh, nstate) for dr, adj in enumerate((a0, a1))], axis=0)
        du = _assemble_du(du_pool, du_dir, du_proj, f"{t}_du")
        dh = _mm(du, wi, "nt", F32, f"{t}_in_dx", b_idx=l)
        big["w_in"][l] = _mm(sv["h"], du, "tn", COMM_DTYPE, f"{t}_in_dw")
        dx, dss_mix, small["g_pre_mix"][l] = _norm_mod_bwd(dh, sv["xc"], row(g_pre_mix, l), md, 0, 1, dx, n_ctx,
                                                           f"{t}_pre_mix")
        dmods[l] = jnp.concatenate([dss_mix, dgate_mix, dss_ffn, dgate_ffn], axis=1).reshape(2, ada_w)

    grad_x = dx[n_ctx:][None]

    dmod_all = _gather_devices([jnp.stack(dmods, axis=0)], "gather_dmods")[0]
    ada_rows, db_ada = _ada_rows(dmod_all, "ada_rows")
    rows_shard = lax.dynamic_slice_in_dim(ada_rows, chip * ada_cols, ada_cols, axis=2)
    dcond_part = _ada_dcond(rows_shard, w_ada, "ada_dcond")
    dcond_parts = _gather_devices([dcond_part], "gather_dcond")[0][0::2]
    grads = {"w_ada": _ada_dw(cond, rows_shard, "ada_dw"), "b_ada": db_ada[:, 0, :],
             "c_ctx": _ada_dctx(dcond_parts, c_ctx[None, :], N_DEV, "ada_dctx")[0]}

    stacked = {k: jnp.stack(v, axis=0) for k, v in big.items()}
    parts = [stacked["w_in"].reshape(nl, N_CHIPS, d // N_CHIPS, d),
             jnp.transpose(stacked["w_pool"].astype(COMM_DTYPE).reshape(nl, n_pool_groups, N_CHIPS, pool_group // N_CHIPS,
                                                                      pool_group), (0, 2, 1, 3, 4))
             .reshape(nl, N_CHIPS, pw // N_CHIPS, pool_group),
             stacked["w_glu"].reshape(nl, N_CHIPS, sw // N_CHIPS, sw),
             stacked["w_out"].reshape(nl, N_CHIPS, d // N_CHIPS, d),
             stacked["w_up"],
             stacked["w_down"].reshape(nl, N_CHIPS, dff2 // 2 // N_CHIPS, d)]
    r_in, r_pool, r_glu, r_out, r_up, r_down = _reduce_to_shards(parts, k_idx, c_idx, kc_idx, "big")
    grads.update(w_in=r_in, w_pool=r_pool.reshape(w_pool.shape), w_glu=r_glu, w_out=r_out, w_up=r_up, w_down=r_down)

    order = ("pool_scale", "ssm_d", "g_pre_mix", "g_post_mix", "g_pre_ffn", "g_post_ffn", "lam", "bb", "cs", "w_conv")
    pieces = [jnp.stack(small[k], axis=0) for k in order]
    flat = jnp.concatenate([q.reshape(-1) for q in pieces])
    unit = nl * N_CHIPS * SUBLANES * 1024
    padded = -(-flat.shape[0] // unit) * unit
    flat = jnp.concatenate([flat, jnp.zeros((padded - flat.shape[0],), F32)])
    vec = flat.reshape(nl, N_CHIPS, padded // (nl * N_CHIPS * 1024), 1024)
    vec = _gather_all(_reduce_to_shards([vec], k_idx, c_idx, kc_idx, "small"), [F32], k_idx, "small_all")[0].reshape(-1)
    red, pos = {}, 0
    for k, q in zip(order, pieces):
        red[k] = vec[pos:pos + q.size].reshape(q.shape)
        pos += q.size
    for k in ("pool_scale", "ssm_d", "g_pre_mix", "g_post_mix", "g_pre_ffn", "g_post_ffn"):
        grads[k] = red[k][:, 0, :]
    dlam = red["lam"].reshape(nl, ndir, 2, gp)
    dbb = red["bb"].reshape(nl, ndir, 2, nh, gp)
    d_are, d_aim, d_ldt, d_bre, d_bim = _disc_bwd(
        a_re2, a_im2, logdt2, b_re2, b_im2, dlam[:, :, 0].reshape(rows, gp), dlam[:, :, 1].reshape(rows, gp),
        dbb[:, :, 0].reshape(rows, nh, gp), dbb[:, :, 1].reshape(rows, nh, gp), nstate, "s5_discretise_bwd")
    grads["ssm_a_re"] = d_are.reshape(ssm_a_re.shape)
    grads["ssm_a_im"] = d_aim.reshape(ssm_a_im.shape)
    grads["ssm_log_dt"] = d_ldt[:, :ng].reshape(ssm_log_dt.shape)
    grads["ssm_b_re"] = jnp.transpose(d_bre, (0, 2, 1)).reshape(ssm_b_re.shape)
    grads["ssm_b_im"] = jnp.transpose(d_bim, (0, 2, 1)).reshape(ssm_b_im.shape)
    grads["ssm_c_re"] = red["cs"][:, :, 0]
    grads["ssm_c_im"] = -red["cs"][:, :, 1]
    conv_cols = dff2 // N_CHIPS
    grads["w_conv"] = lax.dynamic_slice_in_dim(red["w_conv"], chip * conv_cols, conv_cols, axis=2).reshape(w_conv.shape)

    delta, new_m, new_v = {}, {}, {}
    for k in WEIGHT_NAMES:
        delta[k], new_m[k], new_v[k] = _adamw(weights[k], grads[k], mom1[k], mom2[k], f"adamw_{k}")
    return (loss, grad_x, *[grads[k] for k in WEIGHT_NAMES], *[delta[k] for k in WEIGHT_NAMES],
            *[new_m[k] for k in WEIGHT_NAMES], *[new_v[k] for k in WEIGHT_NAMES])
```

```python
import math

import jax
import jax.numpy as jnp
from jax import lax
from jax.experimental import pallas as pl
from jax.experimental.pallas import tpu as pltpu

F32 = jnp.float32
MXU_DTYPE = jnp.bfloat16
COMM_DTYPE = jnp.bfloat16
HIGHEST = lax.Precision.HIGHEST
VMEM_LIMIT_BYTES = 48 * 1024 * 1024
LANES = 128
SUBLANES = 8
N_CHIPS = 4
N_DEV = 8

EPS = 1e-6
GRID_W = 64
POOL_WINDOWS = (2, 4, 8, 16)
ADAM_LR = 0.001
ADAM_B1 = 0.9
ADAM_B2 = 0.999
ADAM_EPS = 1e-08
ADAM_WD = 0.01
ADAM_STEP = 10
GELU_C0 = math.sqrt(2.0 / math.pi)
GELU_C1 = 0.044715

SDS = jax.ShapeDtypeStruct
ANY = pl.BlockSpec(memory_space=pl.ANY)
MESH = pl.DeviceIdType.MESH


def _cparams(*sem):
    return pltpu.CompilerParams(dimension_semantics=sem if sem else None, vmem_limit_bytes=VMEM_LIMIT_BYTES)


def _pick(n, cands):
    for cand in cands:
        if n % cand == 0:
            return cand
    return n


def _row_tile(n_ctx, n):
    return math.gcd(math.gcd(n_ctx, n - n_ctx), 256)


_DIMS = {"nn": (((1,), (0,)), ((), ())), "nt": (((1,), (1,)), ((), ())), "tn": (((0,), (0,)), ((), ()))}
_TM = (1088, 1024, 512, 384, 256, 128, 64, 32, 16, 8)
_TN = (1024, 1408, 512, 384, 256, 128)
_TK = (1024, 1088, 512, 1408, 384, 256, 128, 64, 32, 16, 8)


def _chunk_of(idx, per, chunks):
    out = 0
    for q in range(1, chunks):
        out = out + (idx >= q * per).astype(jnp.int32)
    return out


def _within(idx, per, chunks):
    return idx - per * _chunk_of(idx, per, chunks)


def _mm_call(args, mode, out_dtype, name, grid, tiles, specs, o_spec, o_shape, a3d=False, b3d=False, out3d=False,
             add=False):
    tm, tn, _ = tiles
    nk = grid[2]

    def operand(ref, is3d):
        v = ref[...]
        if is3d:
            v = pltpu.einshape("tjl->t(jl)", v)
        return v.astype(MXU_DTYPE)

    def body(*refs):
        a_ref, b_ref, o_ref, acc_ref = refs[0], refs[1], refs[-2], refs[-1]
        kk = pl.program_id(2)

        @pl.when(kk == 0)
        def _():
            acc_ref[...] = jnp.zeros(acc_ref.shape, F32)

        acc_ref[...] += lax.dot_general(operand(a_ref, a3d), operand(b_ref, b3d), _DIMS[mode], preferred_element_type=F32)

        @pl.when(kk == nk - 1)
        def _():
            total = acc_ref[...]
            if add:
                total = total + refs[2][...]
            if out3d:
                total = pltpu.einshape("t(jl)->tjl", total, l=LANES)
            o_ref[...] = total.astype(o_ref.dtype)

    return pl.pallas_call(
        body, grid=grid, in_specs=specs, out_specs=o_spec, out_shape=SDS(o_shape, out_dtype),
        scratch_shapes=[pltpu.VMEM((tm, tn), F32)],
        compiler_params=_cparams("parallel", "parallel", "arbitrary"), name=name)(*args)


def _lanes3(blk, imap):
    return (blk[0], blk[1] // LANES, LANES), (lambda i, j, kk: imap(i, j, kk) + (0,))


def _mm(a, b, mode, out_dtype, name, a_idx=None, b_idx=None, a_cols=None, a_chunks=None, b_chunks=None, out_chunks=None,
        a3d=False, b3d=False, out3d=False, add=None):
    a2, b2 = a.shape[-2:], b.shape[-2:]
    if a3d:
        a2 = (a.shape[0], a.shape[1] * a.shape[2])
    if b3d:
        assert mode == "tn"
        b2 = (b.shape[0], b.shape[1] * b.shape[2])
    if a_chunks is not None:
        assert mode == "nt" and a.shape[-3] == a_chunks
        a2 = (a2[0], a2[1] * a_chunks)
    if b_chunks is not None:
        assert b.shape[-3] == b_chunks
        b2 = (b2[0], b2[1] * b_chunks)
    alast = a2[1] if a_cols is None else a_cols[1]
    if mode == "nn":
        m, k, n = a2[0], alast, b2[1]
        assert b2[0] == k
    elif mode == "nt":
        m, k, n = a2[0], alast, b2[0]
        assert b2[1] == k
    else:
        k, m, n = a2[0], alast, b2[1]
        assert b2[0] == k
    n_unit = n // (out_chunks or 1) // (b_chunks if b_chunks and mode != "nt" else 1)
    k_unit = k // (b_chunks if b_chunks and mode == "nt" else 1) // (a_chunks or 1)
    tm, tn, tk = _pick(m, _TM), _pick(n_unit, _TN), _pick(k_unit, _TK)
    nk = k // tk
    a_lane_tile = tm if mode == "tn" else tk
    off = 0
    if a_cols is not None:
        assert a_cols[0] % a_lane_tile == 0
        off = a_cols[0] // a_lane_tile

    if mode == "tn":
        a_blk, a_map = (tk, tm), (lambda i, j, kk: (kk, i + off))
    else:
        a_blk, a_map = (tm, tk), (lambda i, j, kk: (i, kk + off))
        if a_chunks is not None:
            aper = k // a_chunks // tk
            a_blk, a_map = (None, tm, tk), (lambda i, j, kk: (_chunk_of(kk, aper, a_chunks), i, _within(kk, aper, a_chunks)))
    if mode == "nt":
        b_blk, b_map = (tn, tk), (lambda i, j, kk: (j, kk))
        if b_chunks is not None:
            per = k // b_chunks // tk
            b_blk, b_map = (None, tn, tk), (lambda i, j, kk: (_chunk_of(kk, per, b_chunks), j, _within(kk, per, b_chunks)))
    else:
        b_blk, b_map = (tk, tn), (lambda i, j, kk: (kk, j))
        if b_chunks is not None:
            per = n // b_chunks // tn
            b_blk, b_map = (None, tk, tn), (lambda i, j, kk: (_chunk_of(j, per, b_chunks), kk, _within(j, per, b_chunks)))
    if a3d:
        a_blk, a_map = _lanes3(a_blk, a_map)
    if b3d:
        b_blk, b_map = _lanes3(b_blk, b_map)
    if a_idx is not None:
        a_blk, a_map0 = (None,) + a_blk, a_map
        a_map = lambda i, j, kk: (a_idx,) + a_map0(i, j, kk)
    if b_idx is not None:
        b_blk, b_map0 = (None,) + b_blk, b_map
        b_map = lambda i, j, kk: (b_idx,) + b_map0(i, j, kk)
    o_blk, o_map, o_shape = (tm, tn), (lambda i, j, kk: (i, j)), (m, n)
    if out_chunks is not None:
        oper = n // out_chunks // tn
        o_map = lambda i, j, kk: (_chunk_of(j, oper, out_chunks), i, _within(j, oper, out_chunks))
        o_blk, o_shape = (None, tm, tn), (out_chunks, m, n // out_chunks)
    if out3d:
        o_blk, o_map = _lanes3(o_blk, o_map)
        o_shape = (m, n // LANES, LANES)
    specs, args = [pl.BlockSpec(a_blk, a_map), pl.BlockSpec(b_blk, b_map)], [a, b]
    if add is not None:
        specs.append(pl.BlockSpec((tm, tn), lambda i, j, kk: (i, j)))
        args.append(add)
    return _mm_call(args, mode, out_dtype, name, (m // tm, n // tn, nk), (tm, tn, tk), specs, pl.BlockSpec(o_blk, o_map),
                    o_shape, a3d, b3d, out3d, add is not None)


S5_BAND = 2


def _mm_s5(a, b, kind, name, b_idx=None, a_cols=None, add=None):
    nb = S5_BAND
    wide3 = a if kind in ("in_dx", "out", "out_dw") else (b if kind == "in_dw" else None)
    if kind in ("in", "out_dx"):
        rows, wide = a.shape[0], b.shape[-1] if kind == "in" else b.shape[-2]
    else:
        rows, wide = wide3.shape[0], wide3.shape[1] * LANES
    sw = a_cols[1] if a_cols is not None else (b.shape[-1] if kind in ("out", "out_dw") else
                                                 (a.shape[1] if kind == "out_dx" else b.shape[-2]))
    tw, ts = wide // (2 * nb), sw // nb
    nwt = wide // tw
    off = 0 if a_cols is None else a_cols[0] // ts
    half = lambda t: _within(t, nb, nwt // nb)
    lead = (lambda blk, imap: (blk, imap)) if b_idx is None else (
        lambda blk, imap: ((None,) + blk, lambda i, j, kk: (b_idx,) + imap(i, j, kk)))
    rt = _pick(rows, _TM)
    if kind in ("in", "out_dx"):
        mode = "nn" if kind == "in" else "nt"
        a_spec = pl.BlockSpec((rt, ts), lambda i, j, kk: (i, off + half(j)))
        b_blk, b_map = ((ts, tw), lambda i, j, kk: (half(j), j)) if kind == "in" else ((tw, ts), lambda i, j, kk: (j, half(j)))
        o_blk, o_map = _lanes3((rt, tw), lambda i, j, kk: (i, j))
        return _mm_call([a, b], mode, F32, name, (rows // rt, nwt, 1), (rt, tw, ts), [a_spec, pl.BlockSpec(*lead(b_blk, b_map))],
                        pl.BlockSpec(o_blk, o_map), (rows, wide // LANES, LANES), out3d=True)
    if kind in ("out", "in_dx"):
        mode = "nn" if kind == "out" else "nt"
        a_blk, a_map = _lanes3((rt, tw), lambda i, j, kk: (i, kk * nb + j))
        b_blk, b_map = ((tw, ts), lambda i, j, kk: (kk * nb + j, j)) if kind == "out" else (
            (ts, tw), lambda i, j, kk: (j, kk * nb + j))
        specs, args = [pl.BlockSpec(a_blk, a_map), pl.BlockSpec(*lead(b_blk, b_map))], [a, b]
        if add is not None:
            specs.append(pl.BlockSpec((rt, ts), lambda i, j, kk: (i, j)))
            args.append(add)
        return _mm_call(args, mode, F32, name, (rows // rt, nb, nwt // nb), (rt, ts, tw), specs,
                        pl.BlockSpec((rt, ts), lambda i, j, kk: (i, j)), (rows, sw), a3d=True, add=add is not None)
    kt = _pick(rows, _TK)
    if kind == "out_dw":
        a_blk, a_map = _lanes3((kt, tw), lambda i, j, kk: (kk, i))
        return _mm_call([a, b], "tn", F32, name, (nwt, 1, rows // kt), (tw, ts, kt),
                        [pl.BlockSpec(a_blk, a_map), pl.BlockSpec((kt, ts), lambda i, j, kk: (kk, half(i)))],
                        pl.BlockSpec((tw, ts), lambda i, j, kk: (i, 0)), (wide, ts), a3d=True)
    assert kind == "in_dw"
    b_blk, b_map = _lanes3((kt, tw), lambda i, j, kk: (kk, j))
    return _mm_call([a, b], "tn", F32, name, (1, nwt, rows // kt), (ts, tw, kt),
                    [pl.BlockSpec((kt, ts), lambda i, j, kk: (kk, off + half(j))), pl.BlockSpec(b_blk, b_map)],
                    pl.BlockSpec((ts, tw), lambda i, j, kk: (0, j)), (ts, wide), b3d=True)


def _seg_map(nbc):
    return lambda i: (jnp.where(i < nbc, 0, 1), 0, 0)


def _rstd(v):
    return lax.rsqrt(jnp.mean(v * v, axis=-1, keepdims=True) + EPS)


def _norm_mod_fwd(x, g, mods, sh, sc, n_ctx, name):
    n, d = x.shape
    tm = _row_tile(n_ctx, n)
    nbc = n_ctx // tm

    def body(x_ref, g_ref, m_ref, h_ref):
        xv = x_ref[...]
        hn = xv * _rstd(xv) * g_ref[...]
        h_ref[...] = (hn * (1.0 + m_ref[0, sc:sc + 1, :]) + m_ref[0, sh:sh + 1, :]).astype(h_ref.dtype)

    row = pl.BlockSpec((tm, d), lambda i: (i, 0))
    return pl.pallas_call(
        body, grid=(n // tm,),
        in_specs=[row, pl.BlockSpec((1, d), lambda i: (0, 0)), pl.BlockSpec((1, 8, d), _seg_map(nbc))],
        out_specs=row, out_shape=SDS((n, d), MXU_DTYPE), compiler_params=_cparams("parallel"), name=name)(x, g, mods)


def _gate_res_fwd(x, f, g, mods, gi, n_ctx, name):
    n, d = x.shape
    tm = _row_tile(n_ctx, n)
    nbc = n_ctx // tm

    def body(x_ref, f_ref, g_ref, m_ref, o_ref):
        fv = f_ref[...]
        o_ref[...] = x_ref[...] + m_ref[0, gi:gi + 1, :] * (fv * _rstd(fv) * g_ref[...])

    row = pl.BlockSpec((tm, d), lambda i: (i, 0))
    return pl.pallas_call(
        body, grid=(n // tm,),
        in_specs=[row, row, pl.BlockSpec((1, d), lambda i: (0, 0)), pl.BlockSpec((1, 8, d), _seg_map(nbc))],
        out_specs=row, out_shape=SDS((n, d), F32), compiler_params=_cparams("parallel"), name=name)(x, f, g, mods)


def _gate_res_bwd(dx, f, g, mods, gi, n_ctx, name):
    n, d = dx.shape
    tm = _row_tile(n_ctx, n)
    nbc = n_ctx // tm

    def body(dx_ref, f_ref, g_ref, m_ref, df_ref, dgate_ref, dg_ref):
        i = pl.program_id(0)

        @pl.when(i == 0)
        def _():
            dg_ref[...] = jnp.zeros(dg_ref.shape, F32)

        @pl.when(jnp.logical_or(i == 0, i == nbc))
        def _():
            dgate_ref[...] = jnp.zeros(dgate_ref.shape, F32)

        dxv, fv, gv = dx_ref[...], f_ref[...], g_ref[...]
        rs = _rstd(fv)
        nv = fv * rs
        dgate_ref[0] += jnp.sum(dxv * (nv * gv), axis=0, keepdims=True)
        dout = dxv * m_ref[0, gi:gi + 1, :]
        dg_ref[...] += jnp.sum(dout * nv, axis=0, keepdims=True)
        dn = dout * gv
        df_ref[...] = (rs * (dn - nv * jnp.mean(dn * nv, axis=-1, keepdims=True))).astype(df_ref.dtype)

    row = pl.BlockSpec((tm, d), lambda i: (i, 0))
    vec = pl.BlockSpec((1, d), lambda i: (0, 0))
    return pl.pallas_call(
        body, grid=(n // tm,),
        in_specs=[row, row, vec, pl.BlockSpec((1, 8, d), _seg_map(nbc))],
        out_specs=[row, pl.BlockSpec((1, 1, d), _seg_map(nbc)), vec],
        out_shape=[SDS((n, d), MXU_DTYPE), SDS((2, 1, d), F32), SDS((1, d), F32)],
        compiler_params=_cparams("arbitrary"), name=name)(dx, f, g, mods)


def _norm_mod_bwd(dh, x, g, mods, sh, sc, dx_res, n_ctx, name):
    n, d = x.shape
    tm = _row_tile(n_ctx, n)
    nbc = n_ctx // tm

    def body(dh_ref, x_ref, g_ref, m_ref, r_ref, dx_ref, dss_ref, dg_ref):
        i = pl.program_id(0)

        @pl.when(i == 0)
        def _():
            dg_ref[...] = jnp.zeros(dg_ref.shape, F32)

        @pl.when(jnp.logical_or(i == 0, i == nbc))
        def _():
            dss_ref[...] = jnp.zeros(dss_ref.shape, F32)

        dhv, xv, gv = dh_ref[...], x_ref[...], g_ref[...]
        rs = _rstd(xv)
        nv = xv * rs
        dss_ref[0, 0:1, :] += jnp.sum(dhv, axis=0, keepdims=True)
        dss_ref[0, 1:2, :] += jnp.sum(dhv * (nv * gv), axis=0, keepdims=True)
        dhn = dhv * (1.0 + m_ref[0, sc:sc + 1, :])
        dg_ref[...] += jnp.sum(dhn * nv, axis=0, keepdims=True)
        dn = dhn * gv
        dx_ref[...] = r_ref[...] + rs * (dn - nv * jnp.mean(dn * nv, axis=-1, keepdims=True))

    row = pl.BlockSpec((tm, d), lambda i: (i, 0))
    vec = pl.BlockSpec((1, d), lambda i: (0, 0))
    return pl.pallas_call(
        body, grid=(n // tm,),
        in_specs=[row, row, vec, pl.BlockSpec((1, 8, d), _seg_map(nbc)), row],
        out_specs=[row, pl.BlockSpec((1, 2, d), _seg_map(nbc)), vec],
        out_shape=[SDS((n, d), F32), SDS((2, 2, d), F32), SDS((1, d), F32)],
        compiler_params=_cparams("arbitrary"), name=name)(dh, x, g, mods, dx_res)


def _loss_grad(xc, target, n_ctx, name):
    n, d = xc.shape
    tm = _row_tile(n_ctx, n)
    nbc = n_ctx // tm
    nb = n // tm

    def body(x_ref, t_ref, dx_ref, l_ref, acc_ref):
        i = pl.program_id(0)

        @pl.when(i == 0)
        def _():
            acc_ref[...] = jnp.zeros(acc_ref.shape, F32)

        @pl.when(i < nbc)
        def _():
            dx_ref[...] = jnp.zeros(dx_ref.shape, F32)

        @pl.when(i >= nbc)
        def _():
            diff = x_ref[...] - t_ref[...]
            dx_ref[...] = diff * (1.0 / d)
            acc_ref[...] += jnp.sum(diff * diff, axis=0, keepdims=True)

        @pl.when(i == nb - 1)
        def _():
            l_ref[...] = jnp.full(l_ref.shape, (0.5 / d) * jnp.sum(acc_ref[...]), F32)

    row = pl.BlockSpec((tm, d), lambda i: (i, 0))
    return pl.pallas_call(
        body, grid=(nb,),
        in_specs=[row, pl.BlockSpec((tm, d), lambda i: (jnp.maximum(i - nbc, 0), 0))],
        out_specs=[row, pl.BlockSpec((SUBLANES, LANES), lambda i: (0, 0))],
        out_shape=[SDS((n, d), F32), SDS((SUBLANES, LANES), F32)],
        scratch_shapes=[pltpu.VMEM((1, d), F32)],
        compiler_params=_cparams("arbitrary"), name=name)(xc, target)


POOL_PAD = 16


def _pool(src, pool_width, pool_group, n_ctx, bwd, out_dtype, name):
    n = src.shape[0]
    n_lat = n - n_ctx
    gb = pool_group // LANES
    segs = ((0, n_ctx, POOL_PAD), (n_ctx, n_lat, 2 * POOL_PAD + n_ctx))
    total = 3 * POOL_PAD + n

    def body(s_ref, o_ref, scr):
        j = pl.program_id(0)
        for base in (0, POOL_PAD + n_ctx, 2 * POOL_PAD + n):
            scr[pl.ds(base, POOL_PAD), :] = jnp.zeros((POOL_PAD, LANES), F32)
        for gi, w in enumerate(POOL_WINDOWS):
            @pl.when(jnp.logical_and(j >= gi * gb, j < (gi + 1) * gb))
            def _(w=w):
                half = w // 2
                offs = range(-half + 1, half + 1) if bwd else range(-half, half)
                for row0, nseg, base in segs:
                    ch = math.gcd(nseg, 256)

                    def count(c0):
                        t = c0 + lax.broadcasted_iota(jnp.int32, (ch, LANES), 0)
                        return (jnp.minimum(t + half, nseg) - jnp.maximum(t - half, 0)).astype(F32)

                    def fill(ci, carry):
                        c0 = pl.multiple_of(ci * ch, ch)
                        v = s_ref[pl.ds(row0 + c0, ch), :]
                        scr[pl.ds(base + c0, ch), :] = v / count(c0) if bwd else v
                        return carry

                    def window(ci, carry):
                        c0 = pl.multiple_of(ci * ch, ch)
                        acc = jnp.zeros((ch, LANES), F32)
                        for off in offs:
                            acc = acc + scr[pl.ds(c0 + (base + off), ch), :]
                        v = s_ref[pl.ds(row0 + c0, ch), :]
                        res = acc - v if bwd else acc / count(c0) - v
                        o_ref[pl.ds(row0 + c0, ch), :] = res.astype(o_ref.dtype)
                        return carry

                    lax.fori_loop(0, nseg // ch, fill, 0)
                    lax.fori_loop(0, nseg // ch, window, 0)

    blk = pl.BlockSpec((n, LANES), lambda j: (0, j))
    return pl.pallas_call(
        body, grid=(pool_width // LANES,), in_specs=[blk], out_specs=blk,
        out_shape=SDS((n, pool_width), out_dtype), scratch_shapes=[pltpu.VMEM((total, LANES), F32)],
        compiler_params=_cparams("parallel"), name=name)(src)


def _pool_proj_fwd(p, wp, l, scale, name):
    n, pw = p.shape
    ng, c = wp.shape[1], wp.shape[2]
    tm = _pick(n, _TM)

    def body(p_ref, w_ref, s_ref, o_ref):
        y = jnp.dot(p_ref[...], w_ref[...].astype(MXU_DTYPE), preferred_element_type=F32)
        o_ref[...] = (y * s_ref[...]).astype(o_ref.dtype)

    return pl.pallas_call(
        body, grid=(ng, n // tm),
        in_specs=[pl.BlockSpec((tm, c), lambda g, i: (i, g)), pl.BlockSpec((None, None, c, c), lambda g, i: (l, g, 0, 0)),
                  pl.BlockSpec((1, c), lambda g, i: (0, g))],
        out_specs=pl.BlockSpec((tm, c), lambda g, i: (i, g)), out_shape=SDS((n, pw), MXU_DTYPE),
        compiler_params=_cparams("parallel", "parallel"), name=name)(p, wp, scale)


def _pool_proj_bwd(p, dcat, wp, l, scale, name):
    n, pw = p.shape
    ng, c = wp.shape[1], wp.shape[2]
    tm = _pick(n, _TM)

    def body(p_ref, dy_ref, w_ref, s_ref, dp_ref, ds_ref, dw_ref):
        i = pl.program_id(1)

        @pl.when(i == 0)
        def _():
            ds_ref[...] = jnp.zeros(ds_ref.shape, F32)
            dw_ref[...] = jnp.zeros(dw_ref.shape, F32)

        pv, wv, dy = p_ref[...], w_ref[...].astype(MXU_DTYPE), dy_ref[...]
        y = jnp.dot(pv, wv, preferred_element_type=F32)
        ds_ref[...] += jnp.sum(dy * y, axis=0, keepdims=True)
        dpw = (dy * s_ref[...]).astype(MXU_DTYPE)
        dp_ref[...] = lax.dot_general(dpw, wv, _DIMS["nt"], preferred_element_type=F32)
        dw_ref[0] += lax.dot_general(pv, dpw, _DIMS["tn"], preferred_element_type=F32)

    return pl.pallas_call(
        body, grid=(ng, n // tm),
        in_specs=[pl.BlockSpec((tm, c), lambda g, i: (i, g)), pl.BlockSpec((tm, c), lambda g, i: (i, g)),
                  pl.BlockSpec((None, None, c, c), lambda g, i: (l, g, 0, 0)), pl.BlockSpec((1, c), lambda g, i: (0, g))],
        out_specs=[pl.BlockSpec((tm, c), lambda g, i: (i, g)), pl.BlockSpec((1, c), lambda g, i: (0, g)),
                   pl.BlockSpec((1, c, c), lambda g, i: (g, 0, 0))],
        out_shape=[SDS((n, pw), F32), SDS((1, pw), F32), SDS((ng, c, c), F32)],
        compiler_params=_cparams("arbitrary", "arbitrary"), name=name)(p, dcat, wp, scale)


def _disc_math(a_re, a_im, logdt, b_re, b_im):
    dt = jnp.exp(logdt)
    mag = jnp.exp(a_re * dt)
    lam_re = mag * jnp.cos(a_im * dt)
    lam_im = mag * jnp.sin(a_im * dt)
    denom = a_re * a_re + a_im * a_im
    nr, ni = lam_re - 1.0, lam_im
    f_re = ((nr * a_re + ni * a_im) / denom)[:, None, :]
    f_im = ((ni * a_re - nr * a_im) / denom)[:, None, :]
    return lam_re, lam_im, f_re * b_re - f_im * b_im, f_re * b_im + f_im * b_re


def _disc_fwd(a_re, a_im, logdt, b_re, b_im, name):
    def body(ar, ai, ld, br, bi, o_lr, o_li, o_br, o_bi):
        lr, li, bbr, bbi = _disc_math(ar[...], ai[...], ld[...], br[...], bi[...])
        o_lr[...] = lr
        o_li[...] = li
        o_br[...] = bbr
        o_bi[...] = bbi

    return pl.pallas_call(
        body, out_shape=[SDS(a_re.shape, F32), SDS(a_re.shape, F32), SDS(b_re.shape, F32), SDS(b_re.shape, F32)],
        compiler_params=_cparams(), name=name)(a_re, a_im, logdt, b_re, b_im)


def _disc_bwd(a_re, a_im, logdt, b_re, b_im, d_lr, d_li, d_bbr, d_bbi, group, name):
    rows, gp = a_re.shape

    def body(ar, ai, ld, br, bi, g_lr, g_li, g_br, g_bi, o_ar, o_ai, o_ld, o_br, o_bi):
        _, vjp = jax.vjp(_disc_math, ar[...], ai[...], ld[...], br[...], bi[...])
        dar, dai, dld, dbr, dbi = vjp((g_lr[...], g_li[...], g_br[...], g_bi[...]))
        o_ar[...] = dar
        o_ai[...] = dai
        state = lax.broadcasted_iota(jnp.int32, (gp, LANES), 0)
        first = lax.broadcasted_iota(jnp.int32, (gp, LANES), 1) * group
        sel = jnp.logical_and(state >= first, state < first + group).astype(F32)
        o_ld[...] = jnp.dot(dld, sel, precision=HIGHEST, preferred_element_type=F32)
        o_br[...] = dbr
        o_bi[...] = dbi

    return pl.pallas_call(
        body, out_shape=[SDS(a_re.shape, F32), SDS(a_re.shape, F32), SDS((rows, LANES), F32),
                         SDS(b_re.shape, F32), SDS(b_re.shape, F32)],
        compiler_params=_cparams(), name=name)(a_re, a_im, logdt, b_re, b_im, d_lr, d_li, d_bbr, d_bbi)


def _scan_maps(nbc, nb):
    nbl = nb - nbc
    fwd0 = lambda i: (i, 0, 0)
    fwd1 = lambda i: (jnp.where(i < nbc, nbc - 1 - i, nb - 1 - (i - nbc)), 0, 0)
    adj0 = lambda i: (nb - 1 - i, 0, 0)
    adj1 = lambda i: (jnp.where(i < nbl, nbc + i, i - nbl), 0, 0)
    return fwd0, fwd1, adj0, adj1


def _scan_fwd(bu0, bu1, lam, n_ctx, name):
    n, s2, _ = bu0.shape
    s = s2 // 2
    tt = math.gcd(math.gcd(n_ctx, n - n_ctx), 128)
    nbc, nb = n_ctx // tt, n // tt
    fwd0, fwd1, _, _ = _scan_maps(nbc, nb)

    def body(b0_ref, b1_ref, lam_ref, h0_ref, h1_ref, st_ref):
        @pl.when(pl.program_id(0) == 0)
        def _():
            st_ref[...] = jnp.zeros(st_ref.shape, F32)

        lam = [(lam_ref[0], lam_ref[1]), (lam_ref[2], lam_ref[3])]
        lam2 = [(lr * lr - li * li, 2.0 * lr * li) for lr, li in lam]

        def pair(b_ref, h_ref, low, up, lm, lm2, hr, hi):
            (lr, li), (l2r, l2i) = lm, lm2
            both = b_ref[pl.ds(low, 2)]
            first, second = (both[0], both[1]) if up else (both[1], both[0])
            bar, bai = first[0:s], first[s:s2]
            cr = lr * bar - li * bai + second[0:s]
            ci = lr * bai + li * bar + second[s:s2]
            mid = jnp.concatenate([lr * hr - li * hi + bar, lr * hi + li * hr + bai], axis=0)
            nr = l2r * hr - l2i * hi + cr
            ni = l2r * hi + l2i * hr + ci
            last = jnp.concatenate([nr, ni], axis=0)
            h_ref[pl.ds(low, 2)] = jnp.stack([mid, last] if up else [last, mid], axis=0)
            return nr, ni

        def step(jj, carry):
            h0r, h0i, h1r, h1i = carry
            ja = 2 * jj
            h0r, h0i = pair(b0_ref, h0_ref, ja, True, lam[0], lam2[0], h0r, h0i)
            h1r, h1i = pair(b1_ref, h1_ref, tt - 2 - ja, False, lam[1], lam2[1], h1r, h1i)
            return h0r, h0i, h1r, h1i

        out = lax.fori_loop(0, tt // 2, step, (st_ref[0], st_ref[1], st_ref[2], st_ref[3]), unroll=2)
        for q in range(4):
            st_ref[q] = out[q]

    blk = (tt, s2, LANES)
    return pl.pallas_call(
        body, grid=(nb,),
        in_specs=[pl.BlockSpec(blk, fwd0), pl.BlockSpec(blk, fwd1), pl.BlockSpec((4, s, LANES), lambda i: (0, 0, 0))],
        out_specs=[pl.BlockSpec(blk, fwd0), pl.BlockSpec(blk, fwd1)],
        out_shape=[SDS(bu0.shape, F32), SDS(bu1.shape, F32)],
        scratch_shapes=[pltpu.VMEM((4, s, LANES), F32)],
        compiler_params=_cparams("arbitrary"), name=name)(bu0, bu1, lam)


def _scan_bwd(dh0, dh1, h0, h1, lam, n_ctx, name):
    n, s2, _ = dh0.shape
    s = s2 // 2
    tt = math.gcd(math.gcd(n_ctx, n - n_ctx), 128)
    nbc, nb = n_ctx // tt, n // tt
    _, _, adj0, adj1 = _scan_maps(nbc, nb)

    def body(d0_ref, d1_ref, h0_ref, h1_ref, lam_ref, a0_ref, a1_ref, dl_ref, st_ref, acc_ref):
        i = pl.program_id(0)

        @pl.when(i == 0)
        def _():
            st_ref[...] = jnp.zeros(st_ref.shape, F32)
            acc_ref[...] = jnp.zeros(acc_ref.shape, F32)

        lam = [(lam_ref[0], lam_ref[1]), (lam_ref[2], lam_ref[3])]
        lam2 = [(lr * lr - li * li, 2.0 * lr * li) for lr, li in lam]

        def pair(d_ref, h_ref, a_ref, low, up, lm, lm2, ar, ai, cr, ci):
            (lr, li), (l2r, l2i) = lm, lm2
            dd, gg = d_ref[pl.ds(low, 2)], h_ref[pl.ds(low, 2)]
            (da, db), (ga, gb) = ((dd[0], dd[1]), (gg[0], gg[1])) if up else ((dd[1], dd[0]), (gg[1], gg[0]))
            dar, dai = da[0:s], da[s:s2]
            er = lr * dar + li * dai + db[0:s]
            ei = lr * dai - li * dar + db[s:s2]
            mr = lr * ar + li * ai + dar
            mi = lr * ai - li * ar + dai
            gar, gai, gbr, gbi = ga[0:s], ga[s:s2], gb[0:s], gb[s:s2]
            cr = cr + ((ar * gar + ai * gai) + (mr * gbr + mi * gbi))
            ci = ci + ((ai * gar - ar * gai) + (mi * gbr - mr * gbi))
            nr = l2r * ar + l2i * ai + er
            ni = l2r * ai - l2i * ar + ei
            mid, last = jnp.concatenate([mr, mi], axis=0), jnp.concatenate([nr, ni], axis=0)
            a_ref[pl.ds(low, 2)] = jnp.stack([mid, last] if up else [last, mid], axis=0)
            return nr, ni, cr, ci

        def step(jj, carry):
            a0r, a0i, a1r, a1i, c0r, c0i, c1r, c1i = carry
            ja = 2 * jj
            a0r, a0i, c0r, c0i = pair(d0_ref, h0_ref, a0_ref, tt - 2 - ja, False, lam[0], lam2[0], a0r, a0i, c0r, c0i)
            a1r, a1i, c1r, c1i = pair(d1_ref, h1_ref, a1_ref, ja, True, lam[1], lam2[1], a1r, a1i, c1r, c1i)
            return a0r, a0i, a1r, a1i, c0r, c0i, c1r, c1i

        init = tuple(st_ref[q] for q in range(4)) + tuple(acc_ref[q] for q in range(4))
        out = lax.fori_loop(0, tt // 2, step, init, unroll=2)
        for q in range(4):
            st_ref[q] = out[q]
            acc_ref[q] = out[4 + q]

        @pl.when(i == nb - 1)
        def _():
            for q in range(4):
                dl_ref[q] = out[4 + q]

    blk = (tt, s2, LANES)
    small = pl.BlockSpec((4, s, LANES), lambda i: (0, 0, 0))
    return pl.pallas_call(
        body, grid=(nb,),
        in_specs=[pl.BlockSpec(blk, adj0), pl.BlockSpec(blk, adj1), pl.BlockSpec(blk, adj0), pl.BlockSpec(blk, adj1), small],
        out_specs=[pl.BlockSpec(blk, adj0), pl.BlockSpec(blk, adj1), small],
        out_shape=[SDS(dh0.shape, F32), SDS(dh1.shape, F32), SDS((4, s, LANES), F32)],
        scratch_shapes=[pltpu.VMEM((4, s, LANES), F32), pltpu.VMEM((4, s, LANES), F32)],
        compiler_params=_cparams("arbitrary"), name=name)(dh0, dh1, h0, h1, lam)


def _gelu(v):
    th = jnp.tanh(GELU_C0 * (v + GELU_C1 * v * v * v))
    return 0.5 * v * (1.0 + th), th


def _ssm_head_fwd(y, u, ssm_d, wg, l, name):
    n, sw = y.shape
    ucol = u.shape[1] // sw - 1
    tm = _pick(n, _TM)

    def body(y_ref, u_ref, d_ref, w_ref, o_ref):
        act, _ = _gelu(y_ref[...] + d_ref[...] * u_ref[...])
        q = jnp.dot(act.astype(MXU_DTYPE), w_ref[...].astype(MXU_DTYPE), preferred_element_type=F32)
        o_ref[...] = (act * jax.nn.sigmoid(q)).astype(o_ref.dtype)

    row = pl.BlockSpec((tm, sw), lambda i: (i, 0))
    return pl.pallas_call(
        body, grid=(n // tm,),
        in_specs=[row, pl.BlockSpec((tm, sw), lambda i: (i, ucol)), pl.BlockSpec((1, sw), lambda i: (0, 0)),
                  pl.BlockSpec((None, sw, sw), lambda i: (l, 0, 0))],
        out_specs=row, out_shape=SDS((n, sw), MXU_DTYPE), compiler_params=_cparams("parallel"), name=name)(y, u, ssm_d, wg)


def _ssm_head_bwd(dcat, y, u, ssm_d, wg, l, name):
    n, sw = y.shape
    ucol = u.shape[1] // sw - 1
    tm = _pick(n, _TM)

    def body(do_ref, y_ref, u_ref, d_ref, w_ref, dy_ref, du_ref, act_ref, dq_ref, dd_ref):
        @pl.when(pl.program_id(0) == 0)
        def _():
            dd_ref[...] = jnp.zeros(dd_ref.shape, F32)

        uv, dv, do = u_ref[...], d_ref[...], do_ref[...]
        yf = y_ref[...] + dv * uv
        act, th = _gelu(yf)
        wv = w_ref[...].astype(MXU_DTYPE)
        sg = jax.nn.sigmoid(jnp.dot(act.astype(MXU_DTYPE), wv, preferred_element_type=F32))
        dq = (do * act * sg * (1.0 - sg)).astype(MXU_DTYPE)
        dact = do * sg + lax.dot_general(dq, wv, _DIMS["nt"], preferred_element_type=F32)
        dgelu = 0.5 * (1.0 + th) + 0.5 * yf * (1.0 - th * th) * GELU_C0 * (1.0 + 3.0 * GELU_C1 * yf * yf)
        dyf = dact * dgelu
        dy_ref[...] = dyf.astype(dy_ref.dtype)
        du_ref[...] = dyf * dv
        act_ref[...] = act.astype(act_ref.dtype)
        dq_ref[...] = dq
        dd_ref[...] += jnp.sum(dyf * uv, axis=0, keepdims=True)

    row = pl.BlockSpec((tm, sw), lambda i: (i, 0))
    last = pl.BlockSpec((tm, sw), lambda i: (i, ucol))
    vec = pl.BlockSpec((1, sw), lambda i: (0, 0))
    return pl.pallas_call(
        body, grid=(n // tm,),
        in_specs=[last, row, last, vec, pl.BlockSpec((None, sw, sw), lambda i: (l, 0, 0))],
        out_specs=[row, row, row, row, vec],
        out_shape=[SDS((n, sw), MXU_DTYPE), SDS((n, sw), F32), SDS((n, sw), MXU_DTYPE), SDS((n, sw), MXU_DTYPE),
                   SDS((1, sw), F32)],
        compiler_params=_cparams("arbitrary"), name=name)(dcat, y, u, ssm_d, wg)


def _assemble_du(du_pool, du_dir, du_proj, name):
    n, pw = du_pool.shape
    sw = du_dir.shape[1]
    tm = _pick(n, _TM)

    def body(p_ref, a_ref, b_ref, o_ref):
        o_ref[:, 0:pw] = p_ref[...].astype(o_ref.dtype)
        o_ref[:, pw:pw + sw] = (a_ref[...] + b_ref[...]).astype(o_ref.dtype)

    return pl.pallas_call(
        body, grid=(n // tm,),
        in_specs=[pl.BlockSpec((tm, pw), lambda i: (i, 0)), pl.BlockSpec((tm, sw), lambda i: (i, 0)),
                  pl.BlockSpec((tm, sw), lambda i: (i, 0))],
        out_specs=pl.BlockSpec((tm, pw + sw), lambda i: (i, 0)), out_shape=SDS((n, pw + sw), MXU_DTYPE),
        compiler_params=_cparams("parallel"), name=name)(du_pool, du_dir, du_proj)


CONV_PAD = GRID_W + SUBLANES


def _conv_layout(n, n_ctx):
    return CONV_PAD, 2 * CONV_PAD + n_ctx, 3 * CONV_PAD + n


def _col_masks(ch):
    col = lax.broadcasted_iota(jnp.int32, (ch, LANES), 0) % GRID_W
    return col != 0, col != GRID_W - 1


def _fill_padded(scr, src_ref, n, n_ctx):
    base_c, base_l, total = _conv_layout(n, n_ctx)
    for base in (0, base_c + n_ctx, base_l + n - n_ctx):
        scr[pl.ds(base, CONV_PAD), :] = jnp.zeros((CONV_PAD, LANES), F32)
    for row0, nseg, base in ((0, n_ctx, base_c), (n_ctx, n - n_ctx, base_l)):
        ch = math.gcd(nseg, 512)

        def copy(ci, carry, row0=row0, base=base, ch=ch):
            c0 = pl.multiple_of(ci * ch, ch)
            scr[pl.ds(base + c0, ch), :] = src_ref[pl.ds(row0 + c0, ch), :]
            return carry

        lax.fori_loop(0, nseg // ch, copy, 0)


def _conv_ctx(scr, k_ref, base, c0, ch, sign):
    acc = scr[pl.ds(c0 + base, ch), :] * k_ref[4:5, :]
    acc = acc + scr[pl.ds(c0 + (base - sign), ch), :] * k_ref[3:4, :]
    return acc + scr[pl.ds(c0 + (base + sign), ch), :] * k_ref[5:6, :]


def _conv_lat(scr, k_ref, base, c0, ch, sign, m_l, m_r):
    cols = []
    for j in range(3):
        acc = None
        for i in range(3):
            off = sign * (GRID_W * (i - 1) + (j - 1))
            term = scr[pl.ds(c0 + (base + off), ch), :] * k_ref[3 * i + j:3 * i + j + 1, :]
            acc = term if acc is None else acc + term
        cols.append(acc)
    first, last = (m_l, m_r) if sign > 0 else (m_r, m_l)
    return cols[1] + jnp.where(first, cols[0], 0.0) + jnp.where(last, cols[2], 0.0)


def _conv_chunk(n_lat):
    return math.gcd(n_lat, 256)


def _conv_glu_fwd(z, wk, n_ctx, name):
    n, f2 = z.shape
    dff = f2 // 2
    nvt = dff // LANES
    n_lat = n - n_ctx
    base_c, base_l, total = _conv_layout(n, n_ctx)
    ch = _conv_chunk(n_lat)
    assert ch % GRID_W == 0

    def body(zv_ref, zg_ref, kv_ref, kg_ref, a_ref, cv_ref, cg_ref, sv, sg):
        _fill_padded(sv, zv_ref, n, n_ctx)
        _fill_padded(sg, zg_ref, n, n_ctx)

        def emit(cv, cg, row, rows):
            cv_ref[pl.ds(row, rows), :] = cv
            cg_ref[pl.ds(row, rows), :] = cg
            a_ref[pl.ds(row, rows), :] = (cv * cg * jax.nn.sigmoid(cg)).astype(a_ref.dtype)

        emit(_conv_ctx(sv, kv_ref, base_c, 0, n_ctx, 1), _conv_ctx(sg, kg_ref, base_c, 0, n_ctx, 1), 0, n_ctx)
        m_l, m_r = _col_masks(ch)

        def lat(ci, carry):
            c0 = pl.multiple_of(ci * ch, ch)
            emit(_conv_lat(sv, kv_ref, base_l, c0, ch, 1, m_l, m_r), _conv_lat(sg, kg_ref, base_l, c0, ch, 1, m_l, m_r),
                 n_ctx + c0, ch)
            return carry

        lax.fori_loop(0, n_lat // ch, lat, 0)

    col = lambda shift: pl.BlockSpec((n, LANES), lambda j: (0, j + shift))
    kcol = lambda shift: pl.BlockSpec((9, LANES), lambda j: (0, j + shift))
    return pl.pallas_call(
        body, grid=(nvt,), in_specs=[col(0), col(nvt), kcol(0), kcol(nvt)], out_specs=[col(0), col(0), col(0)],
        out_shape=[SDS((n, dff), MXU_DTYPE), SDS((n, dff), F32), SDS((n, dff), F32)],
        scratch_shapes=[pltpu.VMEM((total, LANES), F32), pltpu.VMEM((total, LANES), F32)],
        compiler_params=_cparams("parallel"), name=name)(z, z, wk, wk)


def _conv_glu_bwd(z, cv, cg, da, wk, n_ctx, name):
    n, f2 = z.shape
    dff = f2 // 2
    nvt = dff // LANES
    n_lat = n - n_ctx
    base_c, base_l, total = _conv_layout(n, n_ctx)
    ch = _conv_chunk(n_lat)
    assert ch % GRID_W == 0
    ctx_taps = [(1, 0), (1, 1), (1, 2)]
    lat_taps = [(i, j) for i in range(3) for j in range(3)]

    def tap_sums(acc, scr, d, base, c0, rows, taps, masks):
        acc = list(acc)
        by_col = (d, d, d) if masks is None else (jnp.where(masks[0], d, 0.0), d, jnp.where(masks[1], d, 0.0))
        for i, j in taps:
            src = scr[pl.ds(c0 + (base + GRID_W * (i - 1) + (j - 1)), rows), :]
            acc[3 * i + j] = acc[3 * i + j] + jnp.sum((src * by_col[j]).reshape(rows // SUBLANES, SUBLANES, LANES), axis=0)
        return acc

    def body(zv_ref, zg_ref, cv_ref, cg_ref, da_ref, kv_ref, kg_ref, dz_ref, dkv_ref, dkg_ref, a_ref, sv, sg, dv, dg):
        _fill_padded(sv, zv_ref, n, n_ctx)
        _fill_padded(sg, zg_ref, n, n_ctx)
        for base in (0, base_c + n_ctx, base_l + n_lat):
            dv[pl.ds(base, CONV_PAD), :] = jnp.zeros((CONV_PAD, LANES), F32)
            dg[pl.ds(base, CONV_PAD), :] = jnp.zeros((CONV_PAD, LANES), F32)
        m_l, m_r = _col_masks(ch)

        def first_pass(row, pad_row, rows):
            cv, cg = cv_ref[pl.ds(row, rows), :], cg_ref[pl.ds(row, rows), :]
            sig = jax.nn.sigmoid(cg)
            silu = cg * sig
            a_ref[pl.ds(row, rows), :] = (cv * silu).astype(a_ref.dtype)
            dav = da_ref[pl.ds(row, rows), :]
            dcv = dav * silu
            dcg = dav * cv * (sig * (1.0 + cg * (1.0 - sig)))
            dv[pl.ds(pad_row, rows), :] = dcv
            dg[pl.ds(pad_row, rows), :] = dcg
            return dcv, dcg

        zero = [jnp.zeros((SUBLANES, LANES), F32) for _ in range(9)]
        dcv, dcg = first_pass(0, base_c, n_ctx)
        accv = tap_sums(zero, sv, dcv, base_c, 0, n_ctx, ctx_taps, None)
        accg = tap_sums(zero, sg, dcg, base_c, 0, n_ctx, ctx_taps, None)

        def lat1(ci, carry):
            accv, accg = carry
            c0 = pl.multiple_of(ci * ch, ch)
            dcv, dcg = first_pass(n_ctx + c0, base_l + c0, ch)
            accv = tap_sums(accv, sv, dcv, base_l, c0, ch, lat_taps, (m_l, m_r))
            accg = tap_sums(accg, sg, dcg, base_l, c0, ch, lat_taps, (m_l, m_r))
            return tuple(accv), tuple(accg)

        accv, accg = lax.fori_loop(0, n_lat // ch, lat1, (tuple(accv), tuple(accg)))
        for t in range(9):
            dkv_ref[t:t + 1, :] = jnp.sum(accv[t], axis=0, keepdims=True)
            dkg_ref[t:t + 1, :] = jnp.sum(accg[t], axis=0, keepdims=True)

        dz_ref[0, pl.ds(0, n_ctx), :] = _conv_ctx(dv, kv_ref, base_c, 0, n_ctx, -1).astype(dz_ref.dtype)
        dz_ref[1, pl.ds(0, n_ctx), :] = _conv_ctx(dg, kg_ref, base_c, 0, n_ctx, -1).astype(dz_ref.dtype)

        def lat2(ci, carry):
            c0 = pl.multiple_of(ci * ch, ch)
            dz_ref[0, pl.ds(n_ctx + c0, ch), :] = _conv_lat(dv, kv_ref, base_l, c0, ch, -1, m_l, m_r).astype(dz_ref.dtype)
            dz_ref[1, pl.ds(n_ctx + c0, ch), :] = _conv_lat(dg, kg_ref, base_l, c0, ch, -1, m_l, m_r).astype(dz_ref.dtype)
            return carry

        lax.fori_loop(0, n_lat // ch, lat2, 0)

    col = lambda shift: pl.BlockSpec((n, LANES), lambda j: (0, j + shift))
    kcol = lambda shift: pl.BlockSpec((9, LANES), lambda j: (0, j + shift))
    pad = pltpu.VMEM((total, LANES), F32)
    return pl.pallas_call(
        body, grid=(nvt,), in_specs=[col(0), col(nvt), col(0), col(0), col(0), kcol(0), kcol(nvt)],
        out_specs=[pl.BlockSpec((2, n, LANES), lambda j: (0, 0, j)), kcol(0), kcol(0), col(0)],
        out_shape=[SDS((2, n, dff), MXU_DTYPE), SDS((9, dff), F32), SDS((9, dff), F32), SDS((n, dff), MXU_DTYPE)],
        scratch_shapes=[pad, pad, pad, pad],
        compiler_params=_cparams("parallel"), name=name)(z, z, cv, cg, da, wk, wk)


def _silu(v):
    return v * jax.nn.sigmoid(v)


def _ada_fwd(cond, w_ada, b_shard, name):
    nl, d, cols = w_ada.shape
    tn = _pick(cols, (512, 256, 128))

    def body(c_ref, w_ref, b_ref, o_ref):
        o_ref[...] = jnp.dot(_silu(c_ref[...]), w_ref[...], precision=HIGHEST, preferred_element_type=F32) + b_ref[...]

    return pl.pallas_call(
        body, grid=(nl, cols // tn),
        in_specs=[pl.BlockSpec(cond.shape, lambda l, j: (0, 0)), pl.BlockSpec((None, d, tn), lambda l, j: (l, 0, j)),
                  pl.BlockSpec((None, 1, tn), lambda l, j: (l, 0, j))],
        out_specs=pl.BlockSpec((None, cond.shape[0], tn), lambda l, j: (l, 0, j)),
        out_shape=SDS((nl, cond.shape[0], cols), F32),
        compiler_params=_cparams("parallel", "parallel"), name=name)(cond, w_ada, b_shard)


def _ada_dw(cond, dmod, name):
    nl, rows, cols = dmod.shape
    d = cond.shape[1]
    tn = _pick(cols, (512, 256, 128))

    def body(c_ref, g_ref, o_ref):
        o_ref[...] = lax.dot_general(_silu(c_ref[...]), g_ref[...], _DIMS["tn"], precision=HIGHEST,
                                     preferred_element_type=F32)

    return pl.pallas_call(
        body, grid=(nl, cols // tn),
        in_specs=[pl.BlockSpec(cond.shape, lambda l, j: (0, 0)), pl.BlockSpec((None, rows, tn), lambda l, j: (l, 0, j))],
        out_specs=pl.BlockSpec((None, d, tn), lambda l, j: (l, 0, j)), out_shape=SDS((nl, d, cols), F32),
        compiler_params=_cparams("parallel", "parallel"), name=name)(cond, dmod)


def _ada_dcond(dmod, w_ada, name):
    nl, rows, cols = dmod.shape
    d = w_ada.shape[1]
    tn = _pick(cols, (512, 256, 128))

    def body(g_ref, w_ref, o_ref):
        @pl.when(jnp.logical_and(pl.program_id(0) == 0, pl.program_id(1) == 0))
        def _():
            o_ref[...] = jnp.zeros(o_ref.shape, F32)

        o_ref[...] += lax.dot_general(g_ref[...], w_ref[...], _DIMS["nt"], precision=HIGHEST, preferred_element_type=F32)

    return pl.pallas_call(
        body, grid=(nl, cols // tn),
        in_specs=[pl.BlockSpec((None, rows, tn), lambda l, j: (l, 0, j)), pl.BlockSpec((None, d, tn), lambda l, j: (l, 0, j))],
        out_specs=pl.BlockSpec((rows, d), lambda l, j: (0, 0)), out_shape=SDS((rows, d), F32),
        compiler_params=_cparams("arbitrary", "arbitrary"), name=name)(dmod, w_ada)


def _ada_rows(dmod_all, name):
    nd, nl, _, w = dmod_all.shape
    tn = _pick(w, (2048, 1024, 512, 256, 128))

    def body(g_ref, rows_ref, db_ref):
        ctx = g_ref[0, 0, 0:1, :]
        for b in range(1, nd):
            ctx = ctx + g_ref[b, 0, 0:1, :]
        total = ctx
        for b in range(nd):
            lat = g_ref[b, 0, 1:2, :]
            rows_ref[b:b + 1, :] = lat
            total = total + lat
        rows_ref[nd:nd + 1, :] = ctx
        rows_ref[nd + 1:16, :] = jnp.zeros((16 - nd - 1, tn), F32)
        db_ref[...] = total

    return pl.pallas_call(
        body, grid=(nl, w // tn),
        in_specs=[pl.BlockSpec((nd, 1, 2, tn), lambda l, j: (0, l, 0, j))],
        out_specs=[pl.BlockSpec((None, 16, tn), lambda l, j: (l, 0, j)), pl.BlockSpec((None, 1, tn), lambda l, j: (l, 0, j))],
        out_shape=[SDS((nl, 16, w), F32), SDS((nl, 1, w), F32)],
        compiler_params=_cparams("parallel", "parallel"), name=name)(dmod_all)


def _ada_dctx(parts, c_ctx, row, name):
    def body(p_ref, c_ref, o_ref):
        ds = p_ref[0, row:row + 1, :]
        for k in range(1, p_ref.shape[0]):
            ds = ds + p_ref[k, row:row + 1, :]
        cv = c_ref[...]
        sg = jax.nn.sigmoid(cv)
        o_ref[...] = ds * (sg * (1.0 + cv * (1.0 - sg)))

    return pl.pallas_call(body, out_shape=SDS(c_ctx.shape, F32), compiler_params=_cparams(), name=name)(parts, c_ctx)


ROW_BLOCK_BYTES = 1 << 20


def _as_rows(shape):
    size = math.prod(shape)
    cols = shape[-1] if len(shape) >= 2 and shape[-1] % LANES == 0 else _pick(size, (1024, 512, 256, 128))
    rows = size // cols
    fits = [t for t in (512, 256, 128, 64, 32, 16, 8) if t * cols * 4 <= ROW_BLOCK_BYTES]
    return rows, cols, _pick(rows, fits)


def _adamw(w, g, m, v, name):
    rows, cols, tr = _as_rows(w.shape)
    c1 = 1.0 / (1.0 - ADAM_B1 ** ADAM_STEP)
    c2 = 1.0 / (1.0 - ADAM_B2 ** ADAM_STEP)

    def body(w_ref, g_ref, m_ref, v_ref, d_ref, nm_ref, nv_ref):
        gv = g_ref[...]
        nm = ADAM_B1 * m_ref[...] + (1.0 - ADAM_B1) * gv
        nv = ADAM_B2 * v_ref[...] + (1.0 - ADAM_B2) * (gv * gv)
        nm_ref[...] = nm
        nv_ref[...] = nv
        d_ref[...] = -ADAM_LR * ((nm * c1) / (jnp.sqrt(nv * c2) + ADAM_EPS) + ADAM_WD * w_ref[...])

    blk = pl.BlockSpec((tr, cols), lambda i: (i, 0))
    outs = pl.pallas_call(
        body, grid=(rows // tr,), in_specs=[blk] * 4, out_specs=[blk] * 3, out_shape=[SDS((rows, cols), F32)] * 3,
        compiler_params=_cparams("parallel"), name=name)(*[t.reshape(rows, cols) for t in (w, g, m, v)])
    return tuple(o.reshape(w.shape) for o in outs)


def _tile_rows(rows, cols, dtype):
    size = jnp.dtype(dtype).itemsize
    fits = [t for t in (512, 256, 128, 64, 32, 16, 8) if t * cols * size <= ROW_BLOCK_BYTES and t * size >= 32]
    return _pick(rows, fits)


def _scalar_spec(grid, in_specs, out_specs):
    return pltpu.PrefetchScalarGridSpec(num_scalar_prefetch=1, grid=grid, in_specs=in_specs, out_specs=out_specs)


def _place_chunk(shard, k_idx, dtype, name):
    nl, rows, cols = shard.shape
    tr = _tile_rows(rows, cols, dtype)

    def body(k_ref, s_ref, o_ref):
        o_ref[...] = s_ref[...].astype(o_ref.dtype)

    return pl.pallas_call(
        body, out_shape=SDS((nl, N_CHIPS, rows, cols), dtype),
        grid_spec=_scalar_spec((nl, rows // tr), [pl.BlockSpec((None, tr, cols), lambda l, i, k: (l, i, 0))],
                               pl.BlockSpec((None, None, tr, cols), lambda l, i, k: (l, k[0], i, 0))),
        compiler_params=_cparams("parallel", "parallel"), name=name)(k_idx, shard)


def _pair_sum(grads, recv, c_idx, name):
    half, nch, rows, cols = recv.shape
    tr = _tile_rows(rows, cols, recv.dtype)

    def body(c_ref, g_ref, r_ref, o_ref):
        o_ref[...] = (g_ref[...].astype(F32) + r_ref[...].astype(F32)).astype(o_ref.dtype)

    blk = pl.BlockSpec((None, None, tr, cols), lambda h, q, i, c: (h, q, i, 0))
    return pl.pallas_call(
        body, out_shape=SDS(recv.shape, recv.dtype),
        grid_spec=_scalar_spec((half, nch, rows // tr),
                               [pl.BlockSpec((None, None, tr, cols), lambda h, q, i, c: (c[0] * half + h, q, i, 0)), blk], blk),
        compiler_params=_cparams("parallel", "parallel", "parallel"), name=name)(c_idx, grads, recv)


def _chip_sum(parts, recv, kc_idx, name):
    half, nch, rows, cols = parts.shape
    tr = _tile_rows(rows, cols, F32)

    def body(kc_ref, p_ref, r_ref, o_ref):
        acc = p_ref[...].astype(F32)
        for s in range(r_ref.shape[0]):
            acc = acc + r_ref[s].astype(F32)
        o_ref[...] = acc

    return pl.pallas_call(
        body, out_shape=SDS((2 * half, rows, cols), F32),
        grid_spec=_scalar_spec((half, rows // tr),
                               [pl.BlockSpec((None, None, tr, cols), lambda h, i, kc: (h, kc[0], i, 0)),
                                pl.BlockSpec((nch - 1, None, tr, cols), lambda h, i, kc: (0, h, i, 0))],
                               pl.BlockSpec((None, tr, cols), lambda h, i, kc: (kc[1] * half + h, i, 0))),
        compiler_params=_cparams("parallel", "parallel"), name=name)(kc_idx, parts, recv)


PIECE_BYTES = 3 << 20
MAX_PIECES = 16
PIECE_ROW_ALIGN = 16


def _coords():
    return lax.axis_index("x"), lax.axis_index("y"), lax.axis_index("c")


def _other_chips(x, y):
    return [(1 - x, y), (x, 1 - y), (1 - x, 1 - y)]


def _row_pieces(rows, nbytes):
    pieces = 1
    while (pieces < MAX_PIECES and nbytes // pieces > PIECE_BYTES and rows % (2 * pieces * PIECE_ROW_ALIGN) == 0):
        pieces *= 2
    step = rows // pieces
    return [pl.ds(i * step, step) for i in range(pieces)]


def _nbytes(shape, dtype):
    return math.prod(shape) * jnp.dtype(dtype).itemsize


def _offsets(counts):
    out, pos = [], 0
    for cnt in counts:
        out.append(pos)
        pos += cnt
    return out, pos


def _gather_chips(placed, name):
    nt = len(placed)
    half = [p.shape[0] // 2 for p in placed]
    pieces = [_row_pieces(p.shape[2], _nbytes((h,) + p.shape[2:], p.dtype)) for p, h in zip(placed, half)]
    base, total = _offsets([len(p) for p in pieces])

    def body(*refs):
        o_refs = refs[nt:2 * nt]
        s_nbr, r_nbr, s_fwd, r_fwd, s_sib, r_sib = refs[2 * nt:]
        x, y, c = _coords()
        k, kx, ky, kd = 2 * x + y, 2 * (1 - x) + y, 2 * x + (1 - y), 2 * (1 - x) + (1 - y)
        across_x, across_y, sibling = (1 - x, y, c), (x, 1 - y, c), (x, y, 1 - c)
        sends = []

        def copy(o_ref, rows, slot, rs, ssem, rsem, q, to):
            return pltpu.make_async_remote_copy(
                src_ref=o_ref.at[rows, slot, rs], dst_ref=o_ref.at[rows, slot, rs], send_sem=ssem.at[q], recv_sem=rsem.at[q],
                device_id=to, device_id_type=MESH)

        def start(cp):
            cp.start()
            sends.append(cp)

        work = [(t, i, rs, base[t] + i, 2 * i < len(pieces[t]) or len(pieces[t]) == 1)
                for t in range(nt) for i, rs in enumerate(pieces[t])]
        for t, i, rs, q, _ in work:
            mine = pl.ds(c * half[t], half[t])
            start(copy(o_refs[t], mine, k, rs, s_nbr, r_nbr, 2 * q, across_x))
            start(copy(o_refs[t], mine, k, rs, s_nbr, r_nbr, 2 * q + 1, across_y))
        for t, i, rs, q, via_x in work:
            mine = pl.ds(c * half[t], half[t])
            copy(o_refs[t], mine, kx, rs, s_nbr, r_nbr, 2 * q, across_x).wait_recv()
            start(copy(o_refs[t], mine, kx, rs, s_sib, r_sib, 3 * q, sibling))
            if not via_x:
                start(copy(o_refs[t], mine, kx, rs, s_fwd, r_fwd, q, across_y))
            copy(o_refs[t], mine, ky, rs, s_nbr, r_nbr, 2 * q + 1, across_y).wait_recv()
            start(copy(o_refs[t], mine, ky, rs, s_sib, r_sib, 3 * q + 1, sibling))
            if via_x:
                start(copy(o_refs[t], mine, ky, rs, s_fwd, r_fwd, q, across_x))
        for t, i, rs, q, via_x in work:
            mine = pl.ds(c * half[t], half[t])
            copy(o_refs[t], mine, kd, rs, s_fwd, r_fwd, q, across_x if via_x else across_y).wait_recv()
            start(copy(o_refs[t], mine, kd, rs, s_sib, r_sib, 3 * q + 2, sibling))
        for t, i, rs, q, _ in work:
            theirs = pl.ds((1 - c) * half[t], half[t])
            for r, slot in enumerate((kx, ky, kd)):
                copy(o_refs[t], theirs, slot, rs, s_sib, r_sib, 3 * q + r, sibling).wait_recv()
        for cp in sends:
            cp.wait_send()

    sem = pltpu.SemaphoreType.DMA
    outs = pl.pallas_call(
        body, in_specs=[ANY] * nt, out_specs=[ANY] * nt,
        out_shape=[SDS(p.shape, p.dtype) for p in placed],
        input_output_aliases={t: t for t in range(nt)},
        scratch_shapes=[sem((2 * total,)), sem((2 * total,)), sem((total,)), sem((total,)), sem((3 * total,)),
                        sem((3 * total,))],
        name=name)(*placed)
    return list(outs)


def _pair_send(grads, name):
    nt = len(grads)
    half = [g.shape[0] // 2 for g in grads]
    pieces = [_row_pieces(g.shape[2], _nbytes((h,) + g.shape[1:], g.dtype)) for g, h in zip(grads, half)]
    base, total = _offsets([len(p) for p in pieces])

    def body(*refs):
        g_refs, o_refs = refs[:nt], refs[nt:2 * nt]
        ssem, rsem = refs[2 * nt:]
        x, y, c = _coords()
        cps = []
        for t in range(nt):
            theirs = pl.ds((1 - c) * half[t], half[t])
            for i, rs in enumerate(pieces[t]):
                q = base[t] + i
                cp = pltpu.make_async_remote_copy(
                    src_ref=g_refs[t].at[theirs, :, rs], dst_ref=o_refs[t].at[:, :, rs], send_sem=ssem.at[q],
                    recv_sem=rsem.at[q], device_id=(x, y, 1 - c), device_id_type=MESH)
                cp.start()
                cps.append(cp)
        for cp in cps:
            cp.wait_recv()
        for cp in cps:
            cp.wait_send()

    sem = pltpu.SemaphoreType.DMA
    outs = pl.pallas_call(
        body, in_specs=[ANY] * nt, out_specs=[ANY] * nt,
        out_shape=[SDS((g.shape[0] // 2,) + g.shape[1:], g.dtype) for g in grads],
        scratch_shapes=[sem((total,)), sem((total,))],
        name=name)(*grads)
    return list(outs)


def _chip_send(parts, name):
    nt = len(parts)
    pieces = [_row_pieces(p.shape[2], _nbytes((p.shape[0],) + p.shape[2:], p.dtype)) for p in parts]
    base, total = _offsets([len(p) for p in pieces])

    def body(*refs):
        p_refs, o_refs = refs[:nt], refs[nt:2 * nt]
        ssem, rsem = refs[2 * nt:]
        x, y, c = _coords()
        cps = []
        for t in range(nt):
            for i, rs in enumerate(pieces[t]):
                for r, (px, py) in enumerate(_other_chips(x, y)):
                    q = 3 * (base[t] + i) + r
                    cp = pltpu.make_async_remote_copy(
                        src_ref=p_refs[t].at[:, 2 * px + py, rs], dst_ref=o_refs[t].at[r, :, rs], send_sem=ssem.at[q],
                        recv_sem=rsem.at[q], device_id=(px, py, c), device_id_type=MESH)
                    cp.start()
                    cps.append(cp)
        for cp in cps:
            cp.wait_recv()
        for cp in cps:
            cp.wait_send()

    sem = pltpu.SemaphoreType.DMA
    outs = pl.pallas_call(
        body, in_specs=[ANY] * nt, out_specs=[ANY] * nt,
        out_shape=[SDS((N_CHIPS - 1, p.shape[0]) + p.shape[2:], p.dtype) for p in parts],
        scratch_shapes=[sem((3 * total,)), sem((3 * total,))],
        name=name)(*parts)
    return list(outs)


def _pair_join(bufs, name):
    nt = len(bufs)
    half = [b.shape[0] // 2 for b in bufs]
    pieces = [_row_pieces(b.shape[1], _nbytes((h,) + b.shape[1:], b.dtype)) for b, h in zip(bufs, half)]
    base, total = _offsets([len(p) for p in pieces])

    def body(*refs):
        o_refs = refs[nt:2 * nt]
        ssem, rsem = refs[2 * nt:]
        x, y, c = _coords()
        cps = []
        for t in range(nt):
            mine = pl.ds(c * half[t], half[t])
            for i, rs in enumerate(pieces[t]):
                q = base[t] + i
                cp = pltpu.make_async_remote_copy(
                    src_ref=o_refs[t].at[mine, rs], dst_ref=o_refs[t].at[mine, rs], send_sem=ssem.at[q],
                    recv_sem=rsem.at[q], device_id=(x, y, 1 - c), device_id_type=MESH)
                cp.start()
                cps.append(cp)
        for t in range(nt):
            theirs = pl.ds((1 - c) * half[t], half[t])
            for i, rs in enumerate(pieces[t]):
                q = base[t] + i
                pltpu.make_async_remote_copy(
                    src_ref=o_refs[t].at[theirs, rs], dst_ref=o_refs[t].at[theirs, rs], send_sem=ssem.at[q],
                    recv_sem=rsem.at[q], device_id=(x, y, 1 - c), device_id_type=MESH).wait_recv()
        for cp in cps:
            cp.wait_send()

    sem = pltpu.SemaphoreType.DMA
    outs = pl.pallas_call(
        body, in_specs=[ANY] * nt, out_specs=[ANY] * nt,
        out_shape=[SDS(b.shape, b.dtype) for b in bufs],
        input_output_aliases={t: t for t in range(nt)},
        scratch_shapes=[sem((total,)), sem((total,))],
        name=name)(*bufs)
    return list(outs)


def _gather_devices(vals, name):
    nt = len(vals)
    flips = [(a, b, e) for a in (0, 1) for b in (0, 1) for e in (0, 1)][1:]

    def body(*refs):
        v_refs, o_refs = refs[:nt], refs[nt:2 * nt]
        lsem, ssem, rsem = refs[2 * nt:]
        x, y, c = _coords()
        me = 4 * x + 2 * y + c
        peers = [((1 - x) if a else x, (1 - y) if b else y, (1 - c) if e else c) for a, b, e in flips]
        cps = []
        for t in range(nt):
            loc = pltpu.make_async_copy(v_refs[t], o_refs[t].at[me], lsem.at[t])
            loc.start()
            cps.append(loc)
            for r, peer in enumerate(peers):
                cp = pltpu.make_async_remote_copy(
                    src_ref=v_refs[t], dst_ref=o_refs[t].at[me], send_sem=ssem.at[7 * t + r],
                    recv_sem=rsem.at[7 * t + r], device_id=peer, device_id_type=MESH)
                cp.start()
                cps.append(cp)
        for t in range(nt):
            for r, (px, py, pc) in enumerate(peers):
                pltpu.make_async_remote_copy(
                    src_ref=v_refs[t], dst_ref=o_refs[t].at[4 * px + 2 * py + pc], send_sem=ssem.at[7 * t + r],
                    recv_sem=rsem.at[7 * t + r], device_id=(px, py, pc), device_id_type=MESH).wait_recv()
        for t in range(nt):
            cps[8 * t].wait()
            for r in range(7):
                cps[8 * t + 1 + r].wait_send()

    sem = pltpu.SemaphoreType.DMA
    outs = pl.pallas_call(
        body, in_specs=[ANY] * nt, out_specs=[ANY] * nt,
        out_shape=[SDS((N_DEV,) + v.shape, v.dtype) for v in vals],
        scratch_shapes=[sem((nt,)), sem((7 * nt,)), sem((7 * nt,))],
        name=name)(*vals)
    return list(outs)


def _gather_all(shards, dtypes, k_idx, tag):
    placed = [_place_chunk(s, k_idx, dt, f"{tag}_place{t}") for t, (s, dt) in enumerate(zip(shards, dtypes))]
    return _gather_chips(placed, f"{tag}_gather")


def _reduce_to_shards(grads, k_idx, c_idx, kc_idx, tag):
    recv = _pair_send(grads, f"{tag}_pair_send")
    pair = [_pair_sum(g, r, c_idx, f"{tag}_pair_sum{t}") for t, (g, r) in enumerate(zip(grads, recv))]
    recv = _chip_send(pair, f"{tag}_chip_send")
    bufs = [_chip_sum(p, r, kc_idx, f"{tag}_chip_sum{t}") for t, (p, r) in enumerate(zip(pair, recv))]
    return _pair_join(bufs, f"{tag}_pair_join")


WEIGHT_NAMES = ("c_ctx", "w_ada", "b_ada", "w_in", "w_pool", "pool_scale", "ssm_a_re", "ssm_a_im", "ssm_log_dt",
                "ssm_b_re", "ssm_b_im", "ssm_c_re", "ssm_c_im", "ssm_d", "w_glu", "w_out", "g_pre_mix", "g_post_mix",
                "g_pre_ffn", "g_post_ffn", "w_up", "w_conv", "w_down")


def _block_diag_in(bb, ng):
    nl, nd, npart, h, gp = bb.shape
    p = gp // ng
    w = jnp.einsum("ldqhgp,kg->lkhdqgp", bb.reshape(nl, nd, npart, h, ng, p), jnp.eye(ng, dtype=bb.dtype))
    return w.reshape(nl, ng * h, nd * npart * gp)


def _diag_in_grad(dw, ng, nh, p):
    gl = ng // S5_BAND
    out = jnp.einsum("ghqagp->qhagp", dw.reshape(gl, nh, 2, S5_BAND, gl, p))
    return out.reshape(2, nh, ng * p)


def _block_diag_out(cs, ng):
    nl, nd, npart, _, h, p = cs.shape
    w = jnp.einsum("ldqghp,kg->ldqkpgh", cs, jnp.eye(ng, dtype=cs.dtype))
    return w.reshape(nl, nd * npart * ng * p, ng * h)


def _diag_out_grad(dw, ng, nh, p):
    gl = ng // S5_BAND
    out = jnp.einsum("qagpgh->qaghp", dw.reshape(2, S5_BAND, gl, p, gl, nh))
    return out.reshape(2, ng, nh, p)


def kernel(x, c, ctx, c_ctx, w_ada, b_ada, w_in, w_pool, pool_scale, ssm_a_re, ssm_a_im, ssm_log_dt, ssm_b_re, ssm_b_im, ssm_c_re, ssm_c_im, ssm_d, w_glu, w_out, g_pre_mix, g_post_mix, g_pre_ffn, g_post_ffn, w_up, w_conv, w_down, loss_target, m_c_ctx, m_w_ada, m_b_ada, m_w_in, m_w_pool, m_pool_scale, m_ssm_a_re, m_ssm_a_im, m_ssm_log_dt, m_ssm_b_re, m_ssm_b_im, m_ssm_c_re, m_ssm_c_im, m_ssm_d, m_w_glu, m_w_out, m_g_pre_mix, m_g_post_mix, m_g_pre_ffn, m_g_post_ffn, m_w_up, m_w_conv, m_w_down, v_c_ctx, v_w_ada, v_b_ada, v_w_in, v_w_pool, v_pool_scale, v_ssm_a_re, v_ssm_a_im, v_ssm_log_dt, v_ssm_b_re, v_ssm_b_im, v_ssm_c_re, v_ssm_c_im, v_ssm_d, v_w_glu, v_w_out, v_g_pre_mix, v_g_post_mix, v_g_pre_ffn, v_g_post_ffn, v_w_up, v_w_conv, v_w_down):
    weights = dict(zip(WEIGHT_NAMES, (c_ctx, w_ada, b_ada, w_in, w_pool, pool_scale, ssm_a_re, ssm_a_im, ssm_log_dt,
                                      ssm_b_re, ssm_b_im, ssm_c_re, ssm_c_im, ssm_d, w_glu, w_out, g_pre_mix, g_post_mix,
                                      g_pre_ffn, g_post_ffn, w_up, w_conv, w_down)))
    mom1 = dict(zip(WEIGHT_NAMES, (m_c_ctx, m_w_ada, m_b_ada, m_w_in, m_w_pool, m_pool_scale, m_ssm_a_re, m_ssm_a_im,
                                   m_ssm_log_dt, m_ssm_b_re, m_ssm_b_im, m_ssm_c_re, m_ssm_c_im, m_ssm_d, m_w_glu, m_w_out,
                                   m_g_pre_mix, m_g_post_mix, m_g_pre_ffn, m_g_post_ffn, m_w_up, m_w_conv, m_w_down)))
    mom2 = dict(zip(WEIGHT_NAMES, (v_c_ctx, v_w_ada, v_b_ada, v_w_in, v_w_pool, v_pool_scale, v_ssm_a_re, v_ssm_a_im,
                                   v_ssm_log_dt, v_ssm_b_re, v_ssm_b_im, v_ssm_c_re, v_ssm_c_im, v_ssm_d, v_w_glu, v_w_out,
                                   v_g_pre_mix, v_g_post_mix, v_g_pre_ffn, v_g_post_ffn, v_w_up, v_w_conv, v_w_down)))

    xi, yi, ci = lax.axis_index("x"), lax.axis_index("y"), lax.axis_index("c")
    chip = 2 * xi + yi
    dev = 4 * xi + 2 * yi + ci
    nl = w_in.shape[0]
    n_lat, d = x.shape[1], x.shape[2]
    n_ctx = ctx.shape[1]
    n = n_ctx + n_lat
    _, ndir, ng, nstate, nh = ssm_b_re.shape
    gp = ng * nstate
    sw = ng * nh
    n_pool_groups, pool_group = w_pool.shape[1], w_pool.shape[3]
    pw = n_pool_groups * pool_group
    assert pw + sw == d and pw % sw == 0 and len(POOL_WINDOWS) == n_pool_groups and n_lat % GRID_W == 0
    dff2 = w_up.shape[2] * N_CHIPS
    ada_w = w_ada.shape[2] * N_CHIPS
    ada_cols = w_ada.shape[2]
    s_rows = gp // LANES

    c_pad = jnp.concatenate([c, jnp.zeros((SUBLANES - 1, d), F32)], axis=0)
    c_all = _gather_devices([c_pad], "gather_cond")[0][:, 0, :]
    cond = jnp.concatenate([c_all, c_ctx[None, :], jnp.zeros((16 - N_DEV - 1, d), F32)], axis=0)
    b_shard = lax.dynamic_slice_in_dim(b_ada, chip * ada_cols, ada_cols, axis=1)[:, None, :]
    mod_shard = _ada_fwd(cond, w_ada, b_shard, "ada_fwd")
    k_idx, c_idx, kc_idx = jnp.stack([chip]), jnp.stack([ci]), jnp.stack([chip, ci])
    mod_all = _gather_all([mod_shard], [F32], k_idx, "mods")[0]
    mod_all = jnp.transpose(mod_all, (0, 2, 1, 3)).reshape(nl, 16, ada_w)
    mod_lat = lax.dynamic_index_in_dim(mod_all, dev, axis=1, keepdims=False).reshape(nl, 6, d)
    mod_ctx = mod_all[:, N_DEV].reshape(nl, 6, d)
    mods = jnp.concatenate([jnp.stack([mod_ctx, mod_lat], axis=1), jnp.zeros((nl, 2, 2, d), F32)], axis=2)

    shards = [w_in, w_pool.reshape(nl, pw // N_CHIPS, pool_group), w_glu, w_out, w_up, w_down,
              w_conv.reshape(nl, 9, dff2 // N_CHIPS)]
    g_in, g_pool, g_glu, g_out, g_up, g_down, g_conv = _gather_all(shards, [COMM_DTYPE] * 6 + [F32], k_idx, "weights")
    wi = g_in.reshape(nl, d, d)
    wp = jnp.transpose(g_pool.reshape(nl, N_CHIPS, n_pool_groups, pool_group // N_CHIPS, pool_group),
                       (0, 2, 1, 3, 4)).reshape(nl, n_pool_groups, pool_group, pool_group)
    wg = g_glu.reshape(nl, sw, sw)
    wo = g_out.reshape(nl, d, d)
    wu = g_up
    wd = g_down.reshape(nl, dff2 // 2, d)
    wk = jnp.transpose(g_conv, (0, 2, 1, 3)).reshape(nl, 9, dff2)

    rows = nl * ndir
    a_re2 = ssm_a_re.reshape(rows, gp)
    a_im2 = ssm_a_im.reshape(rows, gp)
    logdt2 = jnp.repeat(ssm_log_dt.reshape(rows, ng), nstate, axis=1)
    b_re2 = jnp.transpose(ssm_b_re.reshape(rows, gp, nh), (0, 2, 1))
    b_im2 = jnp.transpose(ssm_b_im.reshape(rows, gp, nh), (0, 2, 1))
    lam_re, lam_im, bb_re, bb_im = _disc_fwd(a_re2, a_im2, logdt2, b_re2, b_im2, "s5_discretise")
    lam = jnp.stack([lam_re.reshape(nl, ndir, s_rows, LANES), lam_im.reshape(nl, ndir, s_rows, LANES)], axis=2)
    lam = lam.reshape(nl, 2 * ndir, s_rows, LANES)
    bbs = jnp.stack([bb_re.reshape(nl, ndir, nh, gp), bb_im.reshape(nl, ndir, nh, gp)], axis=2)
    w_b = _block_diag_in(bbs.astype(MXU_DTYPE), ng)
    w_b = [w_b[:, :, dr * 2 * gp:(dr + 1) * 2 * gp] for dr in range(ndir)]
    cs = jnp.stack([ssm_c_re, -ssm_c_im], axis=2)
    w_c = _block_diag_out(cs.astype(MXU_DTYPE), ng)
    w_c = [w_c[:, dr * 2 * gp:(dr + 1) * 2 * gp] for dr in range(ndir)]

    def row(v, l):
        return v[l:l + 1]

    xc = jnp.concatenate([ctx[0], x[0]], axis=0)
    saved = []
    for l in range(nl):
        t = f"l{l}"
        md = mods[l]
        h = _norm_mod_fwd(xc, row(g_pre_mix, l), md, 0, 1, n_ctx, f"{t}_pre_mix")
        u = _mm(h, wi, "nn", F32, f"{t}_in_proj", b_idx=l)
        p = _pool(u, pw, pool_group, n_ctx, False, MXU_DTYPE, f"{t}_pool")
        ypool = _pool_proj_fwd(p, wp, l, row(pool_scale, l), f"{t}_pool_proj")
        bu0 = _mm_s5(u, w_b[0], "in", f"{t}_s5_in0", b_idx=l, a_cols=(pw, sw))
        bu1 = _mm_s5(u, w_b[1], "in", f"{t}_s5_in1", b_idx=l, a_cols=(pw, sw))
        h0, h1 = _scan_fwd(bu0, bu1, lam[l], n_ctx, f"{t}_scan")
        y = _mm_s5(h0, w_c[0], "out", f"{t}_s5_out0", b_idx=l)
        y = _mm_s5(h1, w_c[1], "out", f"{t}_s5_out1", b_idx=l, add=y)
        s_out = _ssm_head_fwd(y, u, row(ssm_d, l), wg, l, f"{t}_s5_head")
        cat = jnp.concatenate([ypool, s_out], axis=1)
        mix = _mm(cat, wo, "nn", F32, f"{t}_out_proj", b_idx=l)
        x_mid = _gate_res_fwd(xc, mix, row(g_post_mix, l), md, 2, n_ctx, f"{t}_post_mix")
        h2 = _norm_mod_fwd(x_mid, row(g_pre_ffn, l), md, 3, 4, n_ctx, f"{t}_pre_ffn")
        z = _mm(h2, wu, "nn", F32, f"{t}_up", b_idx=l, b_chunks=N_CHIPS)
        act, cv, cg = _conv_glu_fwd(z, wk[l], n_ctx, f"{t}_conv_glu")
        f = _mm(act, wd, "nn", F32, f"{t}_down", b_idx=l)
        x_out = _gate_res_fwd(x_mid, f, row(g_post_ffn, l), md, 5, n_ctx, f"{t}_post_ffn")
        saved.append(dict(xc=xc, h=h, u=u, p=p, h0=h0, h1=h1, y=y, cat=cat, mix=mix, x_mid=x_mid, h2=h2, z=z, cv=cv, cg=cg, f=f))
        xc = x_out

    dx, loss_tile = _loss_grad(xc, loss_target[0], n_ctx, "loss")
    loss = lax.psum(loss_tile[0, 0], ("x", "y", "c"))

    big = {k: [None] * nl for k in ("w_in", "w_pool", "w_glu", "w_out", "w_up", "w_down")}
    small = {k: [None] * nl for k in ("pool_scale", "ssm_d", "g_pre_mix", "g_post_mix", "g_pre_ffn", "g_post_ffn",
                                      "lam", "bb", "cs", "w_conv")}
    dmods = [None] * nl
    for l in reversed(range(nl)):
        t = f"l{l}b"
        md = mods[l]
        sv = saved[l]
        df, dgate_ffn, small["g_post_ffn"][l] = _gate_res_bwd(dx, sv["f"], row(g_post_ffn, l), md, 5, n_ctx, f"{t}_post_ffn")
        dact = _mm(df, wd, "nt", F32, f"{t}_down_dx", b_idx=l)
        dz, dkv, dkg, act = _conv_glu_bwd(sv["z"], sv["cv"], sv["cg"], dact, wk[l], n_ctx, f"{t}_conv_glu")
        small["w_conv"][l] = jnp.concatenate([dkv, dkg], axis=1)
        big["w_down"][l] = _mm(act, df, "tn", COMM_DTYPE, f"{t}_down_dw")
        big["w_up"][l] = _mm(sv["h2"], dz, "tn", COMM_DTYPE, f"{t}_up_dw", b_chunks=2, out_chunks=N_CHIPS)
        dh2 = _mm(dz, wu, "nt", F32, f"{t}_up_dx", b_idx=l, a_chunks=2, b_chunks=N_CHIPS)
        dx, dss_ffn, small["g_pre_ffn"][l] = _norm_mod_bwd(dh2, sv["x_mid"], row(g_pre_ffn, l), md, 3, 4, dx, n_ctx,
                                                           f"{t}_pre_ffn")
        dmix, dgate_mix, small["g_post_mix"][l] = _gate_res_bwd(dx, sv["mix"], row(g_post_mix, l), md, 2, n_ctx,
                                                                f"{t}_post_mix")
        dcat = _mm(dmix, wo, "nt", F32, f"{t}_out_dx", b_idx=l)
        big["w_out"][l] = _mm(sv["cat"], dmix, "tn", COMM_DTYPE, f"{t}_out_dw")
        dp, small["pool_scale"][l], big["w_pool"][l] = _pool_proj_bwd(sv["p"], dcat, wp, l, row(pool_scale, l),
                                                                      f"{t}_pool_proj")
        du_pool = _pool(dp, pw, pool_group, n_ctx, True, F32, f"{t}_pool")
        dy, du_dir, gact, dq, small["ssm_d"][l] = _ssm_head_bwd(dcat, sv["y"], sv["u"], row(ssm_d, l), wg, l, f"{t}_s5_head")
        big["w_glu"][l] = _mm(gact, dq, "tn", COMM_DTYPE, f"{t}_glu_dw")
        dh0 = _mm_s5(dy, w_c[0], "out_dx", f"{t}_s5_out_dx0", b_idx=l)
        dh1 = _mm_s5(dy, w_c[1], "out_dx", f"{t}_s5_out_dx1", b_idx=l)
        small["cs"][l] = jnp.stack([_diag_out_grad(_mm_s5(sv[hk], dy, "out_dw", f"{t}_s5_out_dw{dr}"), ng, nh, nstate)
                                    for dr, hk in enumerate(("h0", "h1"))], axis=0)
        a0, a1, small["lam"][l] = _scan_bwd(dh0, dh1, sv["h0"], sv["h1"], lam[l], n_ctx, f"{t}_scan")
        du_proj = _mm_s5(a0, w_b[0], "in_dx", f"{t}_s5_in_dx0", b_idx=l)
        du_proj = _mm_s5(a1, w_b[1], "in_dx", f"{t}_s5_in_dx1", b_idx=l, add=du_proj)
        small["bb"][l] = jnp.stack([_diag_in_grad(_mm_s5(sv["u"], adj, "in_dw", f"{t}_s5_in_dw{dr}", a_cols=(pw, sw)),
                                                  ng, nh, nstate) for dr, adj in enumerate((a0, a1))], axis=0)
        du = _assemble_du(du_pool, du_dir, du_proj, f"{t}_du")
        dh = _mm(du, wi, "nt", F32, f"{t}_in_dx", b_idx=l)
        big["w_in"][l] = _mm(sv["h"], du, "tn", COMM_DTYPE, f"{t}_in_dw")
        dx, dss_mix, small["g_pre_mix"][l] = _norm_mod_bwd(dh, sv["xc"], row(g_pre_mix, l), md, 0, 1, dx, n_ctx,
                                                           f"{t}_pre_mix")
        dmods[l] = jnp.concatenate([dss_mix, dgate_mix, dss_ffn, dgate_ffn], axis=1).reshape(2, ada_w)

    grad_x = dx[n_ctx:][None]

    dmod_all = _gather_devices([jnp.stack(dmods, axis=0)], "gather_dmods")[0]
    ada_rows, db_ada = _ada_rows(dmod_all, "ada_rows")
    rows_shard = lax.dynamic_slice_in_dim(ada_rows, chip * ada_cols, ada_cols, axis=2)
    dcond_part = _ada_dcond(rows_shard, w_ada, "ada_dcond")
    dcond_parts = _gather_devices([dcond_part], "gather_dcond")[0][0::2]
    grads = {"w_ada": _ada_dw(cond, rows_shard, "ada_dw"), "b_ada": db_ada[:, 0, :],
             "c_ctx": _ada_dctx(dcond_parts, c_ctx[None, :], N_DEV, "ada_dctx")[0]}

    stacked = {k: jnp.stack(v, axis=0) for k, v in big.items()}
    parts = [stacked["w_in"].reshape(nl, N_CHIPS, d // N_CHIPS, d),
             jnp.transpose(stacked["w_pool"].astype(COMM_DTYPE).reshape(nl, n_pool_groups, N_CHIPS, pool_group // N_CHIPS,
                                                                      pool_group), (0, 2, 1, 3, 4))
             .reshape(nl, N_CHIPS, pw // N_CHIPS, pool_group),
             stacked["w_glu"].reshape(nl, N_CHIPS, sw // N_CHIPS, sw),
             stacked["w_out"].reshape(nl, N_CHIPS, d // N_CHIPS, d),
             stacked["w_up"],
             stacked["w_down"].reshape(nl, N_CHIPS, dff2 // 2 // N_CHIPS, d)]
    r_in, r_pool, r_glu, r_out, r_up, r_down = _reduce_to_shards(parts, k_idx, c_idx, kc_idx, "big")
    grads.update(w_in=r_in, w_pool=r_pool.reshape(w_pool.shape), w_glu=r_glu, w_out=r_out, w_up=r_up, w_down=r_down)

    order = ("pool_scale", "ssm_d", "g_pre_mix", "g_post_mix", "g_pre_ffn", "g_post_ffn", "lam", "bb", "cs", "w_conv")
    pieces = [jnp.stack(small[k], axis=0) for k in order]
    flat = jnp.concatenate([q.reshape(-1) for q in pieces])
    unit = nl * N_CHIPS * SUBLANES * 1024
    padded = -(-flat.shape[0] // unit) * unit
    flat = jnp.concatenate([flat, jnp.zeros((padded - flat.shape[0],), F32)])
    vec = flat.reshape(nl, N_CHIPS, padded // (nl * N_CHIPS * 1024), 1024)
    vec = _gather_all(_reduce_to_shards([vec], k_idx, c_idx, kc_idx, "small"), [F32], k_idx, "small_all")[0].reshape(-1)
    red, pos = {}, 0
    for k, q in zip(order, pieces):
        red[k] = vec[pos:pos + q.size].reshape(q.shape)
        pos += q.size
    for k in ("pool_scale", "ssm_d", "g_pre_mix", "g_post_mix", "g_pre_ffn", "g_post_ffn"):
        grads[k] = red[k][:, 0, :]
    dlam = red["lam"].reshape(nl, ndir, 2, gp)
    dbb = red["bb"].reshape(nl, ndir, 2, nh, gp)
    d_are, d_aim, d_ldt, d_bre, d_bim = _disc_bwd(
        a_re2, a_im2, logdt2, b_re2, b_im2, dlam[:, :, 0].reshape(rows, gp), dlam[:, :, 1].reshape(rows, gp),
        dbb[:, :, 0].reshape(rows, nh, gp), dbb[:, :, 1].reshape(rows, nh, gp), nstate, "s5_discretise_bwd")
    grads["ssm_a_re"] = d_are.reshape(ssm_a_re.shape)
    grads["ssm_a_im"] = d_aim.reshape(ssm_a_im.shape)
    grads["ssm_log_dt"] = d_ldt[:, :ng].reshape(ssm_log_dt.shape)
    grads["ssm_b_re"] = jnp.transpose(d_bre, (0, 2, 1)).reshape(ssm_b_re.shape)
    grads["ssm_b_im"] = jnp.transpose(d_bim, (0, 2, 1)).reshape(ssm_b_im.shape)
    grads["ssm_c_re"] = red["cs"][:, :, 0]
    grads["ssm_c_im"] = -red["cs"][:, :, 1]
    conv_cols = dff2 // N_CHIPS
    grads["w_conv"] = lax.dynamic_slice_in_dim(red["w_conv"], chip * conv_cols, conv_cols, axis=2).reshape(w_conv.shape)

    delta, new_m, new_v = {}, {}, {}
    for k in WEIGHT_NAMES:
        delta[k], new_m[k], new_v[k] = _adamw(weights[k], grads[k], mom1[k], mom2[k], f"adamw_{k}")
    return (loss, grad_x, *[grads[k] for k in WEIGHT_NAMES], *[delta[k] for k in WEIGHT_NAMES],
            *[new_m[k] for k in WEIGHT_NAMES], *[new_v[k] for k in WEIGHT_NAMES])
```

```python
import math

import jax
import jax.numpy as jnp
from jax import lax
from jax.experimental import pallas as pl
from jax.experimental.pallas import tpu as pltpu

F32 = jnp.float32
MXU_DTYPE = jnp.bfloat16
COMM_DTYPE = jnp.bfloat16
HIGHEST = lax.Precision.HIGHEST
VMEM_LIMIT_BYTES = 48 * 1024 * 1024
LANES = 128
SUBLANES = 8
N_CHIPS = 4
N_DEV = 8

EPS = 1e-6
GRID_W = 64
POOL_WINDOWS = (2, 4, 8, 16)
ADAM_LR = 0.001
ADAM_B1 = 0.9
ADAM_B2 = 0.999
ADAM_EPS = 1e-08
ADAM_WD = 0.01
ADAM_STEP = 10
GELU_C0 = math.sqrt(2.0 / math.pi)
GELU_C1 = 0.044715

SDS = jax.ShapeDtypeStruct
ANY = pl.BlockSpec(memory_space=pl.ANY)
MESH = pl.DeviceIdType.MESH


def _cparams(*sem):
    return pltpu.CompilerParams(dimension_semantics=sem if sem else None, vmem_limit_bytes=VMEM_LIMIT_BYTES)


def _pick(n, cands):
    for cand in cands:
        if n % cand == 0:
            return cand
    return n


def _row_tile(n_ctx, n):
    return math.gcd(math.gcd(n_ctx, n - n_ctx), 256)


_DIMS = {"nn": (((1,), (0,)), ((), ())), "nt": (((1,), (1,)), ((), ())), "tn": (((0,), (0,)), ((), ()))}
_TM = (1088, 1024, 512, 384, 256, 128, 64, 32, 16, 8)
_TN = (1024, 1408, 512, 384, 256, 128)
_TK = (2048, 1024, 1088, 512, 1408, 384, 256, 128, 64, 32, 16, 8)


def _chunk_of(idx, per, chunks):
    out = 0
    for q in range(1, chunks):
        out = out + (idx >= q * per).astype(jnp.int32)
    return out


def _within(idx, per, chunks):
    return idx - per * _chunk_of(idx, per, chunks)


def _mm_call(args, mode, out_dtype, name, grid, tiles, specs, o_spec, o_shape, a3d=False, b3d=False, out3d=False,
             add=False):
    tm, tn, _ = tiles
    nk = grid[2]

    def operand(ref, is3d):
        v = ref[...]
        if is3d:
            v = pltpu.einshape("tjl->t(jl)", v)
        return v.astype(MXU_DTYPE)

    def body(*refs):
        a_ref, b_ref, o_ref = refs[0], refs[1], refs[3 if add else 2]

        def product():
            return lax.dot_general(operand(a_ref, a3d), operand(b_ref, b3d), _DIMS[mode], preferred_element_type=F32)

        def finish(total):
            if add:
                total = total + refs[2][...]
            if out3d:
                total = pltpu.einshape("t(jl)->tjl", total, l=LANES)
            o_ref[...] = total.astype(o_ref.dtype)

        if nk == 1:
            finish(product())
            return
        acc_ref = refs[-1]
        kk = pl.program_id(2)

        @pl.when(kk == 0)
        def _():
            acc_ref[...] = jnp.zeros(acc_ref.shape, F32)

        acc_ref[...] += product()

        @pl.when(kk == nk - 1)
        def _():
            finish(acc_ref[...])

    return pl.pallas_call(
        body, grid=grid, in_specs=specs, out_specs=o_spec, out_shape=SDS(o_shape, out_dtype),
        scratch_shapes=[] if nk == 1 else [pltpu.VMEM((tm, tn), F32)],
        compiler_params=_cparams("parallel", "parallel", "arbitrary"), name=name)(*args)


def _lanes3(blk, imap):
    return (blk[0], blk[1] // LANES, LANES), (lambda i, j, kk: imap(i, j, kk) + (0,))


def _mm(a, b, mode, out_dtype, name, a_idx=None, b_idx=None, a_cols=None, a_chunks=None, b_chunks=None, out_chunks=None,
        a3d=False, b3d=False, out3d=False, add=None):
    a2, b2 = a.shape[-2:], b.shape[-2:]
    if a3d:
        a2 = (a.shape[0], a.shape[1] * a.shape[2])
    if b3d:
        assert mode == "tn"
        b2 = (b.shape[0], b.shape[1] * b.shape[2])
    if a_chunks is not None:
        assert mode == "nt" and a.shape[-3] == a_chunks
        a2 = (a2[0], a2[1] * a_chunks)
    if b_chunks is not None:
        assert b.shape[-3] == b_chunks
        b2 = (b2[0], b2[1] * b_chunks)
    alast = a2[1] if a_cols is None else a_cols[1]
    if mode == "nn":
        m, k, n = a2[0], alast, b2[1]
        assert b2[0] == k
    elif mode == "nt":
        m, k, n = a2[0], alast, b2[0]
        assert b2[1] == k
    else:
        k, m, n = a2[0], alast, b2[1]
        assert b2[0] == k
    n_unit = n // (out_chunks or 1) // (b_chunks if b_chunks and mode != "nt" else 1)
    k_unit = k // (b_chunks if b_chunks and mode == "nt" else 1) // (a_chunks or 1)
    tm, tn, tk = _pick(m, _TM), _pick(n_unit, _TN), _pick(k_unit, _TK)
    nk = k // tk
    a_lane_tile = tm if mode == "tn" else tk
    off = 0
    if a_cols is not None:
        assert a_cols[0] % a_lane_tile == 0
        off = a_cols[0] // a_lane_tile

    if mode == "tn":
        a_blk, a_map = (tk, tm), (lambda i, j, kk: (kk, i + off))
    else:
        a_blk, a_map = (tm, tk), (lambda i, j, kk: (i, kk + off))
        if a_chunks is not None:
            aper = k // a_chunks // tk
            a_blk, a_map = (None, tm, tk), (lambda i, j, kk: (_chunk_of(kk, aper, a_chunks), i, _within(kk, aper, a_chunks)))
    if mode == "nt":
        b_blk, b_map = (tn, tk), (lambda i, j, kk: (j, kk))
        if b_chunks is not None:
            per = k // b_chunks // tk
            b_blk, b_map = (None, tn, tk), (lambda i, j, kk: (_chunk_of(kk, per, b_chunks), j, _within(kk, per, b_chunks)))
    else:
        b_blk, b_map = (tk, tn), (lambda i, j, kk: (kk, j))
        if b_chunks is not None:
            per = n // b_chunks // tn
            b_blk, b_map = (None, tk, tn), (lambda i, j, kk: (_chunk_of(j, per, b_chunks), kk, _within(j, per, b_chunks)))
    if a3d:
        a_blk, a_map = _lanes3(a_blk, a_map)
    if b3d:
        b_blk, b_map = _lanes3(b_blk, b_map)
    if a_idx is not None:
        a_blk, a_map0 = (None,) + a_blk, a_map
        a_map = lambda i, j, kk: (a_idx,) + a_map0(i, j, kk)
    if b_idx is not None:
        b_blk, b_map0 = (None,) + b_blk, b_map
        b_map = lambda i, j, kk: (b_idx,) + b_map0(i, j, kk)
    o_blk, o_map, o_shape = (tm, tn), (lambda i, j, kk: (i, j)), (m, n)
    if out_chunks is not None:
        oper = n // out_chunks // tn
        o_map = lambda i, j, kk: (_chunk_of(j, oper, out_chunks), i, _within(j, oper, out_chunks))
        o_blk, o_shape = (None, tm, tn), (out_chunks, m, n // out_chunks)
    if out3d:
        o_blk, o_map = _lanes3(o_blk, o_map)
        o_shape = (m, n // LANES, LANES)
    specs, args = [pl.BlockSpec(a_blk, a_map), pl.BlockSpec(b_blk, b_map)], [a, b]
    if add is not None:
        specs.append(pl.BlockSpec((tm, tn), lambda i, j, kk: (i, j)))
        args.append(add)
    return _mm_call(args, mode, out_dtype, name, (m // tm, n // tn, nk), (tm, tn, tk), specs, pl.BlockSpec(o_blk, o_map),
                    o_shape, a3d, b3d, out3d, add is not None)


S5_BAND = 2


def _mm_s5(a, b, kind, name, b_idx=None, a_cols=None, add=None):
    nb = S5_BAND
    wide3 = a if kind in ("in_dx", "out", "out_dw") else (b if kind == "in_dw" else None)
    if kind in ("in", "out_dx"):
        rows, wide = a.shape[0], b.shape[-1] if kind == "in" else b.shape[-2]
    else:
        rows, wide = wide3.shape[0], wide3.shape[1] * LANES
    sw = a_cols[1] if a_cols is not None else (b.shape[-1] if kind in ("out", "out_dw") else
                                                 (a.shape[1] if kind == "out_dx" else b.shape[-2]))
    tw, ts = wide // (2 * nb), sw // nb
    nwt = wide // tw
    off = 0 if a_cols is None else a_cols[0] // ts
    half = lambda t: _within(t, nb, nwt // nb)
    lead = (lambda blk, imap: (blk, imap)) if b_idx is None else (
        lambda blk, imap: ((None,) + blk, lambda i, j, kk: (b_idx,) + imap(i, j, kk)))
    rt = _pick(rows, _TM)
    if kind in ("in", "out_dx"):
        mode = "nn" if kind == "in" else "nt"
        a_spec = pl.BlockSpec((rt, ts), lambda i, j, kk: (i, off + half(j)))
        b_blk, b_map = ((ts, tw), lambda i, j, kk: (half(j), j)) if kind == "in" else ((tw, ts), lambda i, j, kk: (j, half(j)))
        o_blk, o_map = _lanes3((rt, tw), lambda i, j, kk: (i, j))
        return _mm_call([a, b], mode, F32, name, (rows // rt, nwt, 1), (rt, tw, ts), [a_spec, pl.BlockSpec(*lead(b_blk, b_map))],
                        pl.BlockSpec(o_blk, o_map), (rows, wide // LANES, LANES), out3d=True)
    if kind in ("out", "in_dx"):
        mode = "nn" if kind == "out" else "nt"
        a_blk, a_map = _lanes3((rt, tw), lambda i, j, kk: (i, kk * nb + j))
        b_blk, b_map = ((tw, ts), lambda i, j, kk: (kk * nb + j, j)) if kind == "out" else (
            (ts, tw), lambda i, j, kk: (j, kk * nb + j))
        specs, args = [pl.BlockSpec(a_blk, a_map), pl.BlockSpec(*lead(b_blk, b_map))], [a, b]
        if add is not None:
            specs.append(pl.BlockSpec((rt, ts), lambda i, j, kk: (i, j)))
            args.append(add)
        return _mm_call(args, mode, F32, name, (rows // rt, nb, nwt // nb), (rt, ts, tw), specs,
                        pl.BlockSpec((rt, ts), lambda i, j, kk: (i, j)), (rows, sw), a3d=True, add=add is not None)
    kt = _pick(rows, _TK)
    if kind == "out_dw":
        a_blk, a_map = _lanes3((kt, tw), lambda i, j, kk: (kk, i))
        return _mm_call([a, b], "tn", F32, name, (nwt, 1, rows // kt), (tw, ts, kt),
                        [pl.BlockSpec(a_blk, a_map), pl.BlockSpec((kt, ts), lambda i, j, kk: (kk, half(i)))],
                        pl.BlockSpec((tw, ts), lambda i, j, kk: (i, 0)), (wide, ts), a3d=True)
    assert kind == "in_dw"
    b_blk, b_map = _lanes3((kt, tw), lambda i, j, kk: (kk, j))
    return _mm_call([a, b], "tn", F32, name, (1, nwt, rows // kt), (ts, tw, kt),
                    [pl.BlockSpec((kt, ts), lambda i, j, kk: (kk, off + half(j))), pl.BlockSpec(b_blk, b_map)],
                    pl.BlockSpec((ts, tw), lambda i, j, kk: (0, j)), (ts, wide), b3d=True)


def _seg_map(nbc):
    return lambda i: (jnp.where(i < nbc, 0, 1), 0, 0)


def _rstd(v):
    return lax.rsqrt(jnp.mean(v * v, axis=-1, keepdims=True) + EPS)


def _norm_mod_fwd(x, g, mods, sh, sc, n_ctx, name):
    n, d = x.shape
    tm = _row_tile(n_ctx, n)
    nbc = n_ctx // tm

    def body(x_ref, g_ref, m_ref, h_ref):
        xv = x_ref[...]
        hn = xv * _rstd(xv) * g_ref[...]
        h_ref[...] = (hn * (1.0 + m_ref[0, sc:sc + 1, :]) + m_ref[0, sh:sh + 1, :]).astype(h_ref.dtype)

    row = pl.BlockSpec((tm, d), lambda i: (i, 0))
    return pl.pallas_call(
        body, grid=(n // tm,),
        in_specs=[row, pl.BlockSpec((1, d), lambda i: (0, 0)), pl.BlockSpec((1, 8, d), _seg_map(nbc))],
        out_specs=row, out_shape=SDS((n, d), MXU_DTYPE), compiler_params=_cparams("parallel"), name=name)(x, g, mods)


def _gate_res_fwd(x, f, g, mods, gi, n_ctx, name):
    n, d = x.shape
    tm = _row_tile(n_ctx, n)
    nbc = n_ctx // tm

    def body(x_ref, f_ref, g_ref, m_ref, o_ref):
        fv = f_ref[...]
        o_ref[...] = x_ref[...] + m_ref[0, gi:gi + 1, :] * (fv * _rstd(fv) * g_ref[...])

    row = pl.BlockSpec((tm, d), lambda i: (i, 0))
    return pl.pallas_call(
        body, grid=(n // tm,),
        in_specs=[row, row, pl.BlockSpec((1, d), lambda i: (0, 0)), pl.BlockSpec((1, 8, d), _seg_map(nbc))],
        out_specs=row, out_shape=SDS((n, d), F32), compiler_params=_cparams("parallel"), name=name)(x, f, g, mods)


def _gate_res_bwd(dx, f, g, mods, gi, n_ctx, name):
    n, d = dx.shape
    tm = _row_tile(n_ctx, n)
    nbc = n_ctx // tm

    def body(dx_ref, f_ref, g_ref, m_ref, df_ref, dgate_ref, dg_ref):
        i = pl.program_id(0)

        @pl.when(i == 0)
        def _():
            dg_ref[...] = jnp.zeros(dg_ref.shape, F32)

        @pl.when(jnp.logical_or(i == 0, i == nbc))
        def _():
            dgate_ref[...] = jnp.zeros(dgate_ref.shape, F32)

        dxv, fv, gv = dx_ref[...], f_ref[...], g_ref[...]
        rs = _rstd(fv)
        nv = fv * rs
        dgate_ref[0] += jnp.sum(dxv * (nv * gv), axis=0, keepdims=True)
        dout = dxv * m_ref[0, gi:gi + 1, :]
        dg_ref[...] += jnp.sum(dout * nv, axis=0, keepdims=True)
        dn = dout * gv
        df_ref[...] = (rs * (dn - nv * jnp.mean(dn * nv, axis=-1, keepdims=True))).astype(df_ref.dtype)

    row = pl.BlockSpec((tm, d), lambda i: (i, 0))
    vec = pl.BlockSpec((1, d), lambda i: (0, 0))
    return pl.pallas_call(
        body, grid=(n // tm,),
        in_specs=[row, row, vec, pl.BlockSpec((1, 8, d), _seg_map(nbc))],
        out_specs=[row, pl.BlockSpec((1, 1, d), _seg_map(nbc)), vec],
        out_shape=[SDS((n, d), MXU_DTYPE), SDS((2, 1, d), F32), SDS((1, d), F32)],
        compiler_params=_cparams("arbitrary"), name=name)(dx, f, g, mods)


def _norm_mod_bwd(dh, x, g, mods, sh, sc, dx_res, n_ctx, name):
    n, d = x.shape
    tm = _row_tile(n_ctx, n)
    nbc = n_ctx // tm

    def body(dh_ref, x_ref, g_ref, m_ref, r_ref, dx_ref, dss_ref, dg_ref):
        i = pl.program_id(0)

        @pl.when(i == 0)
        def _():
            dg_ref[...] = jnp.zeros(dg_ref.shape, F32)

        @pl.when(jnp.logical_or(i == 0, i == nbc))
        def _():
            dss_ref[...] = jnp.zeros(dss_ref.shape, F32)

        dhv, xv, gv = dh_ref[...], x_ref[...], g_ref[...]
        rs = _rstd(xv)
        nv = xv * rs
        dss_ref[0, 0:1, :] += jnp.sum(dhv, axis=0, keepdims=True)
        dss_ref[0, 1:2, :] += jnp.sum(dhv * (nv * gv), axis=0, keepdims=True)
        dhn = dhv * (1.0 + m_ref[0, sc:sc + 1, :])
        dg_ref[...] += jnp.sum(dhn * nv, axis=0, keepdims=True)
        dn = dhn * gv
        dx_ref[...] = r_ref[...] + rs * (dn - nv * jnp.mean(dn * nv, axis=-1, keepdims=True))

    row = pl.BlockSpec((tm, d), lambda i: (i, 0))
    vec = pl.BlockSpec((1, d), lambda i: (0, 0))
    return pl.pallas_call(
        body, grid=(n // tm,),
        in_specs=[row, row, vec, pl.BlockSpec((1, 8, d), _seg_map(nbc)), row],
        out_specs=[row, pl.BlockSpec((1, 2, d), _seg_map(nbc)), vec],
        out_shape=[SDS((n, d), F32), SDS((2, 2, d), F32), SDS((1, d), F32)],
        compiler_params=_cparams("arbitrary"), name=name)(dh, x, g, mods, dx_res)


def _loss_grad(xc, target, n_ctx, name):
    n, d = xc.shape
    tm = _row_tile(n_ctx, n)
    nbc = n_ctx // tm
    nb = n // tm

    def body(x_ref, t_ref, dx_ref, l_ref, acc_ref):
        i = pl.program_id(0)

        @pl.when(i == 0)
        def _():
            acc_ref[...] = jnp.zeros(acc_ref.shape, F32)

        @pl.when(i < nbc)
        def _():
            dx_ref[...] = jnp.zeros(dx_ref.shape, F32)

        @pl.when(i >= nbc)
        def _():
            diff = x_ref[...] - t_ref[...]
            dx_ref[...] = diff * (1.0 / d)
            acc_ref[...] += jnp.sum(diff * diff, axis=0, keepdims=True)

        @pl.when(i == nb - 1)
        def _():
            l_ref[...] = jnp.full(l_ref.shape, (0.5 / d) * jnp.sum(acc_ref[...]), F32)

    row = pl.BlockSpec((tm, d), lambda i: (i, 0))
    return pl.pallas_call(
        body, grid=(nb,),
        in_specs=[row, pl.BlockSpec((tm, d), lambda i: (jnp.maximum(i - nbc, 0), 0))],
        out_specs=[row, pl.BlockSpec((SUBLANES, LANES), lambda i: (0, 0))],
        out_shape=[SDS((n, d), F32), SDS((SUBLANES, LANES), F32)],
        scratch_shapes=[pltpu.VMEM((1, d), F32)],
        compiler_params=_cparams("arbitrary"), name=name)(xc, target)


POOL_PAD = 16


def _pool(src, pool_width, pool_group, n_ctx, bwd, out_dtype, name):
    n = src.shape[0]
    n_lat = n - n_ctx
    gb = pool_group // LANES
    segs = ((0, n_ctx, POOL_PAD), (n_ctx, n_lat, 2 * POOL_PAD + n_ctx))
    total = 3 * POOL_PAD + n

    def body(s_ref, o_ref, scr):
        j = pl.program_id(0)
        for base in (0, POOL_PAD + n_ctx, 2 * POOL_PAD + n):
            scr[pl.ds(base, POOL_PAD), :] = jnp.zeros((POOL_PAD, LANES), F32)
        for gi, w in enumerate(POOL_WINDOWS):
            @pl.when(jnp.logical_and(j >= gi * gb, j < (gi + 1) * gb))
            def _(w=w):
                half = w // 2
                offs = range(-half + 1, half + 1) if bwd else range(-half, half)
                for row0, nseg, base in segs:
                    ch = math.gcd(nseg, 256)

                    def count(c0):
                        t = c0 + lax.broadcasted_iota(jnp.int32, (ch, LANES), 0)
                        return (jnp.minimum(t + half, nseg) - jnp.maximum(t - half, 0)).astype(F32)

                    def fill(ci, carry):
                        c0 = pl.multiple_of(ci * ch, ch)
                        v = s_ref[pl.ds(row0 + c0, ch), :]
                        scr[pl.ds(base + c0, ch), :] = v / count(c0) if bwd else v
                        return carry

                    def window(ci, carry):
                        c0 = pl.multiple_of(ci * ch, ch)
                        acc = jnp.zeros((ch, LANES), F32)
                        for off in offs:
                            acc = acc + scr[pl.ds(c0 + (base + off), ch), :]
                        v = s_ref[pl.ds(row0 + c0, ch), :]
                        res = acc - v if bwd else acc / count(c0) - v
                        o_ref[pl.ds(row0 + c0, ch), :] = res.astype(o_ref.dtype)
                        return carry

                    lax.fori_loop(0, nseg // ch, fill, 0)
                    lax.fori_loop(0, nseg // ch, window, 0)

    blk = pl.BlockSpec((n, LANES), lambda j: (0, j))
    return pl.pallas_call(
        body, grid=(pool_width // LANES,), in_specs=[blk], out_specs=blk,
        out_shape=SDS((n, pool_width), out_dtype), scratch_shapes=[pltpu.VMEM((total, LANES), F32)],
        compiler_params=_cparams("parallel"), name=name)(src)


def _pool_proj_fwd(p, wp, l, scale, name):
    n, pw = p.shape
    ng, c = wp.shape[1], wp.shape[2]
    tm = _pick(n, _TM)

    def body(p_ref, w_ref, s_ref, o_ref):
        y = jnp.dot(p_ref[...], w_ref[...].astype(MXU_DTYPE), preferred_element_type=F32)
        o_ref[...] = (y * s_ref[...]).astype(o_ref.dtype)

    return pl.pallas_call(
        body, grid=(ng, n // tm),
        in_specs=[pl.BlockSpec((tm, c), lambda g, i: (i, g)), pl.BlockSpec((None, None, c, c), lambda g, i: (l, g, 0, 0)),
                  pl.BlockSpec((1, c), lambda g, i: (0, g))],
        out_specs=pl.BlockSpec((tm, c), lambda g, i: (i, g)), out_shape=SDS((n, pw), MXU_DTYPE),
        compiler_params=_cparams("parallel", "parallel"), name=name)(p, wp, scale)


def _pool_proj_bwd(p, dcat, wp, l, scale, name):
    n, pw = p.shape
    ng, c = wp.shape[1], wp.shape[2]
    tm = _pick(n, _TM)

    def body(p_ref, dy_ref, w_ref, s_ref, dp_ref, ds_ref, dw_ref):
        i = pl.program_id(1)

        @pl.when(i == 0)
        def _():
            ds_ref[...] = jnp.zeros(ds_ref.shape, F32)
            dw_ref[...] = jnp.zeros(dw_ref.shape, F32)

        pv, wv, dy = p_ref[...], w_ref[...].astype(MXU_DTYPE), dy_ref[...]
        y = jnp.dot(pv, wv, preferred_element_type=F32)
        ds_ref[...] += jnp.sum(dy * y, axis=0, keepdims=True)
        dpw = (dy * s_ref[...]).astype(MXU_DTYPE)
        dp_ref[...] = lax.dot_general(dpw, wv, _DIMS["nt"], preferred_element_type=F32)
        dw_ref[0] += lax.dot_general(pv, dpw, _DIMS["tn"], preferred_element_type=F32)

    return pl.pallas_call(
        body, grid=(ng, n // tm),
        in_specs=[pl.BlockSpec((tm, c), lambda g, i: (i, g)), pl.BlockSpec((tm, c), lambda g, i: (i, g)),
                  pl.BlockSpec((None, None, c, c), lambda g, i: (l, g, 0, 0)), pl.BlockSpec((1, c), lambda g, i: (0, g))],
        out_specs=[pl.BlockSpec((tm, c), lambda g, i: (i, g)), pl.BlockSpec((1, c), lambda g, i: (0, g)),
                   pl.BlockSpec((1, c, c), lambda g, i: (g, 0, 0))],
        out_shape=[SDS((n, pw), F32), SDS((1, pw), F32), SDS((ng, c, c), F32)],
        compiler_params=_cparams("arbitrary", "arbitrary"), name=name)(p, dcat, wp, scale)


def _disc_math(a_re, a_im, logdt, b_re, b_im):
    dt = jnp.exp(logdt)
    mag = jnp.exp(a_re * dt)
    lam_re = mag * jnp.cos(a_im * dt)
    lam_im = mag * jnp.sin(a_im * dt)
    denom = a_re * a_re + a_im * a_im
    nr, ni = lam_re - 1.0, lam_im
    f_re = ((nr * a_re + ni * a_im) / denom)[:, None, :]
    f_im = ((ni * a_re - nr * a_im) / denom)[:, None, :]
    return lam_re, lam_im, f_re * b_re - f_im * b_im, f_re * b_im + f_im * b_re


def _disc_fwd(a_re, a_im, logdt, b_re, b_im, name):
    def body(ar, ai, ld, br, bi, o_lr, o_li, o_br, o_bi):
        lr, li, bbr, bbi = _disc_math(ar[...], ai[...], ld[...], br[...], bi[...])
        o_lr[...] = lr
        o_li[...] = li
        o_br[...] = bbr
        o_bi[...] = bbi

    return pl.pallas_call(
        body, out_shape=[SDS(a_re.shape, F32), SDS(a_re.shape, F32), SDS(b_re.shape, F32), SDS(b_re.shape, F32)],
        compiler_params=_cparams(), name=name)(a_re, a_im, logdt, b_re, b_im)


def _disc_bwd(a_re, a_im, logdt, b_re, b_im, d_lr, d_li, d_bbr, d_bbi, group, name):
    rows, gp = a_re.shape

    def body(ar, ai, ld, br, bi, g_lr, g_li, g_br, g_bi, o_ar, o_ai, o_ld, o_br, o_bi):
        _, vjp = jax.vjp(_disc_math, ar[...], ai[...], ld[...], br[...], bi[...])
        dar, dai, dld, dbr, dbi = vjp((g_lr[...], g_li[...], g_br[...], g_bi[...]))
        o_ar[...] = dar
        o_ai[...] = dai
        state = lax.broadcasted_iota(jnp.int32, (gp, LANES), 0)
        first = lax.broadcasted_iota(jnp.int32, (gp, LANES), 1) * group
        sel = jnp.logical_and(state >= first, state < first + group).astype(F32)
        o_ld[...] = jnp.dot(dld, sel, precision=HIGHEST, preferred_element_type=F32)
        o_br[...] = dbr
        o_bi[...] = dbi

    return pl.pallas_call(
        body, out_shape=[SDS(a_re.shape, F32), SDS(a_re.shape, F32), SDS((rows, LANES), F32),
                         SDS(b_re.shape, F32), SDS(b_re.shape, F32)],
        compiler_params=_cparams(), name=name)(a_re, a_im, logdt, b_re, b_im, d_lr, d_li, d_bbr, d_bbi)


def _scan_maps(nbc, nb):
    nbl = nb - nbc
    fwd0 = lambda i: (i, 0, 0)
    fwd1 = lambda i: (jnp.where(i < nbc, nbc - 1 - i, nb - 1 - (i - nbc)), 0, 0)
    adj0 = lambda i: (nb - 1 - i, 0, 0)
    adj1 = lambda i: (jnp.where(i < nbl, nbc + i, i - nbl), 0, 0)
    return fwd0, fwd1, adj0, adj1


def _scan_fwd(bu0, bu1, lam, n_ctx, name):
    n, s2, _ = bu0.shape
    s = s2 // 2
    tt = math.gcd(math.gcd(n_ctx, n - n_ctx), 128)
    nbc, nb = n_ctx // tt, n // tt
    fwd0, fwd1, _, _ = _scan_maps(nbc, nb)

    def body(b0_ref, b1_ref, lam_ref, h0_ref, h1_ref, st_ref):
        @pl.when(pl.program_id(0) == 0)
        def _():
            st_ref[...] = jnp.zeros(st_ref.shape, F32)

        lam = [(lam_ref[0], lam_ref[1]), (lam_ref[2], lam_ref[3])]
        lam2 = [(lr * lr - li * li, 2.0 * lr * li) for lr, li in lam]

        def pair(b_ref, h_ref, ra, rb, lm, lm2, hr, hi):
            (lr, li), (l2r, l2i) = lm, lm2
            bar, bai = b_ref[ra, 0:s, :], b_ref[ra, s:s2, :]
            cr = lr * bar - li * bai + b_ref[rb, 0:s, :]
            ci = lr * bai + li * bar + b_ref[rb, s:s2, :]
            h_ref[ra, 0:s, :] = lr * hr - li * hi + bar
            h_ref[ra, s:s2, :] = lr * hi + li * hr + bai
            nr = l2r * hr - l2i * hi + cr
            ni = l2r * hi + l2i * hr + ci
            h_ref[rb, 0:s, :] = nr
            h_ref[rb, s:s2, :] = ni
            return nr, ni

        def step(jj, carry):
            h0r, h0i, h1r, h1i = carry
            ja = 2 * jj
            ta = tt - 1 - ja
            h0r, h0i = pair(b0_ref, h0_ref, ja, ja + 1, lam[0], lam2[0], h0r, h0i)
            h1r, h1i = pair(b1_ref, h1_ref, ta, ta - 1, lam[1], lam2[1], h1r, h1i)
            return h0r, h0i, h1r, h1i

        out = lax.fori_loop(0, tt // 2, step, (st_ref[0], st_ref[1], st_ref[2], st_ref[3]), unroll=2)
        for q in range(4):
            st_ref[q] = out[q]

    blk = (tt, s2, LANES)
    return pl.pallas_call(
        body, grid=(nb,),
        in_specs=[pl.BlockSpec(blk, fwd0), pl.BlockSpec(blk, fwd1), pl.BlockSpec((4, s, LANES), lambda i: (0, 0, 0))],
        out_specs=[pl.BlockSpec(blk, fwd0), pl.BlockSpec(blk, fwd1)],
        out_shape=[SDS(bu0.shape, F32), SDS(bu1.shape, F32)],
        scratch_shapes=[pltpu.VMEM((4, s, LANES), F32)],
        compiler_params=_cparams("arbitrary"), name=name)(bu0, bu1, lam)


def _scan_bwd(dh0, dh1, h0, h1, lam, n_ctx, name):
    n, s2, _ = dh0.shape
    s = s2 // 2
    tt = math.gcd(math.gcd(n_ctx, n - n_ctx), 128)
    nbc, nb = n_ctx // tt, n // tt
    _, _, adj0, adj1 = _scan_maps(nbc, nb)

    def body(d0_ref, d1_ref, h0_ref, h1_ref, lam_ref, a0_ref, a1_ref, dl_ref, st_ref, acc_ref):
        i = pl.program_id(0)

        @pl.when(i == 0)
        def _():
            st_ref[...] = jnp.zeros(st_ref.shape, F32)
            acc_ref[...] = jnp.zeros(acc_ref.shape, F32)

        lam = [(lam_ref[0], lam_ref[1]), (lam_ref[2], lam_ref[3])]
        lam2 = [(lr * lr - li * li, 2.0 * lr * li) for lr, li in lam]

        def pair(d_ref, h_ref, a_ref, ra, rb, lm, lm2, ar, ai, cr, ci):
            (lr, li), (l2r, l2i) = lm, lm2
            dar, dai = d_ref[ra, 0:s, :], d_ref[ra, s:s2, :]
            er = lr * dar + li * dai + d_ref[rb, 0:s, :]
            ei = lr * dai - li * dar + d_ref[rb, s:s2, :]
            mr = lr * ar + li * ai + dar
            mi = lr * ai - li * ar + dai
            a_ref[ra, 0:s, :] = mr
            a_ref[ra, s:s2, :] = mi
            gar, gai = h_ref[ra, 0:s, :], h_ref[ra, s:s2, :]
            gbr, gbi = h_ref[rb, 0:s, :], h_ref[rb, s:s2, :]
            cr = cr + ((ar * gar + ai * gai) + (mr * gbr + mi * gbi))
            ci = ci + ((ai * gar - ar * gai) + (mi * gbr - mr * gbi))
            nr = l2r * ar + l2i * ai + er
            ni = l2r * ai - l2i * ar + ei
            a_ref[rb, 0:s, :] = nr
            a_ref[rb, s:s2, :] = ni
            return nr, ni, cr, ci

        def step(jj, carry):
            a0r, a0i, a1r, a1i, c0r, c0i, c1r, c1i = carry
            ja = 2 * jj
            ta = tt - 1 - ja
            a0r, a0i, c0r, c0i = pair(d0_ref, h0_ref, a0_ref, ta, ta - 1, lam[0], lam2[0], a0r, a0i, c0r, c0i)
            a1r, a1i, c1r, c1i = pair(d1_ref, h1_ref, a1_ref, ja, ja + 1, lam[1], lam2[1], a1r, a1i, c1r, c1i)
            return a0r, a0i, a1r, a1i, c0r, c0i, c1r, c1i

        init = tuple(st_ref[q] for q in range(4)) + tuple(acc_ref[q] for q in range(4))
        out = lax.fori_loop(0, tt // 2, step, init, unroll=2)
        for q in range(4):
            st_ref[q] = out[q]
            acc_ref[q] = out[4 + q]

        @pl.when(i == nb - 1)
        def _():
            for q in range(4):
                dl_ref[q] = out[4 + q]

    blk = (tt, s2, LANES)
    small = pl.BlockSpec((4, s, LANES), lambda i: (0, 0, 0))
    return pl.pallas_call(
        body, grid=(nb,),
        in_specs=[pl.BlockSpec(blk, adj0), pl.BlockSpec(blk, adj1), pl.BlockSpec(blk, adj0), pl.BlockSpec(blk, adj1), small],
        out_specs=[pl.BlockSpec(blk, adj0), pl.BlockSpec(blk, adj1), small],
        out_shape=[SDS(dh0.shape, F32), SDS(dh1.shape, F32), SDS((4, s, LANES), F32)],
        scratch_shapes=[pltpu.VMEM((4, s, LANES), F32), pltpu.VMEM((4, s, LANES), F32)],
        compiler_params=_cparams("arbitrary"), name=name)(dh0, dh1, h0, h1, lam)


def _gelu(v):
    th = jnp.tanh(GELU_C0 * (v + GELU_C1 * v * v * v))
    return 0.5 * v * (1.0 + th), th


def _ssm_head_fwd(y, u, ssm_d, wg, l, name):
    n, sw = y.shape
    ucol = u.shape[1] // sw - 1
    tm = _pick(n, _TM)

    def body(y_ref, u_ref, d_ref, w_ref, o_ref):
        act, _ = _gelu(y_ref[...] + d_ref[...] * u_ref[...])
        q = jnp.dot(act.astype(MXU_DTYPE), w_ref[...].astype(MXU_DTYPE), preferred_element_type=F32)
        o_ref[...] = (act * jax.nn.sigmoid(q)).astype(o_ref.dtype)

    row = pl.BlockSpec((tm, sw), lambda i: (i, 0))
    return pl.pallas_call(
        body, grid=(n // tm,),
        in_specs=[row, pl.BlockSpec((tm, sw), lambda i: (i, ucol)), pl.BlockSpec((1, sw), lambda i: (0, 0)),
                  pl.BlockSpec((None, sw, sw), lambda i: (l, 0, 0))],
        out_specs=row, out_shape=SDS((n, sw), MXU_DTYPE), compiler_params=_cparams("parallel"), name=name)(y, u, ssm_d, wg)


def _ssm_head_bwd(dcat, y, u, ssm_d, wg, l, name):
    n, sw = y.shape
    ucol = u.shape[1] // sw - 1
    tm = _pick(n, _TM)

    def body(do_ref, y_ref, u_ref, d_ref, w_ref, dy_ref, du_ref, act_ref, dq_ref, dd_ref):
        @pl.when(pl.program_id(0) == 0)
        def _():
            dd_ref[...] = jnp.zeros(dd_ref.shape, F32)

        uv, dv, do = u_ref[...], d_ref[...], do_ref[...]
        yf = y_ref[...] + dv * uv
        act, th = _gelu(yf)
        wv = w_ref[...].astype(MXU_DTYPE)
        sg = jax.nn.sigmoid(jnp.dot(act.astype(MXU_DTYPE), wv, preferred_element_type=F32))
        dq = (do * act * sg * (1.0 - sg)).astype(MXU_DTYPE)
        dact = do * sg + lax.dot_general(dq, wv, _DIMS["nt"], preferred_element_type=F32)
        dgelu = 0.5 * (1.0 + th) + 0.5 * yf * (1.0 - th * th) * GELU_C0 * (1.0 + 3.0 * GELU_C1 * yf * yf)
        dyf = dact * dgelu
        dy_ref[...] = dyf.astype(dy_ref.dtype)
        du_ref[...] = dyf * dv
        act_ref[...] = act.astype(act_ref.dtype)
        dq_ref[...] = dq
        dd_ref[...] += jnp.sum(dyf * uv, axis=0, keepdims=True)

    row = pl.BlockSpec((tm, sw), lambda i: (i, 0))
    last = pl.BlockSpec((tm, sw), lambda i: (i, ucol))
    vec = pl.BlockSpec((1, sw), lambda i: (0, 0))
    return pl.pallas_call(
        body, grid=(n // tm,),
        in_specs=[last, row, last, vec, pl.BlockSpec((None, sw, sw), lambda i: (l, 0, 0))],
        out_specs=[row, row, row, row, vec],
        out_shape=[SDS((n, sw), MXU_DTYPE), SDS((n, sw), F32), SDS((n, sw), MXU_DTYPE), SDS((n, sw), MXU_DTYPE),
                   SDS((1, sw), F32)],
        compiler_params=_cparams("arbitrary"), name=name)(dcat, y, u, ssm_d, wg)


def _assemble_du(du_pool, du_dir, du_proj, name):
    n, pw = du_pool.shape
    sw = du_dir.shape[1]
    tm = _pick(n, _TM)

    def body(p_ref, a_ref, b_ref, o_ref):
        o_ref[:, 0:pw] = p_ref[...].astype(o_ref.dtype)
        o_ref[:, pw:pw + sw] = (a_ref[...] + b_ref[...]).astype(o_ref.dtype)

    return pl.pallas_call(
        body, grid=(n // tm,),
        in_specs=[pl.BlockSpec((tm, pw), lambda i: (i, 0)), pl.BlockSpec((tm, sw), lambda i: (i, 0)),
                  pl.BlockSpec((tm, sw), lambda i: (i, 0))],
        out_specs=pl.BlockSpec((tm, pw + sw), lambda i: (i, 0)), out_shape=SDS((n, pw + sw), MXU_DTYPE),
        compiler_params=_cparams("parallel"), name=name)(du_pool, du_dir, du_proj)


CONV_PAD = GRID_W + SUBLANES


def _conv_layout(n, n_ctx):
    return CONV_PAD, 2 * CONV_PAD + n_ctx, 3 * CONV_PAD + n


def _col_masks(ch):
    col = lax.broadcasted_iota(jnp.int32, (ch, LANES), 0) % GRID_W
    return col != 0, col != GRID_W - 1


def _fill_padded(scr, src_ref, n, n_ctx):
    base_c, base_l, total = _conv_layout(n, n_ctx)
    for base in (0, base_c + n_ctx, base_l + n - n_ctx):
        scr[pl.ds(base, CONV_PAD), :] = jnp.zeros((CONV_PAD, LANES), F32)
    for row0, nseg, base in ((0, n_ctx, base_c), (n_ctx, n - n_ctx, base_l)):
        ch = math.gcd(nseg, 512)

        def copy(ci, carry, row0=row0, base=base, ch=ch):
            c0 = pl.multiple_of(ci * ch, ch)
            scr[pl.ds(base + c0, ch), :] = src_ref[pl.ds(row0 + c0, ch), :]
            return carry

        lax.fori_loop(0, nseg // ch, copy, 0)


def _conv_ctx(scr, k_ref, base, c0, ch, sign):
    acc = scr[pl.ds(c0 + base, ch), :] * k_ref[4:5, :]
    acc = acc + scr[pl.ds(c0 + (base - sign), ch), :] * k_ref[3:4, :]
    return acc + scr[pl.ds(c0 + (base + sign), ch), :] * k_ref[5:6, :]


def _conv_lat(scr, k_ref, base, c0, ch, sign, m_l, m_r):
    cols = []
    for j in range(3):
        acc = None
        for i in range(3):
            off = sign * (GRID_W * (i - 1) + (j - 1))
            term = scr[pl.ds(c0 + (base + off), ch), :] * k_ref[3 * i + j:3 * i + j + 1, :]
            acc = term if acc is None else acc + term
        cols.append(acc)
    first, last = (m_l, m_r) if sign > 0 else (m_r, m_l)
    return cols[1] + jnp.where(first, cols[0], 0.0) + jnp.where(last, cols[2], 0.0)


def _conv_chunk(n_lat):
    return math.gcd(n_lat, 256)


def _conv_glu_fwd(z, wk, n_ctx, name):
    n, f2 = z.shape
    dff = f2 // 2
    nvt = dff // LANES
    n_lat = n - n_ctx
    base_c, base_l, total = _conv_layout(n, n_ctx)
    ch = _conv_chunk(n_lat)
    assert ch % GRID_W == 0

    def body(zv_ref, zg_ref, kv_ref, kg_ref, a_ref, cv_ref, cg_ref, sv, sg):
        _fill_padded(sv, zv_ref, n, n_ctx)
        _fill_padded(sg, zg_ref, n, n_ctx)

        def emit(cv, cg, row, rows):
            cv_ref[pl.ds(row, rows), :] = cv
            cg_ref[pl.ds(row, rows), :] = cg
            a_ref[pl.ds(row, rows), :] = (cv * cg * jax.nn.sigmoid(cg)).astype(a_ref.dtype)

        emit(_conv_ctx(sv, kv_ref, base_c, 0, n_ctx, 1), _conv_ctx(sg, kg_ref, base_c, 0, n_ctx, 1), 0, n_ctx)
        m_l, m_r = _col_masks(ch)

        def lat(ci, carry):
            c0 = pl.multiple_of(ci * ch, ch)
            emit(_conv_lat(sv, kv_ref, base_l, c0, ch, 1, m_l, m_r), _conv_lat(sg, kg_ref, base_l, c0, ch, 1, m_l, m_r),
                 n_ctx + c0, ch)
            return carry

        lax.fori_loop(0, n_lat // ch, lat, 0)

    col = lambda shift: pl.BlockSpec((n, LANES), lambda j: (0, j + shift))
    kcol = lambda shift: pl.BlockSpec((9, LANES), lambda j: (0, j + shift))
    return pl.pallas_call(
        body, grid=(nvt,), in_specs=[col(0), col(nvt), kcol(0), kcol(nvt)], out_specs=[col(0), col(0), col(0)],
        out_shape=[SDS((n, dff), MXU_DTYPE), SDS((n, dff), F32), SDS((n, dff), F32)],
        scratch_shapes=[pltpu.VMEM((total, LANES), F32), pltpu.VMEM((total, LANES), F32)],
        compiler_params=_cparams("parallel"), name=name)(z, z, wk, wk)


def _conv_glu_bwd(z, cv, cg, da, wk, n_ctx, name):
    n, f2 = z.shape
    dff = f2 // 2
    nvt = dff // LANES
    n_lat = n - n_ctx
    base_c, base_l, total = _conv_layout(n, n_ctx)
    ch = _conv_chunk(n_lat)
    assert ch % GRID_W == 0
    ctx_taps = [(1, 0), (1, 1), (1, 2)]
    lat_taps = [(i, j) for i in range(3) for j in range(3)]

    def tap_sums(acc, scr, d, base, c0, rows, taps, masks):
        acc = list(acc)
        by_col = (d, d, d) if masks is None else (jnp.where(masks[0], d, 0.0), d, jnp.where(masks[1], d, 0.0))
        for i, j in taps:
            src = scr[pl.ds(c0 + (base + GRID_W * (i - 1) + (j - 1)), rows), :]
            acc[3 * i + j] = acc[3 * i + j] + jnp.sum((src * by_col[j]).reshape(rows // SUBLANES, SUBLANES, LANES), axis=0)
        return acc

    def body(zv_ref, zg_ref, cv_ref, cg_ref, da_ref, kv_ref, kg_ref, dz_ref, dkv_ref, dkg_ref, a_ref, sv, sg, dv, dg):
        _fill_padded(sv, zv_ref, n, n_ctx)
        _fill_padded(sg, zg_ref, n, n_ctx)
        for base in (0, base_c + n_ctx, base_l + n_lat):
            dv[pl.ds(base, CONV_PAD), :] = jnp.zeros((CONV_PAD, LANES), F32)
            dg[pl.ds(base, CONV_PAD), :] = jnp.zeros((CONV_PAD, LANES), F32)
        m_l, m_r = _col_masks(ch)

        def first_pass(row, pad_row, rows):
            cv, cg = cv_ref[pl.ds(row, rows), :], cg_ref[pl.ds(row, rows), :]
            sig = jax.nn.sigmoid(cg)
            silu = cg * sig
            a_ref[pl.ds(row, rows), :] = (cv * silu).astype(a_ref.dtype)
            dav = da_ref[pl.ds(row, rows), :]
            dcv = dav * silu
            dcg = dav * cv * (sig * (1.0 + cg * (1.0 - sig)))
            dv[pl.ds(pad_row, rows), :] = dcv
            dg[pl.ds(pad_row, rows), :] = dcg
            return dcv, dcg

        zero = [jnp.zeros((SUBLANES, LANES), F32) for _ in range(9)]
        dcv, dcg = first_pass(0, base_c, n_ctx)
        accv = tap_sums(zero, sv, dcv, base_c, 0, n_ctx, ctx_taps, None)
        accg = tap_sums(zero, sg, dcg, base_c, 0, n_ctx, ctx_taps, None)

        def lat1(ci, carry):
            accv, accg = carry
            c0 = pl.multiple_of(ci * ch, ch)
            dcv, dcg = first_pass(n_ctx + c0, base_l + c0, ch)
            accv = tap_sums(accv, sv, dcv, base_l, c0, ch, lat_taps, (m_l, m_r))
            accg = tap_sums(accg, sg, dcg, base_l, c0, ch, lat_taps, (m_l, m_r))
            return tuple(accv), tuple(accg)

        accv, accg = lax.fori_loop(0, n_lat // ch, lat1, (tuple(accv), tuple(accg)))
        for t in range(9):
            dkv_ref[t:t + 1, :] = jnp.sum(accv[t], axis=0, keepdims=True)
            dkg_ref[t:t + 1, :] = jnp.sum(accg[t], axis=0, keepdims=True)

        dz_ref[0, pl.ds(0, n_ctx), :] = _conv_ctx(dv, kv_ref, base_c, 0, n_ctx, -1).astype(dz_ref.dtype)
        dz_ref[1, pl.ds(0, n_ctx), :] = _conv_ctx(dg, kg_ref, base_c, 0, n_ctx, -1).astype(dz_ref.dtype)

        def lat2(ci, carry):
            c0 = pl.multiple_of(ci * ch, ch)
            dz_ref[0, pl.ds(n_ctx + c0, ch), :] = _conv_lat(dv, kv_ref, base_l, c0, ch, -1, m_l, m_r).astype(dz_ref.dtype)
            dz_ref[1, pl.ds(n_ctx + c0, ch), :] = _conv_lat(dg, kg_ref, base_l, c0, ch, -1, m_l, m_r).astype(dz_ref.dtype)
            return carry

        lax.fori_loop(0, n_lat // ch, lat2, 0)

    col = lambda shift: pl.BlockSpec((n, LANES), lambda j: (0, j + shift))
    kcol = lambda shift: pl.BlockSpec((9, LANES), lambda j: (0, j + shift))
    pad = pltpu.VMEM((total, LANES), F32)
    return pl.pallas_call(
        body, grid=(nvt,), in_specs=[col(0), col(nvt), col(0), col(0), col(0), kcol(0), kcol(nvt)],
        out_specs=[pl.BlockSpec((2, n, LANES), lambda j: (0, 0, j)), kcol(0), kcol(0), col(0)],
        out_shape=[SDS((2, n, dff), MXU_DTYPE), SDS((9, dff), F32), SDS((9, dff), F32), SDS((n, dff), MXU_DTYPE)],
        scratch_shapes=[pad, pad, pad, pad],
        compiler_params=_cparams("parallel"), name=name)(z, z, cv, cg, da, wk, wk)


def _silu(v):
    return v * jax.nn.sigmoid(v)


def _ada_fwd(cond, w_ada, b_shard, name):
    nl, d, cols = w_ada.shape
    tn = _pick(cols, (512, 256, 128))

    def body(c_ref, w_ref, b_ref, o_ref):
        o_ref[...] = jnp.dot(_silu(c_ref[...]), w_ref[...], precision=HIGHEST, preferred_element_type=F32) + b_ref[...]

    return pl.pallas_call(
        body, grid=(nl, cols // tn),
        in_specs=[pl.BlockSpec(cond.shape, lambda l, j: (0, 0)), pl.BlockSpec((None, d, tn), lambda l, j: (l, 0, j)),
                  pl.BlockSpec((None, 1, tn), lambda l, j: (l, 0, j))],
        out_specs=pl.BlockSpec((None, cond.shape[0], tn), lambda l, j: (l, 0, j)),
        out_shape=SDS((nl, cond.shape[0], cols), F32),
        compiler_params=_cparams("parallel", "parallel"), name=name)(cond, w_ada, b_shard)


def _ada_dw(cond, dmod, name):
    nl, rows, cols = dmod.shape
    d = cond.shape[1]
    tn = _pick(cols, (512, 256, 128))

    def body(c_ref, g_ref, o_ref):
        o_ref[...] = lax.dot_general(_silu(c_ref[...]), g_ref[...], _DIMS["tn"], precision=HIGHEST,
                                     preferred_element_type=F32)

    return pl.pallas_call(
        body, grid=(nl, cols // tn),
        in_specs=[pl.BlockSpec(cond.shape, lambda l, j: (0, 0)), pl.BlockSpec((None, rows, tn), lambda l, j: (l, 0, j))],
        out_specs=pl.BlockSpec((None, d, tn), lambda l, j: (l, 0, j)), out_shape=SDS((nl, d, cols), F32),
        compiler_params=_cparams("parallel", "parallel"), name=name)(cond, dmod)


def _ada_dcond(dmod, w_ada, name):
    nl, rows, cols = dmod.shape
    d = w_ada.shape[1]
    tn = _pick(cols, (512, 256, 128))

    def body(g_ref, w_ref, o_ref):
        @pl.when(jnp.logical_and(pl.program_id(0) == 0, pl.program_id(1) == 0))
        def _():
            o_ref[...] = jnp.zeros(o_ref.shape, F32)

        o_ref[...] += lax.dot_general(g_ref[...], w_ref[...], _DIMS["nt"], precision=HIGHEST, preferred_element_type=F32)

    return pl.pallas_call(
        body, grid=(nl, cols // tn),
        in_specs=[pl.BlockSpec((None, rows, tn), lambda l, j: (l, 0, j)), pl.BlockSpec((None, d, tn), lambda l, j: (l, 0, j))],
        out_specs=pl.BlockSpec((rows, d), lambda l, j: (0, 0)), out_shape=SDS((rows, d), F32),
        compiler_params=_cparams("arbitrary", "arbitrary"), name=name)(dmod, w_ada)


def _ada_rows(dmod_all, name):
    nd, nl, _, w = dmod_all.shape
    tn = _pick(w, (2048, 1024, 512, 256, 128))

    def body(g_ref, rows_ref, db_ref):
        ctx = g_ref[0, 0, 0:1, :]
        for b in range(1, nd):
            ctx = ctx + g_ref[b, 0, 0:1, :]
        total = ctx
        for b in range(nd):
            lat = g_ref[b, 0, 1:2, :]
            rows_ref[b:b + 1, :] = lat
            total = total + lat
        rows_ref[nd:nd + 1, :] = ctx
        rows_ref[nd + 1:16, :] = jnp.zeros((16 - nd - 1, tn), F32)
        db_ref[...] = total

    return pl.pallas_call(
        body, grid=(nl, w // tn),
        in_specs=[pl.BlockSpec((nd, 1, 2, tn), lambda l, j: (0, l, 0, j))],
        out_specs=[pl.BlockSpec((None, 16, tn), lambda l, j: (l, 0, j)), pl.BlockSpec((None, 1, tn), lambda l, j: (l, 0, j))],
        out_shape=[SDS((nl, 16, w), F32), SDS((nl, 1, w), F32)],
        compiler_params=_cparams("parallel", "parallel"), name=name)(dmod_all)


def _ada_dctx(parts, c_ctx, row, name):
    def body(p_ref, c_ref, o_ref):
        ds = p_ref[0, row:row + 1, :]
        for k in range(1, p_ref.shape[0]):
            ds = ds + p_ref[k, row:row + 1, :]
        cv = c_ref[...]
        sg = jax.nn.sigmoid(cv)
        o_ref[...] = ds * (sg * (1.0 + cv * (1.0 - sg)))

    return pl.pallas_call(body, out_shape=SDS(c_ctx.shape, F32), compiler_params=_cparams(), name=name)(parts, c_ctx)


ROW_BLOCK_BYTES = 1 << 20


def _as_rows(shape):
    size = math.prod(shape)
    cols = shape[-1] if len(shape) >= 2 and shape[-1] % LANES == 0 else _pick(size, (1024, 512, 256, 128))
    rows = size // cols
    fits = [t for t in (512, 256, 128, 64, 32, 16, 8) if t * cols * 4 <= ROW_BLOCK_BYTES]
    return rows, cols, _pick(rows, fits)


def _adamw(w, g, m, v, name):
    rows, cols, tr = _as_rows(w.shape)
    c1 = 1.0 / (1.0 - ADAM_B1 ** ADAM_STEP)
    c2 = 1.0 / (1.0 - ADAM_B2 ** ADAM_STEP)

    def body(w_ref, g_ref, m_ref, v_ref, d_ref, nm_ref, nv_ref):
        gv = g_ref[...]
        nm = ADAM_B1 * m_ref[...] + (1.0 - ADAM_B1) * gv
        nv = ADAM_B2 * v_ref[...] + (1.0 - ADAM_B2) * (gv * gv)
        nm_ref[...] = nm
        nv_ref[...] = nv
        d_ref[...] = -ADAM_LR * ((nm * c1) / (jnp.sqrt(nv * c2) + ADAM_EPS) + ADAM_WD * w_ref[...])

    blk = pl.BlockSpec((tr, cols), lambda i: (i, 0))
    outs = pl.pallas_call(
        body, grid=(rows // tr,), in_specs=[blk] * 4, out_specs=[blk] * 3, out_shape=[SDS((rows, cols), F32)] * 3,
        compiler_params=_cparams("parallel"), name=name)(*[t.reshape(rows, cols) for t in (w, g, m, v)])
    return tuple(o.reshape(w.shape) for o in outs)


def _tile_rows(rows, cols, dtype):
    size = jnp.dtype(dtype).itemsize
    fits = [t for t in (512, 256, 128, 64, 32, 16, 8) if t * cols * size <= ROW_BLOCK_BYTES and t * size >= 32]
    return _pick(rows, fits)


def _scalar_spec(grid, in_specs, out_specs):
    return pltpu.PrefetchScalarGridSpec(num_scalar_prefetch=1, grid=grid, in_specs=in_specs, out_specs=out_specs)


def _place_chunk(shard, k_idx, dtype, name):
    nl, rows, cols = shard.shape
    tr = _tile_rows(rows, cols, dtype)

    def body(k_ref, s_ref, o_ref):
        o_ref[...] = s_ref[...].astype(o_ref.dtype)

    return pl.pallas_call(
        body, out_shape=SDS((nl, N_CHIPS, rows, cols), dtype),
        grid_spec=_scalar_spec((nl, rows // tr), [pl.BlockSpec((None, tr, cols), lambda l, i, k: (l, i, 0))],
                               pl.BlockSpec((None, None, tr, cols), lambda l, i, k: (l, k[0], i, 0))),
        compiler_params=_cparams("parallel", "parallel"), name=name)(k_idx, shard)


def _pair_sum(grads, recv, c_idx, name):
    half, nch, rows, cols = recv.shape
    tr = _tile_rows(rows, cols, recv.dtype)

    def body(c_ref, g_ref, r_ref, o_ref):
        o_ref[...] = (g_ref[...].astype(F32) + r_ref[...].astype(F32)).astype(o_ref.dtype)

    blk = pl.BlockSpec((None, None, tr, cols), lambda h, q, i, c: (h, q, i, 0))
    return pl.pallas_call(
        body, out_shape=SDS(recv.shape, recv.dtype),
        grid_spec=_scalar_spec((half, nch, rows // tr),
                               [pl.BlockSpec((None, None, tr, cols), lambda h, q, i, c: (c[0] * half + h, q, i, 0)), blk], blk),
        compiler_params=_cparams("parallel", "parallel", "parallel"), name=name)(c_idx, grads, recv)


def _chip_sum(parts, recv, kc_idx, name):
    half, nch, rows, cols = parts.shape
    tr = _tile_rows(rows, cols, F32)

    def body(kc_ref, p_ref, r_ref, o_ref):
        acc = p_ref[...].astype(F32)
        for s in range(r_ref.shape[0]):
            acc = acc + r_ref[s].astype(F32)
        o_ref[...] = acc

    return pl.pallas_call(
        body, out_shape=SDS((2 * half, rows, cols), F32),
        grid_spec=_scalar_spec((half, rows // tr),
                               [pl.BlockSpec((None, None, tr, cols), lambda h, i, kc: (h, kc[0], i, 0)),
                                pl.BlockSpec((nch - 1, None, tr, cols), lambda h, i, kc: (0, h, i, 0))],
                               pl.BlockSpec((None, tr, cols), lambda h, i, kc: (kc[1] * half + h, i, 0))),
        compiler_params=_cparams("parallel", "parallel"), name=name)(kc_idx, parts, recv)


PIECE_BYTES = 3 << 20
MAX_PIECES = 16
PIECE_ROW_ALIGN = 16


def _coords():
    return lax.axis_index("x"), lax.axis_index("y"), lax.axis_index("c")


def _other_chips(x, y):
    return [(1 - x, y), (x, 1 - y), (1 - x, 1 - y)]


def _row_pieces(rows, nbytes):
    pieces = 1
    while (pieces < MAX_PIECES and nbytes // pieces > PIECE_BYTES and rows % (2 * pieces * PIECE_ROW_ALIGN) == 0):
        pieces *= 2
    step = rows // pieces
    return [pl.ds(i * step, step) for i in range(pieces)]


def _nbytes(shape, dtype):
    return math.prod(shape) * jnp.dtype(dtype).itemsize


def _offsets(counts):
    out, pos = [], 0
    for cnt in counts:
        out.append(pos)
        pos += cnt
    return out, pos


def _gather_chips(placed, name):
    nt = len(placed)
    half = [p.shape[0] // 2 for p in placed]
    pieces = [_row_pieces(p.shape[2], _nbytes((h,) + p.shape[2:], p.dtype)) for p, h in zip(placed, half)]
    base, total = _offsets([len(p) for p in pieces])

    def body(*refs):
        o_refs = refs[nt:2 * nt]
        s_nbr, r_nbr, s_fwd, r_fwd, s_sib, r_sib = refs[2 * nt:]
        x, y, c = _coords()
        k, kx, ky, kd = 2 * x + y, 2 * (1 - x) + y, 2 * x + (1 - y), 2 * (1 - x) + (1 - y)
        across_x, across_y, sibling = (1 - x, y, c), (x, 1 - y, c), (x, y, 1 - c)
        sends = []

        def copy(o_ref, rows, slot, rs, ssem, rsem, q, to):
            return pltpu.make_async_remote_copy(
                src_ref=o_ref.at[rows, slot, rs], dst_ref=o_ref.at[rows, slot, rs], send_sem=ssem.at[q], recv_sem=rsem.at[q],
                device_id=to, device_id_type=MESH)

        def start(cp):
            cp.start()
            sends.append(cp)

        work = [(t, i, rs, base[t] + i, 2 * i < len(pieces[t]) or len(pieces[t]) == 1)
                for t in range(nt) for i, rs in enumerate(pieces[t])]
        for t, i, rs, q, _ in work:
            mine = pl.ds(c * half[t], half[t])
            start(copy(o_refs[t], mine, k, rs, s_nbr, r_nbr, 2 * q, across_x))
            start(copy(o_refs[t], mine, k, rs, s_nbr, r_nbr, 2 * q + 1, across_y))
        for t, i, rs, q, via_x in work:
            mine = pl.ds(c * half[t], half[t])
            copy(o_refs[t], mine, kx, rs, s_nbr, r_nbr, 2 * q, across_x).wait_recv()
            start(copy(o_refs[t], mine, kx, rs, s_sib, r_sib, 3 * q, sibling))
            if not via_x:
                start(copy(o_refs[t], mine, kx, rs, s_fwd, r_fwd, q, across_y))
            copy(o_refs[t], mine, ky, rs, s_nbr, r_nbr, 2 * q + 1, across_y).wait_recv()
            start(copy(o_refs[t], mine, ky, rs, s_sib, r_sib, 3 * q + 1, sibling))
            if via_x:
                start(copy(o_refs[t], mine, ky, rs, s_fwd, r_fwd, q, across_x))
        for t, i, rs, q, via_x in work:
            mine = pl.ds(c * half[t], half[t])
            copy(o_refs[t], mine, kd, rs, s_fwd, r_fwd, q, across_x if via_x else across_y).wait_recv()
            start(copy(o_refs[t], mine, kd, rs, s_sib, r_sib, 3 * q + 2, sibling))
        for t, i, rs, q, _ in work:
            theirs = pl.ds((1 - c) * half[t], half[t])
            for r, slot in enumerate((kx, ky, kd)):
                copy(o_refs[t], theirs, slot, rs, s_sib, r_sib, 3 * q + r, sibling).wait_recv()
        for cp in sends:
            cp.wait_send()

    sem = pltpu.SemaphoreType.DMA
    outs = pl.pallas_call(
        body, in_specs=[ANY] * nt, out_specs=[ANY] * nt,
        out_shape=[SDS(p.shape, p.dtype) for p in placed],
        input_output_aliases={t: t for t in range(nt)},
        scratch_shapes=[sem((2 * total,)), sem((2 * total,)), sem((total,)), sem((total,)), sem((3 * total,)),
                        sem((3 * total,))],
        name=name)(*placed)
    return list(outs)


def _pair_send(grads, name):
    nt = len(grads)
    half = [g.shape[0] // 2 for g in grads]
    pieces = [_row_pieces(g.shape[2], _nbytes((h,) + g.shape[1:], g.dtype)) for g, h in zip(grads, half)]
    base, total = _offsets([len(p) for p in pieces])

    def body(*refs):
        g_refs, o_refs = refs[:nt], refs[nt:2 * nt]
        ssem, rsem = refs[2 * nt:]
        x, y, c = _coords()
        cps = []
        for t in range(nt):
            theirs = pl.ds((1 - c) * half[t], half[t])
            for i, rs in enumerate(pieces[t]):
                q = base[t] + i
                cp = pltpu.make_async_remote_copy(
                    src_ref=g_refs[t].at[theirs, :, rs], dst_ref=o_refs[t].at[:, :, rs], send_sem=ssem.at[q],
                    recv_sem=rsem.at[q], device_id=(x, y, 1 - c), device_id_type=MESH)
                cp.start()
                cps.append(cp)
        for cp in cps:
            cp.wait_recv()
        for cp in cps:
            cp.wait_send()

    sem = pltpu.SemaphoreType.DMA
    outs = pl.pallas_call(
        body, in_specs=[ANY] * nt, out_specs=[ANY] * nt,
        out_shape=[SDS((g.shape[0] // 2,) + g.shape[1:], g.dtype) for g in grads],
        scratch_shapes=[sem((total,)), sem((total,))],
        name=name)(*grads)
    return list(outs)


def _chip_send(parts, name):
    nt = len(parts)
    pieces = [_row_pieces(p.shape[2], _nbytes((p.shape[0],) + p.shape[2:], p.dtype)) for p in parts]
    base, total = _offsets([len(p) for p in pieces])

    def body(*refs):
        p_refs, o_refs = refs[:nt], refs[nt:2 * nt]
        ssem, rsem = refs[2 * nt:]
        x, y, c = _coords()
        cps = []
        for t in range(nt):
            for i, rs in enumerate(pieces[t]):
                for r, (px, py) in enumerate(_other_chips(x, y)):
                    q = 3 * (base[t] + i) + r
                    cp = pltpu.make_async_remote_copy(
                        src_ref=p_refs[t].at[:, 2 * px + py, rs], dst_ref=o_refs[t].at[r, :, rs], send_sem=ssem.at[q],
                        recv_sem=rsem.at[q], device_id=(px, py, c), device_id_type=MESH)
                    cp.start()
                    cps.append(cp)
        for cp in cps:
            cp.wait_recv()
        for cp in cps:
            cp.wait_send()

    sem = pltpu.SemaphoreType.DMA
    outs = pl.pallas_call(
        body, in_specs=[ANY] * nt, out_specs=[ANY] * nt,
        out_shape=[SDS((N_CHIPS - 1, p.shape[0]) + p.shape[2:], p.dtype) for p in parts],
        scratch_shapes=[sem((3 * total,)), sem((3 * total,))],
        name=name)(*parts)
    return list(outs)


def _pair_join(bufs, name):
    nt = len(bufs)
    half = [b.shape[0] // 2 for b in bufs]
    pieces = [_row_pieces(b.shape[1], _nbytes((h,) + b.shape[1:], b.dtype)) for b, h in zip(bufs, half)]
    base, total = _offsets([len(p) for p in pieces])

    def body(*refs):
        o_refs = refs[nt:2 * nt]
        ssem, rsem = refs[2 * nt:]
        x, y, c = _coords()
        cps = []
        for t in range(nt):
            mine = pl.ds(c * half[t], half[t])
            for i, rs in enumerate(pieces[t]):
                q = base[t] + i
                cp = pltpu.make_async_remote_copy(
                    src_ref=o_refs[t].at[mine, rs], dst_ref=o_refs[t].at[mine, rs], send_sem=ssem.at[q],
                    recv_sem=rsem.at[q], device_id=(x, y, 1 - c), device_id_type=MESH)
                cp.start()
                cps.append(cp)
        for t in range(nt):
            theirs = pl.ds((1 - c) * half[t], half[t])
            for i, rs in enumerate(pieces[t]):
                q = base[t] + i
                pltpu.make_async_remote_copy(
                    src_ref=o_refs[t].at[theirs, rs], dst_ref=o_refs[t].at[theirs, rs], send_sem=ssem.at[q],
                    recv_sem=rsem.at[q], device_id=(x, y, 1 - c), device_id_type=MESH).wait_recv()
        for cp in cps:
            cp.wait_send()

    sem = pltpu.SemaphoreType.DMA
    outs = pl.pallas_call(
        body, in_specs=[ANY] * nt, out_specs=[ANY] * nt,
        out_shape=[SDS(b.shape, b.dtype) for b in bufs],
        input_output_aliases={t: t for t in range(nt)},
        scratch_shapes=[sem((total,)), sem((total,))],
        name=name)(*bufs)
    return list(outs)


def _gather_devices(vals, name):
    nt = len(vals)
    flips = [(a, b, e) for a in (0, 1) for b in (0, 1) for e in (0, 1)][1:]

    def body(*refs):
        v_refs, o_refs = refs[:nt], refs[nt:2 * nt]
        lsem, ssem, rsem = refs[2 * nt:]
        x, y, c = _coords()
        me = 4 * x + 2 * y + c
        peers = [((1 - x) if a else x, (1 - y) if b else y, (1 - c) if e else c) for a, b, e in flips]
        cps = []
        for t in range(nt):
            loc = pltpu.make_async_copy(v_refs[t], o_refs[t].at[me], lsem.at[t])
            loc.start()
            cps.append(loc)
            for r, peer in enumerate(peers):
                cp = pltpu.make_async_remote_copy(
                    src_ref=v_refs[t], dst_ref=o_refs[t].at[me], send_sem=ssem.at[7 * t + r],
                    recv_sem=rsem.at[7 * t + r], device_id=peer, device_id_type=MESH)
                cp.start()
                cps.append(cp)
        for t in range(nt):
            for r, (px, py, pc) in enumerate(peers):
                pltpu.make_async_remote_copy(
                    src_ref=v_refs[t], dst_ref=o_refs[t].at[4 * px + 2 * py + pc], send_sem=ssem.at[7 * t + r],
                    recv_sem=rsem.at[7 * t + r], device_id=(px, py, pc), device_id_type=MESH).wait_recv()
        for t in range(nt):
            cps[8 * t].wait()
            for r in range(7):
                cps[8 * t + 1 + r].wait_send()

    sem = pltpu.SemaphoreType.DMA
    outs = pl.pallas_call(
        body, in_specs=[ANY] * nt, out_specs=[ANY] * nt,
        out_shape=[SDS((N_DEV,) + v.shape, v.dtype) for v in vals],
        scratch_shapes=[sem((nt,)), sem((7 * nt,)), sem((7 * nt,))],
        name=name)(*vals)
    return list(outs)


def _gather_all(shards, dtypes, k_idx, tag):
    placed = [_place_chunk(s, k_idx, dt, f"{tag}_place{t}") for t, (s, dt) in enumerate(zip(shards, dtypes))]
    return _gather_chips(placed, f"{tag}_gather")


def _reduce_to_shards(grads, k_idx, c_idx, kc_idx, tag):
    recv = _pair_send(grads, f"{tag}_pair_send")
    pair = [_pair_sum(g, r, c_idx, f"{tag}_pair_sum{t}") for t, (g, r) in enumerate(zip(grads, recv))]
    recv = _chip_send(pair, f"{tag}_chip_send")
    bufs = [_chip_sum(p, r, kc_idx, f"{tag}_chip_sum{t}") for t, (p, r) in enumerate(zip(pair, recv))]
    return _pair_join(bufs, f"{tag}_pair_join")


WEIGHT_NAMES = ("c_ctx", "w_ada", "b_ada", "w_in", "w_pool", "pool_scale", "ssm_a_re", "ssm_a_im", "ssm_log_dt",
                "ssm_b_re", "ssm_b_im", "ssm_c_re", "ssm_c_im", "ssm_d", "w_glu", "w_out", "g_pre_mix", "g_post_mix",
                "g_pre_ffn", "g_post_ffn", "w_up", "w_conv", "w_down")


def _block_diag_in(bb, ng):
    nl, nd, npart, h, gp = bb.shape
    p = gp // ng
    w = jnp.einsum("ldqhgp,kg->lkhdqgp", bb.reshape(nl, nd, npart, h, ng, p), jnp.eye(ng, dtype=bb.dtype))
    return w.reshape(nl, ng * h, nd * npart * gp)


def _diag_in_grad(dw, ng, nh, p):
    gl = ng // S5_BAND
    out = jnp.einsum("ghqagp->qhagp", dw.reshape(gl, nh, 2, S5_BAND, gl, p))
    return out.reshape(2, nh, ng * p)


def _block_diag_out(cs, ng):
    nl, nd, npart, _, h, p = cs.shape
    w = jnp.einsum("ldqghp,kg->ldqkpgh", cs, jnp.eye(ng, dtype=cs.dtype))
    return w.reshape(nl, nd * npart * ng * p, ng * h)


def _diag_out_grad(dw, ng, nh, p):
    gl = ng // S5_BAND
    out = jnp.einsum("qagpgh->qaghp", dw.reshape(2, S5_BAND, gl, p, gl, nh))
    return out.reshape(2, ng, nh, p)


def kernel(x, c, ctx, c_ctx, w_ada, b_ada, w_in, w_pool, pool_scale, ssm_a_re, ssm_a_im, ssm_log_dt, ssm_b_re, ssm_b_im, ssm_c_re, ssm_c_im, ssm_d, w_glu, w_out, g_pre_mix, g_post_mix, g_pre_ffn, g_post_ffn, w_up, w_conv, w_down, loss_target, m_c_ctx, m_w_ada, m_b_ada, m_w_in, m_w_pool, m_pool_scale, m_ssm_a_re, m_ssm_a_im, m_ssm_log_dt, m_ssm_b_re, m_ssm_b_im, m_ssm_c_re, m_ssm_c_im, m_ssm_d, m_w_glu, m_w_out, m_g_pre_mix, m_g_post_mix, m_g_pre_ffn, m_g_post_ffn, m_w_up, m_w_conv, m_w_down, v_c_ctx, v_w_ada, v_b_ada, v_w_in, v_w_pool, v_pool_scale, v_ssm_a_re, v_ssm_a_im, v_ssm_log_dt, v_ssm_b_re, v_ssm_b_im, v_ssm_c_re, v_ssm_c_im, v_ssm_d, v_w_glu, v_w_out, v_g_pre_mix, v_g_post_mix, v_g_pre_ffn, v_g_post_ffn, v_w_up, v_w_conv, v_w_down):
    weights = dict(zip(WEIGHT_NAMES, (c_ctx, w_ada, b_ada, w_in, w_pool, pool_scale, ssm_a_re, ssm_a_im, ssm_log_dt,
                                      ssm_b_re, ssm_b_im, ssm_c_re, ssm_c_im, ssm_d, w_glu, w_out, g_pre_mix, g_post_mix,
                                      g_pre_ffn, g_post_ffn, w_up, w_conv, w_down)))
    mom1 = dict(zip(WEIGHT_NAMES, (m_c_ctx, m_w_ada, m_b_ada, m_w_in, m_w_pool, m_pool_scale, m_ssm_a_re, m_ssm_a_im,
                                   m_ssm_log_dt, m_ssm_b_re, m_ssm_b_im, m_ssm_c_re, m_ssm_c_im, m_ssm_d, m_w_glu, m_w_out,
                                   m_g_pre_mix, m_g_post_mix, m_g_pre_ffn, m_g_post_ffn, m_w_up, m_w_conv, m_w_down)))
    mom2 = dict(zip(WEIGHT_NAMES, (v_c_ctx, v_w_ada, v_b_ada, v_w_in, v_w_pool, v_pool_scale, v_ssm_a_re, v_ssm_a_im,
                                   v_ssm_log_dt, v_ssm_b_re, v_ssm_b_im, v_ssm_c_re, v_ssm_c_im, v_ssm_d, v_w_glu, v_w_out,
                                   v_g_pre_mix, v_g_post_mix, v_g_pre_ffn, v_g_post_ffn, v_w_up, v_w_conv, v_w_down)))

    xi, yi, ci = lax.axis_index("x"), lax.axis_index("y"), lax.axis_index("c")
    chip = 2 * xi + yi
    dev = 4 * xi + 2 * yi + ci
    nl = w_in.shape[0]
    n_lat, d = x.shape[1], x.shape[2]
    n_ctx = ctx.shape[1]
    n = n_ctx + n_lat
    _, ndir, ng, nstate, nh = ssm_b_re.shape
    gp = ng * nstate
    sw = ng * nh
    n_pool_groups, pool_group = w_pool.shape[1], w_pool.shape[3]
    pw = n_pool_groups * pool_group
    assert pw + sw == d and pw % sw == 0 and len(POOL_WINDOWS) == n_pool_groups and n_lat % GRID_W == 0
    dff2 = w_up.shape[2] * N_CHIPS
    ada_w = w_ada.shape[2] * N_CHIPS
    ada_cols = w_ada.shape[2]
    s_rows = gp // LANES

    c_pad = jnp.concatenate([c, jnp.zeros((SUBLANES - 1, d), F32)], axis=0)
    c_all = _gather_devices([c_pad], "gather_cond")[0][:, 0, :]
    cond = jnp.concatenate([c_all, c_ctx[None, :], jnp.zeros((16 - N_DEV - 1, d), F32)], axis=0)
    b_shard = lax.dynamic_slice_in_dim(b_ada, chip * ada_cols, ada_cols, axis=1)[:, None, :]
    mod_shard = _ada_fwd(cond, w_ada, b_shard, "ada_fwd")
    k_idx, c_idx, kc_idx = jnp.stack([chip]), jnp.stack([ci]), jnp.stack([chip, ci])
    mod_all = _gather_all([mod_shard], [F32], k_idx, "mods")[0]
    mod_all = jnp.transpose(mod_all, (0, 2, 1, 3)).reshape(nl, 16, ada_w)
    mod_lat = lax.dynamic_index_in_dim(mod_all, dev, axis=1, keepdims=False).reshape(nl, 6, d)
    mod_ctx = mod_all[:, N_DEV].reshape(nl, 6, d)
    mods = jnp.concatenate([jnp.stack([mod_ctx, mod_lat], axis=1), jnp.zeros((nl, 2, 2, d), F32)], axis=2)

    shards = [w_in, w_pool.reshape(nl, pw // N_CHIPS, pool_group), w_glu, w_out, w_up, w_down,
              w_conv.reshape(nl, 9, dff2 // N_CHIPS)]
    g_in, g_pool, g_glu, g_out, g_up, g_down, g_conv = _gather_all(shards, [COMM_DTYPE] * 6 + [F32], k_idx, "weights")
    wi = g_in.reshape(nl, d, d)
    wp = jnp.transpose(g_pool.reshape(nl, N_CHIPS, n_pool_groups, pool_group // N_CHIPS, pool_group),
                       (0, 2, 1, 3, 4)).reshape(nl, n_pool_groups, pool_group, pool_group)
    wg = g_glu.reshape(nl, sw, sw)
    wo = g_out.reshape(nl, d, d)
    wu = g_up
    wd = g_down.reshape(nl, dff2 // 2, d)
    wk = jnp.transpose(g_conv, (0, 2, 1, 3)).reshape(nl, 9, dff2)

    rows = nl * ndir
    a_re2 = ssm_a_re.reshape(rows, gp)
    a_im2 = ssm_a_im.reshape(rows, gp)
    logdt2 = jnp.repeat(ssm_log_dt.reshape(rows, ng), nstate, axis=1)
    b_re2 = jnp.transpose(ssm_b_re.reshape(rows, gp, nh), (0, 2, 1))
    b_im2 = jnp.transpose(ssm_b_im.reshape(rows, gp, nh), (0, 2, 1))
    lam_re, lam_im, bb_re, bb_im = _disc_fwd(a_re2, a_im2, logdt2, b_re2, b_im2, "s5_discretise")
    lam = jnp.stack([lam_re.reshape(nl, ndir, s_rows, LANES), lam_im.reshape(nl, ndir, s_rows, LANES)], axis=2)
    lam = lam.reshape(nl, 2 * ndir, s_rows, LANES)
    bbs = jnp.stack([bb_re.reshape(nl, ndir, nh, gp), bb_im.reshape(nl, ndir, nh, gp)], axis=2)
    w_b = _block_diag_in(bbs.astype(MXU_DTYPE), ng)
    w_b = [w_b[:, :, dr * 2 * gp:(dr + 1) * 2 * gp] for dr in range(ndir)]
    cs = jnp.stack([ssm_c_re, -ssm_c_im], axis=2)
    w_c = _block_diag_out(cs.astype(MXU_DTYPE), ng)
    w_c = [w_c[:, dr * 2 * gp:(dr + 1) * 2 * gp] for dr in range(ndir)]

    def row(v, l):
        return v[l:l + 1]

    xc = jnp.concatenate([ctx[0], x[0]], axis=0)
    saved = []
    for l in range(nl):
        t = f"l{l}"
        md = mods[l]
        h = _norm_mod_fwd(xc, row(g_pre_mix, l), md, 0, 1, n_ctx, f"{t}_pre_mix")
        u = _mm(h, wi, "nn", F32, f"{t}_in_proj", b_idx=l)
        p = _pool(u, pw, pool_group, n_ctx, False, MXU_DTYPE, f"{t}_pool")
        ypool = _pool_proj_fwd(p, wp, l, row(pool_scale, l), f"{t}_pool_proj")
        bu0 = _mm_s5(u, w_b[0], "in", f"{t}_s5_in0", b_idx=l, a_cols=(pw, sw))
        bu1 = _mm_s5(u, w_b[1], "in", f"{t}_s5_in1", b_idx=l, a_cols=(pw, sw))
        h0, h1 = _scan_fwd(bu0, bu1, lam[l], n_ctx, f"{t}_scan")
        y = _mm_s5(h0, w_c[0], "out", f"{t}_s5_out0", b_idx=l)
        y = _mm_s5(h1, w_c[1], "out", f"{t}_s5_out1", b_idx=l, add=y)
        s_out = _ssm_head_fwd(y, u, row(ssm_d, l), wg, l, f"{t}_s5_head")
        cat = jnp.concatenate([ypool, s_out], axis=1)
        mix = _mm(cat, wo, "nn", F32, f"{t}_out_proj", b_idx=l)
        x_mid = _gate_res_fwd(xc, mix, row(g_post_mix, l), md, 2, n_ctx, f"{t}_post_mix")
        h2 = _norm_mod_fwd(x_mid, row(g_pre_ffn, l), md, 3, 4, n_ctx, f"{t}_pre_ffn")
        z = _mm(h2, wu, "nn", F32, f"{t}_up", b_idx=l, b_chunks=N_CHIPS)
        act, cv, cg = _conv_glu_fwd(z, wk[l], n_ctx, f"{t}_conv_glu")
        f = _mm(act, wd, "nn", F32, f"{t}_down", b_idx=l)
        x_out = _gate_res_fwd(x_mid, f, row(g_post_ffn, l), md, 5, n_ctx, f"{t}_post_ffn")
        saved.append(dict(xc=xc, h=h, u=u, p=p, h0=h0, h1=h1, y=y, cat=cat, mix=mix, x_mid=x_mid, h2=h2, z=z, cv=cv, cg=cg, f=f))
        xc = x_out

    dx, loss_tile = _loss_grad(xc, loss_target[0], n_ctx, "loss")
    loss = lax.psum(loss_tile[0, 0], ("x", "y", "c"))

    big = {k: [None] * nl for k in ("w_in", "w_pool", "w_glu", "w_out", "w_up", "w_down")}
    small = {k: [None] * nl for k in ("pool_scale", "ssm_d", "g_pre_mix", "g_post_mix", "g_pre_ffn", "g_post_ffn",
                                      "lam", "bb", "cs", "w_conv")}
    dmods = [None] * nl
    for l in reversed(range(nl)):
        t = f"l{l}b"
        md = mods[l]
        sv = saved[l]
        df, dgate_ffn, small["g_post_ffn"][l] = _gate_res_bwd(dx, sv["f"], row(g_post_ffn, l), md, 5, n_ctx, f"{t}_post_ffn")
        dact = _mm(df, wd, "nt", F32, f"{t}_down_dx", b_idx=l)
        dz, dkv, dkg, act = _conv_glu_bwd(sv["z"], sv["cv"], sv["cg"], dact, wk[l], n_ctx, f"{t}_conv_glu")
        small["w_conv"][l] = jnp.concatenate([dkv, dkg], axis=1)
        big["w_down"][l] = _mm(act, df, "tn", COMM_DTYPE, f"{t}_down_dw")
        big["w_up"][l] = _mm(sv["h2"], dz, "tn", COMM_DTYPE, f"{t}_up_dw", b_chunks=2, out_chunks=N_CHIPS)
        dh2 = _mm(dz, wu, "nt", F32, f"{t}_up_dx", b_idx=l, a_chunks=2, b_chunks=N_CHIPS)
        dx, dss_ffn, small["g_pre_ffn"][l] = _norm_mod_bwd(dh2, sv["x_mid"], row(g_pre_ffn, l), md, 3, 4, dx, n_ctx,
                                                           f"{t}_pre_ffn")
        dmix, dgate_mix, small["g_post_mix"][l] = _gate_res_bwd(dx, sv["mix"], row(g_post_mix, l), md, 2, n_ctx,
                                                                f"{t}_post_mix")
        dcat = _mm(dmix, wo, "nt", F32, f"{t}_out_dx", b_idx=l)
        big["w_out"][l] = _mm(sv["cat"], dmix, "tn", COMM_DTYPE, f"{t}_out_dw")
        dp, small["pool_scale"][l], big["w_pool"][l] = _pool_proj_bwd(sv["p"], dcat, wp, l, row(pool_scale, l),
                                                                      f"{t}_pool_proj")
        du_pool = _pool(dp, pw, pool_group, n_ctx, True, F32, f"{t}_pool")
        dy, du_dir, gact, dq, small["ssm_d"][l] = _ssm_head_bwd(dcat, sv["y"], sv["u"], row(ssm_d, l), wg, l, f"{t}_s5_head")
        big["w_glu"][l] = _mm(gact, dq, "tn", COMM_DTYPE, f"{t}_glu_dw")
        dh0 = _mm_s5(dy, w_c[0], "out_dx", f"{t}_s5_out_dx0", b_idx=l)
        dh1 = _mm_s5(dy, w_c[1], "out_dx", f"{t}_s5_out_dx1", b_idx=l)
        small["cs"][l] = jnp.stack([_diag_out_grad(_mm_s5(sv[hk], dy, "out_dw", f"{t}_s5_out_dw{dr}"), ng, nh, nstate)
                                    for dr, hk in enumerate(("h0", "h1"))], axis=0)
        a0, a1, small["lam"][l] = _scan_bwd(dh0, dh1, sv["h0"], sv["h1"], lam[l], n_ctx, f"{t}_scan")
        du_proj = _mm_s5(a0, w_b[0], "in_dx", f"{t}_s5_in_dx0", b_idx=l)
        du_proj = _mm_s5(a1, w_b[1], "in_dx", f"{t}_s5_in_dx1", b_idx=l, add=du_proj)
        small["bb"][l] = jnp.stack([_diag_in_grad(_mm_s5(sv["u"], adj, "in_dw", f"{t}_s5_in_dw{dr}", a_cols=(pw, sw)),
                                                  ng, nh, nstate) for dr, adj in enumerate((a0, a1))], axis=0)
        du = _assemble_du(du_pool, du_dir, du_proj, f"{t}_du")
        dh = _mm(du, wi, "nt", F32, f"{t}_in_dx", b_idx=l)
        big["w_in"][l] = _mm(sv["h"], du, "tn", COMM_DTYPE, f"{t}_in_dw")
        dx, dss_mix, small["g_pre_mix"][l] = _norm_mod_bwd(dh, sv["xc"], row(g_pre_mix, l), md, 0, 1, dx, n_ctx,
                                                           f"{t}_pre_mix")
        dmods[l] = jnp.concatenate([dss_mix, dgate_mix, dss_ffn, dgate_ffn], axis=1).reshape(2, ada_w)

    grad_x = dx[n_ctx:][None]

    dmod_all = _gather_devices([jnp.stack(dmods, axis=0)], "gather_dmods")[0]
    ada_rows, db_ada = _ada_rows(dmod_all, "ada_rows")
    rows_shard = lax.dynamic_slice_in_dim(ada_rows, chip * ada_cols, ada_cols, axis=2)
    dcond_part = _ada_dcond(rows_shard, w_ada, "ada_dcond")
    dcond_parts = _gather_devices([dcond_part], "gather_dcond")[0][0::2]
    grads = {"w_ada": _ada_dw(cond, rows_shard, "ada_dw"), "b_ada": db_ada[:, 0, :],
             "c_ctx": _ada_dctx(dcond_parts, c_ctx[None, :], N_DEV, "ada_dctx")[0]}

    stacked = {k: jnp.stack(v, axis=0) for k, v in big.items()}
    parts = [stacked["w_in"].reshape(nl, N_CHIPS, d // N_CHIPS, d),
             jnp.transpose(stacked["w_pool"].astype(COMM_DTYPE).reshape(nl, n_pool_groups, N_CHIPS, pool_group // N_CHIPS,
                                                                      pool_group), (0, 2, 1, 3, 4))
             .reshape(nl, N_CHIPS, pw // N_CHIPS, pool_group),
             stacked["w_glu"].reshape(nl, N_CHIPS, sw // N_CHIPS, sw),
             stacked["w_out"].reshape(nl, N_CHIPS, d // N_CHIPS, d),
             stacked["w_up"],
             stacked["w_down"].reshape(nl, N_CHIPS, dff2 // 2 // N_CHIPS, d)]
    r_in, r_pool, r_glu, r_out, r_up, r_down = _reduce_to_shards(parts, k_idx, c_idx, kc_idx, "big")
    grads.update(w_in=r_in, w_pool=r_pool.reshape(w_pool.shape), w_glu=r_glu, w_out=r_out, w_up=r_up, w_down=r_down)

    order = ("pool_scale", "ssm_d", "g_pre_mix", "g_post_mix", "g_pre_ffn", "g_post_ffn", "lam", "bb", "cs", "w_conv")
    pieces = [jnp.stack(small[k], axis=0) for k in order]
    flat = jnp.concatenate([q.reshape(-1) for q in pieces])
    unit = nl * N_CHIPS * SUBLANES * 1024
    padded = -(-flat.shape[0] // unit) * unit
    flat = jnp.concatenate([flat, jnp.zeros((padded - flat.shape[0],), F32)])
    vec = flat.reshape(nl, N_CHIPS, padded // (nl * N_CHIPS * 1024), 1024)
    vec = _gather_all(_reduce_to_shards([vec], k_idx, c_idx, kc_idx, "small"), [F32], k_idx, "small_all")[0].reshape(-1)
    red, pos = {}, 0
    for k, q in zip(order, pieces):
        red[k] = vec[pos:pos + q.size].reshape(q.shape)
        pos += q.size
    for k in ("pool_scale", "ssm_d", "g_pre_mix", "g_post_mix", "g_pre_ffn", "g_post_ffn"):
        grads[k] = red[k][:, 0, :]
    dlam = red["lam"].reshape(nl, ndir, 2, gp)
    dbb = red["bb"].reshape(nl, ndir, 2, nh, gp)
    d_are, d_aim, d_ldt, d_bre, d_bim = _disc_bwd(
        a_re2, a_im2, logdt2, b_re2, b_im2, dlam[:, :, 0].reshape(rows, gp), dlam[:, :, 1].reshape(rows, gp),
        dbb[:, :, 0].reshape(rows, nh, gp), dbb[:, :, 1].reshape(rows, nh, gp), nstate, "s5_discretise_bwd")
    grads["ssm_a_re"] = d_are.reshape(ssm_a_re.shape)
    grads["ssm_a_im"] = d_aim.reshape(ssm_a_im.shape)
    grads["ssm_log_dt"] = d_ldt[:, :ng].reshape(ssm_log_dt.shape)
    grads["ssm_b_re"] = jnp.transpose(d_bre, (0, 2, 1)).reshape(ssm_b_re.shape)
    grads["ssm_b_im"] = jnp.transpose(d_bim, (0, 2, 1)).reshape(ssm_b_im.shape)
    grads["ssm_c_re"] = red["cs"][:, :, 0]
    grads["ssm_c_im"] = -red["cs"][:, :, 1]
    conv_cols = dff2 // N_CHIPS
    grads["w_conv"] = lax.dynamic_slice_in_dim(red["w_conv"], chip * conv_cols, conv_cols, axis=2).reshape(w_conv.shape)

    delta, new_m, new_v = {}, {}, {}
    for k in WEIGHT_NAMES:
        delta[k], new_m[k], new_v[k] = _adamw(weights[k], grads[k], mom1[k], mom2[k], f"adamw_{k}")
    return (loss, grad_x, *[grads[k] for k in WEIGHT_NAMES], *[delta[k] for k in WEIGHT_NAMES],
            *[new_m[k] for k in WEIGHT_NAMES], *[new_v[k] for k in WEIGHT_NAMES])
```

```python
import math

import jax
import jax.numpy as jnp
from jax import lax
from jax.experimental import pallas as pl
from jax.experimental.pallas import tpu as pltpu

F32 = jnp.float32
MXU_DTYPE = jnp.bfloat16
COMM_DTYPE = jnp.bfloat16
HIGHEST = lax.Precision.HIGHEST
VMEM_LIMIT_BYTES = 48 * 1024 * 1024
LANES = 128
SUBLANES = 8
N_CHIPS = 4
N_DEV = 8

EPS = 1e-6
GRID_W = 64
POOL_WINDOWS = (2, 4, 8, 16)
ADAM_LR = 0.001
ADAM_B1 = 0.9
ADAM_B2 = 0.999
ADAM_EPS = 1e-08
ADAM_WD = 0.01
ADAM_STEP = 10
GELU_C0 = math.sqrt(2.0 / math.pi)
GELU_C1 = 0.044715

SDS = jax.ShapeDtypeStruct
ANY = pl.BlockSpec(memory_space=pl.ANY)
MESH = pl.DeviceIdType.MESH


def _cparams(*sem):
    return pltpu.CompilerParams(dimension_semantics=sem if sem else None, vmem_limit_bytes=VMEM_LIMIT_BYTES)


def _pick(n, cands):
    for cand in cands:
        if n % cand == 0:
            return cand
    return n


def _row_tile(n_ctx, n):
    return math.gcd(math.gcd(n_ctx, n - n_ctx), 256)


_DIMS = {"nn": (((1,), (0,)), ((), ())), "nt": (((1,), (1,)), ((), ())), "tn": (((0,), (0,)), ((), ()))}
_TM = (1088, 1024, 512, 384, 256, 128, 64, 32, 16, 8)
_TN = (1024, 1408, 512, 384, 256, 128)
_TK = (2048, 2176, 1024, 1088, 512, 1408, 384, 256, 128, 64, 32, 16, 8)
MM_BLOCK_BYTES = 40 * 1024 * 1024


def _pick_k(k, k_unit, tm, tn, a, b, out_dtype):
    sizes = [jnp.dtype(d).itemsize for d in (a.dtype, b.dtype, out_dtype)]
    for tk in (k_unit,) + _TK:
        if k_unit % tk == 0:
            blocks = 2 * (tm * tk * sizes[0] + tk * tn * sizes[1] + tm * tn * sizes[2]) + (tm * tn * 4 if k // tk > 1 else 0)
            if blocks <= MM_BLOCK_BYTES:
                return tk
    return _pick(k_unit, _TK)


def _chunk_of(idx, per, chunks):
    out = 0
    for q in range(1, chunks):
        out = out + (idx >= q * per).astype(jnp.int32)
    return out


def _within(idx, per, chunks):
    return idx - per * _chunk_of(idx, per, chunks)


def _mm_call(args, mode, out_dtype, name, grid, tiles, specs, o_spec, o_shape, a3d=False, b3d=False, out3d=False,
             add=False):
    tm, tn, _ = tiles
    nk = grid[2]

    def operand(ref, is3d):
        v = ref[...]
        if is3d:
            v = pltpu.einshape("tjl->t(jl)", v)
        return v.astype(MXU_DTYPE)

    def body(*refs):
        a_ref, b_ref, o_ref = refs[0], refs[1], refs[3 if add else 2]

        def product():
            return lax.dot_general(operand(a_ref, a3d), operand(b_ref, b3d), _DIMS[mode], preferred_element_type=F32)

        def finish(total):
            if add:
                total = total + refs[2][...]
            if out3d:
                total = pltpu.einshape("t(jl)->tjl", total, l=LANES)
            o_ref[...] = total.astype(o_ref.dtype)

        if nk == 1:
            finish(product())
            return
        acc_ref = refs[-1]
        kk = pl.program_id(2)

        @pl.when(kk == 0)
        def _():
            acc_ref[...] = jnp.zeros(acc_ref.shape, F32)

        acc_ref[...] += product()

        @pl.when(kk == nk - 1)
        def _():
            finish(acc_ref[...])

    return pl.pallas_call(
        body, grid=grid, in_specs=specs, out_specs=o_spec, out_shape=SDS(o_shape, out_dtype),
        scratch_shapes=[] if nk == 1 else [pltpu.VMEM((tm, tn), F32)],
        compiler_params=_cparams("parallel", "parallel", "arbitrary"), name=name)(*args)


def _lanes3(blk, imap):
    return (blk[0], blk[1] // LANES, LANES), (lambda i, j, kk: imap(i, j, kk) + (0,))


def _mm(a, b, mode, out_dtype, name, a_idx=None, b_idx=None, a_cols=None, a_chunks=None, b_chunks=None, out_chunks=None,
        a3d=False, b3d=False, out3d=False, add=None):
    a2, b2 = a.shape[-2:], b.shape[-2:]
    if a3d:
        a2 = (a.shape[0], a.shape[1] * a.shape[2])
    if b3d:
        assert mode == "tn"
        b2 = (b.shape[0], b.shape[1] * b.shape[2])
    if a_chunks is not None:
        assert mode == "nt" and a.shape[-3] == a_chunks
        a2 = (a2[0], a2[1] * a_chunks)
    if b_chunks is not None:
        assert b.shape[-3] == b_chunks
        b2 = (b2[0], b2[1] * b_chunks)
    alast = a2[1] if a_cols is None else a_cols[1]
    if mode == "nn":
        m, k, n = a2[0], alast, b2[1]
        assert b2[0] == k
    elif mode == "nt":
        m, k, n = a2[0], alast, b2[0]
        assert b2[1] == k
    else:
        k, m, n = a2[0], alast, b2[1]
        assert b2[0] == k
    n_unit = n // (out_chunks or 1) // (b_chunks if b_chunks and mode != "nt" else 1)
    k_unit = k // (b_chunks if b_chunks and mode == "nt" else 1) // (a_chunks or 1)
    tm, tn = _pick(m, _TM), _pick(n_unit, _TN)
    tk = _pick_k(k, k_unit, tm, tn, a, b, out_dtype)
    nk = k // tk
    a_lane_tile = tm if mode == "tn" else tk
    off = 0
    if a_cols is not None:
        assert a_cols[0] % a_lane_tile == 0
        off = a_cols[0] // a_lane_tile

    if mode == "tn":
        a_blk, a_map = (tk, tm), (lambda i, j, kk: (kk, i + off))
    else:
        a_blk, a_map = (tm, tk), (lambda i, j, kk: (i, kk + off))
        if a_chunks is not None:
            aper = k // a_chunks // tk
            a_blk, a_map = (None, tm, tk), (lambda i, j, kk: (_chunk_of(kk, aper, a_chunks), i, _within(kk, aper, a_chunks)))
    if mode == "nt":
        b_blk, b_map = (tn, tk), (lambda i, j, kk: (j, kk))
        if b_chunks is not None:
            per = k // b_chunks // tk
            b_blk, b_map = (None, tn, tk), (lambda i, j, kk: (_chunk_of(kk, per, b_chunks), j, _within(kk, per, b_chunks)))
    else:
        b_blk, b_map = (tk, tn), (lambda i, j, kk: (kk, j))
        if b_chunks is not None:
            per = n // b_chunks // tn
            b_blk, b_map = (None, tk, tn), (lambda i, j, kk: (_chunk_of(j, per, b_chunks), kk, _within(j, per, b_chunks)))
    if a3d:
        a_blk, a_map = _lanes3(a_blk, a_map)
    if b3d:
        b_blk, b_map = _lanes3(b_blk, b_map)
    if a_idx is not None:
        a_blk, a_map0 = (None,) + a_blk, a_map
        a_map = lambda i, j, kk: (a_idx,) + a_map0(i, j, kk)
    if b_idx is not None:
        b_blk, b_map0 = (None,) + b_blk, b_map
        b_map = lambda i, j, kk: (b_idx,) + b_map0(i, j, kk)
    o_blk, o_map, o_shape = (tm, tn), (lambda i, j, kk: (i, j)), (m, n)
    if out_chunks is not None:
        oper = n // out_chunks // tn
        o_map = lambda i, j, kk: (_chunk_of(j, oper, out_chunks), i, _within(j, oper, out_chunks))
        o_blk, o_shape = (None, tm, tn), (out_chunks, m, n // out_chunks)
    if out3d:
        o_blk, o_map = _lanes3(o_blk, o_map)
        o_shape = (m, n // LANES, LANES)
    specs, args = [pl.BlockSpec(a_blk, a_map), pl.BlockSpec(b_blk, b_map)], [a, b]
    if add is not None:
        specs.append(pl.BlockSpec((tm, tn), lambda i, j, kk: (i, j)))
        args.append(add)
    return _mm_call(args, mode, out_dtype, name, (m // tm, n // tn, nk), (tm, tn, tk), specs, pl.BlockSpec(o_blk, o_map),
                    o_shape, a3d, b3d, out3d, add is not None)


S5_BAND = 2


def _mm_s5(a, b, kind, name, b_idx=None, a_cols=None, add=None):
    nb = S5_BAND
    wide3 = a if kind in ("in_dx", "out", "out_dw") else (b if kind == "in_dw" else None)
    if kind in ("in", "out_dx"):
        rows, wide = a.shape[0], b.shape[-1] if kind == "in" else b.shape[-2]
    else:
        rows, wide = wide3.shape[0], wide3.shape[1] * LANES
    sw = a_cols[1] if a_cols is not None else (b.shape[-1] if kind in ("out", "out_dw") else
                                                 (a.shape[1] if kind == "out_dx" else b.shape[-2]))
    tw, ts = wide // (2 * nb), sw // nb
    nwt = wide // tw
    off = 0 if a_cols is None else a_cols[0] // ts
    half = lambda t: _within(t, nb, nwt // nb)
    lead = (lambda blk, imap: (blk, imap)) if b_idx is None else (
        lambda blk, imap: ((None,) + blk, lambda i, j, kk: (b_idx,) + imap(i, j, kk)))
    rt = _pick(rows, _TM)
    if kind in ("in", "out_dx"):
        mode = "nn" if kind == "in" else "nt"
        a_spec = pl.BlockSpec((rt, ts), lambda i, j, kk: (i, off + half(j)))
        b_blk, b_map = ((ts, tw), lambda i, j, kk: (half(j), j)) if kind == "in" else ((tw, ts), lambda i, j, kk: (j, half(j)))
        o_blk, o_map = _lanes3((rt, tw), lambda i, j, kk: (i, j))
        return _mm_call([a, b], mode, F32, name, (rows // rt, nwt, 1), (rt, tw, ts), [a_spec, pl.BlockSpec(*lead(b_blk, b_map))],
                        pl.BlockSpec(o_blk, o_map), (rows, wide // LANES, LANES), out3d=True)
    if kind in ("out", "in_dx"):
        mode = "nn" if kind == "out" else "nt"
        a_blk, a_map = _lanes3((rt, tw), lambda i, j, kk: (i, kk * nb + j))
        b_blk, b_map = ((tw, ts), lambda i, j, kk: (kk * nb + j, j)) if kind == "out" else (
            (ts, tw), lambda i, j, kk: (j, kk * nb + j))
        specs, args = [pl.BlockSpec(a_blk, a_map), pl.BlockSpec(*lead(b_blk, b_map))], [a, b]
        if add is not None:
            specs.append(pl.BlockSpec((rt, ts), lambda i, j, kk: (i, j)))
            args.append(add)
        return _mm_call(args, mode, F32, name, (rows // rt, nb, nwt // nb), (rt, ts, tw), specs,
                        pl.BlockSpec((rt, ts), lambda i, j, kk: (i, j)), (rows, sw), a3d=True, add=add is not None)
    kt = _pick(rows, _TK)
    if kind == "out_dw":
        a_blk, a_map = _lanes3((kt, tw), lambda i, j, kk: (kk, i))
        return _mm_call([a, b], "tn", F32, name, (nwt, 1, rows // kt), (tw, ts, kt),
                        [pl.BlockSpec(a_blk, a_map), pl.BlockSpec((kt, ts), lambda i, j, kk: (kk, half(i)))],
                        pl.BlockSpec((tw, ts), lambda i, j, kk: (i, 0)), (wide, ts), a3d=True)
    assert kind == "in_dw"
    b_blk, b_map = _lanes3((kt, tw), lambda i, j, kk: (kk, j))
    return _mm_call([a, b], "tn", F32, name, (1, nwt, rows // kt), (ts, tw, kt),
                    [pl.BlockSpec((kt, ts), lambda i, j, kk: (kk, off + half(j))), pl.BlockSpec(b_blk, b_map)],
                    pl.BlockSpec((ts, tw), lambda i, j, kk: (0, j)), (ts, wide), b3d=True)


def _seg_map(nbc):
    return lambda i: (jnp.where(i < nbc, 0, 1), 0, 0)


def _rstd(v):
    return lax.rsqrt(jnp.mean(v * v, axis=-1, keepdims=True) + EPS)


def _norm_mod_fwd(x, g, mods, sh, sc, n_ctx, name):
    n, d = x.shape
    tm = _row_tile(n_ctx, n)
    nbc = n_ctx // tm

    def body(x_ref, g_ref, m_ref, h_ref):
        xv = x_ref[...]
        hn = xv * _rstd(xv) * g_ref[...]
        h_ref[...] = (hn * (1.0 + m_ref[0, sc:sc + 1, :]) + m_ref[0, sh:sh + 1, :]).astype(h_ref.dtype)

    row = pl.BlockSpec((tm, d), lambda i: (i, 0))
    return pl.pallas_call(
        body, grid=(n // tm,),
        in_specs=[row, pl.BlockSpec((1, d), lambda i: (0, 0)), pl.BlockSpec((1, 8, d), _seg_map(nbc))],
        out_specs=row, out_shape=SDS((n, d), MXU_DTYPE), compiler_params=_cparams("parallel"), name=name)(x, g, mods)


def _gate_res_fwd(x, f, g, mods, gi, n_ctx, name):
    n, d = x.shape
    tm = _row_tile(n_ctx, n)
    nbc = n_ctx // tm

    def body(x_ref, f_ref, g_ref, m_ref, o_ref):
        fv = f_ref[...]
        o_ref[...] = x_ref[...] + m_ref[0, gi:gi + 1, :] * (fv * _rstd(fv) * g_ref[...])

    row = pl.BlockSpec((tm, d), lambda i: (i, 0))
    return pl.pallas_call(
        body, grid=(n // tm,),
        in_specs=[row, row, pl.BlockSpec((1, d), lambda i: (0, 0)), pl.BlockSpec((1, 8, d), _seg_map(nbc))],
        out_specs=row, out_shape=SDS((n, d), F32), compiler_params=_cparams("parallel"), name=name)(x, f, g, mods)


def _gate_res_bwd(dx, f, g, mods, gi, n_ctx, name):
    n, d = dx.shape
    tm = _row_tile(n_ctx, n)
    nbc = n_ctx // tm

    def body(dx_ref, f_ref, g_ref, m_ref, df_ref, dgate_ref, dg_ref):
        i = pl.program_id(0)

        @pl.when(i == 0)
        def _():
            dg_ref[...] = jnp.zeros(dg_ref.shape, F32)

        @pl.when(jnp.logical_or(i == 0, i == nbc))
        def _():
            dgate_ref[...] = jnp.zeros(dgate_ref.shape, F32)

        dxv, fv, gv = dx_ref[...], f_ref[...], g_ref[...]
        rs = _rstd(fv)
        nv = fv * rs
        dgate_ref[0] += jnp.sum(dxv * (nv * gv), axis=0, keepdims=True)
        dout = dxv * m_ref[0, gi:gi + 1, :]
        dg_ref[...] += jnp.sum(dout * nv, axis=0, keepdims=True)
        dn = dout * gv
        df_ref[...] = (rs * (dn - nv * jnp.mean(dn * nv, axis=-1, keepdims=True))).astype(df_ref.dtype)

    row = pl.BlockSpec((tm, d), lambda i: (i, 0))
    vec = pl.BlockSpec((1, d), lambda i: (0, 0))
    return pl.pallas_call(
        body, grid=(n // tm,),
        in_specs=[row, row, vec, pl.BlockSpec((1, 8, d), _seg_map(nbc))],
        out_specs=[row, pl.BlockSpec((1, 1, d), _seg_map(nbc)), vec],
        out_shape=[SDS((n, d), MXU_DTYPE), SDS((2, 1, d), F32), SDS((1, d), F32)],
        compiler_params=_cparams("arbitrary"), name=name)(dx, f, g, mods)


def _norm_mod_bwd(dh, x, g, mods, sh, sc, dx_res, n_ctx, name):
    n, d = x.shape
    tm = _row_tile(n_ctx, n)
    nbc = n_ctx // tm

    def body(dh_ref, x_ref, g_ref, m_ref, r_ref, dx_ref, dss_ref, dg_ref):
        i = pl.program_id(0)

        @pl.when(i == 0)
        def _():
            dg_ref[...] = jnp.zeros(dg_ref.shape, F32)

        @pl.when(jnp.logical_or(i == 0, i == nbc))
        def _():
            dss_ref[...] = jnp.zeros(dss_ref.shape, F32)

        dhv, xv, gv = dh_ref[...], x_ref[...], g_ref[...]
        rs = _rstd(xv)
        nv = xv * rs
        dss_ref[0, 0:1, :] += jnp.sum(dhv, axis=0, keepdims=True)
        dss_ref[0, 1:2, :] += jnp.sum(dhv * (nv * gv), axis=0, keepdims=True)
        dhn = dhv * (1.0 + m_ref[0, sc:sc + 1, :])
        dg_ref[...] += jnp.sum(dhn * nv, axis=0, keepdims=True)
        dn = dhn * gv
        dx_ref[...] = r_ref[...] + rs * (dn - nv * jnp.mean(dn * nv, axis=-1, keepdims=True))

    row = pl.BlockSpec((tm, d), lambda i: (i, 0))
    vec = pl.BlockSpec((1, d), lambda i: (0, 0))
    return pl.pallas_call(
        body, grid=(n // tm,),
        in_specs=[row, row, vec, pl.BlockSpec((1, 8, d), _seg_map(nbc)), row],
        out_specs=[row, pl.BlockSpec((1, 2, d), _seg_map(nbc)), vec],
        out_shape=[SDS((n, d), F32), SDS((2, 2, d), F32), SDS((1, d), F32)],
        compiler_params=_cparams("arbitrary"), name=name)(dh, x, g, mods, dx_res)


def _loss_grad(xc, target, n_ctx, name):
    n, d = xc.shape
    tm = _row_tile(n_ctx, n)
    nbc = n_ctx // tm
    nb = n // tm

    def body(x_ref, t_ref, dx_ref, l_ref, acc_ref):
        i = pl.program_id(0)

        @pl.when(i == 0)
        def _():
            acc_ref[...] = jnp.zeros(acc_ref.shape, F32)

        @pl.when(i < nbc)
        def _():
            dx_ref[...] = jnp.zeros(dx_ref.shape, F32)

        @pl.when(i >= nbc)
        def _():
            diff = x_ref[...] - t_ref[...]
            dx_ref[...] = diff * (1.0 / d)
            acc_ref[...] += jnp.sum(diff * diff, axis=0, keepdims=True)

        @pl.when(i == nb - 1)
        def _():
            l_ref[...] = jnp.full(l_ref.shape, (0.5 / d) * jnp.sum(acc_ref[...]), F32)

    row = pl.BlockSpec((tm, d), lambda i: (i, 0))
    return pl.pallas_call(
        body, grid=(nb,),
        in_specs=[row, pl.BlockSpec((tm, d), lambda i: (jnp.maximum(i - nbc, 0), 0))],
        out_specs=[row, pl.BlockSpec((SUBLANES, LANES), lambda i: (0, 0))],
        out_shape=[SDS((n, d), F32), SDS((SUBLANES, LANES), F32)],
        scratch_shapes=[pltpu.VMEM((1, d), F32)],
        compiler_params=_cparams("arbitrary"), name=name)(xc, target)


POOL_PAD = 16


def _pool(src, pool_width, pool_group, n_ctx, bwd, out_dtype, name):
    n = src.shape[0]
    n_lat = n - n_ctx
    gb = pool_group // LANES
    segs = ((0, n_ctx, POOL_PAD), (n_ctx, n_lat, 2 * POOL_PAD + n_ctx))
    total = 3 * POOL_PAD + n

    def body(s_ref, o_ref, scr):
        j = pl.program_id(0)
        for base in (0, POOL_PAD + n_ctx, 2 * POOL_PAD + n):
            scr[pl.ds(base, POOL_PAD), :] = jnp.zeros((POOL_PAD, LANES), F32)
        for gi, w in enumerate(POOL_WINDOWS):
            @pl.when(jnp.logical_and(j >= gi * gb, j < (gi + 1) * gb))
            def _(w=w):
                half = w // 2
                offs = range(-half + 1, half + 1) if bwd else range(-half, half)
                for row0, nseg, base in segs:
                    ch = math.gcd(nseg, 256)

                    def count(c0):
                        t = c0 + lax.broadcasted_iota(jnp.int32, (ch, LANES), 0)
                        return (jnp.minimum(t + half, nseg) - jnp.maximum(t - half, 0)).astype(F32)

                    def fill(ci, carry):
                        c0 = pl.multiple_of(ci * ch, ch)
                        v = s_ref[pl.ds(row0 + c0, ch), :]
                        scr[pl.ds(base + c0, ch), :] = v / count(c0) if bwd else v
                        return carry

                    def window(ci, carry):
                        c0 = pl.multiple_of(ci * ch, ch)
                        acc = jnp.zeros((ch, LANES), F32)
                        for off in offs:
                            acc = acc + scr[pl.ds(c0 + (base + off), ch), :]
                        v = s_ref[pl.ds(row0 + c0, ch), :]
                        res = acc - v if bwd else acc / count(c0) - v
                        o_ref[pl.ds(row0 + c0, ch), :] = res.astype(o_ref.dtype)
                        return carry

                    lax.fori_loop(0, nseg // ch, fill, 0)
                    lax.fori_loop(0, nseg // ch, window, 0)

    blk = pl.BlockSpec((n, LANES), lambda j: (0, j))
    return pl.pallas_call(
        body, grid=(pool_width // LANES,), in_specs=[blk], out_specs=blk,
        out_shape=SDS((n, pool_width), out_dtype), scratch_shapes=[pltpu.VMEM((total, LANES), F32)],
        compiler_params=_cparams("parallel"), name=name)(src)


def _pool_proj_fwd(p, wp, l, scale, name):
    n, pw = p.shape
    ng, c = wp.shape[1], wp.shape[2]
    tm = _pick(n, _TM)

    def body(p_ref, w_ref, s_ref, o_ref):
        y = jnp.dot(p_ref[...], w_ref[...].astype(MXU_DTYPE), preferred_element_type=F32)
        o_ref[...] = (y * s_ref[...]).astype(o_ref.dtype)

    return pl.pallas_call(
        body, grid=(ng, n // tm),
        in_specs=[pl.BlockSpec((tm, c), lambda g, i: (i, g)), pl.BlockSpec((None, None, c, c), lambda g, i: (l, g, 0, 0)),
                  pl.BlockSpec((1, c), lambda g, i: (0, g))],
        out_specs=pl.BlockSpec((tm, c), lambda g, i: (i, g)), out_shape=SDS((n, pw), MXU_DTYPE),
        compiler_params=_cparams("parallel", "parallel"), name=name)(p, wp, scale)


def _pool_proj_bwd(p, dcat, wp, l, scale, name):
    n, pw = p.shape
    ng, c = wp.shape[1], wp.shape[2]
    tm = _pick(n, _TM)

    def body(p_ref, dy_ref, w_ref, s_ref, dp_ref, ds_ref, dw_ref):
        i = pl.program_id(1)

        @pl.when(i == 0)
        def _():
            ds_ref[...] = jnp.zeros(ds_ref.shape, F32)
            dw_ref[...] = jnp.zeros(dw_ref.shape, F32)

        pv, wv, dy = p_ref[...], w_ref[...].astype(MXU_DTYPE), dy_ref[...]
        y = jnp.dot(pv, wv, preferred_element_type=F32)
        ds_ref[...] += jnp.sum(dy * y, axis=0, keepdims=True)
        dpw = (dy * s_ref[...]).astype(MXU_DTYPE)
        dp_ref[...] = lax.dot_general(dpw, wv, _DIMS["nt"], preferred_element_type=F32)
        dw_ref[0] += lax.dot_general(pv, dpw, _DIMS["tn"], preferred_element_type=F32)

    return pl.pallas_call(
        body, grid=(ng, n // tm),
        in_specs=[pl.BlockSpec((tm, c), lambda g, i: (i, g)), pl.BlockSpec((tm, c), lambda g, i: (i, g)),
                  pl.BlockSpec((None, None, c, c), lambda g, i: (l, g, 0, 0)), pl.BlockSpec((1, c), lambda g, i: (0, g))],
        out_specs=[pl.BlockSpec((tm, c), lambda g, i: (i, g)), pl.BlockSpec((1, c), lambda g, i: (0, g)),
                   pl.BlockSpec((1, c, c), lambda g, i: (g, 0, 0))],
        out_shape=[SDS((n, pw), F32), SDS((1, pw), F32), SDS((ng, c, c), F32)],
        compiler_params=_cparams("arbitrary", "arbitrary"), name=name)(p, dcat, wp, scale)


def _disc_math(a_re, a_im, logdt, b_re, b_im):
    dt = jnp.exp(logdt)
    mag = jnp.exp(a_re * dt)
    lam_re = mag * jnp.cos(a_im * dt)
    lam_im = mag * jnp.sin(a_im * dt)
    denom = a_re * a_re + a_im * a_im
    nr, ni = lam_re - 1.0, lam_im
    f_re = ((nr * a_re + ni * a_im) / denom)[:, None, :]
    f_im = ((ni * a_re - nr * a_im) / denom)[:, None, :]
    return lam_re, lam_im, f_re * b_re - f_im * b_im, f_re * b_im + f_im * b_re


def _disc_fwd(a_re, a_im, logdt, b_re, b_im, name):
    def body(ar, ai, ld, br, bi, o_lr, o_li, o_br, o_bi):
        lr, li, bbr, bbi = _disc_math(ar[...], ai[...], ld[...], br[...], bi[...])
        o_lr[...] = lr
        o_li[...] = li
        o_br[...] = bbr
        o_bi[...] = bbi

    return pl.pallas_call(
        body, out_shape=[SDS(a_re.shape, F32), SDS(a_re.shape, F32), SDS(b_re.shape, F32), SDS(b_re.shape, F32)],
        compiler_params=_cparams(), name=name)(a_re, a_im, logdt, b_re, b_im)


def _disc_bwd(a_re, a_im, logdt, b_re, b_im, d_lr, d_li, d_bbr, d_bbi, group, name):
    rows, gp = a_re.shape

    def body(ar, ai, ld, br, bi, g_lr, g_li, g_br, g_bi, o_ar, o_ai, o_ld, o_br, o_bi):
        _, vjp = jax.vjp(_disc_math, ar[...], ai[...], ld[...], br[...], bi[...])
        dar, dai, dld, dbr, dbi = vjp((g_lr[...], g_li[...], g_br[...], g_bi[...]))
        o_ar[...] = dar
        o_ai[...] = dai
        state = lax.broadcasted_iota(jnp.int32, (gp, LANES), 0)
        first = lax.broadcasted_iota(jnp.int32, (gp, LANES), 1) * group
        sel = jnp.logical_and(state >= first, state < first + group).astype(F32)
        o_ld[...] = jnp.dot(dld, sel, precision=HIGHEST, preferred_element_type=F32)
        o_br[...] = dbr
        o_bi[...] = dbi

    return pl.pallas_call(
        body, out_shape=[SDS(a_re.shape, F32), SDS(a_re.shape, F32), SDS((rows, LANES), F32),
                         SDS(b_re.shape, F32), SDS(b_re.shape, F32)],
        compiler_params=_cparams(), name=name)(a_re, a_im, logdt, b_re, b_im, d_lr, d_li, d_bbr, d_bbi)


def _scan_maps(nbc, nb):
    nbl = nb - nbc
    fwd0 = lambda i: (i, 0, 0)
    fwd1 = lambda i: (jnp.where(i < nbc, nbc - 1 - i, nb - 1 - (i - nbc)), 0, 0)
    adj0 = lambda i: (nb - 1 - i, 0, 0)
    adj1 = lambda i: (jnp.where(i < nbl, nbc + i, i - nbl), 0, 0)
    return fwd0, fwd1, adj0, adj1


def _scan_fwd(bu0, bu1, lam, n_ctx, name):
    n, s2, _ = bu0.shape
    s = s2 // 2
    tt = math.gcd(math.gcd(n_ctx, n - n_ctx), 128)
    nbc, nb = n_ctx // tt, n // tt
    fwd0, fwd1, _, _ = _scan_maps(nbc, nb)

    def body(b0_ref, b1_ref, lam_ref, h0_ref, h1_ref, st_ref):
        @pl.when(pl.program_id(0) == 0)
        def _():
            st_ref[...] = jnp.zeros(st_ref.shape, F32)

        lam = [(lam_ref[0], lam_ref[1]), (lam_ref[2], lam_ref[3])]
        lam2 = [(lr * lr - li * li, 2.0 * lr * li) for lr, li in lam]

        def pair(b_ref, h_ref, ra, rb, lm, lm2, hr, hi):
            (lr, li), (l2r, l2i) = lm, lm2
            bar, bai = b_ref[ra, 0:s, :], b_ref[ra, s:s2, :]
            cr = lr * bar - li * bai + b_ref[rb, 0:s, :]
            ci = lr * bai + li * bar + b_ref[rb, s:s2, :]
            h_ref[ra, 0:s, :] = lr * hr - li * hi + bar
            h_ref[ra, s:s2, :] = lr * hi + li * hr + bai
            nr = l2r * hr - l2i * hi + cr
            ni = l2r * hi + l2i * hr + ci
            h_ref[rb, 0:s, :] = nr
            h_ref[rb, s:s2, :] = ni
            return nr, ni

        def step(jj, carry):
            h0r, h0i, h1r, h1i = carry
            ja = 2 * jj
            ta = tt - 1 - ja
            h0r, h0i = pair(b0_ref, h0_ref, ja, ja + 1, lam[0], lam2[0], h0r, h0i)
            h1r, h1i = pair(b1_ref, h1_ref, ta, ta - 1, lam[1], lam2[1], h1r, h1i)
            return h0r, h0i, h1r, h1i

        out = lax.fori_loop(0, tt // 2, step, (st_ref[0], st_ref[1], st_ref[2], st_ref[3]), unroll=2)
        for q in range(4):
            st_ref[q] = out[q]

    blk = (tt, s2, LANES)
    return pl.pallas_call(
        body, grid=(nb,),
        in_specs=[pl.BlockSpec(blk, fwd0), pl.BlockSpec(blk, fwd1), pl.BlockSpec((4, s, LANES), lambda i: (0, 0, 0))],
        out_specs=[pl.BlockSpec(blk, fwd0), pl.BlockSpec(blk, fwd1)],
        out_shape=[SDS(bu0.shape, F32), SDS(bu1.shape, F32)],
        scratch_shapes=[pltpu.VMEM((4, s, LANES), F32)],
        compiler_params=_cparams("arbitrary"), name=name)(bu0, bu1, lam)


def _scan_bwd(dh0, dh1, h0, h1, lam, n_ctx, name):
    n, s2, _ = dh0.shape
    s = s2 // 2
    tt = math.gcd(math.gcd(n_ctx, n - n_ctx), 128)
    nbc, nb = n_ctx // tt, n // tt
    _, _, adj0, adj1 = _scan_maps(nbc, nb)

    def body(d0_ref, d1_ref, h0_ref, h1_ref, lam_ref, a0_ref, a1_ref, dl_ref, st_ref, acc_ref):
        i = pl.program_id(0)

        @pl.when(i == 0)
        def _():
            st_ref[...] = jnp.zeros(st_ref.shape, F32)
            acc_ref[...] = jnp.zeros(acc_ref.shape, F32)

        lam = [(lam_ref[0], lam_ref[1]), (lam_ref[2], lam_ref[3])]
        lam2 = [(lr * lr - li * li, 2.0 * lr * li) for lr, li in lam]

        def pair(d_ref, h_ref, a_ref, ra, rb, lm, lm2, ar, ai, cr, ci):
            (lr, li), (l2r, l2i) = lm, lm2
            dar, dai = d_ref[ra, 0:s, :], d_ref[ra, s:s2, :]
            er = lr * dar + li * dai + d_ref[rb, 0:s, :]
            ei = lr * dai - li * dar + d_ref[rb, s:s2, :]
            mr = lr * ar + li * ai + dar
            mi = lr * ai - li * ar + dai
            a_ref[ra, 0:s, :] = mr
            a_ref[ra, s:s2, :] = mi
            gar, gai = h_ref[ra, 0:s, :], h_ref[ra, s:s2, :]
            gbr, gbi = h_ref[rb, 0:s, :], h_ref[rb, s:s2, :]
            cr = cr + ((ar * gar + ai * gai) + (mr * gbr + mi * gbi))
            ci = ci + ((ai * gar - ar * gai) + (mi * gbr - mr * gbi))
            nr = l2r * ar + l2i * ai + er
            ni = l2r * ai - l2i * ar + ei
            a_ref[rb, 0:s, :] = nr
            a_ref[rb, s:s2, :] = ni
            return nr, ni, cr, ci

        def step(jj, carry):
            a0r, a0i, a1r, a1i, c0r, c0i, c1r, c1i = carry
            ja = 2 * jj
            ta = tt - 1 - ja
            a0r, a0i, c0r, c0i = pair(d0_ref, h0_ref, a0_ref, ta, ta - 1, lam[0], lam2[0], a0r, a0i, c0r, c0i)
            a1r, a1i, c1r, c1i = pair(d1_ref, h1_ref, a1_ref, ja, ja + 1, lam[1], lam2[1], a1r, a1i, c1r, c1i)
            return a0r, a0i, a1r, a1i, c0r, c0i, c1r, c1i

        init = tuple(st_ref[q] for q in range(4)) + tuple(acc_ref[q] for q in range(4))
        out = lax.fori_loop(0, tt // 2, step, init, unroll=2)
        for q in range(4):
            st_ref[q] = out[q]
            acc_ref[q] = out[4 + q]

        @pl.when(i == nb - 1)
        def _():
            for q in range(4):
                dl_ref[q] = out[4 + q]

    blk = (tt, s2, LANES)
    small = pl.BlockSpec((4, s, LANES), lambda i: (0, 0, 0))
    return pl.pallas_call(
        body, grid=(nb,),
        in_specs=[pl.BlockSpec(blk, adj0), pl.BlockSpec(blk, adj1), pl.BlockSpec(blk, adj0), pl.BlockSpec(blk, adj1), small],
        out_specs=[pl.BlockSpec(blk, adj0), pl.BlockSpec(blk, adj1), small],
        out_shape=[SDS(dh0.shape, F32), SDS(dh1.shape, F32), SDS((4, s, LANES), F32)],
        scratch_shapes=[pltpu.VMEM((4, s, LANES), F32), pltpu.VMEM((4, s, LANES), F32)],
        compiler_params=_cparams("arbitrary"), name=name)(dh0, dh1, h0, h1, lam)


def _gelu(v):
    th = jnp.tanh(GELU_C0 * (v + GELU_C1 * v * v * v))
    return 0.5 * v * (1.0 + th), th


def _ssm_head_fwd(y, u, ssm_d, wg, l, name):
    n, sw = y.shape
    ucol = u.shape[1] // sw - 1
    tm = _pick(n, _TM)

    def body(y_ref, u_ref, d_ref, w_ref, o_ref):
        act, _ = _gelu(y_ref[...] + d_ref[...] * u_ref[...])
        q = jnp.dot(act.astype(MXU_DTYPE), w_ref[...].astype(MXU_DTYPE), preferred_element_type=F32)
        o_ref[...] = (act * jax.nn.sigmoid(q)).astype(o_ref.dtype)

    row = pl.BlockSpec((tm, sw), lambda i: (i, 0))
    return pl.pallas_call(
        body, grid=(n // tm,),
        in_specs=[row, pl.BlockSpec((tm, sw), lambda i: (i, ucol)), pl.BlockSpec((1, sw), lambda i: (0, 0)),
                  pl.BlockSpec((None, sw, sw), lambda i: (l, 0, 0))],
        out_specs=row, out_shape=SDS((n, sw), MXU_DTYPE), compiler_params=_cparams("parallel"), name=name)(y, u, ssm_d, wg)


def _ssm_head_bwd(dcat, y, u, ssm_d, wg, l, name):
    n, sw = y.shape
    ucol = u.shape[1] // sw - 1
    tm = _pick(n, _TM)

    def body(do_ref, y_ref, u_ref, d_ref, w_ref, dy_ref, du_ref, act_ref, dq_ref, dd_ref):
        @pl.when(pl.program_id(0) == 0)
        def _():
            dd_ref[...] = jnp.zeros(dd_ref.shape, F32)

        uv, dv, do = u_ref[...], d_ref[...], do_ref[...]
        yf = y_ref[...] + dv * uv
        act, th = _gelu(yf)
        wv = w_ref[...].astype(MXU_DTYPE)
        sg = jax.nn.sigmoid(jnp.dot(act.astype(MXU_DTYPE), wv, preferred_element_type=F32))
        dq = (do * act * sg * (1.0 - sg)).astype(MXU_DTYPE)
        dact = do * sg + lax.dot_general(dq, wv, _DIMS["nt"], preferred_element_type=F32)
        dgelu = 0.5 * (1.0 + th) + 0.5 * yf * (1.0 - th * th) * GELU_C0 * (1.0 + 3.0 * GELU_C1 * yf * yf)
        dyf = dact * dgelu
        dy_ref[...] = dyf.astype(dy_ref.dtype)
        du_ref[...] = dyf * dv
        act_ref[...] = act.astype(act_ref.dtype)
        dq_ref[...] = dq
        dd_ref[...] += jnp.sum(dyf * uv, axis=0, keepdims=True)

    row = pl.BlockSpec((tm, sw), lambda i: (i, 0))
    last = pl.BlockSpec((tm, sw), lambda i: (i, ucol))
    vec = pl.BlockSpec((1, sw), lambda i: (0, 0))
    return pl.pallas_call(
        body, grid=(n // tm,),
        in_specs=[last, row, last, vec, pl.BlockSpec((None, sw, sw), lambda i: (l, 0, 0))],
        out_specs=[row, row, row, row, vec],
        out_shape=[SDS((n, sw), MXU_DTYPE), SDS((n, sw), F32), SDS((n, sw), MXU_DTYPE), SDS((n, sw), MXU_DTYPE),
                   SDS((1, sw), F32)],
        compiler_params=_cparams("arbitrary"), name=name)(dcat, y, u, ssm_d, wg)


def _assemble_du(du_pool, du_dir, du_proj, name):
    n, pw = du_pool.shape
    sw = du_dir.shape[1]
    tm = _pick(n, _TM)

    def body(p_ref, a_ref, b_ref, o_ref):
        o_ref[:, 0:pw] = p_ref[...].astype(o_ref.dtype)
        o_ref[:, pw:pw + sw] = (a_ref[...] + b_ref[...]).astype(o_ref.dtype)

    return pl.pallas_call(
        body, grid=(n // tm,),
        in_specs=[pl.BlockSpec((tm, pw), lambda i: (i, 0)), pl.BlockSpec((tm, sw), lambda i: (i, 0)),
                  pl.BlockSpec((tm, sw), lambda i: (i, 0))],
        out_specs=pl.BlockSpec((tm, pw + sw), lambda i: (i, 0)), out_shape=SDS((n, pw + sw), MXU_DTYPE),
        compiler_params=_cparams("parallel"), name=name)(du_pool, du_dir, du_proj)


CONV_PAD = GRID_W + SUBLANES


def _conv_layout(n, n_ctx):
    return CONV_PAD, 2 * CONV_PAD + n_ctx, 3 * CONV_PAD + n


def _col_masks(ch):
    col = lax.broadcasted_iota(jnp.int32, (ch, LANES), 0) % GRID_W
    return col != 0, col != GRID_W - 1


def _fill_padded(scr, src_ref, n, n_ctx):
    base_c, base_l, total = _conv_layout(n, n_ctx)
    for base in (0, base_c + n_ctx, base_l + n - n_ctx):
        scr[pl.ds(base, CONV_PAD), :] = jnp.zeros((CONV_PAD, LANES), F32)
    for row0, nseg, base in ((0, n_ctx, base_c), (n_ctx, n - n_ctx, base_l)):
        ch = math.gcd(nseg, 512)

        def copy(ci, carry, row0=row0, base=base, ch=ch):
            c0 = pl.multiple_of(ci * ch, ch)
            scr[pl.ds(base + c0, ch), :] = src_ref[pl.ds(row0 + c0, ch), :]
            return carry

        lax.fori_loop(0, nseg // ch, copy, 0)


def _conv_ctx(scr, k_ref, base, c0, ch, sign):
    acc = scr[pl.ds(c0 + base, ch), :] * k_ref[4:5, :]
    acc = acc + scr[pl.ds(c0 + (base - sign), ch), :] * k_ref[3:4, :]
    return acc + scr[pl.ds(c0 + (base + sign), ch), :] * k_ref[5:6, :]


def _conv_lat(scr, k_ref, base, c0, ch, sign, m_l, m_r):
    cols = []
    for j in range(3):
        acc = None
        for i in range(3):
            off = sign * (GRID_W * (i - 1) + (j - 1))
            term = scr[pl.ds(c0 + (base + off), ch), :] * k_ref[3 * i + j:3 * i + j + 1, :]
            acc = term if acc is None else acc + term
        cols.append(acc)
    first, last = (m_l, m_r) if sign > 0 else (m_r, m_l)
    return cols[1] + jnp.where(first, cols[0], 0.0) + jnp.where(last, cols[2], 0.0)


def _conv_chunk(n_lat):
    return math.gcd(n_lat, 256)


def _conv_glu_fwd(z, wk, n_ctx, name):
    n, f2 = z.shape
    dff = f2 // 2
    nvt = dff // LANES
    n_lat = n - n_ctx
    base_c, base_l, total = _conv_layout(n, n_ctx)
    ch = _conv_chunk(n_lat)
    assert ch % GRID_W == 0

    def body(zv_ref, zg_ref, kv_ref, kg_ref, a_ref, cv_ref, cg_ref, sv, sg):
        _fill_padded(sv, zv_ref, n, n_ctx)
        _fill_padded(sg, zg_ref, n, n_ctx)

        def emit(cv, cg, row, rows):
            cv_ref[pl.ds(row, rows), :] = cv
            cg_ref[pl.ds(row, rows), :] = cg
            a_ref[pl.ds(row, rows), :] = (cv * cg * jax.nn.sigmoid(cg)).astype(a_ref.dtype)

        emit(_conv_ctx(sv, kv_ref, base_c, 0, n_ctx, 1), _conv_ctx(sg, kg_ref, base_c, 0, n_ctx, 1), 0, n_ctx)
        m_l, m_r = _col_masks(ch)

        def lat(ci, carry):
            c0 = pl.multiple_of(ci * ch, ch)
            emit(_conv_lat(sv, kv_ref, base_l, c0, ch, 1, m_l, m_r), _conv_lat(sg, kg_ref, base_l, c0, ch, 1, m_l, m_r),
                 n_ctx + c0, ch)
            return carry

        lax.fori_loop(0, n_lat // ch, lat, 0)

    col = lambda shift: pl.BlockSpec((n, LANES), lambda j: (0, j + shift))
    kcol = lambda shift: pl.BlockSpec((9, LANES), lambda j: (0, j + shift))
    return pl.pallas_call(
        body, grid=(nvt,), in_specs=[col(0), col(nvt), kcol(0), kcol(nvt)], out_specs=[col(0), col(0), col(0)],
        out_shape=[SDS((n, dff), MXU_DTYPE), SDS((n, dff), F32), SDS((n, dff), F32)],
        scratch_shapes=[pltpu.VMEM((total, LANES), F32), pltpu.VMEM((total, LANES), F32)],
        compiler_params=_cparams("parallel"), name=name)(z, z, wk, wk)


def _conv_glu_bwd(z, cv, cg, da, wk, n_ctx, name):
    n, f2 = z.shape
    dff = f2 // 2
    nvt = dff // LANES
    n_lat = n - n_ctx
    base_c, base_l, total = _conv_layout(n, n_ctx)
    ch = _conv_chunk(n_lat)
    assert ch % GRID_W == 0
    ctx_taps = [(1, 0), (1, 1), (1, 2)]
    lat_taps = [(i, j) for i in range(3) for j in range(3)]

    def tap_sums(acc, scr, d, base, c0, rows, taps, masks):
        acc = list(acc)
        by_col = (d, d, d) if masks is None else (jnp.where(masks[0], d, 0.0), d, jnp.where(masks[1], d, 0.0))
        for i, j in taps:
            src = scr[pl.ds(c0 + (base + GRID_W * (i - 1) + (j - 1)), rows), :]
            acc[3 * i + j] = acc[3 * i + j] + jnp.sum((src * by_col[j]).reshape(rows // SUBLANES, SUBLANES, LANES), axis=0)
        return acc

    def body(zv_ref, zg_ref, cv_ref, cg_ref, da_ref, kv_ref, kg_ref, dz_ref, dkv_ref, dkg_ref, a_ref, sv, sg, dv, dg):
        _fill_padded(sv, zv_ref, n, n_ctx)
        _fill_padded(sg, zg_ref, n, n_ctx)
        for base in (0, base_c + n_ctx, base_l + n_lat):
            dv[pl.ds(base, CONV_PAD), :] = jnp.zeros((CONV_PAD, LANES), F32)
            dg[pl.ds(base, CONV_PAD), :] = jnp.zeros((CONV_PAD, LANES), F32)
        m_l, m_r = _col_masks(ch)

        def first_pass(row, pad_row, rows):
            cv, cg = cv_ref[pl.ds(row, rows), :], cg_ref[pl.ds(row, rows), :]
            sig = jax.nn.sigmoid(cg)
            silu = cg * sig
            a_ref[pl.ds(row, rows), :] = (cv * silu).astype(a_ref.dtype)
            dav = da_ref[pl.ds(row, rows), :]
            dcv = dav * silu
            dcg = dav * cv * (sig * (1.0 + cg * (1.0 - sig)))
            dv[pl.ds(pad_row, rows), :] = dcv
            dg[pl.ds(pad_row, rows), :] = dcg
            return dcv, dcg

        zero = [jnp.zeros((SUBLANES, LANES), F32) for _ in range(9)]
        dcv, dcg = first_pass(0, base_c, n_ctx)
        accv = tap_sums(zero, sv, dcv, base_c, 0, n_ctx, ctx_taps, None)
        accg = tap_sums(zero, sg, dcg, base_c, 0, n_ctx, ctx_taps, None)

        def lat1(ci, carry):
            accv, accg = carry
            c0 = pl.multiple_of(ci * ch, ch)
            dcv, dcg = first_pass(n_ctx + c0, base_l + c0, ch)
            accv = tap_sums(accv, sv, dcv, base_l, c0, ch, lat_taps, (m_l, m_r))
            accg = tap_sums(accg, sg, dcg, base_l, c0, ch, lat_taps, (m_l, m_r))
            return tuple(accv), tuple(accg)

        accv, accg = lax.fori_loop(0, n_lat // ch, lat1, (tuple(accv), tuple(accg)))
        for t in range(9):
            dkv_ref[t:t + 1, :] = jnp.sum(accv[t], axis=0, keepdims=True)
            dkg_ref[t:t + 1, :] = jnp.sum(accg[t], axis=0, keepdims=True)

        dz_ref[0, pl.ds(0, n_ctx), :] = _conv_ctx(dv, kv_ref, base_c, 0, n_ctx, -1).astype(dz_ref.dtype)
        dz_ref[1, pl.ds(0, n_ctx), :] = _conv_ctx(dg, kg_ref, base_c, 0, n_ctx, -1).astype(dz_ref.dtype)

        def lat2(ci, carry):
            c0 = pl.multiple_of(ci * ch, ch)
            dz_ref[0, pl.ds(n_ctx + c0, ch), :] = _conv_lat(dv, kv_ref, base_l, c0, ch, -1, m_l, m_r).astype(dz_ref.dtype)
            dz_ref[1, pl.ds(n_ctx + c0, ch), :] = _conv_lat(dg, kg_ref, base_l, c0, ch, -1, m_l, m_r).astype(dz_ref.dtype)
            return carry

        lax.fori_loop(0, n_lat // ch, lat2, 0)

    col = lambda shift: pl.BlockSpec((n, LANES), lambda j: (0, j + shift))
    kcol = lambda shift: pl.BlockSpec((9, LANES), lambda j: (0, j + shift))
    pad = pltpu.VMEM((total, LANES), F32)
    return pl.pallas_call(
        body, grid=(nvt,), in_specs=[col(0), col(nvt), col(0), col(0), col(0), kcol(0), kcol(nvt)],
        out_specs=[pl.BlockSpec((2, n, LANES), lambda j: (0, 0, j)), kcol(0), kcol(0), col(0)],
        out_shape=[SDS((2, n, dff), MXU_DTYPE), SDS((9, dff), F32), SDS((9, dff), F32), SDS((n, dff), MXU_DTYPE)],
        scratch_shapes=[pad, pad, pad, pad],
        compiler_params=_cparams("parallel"), name=name)(z, z, cv, cg, da, wk, wk)


def _silu(v):
    return v * jax.nn.sigmoid(v)


def _ada_fwd(cond, w_ada, b_shard, name):
    nl, d, cols = w_ada.shape
    tn = _pick(cols, (512, 256, 128))

    def body(c_ref, w_ref, b_ref, o_ref):
        o_ref[...] = jnp.dot(_silu(c_ref[...]), w_ref[...], precision=HIGHEST, preferred_element_type=F32) + b_ref[...]

    return pl.pallas_call(
        body, grid=(nl, cols // tn),
        in_specs=[pl.BlockSpec(cond.shape, lambda l, j: (0, 0)), pl.BlockSpec((None, d, tn), lambda l, j: (l, 0, j)),
                  pl.BlockSpec((None, 1, tn), lambda l, j: (l, 0, j))],
        out_specs=pl.BlockSpec((None, cond.shape[0], tn), lambda l, j: (l, 0, j)),
        out_shape=SDS((nl, cond.shape[0], cols), F32),
        compiler_params=_cparams("parallel", "parallel"), name=name)(cond, w_ada, b_shard)


def _ada_dw(cond, dmod, name):
    nl, rows, cols = dmod.shape
    d = cond.shape[1]
    tn = _pick(cols, (512, 256, 128))

    def body(c_ref, g_ref, o_ref):
        o_ref[...] = lax.dot_general(_silu(c_ref[...]), g_ref[...], _DIMS["tn"], precision=HIGHEST,
                                     preferred_element_type=F32)

    return pl.pallas_call(
        body, grid=(nl, cols // tn),
        in_specs=[pl.BlockSpec(cond.shape, lambda l, j: (0, 0)), pl.BlockSpec((None, rows, tn), lambda l, j: (l, 0, j))],
        out_specs=pl.BlockSpec((None, d, tn), lambda l, j: (l, 0, j)), out_shape=SDS((nl, d, cols), F32),
        compiler_params=_cparams("parallel", "parallel"), name=name)(cond, dmod)


def _ada_dcond(dmod, w_ada, name):
    nl, rows, cols = dmod.shape
    d = w_ada.shape[1]
    tn = _pick(cols, (512, 256, 128))

    def body(g_ref, w_ref, o_ref):
        @pl.when(jnp.logical_and(pl.program_id(0) == 0, pl.program_id(1) == 0))
        def _():
            o_ref[...] = jnp.zeros(o_ref.shape, F32)

        o_ref[...] += lax.dot_general(g_ref[...], w_ref[...], _DIMS["nt"], precision=HIGHEST, preferred_element_type=F32)

    return pl.pallas_call(
        body, grid=(nl, cols // tn),
        in_specs=[pl.BlockSpec((None, rows, tn), lambda l, j: (l, 0, j)), pl.BlockSpec((None, d, tn), lambda l, j: (l, 0, j))],
        out_specs=pl.BlockSpec((rows, d), lambda l, j: (0, 0)), out_shape=SDS((rows, d), F32),
        compiler_params=_cparams("arbitrary", "arbitrary"), name=name)(dmod, w_ada)


def _ada_rows(dmod_all, name):
    nd, nl, _, w = dmod_all.shape
    tn = _pick(w, (2048, 1024, 512, 256, 128))

    def body(g_ref, rows_ref, db_ref):
        ctx = g_ref[0, 0, 0:1, :]
        for b in range(1, nd):
            ctx = ctx + g_ref[b, 0, 0:1, :]
        total = ctx
        for b in range(nd):
            lat = g_ref[b, 0, 1:2, :]
            rows_ref[b:b + 1, :] = lat
            total = total + lat
        rows_ref[nd:nd + 1, :] = ctx
        rows_ref[nd + 1:16, :] = jnp.zeros((16 - nd - 1, tn), F32)
        db_ref[...] = total

    return pl.pallas_call(
        body, grid=(nl, w // tn),
        in_specs=[pl.BlockSpec((nd, 1, 2, tn), lambda l, j: (0, l, 0, j))],
        out_specs=[pl.BlockSpec((None, 16, tn), lambda l, j: (l, 0, j)), pl.BlockSpec((None, 1, tn), lambda l, j: (l, 0, j))],
        out_shape=[SDS((nl, 16, w), F32), SDS((nl, 1, w), F32)],
        compiler_params=_cparams("parallel", "parallel"), name=name)(dmod_all)


def _ada_dctx(parts, c_ctx, row, name):
    def body(p_ref, c_ref, o_ref):
        ds = p_ref[0, row:row + 1, :]
        for k in range(1, p_ref.shape[0]):
            ds = ds + p_ref[k, row:row + 1, :]
        cv = c_ref[...]
        sg = jax.nn.sigmoid(cv)
        o_ref[...] = ds * (sg * (1.0 + cv * (1.0 - sg)))

    return pl.pallas_call(body, out_shape=SDS(c_ctx.shape, F32), compiler_params=_cparams(), name=name)(parts, c_ctx)


ROW_BLOCK_BYTES = 1 << 20


def _as_rows(shape):
    size = math.prod(shape)
    cols = shape[-1] if len(shape) >= 2 and shape[-1] % LANES == 0 else _pick(size, (1024, 512, 256, 128))
    rows = size // cols
    fits = [t for t in (512, 256, 128, 64, 32, 16, 8) if t * cols * 4 <= ROW_BLOCK_BYTES]
    return rows, cols, _pick(rows, fits)


def _adamw(w, g, m, v, name):
    rows, cols, tr = _as_rows(w.shape)
    c1 = 1.0 / (1.0 - ADAM_B1 ** ADAM_STEP)
    c2 = 1.0 / (1.0 - ADAM_B2 ** ADAM_STEP)

    def body(w_ref, g_ref, m_ref, v_ref, d_ref, nm_ref, nv_ref):
        gv = g_ref[...]
        nm = ADAM_B1 * m_ref[...] + (1.0 - ADAM_B1) * gv
        nv = ADAM_B2 * v_ref[...] + (1.0 - ADAM_B2) * (gv * gv)
        nm_ref[...] = nm
        nv_ref[...] = nv
        d_ref[...] = -ADAM_LR * ((nm * c1) / (jnp.sqrt(nv * c2) + ADAM_EPS) + ADAM_WD * w_ref[...])

    blk = pl.BlockSpec((tr, cols), lambda i: (i, 0))
    outs = pl.pallas_call(
        body, grid=(rows // tr,), in_specs=[blk] * 4, out_specs=[blk] * 3, out_shape=[SDS((rows, cols), F32)] * 3,
        compiler_params=_cparams("parallel"), name=name)(*[t.reshape(rows, cols) for t in (w, g, m, v)])
    return tuple(o.reshape(w.shape) for o in outs)


def _tile_rows(rows, cols, dtype):
    size = jnp.dtype(dtype).itemsize
    fits = [t for t in (512, 256, 128, 64, 32, 16, 8) if t * cols * size <= ROW_BLOCK_BYTES and t * size >= 32]
    return _pick(rows, fits)


def _scalar_spec(grid, in_specs, out_specs):
    return pltpu.PrefetchScalarGridSpec(num_scalar_prefetch=1, grid=grid, in_specs=in_specs, out_specs=out_specs)


def _place_chunk(shard, k_idx, dtype, name):
    nl, rows, cols = shard.shape
    tr = _tile_rows(rows, cols, dtype)

    def body(k_ref, s_ref, o_ref):
        o_ref[...] = s_ref[...].astype(o_ref.dtype)

    return pl.pallas_call(
        body, out_shape=SDS((nl, N_CHIPS, rows, cols), dtype),
        grid_spec=_scalar_spec((nl, rows // tr), [pl.BlockSpec((None, tr, cols), lambda l, i, k: (l, i, 0))],
                               pl.BlockSpec((None, None, tr, cols), lambda l, i, k: (l, k[0], i, 0))),
        compiler_params=_cparams("parallel", "parallel"), name=name)(k_idx, shard)


def _pair_sum(grads, recv, c_idx, name):
    half, nch, rows, cols = recv.shape
    tr = _tile_rows(rows, cols, recv.dtype)

    def body(c_ref, g_ref, r_ref, o_ref):
        o_ref[...] = (g_ref[...].astype(F32) + r_ref[...].astype(F32)).astype(o_ref.dtype)

    blk = pl.BlockSpec((None, None, tr, cols), lambda h, q, i, c: (h, q, i, 0))
    return pl.pallas_call(
        body, out_shape=SDS(recv.shape, recv.dtype),
        grid_spec=_scalar_spec((half, nch, rows // tr),
                               [pl.BlockSpec((None, None, tr, cols), lambda h, q, i, c: (c[0] * half + h, q, i, 0)), blk], blk),
        compiler_params=_cparams("parallel", "parallel", "parallel"), name=name)(c_idx, grads, recv)


def _chip_sum(parts, recv, kc_idx, name):
    half, nch, rows, cols = parts.shape
    tr = _tile_rows(rows, cols, F32)

    def body(kc_ref, p_ref, r_ref, o_ref):
        acc = p_ref[...].astype(F32)
        for s in range(r_ref.shape[0]):
            acc = acc + r_ref[s].astype(F32)
        o_ref[...] = acc

    return pl.pallas_call(
        body, out_shape=SDS((2 * half, rows, cols), F32),
        grid_spec=_scalar_spec((half, rows // tr),
                               [pl.BlockSpec((None, None, tr, cols), lambda h, i, kc: (h, kc[0], i, 0)),
                                pl.BlockSpec((nch - 1, None, tr, cols), lambda h, i, kc: (0, h, i, 0))],
                               pl.BlockSpec((None, tr, cols), lambda h, i, kc: (kc[1] * half + h, i, 0))),
        compiler_params=_cparams("parallel", "parallel"), name=name)(kc_idx, parts, recv)


PIECE_BYTES = 3 << 20
MAX_PIECES = 16
PIECE_ROW_ALIGN = 16


def _coords():
    return lax.axis_index("x"), lax.axis_index("y"), lax.axis_index("c")


def _other_chips(x, y):
    return [(1 - x, y), (x, 1 - y), (1 - x, 1 - y)]


def _row_pieces(rows, nbytes):
    pieces = 1
    while (pieces < MAX_PIECES and nbytes // pieces > PIECE_BYTES and rows % (2 * pieces * PIECE_ROW_ALIGN) == 0):
        pieces *= 2
    step = rows // pieces
    return [pl.ds(i * step, step) for i in range(pieces)]


def _nbytes(shape, dtype):
    return math.prod(shape) * jnp.dtype(dtype).itemsize


def _offsets(counts):
    out, pos = [], 0
    for cnt in counts:
        out.append(pos)
        pos += cnt
    return out, pos


def _gather_chips(placed, name):
    nt = len(placed)
    half = [p.shape[0] // 2 for p in placed]
    pieces = [_row_pieces(p.shape[2], _nbytes((h,) + p.shape[2:], p.dtype)) for p, h in zip(placed, half)]
    base, total = _offsets([len(p) for p in pieces])

    def body(*refs):
        o_refs = refs[nt:2 * nt]
        s_nbr, r_nbr, s_fwd, r_fwd, s_sib, r_sib = refs[2 * nt:]
        x, y, c = _coords()
        k, kx, ky, kd = 2 * x + y, 2 * (1 - x) + y, 2 * x + (1 - y), 2 * (1 - x) + (1 - y)
        across_x, across_y, sibling = (1 - x, y, c), (x, 1 - y, c), (x, y, 1 - c)
        sends = []

        def copy(o_ref, rows, slot, rs, ssem, rsem, q, to):
            return pltpu.make_async_remote_copy(
                src_ref=o_ref.at[rows, slot, rs], dst_ref=o_ref.at[rows, slot, rs], send_sem=ssem.at[q], recv_sem=rsem.at[q],
                device_id=to, device_id_type=MESH)

        def start(cp):
            cp.start()
            sends.append(cp)

        work = [(t, i, rs, base[t] + i, 2 * i < len(pieces[t]) or len(pieces[t]) == 1)
                for t in range(nt) for i, rs in enumerate(pieces[t])]
        for t, i, rs, q, _ in work:
            mine = pl.ds(c * half[t], half[t])
            start(copy(o_refs[t], mine, k, rs, s_nbr, r_nbr, 2 * q, across_x))
            start(copy(o_refs[t], mine, k, rs, s_nbr, r_nbr, 2 * q + 1, across_y))
        for t, i, rs, q, via_x in work:
            mine = pl.ds(c * half[t], half[t])
            copy(o_refs[t], mine, kx, rs, s_nbr, r_nbr, 2 * q, across_x).wait_recv()
            start(copy(o_refs[t], mine, kx, rs, s_sib, r_sib, 3 * q, sibling))
            if not via_x:
                start(copy(o_refs[t], mine, kx, rs, s_fwd, r_fwd, q, across_y))
            copy(o_refs[t], mine, ky, rs, s_nbr, r_nbr, 2 * q + 1, across_y).wait_recv()
            start(copy(o_refs[t], mine, ky, rs, s_sib, r_sib, 3 * q + 1, sibling))
            if via_x:
                start(copy(o_refs[t], mine, ky, rs, s_fwd, r_fwd, q, across_x))
        for t, i, rs, q, via_x in work:
            mine = pl.ds(c * half[t], half[t])
            copy(o_refs[t], mine, kd, rs, s_fwd, r_fwd, q, across_x if via_x else across_y).wait_recv()
            start(copy(o_refs[t], mine, kd, rs, s_sib, r_sib, 3 * q + 2, sibling))
        for t, i, rs, q, _ in work:
            theirs = pl.ds((1 - c) * half[t], half[t])
            for r, slot in enumerate((kx, ky, kd)):
                copy(o_refs[t], theirs, slot, rs, s_sib, r_sib, 3 * q + r, sibling).wait_recv()
        for cp in sends:
            cp.wait_send()

    sem = pltpu.SemaphoreType.DMA
    outs = pl.pallas_call(
        body, in_specs=[ANY] * nt, out_specs=[ANY] * nt,
        out_shape=[SDS(p.shape, p.dtype) for p in placed],
        input_output_aliases={t: t for t in range(nt)},
        scratch_shapes=[sem((2 * total,)), sem((2 * total,)), sem((total,)), sem((total,)), sem((3 * total,)),
                        sem((3 * total,))],
        name=name)(*placed)
    return list(outs)


def _pair_send(grads, name):
    nt = len(grads)
    half = [g.shape[0] // 2 for g in grads]
    pieces = [_row_pieces(g.shape[2], _nbytes((h,) + g.shape[1:], g.dtype)) for g, h in zip(grads, half)]
    base, total = _offsets([len(p) for p in pieces])

    def body(*refs):
        g_refs, o_refs = refs[:nt], refs[nt:2 * nt]
        ssem, rsem = refs[2 * nt:]
        x, y, c = _coords()
        cps = []
        for t in range(nt):
            theirs = pl.ds((1 - c) * half[t], half[t])
            for i, rs in enumerate(pieces[t]):
                q = base[t] + i
                cp = pltpu.make_async_remote_copy(
                    src_ref=g_refs[t].at[theirs, :, rs], dst_ref=o_refs[t].at[:, :, rs], send_sem=ssem.at[q],
                    recv_sem=rsem.at[q], device_id=(x, y, 1 - c), device_id_type=MESH)
                cp.start()
                cps.append(cp)
        for cp in cps:
            cp.wait_recv()
        for cp in cps:
            cp.wait_send()

    sem = pltpu.SemaphoreType.DMA
    outs = pl.pallas_call(
        body, in_specs=[ANY] * nt, out_specs=[ANY] * nt,
        out_shape=[SDS((g.shape[0] // 2,) + g.shape[1:], g.dtype) for g in grads],
        scratch_shapes=[sem((total,)), sem((total,))],
        name=name)(*grads)
    return list(outs)


def _chip_send(parts, name):
    nt = len(parts)
    pieces = [_row_pieces(p.shape[2], _nbytes((p.shape[0],) + p.shape[2:], p.dtype)) for p in parts]
    base, total = _offsets([len(p) for p in pieces])

    def body(*refs):
        p_refs, o_refs = refs[:nt], refs[nt:2 * nt]
        ssem, rsem = refs[2 * nt:]
        x, y, c = _coords()
        cps = []
        for t in range(nt):
            for i, rs in enumerate(pieces[t]):
                for r, (px, py) in enumerate(_other_chips(x, y)):
                    q = 3 * (base[t] + i) + r
                    cp = pltpu.make_async_remote_copy(
                        src_ref=p_refs[t].at[:, 2 * px + py, rs], dst_ref=o_refs[t].at[r, :, rs], send_sem=ssem.at[q],
                        recv_sem=rsem.at[q], device_id=(px, py, c), device_id_type=MESH)
                    cp.start()
                    cps.append(cp)
        for cp in cps:
            cp.wait_recv()
        for cp in cps:
            cp.wait_send()

    sem = pltpu.SemaphoreType.DMA
    outs = pl.pallas_call(
        body, in_specs=[ANY] * nt, out_specs=[ANY] * nt,
        out_shape=[SDS((N_CHIPS - 1, p.shape[0]) + p.shape[2:], p.dtype) for p in parts],
        scratch_shapes=[sem((3 * total,)), sem((3 * total,))],
        name=name)(*parts)
    return list(outs)


def _pair_join(bufs, name):
    nt = len(bufs)
    half = [b.shape[0] // 2 for b in bufs]
    pieces = [_row_pieces(b.shape[1], _nbytes((h,) + b.shape[1:], b.dtype)) for b, h in zip(bufs, half)]
    base, total = _offsets([len(p) for p in pieces])

    def body(*refs):
        o_refs = refs[nt:2 * nt]
        ssem, rsem = refs[2 * nt:]
        x, y, c = _coords()
        cps = []
        for t in range(nt):
            mine = pl.ds(c * half[t], half[t])
            for i, rs in enumerate(pieces[t]):
                q = base[t] + i
                cp = pltpu.make_async_remote_copy(
                    src_ref=o_refs[t].at[mine, rs], dst_ref=o_refs[t].at[mine, rs], send_sem=ssem.at[q],
                    recv_sem=rsem.at[q], device_id=(x, y, 1 - c), device_id_type=MESH)
                cp.start()
                cps.append(cp)
        for t in range(nt):
            theirs = pl.ds((1 - c) * half[t], half[t])
            for i, rs in enumerate(pieces[t]):
                q = base[t] + i
                pltpu.make_async_remote_copy(
                    src_ref=o_refs[t].at[theirs, rs], dst_ref=o_refs[t].at[theirs, rs], send_sem=ssem.at[q],
                    recv_sem=rsem.at[q], device_id=(x, y, 1 - c), device_id_type=MESH).wait_recv()
        for cp in cps:
            cp.wait_send()

    sem = pltpu.SemaphoreType.DMA
    outs = pl.pallas_call(
        body, in_specs=[ANY] * nt, out_specs=[ANY] * nt,
        out_shape=[SDS(b.shape, b.dtype) for b in bufs],
        input_output_aliases={t: t for t in range(nt)},
        scratch_shapes=[sem((total,)), sem((total,))],
        name=name)(*bufs)
    return list(outs)


def _gather_devices(vals, name):
    nt = len(vals)
    flips = [(a, b, e) for a in (0, 1) for b in (0, 1) for e in (0, 1)][1:]

    def body(*refs):
        v_refs, o_refs = refs[:nt], refs[nt:2 * nt]
        lsem, ssem, rsem = refs[2 * nt:]
        x, y, c = _coords()
        me = 4 * x + 2 * y + c
        peers = [((1 - x) if a else x, (1 - y) if b else y, (1 - c) if e else c) for a, b, e in flips]
        cps = []
        for t in range(nt):
            loc = pltpu.make_async_copy(v_refs[t], o_refs[t].at[me], lsem.at[t])
            loc.start()
            cps.append(loc)
            for r, peer in enumerate(peers):
                cp = pltpu.make_async_remote_copy(
                    src_ref=v_refs[t], dst_ref=o_refs[t].at[me], send_sem=ssem.at[7 * t + r],
                    recv_sem=rsem.at[7 * t + r], device_id=peer, device_id_type=MESH)
                cp.start()
                cps.append(cp)
        for t in range(nt):
            for r, (px, py, pc) in enumerate(peers):
                pltpu.make_async_remote_copy(
                    src_ref=v_refs[t], dst_ref=o_refs[t].at[4 * px + 2 * py + pc], send_sem=ssem.at[7 * t + r],
                    recv_sem=rsem.at[7 * t + r], device_id=(px, py, pc), device_id_type=MESH).wait_recv()
        for t in range(nt):
            cps[8 * t].wait()
            for r in range(7):
                cps[8 * t + 1 + r].wait_send()

    sem = pltpu.SemaphoreType.DMA
    outs = pl.pallas_call(
        body, in_specs=[ANY] * nt, out_specs=[ANY] * nt,
        out_shape=[SDS((N_DEV,) + v.shape, v.dtype) for v in vals],
        scratch_shapes=[sem((nt,)), sem((7 * nt,)), sem((7 * nt,))],
        name=name)(*vals)
    return list(outs)


def _gather_all(shards, dtypes, k_idx, tag):
    placed = [_place_chunk(s, k_idx, dt, f"{tag}_place{t}") for t, (s, dt) in enumerate(zip(shards, dtypes))]
    return _gather_chips(placed, f"{tag}_gather")


def _reduce_to_shards(grads, k_idx, c_idx, kc_idx, tag):
    recv = _pair_send(grads, f"{tag}_pair_send")
    pair = [_pair_sum(g, r, c_idx, f"{tag}_pair_sum{t}") for t, (g, r) in enumerate(zip(grads, recv))]
    recv = _chip_send(pair, f"{tag}_chip_send")
    bufs = [_chip_sum(p, r, kc_idx, f"{tag}_chip_sum{t}") for t, (p, r) in enumerate(zip(pair, recv))]
    return _pair_join(bufs, f"{tag}_pair_join")


WEIGHT_NAMES = ("c_ctx", "w_ada", "b_ada", "w_in", "w_pool", "pool_scale", "ssm_a_re", "ssm_a_im", "ssm_log_dt",
                "ssm_b_re", "ssm_b_im", "ssm_c_re", "ssm_c_im", "ssm_d", "w_glu", "w_out", "g_pre_mix", "g_post_mix",
                "g_pre_ffn", "g_post_ffn", "w_up", "w_conv", "w_down")


def _block_diag_in(bb, ng):
    nl, nd, npart, h, gp = bb.shape
    p = gp // ng
    w = jnp.einsum("ldqhgp,kg->lkhdqgp", bb.reshape(nl, nd, npart, h, ng, p), jnp.eye(ng, dtype=bb.dtype))
    return w.reshape(nl, ng * h, nd * npart * gp)


def _diag_in_grad(dw, ng, nh, p):
    gl = ng // S5_BAND
    out = jnp.einsum("ghqagp->qhagp", dw.reshape(gl, nh, 2, S5_BAND, gl, p))
    return out.reshape(2, nh, ng * p)


def _block_diag_out(cs, ng):
    nl, nd, npart, _, h, p = cs.shape
    w = jnp.einsum("ldqghp,kg->ldqkpgh", cs, jnp.eye(ng, dtype=cs.dtype))
    return w.reshape(nl, nd * npart * ng * p, ng * h)


def _diag_out_grad(dw, ng, nh, p):
    gl = ng // S5_BAND
    out = jnp.einsum("qagpgh->qaghp", dw.reshape(2, S5_BAND, gl, p, gl, nh))
    return out.reshape(2, ng, nh, p)


def kernel(x, c, ctx, c_ctx, w_ada, b_ada, w_in, w_pool, pool_scale, ssm_a_re, ssm_a_im, ssm_log_dt, ssm_b_re, ssm_b_im, ssm_c_re, ssm_c_im, ssm_d, w_glu, w_out, g_pre_mix, g_post_mix, g_pre_ffn, g_post_ffn, w_up, w_conv, w_down, loss_target, m_c_ctx, m_w_ada, m_b_ada, m_w_in, m_w_pool, m_pool_scale, m_ssm_a_re, m_ssm_a_im, m_ssm_log_dt, m_ssm_b_re, m_ssm_b_im, m_ssm_c_re, m_ssm_c_im, m_ssm_d, m_w_glu, m_w_out, m_g_pre_mix, m_g_post_mix, m_g_pre_ffn, m_g_post_ffn, m_w_up, m_w_conv, m_w_down, v_c_ctx, v_w_ada, v_b_ada, v_w_in, v_w_pool, v_pool_scale, v_ssm_a_re, v_ssm_a_im, v_ssm_log_dt, v_ssm_b_re, v_ssm_b_im, v_ssm_c_re, v_ssm_c_im, v_ssm_d, v_w_glu, v_w_out, v_g_pre_mix, v_g_post_mix, v_g_pre_ffn, v_g_post_ffn, v_w_up, v_w_conv, v_w_down):
    weights = dict(zip(WEIGHT_NAMES, (c_ctx, w_ada, b_ada, w_in, w_pool, pool_scale, ssm_a_re, ssm_a_im, ssm_log_dt,
                                      ssm_b_re, ssm_b_im, ssm_c_re, ssm_c_im, ssm_d, w_glu, w_out, g_pre_mix, g_post_mix,
                                      g_pre_ffn, g_post_ffn, w_up, w_conv, w_down)))
    mom1 = dict(zip(WEIGHT_NAMES, (m_c_ctx, m_w_ada, m_b_ada, m_w_in, m_w_pool, m_pool_scale, m_ssm_a_re, m_ssm_a_im,
                                   m_ssm_log_dt, m_ssm_b_re, m_ssm_b_im, m_ssm_c_re, m_ssm_c_im, m_ssm_d, m_w_glu, m_w_out,
                                   m_g_pre_mix, m_g_post_mix, m_g_pre_ffn, m_g_post_ffn, m_w_up, m_w_conv, m_w_down)))
    mom2 = dict(zip(WEIGHT_NAMES, (v_c_ctx, v_w_ada, v_b_ada, v_w_in, v_w_pool, v_pool_scale, v_ssm_a_re, v_ssm_a_im,
                                   v_ssm_log_dt, v_ssm_b_re, v_ssm_b_im, v_ssm_c_re, v_ssm_c_im, v_ssm_d, v_w_glu, v_w_out,
                                   v_g_pre_mix, v_g_post_mix, v_g_pre_ffn, v_g_post_ffn, v_w_up, v_w_conv, v_w_down)))

    xi, yi, ci = lax.axis_index("x"), lax.axis_index("y"), lax.axis_index("c")
    chip = 2 * xi + yi
    dev = 4 * xi + 2 * yi + ci
    nl = w_in.shape[0]
    n_lat, d = x.shape[1], x.shape[2]
    n_ctx = ctx.shape[1]
    n = n_ctx + n_lat
    _, ndir, ng, nstate, nh = ssm_b_re.shape
    gp = ng * nstate
    sw = ng * nh
    n_pool_groups, pool_group = w_pool.shape[1], w_pool.shape[3]
    pw = n_pool_groups * pool_group
    assert pw + sw == d and pw % sw == 0 and len(POOL_WINDOWS) == n_pool_groups and n_lat % GRID_W == 0
    dff2 = w_up.shape[2] * N_CHIPS
    ada_w = w_ada.shape[2] * N_CHIPS
    ada_cols = w_ada.shape[2]
    s_rows = gp // LANES

    c_pad = jnp.concatenate([c, jnp.zeros((SUBLANES - 1, d), F32)], axis=0)
    c_all = _gather_devices([c_pad], "gather_cond")[0][:, 0, :]
    cond = jnp.concatenate([c_all, c_ctx[None, :], jnp.zeros((16 - N_DEV - 1, d), F32)], axis=0)
    b_shard = lax.dynamic_slice_in_dim(b_ada, chip * ada_cols, ada_cols, axis=1)[:, None, :]
    mod_shard = _ada_fwd(cond, w_ada, b_shard, "ada_fwd")
    k_idx, c_idx, kc_idx = jnp.stack([chip]), jnp.stack([ci]), jnp.stack([chip, ci])
    mod_all = _gather_all([mod_shard], [F32], k_idx, "mods")[0]
    mod_all = jnp.transpose(mod_all, (0, 2, 1, 3)).reshape(nl, 16, ada_w)
    mod_lat = lax.dynamic_index_in_dim(mod_all, dev, axis=1, keepdims=False).reshape(nl, 6, d)
    mod_ctx = mod_all[:, N_DEV].reshape(nl, 6, d)
    mods = jnp.concatenate([jnp.stack([mod_ctx, mod_lat], axis=1), jnp.zeros((nl, 2, 2, d), F32)], axis=2)

    shards = [w_in, w_pool.reshape(nl, pw // N_CHIPS, pool_group), w_glu, w_out, w_up, w_down,
              w_conv.reshape(nl, 9, dff2 // N_CHIPS)]
    g_in, g_pool, g_glu, g_out, g_up, g_down, g_conv = _gather_all(shards, [COMM_DTYPE] * 6 + [F32], k_idx, "weights")
    wi = g_in.reshape(nl, d, d)
    wp = jnp.transpose(g_pool.reshape(nl, N_CHIPS, n_pool_groups, pool_group // N_CHIPS, pool_group),
                       (0, 2, 1, 3, 4)).reshape(nl, n_pool_groups, pool_group, pool_group)
    wg = g_glu.reshape(nl, sw, sw)
    wo = g_out.reshape(nl, d, d)
    wu = g_up
    wd = g_down.reshape(nl, dff2 // 2, d)
    wk = jnp.transpose(g_conv, (0, 2, 1, 3)).reshape(nl, 9, dff2)

    rows = nl * ndir
    a_re2 = ssm_a_re.reshape(rows, gp)
    a_im2 = ssm_a_im.reshape(rows, gp)
    logdt2 = jnp.repeat(ssm_log_dt.reshape(rows, ng), nstate, axis=1)
    b_re2 = jnp.transpose(ssm_b_re.reshape(rows, gp, nh), (0, 2, 1))
    b_im2 = jnp.transpose(ssm_b_im.reshape(rows, gp, nh), (0, 2, 1))
    lam_re, lam_im, bb_re, bb_im = _disc_fwd(a_re2, a_im2, logdt2, b_re2, b_im2, "s5_discretise")
    lam = jnp.stack([lam_re.reshape(nl, ndir, s_rows, LANES), lam_im.reshape(nl, ndir, s_rows, LANES)], axis=2)
    lam = lam.reshape(nl, 2 * ndir, s_rows, LANES)
    bbs = jnp.stack([bb_re.reshape(nl, ndir, nh, gp), bb_im.reshape(nl, ndir, nh, gp)], axis=2)
    w_b = _block_diag_in(bbs.astype(MXU_DTYPE), ng)
    w_b = [w_b[:, :, dr * 2 * gp:(dr + 1) * 2 * gp] for dr in range(ndir)]
    cs = jnp.stack([ssm_c_re, -ssm_c_im], axis=2)
    w_c = _block_diag_out(cs.astype(MXU_DTYPE), ng)
    w_c = [w_c[:, dr * 2 * gp:(dr + 1) * 2 * gp] for dr in range(ndir)]

    def row(v, l):
        return v[l:l + 1]

    xc = jnp.concatenate([ctx[0], x[0]], axis=0)
    saved = []
    for l in range(nl):
        t = f"l{l}"
        md = mods[l]
        h = _norm_mod_fwd(xc, row(g_pre_mix, l), md, 0, 1, n_ctx, f"{t}_pre_mix")
        u = _mm(h, wi, "nn", F32, f"{t}_in_proj", b_idx=l)
        p = _pool(u, pw, pool_group, n_ctx, False, MXU_DTYPE, f"{t}_pool")
        ypool = _pool_proj_fwd(p, wp, l, row(pool_scale, l), f"{t}_pool_proj")
        bu0 = _mm_s5(u, w_b[0], "in", f"{t}_s5_in0", b_idx=l, a_cols=(pw, sw))
        bu1 = _mm_s5(u, w_b[1], "in", f"{t}_s5_in1", b_idx=l, a_cols=(pw, sw))
        h0, h1 = _scan_fwd(bu0, bu1, lam[l], n_ctx, f"{t}_scan")
        y = _mm_s5(h0, w_c[0], "out", f"{t}_s5_out0", b_idx=l)
        y = _mm_s5(h1, w_c[1], "out", f"{t}_s5_out1", b_idx=l, add=y)
        s_out = _ssm_head_fwd(y, u, row(ssm_d, l), wg, l, f"{t}_s5_head")
        cat = jnp.concatenate([ypool, s_out], axis=1)
        mix = _mm(cat, wo, "nn", F32, f"{t}_out_proj", b_idx=l)
        x_mid = _gate_res_fwd(xc, mix, row(g_post_mix, l), md, 2, n_ctx, f"{t}_post_mix")
        h2 = _norm_mod_fwd(x_mid, row(g_pre_ffn, l), md, 3, 4, n_ctx, f"{t}_pre_ffn")
        z = _mm(h2, wu, "nn", F32, f"{t}_up", b_idx=l, b_chunks=N_CHIPS)
        act, cv, cg = _conv_glu_fwd(z, wk[l], n_ctx, f"{t}_conv_glu")
        f = _mm(act, wd, "nn", F32, f"{t}_down", b_idx=l)
        x_out = _gate_res_fwd(x_mid, f, row(g_post_ffn, l), md, 5, n_ctx, f"{t}_post_ffn")
        saved.append(dict(xc=xc, h=h, u=u, p=p, h0=h0, h1=h1, y=y, cat=cat, mix=mix, x_mid=x_mid, h2=h2, z=z, cv=cv, cg=cg, f=f))
        xc = x_out

    dx, loss_tile = _loss_grad(xc, loss_target[0], n_ctx, "loss")
    loss = lax.psum(loss_tile[0, 0], ("x", "y", "c"))

    big = {k: [None] * nl for k in ("w_in", "w_pool", "w_glu", "w_out", "w_up", "w_down")}
    small = {k: [None] * nl for k in ("pool_scale", "ssm_d", "g_pre_mix", "g_post_mix", "g_pre_ffn", "g_post_ffn",
                                      "lam", "bb", "cs", "w_conv")}
    dmods = [None] * nl
    for l in reversed(range(nl)):
        t = f"l{l}b"
        md = mods[l]
        sv = saved[l]
        df, dgate_ffn, small["g_post_ffn"][l] = _gate_res_bwd(dx, sv["f"], row(g_post_ffn, l), md, 5, n_ctx, f"{t}_post_ffn")
        dact = _mm(df, wd, "nt", F32, f"{t}_down_dx", b_idx=l)
        dz, dkv, dkg, act = _conv_glu_bwd(sv["z"], sv["cv"], sv["cg"], dact, wk[l], n_ctx, f"{t}_conv_glu")
        small["w_conv"][l] = jnp.concatenate([dkv, dkg], axis=1)
        big["w_down"][l] = _mm(act, df, "tn", COMM_DTYPE, f"{t}_down_dw")
        big["w_up"][l] = _mm(sv["h2"], dz, "tn", COMM_DTYPE, f"{t}_up_dw", b_chunks=2, out_chunks=N_CHIPS)
        dh2 = _mm(dz, wu, "nt", F32, f"{t}_up_dx", b_idx=l, a_chunks=2, b_chunks=N_CHIPS)
        dx, dss_ffn, small["g_pre_ffn"][l] = _norm_mod_bwd(dh2, sv["x_mid"], row(g_pre_ffn, l), md, 3, 4, dx, n_ctx,
                                                           f"{t}_pre_ffn")
        dmix, dgate_mix, small["g_post_mix"][l] = _gate_res_bwd(dx, sv["mix"], row(g_post_mix, l), md, 2, n_ctx,
                                                                f"{t}_post_mix")
        dcat = _mm(dmix, wo, "nt", F32, f"{t}_out_dx", b_idx=l)
        big["w_out"][l] = _mm(sv["cat"], dmix, "tn", COMM_DTYPE, f"{t}_out_dw")
        dp, small["pool_scale"][l], big["w_pool"][l] = _pool_proj_bwd(sv["p"], dcat, wp, l, row(pool_scale, l),
                                                                      f"{t}_pool_proj")
        du_pool = _pool(dp, pw, pool_group, n_ctx, True, F32, f"{t}_pool")
        dy, du_dir, gact, dq, small["ssm_d"][l] = _ssm_head_bwd(dcat, sv["y"], sv["u"], row(ssm_d, l), wg, l, f"{t}_s5_head")
        big["w_glu"][l] = _mm(gact, dq, "tn", COMM_DTYPE, f"{t}_glu_dw")
        dh0 = _mm_s5(dy, w_c[0], "out_dx", f"{t}_s5_out_dx0", b_idx=l)
        dh1 = _mm_s5(dy, w_c[1], "out_dx", f"{t}_s5_out_dx1", b_idx=l)
        small["cs"][l] = jnp.stack([_diag_out_grad(_mm_s5(sv[hk], dy, "out_dw", f"{t}_s5_out_dw{dr}"), ng, nh, nstate)
                                    for dr, hk in enumerate(("h0", "h1"))], axis=0)
        a0, a1, small["lam"][l] = _scan_bwd(dh0, dh1, sv["h0"], sv["h1"], lam[l], n_ctx, f"{t}_scan")
        du_proj = _mm_s5(a0, w_b[0], "in_dx", f"{t}_s5_in_dx0", b_idx=l)
        du_proj = _mm_s5(a1, w_b[1], "in_dx", f"{t}_s5_in_dx1", b_idx=l, add=du_proj)
        small["bb"][l] = jnp.stack([_diag_in_grad(_mm_s5(sv["u"], adj, "in_dw", f"{t}_s5_in_dw{dr}", a_cols=(pw, sw)),
                                                  ng, nh, nstate) for dr, adj in enumerate((a0, a1))], axis=0)
        du = _assemble_du(du_pool, du_dir, du_proj, f"{t}_du")
        dh = _mm(du, wi, "nt", F32, f"{t}_in_dx", b_idx=l)
        big["w_in"][l] = _mm(sv["h"], du, "tn", COMM_DTYPE, f"{t}_in_dw")
        dx, dss_mix, small["g_pre_mix"][l] = _norm_mod_bwd(dh, sv["xc"], row(g_pre_mix, l), md, 0, 1, dx, n_ctx,
                                                           f"{t}_pre_mix")
        dmods[l] = jnp.concatenate([dss_mix, dgate_mix, dss_ffn, dgate_ffn], axis=1).reshape(2, ada_w)

    grad_x = dx[n_ctx:][None]

    dmod_all = _gather_devices([jnp.stack(dmods, axis=0)], "gather_dmods")[0]
    ada_rows, db_ada = _ada_rows(dmod_all, "ada_rows")
    rows_shard = lax.dynamic_slice_in_dim(ada_rows, chip * ada_cols, ada_cols, axis=2)
    dcond_part = _ada_dcond(rows_shard, w_ada, "ada_dcond")
    dcond_parts = _gather_devices([dcond_part], "gather_dcond")[0][0::2]
    grads = {"w_ada": _ada_dw(cond, rows_shard, "ada_dw"), "b_ada": db_ada[:, 0, :],
             "c_ctx": _ada_dctx(dcond_parts, c_ctx[None, :], N_DEV, "ada_dctx")[0]}

    stacked = {k: jnp.stack(v, axis=0) for k, v in big.items()}
    parts = [stacked["w_in"].reshape(nl, N_CHIPS, d // N_CHIPS, d),
             jnp.transpose(stacked["w_pool"].astype(COMM_DTYPE).reshape(nl, n_pool_groups, N_CHIPS, pool_group // N_CHIPS,
                                                                      pool_group), (0, 2, 1, 3, 4))
             .reshape(nl, N_CHIPS, pw // N_CHIPS, pool_group),
             stacked["w_glu"].reshape(nl, N_CHIPS, sw // N_CHIPS, sw),
             stacked["w_out"].reshape(nl, N_CHIPS, d // N_CHIPS, d),
             stacked["w_up"],
             stacked["w_down"].reshape(nl, N_CHIPS, dff2 // 2 // N_CHIPS, d)]
    r_in, r_pool, r_glu, r_out, r_up, r_down = _reduce_to_shards(parts, k_idx, c_idx, kc_idx, "big")
    grads.update(w_in=r_in, w_pool=r_pool.reshape(w_pool.shape), w_glu=r_glu, w_out=r_out, w_up=r_up, w_down=r_down)

    order = ("pool_scale", "ssm_d", "g_pre_mix", "g_post_mix", "g_pre_ffn", "g_post_ffn", "lam", "bb", "cs", "w_conv")
    pieces = [jnp.stack(small[k], axis=0) for k in order]
    flat = jnp.concatenate([q.reshape(-1) for q in pieces])
    unit = nl * N_CHIPS * SUBLANES * 1024
    padded = -(-flat.shape[0] // unit) * unit
    flat = jnp.concatenate([flat, jnp.zeros((padded - flat.shape[0],), F32)])
    vec = flat.reshape(nl, N_CHIPS, padded // (nl * N_CHIPS * 1024), 1024)
    vec = _gather_all(_reduce_to_shards([vec], k_idx, c_idx, kc_idx, "small"), [F32], k_idx, "small_all")[0].reshape(-1)
    red, pos = {}, 0
    for k, q in zip(order, pieces):
        red[k] = vec[pos:pos + q.size].reshape(q.shape)
        pos += q.size
    for k in ("pool_scale", "ssm_d", "g_pre_mix", "g_post_mix", "g_pre_ffn", "g_post_ffn"):
        grads[k] = red[k][:, 0, :]
    dlam = red["lam"].reshape(nl, ndir, 2, gp)
    dbb = red["bb"].reshape(nl, ndir, 2, nh, gp)
    d_are, d_aim, d_ldt, d_bre, d_bim = _disc_bwd(
        a_re2, a_im2, logdt2, b_re2, b_im2, dlam[:, :, 0].reshape(rows, gp), dlam[:, :, 1].reshape(rows, gp),
        dbb[:, :, 0].reshape(rows, nh, gp), dbb[:, :, 1].reshape(rows, nh, gp), nstate, "s5_discretise_bwd")
    grads["ssm_a_re"] = d_are.reshape(ssm_a_re.shape)
    grads["ssm_a_im"] = d_aim.reshape(ssm_a_im.shape)
    grads["ssm_log_dt"] = d_ldt[:, :ng].reshape(ssm_log_dt.shape)
    grads["ssm_b_re"] = jnp.transpose(d_bre, (0, 2, 1)).reshape(ssm_b_re.shape)
    grads["ssm_b_im"] = jnp.transpose(d_bim, (0, 2, 1)).reshape(ssm_b_im.shape)
    grads["ssm_c_re"] = red["cs"][:, :, 0]
    grads["ssm_c_im"] = -red["cs"][:, :, 1]
    conv_cols = dff2 // N_CHIPS
    grads["w_conv"] = lax.dynamic_slice_in_dim(red["w_conv"], chip * conv_cols, conv_cols, axis=2).reshape(w_conv.shape)

    delta, new_m, new_v = {}, {}, {}
    for k in WEIGHT_NAMES:
        delta[k], new_m[k], new_v[k] = _adamw(weights[k], grads[k], mom1[k], mom2[k], f"adamw_{k}")
    return (loss, grad_x, *[grads[k] for k in WEIGHT_NAMES], *[delta[k] for k in WEIGHT_NAMES],
            *[new_m[k] for k in WEIGHT_NAMES], *[new_v[k] for k in WEIGHT_NAMES])
```

```python
import math

import jax
import jax.numpy as jnp
from jax import lax
from jax.experimental import pallas as pl
from jax.experimental.pallas import tpu as pltpu

F32 = jnp.float32
MXU_DTYPE = jnp.bfloat16
COMM_DTYPE = jnp.bfloat16
HIGHEST = lax.Precision.HIGHEST
VMEM_LIMIT_BYTES = 48 * 1024 * 1024
LANES = 128
SUBLANES = 8
N_CHIPS = 4
N_DEV = 8

EPS = 1e-6
GRID_W = 64
POOL_WINDOWS = (2, 4, 8, 16)
ADAM_LR = 0.001
ADAM_B1 = 0.9
ADAM_B2 = 0.999
ADAM_EPS = 1e-08
ADAM_WD = 0.01
ADAM_STEP = 10
GELU_C0 = math.sqrt(2.0 / math.pi)
GELU_C1 = 0.044715

SDS = jax.ShapeDtypeStruct
ANY = pl.BlockSpec(memory_space=pl.ANY)
MESH = pl.DeviceIdType.MESH


def _cparams(*sem):
    return pltpu.CompilerParams(dimension_semantics=sem if sem else None, vmem_limit_bytes=VMEM_LIMIT_BYTES)


def _pick(n, cands):
    for cand in cands:
        if n % cand == 0:
            return cand
    return n


def _row_tile(n_ctx, n):
    return math.gcd(math.gcd(n_ctx, n - n_ctx), 256)


_DIMS = {"nn": (((1,), (0,)), ((), ())), "nt": (((1,), (1,)), ((), ())), "tn": (((0,), (0,)), ((), ()))}
_TM = (1088, 1024, 512, 384, 256, 128, 64, 32, 16, 8)
_TN = (1024, 1408, 512, 384, 256, 128)
_TK = (2048, 2176, 1024, 1088, 1408, 512, 384, 256, 128, 64, 32, 16, 8)
MM_BLOCK_BYTES = 40 * 1024 * 1024


def _pick_k(k, k_unit, tm, tn, a, b, out_dtype):
    sizes = [jnp.dtype(d).itemsize for d in (a.dtype, b.dtype, out_dtype)]
    for tk in (k_unit,) + _TK:
        if k_unit % tk == 0:
            blocks = 2 * (tm * tk * sizes[0] + tk * tn * sizes[1] + tm * tn * sizes[2]) + (tm * tn * 4 if k // tk > 1 else 0)
            if blocks <= MM_BLOCK_BYTES:
                return tk
    return _pick(k_unit, _TK)


def _chunk_of(idx, per, chunks):
    out = 0
    for q in range(1, chunks):
        out = out + (idx >= q * per).astype(jnp.int32)
    return out


def _within(idx, per, chunks):
    return idx - per * _chunk_of(idx, per, chunks)


def _mm_call(args, mode, out_dtype, name, grid, tiles, specs, o_spec, o_shape, a3d=False, b3d=False, out3d=False,
             add=False):
    tm, tn, _ = tiles
    nk = grid[2]

    def operand(ref, is3d):
        v = ref[...]
        if is3d:
            v = pltpu.einshape("tjl->t(jl)", v)
        return v.astype(MXU_DTYPE)

    def body(*refs):
        a_ref, b_ref, o_ref = refs[0], refs[1], refs[3 if add else 2]

        def product():
            return lax.dot_general(operand(a_ref, a3d), operand(b_ref, b3d), _DIMS[mode], preferred_element_type=F32)

        def finish(total):
            if add:
                total = total + refs[2][...]
            if out3d:
                total = pltpu.einshape("t(jl)->tjl", total, l=LANES)
            o_ref[...] = total.astype(o_ref.dtype)

        if nk == 1:
            finish(product())
            return
        acc_ref = refs[-1]
        kk = pl.program_id(2)

        @pl.when(kk == 0)
        def _():
            acc_ref[...] = jnp.zeros(acc_ref.shape, F32)

        acc_ref[...] += product()

        @pl.when(kk == nk - 1)
        def _():
            finish(acc_ref[...])

    return pl.pallas_call(
        body, grid=grid, in_specs=specs, out_specs=o_spec, out_shape=SDS(o_shape, out_dtype),
        scratch_shapes=[] if nk == 1 else [pltpu.VMEM((tm, tn), F32)],
        compiler_params=_cparams("parallel", "parallel", "arbitrary"), name=name)(*args)


def _lanes3(blk, imap):
    return (blk[0], blk[1] // LANES, LANES), (lambda i, j, kk: imap(i, j, kk) + (0,))


def _mm(a, b, mode, out_dtype, name, a_idx=None, b_idx=None, a_cols=None, a_chunks=None, b_chunks=None, out_chunks=None,
        a3d=False, b3d=False, out3d=False, add=None):
    a2, b2 = a.shape[-2:], b.shape[-2:]
    if a3d:
        a2 = (a.shape[0], a.shape[1] * a.shape[2])
    if b3d:
        assert mode == "tn"
        b2 = (b.shape[0], b.shape[1] * b.shape[2])
    if a_chunks is not None:
        assert mode == "nt" and a.shape[-3] == a_chunks
        a2 = (a2[0], a2[1] * a_chunks)
    if b_chunks is not None:
        assert b.shape[-3] == b_chunks
        b2 = (b2[0], b2[1] * b_chunks)
    alast = a2[1] if a_cols is None else a_cols[1]
    if mode == "nn":
        m, k, n = a2[0], alast, b2[1]
        assert b2[0] == k
    elif mode == "nt":
        m, k, n = a2[0], alast, b2[0]
        assert b2[1] == k
    else:
        k, m, n = a2[0], alast, b2[1]
        assert b2[0] == k
    n_unit = n // (out_chunks or 1) // (b_chunks if b_chunks and mode != "nt" else 1)
    k_unit = k // (b_chunks if b_chunks and mode == "nt" else 1) // (a_chunks or 1)
    tm, tn = _pick(m, _TM), _pick(n_unit, _TN)
    tk = _pick_k(k, k_unit, tm, tn, a, b, out_dtype)
    nk = k // tk
    a_lane_tile = tm if mode == "tn" else tk
    off = 0
    if a_cols is not None:
        assert a_cols[0] % a_lane_tile == 0
        off = a_cols[0] // a_lane_tile

    if mode == "tn":
        a_blk, a_map = (tk, tm), (lambda i, j, kk: (kk, i + off))
    else:
        a_blk, a_map = (tm, tk), (lambda i, j, kk: (i, kk + off))
        if a_chunks is not None:
            aper = k // a_chunks // tk
            a_blk, a_map = (None, tm, tk), (lambda i, j, kk: (_chunk_of(kk, aper, a_chunks), i, _within(kk, aper, a_chunks)))
    if mode == "nt":
        b_blk, b_map = (tn, tk), (lambda i, j, kk: (j, kk))
        if b_chunks is not None:
            per = k // b_chunks // tk
            b_blk, b_map = (None, tn, tk), (lambda i, j, kk: (_chunk_of(kk, per, b_chunks), j, _within(kk, per, b_chunks)))
    else:
        b_blk, b_map = (tk, tn), (lambda i, j, kk: (kk, j))
        if b_chunks is not None:
            per = n // b_chunks // tn
            b_blk, b_map = (None, tk, tn), (lambda i, j, kk: (_chunk_of(j, per, b_chunks), kk, _within(j, per, b_chunks)))
    if a3d:
        a_blk, a_map = _lanes3(a_blk, a_map)
    if b3d:
        b_blk, b_map = _lanes3(b_blk, b_map)
    if a_idx is not None:
        a_blk, a_map0 = (None,) + a_blk, a_map
        a_map = lambda i, j, kk: (a_idx,) + a_map0(i, j, kk)
    if b_idx is not None:
        b_blk, b_map0 = (None,) + b_blk, b_map
        b_map = lambda i, j, kk: (b_idx,) + b_map0(i, j, kk)
    o_blk, o_map, o_shape = (tm, tn), (lambda i, j, kk: (i, j)), (m, n)
    if out_chunks is not None:
        oper = n // out_chunks // tn
        o_map = lambda i, j, kk: (_chunk_of(j, oper, out_chunks), i, _within(j, oper, out_chunks))
        o_blk, o_shape = (None, tm, tn), (out_chunks, m, n // out_chunks)
    if out3d:
        o_blk, o_map = _lanes3(o_blk, o_map)
        o_shape = (m, n // LANES, LANES)
    specs, args = [pl.BlockSpec(a_blk, a_map), pl.BlockSpec(b_blk, b_map)], [a, b]
    if add is not None:
        specs.append(pl.BlockSpec((tm, tn), lambda i, j, kk: (i, j)))
        args.append(add)
    return _mm_call(args, mode, out_dtype, name, (m // tm, n // tn, nk), (tm, tn, tk), specs, pl.BlockSpec(o_blk, o_map),
                    o_shape, a3d, b3d, out3d, add is not None)


S5_BAND = 2


def _mm_s5(a, b, kind, name, b_idx=None, a_cols=None, add=None):
    nb = S5_BAND
    wide3 = a if kind in ("in_dx", "out", "out_dw") else (b if kind == "in_dw" else None)
    if kind in ("in", "out_dx"):
        rows, wide = a.shape[0], b.shape[-1] if kind == "in" else b.shape[-2]
    else:
        rows, wide = wide3.shape[0], wide3.shape[1] * LANES
    sw = a_cols[1] if a_cols is not None else (b.shape[-1] if kind in ("out", "out_dw") else
                                                 (a.shape[1] if kind == "out_dx" else b.shape[-2]))
    tw, ts = wide // (2 * nb), sw // nb
    nwt = wide // tw
    off = 0 if a_cols is None else a_cols[0] // ts
    half = lambda t: _within(t, nb, nwt // nb)
    lead = (lambda blk, imap: (blk, imap)) if b_idx is None else (
        lambda blk, imap: ((None,) + blk, lambda i, j, kk: (b_idx,) + imap(i, j, kk)))
    rt = _pick(rows, _TM)
    if kind in ("in", "out_dx"):
        mode = "nn" if kind == "in" else "nt"
        a_spec = pl.BlockSpec((rt, ts), lambda i, j, kk: (i, off + half(j)))
        b_blk, b_map = ((ts, tw), lambda i, j, kk: (half(j), j)) if kind == "in" else ((tw, ts), lambda i, j, kk: (j, half(j)))
        o_blk, o_map = _lanes3((rt, tw), lambda i, j, kk: (i, j))
        return _mm_call([a, b], mode, F32, name, (rows // rt, nwt, 1), (rt, tw, ts), [a_spec, pl.BlockSpec(*lead(b_blk, b_map))],
                        pl.BlockSpec(o_blk, o_map), (rows, wide // LANES, LANES), out3d=True)
    if kind in ("out", "in_dx"):
        mode = "nn" if kind == "out" else "nt"
        a_blk, a_map = _lanes3((rt, tw), lambda i, j, kk: (i, kk * nb + j))
        b_blk, b_map = ((tw, ts), lambda i, j, kk: (kk * nb + j, j)) if kind == "out" else (
            (ts, tw), lambda i, j, kk: (j, kk * nb + j))
        specs, args = [pl.BlockSpec(a_blk, a_map), pl.BlockSpec(*lead(b_blk, b_map))], [a, b]
        if add is not None:
            specs.append(pl.BlockSpec((rt, ts), lambda i, j, kk: (i, j)))
            args.append(add)
        return _mm_call(args, mode, F32, name, (rows // rt, nb, nwt // nb), (rt, ts, tw), specs,
                        pl.BlockSpec((rt, ts), lambda i, j, kk: (i, j)), (rows, sw), a3d=True, add=add is not None)
    kt = _pick(rows, _TK)
    if kind == "out_dw":
        a_blk, a_map = _lanes3((kt, tw), lambda i, j, kk: (kk, i))
        return _mm_call([a, b], "tn", F32, name, (nwt, 1, rows // kt), (tw, ts, kt),
                        [pl.BlockSpec(a_blk, a_map), pl.BlockSpec((kt, ts), lambda i, j, kk: (kk, half(i)))],
                        pl.BlockSpec((tw, ts), lambda i, j, kk: (i, 0)), (wide, ts), a3d=True)
    assert kind == "in_dw"
    b_blk, b_map = _lanes3((kt, tw), lambda i, j, kk: (kk, j))
    return _mm_call([a, b], "tn", F32, name, (1, nwt, rows // kt), (ts, tw, kt),
                    [pl.BlockSpec((kt, ts), lambda i, j, kk: (kk, off + half(j))), pl.BlockSpec(b_blk, b_map)],
                    pl.BlockSpec((ts, tw), lambda i, j, kk: (0, j)), (ts, wide), b3d=True)


def _seg_map(nbc):
    return lambda i: (jnp.where(i < nbc, 0, 1), 0, 0)


def _rstd(v):
    return lax.rsqrt(jnp.mean(v * v, axis=-1, keepdims=True) + EPS)


def _norm_mod_fwd(x, g, mods, sh, sc, n_ctx, name):
    n, d = x.shape
    tm = _row_tile(n_ctx, n)
    nbc = n_ctx // tm

    def body(x_ref, g_ref, m_ref, h_ref):
        xv = x_ref[...]
        hn = xv * _rstd(xv) * g_ref[...]
        h_ref[...] = (hn * (1.0 + m_ref[0, sc:sc + 1, :]) + m_ref[0, sh:sh + 1, :]).astype(h_ref.dtype)

    row = pl.BlockSpec((tm, d), lambda i: (i, 0))
    return pl.pallas_call(
        body, grid=(n // tm,),
        in_specs=[row, pl.BlockSpec((1, d), lambda i: (0, 0)), pl.BlockSpec((1, 8, d), _seg_map(nbc))],
        out_specs=row, out_shape=SDS((n, d), MXU_DTYPE), compiler_params=_cparams("parallel"), name=name)(x, g, mods)


def _gate_res_fwd(x, f, g, mods, gi, n_ctx, name):
    n, d = x.shape
    tm = _row_tile(n_ctx, n)
    nbc = n_ctx // tm

    def body(x_ref, f_ref, g_ref, m_ref, o_ref):
        fv = f_ref[...]
        o_ref[...] = x_ref[...] + m_ref[0, gi:gi + 1, :] * (fv * _rstd(fv) * g_ref[...])

    row = pl.BlockSpec((tm, d), lambda i: (i, 0))
    return pl.pallas_call(
        body, grid=(n // tm,),
        in_specs=[row, row, pl.BlockSpec((1, d), lambda i: (0, 0)), pl.BlockSpec((1, 8, d), _seg_map(nbc))],
        out_specs=row, out_shape=SDS((n, d), F32), compiler_params=_cparams("parallel"), name=name)(x, f, g, mods)


def _gate_res_bwd(dx, f, g, mods, gi, n_ctx, name):
    n, d = dx.shape
    tm = _row_tile(n_ctx, n)
    nbc = n_ctx // tm

    def body(dx_ref, f_ref, g_ref, m_ref, df_ref, dgate_ref, dg_ref):
        i = pl.program_id(0)

        @pl.when(i == 0)
        def _():
            dg_ref[...] = jnp.zeros(dg_ref.shape, F32)

        @pl.when(jnp.logical_or(i == 0, i == nbc))
        def _():
            dgate_ref[...] = jnp.zeros(dgate_ref.shape, F32)

        dxv, fv, gv = dx_ref[...], f_ref[...], g_ref[...]
        rs = _rstd(fv)
        nv = fv * rs
        dgate_ref[0] += jnp.sum(dxv * (nv * gv), axis=0, keepdims=True)
        dout = dxv * m_ref[0, gi:gi + 1, :]
        dg_ref[...] += jnp.sum(dout * nv, axis=0, keepdims=True)
        dn = dout * gv
        df_ref[...] = (rs * (dn - nv * jnp.mean(dn * nv, axis=-1, keepdims=True))).astype(df_ref.dtype)

    row = pl.BlockSpec((tm, d), lambda i: (i, 0))
    vec = pl.BlockSpec((1, d), lambda i: (0, 0))
    return pl.pallas_call(
        body, grid=(n // tm,),
        in_specs=[row, row, vec, pl.BlockSpec((1, 8, d), _seg_map(nbc))],
        out_specs=[row, pl.BlockSpec((1, 1, d), _seg_map(nbc)), vec],
        out_shape=[SDS((n, d), MXU_DTYPE), SDS((2, 1, d), F32), SDS((1, d), F32)],
        compiler_params=_cparams("arbitrary"), name=name)(dx, f, g, mods)


def _norm_mod_bwd(dh, x, g, mods, sh, sc, dx_res, n_ctx, name):
    n, d = x.shape
    tm = _row_tile(n_ctx, n)
    nbc = n_ctx // tm

    def body(dh_ref, x_ref, g_ref, m_ref, r_ref, dx_ref, dss_ref, dg_ref):
        i = pl.program_id(0)

        @pl.when(i == 0)
        def _():
            dg_ref[...] = jnp.zeros(dg_ref.shape, F32)

        @pl.when(jnp.logical_or(i == 0, i == nbc))
        def _():
            dss_ref[...] = jnp.zeros(dss_ref.shape, F32)

        dhv, xv, gv = dh_ref[...], x_ref[...], g_ref[...]
        rs = _rstd(xv)
        nv = xv * rs
        dss_ref[0, 0:1, :] += jnp.sum(dhv, axis=0, keepdims=True)
        dss_ref[0, 1:2, :] += jnp.sum(dhv * (nv * gv), axis=0, keepdims=True)
        dhn = dhv * (1.0 + m_ref[0, sc:sc + 1, :])
        dg_ref[...] += jnp.sum(dhn * nv, axis=0, keepdims=True)
        dn = dhn * gv
        dx_ref[...] = r_ref[...] + rs * (dn - nv * jnp.mean(dn * nv, axis=-1, keepdims=True))

    row = pl.BlockSpec((tm, d), lambda i: (i, 0))
    vec = pl.BlockSpec((1, d), lambda i: (0, 0))
    return pl.pallas_call(
        body, grid=(n // tm,),
        in_specs=[row, row, vec, pl.BlockSpec((1, 8, d), _seg_map(nbc)), row],
        out_specs=[row, pl.BlockSpec((1, 2, d), _seg_map(nbc)), vec],
        out_shape=[SDS((n, d), F32), SDS((2, 2, d), F32), SDS((1, d), F32)],
        compiler_params=_cparams("arbitrary"), name=name)(dh, x, g, mods, dx_res)


def _loss_grad(xc, target, n_ctx, name):
    n, d = xc.shape
    tm = _row_tile(n_ctx, n)
    nbc = n_ctx // tm
    nb = n // tm

    def body(x_ref, t_ref, dx_ref, l_ref, acc_ref):
        i = pl.program_id(0)

        @pl.when(i == 0)
        def _():
            acc_ref[...] = jnp.zeros(acc_ref.shape, F32)

        @pl.when(i < nbc)
        def _():
            dx_ref[...] = jnp.zeros(dx_ref.shape, F32)

        @pl.when(i >= nbc)
        def _():
            diff = x_ref[...] - t_ref[...]
            dx_ref[...] = diff * (1.0 / d)
            acc_ref[...] += jnp.sum(diff * diff, axis=0, keepdims=True)

        @pl.when(i == nb - 1)
        def _():
            l_ref[...] = jnp.full(l_ref.shape, (0.5 / d) * jnp.sum(acc_ref[...]), F32)

    row = pl.BlockSpec((tm, d), lambda i: (i, 0))
    return pl.pallas_call(
        body, grid=(nb,),
        in_specs=[row, pl.BlockSpec((tm, d), lambda i: (jnp.maximum(i - nbc, 0), 0))],
        out_specs=[row, pl.BlockSpec((SUBLANES, LANES), lambda i: (0, 0))],
        out_shape=[SDS((n, d), F32), SDS((SUBLANES, LANES), F32)],
        scratch_shapes=[pltpu.VMEM((1, d), F32)],
        compiler_params=_cparams("arbitrary"), name=name)(xc, target)


POOL_PAD = 16


def _pool(src, pool_width, pool_group, n_ctx, bwd, out_dtype, name):
    n = src.shape[0]
    n_lat = n - n_ctx
    gb = pool_group // LANES
    segs = ((0, n_ctx, POOL_PAD), (n_ctx, n_lat, 2 * POOL_PAD + n_ctx))
    total = 3 * POOL_PAD + n

    def body(s_ref, o_ref, scr):
        j = pl.program_id(0)
        for base in (0, POOL_PAD + n_ctx, 2 * POOL_PAD + n):
            scr[pl.ds(base, POOL_PAD), :] = jnp.zeros((POOL_PAD, LANES), F32)
        for gi, w in enumerate(POOL_WINDOWS):
            @pl.when(jnp.logical_and(j >= gi * gb, j < (gi + 1) * gb))
            def _(w=w):
                half = w // 2
                offs = range(-half + 1, half + 1) if bwd else range(-half, half)
                for row0, nseg, base in segs:
                    ch = math.gcd(nseg, 256)

                    def count(c0):
                        t = c0 + lax.broadcasted_iota(jnp.int32, (ch, LANES), 0)
                        return (jnp.minimum(t + half, nseg) - jnp.maximum(t - half, 0)).astype(F32)

                    def fill(ci, carry):
                        c0 = pl.multiple_of(ci * ch, ch)
                        v = s_ref[pl.ds(row0 + c0, ch), :]
                        scr[pl.ds(base + c0, ch), :] = v / count(c0) if bwd else v
                        return carry

                    def window(ci, carry):
                        c0 = pl.multiple_of(ci * ch, ch)
                        acc = jnp.zeros((ch, LANES), F32)
                        for off in offs:
                            acc = acc + scr[pl.ds(c0 + (base + off), ch), :]
                        v = s_ref[pl.ds(row0 + c0, ch), :]
                        res = acc - v if bwd else acc / count(c0) - v
                        o_ref[pl.ds(row0 + c0, ch), :] = res.astype(o_ref.dtype)
                        return carry

                    lax.fori_loop(0, nseg // ch, fill, 0)
                    lax.fori_loop(0, nseg // ch, window, 0)

    blk = pl.BlockSpec((n, LANES), lambda j: (0, j))
    return pl.pallas_call(
        body, grid=(pool_width // LANES,), in_specs=[blk], out_specs=blk,
        out_shape=SDS((n, pool_width), out_dtype), scratch_shapes=[pltpu.VMEM((total, LANES), F32)],
        compiler_params=_cparams("parallel"), name=name)(src)


def _pool_proj_fwd(p, wp, l, scale, name):
    n, pw = p.shape
    ng, c = wp.shape[1], wp.shape[2]
    tm = _pick(n, _TM)

    def body(p_ref, w_ref, s_ref, o_ref):
        y = jnp.dot(p_ref[...], w_ref[...].astype(MXU_DTYPE), preferred_element_type=F32)
        o_ref[...] = (y * s_ref[...]).astype(o_ref.dtype)

    return pl.pallas_call(
        body, grid=(ng, n // tm),
        in_specs=[pl.BlockSpec((tm, c), lambda g, i: (i, g)), pl.BlockSpec((None, None, c, c), lambda g, i: (l, g, 0, 0)),
                  pl.BlockSpec((1, c), lambda g, i: (0, g))],
        out_specs=pl.BlockSpec((tm, c), lambda g, i: (i, g)), out_shape=SDS((n, pw), MXU_DTYPE),
        compiler_params=_cparams("parallel", "parallel"), name=name)(p, wp, scale)


def _pool_proj_bwd(p, dcat, wp, l, scale, name):
    n, pw = p.shape
    ng, c = wp.shape[1], wp.shape[2]
    tm = _pick(n, _TM)

    def body(p_ref, dy_ref, w_ref, s_ref, dp_ref, ds_ref, dw_ref):
        i = pl.program_id(1)

        @pl.when(i == 0)
        def _():
            ds_ref[...] = jnp.zeros(ds_ref.shape, F32)
            dw_ref[...] = jnp.zeros(dw_ref.shape, F32)

        pv, wv, dy = p_ref[...], w_ref[...].astype(MXU_DTYPE), dy_ref[...]
        y = jnp.dot(pv, wv, preferred_element_type=F32)
        ds_ref[...] += jnp.sum(dy * y, axis=0, keepdims=True)
        dpw = (dy * s_ref[...]).astype(MXU_DTYPE)
        dp_ref[...] = lax.dot_general(dpw, wv, _DIMS["nt"], preferred_element_type=F32)
        dw_ref[0] += lax.dot_general(pv, dpw, _DIMS["tn"], preferred_element_type=F32)

    return pl.pallas_call(
        body, grid=(ng, n // tm),
        in_specs=[pl.BlockSpec((tm, c), lambda g, i: (i, g)), pl.BlockSpec((tm, c), lambda g, i: (i, g)),
                  pl.BlockSpec((None, None, c, c), lambda g, i: (l, g, 0, 0)), pl.BlockSpec((1, c), lambda g, i: (0, g))],
        out_specs=[pl.BlockSpec((tm, c), lambda g, i: (i, g)), pl.BlockSpec((1, c), lambda g, i: (0, g)),
                   pl.BlockSpec((1, c, c), lambda g, i: (g, 0, 0))],
        out_shape=[SDS((n, pw), F32), SDS((1, pw), F32), SDS((ng, c, c), F32)],
        compiler_params=_cparams("arbitrary", "arbitrary"), name=name)(p, dcat, wp, scale)


def _disc_math(a_re, a_im, logdt, b_re, b_im):
    dt = jnp.exp(logdt)
    mag = jnp.exp(a_re * dt)
    lam_re = mag * jnp.cos(a_im * dt)
    lam_im = mag * jnp.sin(a_im * dt)
    denom = a_re * a_re + a_im * a_im
    nr, ni = lam_re - 1.0, lam_im
    f_re = ((nr * a_re + ni * a_im) / denom)[:, None, :]
    f_im = ((ni * a_re - nr * a_im) / denom)[:, None, :]
    return lam_re, lam_im, f_re * b_re - f_im * b_im, f_re * b_im + f_im * b_re


def _disc_fwd(a_re, a_im, logdt, b_re, b_im, name):
    def body(ar, ai, ld, br, bi, o_lr, o_li, o_br, o_bi):
        lr, li, bbr, bbi = _disc_math(ar[...], ai[...], ld[...], br[...], bi[...])
        o_lr[...] = lr
        o_li[...] = li
        o_br[...] = bbr
        o_bi[...] = bbi

    return pl.pallas_call(
        body, out_shape=[SDS(a_re.shape, F32), SDS(a_re.shape, F32), SDS(b_re.shape, F32), SDS(b_re.shape, F32)],
        compiler_params=_cparams(), name=name)(a_re, a_im, logdt, b_re, b_im)


def _disc_bwd(a_re, a_im, logdt, b_re, b_im, d_lr, d_li, d_bbr, d_bbi, group, name):
    rows, gp = a_re.shape

    def body(ar, ai, ld, br, bi, g_lr, g_li, g_br, g_bi, o_ar, o_ai, o_ld, o_br, o_bi):
        _, vjp = jax.vjp(_disc_math, ar[...], ai[...], ld[...], br[...], bi[...])
        dar, dai, dld, dbr, dbi = vjp((g_lr[...], g_li[...], g_br[...], g_bi[...]))
        o_ar[...] = dar
        o_ai[...] = dai
        state = lax.broadcasted_iota(jnp.int32, (gp, LANES), 0)
        first = lax.broadcasted_iota(jnp.int32, (gp, LANES), 1) * group
        sel = jnp.logical_and(state >= first, state < first + group).astype(F32)
        o_ld[...] = jnp.dot(dld, sel, precision=HIGHEST, preferred_element_type=F32)
        o_br[...] = dbr
        o_bi[...] = dbi

    return pl.pallas_call(
        body, out_shape=[SDS(a_re.shape, F32), SDS(a_re.shape, F32), SDS((rows, LANES), F32),
                         SDS(b_re.shape, F32), SDS(b_re.shape, F32)],
        compiler_params=_cparams(), name=name)(a_re, a_im, logdt, b_re, b_im, d_lr, d_li, d_bbr, d_bbi)


def _scan_maps(nbc, nb):
    nbl = nb - nbc
    fwd0 = lambda i: (i, 0, 0)
    fwd1 = lambda i: (jnp.where(i < nbc, nbc - 1 - i, nb - 1 - (i - nbc)), 0, 0)
    adj0 = lambda i: (nb - 1 - i, 0, 0)
    adj1 = lambda i: (jnp.where(i < nbl, nbc + i, i - nbl), 0, 0)
    return fwd0, fwd1, adj0, adj1


def _scan_fwd(bu0, bu1, lam, n_ctx, name):
    n, s2, _ = bu0.shape
    s = s2 // 2
    tt = math.gcd(math.gcd(n_ctx, n - n_ctx), 128)
    nbc, nb = n_ctx // tt, n // tt
    fwd0, fwd1, _, _ = _scan_maps(nbc, nb)

    def body(b0_ref, b1_ref, lam_ref, h0_ref, h1_ref, st_ref):
        @pl.when(pl.program_id(0) == 0)
        def _():
            st_ref[...] = jnp.zeros(st_ref.shape, F32)

        lam = [(lam_ref[0], lam_ref[1]), (lam_ref[2], lam_ref[3])]
        lam2 = [(lr * lr - li * li, 2.0 * lr * li) for lr, li in lam]

        def pair(b_ref, h_ref, ra, rb, lm, lm2, hr, hi):
            (lr, li), (l2r, l2i) = lm, lm2
            bar, bai = b_ref[ra, 0:s, :], b_ref[ra, s:s2, :]
            cr = lr * bar - li * bai + b_ref[rb, 0:s, :]
            ci = lr * bai + li * bar + b_ref[rb, s:s2, :]
            h_ref[ra, 0:s, :] = lr * hr - li * hi + bar
            h_ref[ra, s:s2, :] = lr * hi + li * hr + bai
            nr = l2r * hr - l2i * hi + cr
            ni = l2r * hi + l2i * hr + ci
            h_ref[rb, 0:s, :] = nr
            h_ref[rb, s:s2, :] = ni
            return nr, ni

        def step(jj, carry):
            h0r, h0i, h1r, h1i = carry
            ja = 2 * jj
            ta = tt - 1 - ja
            h0r, h0i = pair(b0_ref, h0_ref, ja, ja + 1, lam[0], lam2[0], h0r, h0i)
            h1r, h1i = pair(b1_ref, h1_ref, ta, ta - 1, lam[1], lam2[1], h1r, h1i)
            return h0r, h0i, h1r, h1i

        out = lax.fori_loop(0, tt // 2, step, (st_ref[0], st_ref[1], st_ref[2], st_ref[3]), unroll=2)
        for q in range(4):
            st_ref[q] = out[q]

    blk = (tt, s2, LANES)
    return pl.pallas_call(
        body, grid=(nb,),
        in_specs=[pl.BlockSpec(blk, fwd0), pl.BlockSpec(blk, fwd1), pl.BlockSpec((4, s, LANES), lambda i: (0, 0, 0))],
        out_specs=[pl.BlockSpec(blk, fwd0), pl.BlockSpec(blk, fwd1)],
        out_shape=[SDS(bu0.shape, F32), SDS(bu1.shape, F32)],
        scratch_shapes=[pltpu.VMEM((4, s, LANES), F32)],
        compiler_params=_cparams("arbitrary"), name=name)(bu0, bu1, lam)


def _scan_bwd(dh0, dh1, h0, h1, lam, n_ctx, name):
    n, s2, _ = dh0.shape
    s = s2 // 2
    tt = math.gcd(math.gcd(n_ctx, n - n_ctx), 128)
    nbc, nb = n_ctx // tt, n // tt
    _, _, adj0, adj1 = _scan_maps(nbc, nb)

    def body(d0_ref, d1_ref, h0_ref, h1_ref, lam_ref, a0_ref, a1_ref, dl_ref, st_ref, acc_ref):
        i = pl.program_id(0)

        @pl.when(i == 0)
        def _():
            st_ref[...] = jnp.zeros(st_ref.shape, F32)
            acc_ref[...] = jnp.zeros(acc_ref.shape, F32)

        lam = [(lam_ref[0], lam_ref[1]), (lam_ref[2], lam_ref[3])]
        lam2 = [(lr * lr - li * li, 2.0 * lr * li) for lr, li in lam]

        def pair(d_ref, h_ref, a_ref, ra, rb, lm, lm2, ar, ai, cr, ci):
            (lr, li), (l2r, l2i) = lm, lm2
            dar, dai = d_ref[ra, 0:s, :], d_ref[ra, s:s2, :]
            er = lr * dar + li * dai + d_ref[rb, 0:s, :]
            ei = lr * dai - li * dar + d_ref[rb, s:s2, :]
            mr = lr * ar + li * ai + dar
            mi = lr * ai - li * ar + dai
            a_ref[ra, 0:s, :] = mr
            a_ref[ra, s:s2, :] = mi
            gar, gai = h_ref[ra, 0:s, :], h_ref[ra, s:s2, :]
            gbr, gbi = h_ref[rb, 0:s, :], h_ref[rb, s:s2, :]
            cr = cr + ((ar * gar + ai * gai) + (mr * gbr + mi * gbi))
            ci = ci + ((ai * gar - ar * gai) + (mi * gbr - mr * gbi))
            nr = l2r * ar + l2i * ai + er
            ni = l2r * ai - l2i * ar + ei
            a_ref[rb, 0:s, :] = nr
            a_ref[rb, s:s2, :] = ni
            return nr, ni, cr, ci

        def step(jj, carry):
            a0r, a0i, a1r, a1i, c0r, c0i, c1r, c1i = carry
            ja = 2 * jj
            ta = tt - 1 - ja
            a0r, a0i, c0r, c0i = pair(d0_ref, h0_ref, a0_ref, ta, ta - 1, lam[0], lam2[0], a0r, a0i, c0r, c0i)
            a1r, a1i, c1r, c1i = pair(d1_ref, h1_ref, a1_ref, ja, ja + 1, lam[1], lam2[1], a1r, a1i, c1r, c1i)
            return a0r, a0i, a1r, a1i, c0r, c0i, c1r, c1i

        init = tuple(st_ref[q] for q in range(4)) + tuple(acc_ref[q] for q in range(4))
        out = lax.fori_loop(0, tt // 2, step, init, unroll=2)
        for q in range(4):
            st_ref[q] = out[q]
            acc_ref[q] = out[4 + q]

        @pl.when(i == nb - 1)
        def _():
            for q in range(4):
                dl_ref[q] = out[4 + q]

    blk = (tt, s2, LANES)
    small = pl.BlockSpec((4, s, LANES), lambda i: (0, 0, 0))
    return pl.pallas_call(
        body, grid=(nb,),
        in_specs=[pl.BlockSpec(blk, adj0), pl.BlockSpec(blk, adj1), pl.BlockSpec(blk, adj0), pl.BlockSpec(blk, adj1), small],
        out_specs=[pl.BlockSpec(blk, adj0), pl.BlockSpec(blk, adj1), small],
        out_shape=[SDS(dh0.shape, F32), SDS(dh1.shape, F32), SDS((4, s, LANES), F32)],
        scratch_shapes=[pltpu.VMEM((4, s, LANES), F32), pltpu.VMEM((4, s, LANES), F32)],
        compiler_params=_cparams("arbitrary"), name=name)(dh0, dh1, h0, h1, lam)


def _gelu(v):
    th = jnp.tanh(GELU_C0 * (v + GELU_C1 * v * v * v))
    return 0.5 * v * (1.0 + th), th


def _ssm_head_fwd(y, u, ssm_d, wg, l, name):
    n, sw = y.shape
    ucol = u.shape[1] // sw - 1
    tm = _pick(n, _TM)

    def body(y_ref, u_ref, d_ref, w_ref, o_ref):
        act, _ = _gelu(y_ref[...] + d_ref[...] * u_ref[...])
        q = jnp.dot(act.astype(MXU_DTYPE), w_ref[...].astype(MXU_DTYPE), preferred_element_type=F32)
        o_ref[...] = (act * jax.nn.sigmoid(q)).astype(o_ref.dtype)

    row = pl.BlockSpec((tm, sw), lambda i: (i, 0))
    return pl.pallas_call(
        body, grid=(n // tm,),
        in_specs=[row, pl.BlockSpec((tm, sw), lambda i: (i, ucol)), pl.BlockSpec((1, sw), lambda i: (0, 0)),
                  pl.BlockSpec((None, sw, sw), lambda i: (l, 0, 0))],
        out_specs=row, out_shape=SDS((n, sw), MXU_DTYPE), compiler_params=_cparams("parallel"), name=name)(y, u, ssm_d, wg)


def _ssm_head_bwd(dcat, y, u, ssm_d, wg, l, name):
    n, sw = y.shape
    ucol = u.shape[1] // sw - 1
    tm = _pick(n, _TM)

    def body(do_ref, y_ref, u_ref, d_ref, w_ref, dy_ref, du_ref, act_ref, dq_ref, dd_ref):
        @pl.when(pl.program_id(0) == 0)
        def _():
            dd_ref[...] = jnp.zeros(dd_ref.shape, F32)

        uv, dv, do = u_ref[...], d_ref[...], do_ref[...]
        yf = y_ref[...] + dv * uv
        act, th = _gelu(yf)
        wv = w_ref[...].astype(MXU_DTYPE)
        sg = jax.nn.sigmoid(jnp.dot(act.astype(MXU_DTYPE), wv, preferred_element_type=F32))
        dq = (do * act * sg * (1.0 - sg)).astype(MXU_DTYPE)
        dact = do * sg + lax.dot_general(dq, wv, _DIMS["nt"], preferred_element_type=F32)
        dgelu = 0.5 * (1.0 + th) + 0.5 * yf * (1.0 - th * th) * GELU_C0 * (1.0 + 3.0 * GELU_C1 * yf * yf)
        dyf = dact * dgelu
        dy_ref[...] = dyf.astype(dy_ref.dtype)
        du_ref[...] = dyf * dv
        act_ref[...] = act.astype(act_ref.dtype)
        dq_ref[...] = dq
        dd_ref[...] += jnp.sum(dyf * uv, axis=0, keepdims=True)

    row = pl.BlockSpec((tm, sw), lambda i: (i, 0))
    last = pl.BlockSpec((tm, sw), lambda i: (i, ucol))
    vec = pl.BlockSpec((1, sw), lambda i: (0, 0))
    return pl.pallas_call(
        body, grid=(n // tm,),
        in_specs=[last, row, last, vec, pl.BlockSpec((None, sw, sw), lambda i: (l, 0, 0))],
        out_specs=[row, row, row, row, vec],
        out_shape=[SDS((n, sw), MXU_DTYPE), SDS((n, sw), F32), SDS((n, sw), MXU_DTYPE), SDS((n, sw), MXU_DTYPE),
                   SDS((1, sw), F32)],
        compiler_params=_cparams("arbitrary"), name=name)(dcat, y, u, ssm_d, wg)


def _assemble_du(du_pool, du_dir, du_proj, name):
    n, pw = du_pool.shape
    sw = du_dir.shape[1]
    tm = _pick(n, _TM)

    def body(p_ref, a_ref, b_ref, o_ref):
        o_ref[:, 0:pw] = p_ref[...].astype(o_ref.dtype)
        o_ref[:, pw:pw + sw] = (a_ref[...] + b_ref[...]).astype(o_ref.dtype)

    return pl.pallas_call(
        body, grid=(n // tm,),
        in_specs=[pl.BlockSpec((tm, pw), lambda i: (i, 0)), pl.BlockSpec((tm, sw), lambda i: (i, 0)),
                  pl.BlockSpec((tm, sw), lambda i: (i, 0))],
        out_specs=pl.BlockSpec((tm, pw + sw), lambda i: (i, 0)), out_shape=SDS((n, pw + sw), MXU_DTYPE),
        compiler_params=_cparams("parallel"), name=name)(du_pool, du_dir, du_proj)


CONV_PAD = GRID_W + SUBLANES


def _conv_layout(n, n_ctx):
    return CONV_PAD, 2 * CONV_PAD + n_ctx, 3 * CONV_PAD + n


def _col_masks(ch):
    col = lax.broadcasted_iota(jnp.int32, (ch, LANES), 0) % GRID_W
    return col != 0, col != GRID_W - 1


def _fill_padded(scr, src_ref, n, n_ctx):
    base_c, base_l, total = _conv_layout(n, n_ctx)
    for base in (0, base_c + n_ctx, base_l + n - n_ctx):
        scr[pl.ds(base, CONV_PAD), :] = jnp.zeros((CONV_PAD, LANES), F32)
    for row0, nseg, base in ((0, n_ctx, base_c), (n_ctx, n - n_ctx, base_l)):
        ch = math.gcd(nseg, 512)

        def copy(ci, carry, row0=row0, base=base, ch=ch):
            c0 = pl.multiple_of(ci * ch, ch)
            scr[pl.ds(base + c0, ch), :] = src_ref[pl.ds(row0 + c0, ch), :]
            return carry

        lax.fori_loop(0, nseg // ch, copy, 0)


def _conv_ctx(scr, k_ref, base, c0, ch, sign):
    acc = scr[pl.ds(c0 + base, ch), :] * k_ref[4:5, :]
    acc = acc + scr[pl.ds(c0 + (base - sign), ch), :] * k_ref[3:4, :]
    return acc + scr[pl.ds(c0 + (base + sign), ch), :] * k_ref[5:6, :]


def _conv_lat(scr, k_ref, base, c0, ch, sign, m_l, m_r):
    cols = []
    for j in range(3):
        acc = None
        for i in range(3):
            off = sign * (GRID_W * (i - 1) + (j - 1))
            term = scr[pl.ds(c0 + (base + off), ch), :] * k_ref[3 * i + j:3 * i + j + 1, :]
            acc = term if acc is None else acc + term
        cols.append(acc)
    first, last = (m_l, m_r) if sign > 0 else (m_r, m_l)
    return cols[1] + jnp.where(first, cols[0], 0.0) + jnp.where(last, cols[2], 0.0)


def _conv_chunk(n_lat):
    return math.gcd(n_lat, 256)


def _conv_glu_fwd(z, wk, n_ctx, name):
    n, f2 = z.shape
    dff = f2 // 2
    nvt = dff // LANES
    n_lat = n - n_ctx
    base_c, base_l, total = _conv_layout(n, n_ctx)
    ch = _conv_chunk(n_lat)
    assert ch % GRID_W == 0

    def body(zv_ref, zg_ref, kv_ref, kg_ref, a_ref, cv_ref, cg_ref, sv, sg):
        _fill_padded(sv, zv_ref, n, n_ctx)
        _fill_padded(sg, zg_ref, n, n_ctx)

        def emit(cv, cg, row, rows):
            cv_ref[pl.ds(row, rows), :] = cv
            cg_ref[pl.ds(row, rows), :] = cg
            a_ref[pl.ds(row, rows), :] = (cv * cg * jax.nn.sigmoid(cg)).astype(a_ref.dtype)

        emit(_conv_ctx(sv, kv_ref, base_c, 0, n_ctx, 1), _conv_ctx(sg, kg_ref, base_c, 0, n_ctx, 1), 0, n_ctx)
        m_l, m_r = _col_masks(ch)

        def lat(ci, carry):
            c0 = pl.multiple_of(ci * ch, ch)
            emit(_conv_lat(sv, kv_ref, base_l, c0, ch, 1, m_l, m_r), _conv_lat(sg, kg_ref, base_l, c0, ch, 1, m_l, m_r),
                 n_ctx + c0, ch)
            return carry

        lax.fori_loop(0, n_lat // ch, lat, 0)

    col = lambda shift: pl.BlockSpec((n, LANES), lambda j: (0, j + shift))
    kcol = lambda shift: pl.BlockSpec((9, LANES), lambda j: (0, j + shift))
    return pl.pallas_call(
        body, grid=(nvt,), in_specs=[col(0), col(nvt), kcol(0), kcol(nvt)], out_specs=[col(0), col(0), col(0)],
        out_shape=[SDS((n, dff), MXU_DTYPE), SDS((n, dff), F32), SDS((n, dff), F32)],
        scratch_shapes=[pltpu.VMEM((total, LANES), F32), pltpu.VMEM((total, LANES), F32)],
        compiler_params=_cparams("parallel"), name=name)(z, z, wk, wk)


def _conv_glu_bwd(z, cv, cg, da, wk, n_ctx, name):
    n, f2 = z.shape
    dff = f2 // 2
    nvt = dff // LANES
    n_lat = n - n_ctx
    base_c, base_l, total = _conv_layout(n, n_ctx)
    ch = _conv_chunk(n_lat)
    assert ch % GRID_W == 0
    ctx_taps = [(1, 0), (1, 1), (1, 2)]
    lat_taps = [(i, j) for i in range(3) for j in range(3)]

    def tap_sums(acc, scr, d, base, c0, rows, taps, masks):
        acc = list(acc)
        by_col = (d, d, d) if masks is None else (jnp.where(masks[0], d, 0.0), d, jnp.where(masks[1], d, 0.0))
        for i, j in taps:
            src = scr[pl.ds(c0 + (base + GRID_W * (i - 1) + (j - 1)), rows), :]
            acc[3 * i + j] = acc[3 * i + j] + jnp.sum((src * by_col[j]).reshape(rows // SUBLANES, SUBLANES, LANES), axis=0)
        return acc

    def body(zv_ref, zg_ref, cv_ref, cg_ref, da_ref, kv_ref, kg_ref, dz_ref, dkv_ref, dkg_ref, a_ref, sv, sg, dv, dg):
        _fill_padded(sv, zv_ref, n, n_ctx)
        _fill_padded(sg, zg_ref, n, n_ctx)
        for base in (0, base_c + n_ctx, base_l + n_lat):
            dv[pl.ds(base, CONV_PAD), :] = jnp.zeros((CONV_PAD, LANES), F32)
            dg[pl.ds(base, CONV_PAD), :] = jnp.zeros((CONV_PAD, LANES), F32)
        m_l, m_r = _col_masks(ch)

        def first_pass(row, pad_row, rows):
            cv, cg = cv_ref[pl.ds(row, rows), :], cg_ref[pl.ds(row, rows), :]
            sig = jax.nn.sigmoid(cg)
            silu = cg * sig
            a_ref[pl.ds(row, rows), :] = (cv * silu).astype(a_ref.dtype)
            dav = da_ref[pl.ds(row, rows), :]
            dcv = dav * silu
            dcg = dav * cv * (sig * (1.0 + cg * (1.0 - sig)))
            dv[pl.ds(pad_row, rows), :] = dcv
            dg[pl.ds(pad_row, rows), :] = dcg
            return dcv, dcg

        zero = [jnp.zeros((SUBLANES, LANES), F32) for _ in range(9)]
        dcv, dcg = first_pass(0, base_c, n_ctx)
        accv = tap_sums(zero, sv, dcv, base_c, 0, n_ctx, ctx_taps, None)
        accg = tap_sums(zero, sg, dcg, base_c, 0, n_ctx, ctx_taps, None)

        def lat1(ci, carry):
            accv, accg = carry
            c0 = pl.multiple_of(ci * ch, ch)
            dcv, dcg = first_pass(n_ctx + c0, base_l + c0, ch)
            accv = tap_sums(accv, sv, dcv, base_l, c0, ch, lat_taps, (m_l, m_r))
            accg = tap_sums(accg, sg, dcg, base_l, c0, ch, lat_taps, (m_l, m_r))
            return tuple(accv), tuple(accg)

        accv, accg = lax.fori_loop(0, n_lat // ch, lat1, (tuple(accv), tuple(accg)))
        for t in range(9):
            dkv_ref[t:t + 1, :] = jnp.sum(accv[t], axis=0, keepdims=True)
            dkg_ref[t:t + 1, :] = jnp.sum(accg[t], axis=0, keepdims=True)

        dz_ref[0, pl.ds(0, n_ctx), :] = _conv_ctx(dv, kv_ref, base_c, 0, n_ctx, -1).astype(dz_ref.dtype)
        dz_ref[1, pl.ds(0, n_ctx), :] = _conv_ctx(dg, kg_ref, base_c, 0, n_ctx, -1).astype(dz_ref.dtype)

        def lat2(ci, carry):
            c0 = pl.multiple_of(ci * ch, ch)
            dz_ref[0, pl.ds(n_ctx + c0, ch), :] = _conv_lat(dv, kv_ref, base_l, c0, ch, -1, m_l, m_r).astype(dz_ref.dtype)
            dz_ref[1, pl.ds(n_ctx + c0, ch), :] = _conv_lat(dg, kg_ref, base_l, c0, ch, -1, m_l, m_r).astype(dz_ref.dtype)
            return carry

        lax.fori_loop(0, n_lat // ch, lat2, 0)

    col = lambda shift: pl.BlockSpec((n, LANES), lambda j: (0, j + shift))
    kcol = lambda shift: pl.BlockSpec((9, LANES), lambda j: (0, j + shift))
    pad = pltpu.VMEM((total, LANES), F32)
    return pl.pallas_call(
        body, grid=(nvt,), in_specs=[col(0), col(nvt), col(0), col(0), col(0), kcol(0), kcol(nvt)],
        out_specs=[pl.BlockSpec((2, n, LANES), lambda j: (0, 0, j)), kcol(0), kcol(0), col(0)],
        out_shape=[SDS((2, n, dff), MXU_DTYPE), SDS((9, dff), F32), SDS((9, dff), F32), SDS((n, dff), MXU_DTYPE)],
        scratch_shapes=[pad, pad, pad, pad],
        compiler_params=_cparams("parallel"), name=name)(z, z, cv, cg, da, wk, wk)


def _silu(v):
    return v * jax.nn.sigmoid(v)


def _ada_fwd(cond, w_ada, b_shard, name):
    nl, d, cols = w_ada.shape
    tn = _pick(cols, (512, 256, 128))

    def body(c_ref, w_ref, b_ref, o_ref):
        o_ref[...] = jnp.dot(_silu(c_ref[...]), w_ref[...], precision=HIGHEST, preferred_element_type=F32) + b_ref[...]

    return pl.pallas_call(
        body, grid=(nl, cols // tn),
        in_specs=[pl.BlockSpec(cond.shape, lambda l, j: (0, 0)), pl.BlockSpec((None, d, tn), lambda l, j: (l, 0, j)),
                  pl.BlockSpec((None, 1, tn), lambda l, j: (l, 0, j))],
        out_specs=pl.BlockSpec((None, cond.shape[0], tn), lambda l, j: (l, 0, j)),
        out_shape=SDS((nl, cond.shape[0], cols), F32),
        compiler_params=_cparams("parallel", "parallel"), name=name)(cond, w_ada, b_shard)


def _ada_dw(cond, dmod, name):
    nl, rows, cols = dmod.shape
    d = cond.shape[1]
    tn = _pick(cols, (512, 256, 128))

    def body(c_ref, g_ref, o_ref):
        o_ref[...] = lax.dot_general(_silu(c_ref[...]), g_ref[...], _DIMS["tn"], precision=HIGHEST,
                                     preferred_element_type=F32)

    return pl.pallas_call(
        body, grid=(nl, cols // tn),
        in_specs=[pl.BlockSpec(cond.shape, lambda l, j: (0, 0)), pl.BlockSpec((None, rows, tn), lambda l, j: (l, 0, j))],
        out_specs=pl.BlockSpec((None, d, tn), lambda l, j: (l, 0, j)), out_shape=SDS((nl, d, cols), F32),
        compiler_params=_cparams("parallel", "parallel"), name=name)(cond, dmod)


def _ada_dcond(dmod, w_ada, name):
    nl, rows, cols = dmod.shape
    d = w_ada.shape[1]
    tn = _pick(cols, (512, 256, 128))

    def body(g_ref, w_ref, o_ref):
        @pl.when(jnp.logical_and(pl.program_id(0) == 0, pl.program_id(1) == 0))
        def _():
            o_ref[...] = jnp.zeros(o_ref.shape, F32)

        o_ref[...] += lax.dot_general(g_ref[...], w_ref[...], _DIMS["nt"], precision=HIGHEST, preferred_element_type=F32)

    return pl.pallas_call(
        body, grid=(nl, cols // tn),
        in_specs=[pl.BlockSpec((None, rows, tn), lambda l, j: (l, 0, j)), pl.BlockSpec((None, d, tn), lambda l, j: (l, 0, j))],
        out_specs=pl.BlockSpec((rows, d), lambda l, j: (0, 0)), out_shape=SDS((rows, d), F32),
        compiler_params=_cparams("arbitrary", "arbitrary"), name=name)(dmod, w_ada)


def _ada_rows(dmod_all, name):
    nd, nl, _, w = dmod_all.shape
    tn = _pick(w, (2048, 1024, 512, 256, 128))

    def body(g_ref, rows_ref, db_ref):
        ctx = g_ref[0, 0, 0:1, :]
        for b in range(1, nd):
            ctx = ctx + g_ref[b, 0, 0:1, :]
        total = ctx
        for b in range(nd):
            lat = g_ref[b, 0, 1:2, :]
            rows_ref[b:b + 1, :] = lat
            total = total + lat
        rows_ref[nd:nd + 1, :] = ctx
        rows_ref[nd + 1:16, :] = jnp.zeros((16 - nd - 1, tn), F32)
        db_ref[...] = total

    return pl.pallas_call(
        body, grid=(nl, w // tn),
        in_specs=[pl.BlockSpec((nd, 1, 2, tn), lambda l, j: (0, l, 0, j))],
        out_specs=[pl.BlockSpec((None, 16, tn), lambda l, j: (l, 0, j)), pl.BlockSpec((None, 1, tn), lambda l, j: (l, 0, j))],
        out_shape=[SDS((nl, 16, w), F32), SDS((nl, 1, w), F32)],
        compiler_params=_cparams("parallel", "parallel"), name=name)(dmod_all)


def _ada_dctx(parts, c_ctx, row, name):
    def body(p_ref, c_ref, o_ref):
        ds = p_ref[0, row:row + 1, :]
        for k in range(1, p_ref.shape[0]):
            ds = ds + p_ref[k, row:row + 1, :]
        cv = c_ref[...]
        sg = jax.nn.sigmoid(cv)
        o_ref[...] = ds * (sg * (1.0 + cv * (1.0 - sg)))

    return pl.pallas_call(body, out_shape=SDS(c_ctx.shape, F32), compiler_params=_cparams(), name=name)(parts, c_ctx)


ROW_BLOCK_BYTES = 1 << 20


def _as_rows(shape):
    size = math.prod(shape)
    cols = shape[-1] if len(shape) >= 2 and shape[-1] % LANES == 0 else _pick(size, (1024, 512, 256, 128))
    rows = size // cols
    fits = [t for t in (512, 256, 128, 64, 32, 16, 8) if t * cols * 4 <= ROW_BLOCK_BYTES]
    return rows, cols, _pick(rows, fits)


def _adamw(w, g, m, v, name):
    rows, cols, tr = _as_rows(w.shape)
    c1 = 1.0 / (1.0 - ADAM_B1 ** ADAM_STEP)
    c2 = 1.0 / (1.0 - ADAM_B2 ** ADAM_STEP)

    def body(w_ref, g_ref, m_ref, v_ref, d_ref, nm_ref, nv_ref):
        gv = g_ref[...]
        nm = ADAM_B1 * m_ref[...] + (1.0 - ADAM_B1) * gv
        nv = ADAM_B2 * v_ref[...] + (1.0 - ADAM_B2) * (gv * gv)
        nm_ref[...] = nm
        nv_ref[...] = nv
        d_ref[...] = -ADAM_LR * ((nm * c1) / (jnp.sqrt(nv * c2) + ADAM_EPS) + ADAM_WD * w_ref[...])

    blk = pl.BlockSpec((tr, cols), lambda i: (i, 0))
    outs = pl.pallas_call(
        body, grid=(rows // tr,), in_specs=[blk] * 4, out_specs=[blk] * 3, out_shape=[SDS((rows, cols), F32)] * 3,
        compiler_params=_cparams("parallel"), name=name)(*[t.reshape(rows, cols) for t in (w, g, m, v)])
    return tuple(o.reshape(w.shape) for o in outs)


def _tile_rows(rows, cols, dtype):
    size = jnp.dtype(dtype).itemsize
    fits = [t for t in (512, 256, 128, 64, 32, 16, 8) if t * cols * size <= ROW_BLOCK_BYTES and t * size >= 32]
    return _pick(rows, fits)


def _scalar_spec(grid, in_specs, out_specs):
    return pltpu.PrefetchScalarGridSpec(num_scalar_prefetch=1, grid=grid, in_specs=in_specs, out_specs=out_specs)


def _place_chunk(shard, k_idx, dtype, name):
    nl, rows, cols = shard.shape
    tr = _tile_rows(rows, cols, dtype)

    def body(k_ref, s_ref, o_ref):
        o_ref[...] = s_ref[...].astype(o_ref.dtype)

    return pl.pallas_call(
        body, out_shape=SDS((nl, N_CHIPS, rows, cols), dtype),
        grid_spec=_scalar_spec((nl, rows // tr), [pl.BlockSpec((None, tr, cols), lambda l, i, k: (l, i, 0))],
                               pl.BlockSpec((None, None, tr, cols), lambda l, i, k: (l, k[0], i, 0))),
        compiler_params=_cparams("parallel", "parallel"), name=name)(k_idx, shard)


def _pair_sum(grads, recv, c_idx, name):
    half, nch, rows, cols = recv.shape
    tr = _tile_rows(rows, cols, recv.dtype)

    def body(c_ref, g_ref, r_ref, o_ref):
        o_ref[...] = (g_ref[...].astype(F32) + r_ref[...].astype(F32)).astype(o_ref.dtype)

    blk = pl.BlockSpec((None, None, tr, cols), lambda h, q, i, c: (h, q, i, 0))
    return pl.pallas_call(
        body, out_shape=SDS(recv.shape, recv.dtype),
        grid_spec=_scalar_spec((half, nch, rows // tr),
                               [pl.BlockSpec((None, None, tr, cols), lambda h, q, i, c: (c[0] * half + h, q, i, 0)), blk], blk),
        compiler_params=_cparams("parallel", "parallel", "parallel"), name=name)(c_idx, grads, recv)


def _chip_sum(parts, recv, kc_idx, name):
    half, nch, rows, cols = parts.shape
    tr = _tile_rows(rows, cols, F32)

    def body(kc_ref, p_ref, r_ref, o_ref):
        acc = p_ref[...].astype(F32)
        for s in range(r_ref.shape[0]):
            acc = acc + r_ref[s].astype(F32)
        o_ref[...] = acc

    return pl.pallas_call(
        body, out_shape=SDS((2 * half, rows, cols), F32),
        grid_spec=_scalar_spec((half, rows // tr),
                               [pl.BlockSpec((None, None, tr, cols), lambda h, i, kc: (h, kc[0], i, 0)),
                                pl.BlockSpec((nch - 1, None, tr, cols), lambda h, i, kc: (0, h, i, 0))],
                               pl.BlockSpec((None, tr, cols), lambda h, i, kc: (kc[1] * half + h, i, 0))),
        compiler_params=_cparams("parallel", "parallel"), name=name)(kc_idx, parts, recv)


PIECE_BYTES = 3 << 20
MAX_PIECES = 16
PIECE_ROW_ALIGN = 16


def _coords():
    return lax.axis_index("x"), lax.axis_index("y"), lax.axis_index("c")


def _other_chips(x, y):
    return [(1 - x, y), (x, 1 - y), (1 - x, 1 - y)]


def _row_pieces(rows, nbytes):
    pieces = 1
    while (pieces < MAX_PIECES and nbytes // pieces > PIECE_BYTES and rows % (2 * pieces * PIECE_ROW_ALIGN) == 0):
        pieces *= 2
    step = rows // pieces
    return [pl.ds(i * step, step) for i in range(pieces)]


def _nbytes(shape, dtype):
    return math.prod(shape) * jnp.dtype(dtype).itemsize


def _offsets(counts):
    out, pos = [], 0
    for cnt in counts:
        out.append(pos)
        pos += cnt
    return out, pos


def _gather_chips(placed, name):
    nt = len(placed)
    half = [p.shape[0] // 2 for p in placed]
    pieces = [_row_pieces(p.shape[2], _nbytes((h,) + p.shape[2:], p.dtype)) for p, h in zip(placed, half)]
    base, total = _offsets([len(p) for p in pieces])

    def body(*refs):
        o_refs = refs[nt:2 * nt]
        s_nbr, r_nbr, s_fwd, r_fwd, s_sib, r_sib = refs[2 * nt:]
        x, y, c = _coords()
        k, kx, ky, kd = 2 * x + y, 2 * (1 - x) + y, 2 * x + (1 - y), 2 * (1 - x) + (1 - y)
        across_x, across_y, sibling = (1 - x, y, c), (x, 1 - y, c), (x, y, 1 - c)
        sends = []

        def copy(o_ref, rows, slot, rs, ssem, rsem, q, to):
            return pltpu.make_async_remote_copy(
                src_ref=o_ref.at[rows, slot, rs], dst_ref=o_ref.at[rows, slot, rs], send_sem=ssem.at[q], recv_sem=rsem.at[q],
                device_id=to, device_id_type=MESH)

        def start(cp):
            cp.start()
            sends.append(cp)

        work = [(t, i, rs, base[t] + i, 2 * i < len(pieces[t]) or len(pieces[t]) == 1)
                for t in range(nt) for i, rs in enumerate(pieces[t])]
        for t, i, rs, q, _ in work:
            mine = pl.ds(c * half[t], half[t])
            start(copy(o_refs[t], mine, k, rs, s_nbr, r_nbr, 2 * q, across_x))
            start(copy(o_refs[t], mine, k, rs, s_nbr, r_nbr, 2 * q + 1, across_y))
        for t, i, rs, q, via_x in work:
            mine = pl.ds(c * half[t], half[t])
            copy(o_refs[t], mine, kx, rs, s_nbr, r_nbr, 2 * q, across_x).wait_recv()
            start(copy(o_refs[t], mine, kx, rs, s_sib, r_sib, 3 * q, sibling))
            if not via_x:
                start(copy(o_refs[t], mine, kx, rs, s_fwd, r_fwd, q, across_y))
            copy(o_refs[t], mine, ky, rs, s_nbr, r_nbr, 2 * q + 1, across_y).wait_recv()
            start(copy(o_refs[t], mine, ky, rs, s_sib, r_sib, 3 * q + 1, sibling))
            if via_x:
                start(copy(o_refs[t], mine, ky, rs, s_fwd, r_fwd, q, across_x))
        for t, i, rs, q, via_x in work:
            mine = pl.ds(c * half[t], half[t])
            copy(o_refs[t], mine, kd, rs, s_fwd, r_fwd, q, across_x if via_x else across_y).wait_recv()
            start(copy(o_refs[t], mine, kd, rs, s_sib, r_sib, 3 * q + 2, sibling))
        for t, i, rs, q, _ in work:
            theirs = pl.ds((1 - c) * half[t], half[t])
            for r, slot in enumerate((kx, ky, kd)):
                copy(o_refs[t], theirs, slot, rs, s_sib, r_sib, 3 * q + r, sibling).wait_recv()
        for cp in sends:
            cp.wait_send()

    sem = pltpu.SemaphoreType.DMA
    outs = pl.pallas_call(
        body, in_specs=[ANY] * nt, out_specs=[ANY] * nt,
        out_shape=[SDS(p.shape, p.dtype) for p in placed],
        input_output_aliases={t: t for t in range(nt)},
        scratch_shapes=[sem((2 * total,)), sem((2 * total,)), sem((total,)), sem((total,)), sem((3 * total,)),
                        sem((3 * total,))],
        name=name)(*placed)
    return list(outs)


def _pair_send(grads, name):
    nt = len(grads)
    half = [g.shape[0] // 2 for g in grads]
    pieces = [_row_pieces(g.shape[2], _nbytes((h,) + g.shape[1:], g.dtype)) for g, h in zip(grads, half)]
    base, total = _offsets([len(p) for p in pieces])

    def body(*refs):
        g_refs, o_refs = refs[:nt], refs[nt:2 * nt]
        ssem, rsem = refs[2 * nt:]
        x, y, c = _coords()
        cps = []
        for t in range(nt):
            theirs = pl.ds((1 - c) * half[t], half[t])
            for i, rs in enumerate(pieces[t]):
                q = base[t] + i
                cp = pltpu.make_async_remote_copy(
                    src_ref=g_refs[t].at[theirs, :, rs], dst_ref=o_refs[t].at[:, :, rs], send_sem=ssem.at[q],
                    recv_sem=rsem.at[q], device_id=(x, y, 1 - c), device_id_type=MESH)
                cp.start()
                cps.append(cp)
        for cp in cps:
            cp.wait_recv()
        for cp in cps:
            cp.wait_send()

    sem = pltpu.SemaphoreType.DMA
    outs = pl.pallas_call(
        body, in_specs=[ANY] * nt, out_specs=[ANY] * nt,
        out_shape=[SDS((g.shape[0] // 2,) + g.shape[1:], g.dtype) for g in grads],
        scratch_shapes=[sem((total,)), sem((total,))],
        name=name)(*grads)
    return list(outs)


def _chip_send(parts, name):
    nt = len(parts)
    pieces = [_row_pieces(p.shape[2], _nbytes((p.shape[0],) + p.shape[2:], p.dtype)) for p in parts]
    base, total = _offsets([len(p) for p in pieces])

    def body(*refs):
        p_refs, o_refs = refs[:nt], refs[nt:2 * nt]
        ssem, rsem = refs[2 * nt:]
        x, y, c = _coords()
        cps = []
        for t in range(nt):
            for i, rs in enumerate(pieces[t]):
                for r, (px, py) in enumerate(_other_chips(x, y)):
                    q = 3 * (base[t] + i) + r
                    cp = pltpu.make_async_remote_copy(
                        src_ref=p_refs[t].at[:, 2 * px + py, rs], dst_ref=o_refs[t].at[r, :, rs], send_sem=ssem.at[q],
                        recv_sem=rsem.at[q], device_id=(px, py, c), device_id_type=MESH)
                    cp.start()
                    cps.append(cp)
        for cp in cps:
            cp.wait_recv()
        for cp in cps:
            cp.wait_send()

    sem = pltpu.SemaphoreType.DMA
    outs = pl.pallas_call(
        body, in_specs=[ANY] * nt, out_specs=[ANY] * nt,
        out_shape=[SDS((N_CHIPS - 1, p.shape[0]) + p.shape[2:], p.dtype) for p in parts],
        scratch_shapes=[sem((3 * total,)), sem((3 * total,))],
        name=name)(*parts)
    return list(outs)


def _pair_join(bufs, name):
    nt = len(bufs)
    half = [b.shape[0] // 2 for b in bufs]
    pieces = [_row_pieces(b.shape[1], _nbytes((h,) + b.shape[1:], b.dtype)) for b, h in zip(bufs, half)]
    base, total = _offsets([len(p) for p in pieces])

    def body(*refs):
        o_refs = refs[nt:2 * nt]
        ssem, rsem = refs[2 * nt:]
        x, y, c = _coords()
        cps = []
        for t in range(nt):
            mine = pl.ds(c * half[t], half[t])
            for i, rs in enumerate(pieces[t]):
                q = base[t] + i
                cp = pltpu.make_async_remote_copy(
                    src_ref=o_refs[t].at[mine, rs], dst_ref=o_refs[t].at[mine, rs], send_sem=ssem.at[q],
                    recv_sem=rsem.at[q], device_id=(x, y, 1 - c), device_id_type=MESH)
                cp.start()
                cps.append(cp)
        for t in range(nt):
            theirs = pl.ds((1 - c) * half[t], half[t])
            for i, rs in enumerate(pieces[t]):
                q = base[t] + i
                pltpu.make_async_remote_copy(
                    src_ref=o_refs[t].at[theirs, rs], dst_ref=o_refs[t].at[theirs, rs], send_sem=ssem.at[q],
                    recv_sem=rsem.at[q], device_id=(x, y, 1 - c), device_id_type=MESH).wait_recv()
        for cp in cps:
            cp.wait_send()

    sem = pltpu.SemaphoreType.DMA
    outs = pl.pallas_call(
        body, in_specs=[ANY] * nt, out_specs=[ANY] * nt,
        out_shape=[SDS(b.shape, b.dtype) for b in bufs],
        input_output_aliases={t: t for t in range(nt)},
        scratch_shapes=[sem((total,)), sem((total,))],
        name=name)(*bufs)
    return list(outs)


def _gather_devices(vals, name):
    nt = len(vals)
    flips = [(a, b, e) for a in (0, 1) for b in (0, 1) for e in (0, 1)][1:]

    def body(*refs):
        v_refs, o_refs = refs[:nt], refs[nt:2 * nt]
        lsem, ssem, rsem = refs[2 * nt:]
        x, y, c = _coords()
        me = 4 * x + 2 * y + c
        peers = [((1 - x) if a else x, (1 - y) if b else y, (1 - c) if e else c) for a, b, e in flips]
        cps = []
        for t in range(nt):
            loc = pltpu.make_async_copy(v_refs[t], o_refs[t].at[me], lsem.at[t])
            loc.start()
            cps.append(loc)
            for r, peer in enumerate(peers):
                cp = pltpu.make_async_remote_copy(
                    src_ref=v_refs[t], dst_ref=o_refs[t].at[me], send_sem=ssem.at[7 * t + r],
                    recv_sem=rsem.at[7 * t + r], device_id=peer, device_id_type=MESH)
                cp.start()
                cps.append(cp)
        for t in range(nt):
            for r, (px, py, pc) in enumerate(peers):
                pltpu.make_async_remote_copy(
                    src_ref=v_refs[t], dst_ref=o_refs[t].at[4 * px + 2 * py + pc], send_sem=ssem.at[7 * t + r],
                    recv_sem=rsem.at[7 * t + r], device_id=(px, py, pc), device_id_type=MESH).wait_recv()
        for t in range(nt):
            cps[8 * t].wait()
            for r in range(7):
                cps[8 * t + 1 + r].wait_send()

    sem = pltpu.SemaphoreType.DMA
    outs = pl.pallas_call(
        body, in_specs=[ANY] * nt, out_specs=[ANY] * nt,
        out_shape=[SDS((N_DEV,) + v.shape, v.dtype) for v in vals],
        scratch_shapes=[sem((nt,)), sem((7 * nt,)), sem((7 * nt,))],
        name=name)(*vals)
    return list(outs)


def _gather_all(shards, dtypes, k_idx, tag):
    placed = [_place_chunk(s, k_idx, dt, f"{tag}_place{t}") for t, (s, dt) in enumerate(zip(shards, dtypes))]
    return _gather_chips(placed, f"{tag}_gather")


def _reduce_to_shards(grads, k_idx, c_idx, kc_idx, tag):
    recv = _pair_send(grads, f"{tag}_pair_send")
    pair = [_pair_sum(g, r, c_idx, f"{tag}_pair_sum{t}") for t, (g, r) in enumerate(zip(grads, recv))]
    recv = _chip_send(pair, f"{tag}_chip_send")
    bufs = [_chip_sum(p, r, kc_idx, f"{tag}_chip_sum{t}") for t, (p, r) in enumerate(zip(pair, recv))]
    return _pair_join(bufs, f"{tag}_pair_join")


WEIGHT_NAMES = ("c_ctx", "w_ada", "b_ada", "w_in", "w_pool", "pool_scale", "ssm_a_re", "ssm_a_im", "ssm_log_dt",
                "ssm_b_re", "ssm_b_im", "ssm_c_re", "ssm_c_im", "ssm_d", "w_glu", "w_out", "g_pre_mix", "g_post_mix",
                "g_pre_ffn", "g_post_ffn", "w_up", "w_conv", "w_down")


def _block_diag_in(bb, ng):
    nl, nd, npart, h, gp = bb.shape
    p = gp // ng
    w = jnp.einsum("ldqhgp,kg->lkhdqgp", bb.reshape(nl, nd, npart, h, ng, p), jnp.eye(ng, dtype=bb.dtype))
    return w.reshape(nl, ng * h, nd * npart * gp)


def _diag_in_grad(dw, ng, nh, p):
    gl = ng // S5_BAND
    out = jnp.einsum("ghqagp->qhagp", dw.reshape(gl, nh, 2, S5_BAND, gl, p))
    return out.reshape(2, nh, ng * p)


def _block_diag_out(cs, ng):
    nl, nd, npart, _, h, p = cs.shape
    w = jnp.einsum("ldqghp,kg->ldqkpgh", cs, jnp.eye(ng, dtype=cs.dtype))
    return w.reshape(nl, nd * npart * ng * p, ng * h)


def _diag_out_grad(dw, ng, nh, p):
    gl = ng // S5_BAND
    out = jnp.einsum("qagpgh->qaghp", dw.reshape(2, S5_BAND, gl, p, gl, nh))
    return out.reshape(2, ng, nh, p)


def kernel(x, c, ctx, c_ctx, w_ada, b_ada, w_in, w_pool, pool_scale, ssm_a_re, ssm_a_im, ssm_log_dt, ssm_b_re, ssm_b_im, ssm_c_re, ssm_c_im, ssm_d, w_glu, w_out, g_pre_mix, g_post_mix, g_pre_ffn, g_post_ffn, w_up, w_conv, w_down, loss_target, m_c_ctx, m_w_ada, m_b_ada, m_w_in, m_w_pool, m_pool_scale, m_ssm_a_re, m_ssm_a_im, m_ssm_log_dt, m_ssm_b_re, m_ssm_b_im, m_ssm_c_re, m_ssm_c_im, m_ssm_d, m_w_glu, m_w_out, m_g_pre_mix, m_g_post_mix, m_g_pre_ffn, m_g_post_ffn, m_w_up, m_w_conv, m_w_down, v_c_ctx, v_w_ada, v_b_ada, v_w_in, v_w_pool, v_pool_scale, v_ssm_a_re, v_ssm_a_im, v_ssm_log_dt, v_ssm_b_re, v_ssm_b_im, v_ssm_c_re, v_ssm_c_im, v_ssm_d, v_w_glu, v_w_out, v_g_pre_mix, v_g_post_mix, v_g_pre_ffn, v_g_post_ffn, v_w_up, v_w_conv, v_w_down):
    weights = dict(zip(WEIGHT_NAMES, (c_ctx, w_ada, b_ada, w_in, w_pool, pool_scale, ssm_a_re, ssm_a_im, ssm_log_dt,
                                      ssm_b_re, ssm_b_im, ssm_c_re, ssm_c_im, ssm_d, w_glu, w_out, g_pre_mix, g_post_mix,
                                      g_pre_ffn, g_post_ffn, w_up, w_conv, w_down)))
    mom1 = dict(zip(WEIGHT_NAMES, (m_c_ctx, m_w_ada, m_b_ada, m_w_in, m_w_pool, m_pool_scale, m_ssm_a_re, m_ssm_a_im,
                                   m_ssm_log_dt, m_ssm_b_re, m_ssm_b_im, m_ssm_c_re, m_ssm_c_im, m_ssm_d, m_w_glu, m_w_out,
                                   m_g_pre_mix, m_g_post_mix, m_g_pre_ffn, m_g_post_ffn, m_w_up, m_w_conv, m_w_down)))
    mom2 = dict(zip(WEIGHT_NAMES, (v_c_ctx, v_w_ada, v_b_ada, v_w_in, v_w_pool, v_pool_scale, v_ssm_a_re, v_ssm_a_im,
                                   v_ssm_log_dt, v_ssm_b_re, v_ssm_b_im, v_ssm_c_re, v_ssm_c_im, v_ssm_d, v_w_glu, v_w_out,
                                   v_g_pre_mix, v_g_post_mix, v_g_pre_ffn, v_g_post_ffn, v_w_up, v_w_conv, v_w_down)))

    xi, yi, ci = lax.axis_index("x"), lax.axis_index("y"), lax.axis_index("c")
    chip = 2 * xi + yi
    dev = 4 * xi + 2 * yi + ci
    nl = w_in.shape[0]
    n_lat, d = x.shape[1], x.shape[2]
    n_ctx = ctx.shape[1]
    n = n_ctx + n_lat
    _, ndir, ng, nstate, nh = ssm_b_re.shape
    gp = ng * nstate
    sw = ng * nh
    n_pool_groups, pool_group = w_pool.shape[1], w_pool.shape[3]
    pw = n_pool_groups * pool_group
    assert pw + sw == d and pw % sw == 0 and len(POOL_WINDOWS) == n_pool_groups and n_lat % GRID_W == 0
    dff2 = w_up.shape[2] * N_CHIPS
    ada_w = w_ada.shape[2] * N_CHIPS
    ada_cols = w_ada.shape[2]
    s_rows = gp // LANES

    c_pad = jnp.concatenate([c, jnp.zeros((SUBLANES - 1, d), F32)], axis=0)
    c_all = _gather_devices([c_pad], "gather_cond")[0][:, 0, :]
    cond = jnp.concatenate([c_all, c_ctx[None, :], jnp.zeros((16 - N_DEV - 1, d), F32)], axis=0)
    b_shard = lax.dynamic_slice_in_dim(b_ada, chip * ada_cols, ada_cols, axis=1)[:, None, :]
    mod_shard = _ada_fwd(cond, w_ada, b_shard, "ada_fwd")
    k_idx, c_idx, kc_idx = jnp.stack([chip]), jnp.stack([ci]), jnp.stack([chip, ci])
    mod_all = _gather_all([mod_shard], [F32], k_idx, "mods")[0]
    mod_all = jnp.transpose(mod_all, (0, 2, 1, 3)).reshape(nl, 16, ada_w)
    mod_lat = lax.dynamic_index_in_dim(mod_all, dev, axis=1, keepdims=False).reshape(nl, 6, d)
    mod_ctx = mod_all[:, N_DEV].reshape(nl, 6, d)
    mods = jnp.concatenate([jnp.stack([mod_ctx, mod_lat], axis=1), jnp.zeros((nl, 2, 2, d), F32)], axis=2)

    shards = [w_in, w_pool.reshape(nl, pw // N_CHIPS, pool_group), w_glu, w_out, w_up, w_down,
              w_conv.reshape(nl, 9, dff2 // N_CHIPS)]
    g_in, g_pool, g_glu, g_out, g_up, g_down, g_conv = _gather_all(shards, [COMM_DTYPE] * 6 + [F32], k_idx, "weights")
    wi = g_in.reshape(nl, d, d)
    wp = jnp.transpose(g_pool.reshape(nl, N_CHIPS, n_pool_groups, pool_group // N_CHIPS, pool_group),
                       (0, 2, 1, 3, 4)).reshape(nl, n_pool_groups, pool_group, pool_group)
    wg = g_glu.reshape(nl, sw, sw)
    wo = g_out.reshape(nl, d, d)
    wu = g_up
    wd = g_down.reshape(nl, dff2 // 2, d)
    wk = jnp.transpose(g_conv, (0, 2, 1, 3)).reshape(nl, 9, dff2)

    rows = nl * ndir
    a_re2 = ssm_a_re.reshape(rows, gp)
    a_im2 = ssm_a_im.reshape(rows, gp)
    logdt2 = jnp.repeat(ssm_log_dt.reshape(rows, ng), nstate, axis=1)
    b_re2 = jnp.transpose(ssm_b_re.reshape(rows, gp, nh), (0, 2, 1))
    b_im2 = jnp.transpose(ssm_b_im.reshape(rows, gp, nh), (0, 2, 1))
    lam_re, lam_im, bb_re, bb_im = _disc_fwd(a_re2, a_im2, logdt2, b_re2, b_im2, "s5_discretise")
    lam = jnp.stack([lam_re.reshape(nl, ndir, s_rows, LANES), lam_im.reshape(nl, ndir, s_rows, LANES)], axis=2)
    lam = lam.reshape(nl, 2 * ndir, s_rows, LANES)
    bbs = jnp.stack([bb_re.reshape(nl, ndir, nh, gp), bb_im.reshape(nl, ndir, nh, gp)], axis=2)
    w_b = _block_diag_in(bbs.astype(MXU_DTYPE), ng)
    w_b = [w_b[:, :, dr * 2 * gp:(dr + 1) * 2 * gp] for dr in range(ndir)]
    cs = jnp.stack([ssm_c_re, -ssm_c_im], axis=2)
    w_c = _block_diag_out(cs.astype(MXU_DTYPE), ng)
    w_c = [w_c[:, dr * 2 * gp:(dr + 1) * 2 * gp] for dr in range(ndir)]

    def row(v, l):
        return v[l:l + 1]

    xc = jnp.concatenate([ctx[0], x[0]], axis=0)
    saved = []
    for l in range(nl):
        t = f"l{l}"
        md = mods[l]
        h = _norm_mod_fwd(xc, row(g_pre_mix, l), md, 0, 1, n_ctx, f"{t}_pre_mix")
        u = _mm(h, wi, "nn", F32, f"{t}_in_proj", b_idx=l)
        p = _pool(u, pw, pool_group, n_ctx, False, MXU_DTYPE, f"{t}_pool")
        ypool = _pool_proj_fwd(p, wp, l, row(pool_scale, l), f"{t}_pool_proj")
        bu0 = _mm_s5(u, w_b[0], "in", f"{t}_s5_in0", b_idx=l, a_cols=(pw, sw))
        bu1 = _mm_s5(u, w_b[1], "in", f"{t}_s5_in1", b_idx=l, a_cols=(pw, sw))
        h0, h1 = _scan_fwd(bu0, bu1, lam[l], n_ctx, f"{t}_scan")
        y = _mm_s5(h0, w_c[0], "out", f"{t}_s5_out0", b_idx=l)
        y = _mm_s5(h1, w_c[1], "out", f"{t}_s5_out1", b_idx=l, add=y)
        s_out = _ssm_head_fwd(y, u, row(ssm_d, l), wg, l, f"{t}_s5_head")
        cat = jnp.concatenate([ypool, s_out], axis=1)
        mix = _mm(cat, wo, "nn", F32, f"{t}_out_proj", b_idx=l)
        x_mid = _gate_res_fwd(xc, mix, row(g_post_mix, l), md, 2, n_ctx, f"{t}_post_mix")
        h2 = _norm_mod_fwd(x_mid, row(g_pre_ffn, l), md, 3, 4, n_ctx, f"{t}_pre_ffn")
        z = _mm(h2, wu, "nn", F32, f"{t}_up", b_idx=l, b_chunks=N_CHIPS)
        act, cv, cg = _conv_glu_fwd(z, wk[l], n_ctx, f"{t}_conv_glu")
        f = _mm(act, wd, "nn", F32, f"{t}_down", b_idx=l)
        x_out = _gate_res_fwd(x_mid, f, row(g_post_ffn, l), md, 5, n_ctx, f"{t}_post_ffn")
        saved.append(dict(xc=xc, h=h, u=u, p=p, h0=h0, h1=h1, y=y, cat=cat, mix=mix, x_mid=x_mid, h2=h2, z=z, cv=cv, cg=cg, f=f))
        xc = x_out

    dx, loss_tile = _loss_grad(xc, loss_target[0], n_ctx, "loss")
    loss = lax.psum(loss_tile[0, 0], ("x", "y", "c"))

    big = {k: [None] * nl for k in ("w_in", "w_pool", "w_glu", "w_out", "w_up", "w_down")}
    small = {k: [None] * nl for k in ("pool_scale", "ssm_d", "g_pre_mix", "g_post_mix", "g_pre_ffn", "g_post_ffn",
                                      "lam", "bb", "cs", "w_conv")}
    dmods = [None] * nl
    for l in reversed(range(nl)):
        t = f"l{l}b"
        md = mods[l]
        sv = saved[l]
        df, dgate_ffn, small["g_post_ffn"][l] = _gate_res_bwd(dx, sv["f"], row(g_post_ffn, l), md, 5, n_ctx, f"{t}_post_ffn")
        dact = _mm(df, wd, "nt", F32, f"{t}_down_dx", b_idx=l)
        dz, dkv, dkg, act = _conv_glu_bwd(sv["z"], sv["cv"], sv["cg"], dact, wk[l], n_ctx, f"{t}_conv_glu")
        small["w_conv"][l] = jnp.concatenate([dkv, dkg], axis=1)
        big["w_down"][l] = _mm(act, df, "tn", COMM_DTYPE, f"{t}_down_dw")
        big["w_up"][l] = _mm(sv["h2"], dz, "tn", COMM_DTYPE, f"{t}_up_dw", b_chunks=2, out_chunks=N_CHIPS)
        dh2 = _mm(dz, wu, "nt", F32, f"{t}_up_dx", b_idx=l, a_chunks=2, b_chunks=N_CHIPS)
        dx, dss_ffn, small["g_pre_ffn"][l] = _norm_mod_bwd(dh2, sv["x_mid"], row(g_pre_ffn, l), md, 3, 4, dx, n_ctx,
                                                           f"{t}_pre_ffn")
        dmix, dgate_mix, small["g_post_mix"][l] = _gate_res_bwd(dx, sv["mix"], row(g_post_mix, l), md, 2, n_ctx,
                                                                f"{t}_post_mix")
        dcat = _mm(dmix, wo, "nt", F32, f"{t}_out_dx", b_idx=l)
        big["w_out"][l] = _mm(sv["cat"], dmix, "tn", COMM_DTYPE, f"{t}_out_dw")
        dp, small["pool_scale"][l], big["w_pool"][l] = _pool_proj_bwd(sv["p"], dcat, wp, l, row(pool_scale, l),
                                                                      f"{t}_pool_proj")
        du_pool = _pool(dp, pw, pool_group, n_ctx, True, F32, f"{t}_pool")
        dy, du_dir, gact, dq, small["ssm_d"][l] = _ssm_head_bwd(dcat, sv["y"], sv["u"], row(ssm_d, l), wg, l, f"{t}_s5_head")
        big["w_glu"][l] = _mm(gact, dq, "tn", COMM_DTYPE, f"{t}_glu_dw")
        dh0 = _mm_s5(dy, w_c[0], "out_dx", f"{t}_s5_out_dx0", b_idx=l)
        dh1 = _mm_s5(dy, w_c[1], "out_dx", f"{t}_s5_out_dx1", b_idx=l)
        small["cs"][l] = jnp.stack([_diag_out_grad(_mm_s5(sv[hk], dy, "out_dw", f"{t}_s5_out_dw{dr}"), ng, nh, nstate)
                                    for dr, hk in enumerate(("h0", "h1"))], axis=0)
        a0, a1, small["lam"][l] = _scan_bwd(dh0, dh1, sv["h0"], sv["h1"], lam[l], n_ctx, f"{t}_scan")
        du_proj = _mm_s5(a0, w_b[0], "in_dx", f"{t}_s5_in_dx0", b_idx=l)
        du_proj = _mm_s5(a1, w_b[1], "in_dx", f"{t}_s5_in_dx1", b_idx=l, add=du_proj)
        small["bb"][l] = jnp.stack([_diag_in_grad(_mm_s5(sv["u"], adj, "in_dw", f"{t}_s5_in_dw{dr}", a_cols=(pw, sw)),
                                                  ng, nh, nstate) for dr, adj in enumerate((a0, a1))], axis=0)
        du = _assemble_du(du_pool, du_dir, du_proj, f"{t}_du")
        dh = _mm(du, wi, "nt", F32, f"{t}_in_dx", b_idx=l)
        big["w_in"][l] = _mm(sv["h"], du, "tn", COMM_DTYPE, f"{t}_in_dw")
        dx, dss_mix, small["g_pre_mix"][l] = _norm_mod_bwd(dh, sv["xc"], row(g_pre_mix, l), md, 0, 1, dx, n_ctx,
                                                           f"{t}_pre_mix")
        dmods[l] = jnp.concatenate([dss_mix, dgate_mix, dss_ffn, dgate_ffn], axis=1).reshape(2, ada_w)

    grad_x = dx[n_ctx:][None]

    dmod_all = _gather_devices([jnp.stack(dmods, axis=0)], "gather_dmods")[0]
    ada_rows, db_ada = _ada_rows(dmod_all, "ada_rows")
    rows_shard = lax.dynamic_slice_in_dim(ada_rows, chip * ada_cols, ada_cols, axis=2)
    dcond_part = _ada_dcond(rows_shard, w_ada, "ada_dcond")
    dcond_parts = _gather_devices([dcond_part], "gather_dcond")[0][0::2]
    grads = {"w_ada": _ada_dw(cond, rows_shard, "ada_dw"), "b_ada": db_ada[:, 0, :],
             "c_ctx": _ada_dctx(dcond_parts, c_ctx[None, :], N_DEV, "ada_dctx")[0]}

    stacked = {k: jnp.stack(v, axis=0) for k, v in big.items()}
    parts = [stacked["w_in"].reshape(nl, N_CHIPS, d // N_CHIPS, d),
             jnp.transpose(stacked["w_pool"].astype(COMM_DTYPE).reshape(nl, n_pool_groups, N_CHIPS, pool_group // N_CHIPS,
                                                                      pool_group), (0, 2, 1, 3, 4))
             .reshape(nl, N_CHIPS, pw // N_CHIPS, pool_group),
             stacked["w_glu"].reshape(nl, N_CHIPS, sw // N_CHIPS, sw),
             stacked["w_out"].reshape(nl, N_CHIPS, d // N_CHIPS, d),
             stacked["w_up"],
             stacked["w_down"].reshape(nl, N_CHIPS, dff2 // 2 // N_CHIPS, d)]
    r_in, r_pool, r_glu, r_out, r_up, r_down = _reduce_to_shards(parts, k_idx, c_idx, kc_idx, "big")
    grads.update(w_in=r_in, w_pool=r_pool.reshape(w_pool.shape), w_glu=r_glu, w_out=r_out, w_up=r_up, w_down=r_down)

    order = ("pool_scale", "ssm_d", "g_pre_mix", "g_post_mix", "g_pre_ffn", "g_post_ffn", "lam", "bb", "cs", "w_conv")
    pieces = [jnp.stack(small[k], axis=0) for k in order]
    flat = jnp.concatenate([q.reshape(-1) for q in pieces])
    unit = nl * N_CHIPS * SUBLANES * 1024
    padded = -(-flat.shape[0] // unit) * unit
    flat = jnp.concatenate([flat, jnp.zeros((padded - flat.shape[0],), F32)])
    vec = flat.reshape(nl, N_CHIPS, padded // (nl * N_CHIPS * 1024), 1024)
    vec = _gather_all(_reduce_to_shards([vec], k_idx, c_idx, kc_idx, "small"), [F32], k_idx, "small_all")[0].reshape(-1)
    red, pos = {}, 0
    for k, q in zip(order, pieces):
        red[k] = vec[pos:pos + q.size].reshape(q.shape)
        pos += q.size
    for k in ("pool_scale", "ssm_d", "g_pre_mix", "g_post_mix", "g_pre_ffn", "g_post_ffn"):
        grads[k] = red[k][:, 0, :]
    dlam = red["lam"].reshape(nl, ndir, 2, gp)
    dbb = red["bb"].reshape(nl, ndir, 2, nh, gp)
    d_are, d_aim, d_ldt, d_bre, d_bim = _disc_bwd(
        a_re2, a_im2, logdt2, b_re2, b_im2, dlam[:, :, 0].reshape(rows, gp), dlam[:, :, 1].reshape(rows, gp),
        dbb[:, :, 0].reshape(rows, nh, gp), dbb[:, :, 1].reshape(rows, nh, gp), nstate, "s5_discretise_bwd")
    grads["ssm_a_re"] = d_are.reshape(ssm_a_re.shape)
    grads["ssm_a_im"] = d_aim.reshape(ssm_a_im.shape)
    grads["ssm_log_dt"] = d_ldt[:, :ng].reshape(ssm_log_dt.shape)
    grads["ssm_b_re"] = jnp.transpose(d_bre, (0, 2, 1)).reshape(ssm_b_re.shape)
    grads["ssm_b_im"] = jnp.transpose(d_bim, (0, 2, 1)).reshape(ssm_b_im.shape)
    grads["ssm_c_re"] = red["cs"][:, :, 0]
    grads["ssm_c_im"] = -red["cs"][:, :, 1]
    conv_cols = dff2 // N_CHIPS
    grads["w_conv"] = lax.dynamic_slice_in_dim(red["w_conv"], chip * conv_cols, conv_cols, axis=2).reshape(w_conv.shape)

    delta, new_m, new_v = {}, {}, {}
    for k in WEIGHT_NAMES:
        delta[k], new_m[k], new_v[k] = _adamw(weights[k], grads[k], mom1[k], mom2[k], f"adamw_{k}")
    return (loss, grad_x, *[grads[k] for k in WEIGHT_NAMES], *[delta[k] for k in WEIGHT_NAMES],
            *[new_m[k] for k in WEIGHT_NAMES], *[new_v[k] for k in WEIGHT_NAMES])
```
